```python
import math
import jax
import jax.numpy as jnp
from jax import lax
import numpy as np

D_MODEL = 1024
BATCH = 32
SEQ = 2048
DEPTH = 2

HEAD_DIM = 64
SSD_WIDTH = 3 * D_MODEL // 8
LRU_WIDTH = D_MODEL // 4
FOX_WIDTH = D_MODEL - SSD_WIDTH - LRU_WIDTH
SSD_HEADS = SSD_WIDTH // HEAD_DIM
SSD_GROUPS = 2
SSD_STATE = 128
SSD_CONV = 4
SSD_CONV_DIM = SSD_WIDTH + 2 * SSD_GROUPS * SSD_STATE
SSD_CHUNK = 128
LRU_BLOCKS = 4
LRU_BLOCK = LRU_WIDTH // LRU_BLOCKS
LRU_CONV = 4
LRU_C = 8.0
FOX_HEADS = FOX_WIDTH // HEAD_DIM
Q_BLOCK = 128
D_FF = ((8 * D_MODEL + 3 * 256 - 1) // (3 * 256)) * 256
PLE_DIM = 256
IN_SPLITS = (SSD_WIDTH, SSD_CONV_DIM, SSD_HEADS, LRU_WIDTH, LRU_WIDTH,
             FOX_WIDTH, FOX_WIDTH, FOX_WIDTH, FOX_HEADS)
IN_COLS = sum(IN_SPLITS)
EPS = 1e-6

kernel_name = "hymba_style_ssd_rglru_fox_trunk"


def _rmsnorm(x, g):
    xf = x.astype(jnp.float32)
    y = xf * lax.rsqrt(jnp.mean(xf * xf, axis=-1, keepdims=True) + EPS)
    return (y * g.astype(jnp.float32)).astype(x.dtype)


def _causal_dwconv(x, w, b):
    k, c = w.shape
    y = lax.conv_general_dilated(
        x, w[:, None, :].astype(x.dtype), window_strides=(1,),
        padding=[(k - 1, 0)], dimension_numbers=("NWC", "WIO", "NWC"),
        feature_group_count=c)
    return y + b.astype(x.dtype)


def _ssd(xs, dt, a, bm, cm, d_skip):
    b, s, h, p = xs.shape
    g, n = bm.shape[2], bm.shape[3]
    nc = s // SSD_CHUNK
    rep = h // g
    bh = jnp.repeat(bm, rep, axis=2)
    chh = jnp.repeat(cm, rep, axis=2)
    xdt = xs * dt[..., None]
    adt = a * dt
    chunk = lambda t: t.reshape((b, nc, SSD_CHUNK) + t.shape[2:])
    xc, bc, cc, ac = chunk(xdt), chunk(bh), chunk(chh), chunk(adt)
    acs = jnp.cumsum(ac, axis=2)
    seg = acs[:, :, :, None, :] - acs[:, :, None, :, :]
    causal = jnp.tril(jnp.ones((SSD_CHUNK, SSD_CHUNK), bool))
    lmat = jnp.exp(jnp.where(causal[None, None, :, :, None], seg, -jnp.inf))
    scores = jnp.einsum("bclhn,bcshn->bclsh", cc, bc) * lmat
    y_diag = jnp.einsum("bclsh,bcshp->bclhp", scores, xc)
    decay_s = jnp.exp(acs[:, :, -1:, :] - acs)
    states = jnp.einsum("bclhn,bclh,bclhp->bchpn", bc, decay_s, xc)
    chunk_decay = jnp.exp(acs[:, :, -1, :])

    def step(carry, inp):
        st, dec = inp
        return carry * dec[..., None, None] + st, carry

    init = jnp.zeros((b, h, p, n), states.dtype)
    _, prev = lax.scan(step, init, (jnp.moveaxis(states, 1, 0),
                                    jnp.moveaxis(chunk_decay, 1, 0).astype(states.dtype)))
    prev = jnp.moveaxis(prev, 0, 1)
    y_off = jnp.einsum("bclhn,bchpn,bclh->bclhp", cc, prev, jnp.exp(acs))
    y = (y_diag + y_off).reshape(b, s, h, p) + xs * d_skip[:, None]
    return y.astype(xs.dtype)


def _rglru(x, w_a, b_a, w_x, b_x, lam):
    b, s, w = x.shape
    xb = x.reshape(b, s, LRU_BLOCKS, LRU_BLOCK)
    r = jax.nn.sigmoid(jnp.einsum("bsgi,gij->bsgj", xb, w_a).reshape(b, s, w) + b_a)
    i = jax.nn.sigmoid(jnp.einsum("bsgi,gij->bsgj", xb, w_x).reshape(b, s, w) + b_x)
    log_a = -LRU_C * r.astype(jnp.float32) * jax.nn.softplus(-lam.astype(jnp.float32))
    a = jnp.exp(log_a)
    mult = jnp.sqrt(-jnp.expm1(2.0 * log_a))
    u = mult * (i * x).astype(jnp.float32)

    def comb(lhs, rhs):
        a1, b1 = lhs
        a2, b2 = rhs
        return a1 * a2, a2 * b1 + b2

    _, hseq = lax.associative_scan(comb, (a, u), axis=1)
    return hseq.astype(x.dtype)


def _forgetting_attention(q, k, v, log_f):
    s, e = q.shape[1], q.shape[3]
    cum = jnp.cumsum(log_f, axis=-1)
    scale = e ** -0.5
    outs = []
    for blk in range(s // Q_BLOCK):
        q0, q1 = blk * Q_BLOCK, (blk + 1) * Q_BLOCK
        logits = jnp.einsum("bqhe,bkhe->bhqk", q[:, q0:q1], k[:, :q1]).astype(jnp.float32) * scale
        logits = logits + cum[:, :, q0:q1, None] - cum[:, :, None, :q1]
        mask = (q0 + jnp.arange(Q_BLOCK))[:, None] >= jnp.arange(q1)[None, :]
        logits = jnp.where(mask[None, None], logits, -jnp.inf)
        probs = jax.nn.softmax(logits, axis=-1).astype(v.dtype)
        outs.append(jnp.einsum("bhqk,bkhe->bqhe", probs, v[:, :q1]))
    return jnp.concatenate(outs, axis=1)


def _mixer(u, w_in, ssd_conv_w, ssd_conv_b, ssd_dt_bias, ssd_a_log, ssd_d, ssd_norm_g,
           lru_conv_w, lru_conv_b, lru_w_a, lru_b_a, lru_w_x, lru_b_x, lru_lambda, lru_norm_g,
           fox_b_f, fox_norm_g, w_out):
    b, s, _ = u.shape
    proj = u @ w_in
    offs = [int(o) for o in np.cumsum(IN_SPLITS)[:-1]]
    (z, xbc, dt_raw, lru_x, lru_gate, fq, fk, fv, f_raw) = jnp.split(proj, offs, axis=-1)

    xbc = jax.nn.silu(_causal_dwconv(xbc, ssd_conv_w, ssd_conv_b))
    xs, bm, cm = jnp.split(xbc, [SSD_WIDTH, SSD_WIDTH + SSD_GROUPS * SSD_STATE], axis=-1)
    xs = xs.reshape(b, s, SSD_HEADS, HEAD_DIM)
    bm = bm.reshape(b, s, SSD_GROUPS, SSD_STATE)
    cm = cm.reshape(b, s, SSD_GROUPS, SSD_STATE)
    dt = jax.nn.softplus(dt_raw.astype(jnp.float32) + ssd_dt_bias.astype(jnp.float32))
    a = -jnp.exp(ssd_a_log.astype(jnp.float32))
    y_ssd = _ssd(xs, dt, a, bm, cm, ssd_d).reshape(b, s, SSD_WIDTH)
    y_ssd = _rmsnorm(y_ssd * jax.nn.silu(z), ssd_norm_g)

    xl = _causal_dwconv(lru_x, lru_conv_w, lru_conv_b)
    hl = _rglru(xl, lru_w_a, lru_b_a, lru_w_x, lru_b_x, lru_lambda)
    y_lru = _rmsnorm(hl * jax.nn.gelu(lru_gate), lru_norm_g)

    q = fq.reshape(b, s, FOX_HEADS, HEAD_DIM)
    k = fk.reshape(b, s, FOX_HEADS, HEAD_DIM)
    v = fv.reshape(b, s, FOX_HEADS, HEAD_DIM)
    log_f = jnp.transpose(jax.nn.log_sigmoid(f_raw.astype(jnp.float32) + fox_b_f.astype(jnp.float32)), (0, 2, 1))
    y_fox = _forgetting_attention(q, k, v, log_f).reshape(b, s, FOX_WIDTH)
    y_fox = _rmsnorm(y_fox, fox_norm_g)

    return jnp.concatenate([y_ssd, y_lru, y_fox], axis=-1) @ w_out


def _fwd_setup_inputs(seed: int = 0) -> dict:
    key = jax.random.key(seed)
    ks = jax.random.split(key, 32)
    f32 = jnp.float32
    L = DEPTH
    nrm = lambda k, shape, scale: jax.random.normal(k, shape, f32) * scale
    gain = lambda k, shape: 1.0 + 0.02 * jax.random.normal(k, shape, f32)
    dt0 = jnp.exp(jax.random.uniform(ks[6], (L, SSD_HEADS), f32, math.log(1e-3), math.log(1e-1)))
    lam_s = jax.random.uniform(ks[16], (L, LRU_WIDTH), f32, 0.9, 0.999) ** (1.0 / LRU_C)
    return {
        "x": jax.random.normal(ks[0], (BATCH, SEQ, D_MODEL), f32),
        "p": jax.random.normal(ks[1], (L, BATCH, SEQ, PLE_DIM), f32),
        "norm1_g": gain(ks[2], (L, D_MODEL)),
        "w_in": nrm(ks[3], (L, D_MODEL, IN_COLS), D_MODEL ** -0.5),
        "ssd_conv_w": nrm(ks[4], (L, SSD_CONV, SSD_CONV_DIM), SSD_CONV ** -0.5),
        "ssd_conv_b": nrm(ks[5], (L, SSD_CONV_DIM), 0.02),
        "ssd_dt_bias": dt0 + jnp.log(-jnp.expm1(-dt0)),
        "ssd_a_log": jnp.log(jax.random.uniform(ks[7], (L, SSD_HEADS), f32, 1.0, 16.0)),
        "ssd_d": gain(ks[8], (L, SSD_HEADS)),
        "ssd_norm_g": gain(ks[9], (L, SSD_WIDTH)),
        "lru_conv_w": nrm(ks[10], (L, LRU_CONV, LRU_WIDTH), LRU_CONV ** -0.5),
        "lru_conv_b": nrm(ks[11], (L, LRU_WIDTH), 0.02),
        "lru_w_a": nrm(ks[12], (L, LRU_BLOCKS, LRU_BLOCK, LRU_BLOCK), LRU_BLOCK ** -0.5),
        "lru_b_a": nrm(ks[13], (L, LRU_WIDTH), 0.02),
        "lru_w_x": nrm(ks[14], (L, LRU_BLOCKS, LRU_BLOCK, LRU_BLOCK), LRU_BLOCK ** -0.5),
        "lru_b_x": nrm(ks[15], (L, LRU_WIDTH), 0.02),
        "lru_lambda": jnp.log(lam_s) - jnp.log1p(-lam_s),
        "lru_norm_g": gain(ks[17], (L, LRU_WIDTH)),
        "fox_b_f": 3.0 + nrm(ks[18], (L, FOX_HEADS), 0.1),
        "fox_norm_g": gain(ks[19], (L, FOX_WIDTH)),
        "w_out": nrm(ks[20], (L, D_MODEL, D_MODEL), D_MODEL ** -0.5),
        "norm2_g": gain(ks[21], (L, D_MODEL)),
        "w_gate": nrm(ks[22], (L, D_MODEL, D_FF), D_MODEL ** -0.5),
        "w_up": nrm(ks[23], (L, D_MODEL, D_FF), D_MODEL ** -0.5),
        "w_down": nrm(ks[24], (L, D_FF, D_MODEL), D_FF ** -0.5),
        "norm3_g": gain(ks[25], (L, D_MODEL)),
        "w_ple_gate": nrm(ks[26], (L, D_MODEL, D_MODEL), D_MODEL ** -0.5),
        "b_ple_gate": nrm(ks[27], (L, D_MODEL), 0.02),
        "w_ple_proj": nrm(ks[28], (L, PLE_DIM, D_MODEL), PLE_DIM ** -0.5),
        "final_norm_g": gain(ks[29], (D_MODEL,)),
    }


def _fwd_reference(x, p, norm1_g, w_in, ssd_conv_w, ssd_conv_b, ssd_dt_bias, ssd_a_log, ssd_d,
              ssd_norm_g, lru_conv_w, lru_conv_b, lru_w_a, lru_b_a, lru_w_x, lru_b_x,
              lru_lambda, lru_norm_g, fox_b_f, fox_norm_g, w_out, norm2_g, w_gate, w_up,
              w_down, norm3_g, w_ple_gate, b_ple_gate, w_ple_proj, final_norm_g):
    h = x
    for i in range(DEPTH):
        u = _rmsnorm(h, norm1_g[i])
        h = h + _mixer(u, w_in[i], ssd_conv_w[i], ssd_conv_b[i], ssd_dt_bias[i], ssd_a_log[i],
                       ssd_d[i], ssd_norm_g[i], lru_conv_w[i], lru_conv_b[i], lru_w_a[i],
                       lru_b_a[i], lru_w_x[i], lru_b_x[i], lru_lambda[i], lru_norm_g[i],
                       fox_b_f[i], fox_norm_g[i], w_out[i])
        u = _rmsnorm(h, norm2_g[i])
        h = h + (jax.nn.silu(u @ w_gate[i]) * (u @ w_up[i])) @ w_down[i]
        u = _rmsnorm(h, norm3_g[i])
        gate = jax.nn.sigmoid(u @ w_ple_gate[i] + b_ple_gate[i])
        h = h + gate * (p[i] @ w_ple_proj[i])
    return _rmsnorm(h, final_norm_g)


import jax as _jax
import jax.numpy as _jnp

TWIN_FORMAT = 'train_step'
FWD_PARAMS = ['x', 'p', 'norm1_g', 'w_in', 'ssd_conv_w', 'ssd_conv_b', 'ssd_dt_bias', 'ssd_a_log', 'ssd_d', 'ssd_norm_g', 'lru_conv_w', 'lru_conv_b', 'lru_w_a', 'lru_b_a', 'lru_w_x', 'lru_b_x', 'lru_lambda', 'lru_norm_g', 'fox_b_f', 'fox_norm_g', 'w_out', 'norm2_g', 'w_gate', 'w_up', 'w_down', 'norm3_g', 'w_ple_gate', 'b_ple_gate', 'w_ple_proj', 'final_norm_g']
TWIN_WEIGHTS = ['norm1_g', 'w_in', 'ssd_conv_w', 'ssd_conv_b', 'ssd_dt_bias', 'ssd_a_log', 'ssd_d', 'ssd_norm_g', 'lru_conv_w', 'lru_conv_b', 'lru_w_a', 'lru_b_a', 'lru_w_x', 'lru_b_x', 'lru_lambda', 'lru_norm_g', 'fox_b_f', 'fox_norm_g', 'w_out', 'norm2_g', 'w_gate', 'w_up', 'w_down', 'norm3_g', 'w_ple_gate', 'b_ple_gate', 'w_ple_proj', 'final_norm_g']
TWIN_DIFF_INPUT = 'x'
TWIN_INPUTS = ['x', 'p', 'norm1_g', 'w_in', 'ssd_conv_w', 'ssd_conv_b', 'ssd_dt_bias', 'ssd_a_log', 'ssd_d', 'ssd_norm_g', 'lru_conv_w', 'lru_conv_b', 'lru_w_a', 'lru_b_a', 'lru_w_x', 'lru_b_x', 'lru_lambda', 'lru_norm_g', 'fox_b_f', 'fox_norm_g', 'w_out', 'norm2_g', 'w_gate', 'w_up', 'w_down', 'norm3_g', 'w_ple_gate', 'b_ple_gate', 'w_ple_proj', 'final_norm_g', 'loss_target', 'm_norm1_g', 'm_w_in', 'm_ssd_conv_w', 'm_ssd_conv_b', 'm_ssd_dt_bias', 'm_ssd_a_log', 'm_ssd_d', 'm_ssd_norm_g', 'm_lru_conv_w', 'm_lru_conv_b', 'm_lru_w_a', 'm_lru_b_a', 'm_lru_w_x', 'm_lru_b_x', 'm_lru_lambda', 'm_lru_norm_g', 'm_fox_b_f', 'm_fox_norm_g', 'm_w_out', 'm_norm2_g', 'm_w_gate', 'm_w_up', 'm_w_down', 'm_norm3_g', 'm_w_ple_gate', 'm_b_ple_gate', 'm_w_ple_proj', 'm_final_norm_g', 'v_norm1_g', 'v_w_in', 'v_ssd_conv_w', 'v_ssd_conv_b', 'v_ssd_dt_bias', 'v_ssd_a_log', 'v_ssd_d', 'v_ssd_norm_g', 'v_lru_conv_w', 'v_lru_conv_b', 'v_lru_w_a', 'v_lru_b_a', 'v_lru_w_x', 'v_lru_b_x', 'v_lru_lambda', 'v_lru_norm_g', 'v_fox_b_f', 'v_fox_norm_g', 'v_w_out', 'v_norm2_g', 'v_w_gate', 'v_w_up', 'v_w_down', 'v_norm3_g', 'v_w_ple_gate', 'v_b_ple_gate', 'v_w_ple_proj', 'v_final_norm_g']
TWIN_OUTPUTS = ['loss', 'grad_x', 'grad_norm1_g', 'grad_w_in', 'grad_ssd_conv_w', 'grad_ssd_conv_b', 'grad_ssd_dt_bias', 'grad_ssd_a_log', 'grad_ssd_d', 'grad_ssd_norm_g', 'grad_lru_conv_w', 'grad_lru_conv_b', 'grad_lru_w_a', 'grad_lru_b_a', 'grad_lru_w_x', 'grad_lru_b_x', 'grad_lru_lambda', 'grad_lru_norm_g', 'grad_fox_b_f', 'grad_fox_norm_g', 'grad_w_out', 'grad_norm2_g', 'grad_w_gate', 'grad_w_up', 'grad_w_down', 'grad_norm3_g', 'grad_w_ple_gate', 'grad_b_ple_gate', 'grad_w_ple_proj', 'grad_final_norm_g', 'delta_norm1_g', 'delta_w_in', 'delta_ssd_conv_w', 'delta_ssd_conv_b', 'delta_ssd_dt_bias', 'delta_ssd_a_log', 'delta_ssd_d', 'delta_ssd_norm_g', 'delta_lru_conv_w', 'delta_lru_conv_b', 'delta_lru_w_a', 'delta_lru_b_a', 'delta_lru_w_x', 'delta_lru_b_x', 'delta_lru_lambda', 'delta_lru_norm_g', 'delta_fox_b_f', 'delta_fox_norm_g', 'delta_w_out', 'delta_norm2_g', 'delta_w_gate', 'delta_w_up', 'delta_w_down', 'delta_norm3_g', 'delta_w_ple_gate', 'delta_b_ple_gate', 'delta_w_ple_proj', 'delta_final_norm_g', 'new_m_norm1_g', 'new_m_w_in', 'new_m_ssd_conv_w', 'new_m_ssd_conv_b', 'new_m_ssd_dt_bias', 'new_m_ssd_a_log', 'new_m_ssd_d', 'new_m_ssd_norm_g', 'new_m_lru_conv_w', 'new_m_lru_conv_b', 'new_m_lru_w_a', 'new_m_lru_b_a', 'new_m_lru_w_x', 'new_m_lru_b_x', 'new_m_lru_lambda', 'new_m_lru_norm_g', 'new_m_fox_b_f', 'new_m_fox_norm_g', 'new_m_w_out', 'new_m_norm2_g', 'new_m_w_gate', 'new_m_w_up', 'new_m_w_down', 'new_m_norm3_g', 'new_m_w_ple_gate', 'new_m_b_ple_gate', 'new_m_w_ple_proj', 'new_m_final_norm_g', 'new_v_norm1_g', 'new_v_w_in', 'new_v_ssd_conv_w', 'new_v_ssd_conv_b', 'new_v_ssd_dt_bias', 'new_v_ssd_a_log', 'new_v_ssd_d', 'new_v_ssd_norm_g', 'new_v_lru_conv_w', 'new_v_lru_conv_b', 'new_v_lru_w_a', 'new_v_lru_b_a', 'new_v_lru_w_x', 'new_v_lru_b_x', 'new_v_lru_lambda', 'new_v_lru_norm_g', 'new_v_fox_b_f', 'new_v_fox_norm_g', 'new_v_w_out', 'new_v_norm2_g', 'new_v_w_gate', 'new_v_w_up', 'new_v_w_down', 'new_v_norm3_g', 'new_v_w_ple_gate', 'new_v_b_ple_gate', 'new_v_w_ple_proj', 'new_v_final_norm_g']
TWIN_LEAF_KINDS = {'loss': 'loss', 'grad_x': 'grad_x', 'grad_norm1_g': 'grad_w', 'grad_w_in': 'grad_w', 'grad_ssd_conv_w': 'grad_w', 'grad_ssd_conv_b': 'grad_w', 'grad_ssd_dt_bias': 'grad_w', 'grad_ssd_a_log': 'grad_w', 'grad_ssd_d': 'grad_w', 'grad_ssd_norm_g': 'grad_w', 'grad_lru_conv_w': 'grad_w', 'grad_lru_conv_b': 'grad_w', 'grad_lru_w_a': 'grad_w', 'grad_lru_b_a': 'grad_w', 'grad_lru_w_x': 'grad_w', 'grad_lru_b_x': 'grad_w', 'grad_lru_lambda': 'grad_w', 'grad_lru_norm_g': 'grad_w', 'grad_fox_b_f': 'grad_w', 'grad_fox_norm_g': 'grad_w', 'grad_w_out': 'grad_w', 'grad_norm2_g': 'grad_w', 'grad_w_gate': 'grad_w', 'grad_w_up': 'grad_w', 'grad_w_down': 'grad_w', 'grad_norm3_g': 'grad_w', 'grad_w_ple_gate': 'grad_w', 'grad_b_ple_gate': 'grad_w', 'grad_w_ple_proj': 'grad_w', 'grad_final_norm_g': 'grad_w', 'delta_norm1_g': 'delta_w', 'delta_w_in': 'delta_w', 'delta_ssd_conv_w': 'delta_w', 'delta_ssd_conv_b': 'delta_w', 'delta_ssd_dt_bias': 'delta_w', 'delta_ssd_a_log': 'delta_w', 'delta_ssd_d': 'delta_w', 'delta_ssd_norm_g': 'delta_w', 'delta_lru_conv_w': 'delta_w', 'delta_lru_conv_b': 'delta_w', 'delta_lru_w_a': 'delta_w', 'delta_lru_b_a': 'delta_w', 'delta_lru_w_x': 'delta_w', 'delta_lru_b_x': 'delta_w', 'delta_lru_lambda': 'delta_w', 'delta_lru_norm_g': 'delta_w', 'delta_fox_b_f': 'delta_w', 'delta_fox_norm_g': 'delta_w', 'delta_w_out': 'delta_w', 'delta_norm2_g': 'delta_w', 'delta_w_gate': 'delta_w', 'delta_w_up': 'delta_w', 'delta_w_down': 'delta_w', 'delta_norm3_g': 'delta_w', 'delta_w_ple_gate': 'delta_w', 'delta_b_ple_gate': 'delta_w', 'delta_w_ple_proj': 'delta_w', 'delta_final_norm_g': 'delta_w', 'new_m_norm1_g': 'new_m', 'new_m_w_in': 'new_m', 'new_m_ssd_conv_w': 'new_m', 'new_m_ssd_conv_b': 'new_m', 'new_m_ssd_dt_bias': 'new_m', 'new_m_ssd_a_log': 'new_m', 'new_m_ssd_d': 'new_m', 'new_m_ssd_norm_g': 'new_m', 'new_m_lru_conv_w': 'new_m', 'new_m_lru_conv_b': 'new_m', 'new_m_lru_w_a': 'new_m', 'new_m_lru_b_a': 'new_m', 'new_m_lru_w_x': 'new_m', 'new_m_lru_b_x': 'new_m', 'new_m_lru_lambda': 'new_m', 'new_m_lru_norm_g': 'new_m', 'new_m_fox_b_f': 'new_m', 'new_m_fox_norm_g': 'new_m', 'new_m_w_out': 'new_m', 'new_m_norm2_g': 'new_m', 'new_m_w_gate': 'new_m', 'new_m_w_up': 'new_m', 'new_m_w_down': 'new_m', 'new_m_norm3_g': 'new_m', 'new_m_w_ple_gate': 'new_m', 'new_m_b_ple_gate': 'new_m', 'new_m_w_ple_proj': 'new_m', 'new_m_final_norm_g': 'new_m', 'new_v_norm1_g': 'new_v', 'new_v_w_in': 'new_v', 'new_v_ssd_conv_w': 'new_v', 'new_v_ssd_conv_b': 'new_v', 'new_v_ssd_dt_bias': 'new_v', 'new_v_ssd_a_log': 'new_v', 'new_v_ssd_d': 'new_v', 'new_v_ssd_norm_g': 'new_v', 'new_v_lru_conv_w': 'new_v', 'new_v_lru_conv_b': 'new_v', 'new_v_lru_w_a': 'new_v', 'new_v_lru_b_a': 'new_v', 'new_v_lru_w_x': 'new_v', 'new_v_lru_b_x': 'new_v', 'new_v_lru_lambda': 'new_v', 'new_v_lru_norm_g': 'new_v', 'new_v_fox_b_f': 'new_v', 'new_v_fox_norm_g': 'new_v', 'new_v_w_out': 'new_v', 'new_v_norm2_g': 'new_v', 'new_v_w_gate': 'new_v', 'new_v_w_up': 'new_v', 'new_v_w_down': 'new_v', 'new_v_norm3_g': 'new_v', 'new_v_w_ple_gate': 'new_v', 'new_v_b_ple_gate': 'new_v', 'new_v_w_ple_proj': 'new_v', 'new_v_final_norm_g': 'new_v'}


def _forward(args):
    return _fwd_reference(*[args[k] for k in FWD_PARAMS])


def _output_shape():
    out = _jax.eval_shape(lambda: _forward(_fwd_setup_inputs(0)))
    return out.shape, out.dtype

N_MICROBATCH = 1
ADAM_LR = 0.001
ADAM_B1 = 0.9
ADAM_B2 = 0.999
ADAM_EPS = 1e-08
ADAM_WD = 0.01
ADAM_STEP = 10
PER_EXAMPLE_BATCH_AXIS = {'x': 0, 'p': 1, 'loss_target': 0}
SHARED_INPUTS = []
_WEIGHT_DTYPES = {'norm1_g': _jnp.float32, 'w_in': _jnp.float32, 'ssd_conv_w': _jnp.float32, 'ssd_conv_b': _jnp.float32, 'ssd_dt_bias': _jnp.float32, 'ssd_a_log': _jnp.float32, 'ssd_d': _jnp.float32, 'ssd_norm_g': _jnp.float32, 'lru_conv_w': _jnp.float32, 'lru_conv_b': _jnp.float32, 'lru_w_a': _jnp.float32, 'lru_b_a': _jnp.float32, 'lru_w_x': _jnp.float32, 'lru_b_x': _jnp.float32, 'lru_lambda': _jnp.float32, 'lru_norm_g': _jnp.float32, 'fox_b_f': _jnp.float32, 'fox_norm_g': _jnp.float32, 'w_out': _jnp.float32, 'norm2_g': _jnp.float32, 'w_gate': _jnp.float32, 'w_up': _jnp.float32, 'w_down': _jnp.float32, 'norm3_g': _jnp.float32, 'w_ple_gate': _jnp.float32, 'b_ple_gate': _jnp.float32, 'w_ple_proj': _jnp.float32, 'final_norm_g': _jnp.float32}
MOMENT_SCALE = {'norm1_g': 2.778457e-01, 'w_in': 1.559235e-01, 'ssd_conv_w': 1.219919e-01, 'ssd_conv_b': 1.727793e-01, 'ssd_dt_bias': 3.486714e-01, 'ssd_a_log': 9.310853e-01, 'ssd_d': 2.631228e+00, 'ssd_norm_g': 1.787750e-01, 'lru_conv_w': 1.972063e-01, 'lru_conv_b': 2.723876e+00, 'lru_w_a': 1.096320e-01, 'lru_b_a': 5.750646e-02, 'lru_w_x': 2.114463e-01, 'lru_b_x': 6.318693e-02, 'lru_lambda': 1.069370e-01, 'lru_norm_g': 2.243441e-01, 'fox_b_f': 1.001825e+00, 'fox_norm_g': 1.852414e-01, 'w_out': 1.829555e-01, 'norm2_g': 1.316631e-01, 'w_gate': 5.610456e-02, 'w_up': 5.466320e-02, 'w_down': 9.053769e-02, 'norm3_g': 3.279000e-02, 'w_ple_gate': 3.164798e-02, 'b_ple_gate': 3.886306e-02, 'w_ple_proj': 8.115624e-02, 'final_norm_g': 6.384146e+01}


def _to_microbatches(a, axis):
    t = _jnp.moveaxis(a, axis, 0)
    t = t.reshape((N_MICROBATCH, t.shape[0] // N_MICROBATCH) + t.shape[1:])
    return _jnp.moveaxis(t, 1, axis + 1)


def setup_inputs(seed: int = 0) -> dict:
    inp = _fwd_setup_inputs(seed)
    key = _jax.random.fold_in(_jax.random.key(seed), 7919)
    shape, _ = _output_shape()
    out = dict(inp)
    out["loss_target"] = _jax.random.normal(_jax.random.fold_in(key, 0), shape, _jnp.float32)
    for i, name in enumerate(TWIN_WEIGHTS):
        w = inp[name].astype(_jnp.float32)
        if MOMENT_SCALE is None:
            s = _jnp.sqrt(_jnp.mean(_jnp.square(w)) + 1e-30)
        else:
            s = MOMENT_SCALE[name]
        km, kv = _jax.random.split(_jax.random.fold_in(key, i + 1))
        out[name] = w
        out["m_" + name] = s * _jax.random.normal(km, w.shape, _jnp.float32)
        out["v_" + name] = (s * s) * _jax.random.uniform(kv, w.shape, _jnp.float32, 0.5, 1.5)
    if N_MICROBATCH > 1:
        for name, axis in PER_EXAMPLE_BATCH_AXIS.items():
            out[name] = _to_microbatches(out[name], axis)
    return {'x': out['x'], 'p': out['p'], 'norm1_g': out['norm1_g'], 'w_in': out['w_in'], 'ssd_conv_w': out['ssd_conv_w'], 'ssd_conv_b': out['ssd_conv_b'], 'ssd_dt_bias': out['ssd_dt_bias'], 'ssd_a_log': out['ssd_a_log'], 'ssd_d': out['ssd_d'], 'ssd_norm_g': out['ssd_norm_g'], 'lru_conv_w': out['lru_conv_w'], 'lru_conv_b': out['lru_conv_b'], 'lru_w_a': out['lru_w_a'], 'lru_b_a': out['lru_b_a'], 'lru_w_x': out['lru_w_x'], 'lru_b_x': out['lru_b_x'], 'lru_lambda': out['lru_lambda'], 'lru_norm_g': out['lru_norm_g'], 'fox_b_f': out['fox_b_f'], 'fox_norm_g': out['fox_norm_g'], 'w_out': out['w_out'], 'norm2_g': out['norm2_g'], 'w_gate': out['w_gate'], 'w_up': out['w_up'], 'w_down': out['w_down'], 'norm3_g': out['norm3_g'], 'w_ple_gate': out['w_ple_gate'], 'b_ple_gate': out['b_ple_gate'], 'w_ple_proj': out['w_ple_proj'], 'final_norm_g': out['final_norm_g'], 'loss_target': out['loss_target'], 'm_norm1_g': out['m_norm1_g'], 'm_w_in': out['m_w_in'], 'm_ssd_conv_w': out['m_ssd_conv_w'], 'm_ssd_conv_b': out['m_ssd_conv_b'], 'm_ssd_dt_bias': out['m_ssd_dt_bias'], 'm_ssd_a_log': out['m_ssd_a_log'], 'm_ssd_d': out['m_ssd_d'], 'm_ssd_norm_g': out['m_ssd_norm_g'], 'm_lru_conv_w': out['m_lru_conv_w'], 'm_lru_conv_b': out['m_lru_conv_b'], 'm_lru_w_a': out['m_lru_w_a'], 'm_lru_b_a': out['m_lru_b_a'], 'm_lru_w_x': out['m_lru_w_x'], 'm_lru_b_x': out['m_lru_b_x'], 'm_lru_lambda': out['m_lru_lambda'], 'm_lru_norm_g': out['m_lru_norm_g'], 'm_fox_b_f': out['m_fox_b_f'], 'm_fox_norm_g': out['m_fox_norm_g'], 'm_w_out': out['m_w_out'], 'm_norm2_g': out['m_norm2_g'], 'm_w_gate': out['m_w_gate'], 'm_w_up': out['m_w_up'], 'm_w_down': out['m_w_down'], 'm_norm3_g': out['m_norm3_g'], 'm_w_ple_gate': out['m_w_ple_gate'], 'm_b_ple_gate': out['m_b_ple_gate'], 'm_w_ple_proj': out['m_w_ple_proj'], 'm_final_norm_g': out['m_final_norm_g'], 'v_norm1_g': out['v_norm1_g'], 'v_w_in': out['v_w_in'], 'v_ssd_conv_w': out['v_ssd_conv_w'], 'v_ssd_conv_b': out['v_ssd_conv_b'], 'v_ssd_dt_bias': out['v_ssd_dt_bias'], 'v_ssd_a_log': out['v_ssd_a_log'], 'v_ssd_d': out['v_ssd_d'], 'v_ssd_norm_g': out['v_ssd_norm_g'], 'v_lru_conv_w': out['v_lru_conv_w'], 'v_lru_conv_b': out['v_lru_conv_b'], 'v_lru_w_a': out['v_lru_w_a'], 'v_lru_b_a': out['v_lru_b_a'], 'v_lru_w_x': out['v_lru_w_x'], 'v_lru_b_x': out['v_lru_b_x'], 'v_lru_lambda': out['v_lru_lambda'], 'v_lru_norm_g': out['v_lru_norm_g'], 'v_fox_b_f': out['v_fox_b_f'], 'v_fox_norm_g': out['v_fox_norm_g'], 'v_w_out': out['v_w_out'], 'v_norm2_g': out['v_norm2_g'], 'v_w_gate': out['v_w_gate'], 'v_w_up': out['v_w_up'], 'v_w_down': out['v_w_down'], 'v_norm3_g': out['v_norm3_g'], 'v_w_ple_gate': out['v_w_ple_gate'], 'v_b_ple_gate': out['v_b_ple_gate'], 'v_w_ple_proj': out['v_w_ple_proj'], 'v_final_norm_g': out['v_final_norm_g']}


def _loss(weights, diff, rest, loss_target):
    with _jax.named_scope("forward"):
        args = {**rest, TWIN_DIFF_INPUT: diff, **{k: w.astype(_WEIGHT_DTYPES[k]) for k, w in weights.items()}}
        y = _forward(args)
    with _jax.named_scope("loss_head"):
        err = _jnp.square(y.astype(_jnp.float32) - loss_target)
        return 0.5 * _jnp.sum(_jnp.mean(err, axis=-1)) if err.ndim else 0.5 * err


def _adamw(w, g, m, v):
    m = ADAM_B1 * m + (1.0 - ADAM_B1) * g
    v = ADAM_B2 * v + (1.0 - ADAM_B2) * _jnp.square(g)
    m_hat = m / (1.0 - ADAM_B1 ** ADAM_STEP)
    v_hat = v / (1.0 - ADAM_B2 ** ADAM_STEP)
    delta = -ADAM_LR * (m_hat / (_jnp.sqrt(v_hat) + ADAM_EPS) + ADAM_WD * w)
    return delta, m, v


def reference(x, p, norm1_g, w_in, ssd_conv_w, ssd_conv_b, ssd_dt_bias, ssd_a_log, ssd_d, ssd_norm_g, lru_conv_w, lru_conv_b, lru_w_a, lru_b_a, lru_w_x, lru_b_x, lru_lambda, lru_norm_g, fox_b_f, fox_norm_g, w_out, norm2_g, w_gate, w_up, w_down, norm3_g, w_ple_gate, b_ple_gate, w_ple_proj, final_norm_g, loss_target, m_norm1_g, m_w_in, m_ssd_conv_w, m_ssd_conv_b, m_ssd_dt_bias, m_ssd_a_log, m_ssd_d, m_ssd_norm_g, m_lru_conv_w, m_lru_conv_b, m_lru_w_a, m_lru_b_a, m_lru_w_x, m_lru_b_x, m_lru_lambda, m_lru_norm_g, m_fox_b_f, m_fox_norm_g, m_w_out, m_norm2_g, m_w_gate, m_w_up, m_w_down, m_norm3_g, m_w_ple_gate, m_b_ple_gate, m_w_ple_proj, m_final_norm_g, v_norm1_g, v_w_in, v_ssd_conv_w, v_ssd_conv_b, v_ssd_dt_bias, v_ssd_a_log, v_ssd_d, v_ssd_norm_g, v_lru_conv_w, v_lru_conv_b, v_lru_w_a, v_lru_b_a, v_lru_w_x, v_lru_b_x, v_lru_lambda, v_lru_norm_g, v_fox_b_f, v_fox_norm_g, v_w_out, v_norm2_g, v_w_gate, v_w_up, v_w_down, v_norm3_g, v_w_ple_gate, v_b_ple_gate, v_w_ple_proj, v_final_norm_g):
    given = dict(x=x, p=p, norm1_g=norm1_g, w_in=w_in, ssd_conv_w=ssd_conv_w, ssd_conv_b=ssd_conv_b, ssd_dt_bias=ssd_dt_bias, ssd_a_log=ssd_a_log, ssd_d=ssd_d, ssd_norm_g=ssd_norm_g, lru_conv_w=lru_conv_w, lru_conv_b=lru_conv_b, lru_w_a=lru_w_a, lru_b_a=lru_b_a, lru_w_x=lru_w_x, lru_b_x=lru_b_x, lru_lambda=lru_lambda, lru_norm_g=lru_norm_g, fox_b_f=fox_b_f, fox_norm_g=fox_norm_g, w_out=w_out, norm2_g=norm2_g, w_gate=w_gate, w_up=w_up, w_down=w_down, norm3_g=norm3_g, w_ple_gate=w_ple_gate, b_ple_gate=b_ple_gate, w_ple_proj=w_ple_proj, final_norm_g=final_norm_g, loss_target=loss_target, m_norm1_g=m_norm1_g, m_w_in=m_w_in, m_ssd_conv_w=m_ssd_conv_w, m_ssd_conv_b=m_ssd_conv_b, m_ssd_dt_bias=m_ssd_dt_bias, m_ssd_a_log=m_ssd_a_log, m_ssd_d=m_ssd_d, m_ssd_norm_g=m_ssd_norm_g, m_lru_conv_w=m_lru_conv_w, m_lru_conv_b=m_lru_conv_b, m_lru_w_a=m_lru_w_a, m_lru_b_a=m_lru_b_a, m_lru_w_x=m_lru_w_x, m_lru_b_x=m_lru_b_x, m_lru_lambda=m_lru_lambda, m_lru_norm_g=m_lru_norm_g, m_fox_b_f=m_fox_b_f, m_fox_norm_g=m_fox_norm_g, m_w_out=m_w_out, m_norm2_g=m_norm2_g, m_w_gate=m_w_gate, m_w_up=m_w_up, m_w_down=m_w_down, m_norm3_g=m_norm3_g, m_w_ple_gate=m_w_ple_gate, m_b_ple_gate=m_b_ple_gate, m_w_ple_proj=m_w_ple_proj, m_final_norm_g=m_final_norm_g, v_norm1_g=v_norm1_g, v_w_in=v_w_in, v_ssd_conv_w=v_ssd_conv_w, v_ssd_conv_b=v_ssd_conv_b, v_ssd_dt_bias=v_ssd_dt_bias, v_ssd_a_log=v_ssd_a_log, v_ssd_d=v_ssd_d, v_ssd_norm_g=v_ssd_norm_g, v_lru_conv_w=v_lru_conv_w, v_lru_conv_b=v_lru_conv_b, v_lru_w_a=v_lru_w_a, v_lru_b_a=v_lru_b_a, v_lru_w_x=v_lru_w_x, v_lru_b_x=v_lru_b_x, v_lru_lambda=v_lru_lambda, v_lru_norm_g=v_lru_norm_g, v_fox_b_f=v_fox_b_f, v_fox_norm_g=v_fox_norm_g, v_w_out=v_w_out, v_norm2_g=v_norm2_g, v_w_gate=v_w_gate, v_w_up=v_w_up, v_w_down=v_w_down, v_norm3_g=v_norm3_g, v_w_ple_gate=v_w_ple_gate, v_b_ple_gate=v_b_ple_gate, v_w_ple_proj=v_w_ple_proj, v_final_norm_g=v_final_norm_g)
    weights = {n: given[n] for n in TWIN_WEIGHTS}
    shared = {n: given[n] for n in SHARED_INPUTS}
    per_example = {n: given[n] for n in ['x', 'p']}
    grad_fn = _jax.value_and_grad(_loss, argnums=(0, 1))

    def one_microbatch(ex, loss_target):
        ex = dict(ex)
        diff = ex.pop(TWIN_DIFF_INPUT)
        return grad_fn(weights, diff, {**shared, **ex}, loss_target)

    if N_MICROBATCH == 1:
        loss, (grad_w, grad_x) = one_microbatch(per_example, given["loss_target"])
    else:
        def body(carry, xs):
            loss_sum, grad_sum = carry
            l_k, (gw_k, gx_k) = one_microbatch(xs[0], xs[1])
            with _jax.named_scope("update"):
                return (loss_sum + l_k, _jax.tree.map(_jnp.add, grad_sum, gw_k)), gx_k

        init = (_jnp.zeros((), _jnp.float32), _jax.tree.map(_jnp.zeros_like, weights))
        (loss, grad_w), grad_x = _jax.lax.scan(body, init, (per_example, given["loss_target"]))
    with _jax.named_scope("update"):
        delta_w, new_m, new_v = {}, {}, {}
        for n in TWIN_WEIGHTS:
            delta_w[n], new_m[n], new_v[n] = _adamw(weights[n], grad_w[n], given["m_" + n], given["v_" + n])
    return (loss, grad_x, *[grad_w[n] for n in TWIN_WEIGHTS], *[delta_w[n] for n in TWIN_WEIGHTS],
            *[new_m[n] for n in TWIN_WEIGHTS], *[new_v[n] for n in TWIN_WEIGHTS])
```

```python
import functools
import math

import jax
import jax.numpy as jnp
from jax import lax
from jax.experimental import pallas as pl
from jax.experimental.pallas import tpu as pltpu

F32 = jnp.float32
BF16 = jnp.bfloat16

N_DEV = 8
D_MODEL = 1024
DEPTH = 2
HEAD_DIM = 64
SSD_WIDTH = 384
LRU_WIDTH = 256
FOX_WIDTH = 384
SSD_HEADS = 6
SSD_STATE = 128
CHUNK = 128
FOX_HEADS = 6
D_FF = 2816
PLE_DIM = 256
IN_COLS = 2956
PROJ_COLS = 3072
LRU_C = 8.0
EPS = 1e-6
NEG = -1e30

ADAM_LR = 0.001
ADAM_B1 = 0.9
ADAM_B2 = 0.999
ADAM_EPS = 1e-08
ADAM_WD = 0.01
ADAM_STEP = 10

VMEM_LIMIT = 56 * 1024 * 1024

WEIGHTS = ['norm1_g', 'w_in', 'ssd_conv_w', 'ssd_conv_b', 'ssd_dt_bias', 'ssd_a_log', 'ssd_d', 'ssd_norm_g',
           'lru_conv_w', 'lru_conv_b', 'lru_w_a', 'lru_b_a', 'lru_w_x', 'lru_b_x', 'lru_lambda', 'lru_norm_g',
           'fox_b_f', 'fox_norm_g', 'w_out', 'norm2_g', 'w_gate', 'w_up', 'w_down', 'norm3_g', 'w_ple_gate',
           'b_ple_gate', 'w_ple_proj', 'final_norm_g']
BIG = {'w_in': (1, (DEPTH, D_MODEL, IN_COLS)), 'w_out': (1, (DEPTH, D_MODEL, D_MODEL)),
       'w_gate': (2, (DEPTH, D_MODEL, D_FF)), 'w_up': (2, (DEPTH, D_MODEL, D_FF)),
       'w_down': (1, (DEPTH, D_FF, D_MODEL)), 'w_ple_gate': (1, (DEPTH, D_MODEL, D_MODEL)),
       'w_ple_proj': (2, (DEPTH, PLE_DIM, D_MODEL))}
CONV_SHARDED = {'ssd_conv_w': (DEPTH, 4, 896), 'lru_conv_w': (DEPTH, 4, 256)}


def _dot(a, b):
    return jnp.dot(a, b, preferred_element_type=F32)


def _dot_nt(a, b):
    return lax.dot_general(a, b, (((1,), (1,)), ((), ())), preferred_element_type=F32)


def _dot_tn(a, b):
    return lax.dot_general(a, b, (((0,), (0,)), ((), ())), preferred_element_type=F32)


def _params(sem):
    return pltpu.CompilerParams(dimension_semantics=sem, vmem_limit_bytes=VMEM_LIMIT)


def _pick_tile(n, cap):
    if n <= cap:
        return n
    best = 128
    for t in range(128, cap + 1, 128):
        if n % t == 0:
            best = t
    assert n % best == 0, (n, cap)
    return best


def _matmul(a, b, mode, name, out_dtype=F32):
    if mode == 'tn':
        k_dim, m_dim = a.shape
        n_dim = b.shape[1]
    else:
        m_dim, k_dim = a.shape
        n_dim = b.shape[1] if mode == 'nn' else b.shape[0]
    tm = _pick_tile(m_dim, 512 if mode != 'tn' else 1024)
    tn = _pick_tile(n_dim, 1536 if mode != 'tn' else 1024)
    tk = _pick_tile(k_dim, 3072 if mode != 'tn' else 512)
    nk = k_dim // tk
    grid = (n_dim // tn, m_dim // tm, nk)

    def body(a_ref, b_ref, o_ref, acc_ref):
        kk = pl.program_id(2)

        @pl.when(kk == 0)
        def _():
            acc_ref[...] = jnp.zeros_like(acc_ref)

        if mode == 'nn':
            acc_ref[...] += _dot(a_ref[...], b_ref[...])
        elif mode == 'nt':
            acc_ref[...] += _dot_nt(a_ref[...], b_ref[...])
        else:
            acc_ref[...] += _dot_tn(a_ref[...], b_ref[...])

        @pl.when(kk == nk - 1)
        def _():
            o_ref[...] = acc_ref[...].astype(o_ref.dtype)

    if mode == 'nn':
        a_spec = pl.BlockSpec((tm, tk), lambda j, i, k: (i, k))
        b_spec = pl.BlockSpec((tk, tn), lambda j, i, k: (k, j))
    elif mode == 'nt':
        a_spec = pl.BlockSpec((tm, tk), lambda j, i, k: (i, k))
        b_spec = pl.BlockSpec((tn, tk), lambda j, i, k: (j, k))
    else:
        a_spec = pl.BlockSpec((tk, tm), lambda j, i, k: (k, i))
        b_spec = pl.BlockSpec((tk, tn), lambda j, i, k: (k, j))
    return pl.pallas_call(
        body, name=name, grid=grid,
        out_shape=jax.ShapeDtypeStruct((m_dim, n_dim), out_dtype),
        in_specs=[a_spec, b_spec],
        out_specs=pl.BlockSpec((tm, tn), lambda j, i, k: (i, j)),
        scratch_shapes=[pltpu.VMEM((tm, tn), F32)],
        compiler_params=_params(("parallel", "parallel", "arbitrary")),
    )(a, b)


def _rowcall(name, fn, tiled, consts, outs, accs=(), tile=512, scratch=()):
    specs, arrays = [], []
    for t in tiled:
        if isinstance(t, tuple):
            arr, width, blk = t
            specs.append(pl.BlockSpec((tile, width), functools.partial(lambda i, blk: (i, blk), blk=blk)))
        else:
            arr = t
            specs.append(pl.BlockSpec((tile, arr.shape[1]), lambda i: (i, 0)))
        arrays.append(arr)
    rows = arrays[0].shape[0]
    assert rows % tile == 0, (name, rows, tile)
    for c in consts:
        specs.append(pl.BlockSpec(c.shape, lambda i: (0, 0)))
        arrays.append(c)
    n_in, n_out, n_acc = len(arrays), len(outs), len(accs)
    out_shape = [jax.ShapeDtypeStruct((rows, c), dt) for c, dt in outs]
    out_specs = [pl.BlockSpec((tile, c), lambda i: (i, 0)) for c, _ in outs]
    out_shape += [jax.ShapeDtypeStruct(s, dt) for s, dt in accs]
    out_specs += [pl.BlockSpec(s, lambda i: (0, 0)) for s, _ in accs]

    def body(*refs):
        ins = [r[...] for r in refs[:n_in]]
        out_refs = refs[n_in:n_in + n_out]
        acc_refs = refs[n_in + n_out:n_in + n_out + n_acc]
        scr = refs[n_in + n_out + n_acc:]
        res = fn(*ins, *scr)
        if not isinstance(res, (tuple, list)):
            res = (res,)
        assert len(res) == n_out + n_acc, (name, len(res))
        for r, v in zip(out_refs, res[:n_out]):
            r[...] = v.astype(r.dtype)
        if n_acc:
            first = pl.program_id(0) == 0

            @pl.when(first)
            def _():
                for r, v in zip(acc_refs, res[n_out:]):
                    r[...] = v.astype(r.dtype)

            @pl.when(jnp.logical_not(first))
            def _():
                for r, v in zip(acc_refs, res[n_out:]):
                    r[...] += v.astype(r.dtype)

    res = pl.pallas_call(
        body, name=name, grid=(rows // tile,),
        out_shape=out_shape, in_specs=specs, out_specs=out_specs,
        scratch_shapes=list(scratch),
        compiler_params=_params(("arbitrary",)),
    )(*arrays)
    return res


def _sigmoid(x):
    return 1.0 / (1.0 + jnp.exp(-x))


def _softplus(x):
    return jnp.maximum(x, 0.0) + jnp.log(1.0 + jnp.exp(-jnp.abs(x)))


def _silu(x):
    return x * _sigmoid(x)


def _dsilu(x):
    s = _sigmoid(x)
    return s * (1.0 + x * (1.0 - s))


_GELU_C = math.sqrt(2.0 / math.pi)


def _gelu(x):
    return 0.5 * x * (1.0 + jnp.tanh(_GELU_C * (x + 0.044715 * x * x * x)))


def _dgelu(x):
    t = jnp.tanh(_GELU_C * (x + 0.044715 * x * x * x))
    return 0.5 * (1.0 + t) + 0.5 * x * (1.0 - t * t) * _GELU_C * (1.0 + 3.0 * 0.044715 * x * x)


def _neg_expm1(x):
    series = -x * (1.0 + x * (0.5 + x * (1.0 / 6.0 + x * (1.0 / 24.0 + x * (1.0 / 120.0)))))
    return jnp.where(x > -0.03, series, 1.0 - jnp.exp(x))


def _rms(x, g):
    r = lax.rsqrt(jnp.mean(x * x, axis=-1, keepdims=True) + EPS)
    return x * r * g


def _rms_bwd(x, g, dy):
    r = lax.rsqrt(jnp.mean(x * x, axis=-1, keepdims=True) + EPS)
    xhat = x * r
    dg = jnp.sum(dy * xhat, axis=0, keepdims=True)
    dxhat = dy * g
    dx = r * (dxhat - xhat * jnp.mean(dxhat * xhat, axis=-1, keepdims=True))
    return dx, dg


def _row_iota(shape):
    return lax.broadcasted_iota(jnp.int32, shape, 0)


def _shift_down(x, j):
    if j == 0:
        return x
    return jnp.where(_row_iota(x.shape) >= j, pltpu.roll(x, j, 0), 0.0)


def _shift_up(x, j):
    if j == 0:
        return x
    n = x.shape[0]
    return jnp.where(_row_iota(x.shape) < n - j, pltpu.roll(x, n - j, 0), 0.0)


def _conv(x, w, b):
    y = b + w[3:4, :] * x
    for k in range(3):
        y = y + w[k:k + 1, :] * _shift_down(x, 3 - k)
    return y


def _conv_bwd(x, w, dy):
    dx = w[3:4, :] * dy
    dws = []
    for k in range(3):
        dx = dx + w[k:k + 1, :] * _shift_up(dy, 3 - k)
        dws.append(jnp.sum(dy * _shift_down(x, 3 - k), axis=0, keepdims=True))
    dws.append(jnp.sum(dy * x, axis=0, keepdims=True))
    return dx, jnp.concatenate(dws, axis=0), jnp.sum(dy, axis=0, keepdims=True)


def _split3(x):
    hi = x.astype(BF16)
    r1 = x - hi.astype(F32)
    mid = r1.astype(BF16)
    lo = (r1 - mid.astype(F32)).astype(BF16)
    return hi, mid, lo


def _tri_dot(tri, x):
    hi, mid, lo = _split3(x)
    return _dot(tri, hi) + _dot(tri, mid) + _dot(tri, lo)


def _cumsum_rows(x):
    n = x.shape[0] // CHUNK
    r = lax.broadcasted_iota(jnp.int32, (CHUNK, CHUNK), 0)
    c = lax.broadcasted_iota(jnp.int32, (CHUNK, CHUNK), 1)
    tri = (r >= c).astype(BF16)
    carry = jnp.zeros((1, x.shape[1]), F32)
    cums, prevs, ends = [], [], []
    for i in range(n):
        blk = _tri_dot(tri, x[i * CHUNK:(i + 1) * CHUNK]) + carry
        prevs.append(jnp.broadcast_to(carry, blk.shape))
        carry = blk[CHUNK - 1:CHUNK, :]
        ends.append(jnp.broadcast_to(carry, blk.shape))
        cums.append(blk)
    return jnp.concatenate(cums, 0), jnp.concatenate(prevs, 0), jnp.concatenate(ends, 0)


def _rev_cumsum_rows(x):
    n = x.shape[0] // CHUNK
    r = lax.broadcasted_iota(jnp.int32, (CHUNK, CHUNK), 0)
    c = lax.broadcasted_iota(jnp.int32, (CHUNK, CHUNK), 1)
    tri = (r <= c).astype(BF16)
    carry = jnp.zeros((1, x.shape[1]), F32)
    local, whole = [None] * n, [None] * n
    for i in range(n - 1, -1, -1):
        local[i] = _tri_dot(tri, x[i * CHUNK:(i + 1) * CHUNK])
        whole[i] = local[i] + carry
        carry = whole[i][0:1, :]
    return jnp.concatenate(local, 0), jnp.concatenate(whole, 0)


def _convsilu_fwd(name, seg, seq, w, b, dtype):
    return _rowcall(name, lambda raw, w, b: _silu(_conv(raw, w, b)), [seg], [w, b], [(seg[1], dtype)], tile=seq)[0]


def _convsilu_bwd(name, seg, seq, w, b, dy):
    def fn(raw, dy, w, b):
        return _conv_bwd(raw, w, dy * _dsilu(_conv(raw, w, b)))

    width = seg[1]
    return _rowcall(name, fn, [seg, dy], [w, b], [(width, BF16)], [((4, width), F32), ((1, width), F32)], tile=seq)


def _small_fwd(seg, seq, bias128, alog128):
    def fn(small, bias, alog):
        lane = lax.broadcasted_iota(jnp.int32, small.shape, 1)
        a = -jnp.exp(alog)
        s = small + bias
        dt = _softplus(s)
        logf = -_softplus(-s)
        pre = jnp.where(lane < SSD_HEADS, a * dt, jnp.where(lane < 2 * SSD_HEADS, logf, 0.0))
        cum, prev, end = _cumsum_rows(pre)
        return dt, cum, prev, end

    return _rowcall("small_fwd", fn, [seg], [bias128, alog128], [(128, F32)] * 4, tile=seq)


def _small_bwd(seg, seq, dcum, dend, ddt, dt_arr, bias128, alog128):
    def fn(small, dcum, dend, ddt, dt_arr, bias, alog):
        lane = lax.broadcasted_iota(jnp.int32, small.shape, 1)
        a = -jnp.exp(alog)
        sig = _sigmoid(small + bias)
        local, whole = _rev_cumsum_rows(dcum)
        dadt = local + dend
        d_dt = ddt + a * dadt
        ds = jnp.where(lane < SSD_HEADS, d_dt * sig, jnp.where(lane < 2 * SSD_HEADS, whole * (1.0 - sig), 0.0))
        da = jnp.sum(jnp.where(lane < SSD_HEADS, dadt * dt_arr, 0.0), axis=0, keepdims=True)
        return ds, jnp.sum(ds, axis=0, keepdims=True), da * a

    return _rowcall("small_bwd", fn, [seg, dcum, dend, ddt, dt_arr], [bias128, alog128], [(128, BF16)],
                    [((1, 128), F32), ((1, 128), F32)], tile=seq)


HEAD_PAIRS = SSD_HEADS // 2


def _ssd_specs(nc, reverse):
    def at(c):
        return nc - 1 - c if reverse else c

    x_spec = pl.BlockSpec((1, CHUNK, SSD_WIDTH), lambda b, c: (b, at(c), 0))
    bc_spec = pl.BlockSpec((1, CHUNK, 256), lambda b, c: (b, at(c), 0))
    col_spec = pl.BlockSpec((1, CHUNK, 128), lambda b, c: (b, at(c), 0))
    row_spec = pl.BlockSpec((1, 8, CHUNK), lambda b, c: (b, 0, at(c)))
    st_spec = pl.BlockSpec((1, 1, HEAD_PAIRS, SSD_STATE, 128), lambda b, c: (b, at(c), 0, 0, 0))
    return x_spec, bc_spec, col_spec, row_spec, st_spec


def _ssd_head(h, dtb, acb, apb, aeb, arb):
    return dtb[:, h:h + 1], acb[:, h:h + 1], apb[:, h:h + 1], aeb[:, h:h + 1], arb[h:h + 1, :]


def _ssd_fwd(x, bm, cm, dt_arr, cum, prev, end, a_row):
    nb, seq, _ = x.shape
    nc = seq // CHUNK
    x_spec, bc_spec, col_spec, row_spec, st_spec = _ssd_specs(nc, False)

    def body(x_ref, b_ref, c_ref, dt_ref, ac_ref, ap_ref, ae_ref, ar_ref, y_ref, st_ref, s_scr):
        @pl.when(pl.program_id(1) == 0)
        def _():
            s_scr[...] = jnp.zeros_like(s_scr)

        causal = (lax.broadcasted_iota(jnp.int32, (CHUNK, CHUNK), 0)
                  >= lax.broadcasted_iota(jnp.int32, (CHUNK, CHUNK), 1))
        low = lax.broadcasted_iota(jnp.int32, (CHUNK, 128), 1) < HEAD_DIM
        cols = (dt_ref[0], ac_ref[0], ap_ref[0], ae_ref[0], ar_ref[0])
        bcs = [b_ref[0, :, g * 128:(g + 1) * 128] for g in range(2)]
        ccs = [c_ref[0, :, g * 128:(g + 1) * 128] for g in range(2)]
        ms = [_dot_nt(ccs[g], bcs[g]) for g in range(2)]
        for pi in range(HEAD_PAIRS):
            x2 = x_ref[0, :, pi * 128:(pi + 1) * 128]
            dt2 = jnp.where(low, cols[0][:, 2 * pi:2 * pi + 1], cols[0][:, 2 * pi + 1:2 * pi + 2])
            xdt = (x2 * dt2).astype(BF16)
            sprev = s_scr[pi]
            st_ref[0, 0, pi] = sprev
            spb = sprev.astype(BF16)
            ys, us = [], []
            for h in (2 * pi, 2 * pi + 1):
                g = h // 3
                _, ac, ap, ae, ar = _ssd_head(h, *cols)
                lm = jnp.exp(jnp.where(causal, ac - ar, NEG))
                gm = (ms[g] * lm).astype(BF16)
                ys.append(_dot(gm, xdt) + jnp.exp(ac - ap) * _dot(ccs[g], spb))
                bdec = (bcs[g].astype(F32) * jnp.exp(ae - ac)).astype(BF16)
                us.append(jnp.exp(ae[0:1, :] - ap[0:1, :]) * sprev + _dot_tn(bdec, xdt))
            y_ref[0, :, pi * 128:(pi + 1) * 128] = jnp.where(low, ys[0], ys[1])
            s_scr[pi] = jnp.where(low, us[0], us[1])

    return pl.pallas_call(
        body, name="ssd_fwd", grid=(nb, nc),
        out_shape=[jax.ShapeDtypeStruct(x.shape, F32),
                   jax.ShapeDtypeStruct((nb, nc, HEAD_PAIRS, SSD_STATE, 128), F32)],
        in_specs=[x_spec, bc_spec, bc_spec, col_spec, col_spec, col_spec, col_spec, row_spec],
        out_specs=[x_spec, st_spec],
        scratch_shapes=[pltpu.VMEM((HEAD_PAIRS, SSD_STATE, 128), F32)],
        compiler_params=_params(("parallel", "arbitrary")),
    )(x, bm, cm, dt_arr, cum, prev, end, a_row)


def _ssd_bwd(x_h, bm, cm, dt_arr, cum, prev, end, a_row, states, dy_h):
    nb, seq, _ = x_h.shape
    nc = seq // CHUNK
    x_spec, bc_spec, col_spec, row_spec, st_spec = _ssd_specs(nc, True)

    def body(x_ref, b_ref, c_ref, dt_ref, ac_ref, ap_ref, ae_ref, ar_ref, st_ref, dy_ref,
             dx_ref, db_ref, dc_ref, da_ref, dend_ref, ddt_ref, ds_scr):
        @pl.when(pl.program_id(1) == 0)
        def _():
            ds_scr[...] = jnp.zeros_like(ds_scr)

        causal = (lax.broadcasted_iota(jnp.int32, (CHUNK, CHUNK), 0)
                  >= lax.broadcasted_iota(jnp.int32, (CHUNK, CHUNK), 1))
        lane = lax.broadcasted_iota(jnp.int32, (CHUNK, 128), 1)
        low = lane < HEAD_DIM
        cols = (dt_ref[0], ac_ref[0], ap_ref[0], ae_ref[0], ar_ref[0])
        bcs = [b_ref[0, :, g * 128:(g + 1) * 128] for g in range(2)]
        ccs = [c_ref[0, :, g * 128:(g + 1) * 128] for g in range(2)]
        ms = [_dot_nt(ccs[g], bcs[g]) for g in range(2)]
        dms = [jnp.zeros((CHUNK, CHUNK), F32) for _ in range(2)]
        dc_accs = [jnp.zeros((CHUNK, SSD_STATE), F32) for _ in range(2)]
        db_accs = [jnp.zeros((CHUNK, SSD_STATE), F32) for _ in range(2)]
        da_blk = jnp.zeros((CHUNK, 128), F32)
        dend_blk = jnp.zeros((CHUNK, 128), F32)
        ddt_blk = jnp.zeros((CHUNK, 128), F32)
        for pi in range(HEAD_PAIRS):
            x2 = x_ref[0, :, pi * 128:(pi + 1) * 128]
            dy2 = dy_ref[0, :, pi * 128:(pi + 1) * 128]
            dt2 = jnp.where(low, cols[0][:, 2 * pi:2 * pi + 1], cols[0][:, 2 * pi + 1:2 * pi + 2])
            xdt_f = x2 * dt2
            xdt = xdt_f.astype(BF16)
            dyb = dy2.astype(BF16)
            dsn = ds_scr[pi]
            dsb = dsn.astype(BF16)
            sprev_f = st_ref[0, 0, pi]
            sprev = sprev_f.astype(BF16)
            dxdts, dss = [], []
            for h in (2 * pi, 2 * pi + 1):
                g = h // 3
                mine = low if h % 2 == 0 else jnp.logical_not(low)
                _, ac, ap, ae, ar = _ssd_head(h, *cols)
                bc, cc, m = bcs[g], ccs[g], ms[g]
                lm = jnp.exp(jnp.where(causal, ac - ar, NEG))
                gm = (m * lm).astype(BF16)
                dy_m = jnp.where(mine, dy2, 0.0)
                dyb_m = dy_m.astype(BF16)
                xdt_m = jnp.where(mine, xdt_f, 0.0)
                e_in = jnp.exp(ac - ap)
                f_out = jnp.exp(ae - ac)
                whole = jnp.exp(ae[0:1, :] - ap[0:1, :])
                dg = _dot_nt(dyb_m, xdt)
                dxdt_off = f_out * _dot(bc, dsb)
                dxdt = _dot_tn(gm, dyb) + dxdt_off
                dmj = dg * lm
                dms[g] = dms[g] + dmj
                dc_accs[g] = dc_accs[g] + e_in * _dot_nt(dyb_m, sprev)
                db_accs[g] = db_accs[g] + f_out * _dot_nt(xdt_m.astype(BF16), dsb)
                dss.append(whole * dsn + _dot_tn((cc.astype(F32) * e_in).astype(BF16), dyb))
                wmat = dmj * m
                r_in = jnp.sum(dy_m * (e_in * _dot(cc, sprev)), axis=1, keepdims=True)
                q_out = jnp.sum(xdt_m * dxdt_off, axis=1, keepdims=True)
                daj = (jnp.sum(wmat, axis=1, keepdims=True) - jnp.sum(wmat.T, axis=1, keepdims=True)
                       + r_in - q_out)
                cross = jnp.where(mine, dsn * sprev_f, 0.0)
                dendj = (jnp.sum(q_out, axis=0, keepdims=True)
                         + whole * jnp.sum(jnp.sum(cross, axis=1, keepdims=True), axis=0, keepdims=True))
                ddtj = jnp.sum(jnp.where(mine, dxdt * x2, 0.0), axis=1, keepdims=True)
                dxdts.append(dxdt)
                da_blk = jnp.where(lane == h, daj, da_blk)
                dend_blk = jnp.where(lane == h, dendj, dend_blk)
                ddt_blk = jnp.where(lane == h, ddtj, ddt_blk)
            dx_ref[0, :, pi * 128:(pi + 1) * 128] = jnp.where(low, dxdts[0], dxdts[1]) * dt2
            ds_scr[pi] = jnp.where(low, dss[0], dss[1])
        for g in range(2):
            dmb = dms[g].astype(BF16)
            dc_ref[0, :, g * 128:(g + 1) * 128] = dc_accs[g] + _dot(dmb, bcs[g])
            db_ref[0, :, g * 128:(g + 1) * 128] = db_accs[g] + _dot_tn(dmb, ccs[g])
        da_ref[0] = da_blk
        dend_ref[0] = dend_blk
        ddt_ref[0] = ddt_blk

    col_shape = jax.ShapeDtypeStruct((nb, seq, 128), F32)
    return pl.pallas_call(
        body, name="ssd_bwd", grid=(nb, nc),
        out_shape=[jax.ShapeDtypeStruct(x_h.shape, F32),
                   jax.ShapeDtypeStruct((nb, seq, 256), F32), jax.ShapeDtypeStruct((nb, seq, 256), F32),
                   col_shape, col_shape, col_shape],
        in_specs=[x_spec, bc_spec, bc_spec, col_spec, col_spec, col_spec, col_spec, row_spec, st_spec, x_spec],
        out_specs=[x_spec, bc_spec, bc_spec, col_spec, col_spec, col_spec],
        scratch_shapes=[pltpu.VMEM((HEAD_PAIRS, SSD_STATE, 128), F32)],
        compiler_params=_params(("parallel", "arbitrary")),
    )(x_h, bm, cm, dt_arr, cum, prev, end, a_row, states, dy_h)


def _lru_gates(xl, wa, ba, wx, bx, lam):
    xb = xl.astype(BF16)
    r = _sigmoid(_dot(xb, wa) + ba)
    i = _sigmoid(_dot(xb, wx) + bx)
    sp = _softplus(-lam)
    log_a = -LRU_C * r * sp
    a = jnp.exp(log_a)
    mult = jnp.sqrt(_neg_expm1(2.0 * log_a))
    return r, i, sp, log_a, a, mult


def _scan_chunks(a_ref, u_ref, h_ref, seq, reverse):
    nc = seq // CHUNK
    width = a_ref.shape[1]
    row = lax.broadcasted_iota(jnp.int32, (CHUNK, width), 0)

    def chunk(ci, carry):
        c = nc - 1 - ci if reverse else ci
        rows = pl.ds(pl.multiple_of(c * CHUNK, CHUNK), CHUNK)
        av, bv = a_ref[rows, :], u_ref[rows, :]
        d = 1
        while d < CHUNK:
            if reverse:
                keep = row < CHUNK - d
                a_sh = jnp.where(keep, pltpu.roll(av, CHUNK - d, 0), 1.0)
                b_sh = jnp.where(keep, pltpu.roll(bv, CHUNK - d, 0), 0.0)
            else:
                keep = row >= d
                a_sh = jnp.where(keep, pltpu.roll(av, d, 0), 1.0)
                b_sh = jnp.where(keep, pltpu.roll(bv, d, 0), 0.0)
            bv = av * b_sh + bv
            av = av * a_sh
            d *= 2
        hv = bv + av * carry
        h_ref[rows, :] = hv
        return hv[0:1, :] if reverse else hv[CHUNK - 1:CHUNK, :]

    lax.fori_loop(0, nc, chunk, jnp.zeros((1, width), F32))


def _lru_fwd(proj, seq, cw, cb, wa, ba, wx, bx, lam):
    def fn(raw, cw, cb, wa, ba, wx, bx, lam, a_scr, u_scr, h_scr):
        xl = _conv(raw, cw, cb)
        r, i, sp, log_a, a, mult = _lru_gates(xl, wa, ba, wx, bx, lam)
        a_scr[...] = a
        u_scr[...] = mult * (i * xl)
        _scan_chunks(a_scr, u_scr, h_scr, seq, reverse=False)
        return h_scr[...], xl

    return _rowcall("lru_fwd", fn, [(proj, 256, 2)], [cw, cb, wa, ba, wx, bx, lam],
                    [(256, F32), (256, F32)], tile=seq,
                    scratch=[pltpu.VMEM((seq, 256), F32)] * 3)


def _lru_bwd(proj, seq, xl_all, h_all, dh_all, cw, cb, wa, ba, wx, bx, lam):
    def fn(raw, xl, hseq, dh, cw, cb, wa, ba, wx, bx, lam, a_scr, u_scr, h_scr):
        r, i, sp, log_a, a, mult = _lru_gates(xl, wa, ba, wx, bx, lam)
        a_scr[...] = _shift_up(a, 1)
        u_scr[...] = dh
        _scan_chunks(a_scr, u_scr, h_scr, seq, reverse=True)
        dht = h_scr[...]
        da = dht * _shift_down(hseq, 1)
        gated = i * xl
        dgated = dht * mult
        dmult = dht * gated
        dlog_a = da * a - dmult * (a * a) / mult
        dr = dlog_a * (-LRU_C * sp)
        dsp = jnp.sum(dlog_a * (-LRU_C * r), axis=0, keepdims=True)
        dlam = -dsp * _sigmoid(-lam)
        dpa = dr * r * (1.0 - r)
        dpx = (dgated * xl) * i * (1.0 - i)
        dpa_b, dpx_b = dpa.astype(BF16), dpx.astype(BF16)
        dxl = dgated * i + _dot_nt(dpa_b, wa) + _dot_nt(dpx_b, wx)
        xb = xl.astype(BF16)
        dwa = _dot_tn(xb, dpa_b)
        dwx = _dot_tn(xb, dpx_b)
        draw, dcw, dcb = _conv_bwd(raw, cw, dxl)
        return (draw, dcw, dcb, dwa, jnp.sum(dpa, axis=0, keepdims=True), dwx,
                jnp.sum(dpx, axis=0, keepdims=True), dlam)

    return _rowcall("lru_bwd", fn, [(proj, 256, 2), xl_all, h_all, dh_all], [cw, cb, wa, ba, wx, bx, lam],
                    [(256, BF16)],
                    [((4, 256), F32), ((1, 256), F32), ((256, 256), F32), ((1, 256), F32), ((256, 256), F32),
                     ((1, 256), F32), ((1, 256), F32)],
                    tile=seq, scratch=[pltpu.VMEM((seq, 256), F32)] * 3)


FOX_SCALE = HEAD_DIM ** -0.5


def _fox_fwd(q, k, v, cum_col, cum_row):
    nb, nh, seq, e = q.shape
    nq = seq // CHUNK

    def body(q_ref, k_ref, v_ref, cq_ref, ck_ref, o_ref, lse_ref):
        qi = pl.program_id(2)
        qv = q_ref[0, 0]
        cq = cq_ref[0, 0]
        rowi = lax.broadcasted_iota(jnp.int32, (CHUNK, CHUNK), 0)
        coli = lax.broadcasted_iota(jnp.int32, (CHUNK, CHUNK), 1)

        def step(j, carry):
            m_i, l_i, acc = carry
            cols = pl.ds(pl.multiple_of(j * CHUNK, CHUNK), CHUNK)
            kv = k_ref[0, 0, cols, :]
            vv = v_ref[0, 0, cols, :]
            ck = ck_ref[0, 0, :, cols]
            s = _dot_nt(qv, kv) * FOX_SCALE + cq - ck
            s = jnp.where(qi * CHUNK + rowi >= j * CHUNK + coli, s, NEG)
            m_new = jnp.maximum(m_i, jnp.max(s, axis=1, keepdims=True))
            p = jnp.exp(s - m_new)
            alpha = jnp.exp(m_i - m_new)
            l_new = alpha * l_i + jnp.sum(p, axis=1, keepdims=True)
            acc_new = alpha * acc + _dot(p.astype(BF16), vv)
            return m_new, l_new, acc_new

        init = (jnp.full((CHUNK, 1), NEG, F32), jnp.zeros((CHUNK, 1), F32), jnp.zeros((CHUNK, e), F32))
        m_i, l_i, acc = lax.fori_loop(0, qi + 1, step, init)
        o_ref[0, 0] = acc / l_i
        lse_ref[0, 0] = m_i + jnp.log(l_i)

    blk = pl.BlockSpec((1, 1, CHUNK, e), lambda b, h, i: (b, h, i, 0))
    full = pl.BlockSpec((1, 1, seq, e), lambda b, h, i: (b, h, 0, 0))
    col_blk = pl.BlockSpec((1, 1, CHUNK, 1), lambda b, h, i: (b, h, i, 0))
    row_full = pl.BlockSpec((1, 1, 1, seq), lambda b, h, i: (b, h, 0, 0))
    return pl.pallas_call(
        body, name="fox_fwd", grid=(nb, nh, nq),
        out_shape=[jax.ShapeDtypeStruct((nb, nh, seq, e), F32), jax.ShapeDtypeStruct((nb, nh, seq, 1), F32)],
        in_specs=[blk, full, full, col_blk, row_full],
        out_specs=[blk, col_blk],
        compiler_params=_params(("parallel", "parallel", "arbitrary")),
    )(q, k, v, cum_col, cum_row)


def _fox_bwd(q, k, v, o, lse, do, cum_col, cum_row):
    nb, nh, seq, e = q.shape
    nq = seq // CHUNK

    def body(q_ref, k_ref, v_ref, o_ref, lse_ref, do_ref, cq_ref, ck_ref, dq_ref, dk_ref, dv_ref, dcum_ref,
             dcq_ref):
        kj = pl.program_id(2)

        @pl.when(kj == 0)
        def _():
            dq_ref[...] = jnp.zeros_like(dq_ref)
            dcq_ref[...] = jnp.zeros_like(dcq_ref)

        kv = k_ref[0, 0]
        vv = v_ref[0, 0]
        ck = ck_ref[0, 0]
        rowi = lax.broadcasted_iota(jnp.int32, (CHUNK, CHUNK), 0)
        coli = lax.broadcasted_iota(jnp.int32, (CHUNK, CHUNK), 1)

        def step(i, carry):
            dk, dv, csum = carry
            rows = pl.ds(pl.multiple_of(i * CHUNK, CHUNK), CHUNK)
            qv = q_ref[0, 0, rows, :]
            dov = do_ref[0, 0, rows, :]
            dob = dov.astype(BF16)
            delta = jnp.sum(dov * o_ref[0, 0, rows, :], axis=1, keepdims=True)
            s = _dot_nt(qv, kv) * FOX_SCALE + cq_ref[0, 0, rows, :] - ck
            s = jnp.where(i * CHUNK + rowi >= kj * CHUNK + coli, s, NEG)
            p = jnp.exp(s - lse_ref[0, 0, rows, :])
            dp = _dot_nt(dob, vv)
            ds = p * (dp - delta)
            dsb = ds.astype(BF16)
            dv = dv + _dot_tn(p.astype(BF16), dob)
            dk = dk + _dot_tn(dsb, qv)
            dq_ref[0, 0, rows, :] += _dot(dsb, kv) * FOX_SCALE
            dcq_ref[0, 0, rows, :] += jnp.sum(ds, axis=1, keepdims=True)
            csum = csum + jnp.sum(ds, axis=0, keepdims=True)
            return dk, dv, csum

        init = (jnp.zeros((CHUNK, e), F32), jnp.zeros((CHUNK, e), F32), jnp.zeros((1, CHUNK), F32))
        dk, dv, csum = lax.fori_loop(kj, nq, step, init)
        dk_ref[0, 0] = dk * FOX_SCALE
        dv_ref[0, 0] = dv
        dcum_ref[0, 0] = -csum

    blk = pl.BlockSpec((1, 1, CHUNK, e), lambda b, h, j: (b, h, j, 0))
    full = pl.BlockSpec((1, 1, seq, e), lambda b, h, j: (b, h, 0, 0))
    col_full = pl.BlockSpec((1, 1, seq, 1), lambda b, h, j: (b, h, 0, 0))
    row_blk = pl.BlockSpec((1, 1, 1, CHUNK), lambda b, h, j: (b, h, 0, j))
    return pl.pallas_call(
        body, name="fox_bwd", grid=(nb, nh, nq),
        out_shape=[jax.ShapeDtypeStruct((nb, nh, seq, e), F32), jax.ShapeDtypeStruct((nb, nh, seq, e), F32),
                   jax.ShapeDtypeStruct((nb, nh, seq, e), F32), jax.ShapeDtypeStruct((nb, nh, 1, seq), F32),
                   jax.ShapeDtypeStruct((nb, nh, seq, 1), F32)],
        in_specs=[full, blk, blk, full, col_full, full, col_full, row_blk],
        out_specs=[full, blk, blk, row_blk, col_full],
        compiler_params=_params(("parallel", "parallel", "arbitrary")),
    )(q, k, v, o, lse, do, cum_col, cum_row)


_ANY = pl.BlockSpec(memory_space=pl.ANY)


def _place():
    return lax.axis_index("x"), lax.axis_index("y"), lax.axis_index("c")


def _all_gather(shard, name):
    def body(x_ref, out_ref, send_sems, recv_sems, local_sem):
        x, y, c = _place()
        me, sibling = (x, y, c), (x, y, 1 - c)
        chips = [(1 - x, y), (x, 1 - y), (1 - x, 1 - y)]

        def rows(px, py, pc):
            return out_ref.at[4 * px + 2 * py + pc]

        def copy(k, block, to, src=None):
            return pltpu.make_async_remote_copy(
                src_ref=rows(*block) if src is None else src, dst_ref=rows(*block),
                send_sem=send_sems.at[k], recv_sem=recv_sems.at[k],
                device_id=to, device_id_type=pl.DeviceIdType.MESH)

        mine = pltpu.make_async_copy(x_ref, rows(*me), local_sem)
        mine.start()
        first = [copy(0, me, sibling, src=x_ref)]
        first += [copy(1 + j, me, (*chip, c), src=x_ref) for j, chip in enumerate(chips)]
        for cp in first:
            cp.start()
        passed = [copy(4 + j, (*chip, c), sibling) for j, chip in enumerate(chips)]
        for j, chip in enumerate(chips):
            copy(1 + j, (*chip, c), me).wait_recv()
            passed[j].start()
        copy(0, sibling, me).wait_recv()
        for j, chip in enumerate(chips):
            copy(4 + j, (*chip, 1 - c), me).wait_recv()
        for cp in first + passed:
            cp.wait_send()
        mine.wait()

    return pl.pallas_call(
        body, name=name,
        out_shape=jax.ShapeDtypeStruct((N_DEV,) + shard.shape, shard.dtype),
        in_specs=[_ANY], out_specs=_ANY,
        scratch_shapes=[pltpu.SemaphoreType.DMA((7,)), pltpu.SemaphoreType.DMA((7,)), pltpu.SemaphoreType.DMA],
    )(shard)


def _sibling_exchange(full, name):
    def body(g_ref, out_ref, send_sems, recv_sems):
        x, y, c = _place()
        copies = []
        for k in range(4):
            px, py = k // 2, k % 2
            copies.append(pltpu.make_async_remote_copy(
                src_ref=g_ref.at[4 * px + 2 * py + (1 - c)], dst_ref=out_ref.at[k],
                send_sem=send_sems.at[k], recv_sem=recv_sems.at[k],
                device_id=(x, y, 1 - c), device_id_type=pl.DeviceIdType.MESH))
        for cp in copies:
            cp.start()
        for cp in copies:
            cp.wait()

    return pl.pallas_call(
        body, name=name,
        out_shape=jax.ShapeDtypeStruct((4,) + full.shape[1:], full.dtype),
        in_specs=[_ANY], out_specs=_ANY,
        scratch_shapes=[pltpu.SemaphoreType.DMA((4,)), pltpu.SemaphoreType.DMA((4,))],
    )(full)


def _chip_exchange(part, name):
    def body(p_ref, out_ref, send_sems, recv_sems):
        x, y, c = _place()
        peers = [(1 - x, y), (x, 1 - y), (1 - x, 1 - y)]
        copies = []
        for k, (px, py) in enumerate(peers):
            copies.append(pltpu.make_async_remote_copy(
                src_ref=p_ref.at[2 * px + py], dst_ref=out_ref.at[k],
                send_sem=send_sems.at[k], recv_sem=recv_sems.at[k],
                device_id=(px, py, c), device_id_type=pl.DeviceIdType.MESH))
        for cp in copies:
            cp.start()
        for cp in copies:
            cp.wait()

    return pl.pallas_call(
        body, name=name,
        out_shape=jax.ShapeDtypeStruct((3,) + part.shape[1:], part.dtype),
        in_specs=[_ANY], out_specs=_ANY,
        scratch_shapes=[pltpu.SemaphoreType.DMA((3,)), pltpu.SemaphoreType.DMA((3,))],
    )(part)


def _reduce_scatter(full):
    x, y, c = _place()
    rows = full.shape[1]
    mine4 = jnp.stack([lax.dynamic_index_in_dim(full, 4 * (k // 2) + 2 * (k % 2) + c, 0, keepdims=False)
                       for k in range(4)])
    got = _sibling_exchange(full, "rs_sibling")
    tile = _pick_rows(rows)
    part = _rowcall("rs_pair_sum", lambda a, b: a + b,
                    [mine4.reshape(4 * rows, 1024), got.reshape(4 * rows, 1024)], [], [(1024, F32)],
                    tile=tile)[0].reshape(4, rows, 1024)
    others = _chip_exchange(part, "rs_chips")
    own = lax.dynamic_index_in_dim(part, 2 * x + y, 0, keepdims=False)
    return _rowcall("rs_chip_sum", lambda a, b, c_, d: ((a + b) + c_) + d,
                    [own, others[0], others[1], others[2]], [], [(1024, F32)], tile=tile)[0]


def _pick_rows(rows):
    for t in (512, 256, 128, 64, 32, 16, 8):
        if rows % t == 0:
            return t
    raise ValueError(rows)


def _all_reduce_small(vec):
    gathered = _all_gather(vec, "ar_gather")
    rows = vec.shape[0]

    def fn(*blocks):
        s = blocks[0]
        for b in blocks[1:]:
            s = s + b
        return s

    return _rowcall("ar_sum", fn, [gathered[j] for j in range(N_DEV)], [], [(1024, F32)],
                    tile=_pick_rows(rows))[0]


def _pad_rows(flat, mult):
    n = flat.shape[-1]
    per = mult * 1024
    padded = -(-n // per) * per
    pad = [(0, 0)] * (flat.ndim - 1) + [(0, padded - n)]
    return jnp.pad(flat, pad).reshape(flat.shape[:-1] + (padded // 1024, 1024))


def _big_local_shape(name):
    axis, shape = BIG[name]
    s = list(shape)
    s[axis] //= N_DEV
    return tuple(s)


def _pack_big_full(full):
    parts = []
    for name, (axis, shape) in BIG.items():
        a = full[name]
        if axis == 1:
            a = a.reshape(shape[0], N_DEV, shape[1] // N_DEV, shape[2]).transpose(1, 0, 2, 3)
        else:
            a = a.reshape(shape[0], shape[1], N_DEV, shape[2] // N_DEV).transpose(2, 0, 1, 3)
        parts.append(a.reshape(N_DEV, -1))
    return _pad_rows(jnp.concatenate(parts, axis=1), 512)


def _unpack_big_full(gathered):
    flat = gathered.reshape(N_DEV, -1)
    out, off = {}, 0
    for name, (axis, shape) in BIG.items():
        loc = _big_local_shape(name)
        n = math.prod(loc)
        a = flat[:, off:off + n].reshape((N_DEV,) + loc)
        off += n
        if axis == 1:
            a = a.transpose(1, 0, 2, 3)
        else:
            a = a.transpose(1, 2, 0, 3)
        out[name] = a.reshape(shape)
    return out


def _pack_big_local(local):
    return _pad_rows(jnp.concatenate([local[name].reshape(-1) for name in BIG]), 512)


def _unpack_big_local(buf):
    flat = buf.reshape(-1)
    out, off = {}, 0
    for name in BIG:
        loc = _big_local_shape(name)
        n = math.prod(loc)
        out[name] = flat[off:off + n].reshape(loc)
        off += n
    return out


def _pack_list(arrays, mult):
    return _pad_rows(jnp.concatenate([a.reshape(-1) for a in arrays]), mult)


def _unpack_list(buf, shapes):
    flat = buf.reshape(-1)
    out, off = [], 0
    for s in shapes:
        n = math.prod(s)
        out.append(flat[off:off + n].reshape(s))
        off += n
    return out


def _adamw(w, g, m, v):
    c1 = 1.0 / (1.0 - ADAM_B1 ** ADAM_STEP)
    c2 = 1.0 / (1.0 - ADAM_B2 ** ADAM_STEP)

    def fn(w, g, m, v):
        m_new = ADAM_B1 * m + (1.0 - ADAM_B1) * g
        v_new = ADAM_B2 * v + (1.0 - ADAM_B2) * (g * g)
        delta = -ADAM_LR * ((m_new * c1) / (jnp.sqrt(v_new * c2) + ADAM_EPS) + ADAM_WD * w)
        return delta, m_new, v_new

    return _rowcall("adamw", fn, [w, g, m, v], [], [(1024, F32)] * 3, tile=_pick_rows(w.shape[0]))


def _to_heads(a, nb, seq):
    return a.reshape(nb, seq, -1, HEAD_DIM).transpose(0, 2, 1, 3)


def _from_heads(a):
    nb, nh, seq, e = a.shape
    return a.transpose(0, 2, 1, 3).reshape(nb * seq, nh * e)


def _regroup_w_in(w):
    pad = jnp.zeros((w.shape[0], 116), w.dtype)
    return jnp.concatenate([w[:, 768:1280], w[:, 1286:1798], w[:, 1280:1286], w[:, 2950:2956], pad,
                            w[:, 0:768], w[:, 1798:2950]], axis=1)


def _ungroup_w_in(wp):
    return jnp.concatenate([wp[:, 1152:1920], wp[:, 0:512], wp[:, 1024:1030], wp[:, 512:1024],
                            wp[:, 1920:3072], wp[:, 1030:1036]], axis=1)


def _block_diag(w):
    out = jnp.zeros((LRU_WIDTH, LRU_WIDTH), w.dtype)
    for g in range(4):
        out = lax.dynamic_update_slice(out, w[g], (64 * g, 64 * g))
    return out


def _block_diag_grad(full):
    return jnp.stack([full[64 * g:64 * (g + 1), 64 * g:64 * (g + 1)] for g in range(4)])


def _row(v):
    return v.reshape(1, -1).astype(F32)


def _lane128(*pieces):
    flat = jnp.concatenate([p.reshape(-1).astype(F32) for p in pieces])
    return jnp.pad(flat, (0, 128 - flat.shape[0])).reshape(1, 128)


def _layer_consts(w):
    c = {}
    cw, cb = w['ssd_conv_w'], w['ssd_conv_b']
    c['cw_x'], c['cw_b'], c['cw_c'] = cw[:, 0:384], cw[:, 384:640], cw[:, 640:896]
    c['cb_x'], c['cb_b'], c['cb_c'] = _row(cb[0:384]), _row(cb[384:640]), _row(cb[640:896])
    c['bias128'] = _lane128(w['ssd_dt_bias'], w['fox_b_f'])
    c['alog128'] = _lane128(w['ssd_a_log'])
    c['d384'] = _row(jnp.repeat(w['ssd_d'], HEAD_DIM))
    c['lcw'], c['lcb'] = w['lru_conv_w'], _row(w['lru_conv_b'])
    c['wa'], c['wx'] = _block_diag(w['lru_w_a']).astype(BF16), _block_diag(w['lru_w_x']).astype(BF16)
    c['ba'], c['bx'], c['lam'] = _row(w['lru_b_a']), _row(w['lru_b_x']), _row(w['lru_lambda'])
    return c


def _layer_fwd(h0, p_i, w, c):
    nb, seq = c['nb'], c['seq']

    u1 = _rowcall("norm1", lambda h, g: _rms(h, g), [h0], [_row(w['norm1_g'])], [(D_MODEL, BF16)])[0]
    proj = _matmul(u1, w['w_in_p'], 'nn', "proj")

    xs_c = _convsilu_fwd("conv_x", (proj, 384, 4), seq, c['cw_x'], c['cb_x'], F32)
    b_c = _convsilu_fwd("conv_b", (proj, 256, 0), seq, c['cw_b'], c['cb_b'], BF16)
    c_c = _convsilu_fwd("conv_c", (proj, 256, 1), seq, c['cw_c'], c['cb_c'], BF16)
    dt_arr, cum, prev, end = _small_fwd((proj, 128, 8), seq, c['bias128'], c['alog128'])
    x_h = xs_c.reshape(nb, seq, SSD_WIDTH)
    cum3 = cum.reshape(nb, seq, 128)
    a_row = cum3[:, :, 0:8].transpose(0, 2, 1)
    ssd_in = (x_h, b_c.reshape(nb, seq, 256), c_c.reshape(nb, seq, 256), dt_arr.reshape(nb, seq, 128), cum3,
              prev.reshape(nb, seq, 128), end.reshape(nb, seq, 128), a_row)
    y_h, states = _ssd_fwd(*ssd_in)
    y_core = y_h.reshape(nb * seq, SSD_WIDTH)

    hseq, xl = _lru_fwd(proj, seq, c['lcw'], c['lcb'], c['wa'], c['ba'], c['wx'], c['bx'], c['lam'])

    q = _to_heads(proj[:, 1920:2304].astype(BF16), nb, seq)
    k = _to_heads(proj[:, 2304:2688].astype(BF16), nb, seq)
    v = _to_heads(proj[:, 2688:3072].astype(BF16), nb, seq)
    cf = cum3[:, :, SSD_HEADS:2 * SSD_HEADS].transpose(0, 2, 1)
    cf_col, cf_row = cf[..., None], cf[:, :, None, :]
    o_h, lse = _fox_fwd(q, k, v, cf_col, cf_row)
    y_fox = _from_heads(o_h)

    def post(yc, xs, z, hs, lg, yf, d, g1, g2, g3):
        y1 = _rms((yc + xs * d) * _silu(z), g1)
        y2 = _rms(hs * _gelu(lg), g2)
        y3 = _rms(yf, g3)
        return jnp.concatenate([y1, y2, y3], axis=-1)

    post_consts = [c['d384'], _row(w['ssd_norm_g']), _row(w['lru_norm_g']), _row(w['fox_norm_g'])]
    ycat = _rowcall("mix_post", post, [y_core, xs_c, (proj, 384, 3), hseq, (proj, 256, 3), y_fox], post_consts,
                    [(D_MODEL, BF16)])[0]
    mix = _matmul(ycat, w['w_out'], 'nn', "mix_out")

    def res_norm(h, d, g):
        hn = h + d
        return hn, _rms(hn, g)

    h1, u2 = _rowcall("res_norm2", res_norm, [h0, mix], [_row(w['norm2_g'])], [(D_MODEL, F32), (D_MODEL, BF16)])
    gu = _matmul(u2, w['w_gu'], 'nn', "ffn_in")
    act = _rowcall("swiglu", lambda gt, up: _silu(gt) * up, [(gu, D_FF, 0), (gu, D_FF, 1)], [], [(D_FF, BF16)])[0]
    ff = _matmul(act, w['w_down'], 'nn', "ffn_out")
    h2, u3 = _rowcall("res_norm3", res_norm, [h1, ff], [_row(w['norm3_g'])], [(D_MODEL, F32), (D_MODEL, BF16)])
    pg = _matmul(u3, w['w_ple_gate'], 'nn', "ple_gate")
    pp = _matmul(p_i, w['w_ple_proj'], 'nn', "ple_proj")
    h3 = _rowcall("ple", lambda h, a, b, bias: h + _sigmoid(a + bias) * b, [h2, pg, pp], [_row(w['b_ple_gate'])],
                  [(D_MODEL, F32)])[0]
    saved = dict(h0=h0, u1=u1, proj=proj, xs_c=xs_c, dt_arr=dt_arr, ssd_in=ssd_in, states=states,
                 y_core=y_core, hseq=hseq, xl=xl, q=q, k=k, v=v, cf_col=cf_col, cf_row=cf_row, o_h=o_h, lse=lse,
                 y_fox=y_fox, post_consts=post_consts, ycat=ycat, h1=h1, u2=u2, gu=gu, act=act, h2=h2, u3=u3,
                 pg=pg, pp=pp, p_i=p_i)
    return h3, saved


def _layer_bwd(dh3, s, w, c):
    nb, seq = c['nb'], c['seq']
    g = {}

    def ple_bwd(dh, a, b, bias):
        gate = _sigmoid(a + bias)
        dpg = dh * b * gate * (1.0 - gate)
        return dh * gate, dpg, jnp.sum(dpg, axis=0, keepdims=True)

    dpp, dpg, g['b_ple_gate'] = _rowcall("ple_bwd", ple_bwd, [dh3, s['pg'], s['pp']], [_row(w['b_ple_gate'])],
                                         [(D_MODEL, BF16), (D_MODEL, BF16)], [((1, D_MODEL), F32)])
    g['w_ple_proj'] = _matmul(s['p_i'], dpp, 'tn', "d_w_ple_proj")
    g['w_ple_gate'] = _matmul(s['u3'], dpg, 'tn', "d_w_ple_gate")
    du3 = _matmul(dpg, w['w_ple_gate'], 'nt', "d_u3")

    def norm_bwd(h, du, dh, gain):
        dx, dg = _rms_bwd(h, gain, du)
        dhn = dh + dx
        return dhn, dhn, dg

    dh2, dh2_b, g['norm3_g'] = _rowcall("norm3_bwd", norm_bwd, [s['h2'], du3, dh3], [_row(w['norm3_g'])],
                                        [(D_MODEL, F32), (D_MODEL, BF16)], [((1, D_MODEL), F32)])
    g['w_down'] = _matmul(s['act'], dh2_b, 'tn', "d_w_down")
    dact = _matmul(dh2_b, w['w_down'], 'nt', "d_act")

    def swiglu_bwd(gt, up, da):
        return jnp.concatenate([da * up * _dsilu(gt), da * _silu(gt)], axis=-1)

    dgu = _rowcall("swiglu_bwd", swiglu_bwd, [(s['gu'], D_FF, 0), (s['gu'], D_FF, 1), dact], [],
                   [(2 * D_FF, BF16)])[0]
    g['w_gu'] = _matmul(s['u2'], dgu, 'tn', "d_w_gu")
    du2 = _matmul(dgu, w['w_gu'], 'nt', "d_u2")
    dh1, dh1_b, g['norm2_g'] = _rowcall("norm2_bwd", norm_bwd, [s['h1'], du2, dh2], [_row(w['norm2_g'])],
                                        [(D_MODEL, F32), (D_MODEL, BF16)], [((1, D_MODEL), F32)])
    g['w_out'] = _matmul(s['ycat'], dh1_b, 'tn', "d_w_out")
    dycat = _matmul(dh1_b, w['w_out'], 'nt', "d_ycat")

    def post_bwd(dy, yc, xs, z, hs, lg, yf, d, g1, g2, g3):
        sz = _silu(z)
        ytot = yc + xs * d
        dpre1, dg1 = _rms_bwd(ytot * sz, g1, dy[:, 0:384])
        dytot = dpre1 * sz
        dz = dpre1 * ytot * _dsilu(z)
        dd = jnp.sum(dytot * xs, axis=0, keepdims=True)
        gl = _gelu(lg)
        dpre2, dg2 = _rms_bwd(hs * gl, g2, dy[:, 384:640])
        dyf, dg3 = _rms_bwd(yf, g3, dy[:, 640:1024])
        return dytot, dytot * d, dz, dpre2 * gl, dpre2 * hs * _dgelu(lg), dyf, dd, dg1, dg2, dg3

    (dy_core, dxs_skip, dz, dhseq, dlg, dy_fox, dd384, g['ssd_norm_g'], g['lru_norm_g'], g['fox_norm_g']) = _rowcall(
        "mix_post_bwd", post_bwd,
        [dycat, s['y_core'], s['xs_c'], (s['proj'], 384, 3), s['hseq'], (s['proj'], 256, 3), s['y_fox']],
        s['post_consts'],
        [(384, F32), (384, F32), (384, BF16), (256, F32), (256, BF16), (384, F32)],
        [((1, 384), F32), ((1, 384), F32), ((1, 256), F32), ((1, 384), F32)])
    g['ssd_d'] = dd384.reshape(SSD_HEADS, HEAD_DIM).sum(axis=1)

    do_h = _to_heads(dy_fox, nb, seq)
    dq_h, dk_h, dv_h, dcf_row, dcf_col = _fox_bwd(s['q'], s['k'], s['v'], s['o_h'], s['lse'], do_h,
                                                  s['cf_col'], s['cf_row'])
    dq, dk, dv = (_from_heads(t).astype(BF16) for t in (dq_h, dk_h, dv_h))

    dx_h, db_c, dc_c, da_arr, dend_arr, ddt_arr = _ssd_bwd(*s['ssd_in'], s['states'],
                                                           dy_core.reshape(nb, seq, SSD_WIDTH))
    dxs_c = dx_h.reshape(nb * seq, SSD_WIDTH) + dxs_skip
    dcf = (dcf_row.reshape(nb, FOX_HEADS, seq) + dcf_col.reshape(nb, FOX_HEADS, seq)).transpose(0, 2, 1)
    dcum = jnp.concatenate([da_arr[:, :, 0:SSD_HEADS], dcf,
                            jnp.zeros((nb, seq, 128 - 2 * SSD_HEADS), F32)], axis=-1).reshape(nb * seq, 128)
    proj = s['proj']
    dxs_raw, dcw_x, dcb_x = _convsilu_bwd("conv_x_bwd", (proj, 384, 4), seq, c['cw_x'], c['cb_x'], dxs_c)
    db_raw, dcw_b, dcb_b = _convsilu_bwd("conv_b_bwd", (proj, 256, 0), seq, c['cw_b'], c['cb_b'],
                                         db_c.reshape(nb * seq, 256))
    dc_raw, dcw_c, dcb_c = _convsilu_bwd("conv_c_bwd", (proj, 256, 1), seq, c['cw_c'], c['cb_c'],
                                         dc_c.reshape(nb * seq, 256))
    dsmall, dbias128, dalog128 = _small_bwd((proj, 128, 8), seq, dcum, dend_arr.reshape(nb * seq, 128),
                                            ddt_arr.reshape(nb * seq, 128), s['dt_arr'], c['bias128'], c['alog128'])
    g['ssd_conv_w'] = jnp.concatenate([dcw_x, dcw_b, dcw_c], axis=1)
    g['ssd_conv_b'] = jnp.concatenate([dcb_x, dcb_b, dcb_c], axis=1).reshape(-1)
    g['ssd_dt_bias'] = dbias128[0, 0:SSD_HEADS]
    g['fox_b_f'] = dbias128[0, SSD_HEADS:2 * SSD_HEADS]
    g['ssd_a_log'] = dalog128[0, 0:SSD_HEADS]

    (dlru_raw, g['lru_conv_w'], dlcb, dwa, dba, dwx, dbx, dlam) = _lru_bwd(
        s['proj'], seq, s['xl'], s['hseq'], dhseq, c['lcw'], c['lcb'], c['wa'], c['ba'], c['wx'], c['bx'], c['lam'])
    g['lru_conv_b'], g['lru_b_a'], g['lru_b_x'], g['lru_lambda'] = (t.reshape(-1) for t in (dlcb, dba, dbx, dlam))
    g['lru_w_a'], g['lru_w_x'] = _block_diag_grad(dwa), _block_diag_grad(dwx)

    dproj = jnp.concatenate([db_raw, dc_raw, dlru_raw, dlg, dsmall, dz, dxs_raw, dq, dk, dv], axis=1)
    g['w_in_p'] = _matmul(s['u1'], dproj, 'tn', "d_w_in")
    du1 = _matmul(dproj, w['w_in_p'], 'nt', "d_u1")

    def norm1_bwd(h, du, dh, gain):
        dx, dg = _rms_bwd(h, gain, du)
        return dh + dx, dg

    dh0, g['norm1_g'] = _rowcall("norm1_bwd", norm1_bwd, [s['h0'], du1, dh1], [_row(w['norm1_g'])],
                                 [(D_MODEL, F32)], [((1, D_MODEL), F32)])
    for name in ('b_ple_gate', 'norm3_g', 'norm2_g', 'norm1_g', 'ssd_norm_g', 'lru_norm_g', 'fox_norm_g'):
        g[name] = g[name].reshape(-1)
    g['w_in'] = _ungroup_w_in(g.pop('w_in_p'))
    wgu = g.pop('w_gu')
    g['w_gate'], g['w_up'] = wgu[:, 0:D_FF], wgu[:, D_FF:2 * D_FF]
    return dh0, g


def _local_step(x, p, target, big, small):
    nb, seq, _ = x.shape
    tokens = nb * seq
    h = x.reshape(tokens, D_MODEL)
    layers, saves = [], []
    for i in range(DEPTH):
        w = {name: small[name][i] for name in small if name != 'final_norm_g'}
        w['w_in_p'] = _regroup_w_in(big['w_in'][i])
        w['w_out'] = big['w_out'][i]
        w['w_gu'] = jnp.concatenate([big['w_gate'][i], big['w_up'][i]], axis=1)
        w['w_down'] = big['w_down'][i]
        w['w_ple_gate'] = big['w_ple_gate'][i]
        w['w_ple_proj'] = big['w_ple_proj'][i]
        c = _layer_consts(w)
        c['nb'], c['seq'] = nb, seq
        h, s = _layer_fwd(h, p[i].reshape(tokens, PLE_DIM).astype(BF16), w, c)
        layers.append((w, c))
        saves.append(s)

    def head(hf, tgt, gain):
        r = lax.rsqrt(jnp.mean(hf * hf, axis=-1, keepdims=True) + EPS)
        xhat = hf * r
        err = xhat * gain - tgt
        loss = 0.5 * jnp.sum(jnp.mean(err * err, axis=-1, keepdims=True), axis=0, keepdims=True)
        dy = err * (1.0 / D_MODEL)
        dg = jnp.sum(dy * xhat, axis=0, keepdims=True)
        dxhat = dy * gain
        dh = r * (dxhat - xhat * jnp.mean(dxhat * xhat, axis=-1, keepdims=True))
        return dh, jnp.broadcast_to(loss, (1, 128)), dg

    dh, loss128, dgf = _rowcall("loss_head", head, [h, target.reshape(tokens, D_MODEL)],
                                [_row(small['final_norm_g'])], [(D_MODEL, F32)],
                                [((1, 128), F32), ((1, D_MODEL), F32)])
    grads = {'final_norm_g': dgf.reshape(-1)}
    per_layer = [None] * DEPTH
    for i in range(DEPTH - 1, -1, -1):
        w, c = layers[i]
        dh, per_layer[i] = _layer_bwd(dh, saves[i], w, c)
    for name in per_layer[0]:
        grads[name] = jnp.stack([per_layer[i][name] for i in range(DEPTH)])
    return loss128[0, 0], dh.reshape(nb, seq, D_MODEL), grads


def kernel(x, p, norm1_g, w_in, ssd_conv_w, ssd_conv_b, ssd_dt_bias, ssd_a_log, ssd_d, ssd_norm_g, lru_conv_w, lru_conv_b, lru_w_a, lru_b_a, lru_w_x, lru_b_x, lru_lambda, lru_norm_g, fox_b_f, fox_norm_g, w_out, norm2_g, w_gate, w_up, w_down, norm3_g, w_ple_gate, b_ple_gate, w_ple_proj, final_norm_g, loss_target, m_norm1_g, m_w_in, m_ssd_conv_w, m_ssd_conv_b, m_ssd_dt_bias, m_ssd_a_log, m_ssd_d, m_ssd_norm_g, m_lru_conv_w, m_lru_conv_b, m_lru_w_a, m_lru_b_a, m_lru_w_x, m_lru_b_x, m_lru_lambda, m_lru_norm_g, m_fox_b_f, m_fox_norm_g, m_w_out, m_norm2_g, m_w_gate, m_w_up, m_w_down, m_norm3_g, m_w_ple_gate, m_b_ple_gate, m_w_ple_proj, m_final_norm_g, v_norm1_g, v_w_in, v_ssd_conv_w, v_ssd_conv_b, v_ssd_dt_bias, v_ssd_a_log, v_ssd_d, v_ssd_norm_g, v_lru_conv_w, v_lru_conv_b, v_lru_w_a, v_lru_b_a, v_lru_w_x, v_lru_b_x, v_lru_lambda, v_lru_norm_g, v_fox_b_f, v_fox_norm_g, v_w_out, v_norm2_g, v_w_gate, v_w_up, v_w_down, v_norm3_g, v_w_ple_gate, v_b_ple_gate, v_w_ple_proj, v_final_norm_g):
    args = dict(locals())
    w_loc = {n: args[n] for n in WEIGHTS}
    m_loc = {n: args['m_' + n] for n in WEIGHTS}
    v_loc = {n: args['v_' + n] for n in WEIGHTS}
    dev = 4 * lax.axis_index("x") + 2 * lax.axis_index("y") + lax.axis_index("c")

    big = _unpack_big_full(_all_gather(_pack_big_local({n: w_loc[n].astype(BF16) for n in BIG}), "gather_w"))
    conv_names = list(CONV_SHARDED)
    conv_loc_shapes = [w_loc[n].shape for n in conv_names]
    conv_all = _all_gather(_pack_list([w_loc[n] for n in conv_names], 8), "gather_conv")
    small = {n: w_loc[n] for n in WEIGHTS if n not in BIG and n not in CONV_SHARDED}
    per_dev = [_unpack_list(conv_all[j], conv_loc_shapes) for j in range(N_DEV)]
    for idx, n in enumerate(conv_names):
        small[n] = jnp.concatenate([per_dev[j][idx] for j in range(N_DEV)], axis=2)

    loss_part, dx, grads = _local_step(x, p, loss_target, big, small)
    loss = lax.psum(loss_part, ("x", "y", "c"))

    g_big = _unpack_big_local(_reduce_scatter(_pack_big_full({n: grads[n] for n in BIG})))
    small_names = [n for n in WEIGHTS if n not in BIG]
    small_shapes = [grads[n].shape for n in small_names]
    g_small = dict(zip(small_names, _unpack_list(
        _all_reduce_small(_pack_list([grads[n] for n in small_names], 8)), small_shapes)))
    g_loc = {}
    for n in WEIGHTS:
        if n in BIG:
            g_loc[n] = g_big[n]
        elif n in CONV_SHARDED:
            width = CONV_SHARDED[n][2] // N_DEV
            g_loc[n] = lax.dynamic_slice_in_dim(g_small[n], dev * width, width, axis=2)
        else:
            g_loc[n] = g_small[n]

    shapes = [w_loc[n].shape for n in WEIGHTS]
    packed = [_pack_list([d[n] for n in WEIGHTS], 256) for d in (w_loc, g_loc, m_loc, v_loc)]
    delta, m_new, v_new = (_unpack_list(t, shapes) for t in _adamw(*packed))
    return (loss, dx, *[g_loc[n] for n in WEIGHTS], *delta, *m_new, *v_new)
```

```python
import functools
import math

import jax
import jax.numpy as jnp
from jax import lax
from jax.experimental import pallas as pl
from jax.experimental.pallas import tpu as pltpu

F32 = jnp.float32
BF16 = jnp.bfloat16

N_DEV = 8
D_MODEL = 1024
DEPTH = 2
HEAD_DIM = 64
SSD_WIDTH = 384
LRU_WIDTH = 256
FOX_WIDTH = 384
SSD_HEADS = 6
SSD_STATE = 128
CHUNK = 128
FOX_HEADS = 6
D_FF = 2816
FF_SHARD = D_FF // N_DEV
FF_SHARD_P = 384
D_FF_P = N_DEV * FF_SHARD_P
PLE_DIM = 256
IN_COLS = 2956
PROJ_COLS = 3072
LRU_C = 8.0
EPS = 1e-6
NEG = -1e30

ADAM_LR = 0.001
ADAM_B1 = 0.9
ADAM_B2 = 0.999
ADAM_EPS = 1e-08
ADAM_WD = 0.01
ADAM_STEP = 10

VMEM_LIMIT = 56 * 1024 * 1024

WEIGHTS = ['norm1_g', 'w_in', 'ssd_conv_w', 'ssd_conv_b', 'ssd_dt_bias', 'ssd_a_log', 'ssd_d', 'ssd_norm_g',
           'lru_conv_w', 'lru_conv_b', 'lru_w_a', 'lru_b_a', 'lru_w_x', 'lru_b_x', 'lru_lambda', 'lru_norm_g',
           'fox_b_f', 'fox_norm_g', 'w_out', 'norm2_g', 'w_gate', 'w_up', 'w_down', 'norm3_g', 'w_ple_gate',
           'b_ple_gate', 'w_ple_proj', 'final_norm_g']
BIG = {'w_in': (1, (DEPTH, D_MODEL, IN_COLS)), 'w_out': (1, (DEPTH, D_MODEL, D_MODEL)),
       'w_gate': (2, (DEPTH, D_MODEL, D_FF)), 'w_up': (2, (DEPTH, D_MODEL, D_FF)),
       'w_down': (1, (DEPTH, D_FF, D_MODEL)), 'w_ple_gate': (1, (DEPTH, D_MODEL, D_MODEL)),
       'w_ple_proj': (2, (DEPTH, PLE_DIM, D_MODEL))}
CONV_SHARDED = {'ssd_conv_w': (DEPTH, 4, 896), 'lru_conv_w': (DEPTH, 4, 256)}


def _dot(a, b):
    return jnp.dot(a, b, preferred_element_type=F32)


def _dot_nt(a, b):
    return lax.dot_general(a, b, (((1,), (1,)), ((), ())), preferred_element_type=F32)


def _dot_tn(a, b):
    return lax.dot_general(a, b, (((0,), (0,)), ((), ())), preferred_element_type=F32)


def _params(sem):
    return pltpu.CompilerParams(dimension_semantics=sem, vmem_limit_bytes=VMEM_LIMIT)


def _pick_tile(n, cap):
    if n <= cap:
        return n
    best = 128
    for t in range(128, cap + 1, 128):
        if n % t == 0:
            best = t
    assert n % best == 0, (n, cap)
    return best


def _matmul(a, b, mode, name, out_dtype=F32):
    if mode == 'tn':
        k_dim, m_dim = a.shape
        n_dim = b.shape[1]
    else:
        m_dim, k_dim = a.shape
        n_dim = b.shape[1] if mode == 'nn' else b.shape[0]
    tm = _pick_tile(m_dim, 512 if mode != 'tn' else 1024)
    tn = _pick_tile(n_dim, 1536 if mode != 'tn' else 1024)
    tk = _pick_tile(k_dim, 3072 if mode != 'tn' else 512)
    nk = k_dim // tk
    grid = (n_dim // tn, m_dim // tm, nk)

    def body(a_ref, b_ref, o_ref, acc_ref):
        kk = pl.program_id(2)

        @pl.when(kk == 0)
        def _():
            acc_ref[...] = jnp.zeros_like(acc_ref)

        if mode == 'nn':
            acc_ref[...] += _dot(a_ref[...], b_ref[...])
        elif mode == 'nt':
            acc_ref[...] += _dot_nt(a_ref[...], b_ref[...])
        else:
            acc_ref[...] += _dot_tn(a_ref[...], b_ref[...])

        @pl.when(kk == nk - 1)
        def _():
            o_ref[...] = acc_ref[...].astype(o_ref.dtype)

    if mode == 'nn':
        a_spec = pl.BlockSpec((tm, tk), lambda j, i, k: (i, k))
        b_spec = pl.BlockSpec((tk, tn), lambda j, i, k: (k, j))
    elif mode == 'nt':
        a_spec = pl.BlockSpec((tm, tk), lambda j, i, k: (i, k))
        b_spec = pl.BlockSpec((tn, tk), lambda j, i, k: (j, k))
    else:
        a_spec = pl.BlockSpec((tk, tm), lambda j, i, k: (k, i))
        b_spec = pl.BlockSpec((tk, tn), lambda j, i, k: (k, j))
    return pl.pallas_call(
        body, name=name, grid=grid,
        out_shape=jax.ShapeDtypeStruct((m_dim, n_dim), out_dtype),
        in_specs=[a_spec, b_spec],
        out_specs=pl.BlockSpec((tm, tn), lambda j, i, k: (i, j)),
        scratch_shapes=[pltpu.VMEM((tm, tn), F32)],
        compiler_params=_params(("parallel", "parallel", "arbitrary")),
    )(a, b)


def _rowcall(name, fn, tiled, consts, outs, accs=(), tile=512, scratch=()):
    specs, arrays = [], []
    for t in tiled:
        if isinstance(t, tuple):
            arr, width, blk = t
            specs.append(pl.BlockSpec((tile, width), functools.partial(lambda i, blk: (i, blk), blk=blk)))
        else:
            arr = t
            specs.append(pl.BlockSpec((tile, arr.shape[1]), lambda i: (i, 0)))
        arrays.append(arr)
    rows = arrays[0].shape[0]
    assert rows % tile == 0, (name, rows, tile)
    for c in consts:
        specs.append(pl.BlockSpec(c.shape, lambda i: (0, 0)))
        arrays.append(c)
    n_in, n_out, n_acc = len(arrays), len(outs), len(accs)
    out_shape = [jax.ShapeDtypeStruct((rows, c), dt) for c, dt in outs]
    out_specs = [pl.BlockSpec((tile, c), lambda i: (i, 0)) for c, _ in outs]
    out_shape += [jax.ShapeDtypeStruct(s, dt) for s, dt in accs]
    out_specs += [pl.BlockSpec(s, lambda i: (0, 0)) for s, _ in accs]

    def body(*refs):
        ins = [r[...] for r in refs[:n_in]]
        out_refs = refs[n_in:n_in + n_out]
        acc_refs = refs[n_in + n_out:n_in + n_out + n_acc]
        scr = refs[n_in + n_out + n_acc:]
        res = fn(*ins, *scr)
        if not isinstance(res, (tuple, list)):
            res = (res,)
        assert len(res) == n_out + n_acc, (name, len(res))
        for r, v in zip(out_refs, res[:n_out]):
            r[...] = v.astype(r.dtype)
        if n_acc:
            first = pl.program_id(0) == 0

            @pl.when(first)
            def _():
                for r, v in zip(acc_refs, res[n_out:]):
                    r[...] = v.astype(r.dtype)

            @pl.when(jnp.logical_not(first))
            def _():
                for r, v in zip(acc_refs, res[n_out:]):
                    r[...] += v.astype(r.dtype)

    res = pl.pallas_call(
        body, name=name, grid=(rows // tile,),
        out_shape=out_shape, in_specs=specs, out_specs=out_specs,
        scratch_shapes=list(scratch),
        compiler_params=_params(("arbitrary",)),
    )(*arrays)
    return res


def _sigmoid(x):
    return 1.0 / (1.0 + jnp.exp(-x))


def _softplus(x):
    return jnp.maximum(x, 0.0) + jnp.log(1.0 + jnp.exp(-jnp.abs(x)))


def _silu(x):
    return x * _sigmoid(x)


def _dsilu(x):
    s = _sigmoid(x)
    return s * (1.0 + x * (1.0 - s))


_GELU_C = math.sqrt(2.0 / math.pi)


def _gelu(x):
    return 0.5 * x * (1.0 + jnp.tanh(_GELU_C * (x + 0.044715 * x * x * x)))


def _dgelu(x):
    t = jnp.tanh(_GELU_C * (x + 0.044715 * x * x * x))
    return 0.5 * (1.0 + t) + 0.5 * x * (1.0 - t * t) * _GELU_C * (1.0 + 3.0 * 0.044715 * x * x)


def _neg_expm1(x):
    series = -x * (1.0 + x * (0.5 + x * (1.0 / 6.0 + x * (1.0 / 24.0 + x * (1.0 / 120.0)))))
    return jnp.where(x > -0.03, series, 1.0 - jnp.exp(x))


def _rms(x, g):
    r = lax.rsqrt(jnp.mean(x * x, axis=-1, keepdims=True) + EPS)
    return x * r * g


def _rms_bwd(x, g, dy):
    r = lax.rsqrt(jnp.mean(x * x, axis=-1, keepdims=True) + EPS)
    xhat = x * r
    dg = jnp.sum(dy * xhat, axis=0, keepdims=True)
    dxhat = dy * g
    dx = r * (dxhat - xhat * jnp.mean(dxhat * xhat, axis=-1, keepdims=True))
    return dx, dg


def _row_iota(shape):
    return lax.broadcasted_iota(jnp.int32, shape, 0)


def _shift_down(x, j):
    if j == 0:
        return x
    return jnp.where(_row_iota(x.shape) >= j, pltpu.roll(x, j, 0), 0.0)


def _shift_up(x, j):
    if j == 0:
        return x
    n = x.shape[0]
    return jnp.where(_row_iota(x.shape) < n - j, pltpu.roll(x, n - j, 0), 0.0)


def _conv(x, w, b):
    y = b + w[3:4, :] * x
    for k in range(3):
        y = y + w[k:k + 1, :] * _shift_down(x, 3 - k)
    return y


def _conv_bwd(x, w, dy):
    dx = w[3:4, :] * dy
    dws = []
    for k in range(3):
        dx = dx + w[k:k + 1, :] * _shift_up(dy, 3 - k)
        dws.append(jnp.sum(dy * _shift_down(x, 3 - k), axis=0, keepdims=True))
    dws.append(jnp.sum(dy * x, axis=0, keepdims=True))
    return dx, jnp.concatenate(dws, axis=0), jnp.sum(dy, axis=0, keepdims=True)


def _split3(x):
    hi = x.astype(BF16)
    r1 = x - hi.astype(F32)
    mid = r1.astype(BF16)
    lo = (r1 - mid.astype(F32)).astype(BF16)
    return hi, mid, lo


def _tri_dot(tri, x):
    hi, mid, lo = _split3(x)
    return _dot(tri, hi) + _dot(tri, mid) + _dot(tri, lo)


def _cumsum_rows(x):
    n = x.shape[0] // CHUNK
    r = lax.broadcasted_iota(jnp.int32, (CHUNK, CHUNK), 0)
    c = lax.broadcasted_iota(jnp.int32, (CHUNK, CHUNK), 1)
    tri = (r >= c).astype(BF16)
    carry = jnp.zeros((1, x.shape[1]), F32)
    cums, prevs, ends = [], [], []
    for i in range(n):
        blk = _tri_dot(tri, x[i * CHUNK:(i + 1) * CHUNK]) + carry
        prevs.append(jnp.broadcast_to(carry, blk.shape))
        carry = blk[CHUNK - 1:CHUNK, :]
        ends.append(jnp.broadcast_to(carry, blk.shape))
        cums.append(blk)
    return jnp.concatenate(cums, 0), jnp.concatenate(prevs, 0), jnp.concatenate(ends, 0)


def _rev_cumsum_rows(x):
    n = x.shape[0] // CHUNK
    r = lax.broadcasted_iota(jnp.int32, (CHUNK, CHUNK), 0)
    c = lax.broadcasted_iota(jnp.int32, (CHUNK, CHUNK), 1)
    tri = (r <= c).astype(BF16)
    carry = jnp.zeros((1, x.shape[1]), F32)
    local, whole = [None] * n, [None] * n
    for i in range(n - 1, -1, -1):
        local[i] = _tri_dot(tri, x[i * CHUNK:(i + 1) * CHUNK])
        whole[i] = local[i] + carry
        carry = whole[i][0:1, :]
    return jnp.concatenate(local, 0), jnp.concatenate(whole, 0)


def _convsilu_fwd(name, seg, seq, w, b, dtype):
    return _rowcall(name, lambda raw, w, b: _silu(_conv(raw, w, b)), [seg], [w, b], [(seg[1], dtype)], tile=seq)[0]


def _convsilu_bwd(name, seg, seq, w, b, dy):
    def fn(raw, dy, w, b):
        return _conv_bwd(raw, w, dy * _dsilu(_conv(raw, w, b)))

    width = seg[1]
    return _rowcall(name, fn, [seg, dy], [w, b], [(width, BF16)], [((4, width), F32), ((1, width), F32)], tile=seq)


def _small_fwd(seg, seq, bias128, alog128):
    def fn(small, bias, alog):
        lane = lax.broadcasted_iota(jnp.int32, small.shape, 1)
        a = -jnp.exp(alog)
        s = small + bias
        dt = _softplus(s)
        logf = -_softplus(-s)
        pre = jnp.where(lane < SSD_HEADS, a * dt, jnp.where(lane < 2 * SSD_HEADS, logf, 0.0))
        cum, prev, end = _cumsum_rows(pre)
        return dt, cum, prev, end

    return _rowcall("small_fwd", fn, [seg], [bias128, alog128], [(128, F32)] * 4, tile=seq)


def _small_bwd(seg, seq, dcum, dend, ddt, dt_arr, bias128, alog128):
    def fn(small, dcum, dend, ddt, dt_arr, bias, alog):
        lane = lax.broadcasted_iota(jnp.int32, small.shape, 1)
        a = -jnp.exp(alog)
        sig = _sigmoid(small + bias)
        local, whole = _rev_cumsum_rows(dcum)
        dadt = local + dend
        d_dt = ddt + a * dadt
        ds = jnp.where(lane < SSD_HEADS, d_dt * sig, jnp.where(lane < 2 * SSD_HEADS, whole * (1.0 - sig), 0.0))
        da = jnp.sum(jnp.where(lane < SSD_HEADS, dadt * dt_arr, 0.0), axis=0, keepdims=True)
        return ds, jnp.sum(ds, axis=0, keepdims=True), da * a

    return _rowcall("small_bwd", fn, [seg, dcum, dend, ddt, dt_arr], [bias128, alog128], [(128, BF16)],
                    [((1, 128), F32), ((1, 128), F32)], tile=seq)


HEAD_PAIRS = SSD_HEADS // 2


def _ssd_specs(nc, reverse):
    def at(c):
        return nc - 1 - c if reverse else c

    x_spec = pl.BlockSpec((1, CHUNK, SSD_WIDTH), lambda b, c: (b, at(c), 0))
    bc_spec = pl.BlockSpec((1, CHUNK, 256), lambda b, c: (b, at(c), 0))
    col_spec = pl.BlockSpec((1, CHUNK, 128), lambda b, c: (b, at(c), 0))
    row_spec = pl.BlockSpec((1, 8, CHUNK), lambda b, c: (b, 0, at(c)))
    st_spec = pl.BlockSpec((1, 1, HEAD_PAIRS, SSD_STATE, 128), lambda b, c: (b, at(c), 0, 0, 0))
    return x_spec, bc_spec, col_spec, row_spec, st_spec


def _ssd_head(h, dtb, acb, apb, aeb, arb):
    return dtb[:, h:h + 1], acb[:, h:h + 1], apb[:, h:h + 1], aeb[:, h:h + 1], arb[h:h + 1, :]


def _ssd_fwd(x, bm, cm, dt_arr, cum, prev, end, a_row):
    nb, seq, _ = x.shape
    nc = seq // CHUNK
    x_spec, bc_spec, col_spec, row_spec, st_spec = _ssd_specs(nc, False)

    def body(x_ref, b_ref, c_ref, dt_ref, ac_ref, ap_ref, ae_ref, ar_ref, y_ref, st_ref, s_scr):
        @pl.when(pl.program_id(1) == 0)
        def _():
            s_scr[...] = jnp.zeros_like(s_scr)

        causal = (lax.broadcasted_iota(jnp.int32, (CHUNK, CHUNK), 0)
                  >= lax.broadcasted_iota(jnp.int32, (CHUNK, CHUNK), 1))
        low = lax.broadcasted_iota(jnp.int32, (CHUNK, 128), 1) < HEAD_DIM
        cols = (dt_ref[0], ac_ref[0], ap_ref[0], ae_ref[0], ar_ref[0])
        bcs = [b_ref[0, :, g * 128:(g + 1) * 128] for g in range(2)]
        ccs = [c_ref[0, :, g * 128:(g + 1) * 128] for g in range(2)]
        ms = [_dot_nt(ccs[g], bcs[g]) for g in range(2)]
        for pi in range(HEAD_PAIRS):
            x2 = x_ref[0, :, pi * 128:(pi + 1) * 128]
            dt2 = jnp.where(low, cols[0][:, 2 * pi:2 * pi + 1], cols[0][:, 2 * pi + 1:2 * pi + 2])
            xdt = (x2 * dt2).astype(BF16)
            sprev = s_scr[pi]
            st_ref[0, 0, pi] = sprev
            spb = sprev.astype(BF16)
            ys, us = [], []
            for h in (2 * pi, 2 * pi + 1):
                g = h // 3
                _, ac, ap, ae, ar = _ssd_head(h, *cols)
                lm = jnp.exp(jnp.where(causal, ac - ar, NEG))
                gm = (ms[g] * lm).astype(BF16)
                ys.append(_dot(gm, xdt) + jnp.exp(ac - ap) * _dot(ccs[g], spb))
                bdec = (bcs[g].astype(F32) * jnp.exp(ae - ac)).astype(BF16)
                us.append(jnp.exp(ae[0:1, :] - ap[0:1, :]) * sprev + _dot_tn(bdec, xdt))
            y_ref[0, :, pi * 128:(pi + 1) * 128] = jnp.where(low, ys[0], ys[1])
            s_scr[pi] = jnp.where(low, us[0], us[1])

    return pl.pallas_call(
        body, name="ssd_fwd", grid=(nb, nc),
        out_shape=[jax.ShapeDtypeStruct(x.shape, F32),
                   jax.ShapeDtypeStruct((nb, nc, HEAD_PAIRS, SSD_STATE, 128), F32)],
        in_specs=[x_spec, bc_spec, bc_spec, col_spec, col_spec, col_spec, col_spec, row_spec],
        out_specs=[x_spec, st_spec],
        scratch_shapes=[pltpu.VMEM((HEAD_PAIRS, SSD_STATE, 128), F32)],
        compiler_params=_params(("parallel", "arbitrary")),
    )(x, bm, cm, dt_arr, cum, prev, end, a_row)


def _ssd_bwd(x_h, bm, cm, dt_arr, cum, prev, end, a_row, states, dy_h):
    nb, seq, _ = x_h.shape
    nc = seq // CHUNK
    x_spec, bc_spec, col_spec, row_spec, st_spec = _ssd_specs(nc, True)

    def body(x_ref, b_ref, c_ref, dt_ref, ac_ref, ap_ref, ae_ref, ar_ref, st_ref, dy_ref,
             dx_ref, db_ref, dc_ref, da_ref, dend_ref, ddt_ref, ds_scr):
        @pl.when(pl.program_id(1) == 0)
        def _():
            ds_scr[...] = jnp.zeros_like(ds_scr)

        causal = (lax.broadcasted_iota(jnp.int32, (CHUNK, CHUNK), 0)
                  >= lax.broadcasted_iota(jnp.int32, (CHUNK, CHUNK), 1))
        lane = lax.broadcasted_iota(jnp.int32, (CHUNK, 128), 1)
        low = lane < HEAD_DIM
        cols = (dt_ref[0], ac_ref[0], ap_ref[0], ae_ref[0], ar_ref[0])
        bcs = [b_ref[0, :, g * 128:(g + 1) * 128] for g in range(2)]
        ccs = [c_ref[0, :, g * 128:(g + 1) * 128] for g in range(2)]
        ms = [_dot_nt(ccs[g], bcs[g]) for g in range(2)]
        dms = [jnp.zeros((CHUNK, CHUNK), F32) for _ in range(2)]
        dc_accs = [jnp.zeros((CHUNK, SSD_STATE), F32) for _ in range(2)]
        db_accs = [jnp.zeros((CHUNK, SSD_STATE), F32) for _ in range(2)]
        da_blk = jnp.zeros((CHUNK, 128), F32)
        dend_blk = jnp.zeros((CHUNK, 128), F32)
        ddt_blk = jnp.zeros((CHUNK, 128), F32)
        for pi in range(HEAD_PAIRS):
            x2 = x_ref[0, :, pi * 128:(pi + 1) * 128]
            dy2 = dy_ref[0, :, pi * 128:(pi + 1) * 128]
            dt2 = jnp.where(low, cols[0][:, 2 * pi:2 * pi + 1], cols[0][:, 2 * pi + 1:2 * pi + 2])
            xdt_f = x2 * dt2
            xdt = xdt_f.astype(BF16)
            dyb = dy2.astype(BF16)
            dsn = ds_scr[pi]
            dsb = dsn.astype(BF16)
            sprev_f = st_ref[0, 0, pi]
            sprev = sprev_f.astype(BF16)
            dxdts, dss = [], []
            for h in (2 * pi, 2 * pi + 1):
                g = h // 3
                mine = low if h % 2 == 0 else jnp.logical_not(low)
                _, ac, ap, ae, ar = _ssd_head(h, *cols)
                bc, cc, m = bcs[g], ccs[g], ms[g]
                lm = jnp.exp(jnp.where(causal, ac - ar, NEG))
                gm = (m * lm).astype(BF16)
                dy_m = jnp.where(mine, dy2, 0.0)
                dyb_m = dy_m.astype(BF16)
                xdt_m = jnp.where(mine, xdt_f, 0.0)
                e_in = jnp.exp(ac - ap)
                f_out = jnp.exp(ae - ac)
                whole = jnp.exp(ae[0:1, :] - ap[0:1, :])
                dg = _dot_nt(dyb_m, xdt)
                dxdt_off = f_out * _dot(bc, dsb)
                dxdt = _dot_tn(gm, dyb) + dxdt_off
                dmj = dg * lm
                dms[g] = dms[g] + dmj
                dc_accs[g] = dc_accs[g] + e_in * _dot_nt(dyb_m, sprev)
                db_accs[g] = db_accs[g] + f_out * _dot_nt(xdt_m.astype(BF16), dsb)
                dss.append(whole * dsn + _dot_tn((cc.astype(F32) * e_in).astype(BF16), dyb))
                wmat = dmj * m
                r_in = jnp.sum(dy_m * (e_in * _dot(cc, sprev)), axis=1, keepdims=True)
                q_out = jnp.sum(xdt_m * dxdt_off, axis=1, keepdims=True)
                daj = (jnp.sum(wmat, axis=1, keepdims=True) - jnp.sum(wmat.T, axis=1, keepdims=True)
                       + r_in - q_out)
                cross = jnp.where(mine, dsn * sprev_f, 0.0)
                dendj = (jnp.sum(q_out, axis=0, keepdims=True)
                         + whole * jnp.sum(jnp.sum(cross, axis=1, keepdims=True), axis=0, keepdims=True))
                ddtj = jnp.sum(jnp.where(mine, dxdt * x2, 0.0), axis=1, keepdims=True)
                dxdts.append(dxdt)
                da_blk = jnp.where(lane == h, daj, da_blk)
                dend_blk = jnp.where(lane == h, dendj, dend_blk)
                ddt_blk = jnp.where(lane == h, ddtj, ddt_blk)
            dx_ref[0, :, pi * 128:(pi + 1) * 128] = jnp.where(low, dxdts[0], dxdts[1]) * dt2
            ds_scr[pi] = jnp.where(low, dss[0], dss[1])
        for g in range(2):
            dmb = dms[g].astype(BF16)
            dc_ref[0, :, g * 128:(g + 1) * 128] = dc_accs[g] + _dot(dmb, bcs[g])
            db_ref[0, :, g * 128:(g + 1) * 128] = db_accs[g] + _dot_tn(dmb, ccs[g])
        da_ref[0] = da_blk
        dend_ref[0] = dend_blk
        ddt_ref[0] = ddt_blk

    col_shape = jax.ShapeDtypeStruct((nb, seq, 128), F32)
    return pl.pallas_call(
        body, name="ssd_bwd", grid=(nb, nc),
        out_shape=[jax.ShapeDtypeStruct(x_h.shape, F32),
                   jax.ShapeDtypeStruct((nb, seq, 256), F32), jax.ShapeDtypeStruct((nb, seq, 256), F32),
                   col_shape, col_shape, col_shape],
        in_specs=[x_spec, bc_spec, bc_spec, col_spec, col_spec, col_spec, col_spec, row_spec, st_spec, x_spec],
        out_specs=[x_spec, bc_spec, bc_spec, col_spec, col_spec, col_spec],
        scratch_shapes=[pltpu.VMEM((HEAD_PAIRS, SSD_STATE, 128), F32)],
        compiler_params=_params(("parallel", "arbitrary")),
    )(x_h, bm, cm, dt_arr, cum, prev, end, a_row, states, dy_h)


def _lru_gates(xl, wa, ba, wx, bx, lam):
    xb = xl.astype(BF16)
    r = _sigmoid(_dot(xb, wa) + ba)
    i = _sigmoid(_dot(xb, wx) + bx)
    sp = _softplus(-lam)
    log_a = -LRU_C * r * sp
    a = jnp.exp(log_a)
    mult = jnp.sqrt(_neg_expm1(2.0 * log_a))
    return r, i, sp, log_a, a, mult


def _scan_chunks(a_ref, u_ref, h_ref, seq, reverse):
    nc = seq // CHUNK
    width = a_ref.shape[1]
    row = lax.broadcasted_iota(jnp.int32, (CHUNK, width), 0)

    def chunk(ci, carry):
        c = nc - 1 - ci if reverse else ci
        rows = pl.ds(pl.multiple_of(c * CHUNK, CHUNK), CHUNK)
        av, bv = a_ref[rows, :], u_ref[rows, :]
        d = 1
        while d < CHUNK:
            if reverse:
                keep = row < CHUNK - d
                a_sh = jnp.where(keep, pltpu.roll(av, CHUNK - d, 0), 1.0)
                b_sh = jnp.where(keep, pltpu.roll(bv, CHUNK - d, 0), 0.0)
            else:
                keep = row >= d
                a_sh = jnp.where(keep, pltpu.roll(av, d, 0), 1.0)
                b_sh = jnp.where(keep, pltpu.roll(bv, d, 0), 0.0)
            bv = av * b_sh + bv
            av = av * a_sh
            d *= 2
        hv = bv + av * carry
        h_ref[rows, :] = hv
        return hv[0:1, :] if reverse else hv[CHUNK - 1:CHUNK, :]

    lax.fori_loop(0, nc, chunk, jnp.zeros((1, width), F32))


def _lru_fwd(proj, seq, cw, cb, wa, ba, wx, bx, lam):
    def fn(raw, cw, cb, wa, ba, wx, bx, lam, a_scr, u_scr, h_scr):
        xl = _conv(raw, cw, cb)
        r, i, sp, log_a, a, mult = _lru_gates(xl, wa, ba, wx, bx, lam)
        a_scr[...] = a
        u_scr[...] = mult * (i * xl)
        _scan_chunks(a_scr, u_scr, h_scr, seq, reverse=False)
        return h_scr[...], xl

    return _rowcall("lru_fwd", fn, [(proj, 256, 2)], [cw, cb, wa, ba, wx, bx, lam],
                    [(256, F32), (256, F32)], tile=seq,
                    scratch=[pltpu.VMEM((seq, 256), F32)] * 3)


def _lru_bwd(proj, seq, xl_all, h_all, dh_all, cw, cb, wa, ba, wx, bx, lam):
    def fn(raw, xl, hseq, dh, cw, cb, wa, ba, wx, bx, lam, a_scr, u_scr, h_scr):
        r, i, sp, log_a, a, mult = _lru_gates(xl, wa, ba, wx, bx, lam)
        a_scr[...] = _shift_up(a, 1)
        u_scr[...] = dh
        _scan_chunks(a_scr, u_scr, h_scr, seq, reverse=True)
        dht = h_scr[...]
        da = dht * _shift_down(hseq, 1)
        gated = i * xl
        dgated = dht * mult
        dmult = dht * gated
        dlog_a = da * a - dmult * (a * a) / mult
        dr = dlog_a * (-LRU_C * sp)
        dsp = jnp.sum(dlog_a * (-LRU_C * r), axis=0, keepdims=True)
        dlam = -dsp * _sigmoid(-lam)
        dpa = dr * r * (1.0 - r)
        dpx = (dgated * xl) * i * (1.0 - i)
        dpa_b, dpx_b = dpa.astype(BF16), dpx.astype(BF16)
        dxl = dgated * i + _dot_nt(dpa_b, wa) + _dot_nt(dpx_b, wx)
        xb = xl.astype(BF16)
        dwa = _dot_tn(xb, dpa_b)
        dwx = _dot_tn(xb, dpx_b)
        draw, dcw, dcb = _conv_bwd(raw, cw, dxl)
        return (draw, dcw, dcb, dwa, jnp.sum(dpa, axis=0, keepdims=True), dwx,
                jnp.sum(dpx, axis=0, keepdims=True), dlam)

    return _rowcall("lru_bwd", fn, [(proj, 256, 2), xl_all, h_all, dh_all], [cw, cb, wa, ba, wx, bx, lam],
                    [(256, BF16)],
                    [((4, 256), F32), ((1, 256), F32), ((256, 256), F32), ((1, 256), F32), ((256, 256), F32),
                     ((1, 256), F32), ((1, 256), F32)],
                    tile=seq, scratch=[pltpu.VMEM((seq, 256), F32)] * 3)


FOX_SCALE = HEAD_DIM ** -0.5
FOX_BLOCK = 512


def _fox_fwd(q, k, v, cum_col, cum_row):
    nb, nh, seq, e = q.shape
    tb = min(FOX_BLOCK, seq)
    nq = seq // tb

    def body(q_ref, k_ref, v_ref, cq_ref, ck_ref, o_ref, lse_ref):
        qi = pl.program_id(2)
        qv = q_ref[0, 0]
        cq = cq_ref[0, 0]
        rowi = lax.broadcasted_iota(jnp.int32, (tb, tb), 0)
        coli = lax.broadcasted_iota(jnp.int32, (tb, tb), 1)

        def step(j, carry):
            m_i, l_i, acc = carry
            cols = pl.ds(pl.multiple_of(j * tb, tb), tb)
            kv = k_ref[0, 0, cols, :]
            vv = v_ref[0, 0, cols, :]
            ck = ck_ref[0, 0, :, cols]
            s = _dot_nt(qv, kv) * FOX_SCALE + cq - ck
            s = jnp.where(qi * tb + rowi >= j * tb + coli, s, NEG)
            m_new = jnp.maximum(m_i, jnp.max(s, axis=1, keepdims=True))
            p = jnp.exp(s - m_new)
            alpha = jnp.exp(m_i - m_new)
            l_new = alpha * l_i + jnp.sum(p, axis=1, keepdims=True)
            acc_new = alpha * acc + _dot(p.astype(BF16), vv)
            return m_new, l_new, acc_new

        init = (jnp.full((tb, 1), NEG, F32), jnp.zeros((tb, 1), F32), jnp.zeros((tb, e), F32))
        m_i, l_i, acc = lax.fori_loop(0, qi + 1, step, init)
        o_ref[0, 0] = acc / l_i
        lse_ref[0, 0] = m_i + jnp.log(l_i)

    blk = pl.BlockSpec((1, 1, tb, e), lambda b, h, i: (b, h, i, 0))
    full = pl.BlockSpec((1, 1, seq, e), lambda b, h, i: (b, h, 0, 0))
    col_blk = pl.BlockSpec((1, 1, tb, 1), lambda b, h, i: (b, h, i, 0))
    row_full = pl.BlockSpec((1, 1, 1, seq), lambda b, h, i: (b, h, 0, 0))
    return pl.pallas_call(
        body, name="fox_fwd", grid=(nb, nh, nq),
        out_shape=[jax.ShapeDtypeStruct((nb, nh, seq, e), F32), jax.ShapeDtypeStruct((nb, nh, seq, 1), F32)],
        in_specs=[blk, full, full, col_blk, row_full],
        out_specs=[blk, col_blk],
        compiler_params=_params(("parallel", "parallel", "arbitrary")),
    )(q, k, v, cum_col, cum_row)


def _fox_bwd(q, k, v, o, lse, do, cum_col, cum_row):
    nb, nh, seq, e = q.shape
    tb = min(FOX_BLOCK, seq)
    nq = seq // tb

    def body(q_ref, k_ref, v_ref, o_ref, lse_ref, do_ref, cq_ref, ck_ref, dq_ref, dk_ref, dv_ref, dcum_ref,
             dcq_ref):
        kj = pl.program_id(2)

        @pl.when(kj == 0)
        def _():
            dq_ref[...] = jnp.zeros_like(dq_ref)
            dcq_ref[...] = jnp.zeros_like(dcq_ref)

        kv = k_ref[0, 0]
        vv = v_ref[0, 0]
        ck = ck_ref[0, 0]
        rowi = lax.broadcasted_iota(jnp.int32, (tb, tb), 0)
        coli = lax.broadcasted_iota(jnp.int32, (tb, tb), 1)

        def step(i, carry):
            dk, dv, csum = carry
            rows = pl.ds(pl.multiple_of(i * tb, tb), tb)
            qv = q_ref[0, 0, rows, :]
            dov = do_ref[0, 0, rows, :]
            dob = dov.astype(BF16)
            delta = jnp.sum(dov * o_ref[0, 0, rows, :], axis=1, keepdims=True)
            s = _dot_nt(qv, kv) * FOX_SCALE + cq_ref[0, 0, rows, :] - ck
            s = jnp.where(i * tb + rowi >= kj * tb + coli, s, NEG)
            p = jnp.exp(s - lse_ref[0, 0, rows, :])
            dp = _dot_nt(dob, vv)
            ds = p * (dp - delta)
            dsb = ds.astype(BF16)
            dv = dv + _dot_tn(p.astype(BF16), dob)
            dk = dk + _dot_tn(dsb, qv)
            dq_ref[0, 0, rows, :] += _dot(dsb, kv) * FOX_SCALE
            dcq_ref[0, 0, rows, :] += jnp.sum(ds, axis=1, keepdims=True)
            csum = csum + jnp.sum(ds, axis=0, keepdims=True)
            return dk, dv, csum

        init = (jnp.zeros((tb, e), F32), jnp.zeros((tb, e), F32), jnp.zeros((1, tb), F32))
        dk, dv, csum = lax.fori_loop(kj, nq, step, init)
        dk_ref[0, 0] = dk * FOX_SCALE
        dv_ref[0, 0] = dv
        dcum_ref[0, 0] = -csum

    blk = pl.BlockSpec((1, 1, tb, e), lambda b, h, j: (b, h, j, 0))
    full = pl.BlockSpec((1, 1, seq, e), lambda b, h, j: (b, h, 0, 0))
    col_full = pl.BlockSpec((1, 1, seq, 1), lambda b, h, j: (b, h, 0, 0))
    row_blk = pl.BlockSpec((1, 1, 1, tb), lambda b, h, j: (b, h, 0, j))
    return pl.pallas_call(
        body, name="fox_bwd", grid=(nb, nh, nq),
        out_shape=[jax.ShapeDtypeStruct((nb, nh, seq, e), F32), jax.ShapeDtypeStruct((nb, nh, seq, e), F32),
                   jax.ShapeDtypeStruct((nb, nh, seq, e), F32), jax.ShapeDtypeStruct((nb, nh, 1, seq), F32),
                   jax.ShapeDtypeStruct((nb, nh, seq, 1), F32)],
        in_specs=[full, blk, blk, full, col_full, full, col_full, row_blk],
        out_specs=[full, blk, blk, row_blk, col_full],
        compiler_params=_params(("parallel", "parallel", "arbitrary")),
    )(q, k, v, o, lse, do, cum_col, cum_row)


_ANY = pl.BlockSpec(memory_space=pl.ANY)


def _place():
    return lax.axis_index("x"), lax.axis_index("y"), lax.axis_index("c")


def _all_gather(shards, name):
    n = len(shards)

    def body(*refs):
        x_refs, out_refs = refs[:n], refs[n:2 * n]
        send_sems, recv_sems, local_sems = refs[2 * n:]
        x, y, c = _place()
        me, sibling = (x, y, c), (x, y, 1 - c)
        chips = [(1 - x, y), (x, 1 - y), (1 - x, 1 - y)]

        def rows(a, px, py, pc):
            return out_refs[a].at[4 * px + 2 * py + pc]

        def copy(a, k, block, to, src=None):
            return pltpu.make_async_remote_copy(
                src_ref=rows(a, *block) if src is None else src, dst_ref=rows(a, *block),
                send_sem=send_sems.at[a, k], recv_sem=recv_sems.at[a, k],
                device_id=to, device_id_type=pl.DeviceIdType.MESH)

        mine = [pltpu.make_async_copy(x_refs[a], rows(a, *me), local_sems.at[a]) for a in range(n)]
        for cp in mine:
            cp.start()
        first = []
        for a in range(n):
            first.append(copy(a, 0, me, sibling, src=x_refs[a]))
            first += [copy(a, 1 + j, me, (*chip, c), src=x_refs[a]) for j, chip in enumerate(chips)]
        for cp in first:
            cp.start()
        passed = []
        for j, chip in enumerate(chips):
            for a in range(n):
                copy(a, 1 + j, (*chip, c), me).wait_recv()
                passed.append(copy(a, 4 + j, (*chip, c), sibling))
                passed[-1].start()
        for a in range(n):
            copy(a, 0, sibling, me).wait_recv()
            for j, chip in enumerate(chips):
                copy(a, 4 + j, (*chip, 1 - c), me).wait_recv()
        for cp in first + passed:
            cp.wait_send()
        for cp in mine:
            cp.wait()

    return pl.pallas_call(
        body, name=name,
        out_shape=[jax.ShapeDtypeStruct((N_DEV,) + s.shape, s.dtype) for s in shards],
        in_specs=[_ANY] * n, out_specs=[_ANY] * n,
        scratch_shapes=[pltpu.SemaphoreType.DMA((n, 7)), pltpu.SemaphoreType.DMA((n, 7)),
                        pltpu.SemaphoreType.DMA((n,))],
    )(*shards)


def _sibling_exchange(full, name):
    n = len(full)

    def body(*refs):
        g_refs, out_refs = refs[:n], refs[n:2 * n]
        send_sems, recv_sems = refs[2 * n:]
        x, y, c = _place()
        copies = []
        for a in range(n):
            for k in range(4):
                px, py = k // 2, k % 2
                copies.append(pltpu.make_async_remote_copy(
                    src_ref=g_refs[a].at[4 * px + 2 * py + (1 - c)], dst_ref=out_refs[a].at[k],
                    send_sem=send_sems.at[a, k], recv_sem=recv_sems.at[a, k],
                    device_id=(x, y, 1 - c), device_id_type=pl.DeviceIdType.MESH))
        for cp in copies:
            cp.start()
        for cp in copies:
            cp.wait()

    return pl.pallas_call(
        body, name=name,
        out_shape=[jax.ShapeDtypeStruct((4,) + f.shape[1:], f.dtype) for f in full],
        in_specs=[_ANY] * n, out_specs=[_ANY] * n,
        scratch_shapes=[pltpu.SemaphoreType.DMA((n, 4)), pltpu.SemaphoreType.DMA((n, 4))],
    )(*full)


def _chip_exchange(part, name):
    n = len(part)

    def body(*refs):
        p_refs, out_refs = refs[:n], refs[n:2 * n]
        send_sems, recv_sems = refs[2 * n:]
        x, y, c = _place()
        peers = [(1 - x, y), (x, 1 - y), (1 - x, 1 - y)]
        copies = []
        for a in range(n):
            for k, (px, py) in enumerate(peers):
                copies.append(pltpu.make_async_remote_copy(
                    src_ref=p_refs[a].at[2 * px + py], dst_ref=out_refs[a].at[k],
                    send_sem=send_sems.at[a, k], recv_sem=recv_sems.at[a, k],
                    device_id=(px, py, c), device_id_type=pl.DeviceIdType.MESH))
        for cp in copies:
            cp.start()
        for cp in copies:
            cp.wait()

    return pl.pallas_call(
        body, name=name,
        out_shape=[jax.ShapeDtypeStruct((3,) + p.shape[1:], p.dtype) for p in part],
        in_specs=[_ANY] * n, out_specs=[_ANY] * n,
        scratch_shapes=[pltpu.SemaphoreType.DMA((n, 3)), pltpu.SemaphoreType.DMA((n, 3))],
    )(*part)


def _pick_rows(rows, cap=512):
    t = cap
    while t >= 8:
        if rows % t == 0:
            return t
        t //= 2
    raise ValueError(rows)


def _pair_sum(full, got, name):
    _, rows, cols = full.shape
    tile = _pick_rows(rows, 256)
    c = lax.axis_index("c").astype(jnp.int32).reshape(1)

    def body(c_ref, a_ref, b_ref, o_ref):
        o_ref[...] = a_ref[...] + b_ref[...]

    blk = (1, tile, cols)
    return pl.pallas_call(
        body, name=name,
        grid_spec=pltpu.PrefetchScalarGridSpec(
            num_scalar_prefetch=1, grid=(4, rows // tile),
            in_specs=[pl.BlockSpec(blk, lambda k, i, c_ref: (4 * (k // 2) + 2 * (k % 2) + c_ref[0], i, 0)),
                      pl.BlockSpec(blk, lambda k, i, c_ref: (k, i, 0))],
            out_specs=pl.BlockSpec(blk, lambda k, i, c_ref: (k, i, 0))),
        out_shape=jax.ShapeDtypeStruct((4, rows, cols), full.dtype),
        compiler_params=_params(("arbitrary", "arbitrary")),
    )(c, full, got)


def _adam_math(w, g, m, v):
    c1 = 1.0 / (1.0 - ADAM_B1 ** ADAM_STEP)
    c2 = 1.0 / (1.0 - ADAM_B2 ** ADAM_STEP)
    m_new = ADAM_B1 * m + (1.0 - ADAM_B1) * g
    v_new = ADAM_B2 * v + (1.0 - ADAM_B2) * (g * g)
    delta = -ADAM_LR * ((m_new * c1) / (jnp.sqrt(v_new * c2) + ADAM_EPS) + ADAM_WD * w)
    return delta, m_new, v_new


def _sum_adamw(part, others, w, m, v, name):
    _, rows, cols = part.shape
    tile = _pick_rows(rows, 128)
    own = (2 * lax.axis_index("x") + lax.axis_index("y")).astype(jnp.int32).reshape(1)

    def body(own_ref, p_ref, o_ref, w_ref, m_ref, v_ref, g_out, d_out, m_out, v_out):
        g = ((p_ref[0] + o_ref[0]) + o_ref[1]) + o_ref[2]
        delta, m_new, v_new = _adam_math(w_ref[...], g, m_ref[...], v_ref[...])
        g_out[...] = g
        d_out[...] = delta
        m_out[...] = m_new
        v_out[...] = v_new

    flat = pl.BlockSpec((tile, cols), lambda i, own_ref: (i, 0))
    shape = jax.ShapeDtypeStruct((rows, cols), F32)
    return pl.pallas_call(
        body, name=name,
        grid_spec=pltpu.PrefetchScalarGridSpec(
            num_scalar_prefetch=1, grid=(rows // tile,),
            in_specs=[pl.BlockSpec((1, tile, cols), lambda i, own_ref: (own_ref[0], i, 0)),
                      pl.BlockSpec((3, tile, cols), lambda i, own_ref: (0, i, 0)), flat, flat, flat],
            out_specs=[flat] * 4),
        out_shape=[shape] * 4,
        compiler_params=_params(("arbitrary",)),
    )(own, part, others, w, m, v)


def _all_reduce_small(vec):
    gathered = _all_gather([vec], "ar_gather")[0]
    rows = vec.shape[0]

    def fn(*blocks):
        s = blocks[0]
        for b in blocks[1:]:
            s = s + b
        return s

    return _rowcall("ar_sum", fn, [gathered[j] for j in range(N_DEV)], [], [(1024, F32)],
                    tile=_pick_rows(rows))[0]


def _pad_rows(flat, mult):
    n = flat.shape[-1]
    per = mult * 1024
    padded = -(-n // per) * per
    pad = [(0, 0)] * (flat.ndim - 1) + [(0, padded - n)]
    return jnp.pad(flat, pad).reshape(flat.shape[:-1] + (padded // 1024, 1024))


def _regroup_w_in(w):
    pad = jnp.zeros((w.shape[0], 116), w.dtype)
    return jnp.concatenate([w[:, 768:1280], w[:, 1286:1798], w[:, 1280:1286], w[:, 2950:2956], pad,
                            w[:, 0:768], w[:, 1798:2950]], axis=1)


def _ungroup_w_in(wp):
    return jnp.concatenate([wp[:, 1152:1920], wp[:, 0:512], wp[:, 1024:1030], wp[:, 512:1024],
                            wp[:, 1920:3072], wp[:, 1030:1036]], axis=1)


def _to_shard(name, a):
    if name == 'w_in':
        return _regroup_w_in(a)
    if name in ('w_gate', 'w_up'):
        return jnp.pad(a, ((0, 0), (0, FF_SHARD_P - FF_SHARD)))
    if name == 'w_down':
        return jnp.pad(a, ((0, FF_SHARD_P - FF_SHARD), (0, 0)))
    return a


def _from_shard(name, a):
    if name == 'w_in':
        return _ungroup_w_in(a)
    if name in ('w_gate', 'w_up'):
        return a[:, 0:FF_SHARD]
    if name == 'w_down':
        return a[0:FF_SHARD, :]
    return a


def _whole(name, gathered):
    if BIG[name][0] == 1:
        return gathered.reshape(-1, gathered.shape[-1])
    return gathered.transpose(1, 0, 2).reshape(gathered.shape[1], -1)


def _split(name, whole):
    if BIG[name][0] == 1:
        return whole.reshape(N_DEV, whole.shape[0] // N_DEV, whole.shape[1])
    return whole.reshape(whole.shape[0], N_DEV, whole.shape[1] // N_DEV).transpose(1, 0, 2)


def _pack_list(arrays, mult):
    return _pad_rows(jnp.concatenate([a.reshape(-1) for a in arrays]), mult)


def _unpack_list(buf, shapes):
    flat = buf.reshape(-1)
    out, off = [], 0
    for s in shapes:
        n = math.prod(s)
        out.append(flat[off:off + n].reshape(s))
        off += n
    return out


def _adamw(w, g, m, v):
    return _rowcall("adamw", _adam_math, [w, g, m, v], [], [(1024, F32)] * 3, tile=_pick_rows(w.shape[0]))


def _to_heads(a, nb, seq):
    return a.reshape(nb, seq, -1, HEAD_DIM).transpose(0, 2, 1, 3)


def _from_heads(a):
    nb, nh, seq, e = a.shape
    return a.transpose(0, 2, 1, 3).reshape(nb * seq, nh * e)


def _block_diag(w):
    out = jnp.zeros((LRU_WIDTH, LRU_WIDTH), w.dtype)
    for g in range(4):
        out = lax.dynamic_update_slice(out, w[g], (64 * g, 64 * g))
    return out


def _block_diag_grad(full):
    return jnp.stack([full[64 * g:64 * (g + 1), 64 * g:64 * (g + 1)] for g in range(4)])


def _row(v):
    return v.reshape(1, -1).astype(F32)


def _lane128(*pieces):
    flat = jnp.concatenate([p.reshape(-1).astype(F32) for p in pieces])
    return jnp.pad(flat, (0, 128 - flat.shape[0])).reshape(1, 128)


def _layer_consts(w):
    c = {}
    cw, cb = w['ssd_conv_w'], w['ssd_conv_b']
    c['cw_x'], c['cw_b'], c['cw_c'] = cw[:, 0:384], cw[:, 384:640], cw[:, 640:896]
    c['cb_x'], c['cb_b'], c['cb_c'] = _row(cb[0:384]), _row(cb[384:640]), _row(cb[640:896])
    c['bias128'] = _lane128(w['ssd_dt_bias'], w['fox_b_f'])
    c['alog128'] = _lane128(w['ssd_a_log'])
    c['d384'] = _row(jnp.repeat(w['ssd_d'], HEAD_DIM))
    c['lcw'], c['lcb'] = w['lru_conv_w'], _row(w['lru_conv_b'])
    c['wa'], c['wx'] = _block_diag(w['lru_w_a']).astype(BF16), _block_diag(w['lru_w_x']).astype(BF16)
    c['ba'], c['bx'], c['lam'] = _row(w['lru_b_a']), _row(w['lru_b_x']), _row(w['lru_lambda'])
    return c


def _layer_fwd(h0, p_i, w, c):
    nb, seq = c['nb'], c['seq']

    u1 = _rowcall("norm1", lambda h, g: _rms(h, g), [h0], [_row(w['norm1_g'])], [(D_MODEL, BF16)])[0]
    proj = _matmul(u1, w['w_in'], 'nn', "proj")

    xs_c = _convsilu_fwd("conv_x", (proj, 384, 4), seq, c['cw_x'], c['cb_x'], F32)
    b_c = _convsilu_fwd("conv_b", (proj, 256, 0), seq, c['cw_b'], c['cb_b'], BF16)
    c_c = _convsilu_fwd("conv_c", (proj, 256, 1), seq, c['cw_c'], c['cb_c'], BF16)
    dt_arr, cum, prev, end = _small_fwd((proj, 128, 8), seq, c['bias128'], c['alog128'])
    x_h = xs_c.reshape(nb, seq, SSD_WIDTH)
    cum3 = cum.reshape(nb, seq, 128)
    a_row = cum3[:, :, 0:8].transpose(0, 2, 1)
    ssd_in = (x_h, b_c.reshape(nb, seq, 256), c_c.reshape(nb, seq, 256), dt_arr.reshape(nb, seq, 128), cum3,
              prev.reshape(nb, seq, 128), end.reshape(nb, seq, 128), a_row)
    y_h, states = _ssd_fwd(*ssd_in)
    y_core = y_h.reshape(nb * seq, SSD_WIDTH)

    hseq, xl = _lru_fwd(proj, seq, c['lcw'], c['lcb'], c['wa'], c['ba'], c['wx'], c['bx'], c['lam'])

    q = _to_heads(proj[:, 1920:2304].astype(BF16), nb, seq)
    k = _to_heads(proj[:, 2304:2688].astype(BF16), nb, seq)
    v = _to_heads(proj[:, 2688:3072].astype(BF16), nb, seq)
    cf = cum3[:, :, SSD_HEADS:2 * SSD_HEADS].transpose(0, 2, 1)
    cf_col, cf_row = cf[..., None], cf[:, :, None, :]
    o_h, lse = _fox_fwd(q, k, v, cf_col, cf_row)
    y_fox = _from_heads(o_h)

    def post(yc, xs, z, hs, lg, yf, d, g1, g2, g3):
        y1 = _rms((yc + xs * d) * _silu(z), g1)
        y2 = _rms(hs * _gelu(lg), g2)
        y3 = _rms(yf, g3)
        return jnp.concatenate([y1, y2, y3], axis=-1)

    post_consts = [c['d384'], _row(w['ssd_norm_g']), _row(w['lru_norm_g']), _row(w['fox_norm_g'])]
    ycat = _rowcall("mix_post", post, [y_core, xs_c, (proj, 384, 3), hseq, (proj, 256, 3), y_fox], post_consts,
                    [(D_MODEL, BF16)])[0]
    mix = _matmul(ycat, w['w_out'], 'nn', "mix_out")

    def res_norm(h, d, g):
        hn = h + d
        return hn, _rms(hn, g)

    h1, u2 = _rowcall("res_norm2", res_norm, [h0, mix], [_row(w['norm2_g'])], [(D_MODEL, F32), (D_MODEL, BF16)])
    gu = _matmul(u2, w['w_gu'], 'nn', "ffn_in")
    act = _rowcall("swiglu", lambda gt, up: _silu(gt) * up, [(gu, D_FF_P, 0), (gu, D_FF_P, 1)], [], [(D_FF_P, BF16)])[0]
    ff = _matmul(act, w['w_down'], 'nn', "ffn_out")
    h2, u3 = _rowcall("res_norm3", res_norm, [h1, ff], [_row(w['norm3_g'])], [(D_MODEL, F32), (D_MODEL, BF16)])
    pg = _matmul(u3, w['w_ple_gate'], 'nn', "ple_gate")
    pp = _matmul(p_i, w['w_ple_proj'], 'nn', "ple_proj")
    h3 = _rowcall("ple", lambda h, a, b, bias: h + _sigmoid(a + bias) * b, [h2, pg, pp], [_row(w['b_ple_gate'])],
                  [(D_MODEL, F32)])[0]
    saved = dict(h0=h0, u1=u1, proj=proj, xs_c=xs_c, dt_arr=dt_arr, ssd_in=ssd_in, states=states,
                 y_core=y_core, hseq=hseq, xl=xl, q=q, k=k, v=v, cf_col=cf_col, cf_row=cf_row, o_h=o_h, lse=lse,
                 y_fox=y_fox, post_consts=post_consts, ycat=ycat, h1=h1, u2=u2, gu=gu, act=act, h2=h2, u3=u3,
                 pg=pg, pp=pp, p_i=p_i)
    return h3, saved


def _layer_bwd(dh3, s, w, c):
    nb, seq = c['nb'], c['seq']
    g = {}

    def ple_bwd(dh, a, b, bias):
        gate = _sigmoid(a + bias)
        dpg = dh * b * gate * (1.0 - gate)
        return dh * gate, dpg, jnp.sum(dpg, axis=0, keepdims=True)

    dpp, dpg, g['b_ple_gate'] = _rowcall("ple_bwd", ple_bwd, [dh3, s['pg'], s['pp']], [_row(w['b_ple_gate'])],
                                         [(D_MODEL, BF16), (D_MODEL, BF16)], [((1, D_MODEL), F32)])
    g['w_ple_proj'] = _matmul(s['p_i'], dpp, 'tn', "d_w_ple_proj")
    g['w_ple_gate'] = _matmul(s['u3'], dpg, 'tn', "d_w_ple_gate")
    du3 = _matmul(dpg, w['w_ple_gate'], 'nt', "d_u3")

    def norm_bwd(h, du, dh, gain):
        dx, dg = _rms_bwd(h, gain, du)
        dhn = dh + dx
        return dhn, dhn, dg

    dh2, dh2_b, g['norm3_g'] = _rowcall("norm3_bwd", norm_bwd, [s['h2'], du3, dh3], [_row(w['norm3_g'])],
                                        [(D_MODEL, F32), (D_MODEL, BF16)], [((1, D_MODEL), F32)])
    g['w_down'] = _matmul(s['act'], dh2_b, 'tn', "d_w_down")
    dact = _matmul(dh2_b, w['w_down'], 'nt', "d_act")

    def swiglu_bwd(gt, up, da):
        return jnp.concatenate([da * up * _dsilu(gt), da * _silu(gt)], axis=-1)

    dgu = _rowcall("swiglu_bwd", swiglu_bwd, [(s['gu'], D_FF_P, 0), (s['gu'], D_FF_P, 1), dact], [],
                   [(2 * D_FF_P, BF16)])[0]
    g['w_gu'] = _matmul(s['u2'], dgu, 'tn', "d_w_gu")
    du2 = _matmul(dgu, w['w_gu'], 'nt', "d_u2")
    dh1, dh1_b, g['norm2_g'] = _rowcall("norm2_bwd", norm_bwd, [s['h1'], du2, dh2], [_row(w['norm2_g'])],
                                        [(D_MODEL, F32), (D_MODEL, BF16)], [((1, D_MODEL), F32)])
    g['w_out'] = _matmul(s['ycat'], dh1_b, 'tn', "d_w_out")
    dycat = _matmul(dh1_b, w['w_out'], 'nt', "d_ycat")

    def post_bwd(dy, yc, xs, z, hs, lg, yf, d, g1, g2, g3):
        sz = _silu(z)
        ytot = yc + xs * d
        dpre1, dg1 = _rms_bwd(ytot * sz, g1, dy[:, 0:384])
        dytot = dpre1 * sz
        dz = dpre1 * ytot * _dsilu(z)
        dd = jnp.sum(dytot * xs, axis=0, keepdims=True)
        gl = _gelu(lg)
        dpre2, dg2 = _rms_bwd(hs * gl, g2, dy[:, 384:640])
        dyf, dg3 = _rms_bwd(yf, g3, dy[:, 640:1024])
        return dytot, dytot * d, dz, dpre2 * gl, dpre2 * hs * _dgelu(lg), dyf, dd, dg1, dg2, dg3

    (dy_core, dxs_skip, dz, dhseq, dlg, dy_fox, dd384, g['ssd_norm_g'], g['lru_norm_g'], g['fox_norm_g']) = _rowcall(
        "mix_post_bwd", post_bwd,
        [dycat, s['y_core'], s['xs_c'], (s['proj'], 384, 3), s['hseq'], (s['proj'], 256, 3), s['y_fox']],
        s['post_consts'],
        [(384, F32), (384, F32), (384, BF16), (256, F32), (256, BF16), (384, F32)],
        [((1, 384), F32), ((1, 384), F32), ((1, 256), F32), ((1, 384), F32)])
    g['ssd_d'] = dd384.reshape(SSD_HEADS, HEAD_DIM).sum(axis=1)

    do_h = _to_heads(dy_fox, nb, seq)
    dq_h, dk_h, dv_h, dcf_row, dcf_col = _fox_bwd(s['q'], s['k'], s['v'], s['o_h'], s['lse'], do_h,
                                                  s['cf_col'], s['cf_row'])
    dq, dk, dv = (_from_heads(t).astype(BF16) for t in (dq_h, dk_h, dv_h))

    dx_h, db_c, dc_c, da_arr, dend_arr, ddt_arr = _ssd_bwd(*s['ssd_in'], s['states'],
                                                           dy_core.reshape(nb, seq, SSD_WIDTH))
    dxs_c = dx_h.reshape(nb * seq, SSD_WIDTH) + dxs_skip
    dcf = (dcf_row.reshape(nb, FOX_HEADS, seq) + dcf_col.reshape(nb, FOX_HEADS, seq)).transpose(0, 2, 1)
    dcum = jnp.concatenate([da_arr[:, :, 0:SSD_HEADS], dcf,
                            jnp.zeros((nb, seq, 128 - 2 * SSD_HEADS), F32)], axis=-1).reshape(nb * seq, 128)
    proj = s['proj']
    dxs_raw, dcw_x, dcb_x = _convsilu_bwd("conv_x_bwd", (proj, 384, 4), seq, c['cw_x'], c['cb_x'], dxs_c)
    db_raw, dcw_b, dcb_b = _convsilu_bwd("conv_b_bwd", (proj, 256, 0), seq, c['cw_b'], c['cb_b'],
                                         db_c.reshape(nb * seq, 256))
    dc_raw, dcw_c, dcb_c = _convsilu_bwd("conv_c_bwd", (proj, 256, 1), seq, c['cw_c'], c['cb_c'],
                                         dc_c.reshape(nb * seq, 256))
    dsmall, dbias128, dalog128 = _small_bwd((proj, 128, 8), seq, dcum, dend_arr.reshape(nb * seq, 128),
                                            ddt_arr.reshape(nb * seq, 128), s['dt_arr'], c['bias128'], c['alog128'])
    g['ssd_conv_w'] = jnp.concatenate([dcw_x, dcw_b, dcw_c], axis=1)
    g['ssd_conv_b'] = jnp.concatenate([dcb_x, dcb_b, dcb_c], axis=1).reshape(-1)
    g['ssd_dt_bias'] = dbias128[0, 0:SSD_HEADS]
    g['fox_b_f'] = dbias128[0, SSD_HEADS:2 * SSD_HEADS]
    g['ssd_a_log'] = dalog128[0, 0:SSD_HEADS]

    (dlru_raw, g['lru_conv_w'], dlcb, dwa, dba, dwx, dbx, dlam) = _lru_bwd(
        s['proj'], seq, s['xl'], s['hseq'], dhseq, c['lcw'], c['lcb'], c['wa'], c['ba'], c['wx'], c['bx'], c['lam'])
    g['lru_conv_b'], g['lru_b_a'], g['lru_b_x'], g['lru_lambda'] = (t.reshape(-1) for t in (dlcb, dba, dbx, dlam))
    g['lru_w_a'], g['lru_w_x'] = _block_diag_grad(dwa), _block_diag_grad(dwx)

    dproj = jnp.concatenate([db_raw, dc_raw, dlru_raw, dlg, dsmall, dz, dxs_raw, dq, dk, dv], axis=1)
    g['w_in'] = _matmul(s['u1'], dproj, 'tn', "d_w_in")
    du1 = _matmul(dproj, w['w_in'], 'nt', "d_u1")

    def norm1_bwd(h, du, dh, gain):
        dx, dg = _rms_bwd(h, gain, du)
        return dh + dx, dg

    dh0, g['norm1_g'] = _rowcall("norm1_bwd", norm1_bwd, [s['h0'], du1, dh1], [_row(w['norm1_g'])],
                                 [(D_MODEL, F32)], [((1, D_MODEL), F32)])
    for name in ('b_ple_gate', 'norm3_g', 'norm2_g', 'norm1_g', 'ssd_norm_g', 'lru_norm_g', 'fox_norm_g'):
        g[name] = g[name].reshape(-1)
    wgu = g.pop('w_gu')
    g['w_gate'], g['w_up'] = wgu[:, 0:D_FF_P], wgu[:, D_FF_P:2 * D_FF_P]
    return dh0, g


def _local_step(x, p, target, big, small):
    nb, seq, _ = x.shape
    tokens = nb * seq
    h = x.reshape(tokens, D_MODEL)
    layers, saves = [], []
    for i in range(DEPTH):
        w = {name: small[name][i] for name in small if name != 'final_norm_g'}
        for name in ('w_in', 'w_out', 'w_down', 'w_ple_gate', 'w_ple_proj'):
            w[name] = big[name][i]
        w['w_gu'] = jnp.concatenate([big['w_gate'][i], big['w_up'][i]], axis=1)
        c = _layer_consts(w)
        c['nb'], c['seq'] = nb, seq
        h, s = _layer_fwd(h, p[i].reshape(tokens, PLE_DIM).astype(BF16), w, c)
        layers.append((w, c))
        saves.append(s)

    def head(hf, tgt, gain):
        r = lax.rsqrt(jnp.mean(hf * hf, axis=-1, keepdims=True) + EPS)
        xhat = hf * r
        err = xhat * gain - tgt
        loss = 0.5 * jnp.sum(jnp.mean(err * err, axis=-1, keepdims=True), axis=0, keepdims=True)
        dy = err * (1.0 / D_MODEL)
        dg = jnp.sum(dy * xhat, axis=0, keepdims=True)
        dxhat = dy * gain
        dh = r * (dxhat - xhat * jnp.mean(dxhat * xhat, axis=-1, keepdims=True))
        return dh, jnp.broadcast_to(loss, (1, 128)), dg

    dh, loss128, dgf = _rowcall("loss_head", head, [h, target.reshape(tokens, D_MODEL)],
                                [_row(small['final_norm_g'])], [(D_MODEL, F32)],
                                [((1, 128), F32), ((1, D_MODEL), F32)])
    grads = {'final_norm_g': dgf.reshape(-1)}
    per_layer = [None] * DEPTH
    for i in range(DEPTH - 1, -1, -1):
        w, c = layers[i]
        dh, per_layer[i] = _layer_bwd(dh, saves[i], w, c)
    for name in per_layer[0]:
        if name in BIG:
            grads[name] = [per_layer[i][name] for i in range(DEPTH)]
        else:
            grads[name] = jnp.stack([per_layer[i][name] for i in range(DEPTH)])
    return loss128[0, 0], dh.reshape(nb, seq, D_MODEL), grads


def kernel(x, p, norm1_g, w_in, ssd_conv_w, ssd_conv_b, ssd_dt_bias, ssd_a_log, ssd_d, ssd_norm_g, lru_conv_w, lru_conv_b, lru_w_a, lru_b_a, lru_w_x, lru_b_x, lru_lambda, lru_norm_g, fox_b_f, fox_norm_g, w_out, norm2_g, w_gate, w_up, w_down, norm3_g, w_ple_gate, b_ple_gate, w_ple_proj, final_norm_g, loss_target, m_norm1_g, m_w_in, m_ssd_conv_w, m_ssd_conv_b, m_ssd_dt_bias, m_ssd_a_log, m_ssd_d, m_ssd_norm_g, m_lru_conv_w, m_lru_conv_b, m_lru_w_a, m_lru_b_a, m_lru_w_x, m_lru_b_x, m_lru_lambda, m_lru_norm_g, m_fox_b_f, m_fox_norm_g, m_w_out, m_norm2_g, m_w_gate, m_w_up, m_w_down, m_norm3_g, m_w_ple_gate, m_b_ple_gate, m_w_ple_proj, m_final_norm_g, v_norm1_g, v_w_in, v_ssd_conv_w, v_ssd_conv_b, v_ssd_dt_bias, v_ssd_a_log, v_ssd_d, v_ssd_norm_g, v_lru_conv_w, v_lru_conv_b, v_lru_w_a, v_lru_b_a, v_lru_w_x, v_lru_b_x, v_lru_lambda, v_lru_norm_g, v_fox_b_f, v_fox_norm_g, v_w_out, v_norm2_g, v_w_gate, v_w_up, v_w_down, v_norm3_g, v_w_ple_gate, v_b_ple_gate, v_w_ple_proj, v_final_norm_g):
    args = dict(locals())
    w_loc = {n: args[n] for n in WEIGHTS}
    m_loc = {n: args['m_' + n] for n in WEIGHTS}
    v_loc = {n: args['v_' + n] for n in WEIGHTS}
    dev = 4 * lax.axis_index("x") + 2 * lax.axis_index("y") + lax.axis_index("c")

    keys = [(n, i) for n in BIG for i in range(DEPTH)]
    conv_names = list(CONV_SHARDED)
    conv_loc_shapes = [w_loc[n].shape for n in conv_names]
    gathered = _all_gather([_to_shard(n, w_loc[n][i]).astype(BF16) for n, i in keys]
                           + [_pack_list([w_loc[n] for n in conv_names], 8)], "gather_w")
    big = {n: [None] * DEPTH for n in BIG}
    for (n, i), arr in zip(keys, gathered):
        big[n][i] = _whole(n, arr)
    small = {n: w_loc[n] for n in WEIGHTS if n not in BIG and n not in CONV_SHARDED}
    per_dev = [_unpack_list(gathered[-1][j], conv_loc_shapes) for j in range(N_DEV)]
    for idx, n in enumerate(conv_names):
        small[n] = jnp.concatenate([per_dev[j][idx] for j in range(N_DEV)], axis=2)

    loss_part, dx, grads = _local_step(x, p, loss_target, big, small)
    loss = lax.psum(loss_part, ("x", "y", "c"))

    full = [_split(n, grads[n][i]) for n, i in keys]
    got = _sibling_exchange(full, "rs_sibling")
    part = [_pair_sum(f, r, "rs_pair_sum_%s%d" % key) for key, f, r in zip(keys, full, got)]
    others = _chip_exchange(part, "rs_chips")
    out = {kind: {n: [None] * DEPTH for n in BIG} for kind in ('g', 'delta', 'm', 'v')}
    for (n, i), pt, ot in zip(keys, part, others):
        res = _sum_adamw(pt, ot, *[_to_shard(n, d[n][i]) for d in (w_loc, m_loc, v_loc)], "sum_adamw_%s%d" % (n, i))
        for kind, r in zip(('g', 'delta', 'm', 'v'), res):
            out[kind][n][i] = _from_shard(n, r)

    small_names = [n for n in WEIGHTS if n not in BIG]
    small_shapes = [grads[n].shape for n in small_names]
    g_small = dict(zip(small_names, _unpack_list(
        _all_reduce_small(_pack_list([grads[n] for n in small_names], 8)), small_shapes)))
    for n in CONV_SHARDED:
        width = CONV_SHARDED[n][2] // N_DEV
        g_small[n] = lax.dynamic_slice_in_dim(g_small[n], dev * width, width, axis=2)
    shapes = [w_loc[n].shape for n in small_names]
    packed = [_pack_list([d[n] for n in small_names], 8) for d in (w_loc, g_small, m_loc, v_loc)]
    upd = [dict(zip(small_names, _unpack_list(t, shapes))) for t in _adamw(*packed)]
    for kind, d in zip(('g', 'delta', 'm', 'v'), [g_small] + upd):
        for n in small_names:
            out[kind][n] = d[n]
        for n in BIG:
            out[kind][n] = jnp.stack(out[kind][n])
    return (loss, dx, *[out['g'][n] for n in WEIGHTS], *[out['delta'][n] for n in WEIGHTS],
            *[out['m'][n] for n in WEIGHTS], *[out['v'][n] for n in WEIGHTS])
```

```python
import functools
import math

import jax
import jax.numpy as jnp
from jax import lax
from jax.experimental import pallas as pl
from jax.experimental.pallas import tpu as pltpu

F32 = jnp.float32
BF16 = jnp.bfloat16

N_DEV = 8
D_MODEL = 1024
DEPTH = 2
HEAD_DIM = 64
SSD_WIDTH = 384
LRU_WIDTH = 256
FOX_WIDTH = 384
SSD_HEADS = 6
SSD_STATE = 128
CHUNK = 128
FOX_HEADS = 6
D_FF = 2816
FF_SHARD = D_FF // N_DEV
FF_SHARD_P = 384
D_FF_P = N_DEV * FF_SHARD_P
PLE_DIM = 256
IN_COLS = 2956
PROJ_COLS = 3072
LRU_C = 8.0
EPS = 1e-6
NEG = -1e30

ADAM_LR = 0.001
ADAM_B1 = 0.9
ADAM_B2 = 0.999
ADAM_EPS = 1e-08
ADAM_WD = 0.01
ADAM_STEP = 10

VMEM_LIMIT = 56 * 1024 * 1024

WEIGHTS = ['norm1_g', 'w_in', 'ssd_conv_w', 'ssd_conv_b', 'ssd_dt_bias', 'ssd_a_log', 'ssd_d', 'ssd_norm_g',
           'lru_conv_w', 'lru_conv_b', 'lru_w_a', 'lru_b_a', 'lru_w_x', 'lru_b_x', 'lru_lambda', 'lru_norm_g',
           'fox_b_f', 'fox_norm_g', 'w_out', 'norm2_g', 'w_gate', 'w_up', 'w_down', 'norm3_g', 'w_ple_gate',
           'b_ple_gate', 'w_ple_proj', 'final_norm_g']
BIG = {'w_in': (1, (DEPTH, D_MODEL, IN_COLS)), 'w_out': (1, (DEPTH, D_MODEL, D_MODEL)),
       'w_gate': (2, (DEPTH, D_MODEL, D_FF)), 'w_up': (2, (DEPTH, D_MODEL, D_FF)),
       'w_down': (1, (DEPTH, D_FF, D_MODEL)), 'w_ple_gate': (1, (DEPTH, D_MODEL, D_MODEL)),
       'w_ple_proj': (2, (DEPTH, PLE_DIM, D_MODEL))}
CONV_SHARDED = {'ssd_conv_w': (DEPTH, 4, 896), 'lru_conv_w': (DEPTH, 4, 256)}


def _dot(a, b):
    return jnp.dot(a, b, preferred_element_type=F32)


def _dot_nt(a, b):
    return lax.dot_general(a, b, (((1,), (1,)), ((), ())), preferred_element_type=F32)


def _dot_tn(a, b):
    return lax.dot_general(a, b, (((0,), (0,)), ((), ())), preferred_element_type=F32)


def _params(sem):
    return pltpu.CompilerParams(dimension_semantics=sem, vmem_limit_bytes=VMEM_LIMIT)


def _pick_tile(n, cap):
    if n <= cap:
        return n
    best = 128
    for t in range(128, cap + 1, 128):
        if n % t == 0:
            best = t
    assert n % best == 0, (n, cap)
    return best


def _matmul(a, b, mode, name, out_dtype=F32):
    if mode == 'tn':
        k_dim, m_dim = a.shape
        n_dim = b.shape[1]
    else:
        m_dim, k_dim = a.shape
        n_dim = b.shape[1] if mode == 'nn' else b.shape[0]
    tm = _pick_tile(m_dim, 512 if mode != 'tn' else 1024)
    tn = _pick_tile(n_dim, 1536 if mode != 'tn' else 1024)
    tk = _pick_tile(k_dim, 3072 if mode != 'tn' else 512)
    nk = k_dim // tk
    grid = (n_dim // tn, m_dim // tm, nk)

    def body(a_ref, b_ref, o_ref, acc_ref):
        kk = pl.program_id(2)

        @pl.when(kk == 0)
        def _():
            acc_ref[...] = jnp.zeros_like(acc_ref)

        if mode == 'nn':
            acc_ref[...] += _dot(a_ref[...], b_ref[...])
        elif mode == 'nt':
            acc_ref[...] += _dot_nt(a_ref[...], b_ref[...])
        else:
            acc_ref[...] += _dot_tn(a_ref[...], b_ref[...])

        @pl.when(kk == nk - 1)
        def _():
            o_ref[...] = acc_ref[...].astype(o_ref.dtype)

    if mode == 'nn':
        a_spec = pl.BlockSpec((tm, tk), lambda j, i, k: (i, k))
        b_spec = pl.BlockSpec((tk, tn), lambda j, i, k: (k, j))
    elif mode == 'nt':
        a_spec = pl.BlockSpec((tm, tk), lambda j, i, k: (i, k))
        b_spec = pl.BlockSpec((tn, tk), lambda j, i, k: (j, k))
    else:
        a_spec = pl.BlockSpec((tk, tm), lambda j, i, k: (k, i))
        b_spec = pl.BlockSpec((tk, tn), lambda j, i, k: (k, j))
    return pl.pallas_call(
        body, name=name, grid=grid,
        out_shape=jax.ShapeDtypeStruct((m_dim, n_dim), out_dtype),
        in_specs=[a_spec, b_spec],
        out_specs=pl.BlockSpec((tm, tn), lambda j, i, k: (i, j)),
        scratch_shapes=[pltpu.VMEM((tm, tn), F32)],
        compiler_params=_params(("parallel", "parallel", "arbitrary")),
    )(a, b)


def _rowcall(name, fn, tiled, consts, outs, accs=(), tile=512, scratch=()):
    specs, arrays = [], []
    for t in tiled:
        if isinstance(t, tuple):
            arr, width, blk = t
            specs.append(pl.BlockSpec((tile, width), functools.partial(lambda i, blk: (i, blk), blk=blk)))
        else:
            arr = t
            specs.append(pl.BlockSpec((tile, arr.shape[1]), lambda i: (i, 0)))
        arrays.append(arr)
    rows = arrays[0].shape[0]
    assert rows % tile == 0, (name, rows, tile)
    for c in consts:
        specs.append(pl.BlockSpec(c.shape, lambda i: (0, 0)))
        arrays.append(c)
    n_in, n_out, n_acc = len(arrays), len(outs), len(accs)
    out_shape = [jax.ShapeDtypeStruct((rows, c), dt) for c, dt in outs]
    out_specs = [pl.BlockSpec((tile, c), lambda i: (i, 0)) for c, _ in outs]
    out_shape += [jax.ShapeDtypeStruct(s, dt) for s, dt in accs]
    out_specs += [pl.BlockSpec(s, lambda i: (0, 0)) for s, _ in accs]

    def body(*refs):
        ins = [r[...] for r in refs[:n_in]]
        out_refs = refs[n_in:n_in + n_out]
        acc_refs = refs[n_in + n_out:n_in + n_out + n_acc]
        scr = refs[n_in + n_out + n_acc:]
        res = fn(*ins, *scr)
        if not isinstance(res, (tuple, list)):
            res = (res,)
        assert len(res) == n_out + n_acc, (name, len(res))
        for r, v in zip(out_refs, res[:n_out]):
            r[...] = v.astype(r.dtype)
        if n_acc:
            first = pl.program_id(0) == 0

            @pl.when(first)
            def _():
                for r, v in zip(acc_refs, res[n_out:]):
                    r[...] = v.astype(r.dtype)

            @pl.when(jnp.logical_not(first))
            def _():
                for r, v in zip(acc_refs, res[n_out:]):
                    r[...] += v.astype(r.dtype)

    res = pl.pallas_call(
        body, name=name, grid=(rows // tile,),
        out_shape=out_shape, in_specs=specs, out_specs=out_specs,
        scratch_shapes=list(scratch),
        compiler_params=_params(("arbitrary",)),
    )(*arrays)
    return res


def _sigmoid(x):
    return 1.0 / (1.0 + jnp.exp(-x))


def _softplus(x):
    return jnp.maximum(x, 0.0) + jnp.log(1.0 + jnp.exp(-jnp.abs(x)))


def _silu(x):
    return x * _sigmoid(x)


def _dsilu(x):
    s = _sigmoid(x)
    return s * (1.0 + x * (1.0 - s))


_GELU_C = math.sqrt(2.0 / math.pi)


def _gelu(x):
    return 0.5 * x * (1.0 + jnp.tanh(_GELU_C * (x + 0.044715 * x * x * x)))


def _dgelu(x):
    t = jnp.tanh(_GELU_C * (x + 0.044715 * x * x * x))
    return 0.5 * (1.0 + t) + 0.5 * x * (1.0 - t * t) * _GELU_C * (1.0 + 3.0 * 0.044715 * x * x)


def _neg_expm1(x):
    series = -x * (1.0 + x * (0.5 + x * (1.0 / 6.0 + x * (1.0 / 24.0 + x * (1.0 / 120.0)))))
    return jnp.where(x > -0.03, series, 1.0 - jnp.exp(x))


def _rms(x, g):
    r = lax.rsqrt(jnp.mean(x * x, axis=-1, keepdims=True) + EPS)
    return x * r * g


def _rms_bwd(x, g, dy):
    r = lax.rsqrt(jnp.mean(x * x, axis=-1, keepdims=True) + EPS)
    xhat = x * r
    dg = jnp.sum(dy * xhat, axis=0, keepdims=True)
    dxhat = dy * g
    dx = r * (dxhat - xhat * jnp.mean(dxhat * xhat, axis=-1, keepdims=True))
    return dx, dg


def _row_iota(shape):
    return lax.broadcasted_iota(jnp.int32, shape, 0)


def _shift_down(x, j):
    if j == 0:
        return x
    return jnp.where(_row_iota(x.shape) >= j, pltpu.roll(x, j, 0), 0.0)


def _shift_up(x, j):
    if j == 0:
        return x
    n = x.shape[0]
    return jnp.where(_row_iota(x.shape) < n - j, pltpu.roll(x, n - j, 0), 0.0)


def _conv(x, w, b):
    y = b + w[3:4, :] * x
    for k in range(3):
        y = y + w[k:k + 1, :] * _shift_down(x, 3 - k)
    return y


def _conv_bwd(x, w, dy):
    dx = w[3:4, :] * dy
    dws = []
    for k in range(3):
        dx = dx + w[k:k + 1, :] * _shift_up(dy, 3 - k)
        dws.append(jnp.sum(dy * _shift_down(x, 3 - k), axis=0, keepdims=True))
    dws.append(jnp.sum(dy * x, axis=0, keepdims=True))
    return dx, jnp.concatenate(dws, axis=0), jnp.sum(dy, axis=0, keepdims=True)


def _split3(x):
    hi = x.astype(BF16)
    r1 = x - hi.astype(F32)
    mid = r1.astype(BF16)
    lo = (r1 - mid.astype(F32)).astype(BF16)
    return hi, mid, lo


def _tri_dot(tri, x):
    hi, mid, lo = _split3(x)
    return _dot(tri, hi) + _dot(tri, mid) + _dot(tri, lo)


def _cumsum_rows(x):
    n = x.shape[0] // CHUNK
    r = lax.broadcasted_iota(jnp.int32, (CHUNK, CHUNK), 0)
    c = lax.broadcasted_iota(jnp.int32, (CHUNK, CHUNK), 1)
    tri = (r >= c).astype(BF16)
    carry = jnp.zeros((1, x.shape[1]), F32)
    cums, prevs, ends = [], [], []
    for i in range(n):
        blk = _tri_dot(tri, x[i * CHUNK:(i + 1) * CHUNK]) + carry
        prevs.append(jnp.broadcast_to(carry, blk.shape))
        carry = blk[CHUNK - 1:CHUNK, :]
        ends.append(jnp.broadcast_to(carry, blk.shape))
        cums.append(blk)
    return jnp.concatenate(cums, 0), jnp.concatenate(prevs, 0), jnp.concatenate(ends, 0)


def _rev_cumsum_rows(x):
    n = x.shape[0] // CHUNK
    r = lax.broadcasted_iota(jnp.int32, (CHUNK, CHUNK), 0)
    c = lax.broadcasted_iota(jnp.int32, (CHUNK, CHUNK), 1)
    tri = (r <= c).astype(BF16)
    carry = jnp.zeros((1, x.shape[1]), F32)
    local, whole = [None] * n, [None] * n
    for i in range(n - 1, -1, -1):
        local[i] = _tri_dot(tri, x[i * CHUNK:(i + 1) * CHUNK])
        whole[i] = local[i] + carry
        carry = whole[i][0:1, :]
    return jnp.concatenate(local, 0), jnp.concatenate(whole, 0)


def _convsilu_fwd(name, seg, seq, w, b, dtype):
    return _rowcall(name, lambda raw, w, b: _silu(_conv(raw, w, b)), [seg], [w, b], [(seg[1], dtype)], tile=seq)[0]


def _convsilu_bwd(name, seg, seq, w, b, dy):
    def fn(raw, dy, w, b):
        return _conv_bwd(raw, w, dy * _dsilu(_conv(raw, w, b)))

    width = seg[1]
    return _rowcall(name, fn, [seg, dy], [w, b], [(width, BF16)], [((4, width), F32), ((1, width), F32)], tile=seq)


def _small_fwd(seg, seq, bias128, alog128):
    def fn(small, bias, alog):
        lane = lax.broadcasted_iota(jnp.int32, small.shape, 1)
        a = -jnp.exp(alog)
        s = small + bias
        dt = _softplus(s)
        logf = -_softplus(-s)
        pre = jnp.where(lane < SSD_HEADS, a * dt, jnp.where(lane < 2 * SSD_HEADS, logf, 0.0))
        cum, prev, end = _cumsum_rows(pre)
        return dt, cum, prev, end

    return _rowcall("small_fwd", fn, [seg], [bias128, alog128], [(128, F32)] * 4, tile=seq)


def _small_bwd(seg, seq, dcum, dend, ddt, dt_arr, bias128, alog128):
    def fn(small, dcum, dend, ddt, dt_arr, bias, alog):
        lane = lax.broadcasted_iota(jnp.int32, small.shape, 1)
        a = -jnp.exp(alog)
        sig = _sigmoid(small + bias)
        local, whole = _rev_cumsum_rows(dcum)
        dadt = local + dend
        d_dt = ddt + a * dadt
        ds = jnp.where(lane < SSD_HEADS, d_dt * sig, jnp.where(lane < 2 * SSD_HEADS, whole * (1.0 - sig), 0.0))
        da = jnp.sum(jnp.where(lane < SSD_HEADS, dadt * dt_arr, 0.0), axis=0, keepdims=True)
        return ds, jnp.sum(ds, axis=0, keepdims=True), da * a

    return _rowcall("small_bwd", fn, [seg, dcum, dend, ddt, dt_arr], [bias128, alog128], [(128, BF16)],
                    [((1, 128), F32), ((1, 128), F32)], tile=seq)


HEAD_PAIRS = SSD_HEADS // 2


def _ssd_specs(nc, reverse):
    def at(c):
        return nc - 1 - c if reverse else c

    x_spec = pl.BlockSpec((1, CHUNK, SSD_WIDTH), lambda b, c: (b, at(c), 0))
    bc_spec = pl.BlockSpec((1, CHUNK, 256), lambda b, c: (b, at(c), 0))
    col_spec = pl.BlockSpec((1, CHUNK, 128), lambda b, c: (b, at(c), 0))
    row_spec = pl.BlockSpec((1, 8, CHUNK), lambda b, c: (b, 0, at(c)))
    st_spec = pl.BlockSpec((1, 1, HEAD_PAIRS, SSD_STATE, 128), lambda b, c: (b, at(c), 0, 0, 0))
    return x_spec, bc_spec, col_spec, row_spec, st_spec


def _ssd_head(h, dtb, acb, apb, aeb, arb):
    return dtb[:, h:h + 1], acb[:, h:h + 1], apb[:, h:h + 1], aeb[:, h:h + 1], arb[h:h + 1, :]


def _ssd_fwd(x, bm, cm, dt_arr, cum, prev, end, a_row):
    nb, seq, _ = x.shape
    nc = seq // CHUNK
    x_spec, bc_spec, col_spec, row_spec, st_spec = _ssd_specs(nc, False)

    def body(x_ref, b_ref, c_ref, dt_ref, ac_ref, ap_ref, ae_ref, ar_ref, y_ref, st_ref, s_scr):
        @pl.when(pl.program_id(1) == 0)
        def _():
            s_scr[...] = jnp.zeros_like(s_scr)

        causal = (lax.broadcasted_iota(jnp.int32, (CHUNK, CHUNK), 0)
                  >= lax.broadcasted_iota(jnp.int32, (CHUNK, CHUNK), 1))
        low = lax.broadcasted_iota(jnp.int32, (CHUNK, 128), 1) < HEAD_DIM
        cols = (dt_ref[0], ac_ref[0], ap_ref[0], ae_ref[0], ar_ref[0])
        bcs = [b_ref[0, :, g * 128:(g + 1) * 128] for g in range(2)]
        ccs = [c_ref[0, :, g * 128:(g + 1) * 128] for g in range(2)]
        ms = [_dot_nt(ccs[g], bcs[g]) for g in range(2)]
        for pi in range(HEAD_PAIRS):
            x2 = x_ref[0, :, pi * 128:(pi + 1) * 128]
            dt2 = jnp.where(low, cols[0][:, 2 * pi:2 * pi + 1], cols[0][:, 2 * pi + 1:2 * pi + 2])
            xdt = (x2 * dt2).astype(BF16)
            sprev = s_scr[pi]
            st_ref[0, 0, pi] = sprev
            spb = sprev.astype(BF16)
            ys, us = [], []
            for h in (2 * pi, 2 * pi + 1):
                g = h // 3
                _, ac, ap, ae, ar = _ssd_head(h, *cols)
                lm = jnp.exp(jnp.where(causal, ac - ar, NEG))
                gm = (ms[g] * lm).astype(BF16)
                ys.append(_dot(gm, xdt) + jnp.exp(ac - ap) * _dot(ccs[g], spb))
                bdec = (bcs[g].astype(F32) * jnp.exp(ae - ac)).astype(BF16)
                us.append(jnp.exp(ae[0:1, :] - ap[0:1, :]) * sprev + _dot_tn(bdec, xdt))
            y_ref[0, :, pi * 128:(pi + 1) * 128] = jnp.where(low, ys[0], ys[1])
            s_scr[pi] = jnp.where(low, us[0], us[1])

    return pl.pallas_call(
        body, name="ssd_fwd", grid=(nb, nc),
        out_shape=[jax.ShapeDtypeStruct(x.shape, F32),
                   jax.ShapeDtypeStruct((nb, nc, HEAD_PAIRS, SSD_STATE, 128), F32)],
        in_specs=[x_spec, bc_spec, bc_spec, col_spec, col_spec, col_spec, col_spec, row_spec],
        out_specs=[x_spec, st_spec],
        scratch_shapes=[pltpu.VMEM((HEAD_PAIRS, SSD_STATE, 128), F32)],
        compiler_params=_params(("parallel", "arbitrary")),
    )(x, bm, cm, dt_arr, cum, prev, end, a_row)


def _ssd_bwd(x_h, bm, cm, dt_arr, cum, prev, end, a_row, states, dy_h):
    nb, seq, _ = x_h.shape
    nc = seq // CHUNK
    x_spec, bc_spec, col_spec, row_spec, st_spec = _ssd_specs(nc, True)

    def body(x_ref, b_ref, c_ref, dt_ref, ac_ref, ap_ref, ae_ref, ar_ref, st_ref, dy_ref,
             dx_ref, db_ref, dc_ref, da_ref, dend_ref, ddt_ref, ds_scr):
        @pl.when(pl.program_id(1) == 0)
        def _():
            ds_scr[...] = jnp.zeros_like(ds_scr)

        causal = (lax.broadcasted_iota(jnp.int32, (CHUNK, CHUNK), 0)
                  >= lax.broadcasted_iota(jnp.int32, (CHUNK, CHUNK), 1))
        lane = lax.broadcasted_iota(jnp.int32, (CHUNK, 128), 1)
        low = lane < HEAD_DIM
        cols = (dt_ref[0], ac_ref[0], ap_ref[0], ae_ref[0], ar_ref[0])
        bcs = [b_ref[0, :, g * 128:(g + 1) * 128] for g in range(2)]
        ccs = [c_ref[0, :, g * 128:(g + 1) * 128] for g in range(2)]
        ms = [_dot_nt(ccs[g], bcs[g]) for g in range(2)]
        dms = [jnp.zeros((CHUNK, CHUNK), F32) for _ in range(2)]
        dc_accs = [jnp.zeros((CHUNK, SSD_STATE), F32) for _ in range(2)]
        db_accs = [jnp.zeros((CHUNK, SSD_STATE), F32) for _ in range(2)]
        da_blk = jnp.zeros((CHUNK, 128), F32)
        dend_blk = jnp.zeros((CHUNK, 128), F32)
        ddt_blk = jnp.zeros((CHUNK, 128), F32)
        for pi in range(HEAD_PAIRS):
            x2 = x_ref[0, :, pi * 128:(pi + 1) * 128]
            dy2 = dy_ref[0, :, pi * 128:(pi + 1) * 128]
            dt2 = jnp.where(low, cols[0][:, 2 * pi:2 * pi + 1], cols[0][:, 2 * pi + 1:2 * pi + 2])
            xdt_f = x2 * dt2
            xdt = xdt_f.astype(BF16)
            dyb = dy2.astype(BF16)
            dsn = ds_scr[pi]
            dsb = dsn.astype(BF16)
            sprev_f = st_ref[0, 0, pi]
            sprev = sprev_f.astype(BF16)
            dxdts, dss = [], []
            for h in (2 * pi, 2 * pi + 1):
                g = h // 3
                mine = low if h % 2 == 0 else jnp.logical_not(low)
                _, ac, ap, ae, ar = _ssd_head(h, *cols)
                bc, cc, m = bcs[g], ccs[g], ms[g]
                lm = jnp.exp(jnp.where(causal, ac - ar, NEG))
                gm = (m * lm).astype(BF16)
                dy_m = jnp.where(mine, dy2, 0.0)
                dyb_m = dy_m.astype(BF16)
                xdt_m = jnp.where(mine, xdt_f, 0.0)
                e_in = jnp.exp(ac - ap)
                f_out = jnp.exp(ae - ac)
                whole = jnp.exp(ae[0:1, :] - ap[0:1, :])
                dg = _dot_nt(dyb_m, xdt)
                dxdt_off = f_out * _dot(bc, dsb)
                dxdt = _dot_tn(gm, dyb) + dxdt_off
                dmj = dg * lm
                dms[g] = dms[g] + dmj
                dc_accs[g] = dc_accs[g] + e_in * _dot_nt(dyb_m, sprev)
                db_accs[g] = db_accs[g] + f_out * _dot_nt(xdt_m.astype(BF16), dsb)
                dss.append(whole * dsn + _dot_tn((cc.astype(F32) * e_in).astype(BF16), dyb))
                wmat = dmj * m
                r_in = jnp.sum(dy_m * (e_in * _dot(cc, sprev)), axis=1, keepdims=True)
                q_out = jnp.sum(xdt_m * dxdt_off, axis=1, keepdims=True)
                daj = (jnp.sum(wmat, axis=1, keepdims=True) - jnp.sum(wmat.T, axis=1, keepdims=True)
                       + r_in - q_out)
                cross = jnp.where(mine, dsn * sprev_f, 0.0)
                dendj = (jnp.sum(q_out, axis=0, keepdims=True)
                         + whole * jnp.sum(jnp.sum(cross, axis=1, keepdims=True), axis=0, keepdims=True))
                ddtj = jnp.sum(jnp.where(mine, dxdt * x2, 0.0), axis=1, keepdims=True)
                dxdts.append(dxdt)
                da_blk = jnp.where(lane == h, daj, da_blk)
                dend_blk = jnp.where(lane == h, dendj, dend_blk)
                ddt_blk = jnp.where(lane == h, ddtj, ddt_blk)
            dx_ref[0, :, pi * 128:(pi + 1) * 128] = jnp.where(low, dxdts[0], dxdts[1]) * dt2
            ds_scr[pi] = jnp.where(low, dss[0], dss[1])
        for g in range(2):
            dmb = dms[g].astype(BF16)
            dc_ref[0, :, g * 128:(g + 1) * 128] = dc_accs[g] + _dot(dmb, bcs[g])
            db_ref[0, :, g * 128:(g + 1) * 128] = db_accs[g] + _dot_tn(dmb, ccs[g])
        da_ref[0] = da_blk
        dend_ref[0] = dend_blk
        ddt_ref[0] = ddt_blk

    col_shape = jax.ShapeDtypeStruct((nb, seq, 128), F32)
    return pl.pallas_call(
        body, name="ssd_bwd", grid=(nb, nc),
        out_shape=[jax.ShapeDtypeStruct(x_h.shape, F32),
                   jax.ShapeDtypeStruct((nb, seq, 256), F32), jax.ShapeDtypeStruct((nb, seq, 256), F32),
                   col_shape, col_shape, col_shape],
        in_specs=[x_spec, bc_spec, bc_spec, col_spec, col_spec, col_spec, col_spec, row_spec, st_spec, x_spec],
        out_specs=[x_spec, bc_spec, bc_spec, col_spec, col_spec, col_spec],
        scratch_shapes=[pltpu.VMEM((HEAD_PAIRS, SSD_STATE, 128), F32)],
        compiler_params=_params(("parallel", "arbitrary")),
    )(x_h, bm, cm, dt_arr, cum, prev, end, a_row, states, dy_h)


def _lru_gates(xl, wa, ba, wx, bx, lam):
    xb = xl.astype(BF16)
    r = _sigmoid(_dot(xb, wa) + ba)
    i = _sigmoid(_dot(xb, wx) + bx)
    sp = _softplus(-lam)
    log_a = -LRU_C * r * sp
    a = jnp.exp(log_a)
    mult = jnp.sqrt(_neg_expm1(2.0 * log_a))
    return r, i, sp, log_a, a, mult


def _scan_chunks(a_ref, u_ref, h_ref, seq, reverse):
    nc = seq // CHUNK
    width = a_ref.shape[1]
    row = lax.broadcasted_iota(jnp.int32, (CHUNK, width), 0)

    def chunk(ci, carry):
        c = nc - 1 - ci if reverse else ci
        rows = pl.ds(pl.multiple_of(c * CHUNK, CHUNK), CHUNK)
        av, bv = a_ref[rows, :], u_ref[rows, :]
        d = 1
        while d < CHUNK:
            if reverse:
                keep = row < CHUNK - d
                a_sh = jnp.where(keep, pltpu.roll(av, CHUNK - d, 0), 1.0)
                b_sh = jnp.where(keep, pltpu.roll(bv, CHUNK - d, 0), 0.0)
            else:
                keep = row >= d
                a_sh = jnp.where(keep, pltpu.roll(av, d, 0), 1.0)
                b_sh = jnp.where(keep, pltpu.roll(bv, d, 0), 0.0)
            bv = av * b_sh + bv
            av = av * a_sh
            d *= 2
        hv = bv + av * carry
        h_ref[rows, :] = hv
        return hv[0:1, :] if reverse else hv[CHUNK - 1:CHUNK, :]

    lax.fori_loop(0, nc, chunk, jnp.zeros((1, width), F32))


def _lru_fwd(proj, seq, cw, cb, wa, ba, wx, bx, lam):
    def fn(raw, cw, cb, wa, ba, wx, bx, lam, a_scr, u_scr, h_scr):
        xl = _conv(raw, cw, cb)
        r, i, sp, log_a, a, mult = _lru_gates(xl, wa, ba, wx, bx, lam)
        a_scr[...] = a
        u_scr[...] = mult * (i * xl)
        _scan_chunks(a_scr, u_scr, h_scr, seq, reverse=False)
        return h_scr[...], xl

    return _rowcall("lru_fwd", fn, [(proj, 256, 2)], [cw, cb, wa, ba, wx, bx, lam],
                    [(256, F32), (256, F32)], tile=seq,
                    scratch=[pltpu.VMEM((seq, 256), F32)] * 3)


def _lru_bwd(proj, seq, xl_all, h_all, dh_all, cw, cb, wa, ba, wx, bx, lam):
    def fn(raw, xl, hseq, dh, cw, cb, wa, ba, wx, bx, lam, a_scr, u_scr, h_scr):
        r, i, sp, log_a, a, mult = _lru_gates(xl, wa, ba, wx, bx, lam)
        a_scr[...] = _shift_up(a, 1)
        u_scr[...] = dh
        _scan_chunks(a_scr, u_scr, h_scr, seq, reverse=True)
        dht = h_scr[...]
        da = dht * _shift_down(hseq, 1)
        gated = i * xl
        dgated = dht * mult
        dmult = dht * gated
        dlog_a = da * a - dmult * (a * a) / mult
        dr = dlog_a * (-LRU_C * sp)
        dsp = jnp.sum(dlog_a * (-LRU_C * r), axis=0, keepdims=True)
        dlam = -dsp * _sigmoid(-lam)
        dpa = dr * r * (1.0 - r)
        dpx = (dgated * xl) * i * (1.0 - i)
        dpa_b, dpx_b = dpa.astype(BF16), dpx.astype(BF16)
        dxl = dgated * i + _dot_nt(dpa_b, wa) + _dot_nt(dpx_b, wx)
        xb = xl.astype(BF16)
        dwa = _dot_tn(xb, dpa_b)
        dwx = _dot_tn(xb, dpx_b)
        draw, dcw, dcb = _conv_bwd(raw, cw, dxl)
        return (draw, dcw, dcb, dwa, jnp.sum(dpa, axis=0, keepdims=True), dwx,
                jnp.sum(dpx, axis=0, keepdims=True), dlam)

    return _rowcall("lru_bwd", fn, [(proj, 256, 2), xl_all, h_all, dh_all], [cw, cb, wa, ba, wx, bx, lam],
                    [(256, BF16)],
                    [((4, 256), F32), ((1, 256), F32), ((256, 256), F32), ((1, 256), F32), ((256, 256), F32),
                     ((1, 256), F32), ((1, 256), F32)],
                    tile=seq, scratch=[pltpu.VMEM((seq, 256), F32)] * 3)


FOX_SCALE = HEAD_DIM ** -0.5
FOX_BLOCK = 512


class _Side:
    def __init__(self, inputs, out_shape, sems, build):
        self.inputs, self.out_shape, self.sems, self.build = list(inputs), list(out_shape), list(sems), build

    def specs(self):
        any_spec = pl.BlockSpec(memory_space=pl.ANY)
        return [any_spec] * len(self.inputs), [any_spec] * len(self.out_shape)

    def run(self, refs, first, last):
        n_in, n_out = len(self.inputs), len(self.out_shape)
        in_refs, out_refs, sem_refs = refs[:n_in], refs[n_in:n_in + n_out], refs[n_in + n_out:]

        @pl.when(first)
        def _():
            for cp in self.build(in_refs, out_refs, sem_refs):
                cp.start()

        @pl.when(last)
        def _():
            for cp in self.build(in_refs, out_refs, sem_refs):
                cp.wait()


def _grid_ends(grid):
    ids = [pl.program_id(a) for a in range(len(grid))]
    first = functools.reduce(jnp.logical_and, [i == 0 for i in ids])
    last = functools.reduce(jnp.logical_and, [i == n - 1 for i, n in zip(ids, grid)])
    return first, last


def _fox_fwd(q, k, v, cum_col, cum_row, side=None):
    nb, nh, seq, e = q.shape
    tb = min(FOX_BLOCK, seq)
    nq = seq // tb
    grid = (nb, nh, nq)

    def body(*refs):
        q_ref, k_ref, v_ref, cq_ref, ck_ref = refs[:5]
        if side is None:
            o_ref, lse_ref = refs[5:7]
        else:
            n_in = len(side.inputs)
            o_ref, lse_ref = refs[5 + n_in:7 + n_in]
            side.run(refs[5:5 + n_in] + refs[7 + n_in:], *_grid_ends(grid))
        qi = pl.program_id(2)
        qv = q_ref[0, 0]
        cq = cq_ref[0, 0]

        def block(j, carry, diagonal):
            m_i, l_i, acc = carry
            cols = pl.ds(pl.multiple_of(j * tb, tb), tb)
            s = _dot_nt(qv, k_ref[0, 0, cols, :]) + cq - ck_ref[0, 0, :, cols]
            if diagonal:
                s = jnp.where(lax.broadcasted_iota(jnp.int32, (tb, tb), 0)
                              >= lax.broadcasted_iota(jnp.int32, (tb, tb), 1), s, NEG)
            m_new = jnp.maximum(m_i, jnp.max(s, axis=1, keepdims=True))
            p = jnp.exp(s - m_new)
            alpha = jnp.exp(m_i - m_new)
            l_new = alpha * l_i + jnp.sum(p, axis=1, keepdims=True)
            acc_new = alpha * acc + _dot(p.astype(BF16), v_ref[0, 0, cols, :])
            return m_new, l_new, acc_new

        init = (jnp.full((tb, 1), NEG, F32), jnp.zeros((tb, 1), F32), jnp.zeros((tb, e), F32))
        carry = lax.fori_loop(0, qi, lambda j, cr: block(j, cr, False), init)
        m_i, l_i, acc = block(qi, carry, True)
        o_ref[0, 0] = acc / l_i
        lse_ref[0, 0] = m_i + jnp.log(l_i)

    blk = pl.BlockSpec((1, 1, tb, e), lambda b, h, i: (b, h, i, 0))
    full = pl.BlockSpec((1, 1, seq, e), lambda b, h, i: (b, h, 0, 0))
    col_blk = pl.BlockSpec((1, 1, tb, 1), lambda b, h, i: (b, h, i, 0))
    row_full = pl.BlockSpec((1, 1, 1, seq), lambda b, h, i: (b, h, 0, 0))
    side_in, side_out = side.specs() if side else ([], [])
    return pl.pallas_call(
        body, name="fox_fwd", grid=grid,
        out_shape=[jax.ShapeDtypeStruct((nb, nh, seq, e), F32), jax.ShapeDtypeStruct((nb, nh, seq, 1), F32)]
        + (side.out_shape if side else []),
        in_specs=[blk, full, full, col_blk, row_full] + side_in,
        out_specs=[blk, col_blk] + side_out,
        scratch_shapes=side.sems if side else [],
        compiler_params=_params(("arbitrary", "arbitrary", "arbitrary")),
    )(q, k, v, cum_col, cum_row, *(side.inputs if side else []))


def _fox_bwd(q, k, v, o, lse, do, cum_col, cum_row, side=None):
    nb, nh, seq, e = q.shape
    tb = min(FOX_BLOCK, seq)
    nq = seq // tb
    grid = (nb, nh, nq)

    def body(*refs):
        q_ref, k_ref, v_ref, o_ref, lse_ref, do_ref, cq_ref, ck_ref = refs[:8]
        if side is None:
            dq_ref, dk_ref, dv_ref, dcum_ref, dcq_ref = refs[8:13]
        else:
            n_in = len(side.inputs)
            dq_ref, dk_ref, dv_ref, dcum_ref, dcq_ref = refs[8 + n_in:13 + n_in]
            side.run(refs[8:8 + n_in] + refs[13 + n_in:], *_grid_ends(grid))
        kj = pl.program_id(2)

        @pl.when(kj == 0)
        def _():
            dq_ref[...] = jnp.zeros_like(dq_ref)
            dcq_ref[...] = jnp.zeros_like(dcq_ref)

        kv = k_ref[0, 0]
        vv = v_ref[0, 0]
        ck = ck_ref[0, 0]

        def block(i, carry, diagonal):
            dk, dv, csum = carry
            rows = pl.ds(pl.multiple_of(i * tb, tb), tb)
            qv = q_ref[0, 0, rows, :]
            dov = do_ref[0, 0, rows, :]
            dob = dov.astype(BF16)
            delta = jnp.sum(dov * o_ref[0, 0, rows, :], axis=1, keepdims=True)
            s = _dot_nt(qv, kv) + cq_ref[0, 0, rows, :] - ck
            if diagonal:
                s = jnp.where(lax.broadcasted_iota(jnp.int32, (tb, tb), 0)
                              >= lax.broadcasted_iota(jnp.int32, (tb, tb), 1), s, NEG)
            p = jnp.exp(s - lse_ref[0, 0, rows, :])
            dp = _dot_nt(dob, vv)
            ds = p * (dp - delta)
            dsb = ds.astype(BF16)
            dv = dv + _dot_tn(p.astype(BF16), dob)
            dk = dk + _dot_tn(dsb, qv)
            dq_ref[0, 0, rows, :] += _dot(dsb, kv) * FOX_SCALE
            dcq_ref[0, 0, rows, :] += jnp.sum(ds, axis=1, keepdims=True)
            csum = csum + jnp.sum(ds, axis=0, keepdims=True)
            return dk, dv, csum

        init = (jnp.zeros((tb, e), F32), jnp.zeros((tb, e), F32), jnp.zeros((1, tb), F32))
        carry = block(kj, init, True)
        dk, dv, csum = lax.fori_loop(kj + 1, nq, lambda i, cr: block(i, cr, False), carry)
        dk_ref[0, 0] = dk
        dv_ref[0, 0] = dv
        dcum_ref[0, 0] = -csum

    blk = pl.BlockSpec((1, 1, tb, e), lambda b, h, j: (b, h, j, 0))
    full = pl.BlockSpec((1, 1, seq, e), lambda b, h, j: (b, h, 0, 0))
    col_full = pl.BlockSpec((1, 1, seq, 1), lambda b, h, j: (b, h, 0, 0))
    row_blk = pl.BlockSpec((1, 1, 1, tb), lambda b, h, j: (b, h, 0, j))
    side_in, side_out = side.specs() if side else ([], [])
    return pl.pallas_call(
        body, name="fox_bwd", grid=grid,
        out_shape=[jax.ShapeDtypeStruct((nb, nh, seq, e), F32), jax.ShapeDtypeStruct((nb, nh, seq, e), F32),
                   jax.ShapeDtypeStruct((nb, nh, seq, e), F32), jax.ShapeDtypeStruct((nb, nh, 1, seq), F32),
                   jax.ShapeDtypeStruct((nb, nh, seq, 1), F32)] + (side.out_shape if side else []),
        in_specs=[full, blk, blk, full, col_full, full, col_full, row_blk] + side_in,
        out_specs=[full, blk, blk, row_blk, col_full] + side_out,
        scratch_shapes=side.sems if side else [],
        compiler_params=_params(("arbitrary", "arbitrary", "arbitrary")),
    )(q, k, v, o, lse, do, cum_col, cum_row, *(side.inputs if side else []))


_ANY = pl.BlockSpec(memory_space=pl.ANY)


def _place():
    return lax.axis_index("x"), lax.axis_index("y"), lax.axis_index("c")


def _all_gather(shards, name):
    n = len(shards)

    def body(*refs):
        x_refs, out_refs = refs[:n], refs[n:2 * n]
        send_sems, recv_sems, local_sems = refs[2 * n:]
        x, y, c = _place()
        me, sibling = (x, y, c), (x, y, 1 - c)
        chips = [(1 - x, y), (x, 1 - y), (1 - x, 1 - y)]

        def rows(a, px, py, pc):
            return out_refs[a].at[4 * px + 2 * py + pc]

        def copy(a, k, block, to, src=None):
            return pltpu.make_async_remote_copy(
                src_ref=rows(a, *block) if src is None else src, dst_ref=rows(a, *block),
                send_sem=send_sems.at[a, k], recv_sem=recv_sems.at[a, k],
                device_id=to, device_id_type=pl.DeviceIdType.MESH)

        mine = [pltpu.make_async_copy(x_refs[a], rows(a, *me), local_sems.at[a]) for a in range(n)]
        for cp in mine:
            cp.start()
        first = []
        for a in range(n):
            first.append(copy(a, 0, me, sibling, src=x_refs[a]))
            first += [copy(a, 1 + j, me, (*chip, c), src=x_refs[a]) for j, chip in enumerate(chips)]
        for cp in first:
            cp.start()
        passed = []
        for j, chip in enumerate(chips):
            for a in range(n):
                copy(a, 1 + j, (*chip, c), me).wait_recv()
                passed.append(copy(a, 4 + j, (*chip, c), sibling))
                passed[-1].start()
        for a in range(n):
            copy(a, 0, sibling, me).wait_recv()
            for j, chip in enumerate(chips):
                copy(a, 4 + j, (*chip, 1 - c), me).wait_recv()
        for cp in first + passed:
            cp.wait_send()
        for cp in mine:
            cp.wait()

    return pl.pallas_call(
        body, name=name,
        out_shape=[jax.ShapeDtypeStruct((N_DEV,) + s.shape, s.dtype) for s in shards],
        in_specs=[_ANY] * n, out_specs=[_ANY] * n,
        scratch_shapes=[pltpu.SemaphoreType.DMA((n, 7)), pltpu.SemaphoreType.DMA((n, 7)),
                        pltpu.SemaphoreType.DMA((n,))],
    )(*shards)


def _remote(src, dst, send_sem, recv_sem, to):
    return pltpu.make_async_remote_copy(src_ref=src, dst_ref=dst, send_sem=send_sem, recv_sem=recv_sem,
                                        device_id=to, device_id_type=pl.DeviceIdType.MESH)


def _sem_pairs(n, k):
    return [pltpu.SemaphoreType.DMA((n, k)), pltpu.SemaphoreType.DMA((n, k))]


def _sibling_side(full):
    def build(g_refs, out_refs, sems):
        x, y, c = _place()
        return [_remote(g_refs[a].at[4 * (k // 2) + 2 * (k % 2) + (1 - c)], out_refs[a].at[k],
                        sems[0].at[a, k], sems[1].at[a, k], (x, y, 1 - c))
                for a in range(len(g_refs)) for k in range(4)]

    return _Side(full, [jax.ShapeDtypeStruct((4,) + f.shape[1:], f.dtype) for f in full],
                 _sem_pairs(len(full), 4), build)


def _chip_side(part):
    def build(p_refs, out_refs, sems):
        x, y, c = _place()
        peers = [(1 - x, y), (x, 1 - y), (1 - x, 1 - y)]
        return [_remote(p_refs[a].at[2 * px + py], out_refs[a].at[k], sems[0].at[a, k], sems[1].at[a, k],
                        (px, py, c))
                for a in range(len(p_refs)) for k, (px, py) in enumerate(peers)]

    return _Side(part, [jax.ShapeDtypeStruct((3,) + p.shape[1:], p.dtype) for p in part],
                 _sem_pairs(len(part), 3), build)


def _spread_side(shards):
    def build(x_refs, out_refs, sems):
        x, y, c = _place()
        targets = [(x, y, 1 - c), (1 - x, y, c), (x, 1 - y, c), (1 - x, 1 - y, c)]
        cps = []
        for a in range(len(x_refs)):
            slot = out_refs[a].at[4 * x + 2 * y + c]
            cps.append(pltpu.make_async_copy(x_refs[a], slot, sems[2].at[a]))
            cps += [_remote(x_refs[a], slot, sems[0].at[a, k], sems[1].at[a, k], to)
                    for k, to in enumerate(targets)]
        return cps

    n = len(shards)
    return _Side(shards, [jax.ShapeDtypeStruct((N_DEV,) + s.shape, s.dtype) for s in shards],
                 _sem_pairs(n, 4) + [pltpu.SemaphoreType.DMA((n,))], build)


def _pass_side(bufs):
    def build(in_refs, out_refs, sems):
        x, y, c = _place()
        chips = [(1 - x, y), (x, 1 - y), (1 - x, 1 - y)]
        return [_remote(in_refs[a].at[4 * px + 2 * py + c], out_refs[a].at[4 * px + 2 * py + c],
                        sems[0].at[a, j], sems[1].at[a, j], (x, y, 1 - c))
                for a in range(len(in_refs)) for j, (px, py) in enumerate(chips)]

    return _Side(bufs, [jax.ShapeDtypeStruct(b.shape, b.dtype) for b in bufs], _sem_pairs(len(bufs), 3), build)


def _run_side(side, name, in_place=False):
    n_in = len(side.inputs)

    def body(*refs):
        copies = side.build(refs[:n_in], refs[n_in:n_in + len(side.out_shape)],
                            refs[n_in + len(side.out_shape):])
        for cp in copies:
            cp.start()
        for cp in copies:
            cp.wait()

    in_specs, out_specs = side.specs()
    return pl.pallas_call(
        body, name=name, out_shape=side.out_shape, in_specs=in_specs, out_specs=out_specs,
        scratch_shapes=side.sems,
        input_output_aliases={a: a for a in range(n_in)} if in_place else {},
    )(*side.inputs)


def _pick_rows(rows, cap=512):
    t = cap
    while t >= 8:
        if rows % t == 0:
            return t
        t //= 2
    raise ValueError(rows)


def _pair_sum(full, got, name):
    _, rows, cols = full.shape
    tile = _pick_rows(rows, 256)
    c = lax.axis_index("c").astype(jnp.int32).reshape(1)

    def body(c_ref, a_ref, b_ref, o_ref):
        o_ref[...] = a_ref[...] + b_ref[...]

    blk = (1, tile, cols)
    return pl.pallas_call(
        body, name=name,
        grid_spec=pltpu.PrefetchScalarGridSpec(
            num_scalar_prefetch=1, grid=(4, rows // tile),
            in_specs=[pl.BlockSpec(blk, lambda k, i, c_ref: (4 * (k // 2) + 2 * (k % 2) + c_ref[0], i, 0)),
                      pl.BlockSpec(blk, lambda k, i, c_ref: (k, i, 0))],
            out_specs=pl.BlockSpec(blk, lambda k, i, c_ref: (k, i, 0))),
        out_shape=jax.ShapeDtypeStruct((4, rows, cols), full.dtype),
        compiler_params=_params(("arbitrary", "arbitrary")),
    )(c, full, got)


def _adam_math(w, g, m, v):
    c1 = 1.0 / (1.0 - ADAM_B1 ** ADAM_STEP)
    c2 = 1.0 / (1.0 - ADAM_B2 ** ADAM_STEP)
    m_new = ADAM_B1 * m + (1.0 - ADAM_B1) * g
    v_new = ADAM_B2 * v + (1.0 - ADAM_B2) * (g * g)
    delta = -ADAM_LR * ((m_new * c1) / (jnp.sqrt(v_new * c2) + ADAM_EPS) + ADAM_WD * w)
    return delta, m_new, v_new


def _sum_adamw(part, others, w, m, v, name):
    _, rows, cols = part.shape
    tile = _pick_rows(rows, 128)
    own = (2 * lax.axis_index("x") + lax.axis_index("y")).astype(jnp.int32).reshape(1)

    def body(own_ref, p_ref, o_ref, w_ref, m_ref, v_ref, g_out, d_out, m_out, v_out):
        g = ((p_ref[0] + o_ref[0]) + o_ref[1]) + o_ref[2]
        delta, m_new, v_new = _adam_math(w_ref[...], g, m_ref[...], v_ref[...])
        g_out[...] = g
        d_out[...] = delta
        m_out[...] = m_new
        v_out[...] = v_new

    flat = pl.BlockSpec((tile, cols), lambda i, own_ref: (i, 0))
    shape = jax.ShapeDtypeStruct((rows, cols), F32)
    return pl.pallas_call(
        body, name=name,
        grid_spec=pltpu.PrefetchScalarGridSpec(
            num_scalar_prefetch=1, grid=(rows // tile,),
            in_specs=[pl.BlockSpec((1, tile, cols), lambda i, own_ref: (own_ref[0], i, 0)),
                      pl.BlockSpec((3, tile, cols), lambda i, own_ref: (0, i, 0)), flat, flat, flat],
            out_specs=[flat] * 4),
        out_shape=[shape] * 4,
        compiler_params=_params(("arbitrary",)),
    )(own, part, others, w, m, v)


def _all_reduce_small(vec):
    gathered = _all_gather([vec], "ar_gather")[0]
    rows = vec.shape[0]

    def fn(*blocks):
        s = blocks[0]
        for b in blocks[1:]:
            s = s + b
        return s

    return _rowcall("ar_sum", fn, [gathered[j] for j in range(N_DEV)], [], [(1024, F32)],
                    tile=_pick_rows(rows))[0]


def _pad_rows(flat, mult):
    n = flat.shape[-1]
    per = mult * 1024
    padded = -(-n // per) * per
    pad = [(0, 0)] * (flat.ndim - 1) + [(0, padded - n)]
    return jnp.pad(flat, pad).reshape(flat.shape[:-1] + (padded // 1024, 1024))


def _regroup_w_in(w):
    pad = jnp.zeros((w.shape[0], 116), w.dtype)
    return jnp.concatenate([w[:, 768:1280], w[:, 1286:1798], w[:, 1280:1286], w[:, 2950:2956], pad,
                            w[:, 0:768], w[:, 1798:2950]], axis=1)


def _ungroup_w_in(wp):
    return jnp.concatenate([wp[:, 1152:1920], wp[:, 0:512], wp[:, 1024:1030], wp[:, 512:1024],
                            wp[:, 1920:3072], wp[:, 1030:1036]], axis=1)


def _to_shard(name, a):
    if name == 'w_in':
        return _regroup_w_in(a)
    if name in ('w_gate', 'w_up'):
        return jnp.pad(a, ((0, 0), (0, FF_SHARD_P - FF_SHARD)))
    if name == 'w_down':
        return jnp.pad(a, ((0, FF_SHARD_P - FF_SHARD), (0, 0)))
    return a


def _from_shard(name, a):
    if name == 'w_in':
        return _ungroup_w_in(a)
    if name in ('w_gate', 'w_up'):
        return a[:, 0:FF_SHARD]
    if name == 'w_down':
        return a[0:FF_SHARD, :]
    return a


def _whole(name, gathered):
    if BIG[name][0] == 1:
        return gathered.reshape(-1, gathered.shape[-1])
    return gathered.transpose(1, 0, 2).reshape(gathered.shape[1], -1)


def _split(name, whole):
    if BIG[name][0] == 1:
        return whole.reshape(N_DEV, whole.shape[0] // N_DEV, whole.shape[1])
    return whole.reshape(whole.shape[0], N_DEV, whole.shape[1] // N_DEV).transpose(1, 0, 2)


def _pack_list(arrays, mult):
    return _pad_rows(jnp.concatenate([a.reshape(-1) for a in arrays]), mult)


def _unpack_list(buf, shapes):
    flat = buf.reshape(-1)
    out, off = [], 0
    for s in shapes:
        n = math.prod(s)
        out.append(flat[off:off + n].reshape(s))
        off += n
    return out


def _adamw(w, g, m, v):
    return _rowcall("adamw", _adam_math, [w, g, m, v], [], [(1024, F32)] * 3, tile=_pick_rows(w.shape[0]))


def _to_heads(a, nb, seq):
    return a.reshape(nb, seq, -1, HEAD_DIM).transpose(0, 2, 1, 3)


def _from_heads(a):
    nb, nh, seq, e = a.shape
    return a.transpose(0, 2, 1, 3).reshape(nb * seq, nh * e)


def _block_diag(w):
    out = jnp.zeros((LRU_WIDTH, LRU_WIDTH), w.dtype)
    for g in range(4):
        out = lax.dynamic_update_slice(out, w[g], (64 * g, 64 * g))
    return out


def _block_diag_grad(full):
    return jnp.stack([full[64 * g:64 * (g + 1), 64 * g:64 * (g + 1)] for g in range(4)])


def _row(v):
    return v.reshape(1, -1).astype(F32)


def _lane128(*pieces):
    flat = jnp.concatenate([p.reshape(-1).astype(F32) for p in pieces])
    return jnp.pad(flat, (0, 128 - flat.shape[0])).reshape(1, 128)


def _layer_consts(w):
    c = {}
    cw, cb = w['ssd_conv_w'], w['ssd_conv_b']
    c['cw_x'], c['cw_b'], c['cw_c'] = cw[:, 0:384], cw[:, 384:640], cw[:, 640:896]
    c['cb_x'], c['cb_b'], c['cb_c'] = _row(cb[0:384]), _row(cb[384:640]), _row(cb[640:896])
    c['bias128'] = _lane128(w['ssd_dt_bias'], w['fox_b_f'])
    c['alog128'] = _lane128(w['ssd_a_log'])
    c['d384'] = _row(jnp.repeat(w['ssd_d'], HEAD_DIM))
    c['lcw'], c['lcb'] = w['lru_conv_w'], _row(w['lru_conv_b'])
    c['wa'], c['wx'] = _block_diag(w['lru_w_a']).astype(BF16), _block_diag(w['lru_w_x']).astype(BF16)
    c['ba'], c['bx'], c['lam'] = _row(w['lru_b_a']), _row(w['lru_b_x']), _row(w['lru_lambda'])
    return c


def _layer_fwd(h0, p_i, w, c, side=None):
    nb, seq = c['nb'], c['seq']

    u1 = _rowcall("norm1", lambda h, g: _rms(h, g), [h0], [_row(w['norm1_g'])], [(D_MODEL, BF16)])[0]
    proj = _matmul(u1, w['w_in'], 'nn', "proj")

    xs_c = _convsilu_fwd("conv_x", (proj, 384, 4), seq, c['cw_x'], c['cb_x'], F32)
    b_c = _convsilu_fwd("conv_b", (proj, 256, 0), seq, c['cw_b'], c['cb_b'], BF16)
    c_c = _convsilu_fwd("conv_c", (proj, 256, 1), seq, c['cw_c'], c['cb_c'], BF16)
    dt_arr, cum, prev, end = _small_fwd((proj, 128, 8), seq, c['bias128'], c['alog128'])
    x_h = xs_c.reshape(nb, seq, SSD_WIDTH)
    cum3 = cum.reshape(nb, seq, 128)
    a_row = cum3[:, :, 0:8].transpose(0, 2, 1)
    ssd_in = (x_h, b_c.reshape(nb, seq, 256), c_c.reshape(nb, seq, 256), dt_arr.reshape(nb, seq, 128), cum3,
              prev.reshape(nb, seq, 128), end.reshape(nb, seq, 128), a_row)
    y_h, states = _ssd_fwd(*ssd_in)
    y_core = y_h.reshape(nb * seq, SSD_WIDTH)

    hseq, xl = _lru_fwd(proj, seq, c['lcw'], c['lcb'], c['wa'], c['ba'], c['wx'], c['bx'], c['lam'])

    q = _to_heads((proj[:, 1920:2304] * FOX_SCALE).astype(BF16), nb, seq)
    k = _to_heads(proj[:, 2304:2688].astype(BF16), nb, seq)
    v = _to_heads(proj[:, 2688:3072].astype(BF16), nb, seq)
    cf = cum3[:, :, SSD_HEADS:2 * SSD_HEADS].transpose(0, 2, 1)
    cf_col, cf_row = cf[..., None], cf[:, :, None, :]
    o_h, lse, *side_out = _fox_fwd(q, k, v, cf_col, cf_row, side)
    y_fox = _from_heads(o_h)

    def post(yc, xs, z, hs, lg, yf, d, g1, g2, g3):
        y1 = _rms((yc + xs * d) * _silu(z), g1)
        y2 = _rms(hs * _gelu(lg), g2)
        y3 = _rms(yf, g3)
        return jnp.concatenate([y1, y2, y3], axis=-1)

    post_consts = [c['d384'], _row(w['ssd_norm_g']), _row(w['lru_norm_g']), _row(w['fox_norm_g'])]
    ycat = _rowcall("mix_post", post, [y_core, xs_c, (proj, 384, 3), hseq, (proj, 256, 3), y_fox], post_consts,
                    [(D_MODEL, BF16)])[0]
    mix = _matmul(ycat, w['w_out'], 'nn', "mix_out")

    def res_norm(h, d, g):
        hn = h + d
        return hn, _rms(hn, g)

    h1, u2 = _rowcall("res_norm2", res_norm, [h0, mix], [_row(w['norm2_g'])], [(D_MODEL, F32), (D_MODEL, BF16)])
    gu = _matmul(u2, w['w_gu'], 'nn', "ffn_in")
    act = _rowcall("swiglu", lambda gt, up: _silu(gt) * up, [(gu, D_FF_P, 0), (gu, D_FF_P, 1)], [], [(D_FF_P, BF16)])[0]
    ff = _matmul(act, w['w_down'], 'nn', "ffn_out")
    h2, u3 = _rowcall("res_norm3", res_norm, [h1, ff], [_row(w['norm3_g'])], [(D_MODEL, F32), (D_MODEL, BF16)])
    pg = _matmul(u3, w['w_ple_gate'], 'nn', "ple_gate")
    pp = _matmul(p_i, w['w_ple_proj'], 'nn', "ple_proj")
    h3 = _rowcall("ple", lambda h, a, b, bias: h + _sigmoid(a + bias) * b, [h2, pg, pp], [_row(w['b_ple_gate'])],
                  [(D_MODEL, F32)])[0]
    saved = dict(h0=h0, u1=u1, proj=proj, xs_c=xs_c, dt_arr=dt_arr, ssd_in=ssd_in, states=states,
                 y_core=y_core, hseq=hseq, xl=xl, q=q, k=k, v=v, cf_col=cf_col, cf_row=cf_row, o_h=o_h, lse=lse,
                 y_fox=y_fox, post_consts=post_consts, ycat=ycat, h1=h1, u2=u2, gu=gu, act=act, h2=h2, u3=u3,
                 pg=pg, pp=pp, p_i=p_i, side_out=side_out)
    return h3, saved


def _layer_bwd(dh3, s, w, c, side=None):
    nb, seq = c['nb'], c['seq']
    g = {}

    def ple_bwd(dh, a, b, bias):
        gate = _sigmoid(a + bias)
        dpg = dh * b * gate * (1.0 - gate)
        return dh * gate, dpg, jnp.sum(dpg, axis=0, keepdims=True)

    dpp, dpg, g['b_ple_gate'] = _rowcall("ple_bwd", ple_bwd, [dh3, s['pg'], s['pp']], [_row(w['b_ple_gate'])],
                                         [(D_MODEL, BF16), (D_MODEL, BF16)], [((1, D_MODEL), F32)])
    g['w_ple_proj'] = _matmul(s['p_i'], dpp, 'tn', "d_w_ple_proj")
    g['w_ple_gate'] = _matmul(s['u3'], dpg, 'tn', "d_w_ple_gate")
    du3 = _matmul(dpg, w['w_ple_gate'], 'nt', "d_u3")

    def norm_bwd(h, du, dh, gain):
        dx, dg = _rms_bwd(h, gain, du)
        dhn = dh + dx
        return dhn, dhn, dg

    dh2, dh2_b, g['norm3_g'] = _rowcall("norm3_bwd", norm_bwd, [s['h2'], du3, dh3], [_row(w['norm3_g'])],
                                        [(D_MODEL, F32), (D_MODEL, BF16)], [((1, D_MODEL), F32)])
    g['w_down'] = _matmul(s['act'], dh2_b, 'tn', "d_w_down")
    dact = _matmul(dh2_b, w['w_down'], 'nt', "d_act")

    def swiglu_bwd(gt, up, da):
        return jnp.concatenate([da * up * _dsilu(gt), da * _silu(gt)], axis=-1)

    dgu = _rowcall("swiglu_bwd", swiglu_bwd, [(s['gu'], D_FF_P, 0), (s['gu'], D_FF_P, 1), dact], [],
                   [(2 * D_FF_P, BF16)])[0]
    g['w_gu'] = _matmul(s['u2'], dgu, 'tn', "d_w_gu")
    du2 = _matmul(dgu, w['w_gu'], 'nt', "d_u2")
    dh1, dh1_b, g['norm2_g'] = _rowcall("norm2_bwd", norm_bwd, [s['h1'], du2, dh2], [_row(w['norm2_g'])],
                                        [(D_MODEL, F32), (D_MODEL, BF16)], [((1, D_MODEL), F32)])
    g['w_out'] = _matmul(s['ycat'], dh1_b, 'tn', "d_w_out")
    dycat = _matmul(dh1_b, w['w_out'], 'nt', "d_ycat")

    def post_bwd(dy, yc, xs, z, hs, lg, yf, d, g1, g2, g3):
        sz = _silu(z)
        ytot = yc + xs * d
        dpre1, dg1 = _rms_bwd(ytot * sz, g1, dy[:, 0:384])
        dytot = dpre1 * sz
        dz = dpre1 * ytot * _dsilu(z)
        dd = jnp.sum(dytot * xs, axis=0, keepdims=True)
        gl = _gelu(lg)
        dpre2, dg2 = _rms_bwd(hs * gl, g2, dy[:, 384:640])
        dyf, dg3 = _rms_bwd(yf, g3, dy[:, 640:1024])
        return dytot, dytot * d, dz, dpre2 * gl, dpre2 * hs * _dgelu(lg), dyf, dd, dg1, dg2, dg3

    (dy_core, dxs_skip, dz, dhseq, dlg, dy_fox, dd384, g['ssd_norm_g'], g['lru_norm_g'], g['fox_norm_g']) = _rowcall(
        "mix_post_bwd", post_bwd,
        [dycat, s['y_core'], s['xs_c'], (s['proj'], 384, 3), s['hseq'], (s['proj'], 256, 3), s['y_fox']],
        s['post_consts'],
        [(384, F32), (384, F32), (384, BF16), (256, F32), (256, BF16), (384, F32)],
        [((1, 384), F32), ((1, 384), F32), ((1, 256), F32), ((1, 384), F32)])
    g['ssd_d'] = dd384.reshape(SSD_HEADS, HEAD_DIM).sum(axis=1)

    do_h = _to_heads(dy_fox, nb, seq)
    dq_h, dk_h, dv_h, dcf_row, dcf_col, *side_out = _fox_bwd(s['q'], s['k'], s['v'], s['o_h'], s['lse'], do_h,
                                                             s['cf_col'], s['cf_row'], side)
    dq, dk, dv = (_from_heads(t).astype(BF16) for t in (dq_h, dk_h, dv_h))

    dx_h, db_c, dc_c, da_arr, dend_arr, ddt_arr = _ssd_bwd(*s['ssd_in'], s['states'],
                                                           dy_core.reshape(nb, seq, SSD_WIDTH))
    dxs_c = dx_h.reshape(nb * seq, SSD_WIDTH) + dxs_skip
    dcf = (dcf_row.reshape(nb, FOX_HEADS, seq) + dcf_col.reshape(nb, FOX_HEADS, seq)).transpose(0, 2, 1)
    dcum = jnp.concatenate([da_arr[:, :, 0:SSD_HEADS], dcf,
                            jnp.zeros((nb, seq, 128 - 2 * SSD_HEADS), F32)], axis=-1).reshape(nb * seq, 128)
    proj = s['proj']
    dxs_raw, dcw_x, dcb_x = _convsilu_bwd("conv_x_bwd", (proj, 384, 4), seq, c['cw_x'], c['cb_x'], dxs_c)
    db_raw, dcw_b, dcb_b = _convsilu_bwd("conv_b_bwd", (proj, 256, 0), seq, c['cw_b'], c['cb_b'],
                                         db_c.reshape(nb * seq, 256))
    dc_raw, dcw_c, dcb_c = _convsilu_bwd("conv_c_bwd", (proj, 256, 1), seq, c['cw_c'], c['cb_c'],
                                         dc_c.reshape(nb * seq, 256))
    dsmall, dbias128, dalog128 = _small_bwd((proj, 128, 8), seq, dcum, dend_arr.reshape(nb * seq, 128),
                                            ddt_arr.reshape(nb * seq, 128), s['dt_arr'], c['bias128'], c['alog128'])
    g['ssd_conv_w'] = jnp.concatenate([dcw_x, dcw_b, dcw_c], axis=1)
    g['ssd_conv_b'] = jnp.concatenate([dcb_x, dcb_b, dcb_c], axis=1).reshape(-1)
    g['ssd_dt_bias'] = dbias128[0, 0:SSD_HEADS]
    g['fox_b_f'] = dbias128[0, SSD_HEADS:2 * SSD_HEADS]
    g['ssd_a_log'] = dalog128[0, 0:SSD_HEADS]

    (dlru_raw, g['lru_conv_w'], dlcb, dwa, dba, dwx, dbx, dlam) = _lru_bwd(
        s['proj'], seq, s['xl'], s['hseq'], dhseq, c['lcw'], c['lcb'], c['wa'], c['ba'], c['wx'], c['bx'], c['lam'])
    g['lru_conv_b'], g['lru_b_a'], g['lru_b_x'], g['lru_lambda'] = (t.reshape(-1) for t in (dlcb, dba, dbx, dlam))
    g['lru_w_a'], g['lru_w_x'] = _block_diag_grad(dwa), _block_diag_grad(dwx)

    dproj = jnp.concatenate([db_raw, dc_raw, dlru_raw, dlg, dsmall, dz, dxs_raw, dq, dk, dv], axis=1)
    g['w_in'] = _matmul(s['u1'], dproj, 'tn', "d_w_in")
    du1 = _matmul(dproj, w['w_in'], 'nt', "d_u1")

    def norm1_bwd(h, du, dh, gain):
        dx, dg = _rms_bwd(h, gain, du)
        return dh + dx, dg

    dh0, g['norm1_g'] = _rowcall("norm1_bwd", norm1_bwd, [s['h0'], du1, dh1], [_row(w['norm1_g'])],
                                 [(D_MODEL, F32)], [((1, D_MODEL), F32)])
    for name in ('b_ple_gate', 'norm3_g', 'norm2_g', 'norm1_g', 'ssd_norm_g', 'lru_norm_g', 'fox_norm_g'):
        g[name] = g[name].reshape(-1)
    wgu = g.pop('w_gu')
    g['w_gate'], g['w_up'] = wgu[:, 0:D_FF_P], wgu[:, D_FF_P:2 * D_FF_P]
    return dh0, g, side_out


def _local_step(x, p, target, big, small, late_weights=None, early_reduce=None):
    nb, seq, _ = x.shape
    tokens = nb * seq
    h = x.reshape(tokens, D_MODEL)
    layers, saves = [], []
    for i in range(DEPTH):
        if i == 1 and late_weights is not None:
            arrived = late_weights[1](saves[0]['side_out'])
            big = {name: [big[name][0], arrived[name]] for name in big}
        w = {name: small[name][i] for name in small if name != 'final_norm_g'}
        for name in ('w_in', 'w_out', 'w_down', 'w_ple_gate', 'w_ple_proj'):
            w[name] = big[name][i]
        w['w_gu'] = jnp.concatenate([big['w_gate'][i], big['w_up'][i]], axis=1)
        c = _layer_consts(w)
        c['nb'], c['seq'] = nb, seq
        side = late_weights[0] if (i == 0 and late_weights is not None) else None
        h, s = _layer_fwd(h, p[i].reshape(tokens, PLE_DIM).astype(BF16), w, c, side)
        layers.append((w, c))
        saves.append(s)

    def head(hf, tgt, gain):
        r = lax.rsqrt(jnp.mean(hf * hf, axis=-1, keepdims=True) + EPS)
        xhat = hf * r
        err = xhat * gain - tgt
        loss = 0.5 * jnp.sum(jnp.mean(err * err, axis=-1, keepdims=True), axis=0, keepdims=True)
        dy = err * (1.0 / D_MODEL)
        dg = jnp.sum(dy * xhat, axis=0, keepdims=True)
        dxhat = dy * gain
        dh = r * (dxhat - xhat * jnp.mean(dxhat * xhat, axis=-1, keepdims=True))
        return dh, jnp.broadcast_to(loss, (1, 128)), dg

    dh, loss128, dgf = _rowcall("loss_head", head, [h, target.reshape(tokens, D_MODEL)],
                                [_row(small['final_norm_g'])], [(D_MODEL, F32)],
                                [((1, 128), F32), ((1, D_MODEL), F32)])
    grads = {'final_norm_g': dgf.reshape(-1)}
    per_layer = [None] * DEPTH
    carried = []
    for i in range(DEPTH - 1, -1, -1):
        w, c = layers[i]
        side = early_reduce(per_layer[1]) if (i == 0 and early_reduce is not None) else None
        dh, per_layer[i], side_out = _layer_bwd(dh, saves[i], w, c, side)
        carried = side_out if i == 0 else carried
    for name in per_layer[0]:
        if name in BIG:
            grads[name] = [per_layer[i][name] for i in range(DEPTH)]
        else:
            grads[name] = jnp.stack([per_layer[i][name] for i in range(DEPTH)])
    return loss128[0, 0], dh.reshape(nb, seq, D_MODEL), grads, carried


def kernel(x, p, norm1_g, w_in, ssd_conv_w, ssd_conv_b, ssd_dt_bias, ssd_a_log, ssd_d, ssd_norm_g, lru_conv_w, lru_conv_b, lru_w_a, lru_b_a, lru_w_x, lru_b_x, lru_lambda, lru_norm_g, fox_b_f, fox_norm_g, w_out, norm2_g, w_gate, w_up, w_down, norm3_g, w_ple_gate, b_ple_gate, w_ple_proj, final_norm_g, loss_target, m_norm1_g, m_w_in, m_ssd_conv_w, m_ssd_conv_b, m_ssd_dt_bias, m_ssd_a_log, m_ssd_d, m_ssd_norm_g, m_lru_conv_w, m_lru_conv_b, m_lru_w_a, m_lru_b_a, m_lru_w_x, m_lru_b_x, m_lru_lambda, m_lru_norm_g, m_fox_b_f, m_fox_norm_g, m_w_out, m_norm2_g, m_w_gate, m_w_up, m_w_down, m_norm3_g, m_w_ple_gate, m_b_ple_gate, m_w_ple_proj, m_final_norm_g, v_norm1_g, v_w_in, v_ssd_conv_w, v_ssd_conv_b, v_ssd_dt_bias, v_ssd_a_log, v_ssd_d, v_ssd_norm_g, v_lru_conv_w, v_lru_conv_b, v_lru_w_a, v_lru_b_a, v_lru_w_x, v_lru_b_x, v_lru_lambda, v_lru_norm_g, v_fox_b_f, v_fox_norm_g, v_w_out, v_norm2_g, v_w_gate, v_w_up, v_w_down, v_norm3_g, v_w_ple_gate, v_b_ple_gate, v_w_ple_proj, v_final_norm_g):
    args = dict(locals())
    w_loc = {n: args[n] for n in WEIGHTS}
    m_loc = {n: args['m_' + n] for n in WEIGHTS}
    v_loc = {n: args['v_' + n] for n in WEIGHTS}
    dev = 4 * lax.axis_index("x") + 2 * lax.axis_index("y") + lax.axis_index("c")

    keys = [(n, i) for n in BIG for i in range(DEPTH)]
    conv_names = list(CONV_SHARDED)
    conv_loc_shapes = [w_loc[n].shape for n in conv_names]
    shard = {(n, i): _to_shard(n, w_loc[n][i]).astype(BF16) for n, i in keys}
    gathered = _all_gather([shard[n, 0] for n in BIG] + [_pack_list([w_loc[n] for n in conv_names], 8)],
                           "gather_w0")
    big = {n: [_whole(n, arr), None] for n, arr in zip(BIG, gathered)}
    small = {n: w_loc[n] for n in WEIGHTS if n not in BIG and n not in CONV_SHARDED}
    per_dev = [_unpack_list(gathered[-1][j], conv_loc_shapes) for j in range(N_DEV)]
    for idx, n in enumerate(conv_names):
        small[n] = jnp.concatenate([per_dev[j][idx] for j in range(N_DEV)], axis=2)

    def layer1_weights(spread):
        passed = _run_side(_pass_side(spread), "gather_w1_pass", in_place=True)
        return {n: _whole(n, arr) for n, arr in zip(BIG, passed)}

    part = {}

    def pair_sums(layer, grads_of_layer):
        full = [_split(n, grads_of_layer[n]) for n in BIG]
        got = _run_side(_sibling_side(full), "rs_sibling%d" % layer)
        part[layer] = [_pair_sum(f, r, "rs_pair_sum_%s%d" % (n, layer)) for n, f, r in zip(BIG, full, got)]
        return _chip_side(part[layer])

    loss_part, dx, grads, others1 = _local_step(
        x, p, loss_target, big, small,
        late_weights=(_spread_side([shard[n, 1] for n in BIG]), layer1_weights),
        early_reduce=lambda g1: pair_sums(1, g1))
    loss = lax.psum(loss_part, ("x", "y", "c"))
    others0 = _run_side(pair_sums(0, {n: grads[n][0] for n in BIG}), "rs_chips0")
    part = [part[i][j] for j in range(len(BIG)) for i in range(DEPTH)]
    others = [(others0, others1)[i][j] for j in range(len(BIG)) for i in range(DEPTH)]
    out = {kind: {n: [None] * DEPTH for n in BIG} for kind in ('g', 'delta', 'm', 'v')}
    for (n, i), pt, ot in zip(keys, part, others):
        res = _sum_adamw(pt, ot, *[_to_shard(n, d[n][i]) for d in (w_loc, m_loc, v_loc)], "sum_adamw_%s%d" % (n, i))
        for kind, r in zip(('g', 'delta', 'm', 'v'), res):
            out[kind][n][i] = _from_shard(n, r)

    small_names = [n for n in WEIGHTS if n not in BIG]
    small_shapes = [grads[n].shape for n in small_names]
    g_small = dict(zip(small_names, _unpack_list(
        _all_reduce_small(_pack_list([grads[n] for n in small_names], 8)), small_shapes)))
    for n in CONV_SHARDED:
        width = CONV_SHARDED[n][2] // N_DEV
        g_small[n] = lax.dynamic_slice_in_dim(g_small[n], dev * width, width, axis=2)
    shapes = [w_loc[n].shape for n in small_names]
    packed = [_pack_list([d[n] for n in small_names], 8) for d in (w_loc, g_small, m_loc, v_loc)]
    upd = [dict(zip(small_names, _unpack_list(t, shapes))) for t in _adamw(*packed)]
    for kind, d in zip(('g', 'delta', 'm', 'v'), [g_small] + upd):
        for n in small_names:
            out[kind][n] = d[n]
        for n in BIG:
            out[kind][n] = jnp.stack(out[kind][n])
    return (loss, dx, *[out['g'][n] for n in WEIGHTS], *[out['delta'][n] for n in WEIGHTS],
            *[out['m'][n] for n in WEIGHTS], *[out['v'][n] for n in WEIGHTS])
```

```python
import functools
import math

import jax
import jax.numpy as jnp
from jax import lax
from jax.experimental import pallas as pl
from jax.experimental.pallas import tpu as pltpu

F32 = jnp.float32
BF16 = jnp.bfloat16

N_DEV = 8
D_MODEL = 1024
DEPTH = 2
HEAD_DIM = 64
SSD_WIDTH = 384
LRU_WIDTH = 256
FOX_WIDTH = 384
SSD_HEADS = 6
SSD_STATE = 128
CHUNK = 128
FOX_HEADS = 6
D_FF = 2816
FF_SHARD = D_FF // N_DEV
FF_SHARD_P = 384
D_FF_P = N_DEV * FF_SHARD_P
PLE_DIM = 256
IN_COLS = 2956
PROJ_COLS = 3072
LRU_C = 8.0
EPS = 1e-6
NEG = -1e30

ADAM_LR = 0.001
ADAM_B1 = 0.9
ADAM_B2 = 0.999
ADAM_EPS = 1e-08
ADAM_WD = 0.01
ADAM_STEP = 10

VMEM_LIMIT = 56 * 1024 * 1024

WEIGHTS = ['norm1_g', 'w_in', 'ssd_conv_w', 'ssd_conv_b', 'ssd_dt_bias', 'ssd_a_log', 'ssd_d', 'ssd_norm_g',
           'lru_conv_w', 'lru_conv_b', 'lru_w_a', 'lru_b_a', 'lru_w_x', 'lru_b_x', 'lru_lambda', 'lru_norm_g',
           'fox_b_f', 'fox_norm_g', 'w_out', 'norm2_g', 'w_gate', 'w_up', 'w_down', 'norm3_g', 'w_ple_gate',
           'b_ple_gate', 'w_ple_proj', 'final_norm_g']
BIG = {'w_in': (1, (DEPTH, D_MODEL, IN_COLS)), 'w_out': (1, (DEPTH, D_MODEL, D_MODEL)),
       'w_gate': (2, (DEPTH, D_MODEL, D_FF)), 'w_up': (2, (DEPTH, D_MODEL, D_FF)),
       'w_down': (1, (DEPTH, D_FF, D_MODEL)), 'w_ple_gate': (1, (DEPTH, D_MODEL, D_MODEL)),
       'w_ple_proj': (2, (DEPTH, PLE_DIM, D_MODEL))}
CONV_SHARDED = {'ssd_conv_w': (DEPTH, 4, 896), 'lru_conv_w': (DEPTH, 4, 256)}


def _dot(a, b):
    return jnp.dot(a, b, preferred_element_type=F32)


def _dot_nt(a, b):
    return lax.dot_general(a, b, (((1,), (1,)), ((), ())), preferred_element_type=F32)


def _dot_tn(a, b):
    return lax.dot_general(a, b, (((0,), (0,)), ((), ())), preferred_element_type=F32)


def _params(sem):
    return pltpu.CompilerParams(dimension_semantics=sem, vmem_limit_bytes=VMEM_LIMIT)


def _pick_tile(n, cap):
    if n <= cap:
        return n
    best = 128
    for t in range(128, cap + 1, 128):
        if n % t == 0:
            best = t
    assert n % best == 0, (n, cap)
    return best


def _matmul(a, b, mode, name, out_dtype=F32, side=None):
    if mode == 'tn':
        k_dim, m_dim = a.shape
        n_dim = b.shape[1]
    else:
        m_dim, k_dim = a.shape
        n_dim = b.shape[1] if mode == 'nn' else b.shape[0]
    tm = _pick_tile(m_dim, 512 if mode != 'tn' else 1024)
    tn = _pick_tile(n_dim, 1536 if mode != 'tn' else 1024)
    tk = _pick_tile(k_dim, 3072 if mode != 'tn' else 512)
    nk = k_dim // tk
    grid = (n_dim // tn, m_dim // tm, nk)

    n_in = len(side.inputs) if side else 0
    n_out = len(side.out_shape) if side else 0

    def body(*refs):
        a_ref, b_ref, o_ref, acc_ref = refs[0], refs[1], refs[2 + n_in], refs[3 + n_in + n_out]
        if side is not None:
            side.run(refs[2:2 + n_in] + refs[3 + n_in:3 + n_in + n_out] + refs[4 + n_in + n_out:],
                     *_grid_ends(grid))
        kk = pl.program_id(2)

        @pl.when(kk == 0)
        def _():
            acc_ref[...] = jnp.zeros_like(acc_ref)

        if mode == 'nn':
            acc_ref[...] += _dot(a_ref[...], b_ref[...])
        elif mode == 'nt':
            acc_ref[...] += _dot_nt(a_ref[...], b_ref[...])
        else:
            acc_ref[...] += _dot_tn(a_ref[...], b_ref[...])

        @pl.when(kk == nk - 1)
        def _():
            o_ref[...] = acc_ref[...].astype(o_ref.dtype)

    if mode == 'nn':
        a_spec = pl.BlockSpec((tm, tk), lambda j, i, k: (i, k))
        b_spec = pl.BlockSpec((tk, tn), lambda j, i, k: (k, j))
    elif mode == 'nt':
        a_spec = pl.BlockSpec((tm, tk), lambda j, i, k: (i, k))
        b_spec = pl.BlockSpec((tn, tk), lambda j, i, k: (j, k))
    else:
        a_spec = pl.BlockSpec((tk, tm), lambda j, i, k: (k, i))
        b_spec = pl.BlockSpec((tk, tn), lambda j, i, k: (k, j))
    side_in, side_out = side.specs() if side else ([], [])
    res = pl.pallas_call(
        body, name=name, grid=grid,
        out_shape=[jax.ShapeDtypeStruct((m_dim, n_dim), out_dtype)] + (side.out_shape if side else []),
        in_specs=[a_spec, b_spec] + side_in,
        out_specs=[pl.BlockSpec((tm, tn), lambda j, i, k: (i, j))] + side_out,
        scratch_shapes=[pltpu.VMEM((tm, tn), F32)] + (side.sems if side else []),
        compiler_params=_params(("arbitrary", "arbitrary", "arbitrary") if side
                                else ("parallel", "parallel", "arbitrary")),
    )(a, b, *(side.inputs if side else []))
    return res if side else res[0]


def _rowcall(name, fn, tiled, consts, outs, accs=(), tile=512, scratch=()):
    specs, arrays = [], []
    for t in tiled:
        if isinstance(t, tuple):
            arr, width, blk = t
            specs.append(pl.BlockSpec((tile, width), functools.partial(lambda i, blk: (i, blk), blk=blk)))
        else:
            arr = t
            specs.append(pl.BlockSpec((tile, arr.shape[1]), lambda i: (i, 0)))
        arrays.append(arr)
    rows = arrays[0].shape[0]
    assert rows % tile == 0, (name, rows, tile)
    for c in consts:
        specs.append(pl.BlockSpec(c.shape, lambda i: (0, 0)))
        arrays.append(c)
    n_in, n_out, n_acc = len(arrays), len(outs), len(accs)
    out_shape = [jax.ShapeDtypeStruct((rows, c), dt) for c, dt in outs]
    out_specs = [pl.BlockSpec((tile, c), lambda i: (i, 0)) for c, _ in outs]
    out_shape += [jax.ShapeDtypeStruct(s, dt) for s, dt in accs]
    out_specs += [pl.BlockSpec(s, lambda i: (0, 0)) for s, _ in accs]

    def body(*refs):
        ins = [r[...] for r in refs[:n_in]]
        out_refs = refs[n_in:n_in + n_out]
        acc_refs = refs[n_in + n_out:n_in + n_out + n_acc]
        scr = refs[n_in + n_out + n_acc:]
        res = fn(*ins, *scr)
        if not isinstance(res, (tuple, list)):
            res = (res,)
        assert len(res) == n_out + n_acc, (name, len(res))
        for r, v in zip(out_refs, res[:n_out]):
            r[...] = v.astype(r.dtype)
        if n_acc:
            first = pl.program_id(0) == 0

            @pl.when(first)
            def _():
                for r, v in zip(acc_refs, res[n_out:]):
                    r[...] = v.astype(r.dtype)

            @pl.when(jnp.logical_not(first))
            def _():
                for r, v in zip(acc_refs, res[n_out:]):
                    r[...] += v.astype(r.dtype)

    res = pl.pallas_call(
        body, name=name, grid=(rows // tile,),
        out_shape=out_shape, in_specs=specs, out_specs=out_specs,
        scratch_shapes=list(scratch),
        compiler_params=_params(("arbitrary",)),
    )(*arrays)
    return res


def _sigmoid(x):
    return 1.0 / (1.0 + jnp.exp(-x))


def _softplus(x):
    return jnp.maximum(x, 0.0) + jnp.log(1.0 + jnp.exp(-jnp.abs(x)))


def _silu(x):
    return x * _sigmoid(x)


def _dsilu(x):
    s = _sigmoid(x)
    return s * (1.0 + x * (1.0 - s))


_GELU_C = math.sqrt(2.0 / math.pi)


def _gelu(x):
    return 0.5 * x * (1.0 + jnp.tanh(_GELU_C * (x + 0.044715 * x * x * x)))


def _dgelu(x):
    t = jnp.tanh(_GELU_C * (x + 0.044715 * x * x * x))
    return 0.5 * (1.0 + t) + 0.5 * x * (1.0 - t * t) * _GELU_C * (1.0 + 3.0 * 0.044715 * x * x)


def _neg_expm1(x):
    series = -x * (1.0 + x * (0.5 + x * (1.0 / 6.0 + x * (1.0 / 24.0 + x * (1.0 / 120.0)))))
    return jnp.where(x > -0.03, series, 1.0 - jnp.exp(x))


def _rms(x, g):
    r = lax.rsqrt(jnp.mean(x * x, axis=-1, keepdims=True) + EPS)
    return x * r * g


def _rms_bwd(x, g, dy):
    r = lax.rsqrt(jnp.mean(x * x, axis=-1, keepdims=True) + EPS)
    xhat = x * r
    dg = jnp.sum(dy * xhat, axis=0, keepdims=True)
    dxhat = dy * g
    dx = r * (dxhat - xhat * jnp.mean(dxhat * xhat, axis=-1, keepdims=True))
    return dx, dg


def _row_iota(shape):
    return lax.broadcasted_iota(jnp.int32, shape, 0)


def _shift_down(x, j):
    if j == 0:
        return x
    return jnp.where(_row_iota(x.shape) >= j, pltpu.roll(x, j, 0), 0.0)


def _shift_up(x, j):
    if j == 0:
        return x
    n = x.shape[0]
    return jnp.where(_row_iota(x.shape) < n - j, pltpu.roll(x, n - j, 0), 0.0)


def _conv(x, w, b):
    y = b + w[3:4, :] * x
    for k in range(3):
        y = y + w[k:k + 1, :] * _shift_down(x, 3 - k)
    return y


def _conv_bwd(x, w, dy):
    dx = w[3:4, :] * dy
    dws = []
    for k in range(3):
        dx = dx + w[k:k + 1, :] * _shift_up(dy, 3 - k)
        dws.append(jnp.sum(dy * _shift_down(x, 3 - k), axis=0, keepdims=True))
    dws.append(jnp.sum(dy * x, axis=0, keepdims=True))
    return dx, jnp.concatenate(dws, axis=0), jnp.sum(dy, axis=0, keepdims=True)


def _split3(x):
    hi = x.astype(BF16)
    r1 = x - hi.astype(F32)
    mid = r1.astype(BF16)
    lo = (r1 - mid.astype(F32)).astype(BF16)
    return hi, mid, lo


def _tri_dot(tri, x):
    hi, mid, lo = _split3(x)
    return _dot(tri, hi) + _dot(tri, mid) + _dot(tri, lo)


def _cumsum_rows(x):
    n = x.shape[0] // CHUNK
    r = lax.broadcasted_iota(jnp.int32, (CHUNK, CHUNK), 0)
    c = lax.broadcasted_iota(jnp.int32, (CHUNK, CHUNK), 1)
    tri = (r >= c).astype(BF16)
    carry = jnp.zeros((1, x.shape[1]), F32)
    cums, prevs, ends = [], [], []
    for i in range(n):
        blk = _tri_dot(tri, x[i * CHUNK:(i + 1) * CHUNK]) + carry
        prevs.append(jnp.broadcast_to(carry, blk.shape))
        carry = blk[CHUNK - 1:CHUNK, :]
        ends.append(jnp.broadcast_to(carry, blk.shape))
        cums.append(blk)
    return jnp.concatenate(cums, 0), jnp.concatenate(prevs, 0), jnp.concatenate(ends, 0)


def _rev_cumsum_rows(x):
    n = x.shape[0] // CHUNK
    r = lax.broadcasted_iota(jnp.int32, (CHUNK, CHUNK), 0)
    c = lax.broadcasted_iota(jnp.int32, (CHUNK, CHUNK), 1)
    tri = (r <= c).astype(BF16)
    carry = jnp.zeros((1, x.shape[1]), F32)
    local, whole = [None] * n, [None] * n
    for i in range(n - 1, -1, -1):
        local[i] = _tri_dot(tri, x[i * CHUNK:(i + 1) * CHUNK])
        whole[i] = local[i] + carry
        carry = whole[i][0:1, :]
    return jnp.concatenate(local, 0), jnp.concatenate(whole, 0)


def _convsilu_fwd(name, seg, seq, w, b, dtype):
    return _rowcall(name, lambda raw, w, b: _silu(_conv(raw, w, b)), [seg], [w, b], [(seg[1], dtype)], tile=seq)[0]


def _convsilu_bwd(name, seg, seq, w, b, dy):
    def fn(raw, dy, w, b):
        return _conv_bwd(raw, w, dy * _dsilu(_conv(raw, w, b)))

    width = seg[1]
    return _rowcall(name, fn, [seg, dy], [w, b], [(width, BF16)], [((4, width), F32), ((1, width), F32)], tile=seq)


def _small_fwd(seg, seq, bias128, alog128):
    def fn(small, bias, alog):
        lane = lax.broadcasted_iota(jnp.int32, small.shape, 1)
        a = -jnp.exp(alog)
        s = small + bias
        dt = _softplus(s)
        logf = -_softplus(-s)
        pre = jnp.where(lane < SSD_HEADS, a * dt, jnp.where(lane < 2 * SSD_HEADS, logf, 0.0))
        cum, prev, end = _cumsum_rows(pre)
        return dt, cum, prev, end

    return _rowcall("small_fwd", fn, [seg], [bias128, alog128], [(128, F32)] * 4, tile=seq)


def _small_bwd(seg, seq, dcum, dend, ddt, dt_arr, bias128, alog128):
    def fn(small, dcum, dend, ddt, dt_arr, bias, alog):
        lane = lax.broadcasted_iota(jnp.int32, small.shape, 1)
        a = -jnp.exp(alog)
        sig = _sigmoid(small + bias)
        local, whole = _rev_cumsum_rows(dcum)
        dadt = local + dend
        d_dt = ddt + a * dadt
        ds = jnp.where(lane < SSD_HEADS, d_dt * sig, jnp.where(lane < 2 * SSD_HEADS, whole * (1.0 - sig), 0.0))
        da = jnp.sum(jnp.where(lane < SSD_HEADS, dadt * dt_arr, 0.0), axis=0, keepdims=True)
        return ds, jnp.sum(ds, axis=0, keepdims=True), da * a

    return _rowcall("small_bwd", fn, [seg, dcum, dend, ddt, dt_arr], [bias128, alog128], [(128, BF16)],
                    [((1, 128), F32), ((1, 128), F32)], tile=seq)


HEAD_PAIRS = SSD_HEADS // 2


def _ssd_specs(nc, reverse):
    def at(c):
        return nc - 1 - c if reverse else c

    x_spec = pl.BlockSpec((1, CHUNK, SSD_WIDTH), lambda b, c: (b, at(c), 0))
    bc_spec = pl.BlockSpec((1, CHUNK, 256), lambda b, c: (b, at(c), 0))
    col_spec = pl.BlockSpec((1, CHUNK, 128), lambda b, c: (b, at(c), 0))
    row_spec = pl.BlockSpec((1, 8, CHUNK), lambda b, c: (b, 0, at(c)))
    st_spec = pl.BlockSpec((1, 1, HEAD_PAIRS, SSD_STATE, 128), lambda b, c: (b, at(c), 0, 0, 0))
    return x_spec, bc_spec, col_spec, row_spec, st_spec


def _ssd_head(h, dtb, acb, apb, aeb, arb):
    return dtb[:, h:h + 1], acb[:, h:h + 1], apb[:, h:h + 1], aeb[:, h:h + 1], arb[h:h + 1, :]


def _ssd_fwd(x, bm, cm, dt_arr, cum, prev, end, a_row):
    nb, seq, _ = x.shape
    nc = seq // CHUNK
    x_spec, bc_spec, col_spec, row_spec, st_spec = _ssd_specs(nc, False)

    def body(x_ref, b_ref, c_ref, dt_ref, ac_ref, ap_ref, ae_ref, ar_ref, y_ref, st_ref, s_scr):
        @pl.when(pl.program_id(1) == 0)
        def _():
            s_scr[...] = jnp.zeros_like(s_scr)

        causal = (lax.broadcasted_iota(jnp.int32, (CHUNK, CHUNK), 0)
                  >= lax.broadcasted_iota(jnp.int32, (CHUNK, CHUNK), 1))
        low = lax.broadcasted_iota(jnp.int32, (CHUNK, 128), 1) < HEAD_DIM
        cols = (dt_ref[0], ac_ref[0], ap_ref[0], ae_ref[0], ar_ref[0])
        bcs = [b_ref[0, :, g * 128:(g + 1) * 128] for g in range(2)]
        ccs = [c_ref[0, :, g * 128:(g + 1) * 128] for g in range(2)]
        ms = [_dot_nt(ccs[g], bcs[g]) for g in range(2)]
        for pi in range(HEAD_PAIRS):
            x2 = x_ref[0, :, pi * 128:(pi + 1) * 128]
            dt2 = jnp.where(low, cols[0][:, 2 * pi:2 * pi + 1], cols[0][:, 2 * pi + 1:2 * pi + 2])
            xdt = (x2 * dt2).astype(BF16)
            sprev = s_scr[pi]
            st_ref[0, 0, pi] = sprev
            spb = sprev.astype(BF16)
            ys, us = [], []
            for h in (2 * pi, 2 * pi + 1):
                g = h // 3
                _, ac, ap, ae, ar = _ssd_head(h, *cols)
                lm = jnp.exp(jnp.where(causal, ac - ar, NEG))
                gm = (ms[g] * lm).astype(BF16)
                ys.append(_dot(gm, xdt) + jnp.exp(ac - ap) * _dot(ccs[g], spb))
                bdec = (bcs[g].astype(F32) * jnp.exp(ae - ac)).astype(BF16)
                us.append(jnp.exp(ae[0:1, :] - ap[0:1, :]) * sprev + _dot_tn(bdec, xdt))
            y_ref[0, :, pi * 128:(pi + 1) * 128] = jnp.where(low, ys[0], ys[1])
            s_scr[pi] = jnp.where(low, us[0], us[1])

    return pl.pallas_call(
        body, name="ssd_fwd", grid=(nb, nc),
        out_shape=[jax.ShapeDtypeStruct(x.shape, F32),
                   jax.ShapeDtypeStruct((nb, nc, HEAD_PAIRS, SSD_STATE, 128), F32)],
        in_specs=[x_spec, bc_spec, bc_spec, col_spec, col_spec, col_spec, col_spec, row_spec],
        out_specs=[x_spec, st_spec],
        scratch_shapes=[pltpu.VMEM((HEAD_PAIRS, SSD_STATE, 128), F32)],
        compiler_params=_params(("parallel", "arbitrary")),
    )(x, bm, cm, dt_arr, cum, prev, end, a_row)


def _ssd_bwd(x_h, bm, cm, dt_arr, cum, prev, end, a_row, states, dy_h):
    nb, seq, _ = x_h.shape
    nc = seq // CHUNK
    x_spec, bc_spec, col_spec, row_spec, st_spec = _ssd_specs(nc, True)

    def body(x_ref, b_ref, c_ref, dt_ref, ac_ref, ap_ref, ae_ref, ar_ref, st_ref, dy_ref,
             dx_ref, db_ref, dc_ref, da_ref, dend_ref, ddt_ref, ds_scr):
        @pl.when(pl.program_id(1) == 0)
        def _():
            ds_scr[...] = jnp.zeros_like(ds_scr)

        causal = (lax.broadcasted_iota(jnp.int32, (CHUNK, CHUNK), 0)
                  >= lax.broadcasted_iota(jnp.int32, (CHUNK, CHUNK), 1))
        lane = lax.broadcasted_iota(jnp.int32, (CHUNK, 128), 1)
        low = lane < HEAD_DIM
        cols = (dt_ref[0], ac_ref[0], ap_ref[0], ae_ref[0], ar_ref[0])
        bcs = [b_ref[0, :, g * 128:(g + 1) * 128] for g in range(2)]
        ccs = [c_ref[0, :, g * 128:(g + 1) * 128] for g in range(2)]
        ms = [_dot_nt(ccs[g], bcs[g]) for g in range(2)]
        dms = [jnp.zeros((CHUNK, CHUNK), F32) for _ in range(2)]
        dc_accs = [jnp.zeros((CHUNK, SSD_STATE), F32) for _ in range(2)]
        db_accs = [jnp.zeros((CHUNK, SSD_STATE), F32) for _ in range(2)]
        da_blk = jnp.zeros((CHUNK, 128), F32)
        dend_blk = jnp.zeros((CHUNK, 128), F32)
        ddt_blk = jnp.zeros((CHUNK, 128), F32)
        for pi in range(HEAD_PAIRS):
            x2 = x_ref[0, :, pi * 128:(pi + 1) * 128]
            dy2 = dy_ref[0, :, pi * 128:(pi + 1) * 128]
            dt2 = jnp.where(low, cols[0][:, 2 * pi:2 * pi + 1], cols[0][:, 2 * pi + 1:2 * pi + 2])
            xdt_f = x2 * dt2
            xdt = xdt_f.astype(BF16)
            dyb = dy2.astype(BF16)
            dsn = ds_scr[pi]
            dsb = dsn.astype(BF16)
            sprev_f = st_ref[0, 0, pi]
            sprev = sprev_f.astype(BF16)
            dxdts, dss = [], []
            for h in (2 * pi, 2 * pi + 1):
                g = h // 3
                mine = low if h % 2 == 0 else jnp.logical_not(low)
                _, ac, ap, ae, ar = _ssd_head(h, *cols)
                bc, cc, m = bcs[g], ccs[g], ms[g]
                lm = jnp.exp(jnp.where(causal, ac - ar, NEG))
                gm = (m * lm).astype(BF16)
                dy_m = jnp.where(mine, dy2, 0.0)
                dyb_m = dy_m.astype(BF16)
                xdt_m = jnp.where(mine, xdt_f, 0.0)
                e_in = jnp.exp(ac - ap)
                f_out = jnp.exp(ae - ac)
                whole = jnp.exp(ae[0:1, :] - ap[0:1, :])
                dg = _dot_nt(dyb_m, xdt)
                dxdt_off = f_out * _dot(bc, dsb)
                dxdt = _dot_tn(gm, dyb) + dxdt_off
                dmj = dg * lm
                dms[g] = dms[g] + dmj
                dc_accs[g] = dc_accs[g] + e_in * _dot_nt(dyb_m, sprev)
                db_accs[g] = db_accs[g] + f_out * _dot_nt(xdt_m.astype(BF16), dsb)
                dss.append(whole * dsn + _dot_tn((cc.astype(F32) * e_in).astype(BF16), dyb))
                wmat = dmj * m
                r_in = jnp.sum(dy_m * (e_in * _dot(cc, sprev)), axis=1, keepdims=True)
                q_out = jnp.sum(xdt_m * dxdt_off, axis=1, keepdims=True)
                daj = (jnp.sum(wmat, axis=1, keepdims=True) - jnp.sum(wmat.T, axis=1, keepdims=True)
                       + r_in - q_out)
                cross = jnp.where(mine, dsn * sprev_f, 0.0)
                dendj = (jnp.sum(q_out, axis=0, keepdims=True)
                         + whole * jnp.sum(jnp.sum(cross, axis=1, keepdims=True), axis=0, keepdims=True))
                ddtj = jnp.sum(jnp.where(mine, dxdt * x2, 0.0), axis=1, keepdims=True)
                dxdts.append(dxdt)
                da_blk = jnp.where(lane == h, daj, da_blk)
                dend_blk = jnp.where(lane == h, dendj, dend_blk)
                ddt_blk = jnp.where(lane == h, ddtj, ddt_blk)
            dx_ref[0, :, pi * 128:(pi + 1) * 128] = jnp.where(low, dxdts[0], dxdts[1]) * dt2
            ds_scr[pi] = jnp.where(low, dss[0], dss[1])
        for g in range(2):
            dmb = dms[g].astype(BF16)
            dc_ref[0, :, g * 128:(g + 1) * 128] = dc_accs[g] + _dot(dmb, bcs[g])
            db_ref[0, :, g * 128:(g + 1) * 128] = db_accs[g] + _dot_tn(dmb, ccs[g])
        da_ref[0] = da_blk
        dend_ref[0] = dend_blk
        ddt_ref[0] = ddt_blk

    col_shape = jax.ShapeDtypeStruct((nb, seq, 128), F32)
    return pl.pallas_call(
        body, name="ssd_bwd", grid=(nb, nc),
        out_shape=[jax.ShapeDtypeStruct(x_h.shape, F32),
                   jax.ShapeDtypeStruct((nb, seq, 256), F32), jax.ShapeDtypeStruct((nb, seq, 256), F32),
                   col_shape, col_shape, col_shape],
        in_specs=[x_spec, bc_spec, bc_spec, col_spec, col_spec, col_spec, col_spec, row_spec, st_spec, x_spec],
        out_specs=[x_spec, bc_spec, bc_spec, col_spec, col_spec, col_spec],
        scratch_shapes=[pltpu.VMEM((HEAD_PAIRS, SSD_STATE, 128), F32)],
        compiler_params=_params(("parallel", "arbitrary")),
    )(x_h, bm, cm, dt_arr, cum, prev, end, a_row, states, dy_h)


def _lru_gates(xl, wa, ba, wx, bx, lam):
    xb = xl.astype(BF16)
    r = _sigmoid(_dot(xb, wa) + ba)
    i = _sigmoid(_dot(xb, wx) + bx)
    sp = _softplus(-lam)
    log_a = -LRU_C * r * sp
    a = jnp.exp(log_a)
    mult = jnp.sqrt(_neg_expm1(2.0 * log_a))
    return r, i, sp, log_a, a, mult


def _scan_chunks(a_ref, u_ref, h_ref, seq, reverse):
    nc = seq // CHUNK
    width = a_ref.shape[1]
    row = lax.broadcasted_iota(jnp.int32, (CHUNK, width), 0)

    def chunk(ci, carry):
        c = nc - 1 - ci if reverse else ci
        rows = pl.ds(pl.multiple_of(c * CHUNK, CHUNK), CHUNK)
        av, bv = a_ref[rows, :], u_ref[rows, :]
        d = 1
        while d < CHUNK:
            if reverse:
                keep = row < CHUNK - d
                a_sh = jnp.where(keep, pltpu.roll(av, CHUNK - d, 0), 1.0)
                b_sh = jnp.where(keep, pltpu.roll(bv, CHUNK - d, 0), 0.0)
            else:
                keep = row >= d
                a_sh = jnp.where(keep, pltpu.roll(av, d, 0), 1.0)
                b_sh = jnp.where(keep, pltpu.roll(bv, d, 0), 0.0)
            bv = av * b_sh + bv
            av = av * a_sh
            d *= 2
        hv = bv + av * carry
        h_ref[rows, :] = hv
        return hv[0:1, :] if reverse else hv[CHUNK - 1:CHUNK, :]

    lax.fori_loop(0, nc, chunk, jnp.zeros((1, width), F32))


def _lru_fwd(proj, seq, cw, cb, wa, ba, wx, bx, lam):
    def fn(raw, cw, cb, wa, ba, wx, bx, lam, a_scr, u_scr, h_scr):
        xl = _conv(raw, cw, cb)
        r, i, sp, log_a, a, mult = _lru_gates(xl, wa, ba, wx, bx, lam)
        a_scr[...] = a
        u_scr[...] = mult * (i * xl)
        _scan_chunks(a_scr, u_scr, h_scr, seq, reverse=False)
        return h_scr[...], xl

    return _rowcall("lru_fwd", fn, [(proj, 256, 2)], [cw, cb, wa, ba, wx, bx, lam],
                    [(256, F32), (256, F32)], tile=seq,
                    scratch=[pltpu.VMEM((seq, 256), F32)] * 3)


def _lru_bwd(proj, seq, xl_all, h_all, dh_all, cw, cb, wa, ba, wx, bx, lam):
    def fn(raw, xl, hseq, dh, cw, cb, wa, ba, wx, bx, lam, a_scr, u_scr, h_scr):
        r, i, sp, log_a, a, mult = _lru_gates(xl, wa, ba, wx, bx, lam)
        a_scr[...] = _shift_up(a, 1)
        u_scr[...] = dh
        _scan_chunks(a_scr, u_scr, h_scr, seq, reverse=True)
        dht = h_scr[...]
        da = dht * _shift_down(hseq, 1)
        gated = i * xl
        dgated = dht * mult
        dmult = dht * gated
        dlog_a = da * a - dmult * (a * a) / mult
        dr = dlog_a * (-LRU_C * sp)
        dsp = jnp.sum(dlog_a * (-LRU_C * r), axis=0, keepdims=True)
        dlam = -dsp * _sigmoid(-lam)
        dpa = dr * r * (1.0 - r)
        dpx = (dgated * xl) * i * (1.0 - i)
        dpa_b, dpx_b = dpa.astype(BF16), dpx.astype(BF16)
        dxl = dgated * i + _dot_nt(dpa_b, wa) + _dot_nt(dpx_b, wx)
        xb = xl.astype(BF16)
        dwa = _dot_tn(xb, dpa_b)
        dwx = _dot_tn(xb, dpx_b)
        draw, dcw, dcb = _conv_bwd(raw, cw, dxl)
        return (draw, dcw, dcb, dwa, jnp.sum(dpa, axis=0, keepdims=True), dwx,
                jnp.sum(dpx, axis=0, keepdims=True), dlam)

    return _rowcall("lru_bwd", fn, [(proj, 256, 2), xl_all, h_all, dh_all], [cw, cb, wa, ba, wx, bx, lam],
                    [(256, BF16)],
                    [((4, 256), F32), ((1, 256), F32), ((256, 256), F32), ((1, 256), F32), ((256, 256), F32),
                     ((1, 256), F32), ((1, 256), F32)],
                    tile=seq, scratch=[pltpu.VMEM((seq, 256), F32)] * 3)


FOX_SCALE = HEAD_DIM ** -0.5
FOX_BLOCK = 512


class _Side:
    def __init__(self, inputs, out_shape, sems, build):
        self.inputs, self.out_shape, self.sems, self.build = list(inputs), list(out_shape), list(sems), build

    def specs(self):
        any_spec = pl.BlockSpec(memory_space=pl.ANY)
        return [any_spec] * len(self.inputs), [any_spec] * len(self.out_shape)

    def run(self, refs, first, last):
        n_in, n_out = len(self.inputs), len(self.out_shape)
        in_refs, out_refs, sem_refs = refs[:n_in], refs[n_in:n_in + n_out], refs[n_in + n_out:]

        @pl.when(first)
        def _():
            for cp in self.build(in_refs, out_refs, sem_refs):
                cp.start()

        @pl.when(last)
        def _():
            for cp in self.build(in_refs, out_refs, sem_refs):
                cp.wait()


def _grid_ends(grid):
    ids = [pl.program_id(a) for a in range(len(grid))]
    first = functools.reduce(jnp.logical_and, [i == 0 for i in ids])
    last = functools.reduce(jnp.logical_and, [i == n - 1 for i, n in zip(ids, grid)])
    return first, last


def _fox_fwd(q, k, v, cum_col, cum_row, side=None):
    nb, nh, seq, e = q.shape
    tb = min(FOX_BLOCK, seq)
    nq = seq // tb
    grid = (nb, nh, nq)

    def body(*refs):
        q_ref, k_ref, v_ref, cq_ref, ck_ref = refs[:5]
        if side is None:
            o_ref, lse_ref = refs[5:7]
        else:
            n_in = len(side.inputs)
            o_ref, lse_ref = refs[5 + n_in:7 + n_in]
            side.run(refs[5:5 + n_in] + refs[7 + n_in:], *_grid_ends(grid))
        qi = pl.program_id(2)
        qv = q_ref[0, 0]
        cq = cq_ref[0, 0]

        def block(j, carry, diagonal):
            m_i, l_i, acc = carry
            cols = pl.ds(pl.multiple_of(j * tb, tb), tb)
            s = _dot_nt(qv, k_ref[0, 0, cols, :]) + cq - ck_ref[0, 0, :, cols]
            if diagonal:
                s = jnp.where(lax.broadcasted_iota(jnp.int32, (tb, tb), 0)
                              >= lax.broadcasted_iota(jnp.int32, (tb, tb), 1), s, NEG)
            m_new = jnp.maximum(m_i, jnp.max(s, axis=1, keepdims=True))
            p = jnp.exp(s - m_new)
            alpha = jnp.exp(m_i - m_new)
            l_new = alpha * l_i + jnp.sum(p, axis=1, keepdims=True)
            acc_new = alpha * acc + _dot(p.astype(BF16), v_ref[0, 0, cols, :])
            return m_new, l_new, acc_new

        init = (jnp.full((tb, 1), NEG, F32), jnp.zeros((tb, 1), F32), jnp.zeros((tb, e), F32))
        carry = lax.fori_loop(0, qi, lambda j, cr: block(j, cr, False), init)
        m_i, l_i, acc = block(qi, carry, True)
        o_ref[0, 0] = acc / l_i
        lse_ref[0, 0] = m_i + jnp.log(l_i)

    blk = pl.BlockSpec((1, 1, tb, e), lambda b, h, i: (b, h, i, 0))
    full = pl.BlockSpec((1, 1, seq, e), lambda b, h, i: (b, h, 0, 0))
    col_blk = pl.BlockSpec((1, 1, tb, 1), lambda b, h, i: (b, h, i, 0))
    row_full = pl.BlockSpec((1, 1, 1, seq), lambda b, h, i: (b, h, 0, 0))
    side_in, side_out = side.specs() if side else ([], [])
    return pl.pallas_call(
        body, name="fox_fwd", grid=grid,
        out_shape=[jax.ShapeDtypeStruct((nb, nh, seq, e), F32), jax.ShapeDtypeStruct((nb, nh, seq, 1), F32)]
        + (side.out_shape if side else []),
        in_specs=[blk, full, full, col_blk, row_full] + side_in,
        out_specs=[blk, col_blk] + side_out,
        scratch_shapes=side.sems if side else [],
        compiler_params=_params(("arbitrary", "arbitrary", "arbitrary")),
    )(q, k, v, cum_col, cum_row, *(side.inputs if side else []))


def _fox_bwd(q, k, v, o, lse, do, cum_col, cum_row, side=None):
    nb, nh, seq, e = q.shape
    tb = min(FOX_BLOCK, seq)
    nq = seq // tb
    grid = (nb, nh, nq)

    def body(*refs):
        q_ref, k_ref, v_ref, o_ref, lse_ref, do_ref, cq_ref, ck_ref = refs[:8]
        if side is None:
            dq_ref, dk_ref, dv_ref, dcum_ref, dcq_ref = refs[8:13]
        else:
            n_in = len(side.inputs)
            dq_ref, dk_ref, dv_ref, dcum_ref, dcq_ref = refs[8 + n_in:13 + n_in]
            side.run(refs[8:8 + n_in] + refs[13 + n_in:], *_grid_ends(grid))
        kj = pl.program_id(2)

        @pl.when(kj == 0)
        def _():
            dq_ref[...] = jnp.zeros_like(dq_ref)
            dcq_ref[...] = jnp.zeros_like(dcq_ref)

        kv = k_ref[0, 0]
        vv = v_ref[0, 0]
        ck = ck_ref[0, 0]

        def block(i, carry, diagonal):
            dk, dv, csum = carry
            rows = pl.ds(pl.multiple_of(i * tb, tb), tb)
            qv = q_ref[0, 0, rows, :]
            dov = do_ref[0, 0, rows, :]
            dob = dov.astype(BF16)
            delta = jnp.sum(dov * o_ref[0, 0, rows, :], axis=1, keepdims=True)
            s = _dot_nt(qv, kv) + cq_ref[0, 0, rows, :] - ck
            if diagonal:
                s = jnp.where(lax.broadcasted_iota(jnp.int32, (tb, tb), 0)
                              >= lax.broadcasted_iota(jnp.int32, (tb, tb), 1), s, NEG)
            p = jnp.exp(s - lse_ref[0, 0, rows, :])
            dp = _dot_nt(dob, vv)
            ds = p * (dp - delta)
            dsb = ds.astype(BF16)
            dv = dv + _dot_tn(p.astype(BF16), dob)
            dk = dk + _dot_tn(dsb, qv)
            dq_ref[0, 0, rows, :] += _dot(dsb, kv) * FOX_SCALE
            dcq_ref[0, 0, rows, :] += jnp.sum(ds, axis=1, keepdims=True)
            csum = csum + jnp.sum(ds, axis=0, keepdims=True)
            return dk, dv, csum

        init = (jnp.zeros((tb, e), F32), jnp.zeros((tb, e), F32), jnp.zeros((1, tb), F32))
        carry = block(kj, init, True)
        dk, dv, csum = lax.fori_loop(kj + 1, nq, lambda i, cr: block(i, cr, False), carry)
        dk_ref[0, 0] = dk
        dv_ref[0, 0] = dv
        dcum_ref[0, 0] = -csum

    blk = pl.BlockSpec((1, 1, tb, e), lambda b, h, j: (b, h, j, 0))
    full = pl.BlockSpec((1, 1, seq, e), lambda b, h, j: (b, h, 0, 0))
    col_full = pl.BlockSpec((1, 1, seq, 1), lambda b, h, j: (b, h, 0, 0))
    row_blk = pl.BlockSpec((1, 1, 1, tb), lambda b, h, j: (b, h, 0, j))
    side_in, side_out = side.specs() if side else ([], [])
    return pl.pallas_call(
        body, name="fox_bwd", grid=grid,
        out_shape=[jax.ShapeDtypeStruct((nb, nh, seq, e), F32), jax.ShapeDtypeStruct((nb, nh, seq, e), F32),
                   jax.ShapeDtypeStruct((nb, nh, seq, e), F32), jax.ShapeDtypeStruct((nb, nh, 1, seq), F32),
                   jax.ShapeDtypeStruct((nb, nh, seq, 1), F32)] + (side.out_shape if side else []),
        in_specs=[full, blk, blk, full, col_full, full, col_full, row_blk] + side_in,
        out_specs=[full, blk, blk, row_blk, col_full] + side_out,
        scratch_shapes=side.sems if side else [],
        compiler_params=_params(("arbitrary", "arbitrary", "arbitrary")),
    )(q, k, v, o, lse, do, cum_col, cum_row, *(side.inputs if side else []))


_ANY = pl.BlockSpec(memory_space=pl.ANY)


def _place():
    return lax.axis_index("x"), lax.axis_index("y"), lax.axis_index("c")


def _all_gather(shards, name):
    n = len(shards)

    def body(*refs):
        x_refs, out_refs = refs[:n], refs[n:2 * n]
        send_sems, recv_sems, local_sems = refs[2 * n:]
        x, y, c = _place()
        me, sibling = (x, y, c), (x, y, 1 - c)
        chips = [(1 - x, y), (x, 1 - y), (1 - x, 1 - y)]

        def rows(a, px, py, pc):
            return out_refs[a].at[4 * px + 2 * py + pc]

        def copy(a, k, block, to, src=None):
            return pltpu.make_async_remote_copy(
                src_ref=rows(a, *block) if src is None else src, dst_ref=rows(a, *block),
                send_sem=send_sems.at[a, k], recv_sem=recv_sems.at[a, k],
                device_id=to, device_id_type=pl.DeviceIdType.MESH)

        mine = [pltpu.make_async_copy(x_refs[a], rows(a, *me), local_sems.at[a]) for a in range(n)]
        for cp in mine:
            cp.start()
        first = []
        for a in range(n):
            first.append(copy(a, 0, me, sibling, src=x_refs[a]))
            first += [copy(a, 1 + j, me, (*chip, c), src=x_refs[a]) for j, chip in enumerate(chips)]
        for cp in first:
            cp.start()
        passed = []
        for j, chip in enumerate(chips):
            for a in range(n):
                copy(a, 1 + j, (*chip, c), me).wait_recv()
                passed.append(copy(a, 4 + j, (*chip, c), sibling))
                passed[-1].start()
        for a in range(n):
            copy(a, 0, sibling, me).wait_recv()
            for j, chip in enumerate(chips):
                copy(a, 4 + j, (*chip, 1 - c), me).wait_recv()
        for cp in first + passed:
            cp.wait_send()
        for cp in mine:
            cp.wait()

    return pl.pallas_call(
        body, name=name,
        out_shape=[jax.ShapeDtypeStruct((N_DEV,) + s.shape, s.dtype) for s in shards],
        in_specs=[_ANY] * n, out_specs=[_ANY] * n,
        scratch_shapes=[pltpu.SemaphoreType.DMA((n, 7)), pltpu.SemaphoreType.DMA((n, 7)),
                        pltpu.SemaphoreType.DMA((n,))],
    )(*shards)


def _remote(src, dst, send_sem, recv_sem, to):
    return pltpu.make_async_remote_copy(src_ref=src, dst_ref=dst, send_sem=send_sem, recv_sem=recv_sem,
                                        device_id=to, device_id_type=pl.DeviceIdType.MESH)


def _sem_pairs(n, k):
    return [pltpu.SemaphoreType.DMA((n, k)), pltpu.SemaphoreType.DMA((n, k))]


def _sibling_side(full):
    def build(g_refs, out_refs, sems):
        x, y, c = _place()
        return [_remote(g_refs[a].at[4 * (k // 2) + 2 * (k % 2) + (1 - c)], out_refs[a].at[k],
                        sems[0].at[a, k], sems[1].at[a, k], (x, y, 1 - c))
                for a in range(len(g_refs)) for k in range(4)]

    return _Side(full, [jax.ShapeDtypeStruct((4,) + f.shape[1:], f.dtype) for f in full],
                 _sem_pairs(len(full), 4), build)


def _chip_side(part):
    def build(p_refs, out_refs, sems):
        x, y, c = _place()
        peers = [(1 - x, y), (x, 1 - y), (1 - x, 1 - y)]
        return [_remote(p_refs[a].at[2 * px + py], out_refs[a].at[k], sems[0].at[a, k], sems[1].at[a, k],
                        (px, py, c))
                for a in range(len(p_refs)) for k, (px, py) in enumerate(peers)]

    return _Side(part, [jax.ShapeDtypeStruct((3,) + p.shape[1:], p.dtype) for p in part],
                 _sem_pairs(len(part), 3), build)


def _spread_side(shards):
    def build(x_refs, out_refs, sems):
        x, y, c = _place()
        targets = [(x, y, 1 - c), (1 - x, y, c), (x, 1 - y, c), (1 - x, 1 - y, c)]
        cps = []
        for a in range(len(x_refs)):
            slot = out_refs[a].at[4 * x + 2 * y + c]
            cps.append(pltpu.make_async_copy(x_refs[a], slot, sems[2].at[a]))
            cps += [_remote(x_refs[a], slot, sems[0].at[a, k], sems[1].at[a, k], to)
                    for k, to in enumerate(targets)]
        return cps

    n = len(shards)
    return _Side(shards, [jax.ShapeDtypeStruct((N_DEV,) + s.shape, s.dtype) for s in shards],
                 _sem_pairs(n, 4) + [pltpu.SemaphoreType.DMA((n,))], build)


def _pass_side(bufs):
    def build(in_refs, out_refs, sems):
        x, y, c = _place()
        chips = [(1 - x, y), (x, 1 - y), (1 - x, 1 - y)]
        return [_remote(in_refs[a].at[4 * px + 2 * py + c], out_refs[a].at[4 * px + 2 * py + c],
                        sems[0].at[a, j], sems[1].at[a, j], (x, y, 1 - c))
                for a in range(len(in_refs)) for j, (px, py) in enumerate(chips)]

    return _Side(bufs, [jax.ShapeDtypeStruct(b.shape, b.dtype) for b in bufs], _sem_pairs(len(bufs), 3), build)


def _run_side(side, name, in_place=False):
    n_in = len(side.inputs)

    def body(*refs):
        copies = side.build(refs[:n_in], refs[n_in:n_in + len(side.out_shape)],
                            refs[n_in + len(side.out_shape):])
        for cp in copies:
            cp.start()
        for cp in copies:
            cp.wait()

    in_specs, out_specs = side.specs()
    return pl.pallas_call(
        body, name=name, out_shape=side.out_shape, in_specs=in_specs, out_specs=out_specs,
        scratch_shapes=side.sems,
        input_output_aliases={a: a for a in range(n_in)} if in_place else {},
    )(*side.inputs)


def _pick_rows(rows, cap=512):
    t = cap
    while t >= 8:
        if rows % t == 0:
            return t
        t //= 2
    raise ValueError(rows)


def _pair_sum(full, got, name):
    _, rows, cols = full.shape
    tile = _pick_rows(rows, 256)
    c = lax.axis_index("c").astype(jnp.int32).reshape(1)

    def body(c_ref, a_ref, b_ref, o_ref):
        o_ref[...] = a_ref[...] + b_ref[...]

    blk = (1, tile, cols)
    return pl.pallas_call(
        body, name=name,
        grid_spec=pltpu.PrefetchScalarGridSpec(
            num_scalar_prefetch=1, grid=(4, rows // tile),
            in_specs=[pl.BlockSpec(blk, lambda k, i, c_ref: (4 * (k // 2) + 2 * (k % 2) + c_ref[0], i, 0)),
                      pl.BlockSpec(blk, lambda k, i, c_ref: (k, i, 0))],
            out_specs=pl.BlockSpec(blk, lambda k, i, c_ref: (k, i, 0))),
        out_shape=jax.ShapeDtypeStruct((4, rows, cols), full.dtype),
        compiler_params=_params(("arbitrary", "arbitrary")),
    )(c, full, got)


def _adam_math(w, g, m, v):
    c1 = 1.0 / (1.0 - ADAM_B1 ** ADAM_STEP)
    c2 = 1.0 / (1.0 - ADAM_B2 ** ADAM_STEP)
    m_new = ADAM_B1 * m + (1.0 - ADAM_B1) * g
    v_new = ADAM_B2 * v + (1.0 - ADAM_B2) * (g * g)
    delta = -ADAM_LR * ((m_new * c1) / (jnp.sqrt(v_new * c2) + ADAM_EPS) + ADAM_WD * w)
    return delta, m_new, v_new


def _sum_adamw(part, others, w, m, v, name):
    _, rows, cols = part.shape
    tile = _pick_rows(rows, 128)
    own = (2 * lax.axis_index("x") + lax.axis_index("y")).astype(jnp.int32).reshape(1)

    def body(own_ref, p_ref, o_ref, w_ref, m_ref, v_ref, g_out, d_out, m_out, v_out):
        g = ((p_ref[0] + o_ref[0]) + o_ref[1]) + o_ref[2]
        delta, m_new, v_new = _adam_math(w_ref[...], g, m_ref[...], v_ref[...])
        g_out[...] = g
        d_out[...] = delta
        m_out[...] = m_new
        v_out[...] = v_new

    flat = pl.BlockSpec((tile, cols), lambda i, own_ref: (i, 0))
    shape = jax.ShapeDtypeStruct((rows, cols), F32)
    return pl.pallas_call(
        body, name=name,
        grid_spec=pltpu.PrefetchScalarGridSpec(
            num_scalar_prefetch=1, grid=(rows // tile,),
            in_specs=[pl.BlockSpec((1, tile, cols), lambda i, own_ref: (own_ref[0], i, 0)),
                      pl.BlockSpec((3, tile, cols), lambda i, own_ref: (0, i, 0)), flat, flat, flat],
            out_specs=[flat] * 4),
        out_shape=[shape] * 4,
        compiler_params=_params(("arbitrary",)),
    )(own, part, others, w, m, v)


def _all_reduce_small(vec):
    gathered = _all_gather([vec], "ar_gather")[0]
    rows = vec.shape[0]

    def fn(*blocks):
        s = blocks[0]
        for b in blocks[1:]:
            s = s + b
        return s

    return _rowcall("ar_sum", fn, [gathered[j] for j in range(N_DEV)], [], [(1024, F32)],
                    tile=_pick_rows(rows))[0]


def _pad_rows(flat, mult):
    n = flat.shape[-1]
    per = mult * 1024
    padded = -(-n // per) * per
    pad = [(0, 0)] * (flat.ndim - 1) + [(0, padded - n)]
    return jnp.pad(flat, pad).reshape(flat.shape[:-1] + (padded // 1024, 1024))


def _regroup_w_in(w):
    pad = jnp.zeros((w.shape[0], 116), w.dtype)
    return jnp.concatenate([w[:, 768:1280], w[:, 1286:1798], w[:, 1280:1286], w[:, 2950:2956], pad,
                            w[:, 0:768], w[:, 1798:2950]], axis=1)


def _ungroup_w_in(wp):
    return jnp.concatenate([wp[:, 1152:1920], wp[:, 0:512], wp[:, 1024:1030], wp[:, 512:1024],
                            wp[:, 1920:3072], wp[:, 1030:1036]], axis=1)


def _to_shard(name, a):
    if name == 'w_in':
        return _regroup_w_in(a)
    if name in ('w_gate', 'w_up'):
        return jnp.pad(a, ((0, 0), (0, FF_SHARD_P - FF_SHARD)))
    if name == 'w_down':
        return jnp.pad(a, ((0, FF_SHARD_P - FF_SHARD), (0, 0)))
    return a


def _from_shard(name, a):
    if name == 'w_in':
        return _ungroup_w_in(a)
    if name in ('w_gate', 'w_up'):
        return a[:, 0:FF_SHARD]
    if name == 'w_down':
        return a[0:FF_SHARD, :]
    return a


def _whole(name, gathered):
    if BIG[name][0] == 1:
        return gathered.reshape(-1, gathered.shape[-1])
    return gathered.transpose(1, 0, 2).reshape(gathered.shape[1], -1)


def _split(name, whole):
    if BIG[name][0] == 1:
        return whole.reshape(N_DEV, whole.shape[0] // N_DEV, whole.shape[1])
    return whole.reshape(whole.shape[0], N_DEV, whole.shape[1] // N_DEV).transpose(1, 0, 2)


def _pack_list(arrays, mult):
    return _pad_rows(jnp.concatenate([a.reshape(-1) for a in arrays]), mult)


def _unpack_list(buf, shapes):
    flat = buf.reshape(-1)
    out, off = [], 0
    for s in shapes:
        n = math.prod(s)
        out.append(flat[off:off + n].reshape(s))
        off += n
    return out


def _adamw(w, g, m, v):
    return _rowcall("adamw", _adam_math, [w, g, m, v], [], [(1024, F32)] * 3, tile=_pick_rows(w.shape[0]))


def _to_heads(a, nb, seq):
    return a.reshape(nb, seq, -1, HEAD_DIM).transpose(0, 2, 1, 3)


def _from_heads(a):
    nb, nh, seq, e = a.shape
    return a.transpose(0, 2, 1, 3).reshape(nb * seq, nh * e)


def _block_diag(w):
    out = jnp.zeros((LRU_WIDTH, LRU_WIDTH), w.dtype)
    for g in range(4):
        out = lax.dynamic_update_slice(out, w[g], (64 * g, 64 * g))
    return out


def _block_diag_grad(full):
    return jnp.stack([full[64 * g:64 * (g + 1), 64 * g:64 * (g + 1)] for g in range(4)])


def _row(v):
    return v.reshape(1, -1).astype(F32)


def _lane128(*pieces):
    flat = jnp.concatenate([p.reshape(-1).astype(F32) for p in pieces])
    return jnp.pad(flat, (0, 128 - flat.shape[0])).reshape(1, 128)


def _layer_consts(w):
    c = {}
    cw, cb = w['ssd_conv_w'], w['ssd_conv_b']
    c['cw_x'], c['cw_b'], c['cw_c'] = cw[:, 0:384], cw[:, 384:640], cw[:, 640:896]
    c['cb_x'], c['cb_b'], c['cb_c'] = _row(cb[0:384]), _row(cb[384:640]), _row(cb[640:896])
    c['bias128'] = _lane128(w['ssd_dt_bias'], w['fox_b_f'])
    c['alog128'] = _lane128(w['ssd_a_log'])
    c['d384'] = _row(jnp.repeat(w['ssd_d'], HEAD_DIM))
    c['lcw'], c['lcb'] = w['lru_conv_w'], _row(w['lru_conv_b'])
    c['wa'], c['wx'] = _block_diag(w['lru_w_a']).astype(BF16), _block_diag(w['lru_w_x']).astype(BF16)
    c['ba'], c['bx'], c['lam'] = _row(w['lru_b_a']), _row(w['lru_b_x']), _row(w['lru_lambda'])
    return c


def _layer_fwd(h0, p_i, w, c, hooks=None, layer=0):
    nb, seq = c['nb'], c['seq']

    def carried(stage):
        return hooks.fwd_side(layer, stage) if hooks is not None else None

    def arrived(outs):
        if hooks is not None:
            hooks.fwd_done(outs)
            w.update(hooks.weights(layer))

    u1 = _rowcall("norm1", lambda h, g: _rms(h, g), [h0], [_row(w['norm1_g'])], [(D_MODEL, BF16)])[0]
    proj = _matmul(u1, w['w_in'], 'nn', "proj")

    xs_c = _convsilu_fwd("conv_x", (proj, 384, 4), seq, c['cw_x'], c['cb_x'], F32)
    b_c = _convsilu_fwd("conv_b", (proj, 256, 0), seq, c['cw_b'], c['cb_b'], BF16)
    c_c = _convsilu_fwd("conv_c", (proj, 256, 1), seq, c['cw_c'], c['cb_c'], BF16)
    dt_arr, cum, prev, end = _small_fwd((proj, 128, 8), seq, c['bias128'], c['alog128'])
    x_h = xs_c.reshape(nb, seq, SSD_WIDTH)
    cum3 = cum.reshape(nb, seq, 128)
    a_row = cum3[:, :, 0:8].transpose(0, 2, 1)
    ssd_in = (x_h, b_c.reshape(nb, seq, 256), c_c.reshape(nb, seq, 256), dt_arr.reshape(nb, seq, 128), cum3,
              prev.reshape(nb, seq, 128), end.reshape(nb, seq, 128), a_row)
    y_h, states = _ssd_fwd(*ssd_in)
    y_core = y_h.reshape(nb * seq, SSD_WIDTH)

    hseq, xl = _lru_fwd(proj, seq, c['lcw'], c['lcb'], c['wa'], c['ba'], c['wx'], c['bx'], c['lam'])

    q = _to_heads((proj[:, 1920:2304] * FOX_SCALE).astype(BF16), nb, seq)
    k = _to_heads(proj[:, 2304:2688].astype(BF16), nb, seq)
    v = _to_heads(proj[:, 2688:3072].astype(BF16), nb, seq)
    cf = cum3[:, :, SSD_HEADS:2 * SSD_HEADS].transpose(0, 2, 1)
    cf_col, cf_row = cf[..., None], cf[:, :, None, :]
    o_h, lse, *side_out = _fox_fwd(q, k, v, cf_col, cf_row, carried('attention'))
    arrived(side_out)
    y_fox = _from_heads(o_h)

    def post(yc, xs, z, hs, lg, yf, d, g1, g2, g3):
        y1 = _rms((yc + xs * d) * _silu(z), g1)
        y2 = _rms(hs * _gelu(lg), g2)
        y3 = _rms(yf, g3)
        return jnp.concatenate([y1, y2, y3], axis=-1)

    post_consts = [c['d384'], _row(w['ssd_norm_g']), _row(w['lru_norm_g']), _row(w['fox_norm_g'])]
    ycat = _rowcall("mix_post", post, [y_core, xs_c, (proj, 384, 3), hseq, (proj, 256, 3), y_fox], post_consts,
                    [(D_MODEL, BF16)])[0]
    mix = _matmul(ycat, w['w_out'], 'nn', "mix_out")

    def res_norm(h, d, g):
        hn = h + d
        return hn, _rms(hn, g)

    h1, u2 = _rowcall("res_norm2", res_norm, [h0, mix], [_row(w['norm2_g'])], [(D_MODEL, F32), (D_MODEL, BF16)])
    side = carried('ffn_in')
    if side is None:
        gu = _matmul(u2, w['w_gu'], 'nn', "ffn_in")
    else:
        gu, *side_out = _matmul(u2, w['w_gu'], 'nn', "ffn_in", side=side)
        arrived(side_out)
    act = _rowcall("swiglu", lambda gt, up: _silu(gt) * up, [(gu, D_FF_P, 0), (gu, D_FF_P, 1)], [], [(D_FF_P, BF16)])[0]
    ff = _matmul(act, w['w_down'], 'nn', "ffn_out")
    h2, u3 = _rowcall("res_norm3", res_norm, [h1, ff], [_row(w['norm3_g'])], [(D_MODEL, F32), (D_MODEL, BF16)])
    pg = _matmul(u3, w['w_ple_gate'], 'nn', "ple_gate")
    pp = _matmul(p_i, w['w_ple_proj'], 'nn', "ple_proj")
    h3 = _rowcall("ple", lambda h, a, b, bias: h + _sigmoid(a + bias) * b, [h2, pg, pp], [_row(w['b_ple_gate'])],
                  [(D_MODEL, F32)])[0]
    saved = dict(h0=h0, u1=u1, proj=proj, xs_c=xs_c, dt_arr=dt_arr, ssd_in=ssd_in, states=states,
                 y_core=y_core, hseq=hseq, xl=xl, q=q, k=k, v=v, cf_col=cf_col, cf_row=cf_row, o_h=o_h, lse=lse,
                 y_fox=y_fox, post_consts=post_consts, ycat=ycat, h1=h1, u2=u2, gu=gu, act=act, h2=h2, u3=u3,
                 pg=pg, pp=pp, p_i=p_i)
    return h3, saved


def _layer_bwd(dh3, s, w, c, hooks=None, layer=0):
    nb, seq = c['nb'], c['seq']
    g = {}

    def ple_bwd(dh, a, b, bias):
        gate = _sigmoid(a + bias)
        dpg = dh * b * gate * (1.0 - gate)
        return dh * gate, dpg, jnp.sum(dpg, axis=0, keepdims=True)

    dpp, dpg, g['b_ple_gate'] = _rowcall("ple_bwd", ple_bwd, [dh3, s['pg'], s['pp']], [_row(w['b_ple_gate'])],
                                         [(D_MODEL, BF16), (D_MODEL, BF16)], [((1, D_MODEL), F32)])
    g['w_ple_proj'] = _matmul(s['p_i'], dpp, 'tn', "d_w_ple_proj")
    g['w_ple_gate'] = _matmul(s['u3'], dpg, 'tn', "d_w_ple_gate")
    du3 = _matmul(dpg, w['w_ple_gate'], 'nt', "d_u3")

    def norm_bwd(h, du, dh, gain):
        dx, dg = _rms_bwd(h, gain, du)
        dhn = dh + dx
        return dhn, dhn, dg

    dh2, dh2_b, g['norm3_g'] = _rowcall("norm3_bwd", norm_bwd, [s['h2'], du3, dh3], [_row(w['norm3_g'])],
                                        [(D_MODEL, F32), (D_MODEL, BF16)], [((1, D_MODEL), F32)])
    g['w_down'] = _matmul(s['act'], dh2_b, 'tn', "d_w_down")
    dact = _matmul(dh2_b, w['w_down'], 'nt', "d_act")

    def swiglu_bwd(gt, up, da):
        return jnp.concatenate([da * up * _dsilu(gt), da * _silu(gt)], axis=-1)

    dgu = _rowcall("swiglu_bwd", swiglu_bwd, [(s['gu'], D_FF_P, 0), (s['gu'], D_FF_P, 1), dact], [],
                   [(2 * D_FF_P, BF16)])[0]
    g['w_gu'] = _matmul(s['u2'], dgu, 'tn', "d_w_gu")
    du2 = _matmul(dgu, w['w_gu'], 'nt', "d_u2")
    dh1, dh1_b, g['norm2_g'] = _rowcall("norm2_bwd", norm_bwd, [s['h1'], du2, dh2], [_row(w['norm2_g'])],
                                        [(D_MODEL, F32), (D_MODEL, BF16)], [((1, D_MODEL), F32)])
    g['w_out'] = _matmul(s['ycat'], dh1_b, 'tn', "d_w_out")
    dycat = _matmul(dh1_b, w['w_out'], 'nt', "d_ycat")

    def post_bwd(dy, yc, xs, z, hs, lg, yf, d, g1, g2, g3):
        sz = _silu(z)
        ytot = yc + xs * d
        dpre1, dg1 = _rms_bwd(ytot * sz, g1, dy[:, 0:384])
        dytot = dpre1 * sz
        dz = dpre1 * ytot * _dsilu(z)
        dd = jnp.sum(dytot * xs, axis=0, keepdims=True)
        gl = _gelu(lg)
        dpre2, dg2 = _rms_bwd(hs * gl, g2, dy[:, 384:640])
        dyf, dg3 = _rms_bwd(yf, g3, dy[:, 640:1024])
        return dytot, dytot * d, dz, dpre2 * gl, dpre2 * hs * _dgelu(lg), dyf, dd, dg1, dg2, dg3

    (dy_core, dxs_skip, dz, dhseq, dlg, dy_fox, dd384, g['ssd_norm_g'], g['lru_norm_g'], g['fox_norm_g']) = _rowcall(
        "mix_post_bwd", post_bwd,
        [dycat, s['y_core'], s['xs_c'], (s['proj'], 384, 3), s['hseq'], (s['proj'], 256, 3), s['y_fox']],
        s['post_consts'],
        [(384, F32), (384, F32), (384, BF16), (256, F32), (256, BF16), (384, F32)],
        [((1, 384), F32), ((1, 384), F32), ((1, 256), F32), ((1, 384), F32)])
    g['ssd_d'] = dd384.reshape(SSD_HEADS, HEAD_DIM).sum(axis=1)

    do_h = _to_heads(dy_fox, nb, seq)
    side = None
    if hooks is not None:
        ready = {n: g[n] for n in ('w_out', 'w_down', 'w_ple_gate', 'w_ple_proj')}
        ready['w_gate'], ready['w_up'] = g['w_gu'][:, 0:D_FF_P], g['w_gu'][:, D_FF_P:2 * D_FF_P]
        hooks.bwd_ready(layer, ready)
        side = hooks.bwd_side()
    dq_h, dk_h, dv_h, dcf_row, dcf_col, *side_out = _fox_bwd(s['q'], s['k'], s['v'], s['o_h'], s['lse'], do_h,
                                                             s['cf_col'], s['cf_row'], side)
    if hooks is not None:
        hooks.bwd_done(side_out)
    dq, dk, dv = (_from_heads(t).astype(BF16) for t in (dq_h, dk_h, dv_h))

    dx_h, db_c, dc_c, da_arr, dend_arr, ddt_arr = _ssd_bwd(*s['ssd_in'], s['states'],
                                                           dy_core.reshape(nb, seq, SSD_WIDTH))
    dxs_c = dx_h.reshape(nb * seq, SSD_WIDTH) + dxs_skip
    dcf = (dcf_row.reshape(nb, FOX_HEADS, seq) + dcf_col.reshape(nb, FOX_HEADS, seq)).transpose(0, 2, 1)
    dcum = jnp.concatenate([da_arr[:, :, 0:SSD_HEADS], dcf,
                            jnp.zeros((nb, seq, 128 - 2 * SSD_HEADS), F32)], axis=-1).reshape(nb * seq, 128)
    proj = s['proj']
    dxs_raw, dcw_x, dcb_x = _convsilu_bwd("conv_x_bwd", (proj, 384, 4), seq, c['cw_x'], c['cb_x'], dxs_c)
    db_raw, dcw_b, dcb_b = _convsilu_bwd("conv_b_bwd", (proj, 256, 0), seq, c['cw_b'], c['cb_b'],
                                         db_c.reshape(nb * seq, 256))
    dc_raw, dcw_c, dcb_c = _convsilu_bwd("conv_c_bwd", (proj, 256, 1), seq, c['cw_c'], c['cb_c'],
                                         dc_c.reshape(nb * seq, 256))
    dsmall, dbias128, dalog128 = _small_bwd((proj, 128, 8), seq, dcum, dend_arr.reshape(nb * seq, 128),
                                            ddt_arr.reshape(nb * seq, 128), s['dt_arr'], c['bias128'], c['alog128'])
    g['ssd_conv_w'] = jnp.concatenate([dcw_x, dcw_b, dcw_c], axis=1)
    g['ssd_conv_b'] = jnp.concatenate([dcb_x, dcb_b, dcb_c], axis=1).reshape(-1)
    g['ssd_dt_bias'] = dbias128[0, 0:SSD_HEADS]
    g['fox_b_f'] = dbias128[0, SSD_HEADS:2 * SSD_HEADS]
    g['ssd_a_log'] = dalog128[0, 0:SSD_HEADS]

    (dlru_raw, g['lru_conv_w'], dlcb, dwa, dba, dwx, dbx, dlam) = _lru_bwd(
        s['proj'], seq, s['xl'], s['hseq'], dhseq, c['lcw'], c['lcb'], c['wa'], c['ba'], c['wx'], c['bx'], c['lam'])
    g['lru_conv_b'], g['lru_b_a'], g['lru_b_x'], g['lru_lambda'] = (t.reshape(-1) for t in (dlcb, dba, dbx, dlam))
    g['lru_w_a'], g['lru_w_x'] = _block_diag_grad(dwa), _block_diag_grad(dwx)

    dproj = jnp.concatenate([db_raw, dc_raw, dlru_raw, dlg, dsmall, dz, dxs_raw, dq, dk, dv], axis=1)
    g['w_in'] = _matmul(s['u1'], dproj, 'tn', "d_w_in")
    du1 = _matmul(dproj, w['w_in'], 'nt', "d_u1")

    def norm1_bwd(h, du, dh, gain):
        dx, dg = _rms_bwd(h, gain, du)
        return dh + dx, dg

    dh0, g['norm1_g'] = _rowcall("norm1_bwd", norm1_bwd, [s['h0'], du1, dh1], [_row(w['norm1_g'])],
                                 [(D_MODEL, F32)], [((1, D_MODEL), F32)])
    for name in ('b_ple_gate', 'norm3_g', 'norm2_g', 'norm1_g', 'ssd_norm_g', 'lru_norm_g', 'fox_norm_g'):
        g[name] = g[name].reshape(-1)
    if hooks is not None:
        hooks.bwd_ready(layer, {'w_in': g['w_in']})
    wgu = g.pop('w_gu')
    g['w_gate'], g['w_up'] = wgu[:, 0:D_FF_P], wgu[:, D_FF_P:2 * D_FF_P]
    return dh0, g


class _Hooks:
    def __init__(self, shard):
        self.shard = shard
        self.whole = {}
        self.part, self.others = {}, {}
        self.pending, self.flying = [], []

    def first(self, extra):
        got = _all_gather([self.shard['w_in', 0]] + extra, "gather_first")
        self.whole['w_in', 0] = _whole('w_in', got[0])
        return got[1:]

    def fwd_side(self, layer, stage):
        if layer != 0:
            return None
        if stage == 'attention':
            self.flying = [(n, 0) for n in BIG if n != 'w_in']
        else:
            self.flying = [(n, 1) for n in BIG]
        return _spread_side([self.shard[k] for k in self.flying])

    def fwd_done(self, outs):
        if self.flying:
            passed = _run_side(_pass_side(outs), "gather_pass_%s%d" % self.flying[0], in_place=True)
            for k, arr in zip(self.flying, passed):
                self.whole[k] = _whole(k[0], arr)
            self.flying = []

    def weights(self, layer):
        w = {n: self.whole[n, layer] for n in BIG if (n, layer) in self.whole}
        if 'w_gate' in w:
            w['w_gu'] = jnp.concatenate([w['w_gate'], w['w_up']], axis=1)
        return w

    def bwd_ready(self, layer, grads):
        keys = [(n, layer) for n in grads]
        full = [_split(n, grads[n]) for n in grads]
        got = _run_side(_sibling_side(full), "rs_sibling_%s%d" % keys[0])
        for k, f, r in zip(keys, full, got):
            self.part[k] = _pair_sum(f, r, "rs_pair_sum_%s%d" % k)
        self.pending += keys

    def bwd_side(self):
        self.flying, self.pending = self.pending, []
        return _chip_side([self.part[k] for k in self.flying]) if self.flying else None

    def bwd_done(self, outs):
        for k, o in zip(self.flying, outs):
            self.others[k] = o
        self.flying = []

    def flush(self):
        side = self.bwd_side()
        if side is not None:
            self.bwd_done(_run_side(side, "rs_chips_last"))


def _local_step(x, p, target, big, small, hooks=None):
    nb, seq, _ = x.shape
    tokens = nb * seq
    h = x.reshape(tokens, D_MODEL)
    layers, saves = [], []
    for i in range(DEPTH):
        w = {name: small[name][i] for name in small if name != 'final_norm_g'}
        if hooks is not None:
            w.update(hooks.weights(i))
        else:
            for name in ('w_in', 'w_out', 'w_down', 'w_ple_gate', 'w_ple_proj'):
                w[name] = big[name][i]
            w['w_gu'] = jnp.concatenate([big['w_gate'][i], big['w_up'][i]], axis=1)
        c = _layer_consts(w)
        c['nb'], c['seq'] = nb, seq
        h, s = _layer_fwd(h, p[i].reshape(tokens, PLE_DIM).astype(BF16), w, c, hooks, i)
        layers.append((w, c))
        saves.append(s)

    def head(hf, tgt, gain):
        r = lax.rsqrt(jnp.mean(hf * hf, axis=-1, keepdims=True) + EPS)
        xhat = hf * r
        err = xhat * gain - tgt
        loss = 0.5 * jnp.sum(jnp.mean(err * err, axis=-1, keepdims=True), axis=0, keepdims=True)
        dy = err * (1.0 / D_MODEL)
        dg = jnp.sum(dy * xhat, axis=0, keepdims=True)
        dxhat = dy * gain
        dh = r * (dxhat - xhat * jnp.mean(dxhat * xhat, axis=-1, keepdims=True))
        return dh, jnp.broadcast_to(loss, (1, 128)), dg

    dh, loss128, dgf = _rowcall("loss_head", head, [h, target.reshape(tokens, D_MODEL)],
                                [_row(small['final_norm_g'])], [(D_MODEL, F32)],
                                [((1, 128), F32), ((1, D_MODEL), F32)])
    grads = {'final_norm_g': dgf.reshape(-1)}
    per_layer = [None] * DEPTH
    for i in range(DEPTH - 1, -1, -1):
        w, c = layers[i]
        dh, per_layer[i] = _layer_bwd(dh, saves[i], w, c, hooks, i)
    for name in per_layer[0]:
        if name in BIG:
            grads[name] = [per_layer[i][name] for i in range(DEPTH)]
        else:
            grads[name] = jnp.stack([per_layer[i][name] for i in range(DEPTH)])
    return loss128[0, 0], dh.reshape(nb, seq, D_MODEL), grads


def kernel(x, p, norm1_g, w_in, ssd_conv_w, ssd_conv_b, ssd_dt_bias, ssd_a_log, ssd_d, ssd_norm_g, lru_conv_w, lru_conv_b, lru_w_a, lru_b_a, lru_w_x, lru_b_x, lru_lambda, lru_norm_g, fox_b_f, fox_norm_g, w_out, norm2_g, w_gate, w_up, w_down, norm3_g, w_ple_gate, b_ple_gate, w_ple_proj, final_norm_g, loss_target, m_norm1_g, m_w_in, m_ssd_conv_w, m_ssd_conv_b, m_ssd_dt_bias, m_ssd_a_log, m_ssd_d, m_ssd_norm_g, m_lru_conv_w, m_lru_conv_b, m_lru_w_a, m_lru_b_a, m_lru_w_x, m_lru_b_x, m_lru_lambda, m_lru_norm_g, m_fox_b_f, m_fox_norm_g, m_w_out, m_norm2_g, m_w_gate, m_w_up, m_w_down, m_norm3_g, m_w_ple_gate, m_b_ple_gate, m_w_ple_proj, m_final_norm_g, v_norm1_g, v_w_in, v_ssd_conv_w, v_ssd_conv_b, v_ssd_dt_bias, v_ssd_a_log, v_ssd_d, v_ssd_norm_g, v_lru_conv_w, v_lru_conv_b, v_lru_w_a, v_lru_b_a, v_lru_w_x, v_lru_b_x, v_lru_lambda, v_lru_norm_g, v_fox_b_f, v_fox_norm_g, v_w_out, v_norm2_g, v_w_gate, v_w_up, v_w_down, v_norm3_g, v_w_ple_gate, v_b_ple_gate, v_w_ple_proj, v_final_norm_g):
    args = dict(locals())
    w_loc = {n: args[n] for n in WEIGHTS}
    m_loc = {n: args['m_' + n] for n in WEIGHTS}
    v_loc = {n: args['v_' + n] for n in WEIGHTS}
    dev = 4 * lax.axis_index("x") + 2 * lax.axis_index("y") + lax.axis_index("c")

    keys = [(n, i) for n in BIG for i in range(DEPTH)]
    conv_names = list(CONV_SHARDED)
    conv_loc_shapes = [w_loc[n].shape for n in conv_names]
    hooks = _Hooks({(n, i): _to_shard(n, w_loc[n][i]).astype(BF16) for n, i in keys})
    conv_all, = hooks.first([_pack_list([w_loc[n] for n in conv_names], 8)])
    small = {n: w_loc[n] for n in WEIGHTS if n not in BIG and n not in CONV_SHARDED}
    per_dev = [_unpack_list(conv_all[j], conv_loc_shapes) for j in range(N_DEV)]
    for idx, n in enumerate(conv_names):
        small[n] = jnp.concatenate([per_dev[j][idx] for j in range(N_DEV)], axis=2)

    loss_part, dx, grads = _local_step(x, p, loss_target, None, small, hooks)
    loss = lax.psum(loss_part, ("x", "y", "c"))
    hooks.flush()
    out = {kind: {n: [None] * DEPTH for n in BIG} for kind in ('g', 'delta', 'm', 'v')}
    for n, i in keys:
        res = _sum_adamw(hooks.part[n, i], hooks.others[n, i],
                         *[_to_shard(n, d[n][i]) for d in (w_loc, m_loc, v_loc)], "sum_adamw_%s%d" % (n, i))
        for kind, r in zip(('g', 'delta', 'm', 'v'), res):
            out[kind][n][i] = _from_shard(n, r)

    small_names = [n for n in WEIGHTS if n not in BIG]
    small_shapes = [grads[n].shape for n in small_names]
    g_small = dict(zip(small_names, _unpack_list(
        _all_reduce_small(_pack_list([grads[n] for n in small_names], 8)), small_shapes)))
    for n in CONV_SHARDED:
        width = CONV_SHARDED[n][2] // N_DEV
        g_small[n] = lax.dynamic_slice_in_dim(g_small[n], dev * width, width, axis=2)
    shapes = [w_loc[n].shape for n in small_names]
    packed = [_pack_list([d[n] for n in small_names], 8) for d in (w_loc, g_small, m_loc, v_loc)]
    upd = [dict(zip(small_names, _unpack_list(t, shapes))) for t in _adamw(*packed)]
    for kind, d in zip(('g', 'delta', 'm', 'v'), [g_small] + upd):
        for n in small_names:
            out[kind][n] = d[n]
        for n in BIG:
            out[kind][n] = jnp.stack(out[kind][n])
    return (loss, dx, *[out['g'][n] for n in WEIGHTS], *[out['delta'][n] for n in WEIGHTS],
            *[out['m'][n] for n in WEIGHTS], *[out['v'][n] for n in WEIGHTS])
```

```python
import functools
import math

import jax
import jax.numpy as jnp
from jax import lax
from jax.experimental import pallas as pl
from jax.experimental.pallas import tpu as pltpu

F32 = jnp.float32
BF16 = jnp.bfloat16

N_DEV = 8
D_MODEL = 1024
DEPTH = 2
HEAD_DIM = 64
SSD_WIDTH = 384
LRU_WIDTH = 256
FOX_WIDTH = 384
SSD_HEADS = 6
SSD_STATE = 128
CHUNK = 128
FOX_HEADS = 6
D_FF = 2816
FF_SHARD = D_FF // N_DEV
FF_SHARD_P = 384
D_FF_P = N_DEV * FF_SHARD_P
PLE_DIM = 256
IN_COLS = 2956
PROJ_COLS = 3072
LRU_C = 8.0
EPS = 1e-6
NEG = -1e30

ADAM_LR = 0.001
ADAM_B1 = 0.9
ADAM_B2 = 0.999
ADAM_EPS = 1e-08
ADAM_WD = 0.01
ADAM_STEP = 10

VMEM_LIMIT = 56 * 1024 * 1024

WEIGHTS = ['norm1_g', 'w_in', 'ssd_conv_w', 'ssd_conv_b', 'ssd_dt_bias', 'ssd_a_log', 'ssd_d', 'ssd_norm_g',
           'lru_conv_w', 'lru_conv_b', 'lru_w_a', 'lru_b_a', 'lru_w_x', 'lru_b_x', 'lru_lambda', 'lru_norm_g',
           'fox_b_f', 'fox_norm_g', 'w_out', 'norm2_g', 'w_gate', 'w_up', 'w_down', 'norm3_g', 'w_ple_gate',
           'b_ple_gate', 'w_ple_proj', 'final_norm_g']
BIG = {'w_in': (1, (DEPTH, D_MODEL, IN_COLS)), 'w_out': (1, (DEPTH, D_MODEL, D_MODEL)),
       'w_gate': (2, (DEPTH, D_MODEL, D_FF)), 'w_up': (2, (DEPTH, D_MODEL, D_FF)),
       'w_down': (1, (DEPTH, D_FF, D_MODEL)), 'w_ple_gate': (1, (DEPTH, D_MODEL, D_MODEL)),
       'w_ple_proj': (2, (DEPTH, PLE_DIM, D_MODEL))}
CONV_SHARDED = {'ssd_conv_w': (DEPTH, 4, 896), 'lru_conv_w': (DEPTH, 4, 256)}


def _dot(a, b):
    return jnp.dot(a, b, preferred_element_type=F32)


def _dot_nt(a, b):
    return lax.dot_general(a, b, (((1,), (1,)), ((), ())), preferred_element_type=F32)


def _dot_tn(a, b):
    return lax.dot_general(a, b, (((0,), (0,)), ((), ())), preferred_element_type=F32)


def _params(sem):
    return pltpu.CompilerParams(dimension_semantics=sem, vmem_limit_bytes=VMEM_LIMIT)


def _pick_tile(n, cap):
    if n <= cap:
        return n
    best = 128
    for t in range(128, cap + 1, 128):
        if n % t == 0:
            best = t
    assert n % best == 0, (n, cap)
    return best


def _matmul(a, b, mode, name, out_dtype=F32, side=None):
    if mode == 'tn':
        k_dim, m_dim = a.shape
        n_dim = b.shape[1]
    else:
        m_dim, k_dim = a.shape
        n_dim = b.shape[1] if mode == 'nn' else b.shape[0]
    tm = _pick_tile(m_dim, 512 if mode != 'tn' else 1024)
    tn = _pick_tile(n_dim, 1536 if mode != 'tn' else 1024)
    tk = _pick_tile(k_dim, 3072 if mode != 'tn' else 512)
    nk = k_dim // tk
    grid = (n_dim // tn, m_dim // tm, nk)

    n_in = len(side.inputs) if side else 0
    n_out = len(side.out_shape) if side else 0

    def body(*refs):
        a_ref, b_ref, o_ref, acc_ref = refs[0], refs[1], refs[2 + n_in], refs[3 + n_in + n_out]
        if side is not None:
            side.run(refs[2:2 + n_in] + refs[3 + n_in:3 + n_in + n_out] + refs[4 + n_in + n_out:],
                     *_grid_ends(grid))
        kk = pl.program_id(2)

        @pl.when(kk == 0)
        def _():
            acc_ref[...] = jnp.zeros_like(acc_ref)

        if mode == 'nn':
            acc_ref[...] += _dot(a_ref[...], b_ref[...])
        elif mode == 'nt':
            acc_ref[...] += _dot_nt(a_ref[...], b_ref[...])
        else:
            acc_ref[...] += _dot_tn(a_ref[...], b_ref[...])

        @pl.when(kk == nk - 1)
        def _():
            o_ref[...] = acc_ref[...].astype(o_ref.dtype)

    if mode == 'nn':
        a_spec = pl.BlockSpec((tm, tk), lambda j, i, k: (i, k))
        b_spec = pl.BlockSpec((tk, tn), lambda j, i, k: (k, j))
    elif mode == 'nt':
        a_spec = pl.BlockSpec((tm, tk), lambda j, i, k: (i, k))
        b_spec = pl.BlockSpec((tn, tk), lambda j, i, k: (j, k))
    else:
        a_spec = pl.BlockSpec((tk, tm), lambda j, i, k: (k, i))
        b_spec = pl.BlockSpec((tk, tn), lambda j, i, k: (k, j))
    side_in, side_out = side.specs() if side else ([], [])
    res = pl.pallas_call(
        body, name=name, grid=grid,
        out_shape=[jax.ShapeDtypeStruct((m_dim, n_dim), out_dtype)] + (side.out_shape if side else []),
        in_specs=[a_spec, b_spec] + side_in,
        out_specs=[pl.BlockSpec((tm, tn), lambda j, i, k: (i, j))] + side_out,
        scratch_shapes=[pltpu.VMEM((tm, tn), F32)] + (side.sems if side else []),
        compiler_params=_params(("arbitrary", "arbitrary", "arbitrary") if side
                                else ("parallel", "parallel", "arbitrary")),
    )(a, b, *(side.inputs if side else []))
    return res if side else res[0]


def _rowcall(name, fn, tiled, consts, outs, accs=(), tile=512, scratch=()):
    specs, arrays = [], []
    for t in tiled:
        if isinstance(t, tuple):
            arr, width, blk = t
            specs.append(pl.BlockSpec((tile, width), functools.partial(lambda i, blk: (i, blk), blk=blk)))
        else:
            arr = t
            specs.append(pl.BlockSpec((tile, arr.shape[1]), lambda i: (i, 0)))
        arrays.append(arr)
    rows = arrays[0].shape[0]
    assert rows % tile == 0, (name, rows, tile)
    for c in consts:
        specs.append(pl.BlockSpec(c.shape, lambda i: (0, 0)))
        arrays.append(c)
    n_in, n_out, n_acc = len(arrays), len(outs), len(accs)
    out_shape = [jax.ShapeDtypeStruct((rows, c), dt) for c, dt in outs]
    out_specs = [pl.BlockSpec((tile, c), lambda i: (i, 0)) for c, _ in outs]
    out_shape += [jax.ShapeDtypeStruct(s, dt) for s, dt in accs]
    out_specs += [pl.BlockSpec(s, lambda i: (0, 0)) for s, _ in accs]

    def body(*refs):
        ins = [r[...] for r in refs[:n_in]]
        out_refs = refs[n_in:n_in + n_out]
        acc_refs = refs[n_in + n_out:n_in + n_out + n_acc]
        scr = refs[n_in + n_out + n_acc:]
        res = fn(*ins, *scr)
        if not isinstance(res, (tuple, list)):
            res = (res,)
        assert len(res) == n_out + n_acc, (name, len(res))
        for r, v in zip(out_refs, res[:n_out]):
            r[...] = v.astype(r.dtype)
        if n_acc:
            first = pl.program_id(0) == 0

            @pl.when(first)
            def _():
                for r, v in zip(acc_refs, res[n_out:]):
                    r[...] = v.astype(r.dtype)

            @pl.when(jnp.logical_not(first))
            def _():
                for r, v in zip(acc_refs, res[n_out:]):
                    r[...] += v.astype(r.dtype)

    res = pl.pallas_call(
        body, name=name, grid=(rows // tile,),
        out_shape=out_shape, in_specs=specs, out_specs=out_specs,
        scratch_shapes=list(scratch),
        compiler_params=_params(("arbitrary",)),
    )(*arrays)
    return res


def _sigmoid(x):
    return 1.0 / (1.0 + jnp.exp(-x))


def _softplus(x):
    return jnp.maximum(x, 0.0) + jnp.log(1.0 + jnp.exp(-jnp.abs(x)))


def _silu(x):
    return x * _sigmoid(x)


def _dsilu(x):
    s = _sigmoid(x)
    return s * (1.0 + x * (1.0 - s))


_GELU_C = math.sqrt(2.0 / math.pi)


def _gelu(x):
    return 0.5 * x * (1.0 + jnp.tanh(_GELU_C * (x + 0.044715 * x * x * x)))


def _dgelu(x):
    t = jnp.tanh(_GELU_C * (x + 0.044715 * x * x * x))
    return 0.5 * (1.0 + t) + 0.5 * x * (1.0 - t * t) * _GELU_C * (1.0 + 3.0 * 0.044715 * x * x)


def _neg_expm1(x):
    series = -x * (1.0 + x * (0.5 + x * (1.0 / 6.0 + x * (1.0 / 24.0 + x * (1.0 / 120.0)))))
    return jnp.where(x > -0.03, series, 1.0 - jnp.exp(x))


def _rms(x, g):
    r = lax.rsqrt(jnp.mean(x * x, axis=-1, keepdims=True) + EPS)
    return x * r * g


def _rms_bwd(x, g, dy):
    r = lax.rsqrt(jnp.mean(x * x, axis=-1, keepdims=True) + EPS)
    xhat = x * r
    dg = jnp.sum(dy * xhat, axis=0, keepdims=True)
    dxhat = dy * g
    dx = r * (dxhat - xhat * jnp.mean(dxhat * xhat, axis=-1, keepdims=True))
    return dx, dg


def _row_iota(shape):
    return lax.broadcasted_iota(jnp.int32, shape, 0)


def _shift_down(x, j):
    if j == 0:
        return x
    return jnp.where(_row_iota(x.shape) >= j, pltpu.roll(x, j, 0), 0.0)


def _shift_up(x, j):
    if j == 0:
        return x
    n = x.shape[0]
    return jnp.where(_row_iota(x.shape) < n - j, pltpu.roll(x, n - j, 0), 0.0)


def _conv(x, w, b):
    y = b + w[3:4, :] * x
    for k in range(3):
        y = y + w[k:k + 1, :] * _shift_down(x, 3 - k)
    return y


def _conv_bwd(x, w, dy):
    dx = w[3:4, :] * dy
    dws = []
    for k in range(3):
        dx = dx + w[k:k + 1, :] * _shift_up(dy, 3 - k)
        dws.append(jnp.sum(dy * _shift_down(x, 3 - k), axis=0, keepdims=True))
    dws.append(jnp.sum(dy * x, axis=0, keepdims=True))
    return dx, jnp.concatenate(dws, axis=0), jnp.sum(dy, axis=0, keepdims=True)


def _split3(x):
    hi = x.astype(BF16)
    r1 = x - hi.astype(F32)
    mid = r1.astype(BF16)
    lo = (r1 - mid.astype(F32)).astype(BF16)
    return hi, mid, lo


def _tri_dot(tri, x):
    hi, mid, lo = _split3(x)
    return _dot(tri, hi) + _dot(tri, mid) + _dot(tri, lo)


def _cumsum_rows(x):
    n = x.shape[0] // CHUNK
    r = lax.broadcasted_iota(jnp.int32, (CHUNK, CHUNK), 0)
    c = lax.broadcasted_iota(jnp.int32, (CHUNK, CHUNK), 1)
    tri = (r >= c).astype(BF16)
    carry = jnp.zeros((1, x.shape[1]), F32)
    cums, prevs, ends = [], [], []
    for i in range(n):
        blk = _tri_dot(tri, x[i * CHUNK:(i + 1) * CHUNK]) + carry
        prevs.append(jnp.broadcast_to(carry, blk.shape))
        carry = blk[CHUNK - 1:CHUNK, :]
        ends.append(jnp.broadcast_to(carry, blk.shape))
        cums.append(blk)
    return jnp.concatenate(cums, 0), jnp.concatenate(prevs, 0), jnp.concatenate(ends, 0)


def _rev_cumsum_rows(x):
    n = x.shape[0] // CHUNK
    r = lax.broadcasted_iota(jnp.int32, (CHUNK, CHUNK), 0)
    c = lax.broadcasted_iota(jnp.int32, (CHUNK, CHUNK), 1)
    tri = (r <= c).astype(BF16)
    carry = jnp.zeros((1, x.shape[1]), F32)
    local, whole = [None] * n, [None] * n
    for i in range(n - 1, -1, -1):
        local[i] = _tri_dot(tri, x[i * CHUNK:(i + 1) * CHUNK])
        whole[i] = local[i] + carry
        carry = whole[i][0:1, :]
    return jnp.concatenate(local, 0), jnp.concatenate(whole, 0)


def _convsilu_fwd(name, seg, seq, w, b, dtype):
    return _rowcall(name, lambda raw, w, b: _silu(_conv(raw, w, b)), [seg], [w, b], [(seg[1], dtype)], tile=seq)[0]


def _convsilu_bwd(name, seg, seq, w, b, dy):
    def fn(raw, dy, w, b):
        return _conv_bwd(raw, w, dy * _dsilu(_conv(raw, w, b)))

    width = seg[1]
    return _rowcall(name, fn, [seg, dy], [w, b], [(width, BF16)], [((4, width), F32), ((1, width), F32)], tile=seq)


def _small_fwd(seg, seq, bias128, alog128):
    def fn(small, bias, alog):
        lane = lax.broadcasted_iota(jnp.int32, small.shape, 1)
        a = -jnp.exp(alog)
        s = small + bias
        dt = _softplus(s)
        logf = -_softplus(-s)
        pre = jnp.where(lane < SSD_HEADS, a * dt, jnp.where(lane < 2 * SSD_HEADS, logf, 0.0))
        cum, prev, end = _cumsum_rows(pre)
        return dt, cum, prev, end

    return _rowcall("small_fwd", fn, [seg], [bias128, alog128], [(128, F32)] * 4, tile=seq)


def _small_bwd(seg, seq, dcum, dend, ddt, dt_arr, bias128, alog128):
    def fn(small, dcum, dend, ddt, dt_arr, bias, alog):
        lane = lax.broadcasted_iota(jnp.int32, small.shape, 1)
        a = -jnp.exp(alog)
        sig = _sigmoid(small + bias)
        local, whole = _rev_cumsum_rows(dcum)
        dadt = local + dend
        d_dt = ddt + a * dadt
        ds = jnp.where(lane < SSD_HEADS, d_dt * sig, jnp.where(lane < 2 * SSD_HEADS, whole * (1.0 - sig), 0.0))
        da = jnp.sum(jnp.where(lane < SSD_HEADS, dadt * dt_arr, 0.0), axis=0, keepdims=True)
        return ds, jnp.sum(ds, axis=0, keepdims=True), da * a

    return _rowcall("small_bwd", fn, [seg, dcum, dend, ddt, dt_arr], [bias128, alog128], [(128, BF16)],
                    [((1, 128), F32), ((1, 128), F32)], tile=seq)


HEAD_PAIRS = SSD_HEADS // 2


def _ssd_specs(nc, reverse):
    def at(c):
        return nc - 1 - c if reverse else c

    x_spec = pl.BlockSpec((1, CHUNK, SSD_WIDTH), lambda b, c: (b, at(c), 0))
    bc_spec = pl.BlockSpec((1, CHUNK, 256), lambda b, c: (b, at(c), 0))
    col_spec = pl.BlockSpec((1, CHUNK, 128), lambda b, c: (b, at(c), 0))
    row_spec = pl.BlockSpec((1, 8, CHUNK), lambda b, c: (b, 0, at(c)))
    st_spec = pl.BlockSpec((1, 1, HEAD_PAIRS, SSD_STATE, 128), lambda b, c: (b, at(c), 0, 0, 0))
    return x_spec, bc_spec, col_spec, row_spec, st_spec


def _ssd_head(h, dtb, acb, apb, aeb, arb):
    return dtb[:, h:h + 1], acb[:, h:h + 1], apb[:, h:h + 1], aeb[:, h:h + 1], arb[h:h + 1, :]


def _ssd_fwd(x, bm, cm, dt_arr, cum, prev, end, a_row):
    nb, seq, _ = x.shape
    nc = seq // CHUNK
    x_spec, bc_spec, col_spec, row_spec, st_spec = _ssd_specs(nc, False)

    def body(x_ref, b_ref, c_ref, dt_ref, ac_ref, ap_ref, ae_ref, ar_ref, y_ref, st_ref, s_scr):
        @pl.when(pl.program_id(1) == 0)
        def _():
            s_scr[...] = jnp.zeros_like(s_scr)

        causal = (lax.broadcasted_iota(jnp.int32, (CHUNK, CHUNK), 0)
                  >= lax.broadcasted_iota(jnp.int32, (CHUNK, CHUNK), 1))
        low = lax.broadcasted_iota(jnp.int32, (CHUNK, 128), 1) < HEAD_DIM
        cols = (dt_ref[0], ac_ref[0], ap_ref[0], ae_ref[0], ar_ref[0])
        bcs = [b_ref[0, :, g * 128:(g + 1) * 128] for g in range(2)]
        ccs = [c_ref[0, :, g * 128:(g + 1) * 128] for g in range(2)]
        ms = [_dot_nt(ccs[g], bcs[g]) for g in range(2)]
        for pi in range(HEAD_PAIRS):
            x2 = x_ref[0, :, pi * 128:(pi + 1) * 128]
            dt2 = jnp.where(low, cols[0][:, 2 * pi:2 * pi + 1], cols[0][:, 2 * pi + 1:2 * pi + 2])
            xdt = (x2 * dt2).astype(BF16)
            sprev = s_scr[pi]
            st_ref[0, 0, pi] = sprev
            spb = sprev.astype(BF16)
            ys, us = [], []
            for h in (2 * pi, 2 * pi + 1):
                g = h // 3
                _, ac, ap, ae, ar = _ssd_head(h, *cols)
                lm = jnp.exp(jnp.where(causal, ac - ar, NEG))
                gm = (ms[g] * lm).astype(BF16)
                ys.append(_dot(gm, xdt) + jnp.exp(ac - ap) * _dot(ccs[g], spb))
                bdec = (bcs[g].astype(F32) * jnp.exp(ae - ac)).astype(BF16)
                us.append(jnp.exp(ae[0:1, :] - ap[0:1, :]) * sprev + _dot_tn(bdec, xdt))
            y_ref[0, :, pi * 128:(pi + 1) * 128] = jnp.where(low, ys[0], ys[1])
            s_scr[pi] = jnp.where(low, us[0], us[1])

    return pl.pallas_call(
        body, name="ssd_fwd", grid=(nb, nc),
        out_shape=[jax.ShapeDtypeStruct(x.shape, F32),
                   jax.ShapeDtypeStruct((nb, nc, HEAD_PAIRS, SSD_STATE, 128), F32)],
        in_specs=[x_spec, bc_spec, bc_spec, col_spec, col_spec, col_spec, col_spec, row_spec],
        out_specs=[x_spec, st_spec],
        scratch_shapes=[pltpu.VMEM((HEAD_PAIRS, SSD_STATE, 128), F32)],
        compiler_params=_params(("parallel", "arbitrary")),
    )(x, bm, cm, dt_arr, cum, prev, end, a_row)


def _ssd_bwd(x_h, bm, cm, dt_arr, cum, prev, end, a_row, states, dy_h):
    nb, seq, _ = x_h.shape
    nc = seq // CHUNK
    x_spec, bc_spec, col_spec, row_spec, st_spec = _ssd_specs(nc, True)

    def body(x_ref, b_ref, c_ref, dt_ref, ac_ref, ap_ref, ae_ref, ar_ref, st_ref, dy_ref,
             dx_ref, db_ref, dc_ref, da_ref, dend_ref, ddt_ref, ds_scr):
        @pl.when(pl.program_id(1) == 0)
        def _():
            ds_scr[...] = jnp.zeros_like(ds_scr)

        causal = (lax.broadcasted_iota(jnp.int32, (CHUNK, CHUNK), 0)
                  >= lax.broadcasted_iota(jnp.int32, (CHUNK, CHUNK), 1))
        lane = lax.broadcasted_iota(jnp.int32, (CHUNK, 128), 1)
        low = lane < HEAD_DIM
        cols = (dt_ref[0], ac_ref[0], ap_ref[0], ae_ref[0], ar_ref[0])
        bcs = [b_ref[0, :, g * 128:(g + 1) * 128] for g in range(2)]
        ccs = [c_ref[0, :, g * 128:(g + 1) * 128] for g in range(2)]
        ms = [_dot_nt(ccs[g], bcs[g]) for g in range(2)]
        dms = [jnp.zeros((CHUNK, CHUNK), F32) for _ in range(2)]
        dc_accs = [jnp.zeros((CHUNK, SSD_STATE), F32) for _ in range(2)]
        db_accs = [jnp.zeros((CHUNK, SSD_STATE), F32) for _ in range(2)]
        da_blk = jnp.zeros((CHUNK, 128), F32)
        dend_blk = jnp.zeros((CHUNK, 128), F32)
        ddt_blk = jnp.zeros((CHUNK, 128), F32)
        for pi in range(HEAD_PAIRS):
            x2 = x_ref[0, :, pi * 128:(pi + 1) * 128]
            dy2 = dy_ref[0, :, pi * 128:(pi + 1) * 128]
            dt2 = jnp.where(low, cols[0][:, 2 * pi:2 * pi + 1], cols[0][:, 2 * pi + 1:2 * pi + 2])
            xdt_f = x2 * dt2
            xdt = xdt_f.astype(BF16)
            dyb = dy2.astype(BF16)
            dsn = ds_scr[pi]
            dsb = dsn.astype(BF16)
            sprev_f = st_ref[0, 0, pi]
            sprev = sprev_f.astype(BF16)
            dxdts, dss = [], []
            for h in (2 * pi, 2 * pi + 1):
                g = h // 3
                mine = low if h % 2 == 0 else jnp.logical_not(low)
                _, ac, ap, ae, ar = _ssd_head(h, *cols)
                bc, cc, m = bcs[g], ccs[g], ms[g]
                lm = jnp.exp(jnp.where(causal, ac - ar, NEG))
                gm = (m * lm).astype(BF16)
                dy_m = jnp.where(mine, dy2, 0.0)
                dyb_m = dy_m.astype(BF16)
                xdt_m = jnp.where(mine, xdt_f, 0.0)
                e_in = jnp.exp(ac - ap)
                f_out = jnp.exp(ae - ac)
                whole = jnp.exp(ae[0:1, :] - ap[0:1, :])
                dg = _dot_nt(dyb_m, xdt)
                dxdt_off = f_out * _dot(bc, dsb)
                dxdt = _dot_tn(gm, dyb) + dxdt_off
                dmj = dg * lm
                dms[g] = dms[g] + dmj
                dc_accs[g] = dc_accs[g] + e_in * _dot_nt(dyb_m, sprev)
                db_accs[g] = db_accs[g] + f_out * _dot_nt(xdt_m.astype(BF16), dsb)
                dss.append(whole * dsn + _dot_tn((cc.astype(F32) * e_in).astype(BF16), dyb))
                wmat = dmj * m
                r_in = jnp.sum(dy_m * (e_in * _dot(cc, sprev)), axis=1, keepdims=True)
                q_out = jnp.sum(xdt_m * dxdt_off, axis=1, keepdims=True)
                daj = (jnp.sum(wmat, axis=1, keepdims=True) - jnp.sum(wmat.T, axis=1, keepdims=True)
                       + r_in - q_out)
                cross = jnp.where(mine, dsn * sprev_f, 0.0)
                dendj = (jnp.sum(q_out, axis=0, keepdims=True)
                         + whole * jnp.sum(jnp.sum(cross, axis=1, keepdims=True), axis=0, keepdims=True))
                ddtj = jnp.sum(jnp.where(mine, dxdt * x2, 0.0), axis=1, keepdims=True)
                dxdts.append(dxdt)
                da_blk = jnp.where(lane == h, daj, da_blk)
                dend_blk = jnp.where(lane == h, dendj, dend_blk)
                ddt_blk = jnp.where(lane == h, ddtj, ddt_blk)
            dx_ref[0, :, pi * 128:(pi + 1) * 128] = jnp.where(low, dxdts[0], dxdts[1]) * dt2
            ds_scr[pi] = jnp.where(low, dss[0], dss[1])
        for g in range(2):
            dmb = dms[g].astype(BF16)
            dc_ref[0, :, g * 128:(g + 1) * 128] = dc_accs[g] + _dot(dmb, bcs[g])
            db_ref[0, :, g * 128:(g + 1) * 128] = db_accs[g] + _dot_tn(dmb, ccs[g])
        da_ref[0] = da_blk
        dend_ref[0] = dend_blk
        ddt_ref[0] = ddt_blk

    col_shape = jax.ShapeDtypeStruct((nb, seq, 128), F32)
    return pl.pallas_call(
        body, name="ssd_bwd", grid=(nb, nc),
        out_shape=[jax.ShapeDtypeStruct(x_h.shape, F32),
                   jax.ShapeDtypeStruct((nb, seq, 256), F32), jax.ShapeDtypeStruct((nb, seq, 256), F32),
                   col_shape, col_shape, col_shape],
        in_specs=[x_spec, bc_spec, bc_spec, col_spec, col_spec, col_spec, col_spec, row_spec, st_spec, x_spec],
        out_specs=[x_spec, bc_spec, bc_spec, col_spec, col_spec, col_spec],
        scratch_shapes=[pltpu.VMEM((HEAD_PAIRS, SSD_STATE, 128), F32)],
        compiler_params=_params(("parallel", "arbitrary")),
    )(x_h, bm, cm, dt_arr, cum, prev, end, a_row, states, dy_h)


def _lru_gates(xl, wa, ba, wx, bx, lam):
    xb = xl.astype(BF16)
    r = _sigmoid(_dot(xb, wa) + ba)
    i = _sigmoid(_dot(xb, wx) + bx)
    sp = _softplus(-lam)
    log_a = -LRU_C * r * sp
    a = jnp.exp(log_a)
    mult = jnp.sqrt(_neg_expm1(2.0 * log_a))
    return r, i, sp, log_a, a, mult


def _scan_chunks(a_ref, u_ref, h_ref, seq, reverse):
    nc = seq // CHUNK
    width = a_ref.shape[1]
    row = lax.broadcasted_iota(jnp.int32, (CHUNK, width), 0)

    def chunk(ci, carry):
        c = nc - 1 - ci if reverse else ci
        rows = pl.ds(pl.multiple_of(c * CHUNK, CHUNK), CHUNK)
        av, bv = a_ref[rows, :], u_ref[rows, :]
        d = 1
        while d < CHUNK:
            if reverse:
                keep = row < CHUNK - d
                a_sh = jnp.where(keep, pltpu.roll(av, CHUNK - d, 0), 1.0)
                b_sh = jnp.where(keep, pltpu.roll(bv, CHUNK - d, 0), 0.0)
            else:
                keep = row >= d
                a_sh = jnp.where(keep, pltpu.roll(av, d, 0), 1.0)
                b_sh = jnp.where(keep, pltpu.roll(bv, d, 0), 0.0)
            bv = av * b_sh + bv
            av = av * a_sh
            d *= 2
        hv = bv + av * carry
        h_ref[rows, :] = hv
        return hv[0:1, :] if reverse else hv[CHUNK - 1:CHUNK, :]

    lax.fori_loop(0, nc, chunk, jnp.zeros((1, width), F32))


def _lru_fwd(proj, seq, cw, cb, wa, ba, wx, bx, lam):
    def fn(raw, cw, cb, wa, ba, wx, bx, lam, a_scr, u_scr, h_scr):
        xl = _conv(raw, cw, cb)
        r, i, sp, log_a, a, mult = _lru_gates(xl, wa, ba, wx, bx, lam)
        a_scr[...] = a
        u_scr[...] = mult * (i * xl)
        _scan_chunks(a_scr, u_scr, h_scr, seq, reverse=False)
        return h_scr[...], xl

    return _rowcall("lru_fwd", fn, [(proj, 256, 2)], [cw, cb, wa, ba, wx, bx, lam],
                    [(256, F32), (256, F32)], tile=seq,
                    scratch=[pltpu.VMEM((seq, 256), F32)] * 3)


def _lru_bwd(proj, seq, xl_all, h_all, dh_all, cw, cb, wa, ba, wx, bx, lam):
    def fn(raw, xl, hseq, dh, cw, cb, wa, ba, wx, bx, lam, a_scr, u_scr, h_scr):
        r, i, sp, log_a, a, mult = _lru_gates(xl, wa, ba, wx, bx, lam)
        a_scr[...] = _shift_up(a, 1)
        u_scr[...] = dh
        _scan_chunks(a_scr, u_scr, h_scr, seq, reverse=True)
        dht = h_scr[...]
        da = dht * _shift_down(hseq, 1)
        gated = i * xl
        dgated = dht * mult
        dmult = dht * gated
        dlog_a = da * a - dmult * (a * a) / mult
        dr = dlog_a * (-LRU_C * sp)
        dsp = jnp.sum(dlog_a * (-LRU_C * r), axis=0, keepdims=True)
        dlam = -dsp * _sigmoid(-lam)
        dpa = dr * r * (1.0 - r)
        dpx = (dgated * xl) * i * (1.0 - i)
        dpa_b, dpx_b = dpa.astype(BF16), dpx.astype(BF16)
        dxl = dgated * i + _dot_nt(dpa_b, wa) + _dot_nt(dpx_b, wx)
        xb = xl.astype(BF16)
        dwa = _dot_tn(xb, dpa_b)
        dwx = _dot_tn(xb, dpx_b)
        draw, dcw, dcb = _conv_bwd(raw, cw, dxl)
        return (draw, dcw, dcb, dwa, jnp.sum(dpa, axis=0, keepdims=True), dwx,
                jnp.sum(dpx, axis=0, keepdims=True), dlam)

    return _rowcall("lru_bwd", fn, [(proj, 256, 2), xl_all, h_all, dh_all], [cw, cb, wa, ba, wx, bx, lam],
                    [(256, BF16)],
                    [((4, 256), F32), ((1, 256), F32), ((256, 256), F32), ((1, 256), F32), ((256, 256), F32),
                     ((1, 256), F32), ((1, 256), F32)],
                    tile=seq, scratch=[pltpu.VMEM((seq, 256), F32)] * 3)


FOX_SCALE = HEAD_DIM ** -0.5
FOX_BLOCK = 512


class _Side:
    def __init__(self, inputs, out_shape, sems, build):
        self.inputs, self.out_shape, self.sems, self.build = list(inputs), list(out_shape), list(sems), build

    def specs(self):
        any_spec = pl.BlockSpec(memory_space=pl.ANY)
        return [any_spec] * len(self.inputs), [any_spec] * len(self.out_shape)

    def run(self, refs, first, last):
        n_in, n_out = len(self.inputs), len(self.out_shape)
        in_refs, out_refs, sem_refs = refs[:n_in], refs[n_in:n_in + n_out], refs[n_in + n_out:]

        @pl.when(first)
        def _():
            for cp in self.build(in_refs, out_refs, sem_refs):
                cp.start()

        @pl.when(last)
        def _():
            for cp in self.build(in_refs, out_refs, sem_refs):
                cp.wait()


def _grid_ends(grid):
    ids = [pl.program_id(a) for a in range(len(grid))]
    first = functools.reduce(jnp.logical_and, [i == 0 for i in ids])
    last = functools.reduce(jnp.logical_and, [i == n - 1 for i, n in zip(ids, grid)])
    return first, last


Q_BLK, K_BLK, V_BLK = 1920 // 128, 2304 // 128, 2688 // 128
FOX_PAIRS = FOX_HEADS // 2


def _fox_bias(cum, nb, seq):
    cf = cum.reshape(nb, seq, 128)[:, :, SSD_HEADS:SSD_HEADS + FOX_HEADS]
    cols = jnp.pad(cf.reshape(nb * seq, FOX_PAIRS, 2), ((0, 0), (0, 0), (0, 126))).reshape(nb * seq, 384)
    rows = jnp.pad(cf.transpose(0, 2, 1).reshape(nb, FOX_PAIRS, 2, seq), ((0, 0), (0, 0), (0, 6), (0, 0)))
    return cols, rows


def _fox_fwd(proj, bias_cols, bias_rows, nb, seq, side=None):
    tb = min(FOX_BLOCK, seq)
    nq = seq // tb
    grid = (nb, FOX_PAIRS, nq)
    n_in = len(side.inputs) if side else 0

    def body(*refs):
        q_ref, k_ref, v_ref, cq_ref, ck_ref = refs[:5]
        o_ref, lse_ref = refs[5 + n_in:7 + n_in]
        if side is not None:
            side.run(refs[5:5 + n_in] + refs[7 + n_in:], *_grid_ends(grid))
        qi = pl.program_id(2)
        low = lax.broadcasted_iota(jnp.int32, (tb, 128), 1) < HEAD_DIM
        q2 = q_ref[...] * FOX_SCALE
        qm = [jnp.where(low, q2, 0.0).astype(BF16), jnp.where(low, 0.0, q2).astype(BF16)]
        cqs = [cq_ref[:, 0:1], cq_ref[:, 1:2]]

        def block(j, carry, diagonal):
            cols = pl.ds(pl.multiple_of(j * tb, tb), tb)
            k2 = k_ref[cols, :].astype(BF16)
            v2 = v_ref[cols, :].astype(BF16)
            new = []
            for hh in range(2):
                m_i, l_i, acc = carry[hh]
                s = _dot_nt(qm[hh], k2) + cqs[hh] - ck_ref[0, 0, hh:hh + 1, cols]
                if diagonal:
                    s = jnp.where(lax.broadcasted_iota(jnp.int32, (tb, tb), 0)
                                  >= lax.broadcasted_iota(jnp.int32, (tb, tb), 1), s, NEG)
                m_new = jnp.maximum(m_i, jnp.max(s, axis=1, keepdims=True))
                p = jnp.exp(s - m_new)
                alpha = jnp.exp(m_i - m_new)
                new.append((m_new, alpha * l_i + jnp.sum(p, axis=1, keepdims=True),
                            alpha * acc + _dot(p.astype(BF16), v2)))
            return tuple(new)

        one = (jnp.full((tb, 1), NEG, F32), jnp.zeros((tb, 1), F32), jnp.zeros((tb, 128), F32))
        carry = lax.fori_loop(0, qi, lambda j, cr: block(j, cr, False), (one, one))
        (m0, l0, a0), (m1, l1, a1) = block(qi, carry, True)
        o_ref[...] = jnp.where(low, a0 / l0, a1 / l1)
        lse_ref[...] = jnp.where(low, m0 + jnp.log(l0), m1 + jnp.log(l1))

    def blk(first):
        return pl.BlockSpec((tb, 128), lambda b, p, i: (b * nq + i, first + p))

    def seq_blk(first):
        return pl.BlockSpec((seq, 128), lambda b, p, i: (b, first + p))

    row_spec = pl.BlockSpec((1, 1, 8, seq), lambda b, p, i: (b, p, 0, 0))
    side_in, side_out = side.specs() if side else ([], [])
    shape = jax.ShapeDtypeStruct((nb * seq, FOX_WIDTH), F32)
    return pl.pallas_call(
        body, name="fox_fwd", grid=grid,
        out_shape=[shape, shape] + (side.out_shape if side else []),
        in_specs=[blk(Q_BLK), seq_blk(K_BLK), seq_blk(V_BLK), blk(0), row_spec] + side_in,
        out_specs=[blk(0), blk(0)] + side_out,
        scratch_shapes=side.sems if side else [],
        compiler_params=_params(("arbitrary", "arbitrary", "arbitrary")),
    )(proj, proj, proj, bias_cols, bias_rows, *(side.inputs if side else []))


def _fox_bwd(proj, o, lse, do, bias_cols, bias_rows, nb, seq, side=None):
    tb = min(FOX_BLOCK, seq)
    nq = seq // tb
    grid = (nb, FOX_PAIRS, nq)
    n_in = len(side.inputs) if side else 0

    def body(*refs):
        q_ref, k_ref, v_ref, o_ref, lse_ref, do_ref, cq_ref, ck_ref = refs[:8]
        dq_ref, dk_ref, dv_ref, dcum_ref, dcq_ref = refs[8 + n_in:13 + n_in]
        if side is not None:
            side.run(refs[8:8 + n_in] + refs[13 + n_in:], *_grid_ends(grid))
        kj = pl.program_id(2)

        @pl.when(kj == 0)
        def _():
            dq_ref[...] = jnp.zeros_like(dq_ref)
            dcq_ref[...] = jnp.zeros_like(dcq_ref)

        low = lax.broadcasted_iota(jnp.int32, (tb, 128), 1) < HEAD_DIM
        mine = [low, jnp.logical_not(low)]
        k2 = k_ref[...]
        kb = k2.astype(BF16)
        km = [jnp.where(mine[hh], k2, 0.0).astype(BF16) for hh in range(2)]
        vb = v_ref[...].astype(BF16)

        def block(i, carry, diagonal):
            dk, dv, c0, c1 = carry
            csum = [c0, c1]
            rows = pl.ds(pl.multiple_of(i * tb, tb), tb)
            q2 = q_ref[rows, :] * FOX_SCALE
            do2 = do_ref[rows, :]
            prod = do2 * o_ref[rows, :]
            dq_add = jnp.zeros((tb, 128), F32)
            rsum = []
            for hh in range(2):
                qm = jnp.where(mine[hh], q2, 0.0).astype(BF16)
                dom = jnp.where(mine[hh], do2, 0.0).astype(BF16)
                delta = jnp.sum(jnp.where(mine[hh], prod, 0.0), axis=1, keepdims=True)
                s = _dot_nt(qm, kb) + cq_ref[rows, hh:hh + 1] - ck_ref[0, 0, hh:hh + 1, :]
                if diagonal:
                    s = jnp.where(lax.broadcasted_iota(jnp.int32, (tb, tb), 0)
                                  >= lax.broadcasted_iota(jnp.int32, (tb, tb), 1), s, NEG)
                p = jnp.exp(s - lse_ref[rows, HEAD_DIM * hh:HEAD_DIM * hh + 1])
                ds = p * (_dot_nt(dom, vb) - delta)
                dsb = ds.astype(BF16)
                dv = dv + _dot_tn(p.astype(BF16), dom)
                dk = dk + _dot_tn(dsb, qm)
                dq_add = dq_add + _dot(dsb, km[hh])
                rsum.append(jnp.sum(ds, axis=1, keepdims=True))
                csum[hh] = csum[hh] + jnp.sum(ds, axis=0, keepdims=True)
            dq_ref[rows, :] += dq_add * FOX_SCALE
            dcq_ref[rows, :] += jnp.where(low, rsum[0], rsum[1])
            return dk, dv, csum[0], csum[1]

        init = (jnp.zeros((tb, 128), F32), jnp.zeros((tb, 128), F32), jnp.zeros((1, tb), F32),
                jnp.zeros((1, tb), F32))
        carry = block(kj, init, True)
        dk, dv, c0, c1 = lax.fori_loop(kj + 1, nq, lambda i, cr: block(i, cr, False), carry)
        dk_ref[...] = dk.astype(dk_ref.dtype)
        dv_ref[...] = dv.astype(dv_ref.dtype)
        row = lax.broadcasted_iota(jnp.int32, (8, tb), 0)
        dcum_ref[0, 0] = jnp.where(row == 0, -c0, jnp.where(row == 1, -c1, 0.0))

    def blk(first):
        return pl.BlockSpec((tb, 128), lambda b, p, j: (b * nq + j, first + p))

    def seq_blk(first):
        return pl.BlockSpec((seq, 128), lambda b, p, j: (b, first + p))

    row_blk = pl.BlockSpec((1, 1, 8, tb), lambda b, p, j: (b, p, 0, j))
    side_in, side_out = side.specs() if side else ([], [])
    tokens = nb * seq
    return pl.pallas_call(
        body, name="fox_bwd", grid=grid,
        out_shape=[jax.ShapeDtypeStruct((tokens, FOX_WIDTH), F32), jax.ShapeDtypeStruct((tokens, FOX_WIDTH), BF16),
                   jax.ShapeDtypeStruct((tokens, FOX_WIDTH), BF16),
                   jax.ShapeDtypeStruct((nb, FOX_PAIRS, 8, seq), F32),
                   jax.ShapeDtypeStruct((tokens, FOX_WIDTH), F32)] + (side.out_shape if side else []),
        in_specs=[seq_blk(Q_BLK), blk(K_BLK), blk(V_BLK), seq_blk(0), seq_blk(0), seq_blk(0), seq_blk(0), row_blk]
        + side_in,
        out_specs=[seq_blk(0), blk(0), blk(0), row_blk, seq_blk(0)] + side_out,
        scratch_shapes=side.sems if side else [],
        compiler_params=_params(("arbitrary", "arbitrary", "arbitrary")),
    )(proj, proj, proj, o, lse, do, bias_cols, bias_rows, *(side.inputs if side else []))


_ANY = pl.BlockSpec(memory_space=pl.ANY)


def _place():
    return lax.axis_index("x"), lax.axis_index("y"), lax.axis_index("c")


def _all_gather(shards, name):
    n = len(shards)

    def body(*refs):
        x_refs, out_refs = refs[:n], refs[n:2 * n]
        send_sems, recv_sems, local_sems = refs[2 * n:]
        x, y, c = _place()
        me, sibling = (x, y, c), (x, y, 1 - c)
        chips = [(1 - x, y), (x, 1 - y), (1 - x, 1 - y)]

        def rows(a, px, py, pc):
            return out_refs[a].at[4 * px + 2 * py + pc]

        def copy(a, k, block, to, src=None):
            return pltpu.make_async_remote_copy(
                src_ref=rows(a, *block) if src is None else src, dst_ref=rows(a, *block),
                send_sem=send_sems.at[a, k], recv_sem=recv_sems.at[a, k],
                device_id=to, device_id_type=pl.DeviceIdType.MESH)

        mine = [pltpu.make_async_copy(x_refs[a], rows(a, *me), local_sems.at[a]) for a in range(n)]
        for cp in mine:
            cp.start()
        first = []
        for a in range(n):
            first.append(copy(a, 0, me, sibling, src=x_refs[a]))
            first += [copy(a, 1 + j, me, (*chip, c), src=x_refs[a]) for j, chip in enumerate(chips)]
        for cp in first:
            cp.start()
        passed = []
        for j, chip in enumerate(chips):
            for a in range(n):
                copy(a, 1 + j, (*chip, c), me).wait_recv()
                passed.append(copy(a, 4 + j, (*chip, c), sibling))
                passed[-1].start()
        for a in range(n):
            copy(a, 0, sibling, me).wait_recv()
            for j, chip in enumerate(chips):
                copy(a, 4 + j, (*chip, 1 - c), me).wait_recv()
        for cp in first + passed:
            cp.wait_send()
        for cp in mine:
            cp.wait()

    return pl.pallas_call(
        body, name=name,
        out_shape=[jax.ShapeDtypeStruct((N_DEV,) + s.shape, s.dtype) for s in shards],
        in_specs=[_ANY] * n, out_specs=[_ANY] * n,
        scratch_shapes=[pltpu.SemaphoreType.DMA((n, 7)), pltpu.SemaphoreType.DMA((n, 7)),
                        pltpu.SemaphoreType.DMA((n,))],
    )(*shards)


def _remote(src, dst, send_sem, recv_sem, to):
    return pltpu.make_async_remote_copy(src_ref=src, dst_ref=dst, send_sem=send_sem, recv_sem=recv_sem,
                                        device_id=to, device_id_type=pl.DeviceIdType.MESH)


def _sem_pairs(n, k):
    return [pltpu.SemaphoreType.DMA((n, k)), pltpu.SemaphoreType.DMA((n, k))]


def _sibling_side(full):
    def build(g_refs, out_refs, sems):
        x, y, c = _place()
        return [_remote(g_refs[a].at[4 * (k // 2) + 2 * (k % 2) + (1 - c)], out_refs[a].at[k],
                        sems[0].at[a, k], sems[1].at[a, k], (x, y, 1 - c))
                for a in range(len(g_refs)) for k in range(4)]

    return _Side(full, [jax.ShapeDtypeStruct((4,) + f.shape[1:], f.dtype) for f in full],
                 _sem_pairs(len(full), 4), build)


def _chip_side(part):
    def build(p_refs, out_refs, sems):
        x, y, c = _place()
        peers = [(1 - x, y), (x, 1 - y), (1 - x, 1 - y)]
        return [_remote(p_refs[a].at[2 * px + py], out_refs[a].at[k], sems[0].at[a, k], sems[1].at[a, k],
                        (px, py, c))
                for a in range(len(p_refs)) for k, (px, py) in enumerate(peers)]

    return _Side(part, [jax.ShapeDtypeStruct((3,) + p.shape[1:], p.dtype) for p in part],
                 _sem_pairs(len(part), 3), build)


def _spread_side(shards):
    def build(x_refs, out_refs, sems):
        x, y, c = _place()
        targets = [(x, y, 1 - c), (1 - x, y, c), (x, 1 - y, c), (1 - x, 1 - y, c)]
        cps = []
        for a in range(len(x_refs)):
            slot = out_refs[a].at[4 * x + 2 * y + c]
            cps.append(pltpu.make_async_copy(x_refs[a], slot, sems[2].at[a]))
            cps += [_remote(x_refs[a], slot, sems[0].at[a, k], sems[1].at[a, k], to)
                    for k, to in enumerate(targets)]
        return cps

    n = len(shards)
    return _Side(shards, [jax.ShapeDtypeStruct((N_DEV,) + s.shape, s.dtype) for s in shards],
                 _sem_pairs(n, 4) + [pltpu.SemaphoreType.DMA((n,))], build)


def _pass_side(bufs):
    def build(in_refs, out_refs, sems):
        x, y, c = _place()
        chips = [(1 - x, y), (x, 1 - y), (1 - x, 1 - y)]
        return [_remote(in_refs[a].at[4 * px + 2 * py + c], out_refs[a].at[4 * px + 2 * py + c],
                        sems[0].at[a, j], sems[1].at[a, j], (x, y, 1 - c))
                for a in range(len(in_refs)) for j, (px, py) in enumerate(chips)]

    return _Side(bufs, [jax.ShapeDtypeStruct(b.shape, b.dtype) for b in bufs], _sem_pairs(len(bufs), 3), build)


def _run_side(side, name, in_place=False):
    n_in = len(side.inputs)

    def body(*refs):
        copies = side.build(refs[:n_in], refs[n_in:n_in + len(side.out_shape)],
                            refs[n_in + len(side.out_shape):])
        for cp in copies:
            cp.start()
        for cp in copies:
            cp.wait()

    in_specs, out_specs = side.specs()
    return pl.pallas_call(
        body, name=name, out_shape=side.out_shape, in_specs=in_specs, out_specs=out_specs,
        scratch_shapes=side.sems,
        input_output_aliases={a: a for a in range(n_in)} if in_place else {},
    )(*side.inputs)


def _pick_rows(rows, cap=512):
    t = cap
    while t >= 8:
        if rows % t == 0:
            return t
        t //= 2
    raise ValueError(rows)


def _pair_sum(full, got, name):
    _, rows, cols = full.shape
    tile = _pick_rows(rows, 256)
    c = lax.axis_index("c").astype(jnp.int32).reshape(1)

    def body(c_ref, a_ref, b_ref, o_ref):
        o_ref[...] = a_ref[...] + b_ref[...]

    blk = (1, tile, cols)
    return pl.pallas_call(
        body, name=name,
        grid_spec=pltpu.PrefetchScalarGridSpec(
            num_scalar_prefetch=1, grid=(4, rows // tile),
            in_specs=[pl.BlockSpec(blk, lambda k, i, c_ref: (4 * (k // 2) + 2 * (k % 2) + c_ref[0], i, 0)),
                      pl.BlockSpec(blk, lambda k, i, c_ref: (k, i, 0))],
            out_specs=pl.BlockSpec(blk, lambda k, i, c_ref: (k, i, 0))),
        out_shape=jax.ShapeDtypeStruct((4, rows, cols), full.dtype),
        compiler_params=_params(("arbitrary", "arbitrary")),
    )(c, full, got)


def _adam_math(w, g, m, v):
    c1 = 1.0 / (1.0 - ADAM_B1 ** ADAM_STEP)
    c2 = 1.0 / (1.0 - ADAM_B2 ** ADAM_STEP)
    m_new = ADAM_B1 * m + (1.0 - ADAM_B1) * g
    v_new = ADAM_B2 * v + (1.0 - ADAM_B2) * (g * g)
    delta = -ADAM_LR * ((m_new * c1) / (jnp.sqrt(v_new * c2) + ADAM_EPS) + ADAM_WD * w)
    return delta, m_new, v_new


def _sum_adamw(part, others, w, m, v, name):
    _, rows, cols = part.shape
    tile = _pick_rows(rows, 128)
    own = (2 * lax.axis_index("x") + lax.axis_index("y")).astype(jnp.int32).reshape(1)

    def body(own_ref, p_ref, o_ref, w_ref, m_ref, v_ref, g_out, d_out, m_out, v_out):
        g = ((p_ref[0] + o_ref[0]) + o_ref[1]) + o_ref[2]
        delta, m_new, v_new = _adam_math(w_ref[...], g, m_ref[...], v_ref[...])
        g_out[...] = g
        d_out[...] = delta
        m_out[...] = m_new
        v_out[...] = v_new

    flat = pl.BlockSpec((tile, cols), lambda i, own_ref: (i, 0))
    shape = jax.ShapeDtypeStruct((rows, cols), F32)
    return pl.pallas_call(
        body, name=name,
        grid_spec=pltpu.PrefetchScalarGridSpec(
            num_scalar_prefetch=1, grid=(rows // tile,),
            in_specs=[pl.BlockSpec((1, tile, cols), lambda i, own_ref: (own_ref[0], i, 0)),
                      pl.BlockSpec((3, tile, cols), lambda i, own_ref: (0, i, 0)), flat, flat, flat],
            out_specs=[flat] * 4),
        out_shape=[shape] * 4,
        compiler_params=_params(("arbitrary",)),
    )(own, part, others, w, m, v)


def _all_reduce_small(vec):
    gathered = _all_gather([vec], "ar_gather")[0]
    rows = vec.shape[0]

    def fn(*blocks):
        s = blocks[0]
        for b in blocks[1:]:
            s = s + b
        return s

    return _rowcall("ar_sum", fn, [gathered[j] for j in range(N_DEV)], [], [(1024, F32)],
                    tile=_pick_rows(rows))[0]


def _pad_rows(flat, mult):
    n = flat.shape[-1]
    per = mult * 1024
    padded = -(-n // per) * per
    pad = [(0, 0)] * (flat.ndim - 1) + [(0, padded - n)]
    return jnp.pad(flat, pad).reshape(flat.shape[:-1] + (padded // 1024, 1024))


def _regroup_w_in(w):
    pad = jnp.zeros((w.shape[0], 116), w.dtype)
    return jnp.concatenate([w[:, 768:1280], w[:, 1286:1798], w[:, 1280:1286], w[:, 2950:2956], pad,
                            w[:, 0:768], w[:, 1798:2950]], axis=1)


def _ungroup_w_in(wp):
    return jnp.concatenate([wp[:, 1152:1920], wp[:, 0:512], wp[:, 1024:1030], wp[:, 512:1024],
                            wp[:, 1920:3072], wp[:, 1030:1036]], axis=1)


def _to_shard(name, a):
    if name == 'w_in':
        return _regroup_w_in(a)
    if name in ('w_gate', 'w_up'):
        return jnp.pad(a, ((0, 0), (0, FF_SHARD_P - FF_SHARD)))
    if name == 'w_down':
        return jnp.pad(a, ((0, FF_SHARD_P - FF_SHARD), (0, 0)))
    return a


def _from_shard(name, a):
    if name == 'w_in':
        return _ungroup_w_in(a)
    if name in ('w_gate', 'w_up'):
        return a[:, 0:FF_SHARD]
    if name == 'w_down':
        return a[0:FF_SHARD, :]
    return a


def _whole(name, gathered):
    if BIG[name][0] == 1:
        return gathered.reshape(-1, gathered.shape[-1])
    return gathered.transpose(1, 0, 2).reshape(gathered.shape[1], -1)


def _split(name, whole):
    if BIG[name][0] == 1:
        return whole.reshape(N_DEV, whole.shape[0] // N_DEV, whole.shape[1])
    return whole.reshape(whole.shape[0], N_DEV, whole.shape[1] // N_DEV).transpose(1, 0, 2)


def _pack_list(arrays, mult):
    return _pad_rows(jnp.concatenate([a.reshape(-1) for a in arrays]), mult)


def _unpack_list(buf, shapes):
    flat = buf.reshape(-1)
    out, off = [], 0
    for s in shapes:
        n = math.prod(s)
        out.append(flat[off:off + n].reshape(s))
        off += n
    return out


def _adamw(w, g, m, v):
    return _rowcall("adamw", _adam_math, [w, g, m, v], [], [(1024, F32)] * 3, tile=_pick_rows(w.shape[0]))


def _block_diag(w):
    out = jnp.zeros((LRU_WIDTH, LRU_WIDTH), w.dtype)
    for g in range(4):
        out = lax.dynamic_update_slice(out, w[g], (64 * g, 64 * g))
    return out


def _block_diag_grad(full):
    return jnp.stack([full[64 * g:64 * (g + 1), 64 * g:64 * (g + 1)] for g in range(4)])


def _row(v):
    return v.reshape(1, -1).astype(F32)


def _lane128(*pieces):
    flat = jnp.concatenate([p.reshape(-1).astype(F32) for p in pieces])
    return jnp.pad(flat, (0, 128 - flat.shape[0])).reshape(1, 128)


def _layer_consts(w):
    c = {}
    cw, cb = w['ssd_conv_w'], w['ssd_conv_b']
    c['cw_x'], c['cw_b'], c['cw_c'] = cw[:, 0:384], cw[:, 384:640], cw[:, 640:896]
    c['cb_x'], c['cb_b'], c['cb_c'] = _row(cb[0:384]), _row(cb[384:640]), _row(cb[640:896])
    c['bias128'] = _lane128(w['ssd_dt_bias'], w['fox_b_f'])
    c['alog128'] = _lane128(w['ssd_a_log'])
    c['d384'] = _row(jnp.repeat(w['ssd_d'], HEAD_DIM))
    c['lcw'], c['lcb'] = w['lru_conv_w'], _row(w['lru_conv_b'])
    c['wa'], c['wx'] = _block_diag(w['lru_w_a']).astype(BF16), _block_diag(w['lru_w_x']).astype(BF16)
    c['ba'], c['bx'], c['lam'] = _row(w['lru_b_a']), _row(w['lru_b_x']), _row(w['lru_lambda'])
    return c


def _layer_fwd(h0, p_i, w, c, hooks=None, layer=0):
    nb, seq = c['nb'], c['seq']

    def carried(stage):
        return hooks.fwd_side(layer, stage) if hooks is not None else None

    def arrived(outs):
        if hooks is not None:
            hooks.fwd_done(outs)
            w.update(hooks.weights(layer))

    u1 = _rowcall("norm1", lambda h, g: _rms(h, g), [h0], [_row(w['norm1_g'])], [(D_MODEL, BF16)])[0]
    proj = _matmul(u1, w['w_in'], 'nn', "proj")

    xs_c = _convsilu_fwd("conv_x", (proj, 384, 4), seq, c['cw_x'], c['cb_x'], F32)
    b_c = _convsilu_fwd("conv_b", (proj, 256, 0), seq, c['cw_b'], c['cb_b'], BF16)
    c_c = _convsilu_fwd("conv_c", (proj, 256, 1), seq, c['cw_c'], c['cb_c'], BF16)
    dt_arr, cum, prev, end = _small_fwd((proj, 128, 8), seq, c['bias128'], c['alog128'])
    x_h = xs_c.reshape(nb, seq, SSD_WIDTH)
    cum3 = cum.reshape(nb, seq, 128)
    a_row = cum3[:, :, 0:8].transpose(0, 2, 1)
    ssd_in = (x_h, b_c.reshape(nb, seq, 256), c_c.reshape(nb, seq, 256), dt_arr.reshape(nb, seq, 128), cum3,
              prev.reshape(nb, seq, 128), end.reshape(nb, seq, 128), a_row)
    y_h, states = _ssd_fwd(*ssd_in)
    y_core = y_h.reshape(nb * seq, SSD_WIDTH)

    hseq, xl = _lru_fwd(proj, seq, c['lcw'], c['lcb'], c['wa'], c['ba'], c['wx'], c['bx'], c['lam'])

    bias_cols, bias_rows = _fox_bias(cum, nb, seq)
    y_fox, lse, *side_out = _fox_fwd(proj, bias_cols, bias_rows, nb, seq, carried('attention'))
    arrived(side_out)

    def post(yc, xs, z, hs, lg, yf, d, g1, g2, g3):
        y1 = _rms((yc + xs * d) * _silu(z), g1)
        y2 = _rms(hs * _gelu(lg), g2)
        y3 = _rms(yf, g3)
        return jnp.concatenate([y1, y2, y3], axis=-1)

    post_consts = [c['d384'], _row(w['ssd_norm_g']), _row(w['lru_norm_g']), _row(w['fox_norm_g'])]
    ycat = _rowcall("mix_post", post, [y_core, xs_c, (proj, 384, 3), hseq, (proj, 256, 3), y_fox], post_consts,
                    [(D_MODEL, BF16)])[0]
    mix = _matmul(ycat, w['w_out'], 'nn', "mix_out")

    def res_norm(h, d, g):
        hn = h + d
        return hn, _rms(hn, g)

    h1, u2 = _rowcall("res_norm2", res_norm, [h0, mix], [_row(w['norm2_g'])], [(D_MODEL, F32), (D_MODEL, BF16)])
    side = carried('ffn_in')
    if side is None:
        gu = _matmul(u2, w['w_gu'], 'nn', "ffn_in", BF16)
    else:
        gu, *side_out = _matmul(u2, w['w_gu'], 'nn', "ffn_in", BF16, side=side)
        arrived(side_out)
    act = _rowcall("swiglu", lambda gt, up: _silu(gt.astype(F32)) * up.astype(F32),
                   [(gu, D_FF_P, 0), (gu, D_FF_P, 1)], [], [(D_FF_P, BF16)])[0]
    ff = _matmul(act, w['w_down'], 'nn', "ffn_out")
    h2, u3 = _rowcall("res_norm3", res_norm, [h1, ff], [_row(w['norm3_g'])], [(D_MODEL, F32), (D_MODEL, BF16)])
    pg = _matmul(u3, w['w_ple_gate'], 'nn', "ple_gate")
    pp = _matmul(p_i, w['w_ple_proj'], 'nn', "ple_proj")
    h3 = _rowcall("ple", lambda h, a, b, bias: h + _sigmoid(a + bias) * b, [h2, pg, pp], [_row(w['b_ple_gate'])],
                  [(D_MODEL, F32)])[0]
    saved = dict(h0=h0, u1=u1, proj=proj, xs_c=xs_c, dt_arr=dt_arr, ssd_in=ssd_in, states=states,
                 y_core=y_core, hseq=hseq, xl=xl, bias_cols=bias_cols, bias_rows=bias_rows, lse=lse,
                 y_fox=y_fox, post_consts=post_consts, ycat=ycat, h1=h1, u2=u2, gu=gu, act=act, h2=h2, u3=u3,
                 pg=pg, pp=pp, p_i=p_i)
    return h3, saved


def _layer_bwd(dh3, s, w, c, hooks=None, layer=0):
    nb, seq = c['nb'], c['seq']
    g = {}

    def ple_bwd(dh, a, b, bias):
        gate = _sigmoid(a + bias)
        dpg = dh * b * gate * (1.0 - gate)
        return dh * gate, dpg, jnp.sum(dpg, axis=0, keepdims=True)

    dpp, dpg, g['b_ple_gate'] = _rowcall("ple_bwd", ple_bwd, [dh3, s['pg'], s['pp']], [_row(w['b_ple_gate'])],
                                         [(D_MODEL, BF16), (D_MODEL, BF16)], [((1, D_MODEL), F32)])
    g['w_ple_proj'] = _matmul(s['p_i'], dpp, 'tn', "d_w_ple_proj")
    g['w_ple_gate'] = _matmul(s['u3'], dpg, 'tn', "d_w_ple_gate")
    du3 = _matmul(dpg, w['w_ple_gate'], 'nt', "d_u3")

    def norm_bwd(h, du, dh, gain):
        dx, dg = _rms_bwd(h, gain, du)
        dhn = dh + dx
        return dhn, dhn, dg

    dh2, dh2_b, g['norm3_g'] = _rowcall("norm3_bwd", norm_bwd, [s['h2'], du3, dh3], [_row(w['norm3_g'])],
                                        [(D_MODEL, F32), (D_MODEL, BF16)], [((1, D_MODEL), F32)])
    g['w_down'] = _matmul(s['act'], dh2_b, 'tn', "d_w_down")
    dact = _matmul(dh2_b, w['w_down'], 'nt', "d_act", BF16)

    def swiglu_bwd(gt, up, da):
        gt, up, da = gt.astype(F32), up.astype(F32), da.astype(F32)
        return jnp.concatenate([da * up * _dsilu(gt), da * _silu(gt)], axis=-1)

    dgu = _rowcall("swiglu_bwd", swiglu_bwd, [(s['gu'], D_FF_P, 0), (s['gu'], D_FF_P, 1), dact], [],
                   [(2 * D_FF_P, BF16)])[0]
    g['w_gu'] = _matmul(s['u2'], dgu, 'tn', "d_w_gu")
    du2 = _matmul(dgu, w['w_gu'], 'nt', "d_u2")
    dh1, dh1_b, g['norm2_g'] = _rowcall("norm2_bwd", norm_bwd, [s['h1'], du2, dh2], [_row(w['norm2_g'])],
                                        [(D_MODEL, F32), (D_MODEL, BF16)], [((1, D_MODEL), F32)])
    g['w_out'] = _matmul(s['ycat'], dh1_b, 'tn', "d_w_out")
    dycat = _matmul(dh1_b, w['w_out'], 'nt', "d_ycat")

    def post_bwd(dy, yc, xs, z, hs, lg, yf, d, g1, g2, g3):
        sz = _silu(z)
        ytot = yc + xs * d
        dpre1, dg1 = _rms_bwd(ytot * sz, g1, dy[:, 0:384])
        dytot = dpre1 * sz
        dz = dpre1 * ytot * _dsilu(z)
        dd = jnp.sum(dytot * xs, axis=0, keepdims=True)
        gl = _gelu(lg)
        dpre2, dg2 = _rms_bwd(hs * gl, g2, dy[:, 384:640])
        dyf, dg3 = _rms_bwd(yf, g3, dy[:, 640:1024])
        return dytot, dytot * d, dz, dpre2 * gl, dpre2 * hs * _dgelu(lg), dyf, dd, dg1, dg2, dg3

    (dy_core, dxs_skip, dz, dhseq, dlg, dy_fox, dd384, g['ssd_norm_g'], g['lru_norm_g'], g['fox_norm_g']) = _rowcall(
        "mix_post_bwd", post_bwd,
        [dycat, s['y_core'], s['xs_c'], (s['proj'], 384, 3), s['hseq'], (s['proj'], 256, 3), s['y_fox']],
        s['post_consts'],
        [(384, F32), (384, F32), (384, BF16), (256, F32), (256, BF16), (384, F32)],
        [((1, 384), F32), ((1, 384), F32), ((1, 256), F32), ((1, 384), F32)])
    g['ssd_d'] = dd384.reshape(SSD_HEADS, HEAD_DIM).sum(axis=1)

    side = None
    if hooks is not None:
        ready = {n: g[n] for n in ('w_out', 'w_down', 'w_ple_gate', 'w_ple_proj')}
        ready['w_gate'], ready['w_up'] = g['w_gu'][:, 0:D_FF_P], g['w_gu'][:, D_FF_P:2 * D_FF_P]
        hooks.bwd_ready(layer, ready)
        side = hooks.bwd_side()
    dq, dk, dv, dcf_rows, dcf_cols, *side_out = _fox_bwd(s['proj'], s['y_fox'], s['lse'], dy_fox, s['bias_cols'],
                                                         s['bias_rows'], nb, seq, side)
    if hooks is not None:
        hooks.bwd_done(side_out)
    dq = dq.astype(BF16)

    dx_h, db_c, dc_c, da_arr, dend_arr, ddt_arr = _ssd_bwd(*s['ssd_in'], s['states'],
                                                           dy_core.reshape(nb, seq, SSD_WIDTH))
    dxs_c = dx_h.reshape(nb * seq, SSD_WIDTH) + dxs_skip
    dcf = (dcf_rows[:, :, 0:2, :].reshape(nb, FOX_HEADS, seq).transpose(0, 2, 1)
           + dcf_cols.reshape(nb, seq, FOX_HEADS, HEAD_DIM)[:, :, :, 0])
    dcum = jnp.concatenate([da_arr[:, :, 0:SSD_HEADS], dcf,
                            jnp.zeros((nb, seq, 128 - 2 * SSD_HEADS), F32)], axis=-1).reshape(nb * seq, 128)
    proj = s['proj']
    dxs_raw, dcw_x, dcb_x = _convsilu_bwd("conv_x_bwd", (proj, 384, 4), seq, c['cw_x'], c['cb_x'], dxs_c)
    db_raw, dcw_b, dcb_b = _convsilu_bwd("conv_b_bwd", (proj, 256, 0), seq, c['cw_b'], c['cb_b'],
                                         db_c.reshape(nb * seq, 256))
    dc_raw, dcw_c, dcb_c = _convsilu_bwd("conv_c_bwd", (proj, 256, 1), seq, c['cw_c'], c['cb_c'],
                                         dc_c.reshape(nb * seq, 256))
    dsmall, dbias128, dalog128 = _small_bwd((proj, 128, 8), seq, dcum, dend_arr.reshape(nb * seq, 128),
                                            ddt_arr.reshape(nb * seq, 128), s['dt_arr'], c['bias128'], c['alog128'])
    g['ssd_conv_w'] = jnp.concatenate([dcw_x, dcw_b, dcw_c], axis=1)
    g['ssd_conv_b'] = jnp.concatenate([dcb_x, dcb_b, dcb_c], axis=1).reshape(-1)
    g['ssd_dt_bias'] = dbias128[0, 0:SSD_HEADS]
    g['fox_b_f'] = dbias128[0, SSD_HEADS:2 * SSD_HEADS]
    g['ssd_a_log'] = dalog128[0, 0:SSD_HEADS]

    (dlru_raw, g['lru_conv_w'], dlcb, dwa, dba, dwx, dbx, dlam) = _lru_bwd(
        s['proj'], seq, s['xl'], s['hseq'], dhseq, c['lcw'], c['lcb'], c['wa'], c['ba'], c['wx'], c['bx'], c['lam'])
    g['lru_conv_b'], g['lru_b_a'], g['lru_b_x'], g['lru_lambda'] = (t.reshape(-1) for t in (dlcb, dba, dbx, dlam))
    g['lru_w_a'], g['lru_w_x'] = _block_diag_grad(dwa), _block_diag_grad(dwx)

    dproj = jnp.concatenate([db_raw, dc_raw, dlru_raw, dlg, dsmall, dz, dxs_raw, dq, dk, dv], axis=1)
    g['w_in'] = _matmul(s['u1'], dproj, 'tn', "d_w_in")
    du1 = _matmul(dproj, w['w_in'], 'nt', "d_u1")

    def norm1_bwd(h, du, dh, gain):
        dx, dg = _rms_bwd(h, gain, du)
        return dh + dx, dg

    dh0, g['norm1_g'] = _rowcall("norm1_bwd", norm1_bwd, [s['h0'], du1, dh1], [_row(w['norm1_g'])],
                                 [(D_MODEL, F32)], [((1, D_MODEL), F32)])
    for name in ('b_ple_gate', 'norm3_g', 'norm2_g', 'norm1_g', 'ssd_norm_g', 'lru_norm_g', 'fox_norm_g'):
        g[name] = g[name].reshape(-1)
    if hooks is not None:
        hooks.bwd_ready(layer, {'w_in': g['w_in']})
    wgu = g.pop('w_gu')
    g['w_gate'], g['w_up'] = wgu[:, 0:D_FF_P], wgu[:, D_FF_P:2 * D_FF_P]
    return dh0, g


class _Hooks:
    def __init__(self, shard):
        self.shard = shard
        self.whole = {}
        self.part, self.others = {}, {}
        self.pending, self.flying = [], []

    def first(self, extra):
        got = _all_gather([self.shard['w_in', 0]] + extra, "gather_first")
        self.whole['w_in', 0] = _whole('w_in', got[0])
        return got[1:]

    def fwd_side(self, layer, stage):
        if layer != 0:
            return None
        if stage == 'attention':
            self.flying = [(n, 0) for n in BIG if n != 'w_in']
        else:
            self.flying = [(n, 1) for n in BIG]
        return _spread_side([self.shard[k] for k in self.flying])

    def fwd_done(self, outs):
        if self.flying:
            passed = _run_side(_pass_side(outs), "gather_pass_%s%d" % self.flying[0], in_place=True)
            for k, arr in zip(self.flying, passed):
                self.whole[k] = _whole(k[0], arr)
            self.flying = []

    def weights(self, layer):
        w = {n: self.whole[n, layer] for n in BIG if (n, layer) in self.whole}
        if 'w_gate' in w:
            w['w_gu'] = jnp.concatenate([w['w_gate'], w['w_up']], axis=1)
        return w

    def bwd_ready(self, layer, grads):
        keys = [(n, layer) for n in grads]
        full = [_split(n, grads[n]) for n in grads]
        got = _run_side(_sibling_side(full), "rs_sibling_%s%d" % keys[0])
        for k, f, r in zip(keys, full, got):
            self.part[k] = _pair_sum(f, r, "rs_pair_sum_%s%d" % k)
        self.pending += keys

    def bwd_side(self):
        self.flying, self.pending = self.pending, []
        return _chip_side([self.part[k] for k in self.flying]) if self.flying else None

    def bwd_done(self, outs):
        for k, o in zip(self.flying, outs):
            self.others[k] = o
        self.flying = []

    def flush(self):
        side = self.bwd_side()
        if side is not None:
            self.bwd_done(_run_side(side, "rs_chips_last"))


def _local_step(x, p, target, big, small, hooks=None):
    nb, seq, _ = x.shape
    tokens = nb * seq
    h = x.reshape(tokens, D_MODEL)
    layers, saves = [], []
    for i in range(DEPTH):
        w = {name: small[name][i] for name in small if name != 'final_norm_g'}
        if hooks is not None:
            w.update(hooks.weights(i))
        else:
            for name in ('w_in', 'w_out', 'w_down', 'w_ple_gate', 'w_ple_proj'):
                w[name] = big[name][i]
            w['w_gu'] = jnp.concatenate([big['w_gate'][i], big['w_up'][i]], axis=1)
        c = _layer_consts(w)
        c['nb'], c['seq'] = nb, seq
        h, s = _layer_fwd(h, p[i].reshape(tokens, PLE_DIM).astype(BF16), w, c, hooks, i)
        layers.append((w, c))
        saves.append(s)

    def head(hf, tgt, gain):
        r = lax.rsqrt(jnp.mean(hf * hf, axis=-1, keepdims=True) + EPS)
        xhat = hf * r
        err = xhat * gain - tgt
        loss = 0.5 * jnp.sum(jnp.mean(err * err, axis=-1, keepdims=True), axis=0, keepdims=True)
        dy = err * (1.0 / D_MODEL)
        dg = jnp.sum(dy * xhat, axis=0, keepdims=True)
        dxhat = dy * gain
        dh = r * (dxhat - xhat * jnp.mean(dxhat * xhat, axis=-1, keepdims=True))
        return dh, jnp.broadcast_to(loss, (1, 128)), dg

    dh, loss128, dgf = _rowcall("loss_head", head, [h, target.reshape(tokens, D_MODEL)],
                                [_row(small['final_norm_g'])], [(D_MODEL, F32)],
                                [((1, 128), F32), ((1, D_MODEL), F32)])
    grads = {'final_norm_g': dgf.reshape(-1)}
    per_layer = [None] * DEPTH
    for i in range(DEPTH - 1, -1, -1):
        w, c = layers[i]
        dh, per_layer[i] = _layer_bwd(dh, saves[i], w, c, hooks, i)
    for name in per_layer[0]:
        if name in BIG:
            grads[name] = [per_layer[i][name] for i in range(DEPTH)]
        else:
            grads[name] = jnp.stack([per_layer[i][name] for i in range(DEPTH)])
    return loss128[0, 0], dh.reshape(nb, seq, D_MODEL), grads


def kernel(x, p, norm1_g, w_in, ssd_conv_w, ssd_conv_b, ssd_dt_bias, ssd_a_log, ssd_d, ssd_norm_g, lru_conv_w, lru_conv_b, lru_w_a, lru_b_a, lru_w_x, lru_b_x, lru_lambda, lru_norm_g, fox_b_f, fox_norm_g, w_out, norm2_g, w_gate, w_up, w_down, norm3_g, w_ple_gate, b_ple_gate, w_ple_proj, final_norm_g, loss_target, m_norm1_g, m_w_in, m_ssd_conv_w, m_ssd_conv_b, m_ssd_dt_bias, m_ssd_a_log, m_ssd_d, m_ssd_norm_g, m_lru_conv_w, m_lru_conv_b, m_lru_w_a, m_lru_b_a, m_lru_w_x, m_lru_b_x, m_lru_lambda, m_lru_norm_g, m_fox_b_f, m_fox_norm_g, m_w_out, m_norm2_g, m_w_gate, m_w_up, m_w_down, m_norm3_g, m_w_ple_gate, m_b_ple_gate, m_w_ple_proj, m_final_norm_g, v_norm1_g, v_w_in, v_ssd_conv_w, v_ssd_conv_b, v_ssd_dt_bias, v_ssd_a_log, v_ssd_d, v_ssd_norm_g, v_lru_conv_w, v_lru_conv_b, v_lru_w_a, v_lru_b_a, v_lru_w_x, v_lru_b_x, v_lru_lambda, v_lru_norm_g, v_fox_b_f, v_fox_norm_g, v_w_out, v_norm2_g, v_w_gate, v_w_up, v_w_down, v_norm3_g, v_w_ple_gate, v_b_ple_gate, v_w_ple_proj, v_final_norm_g):
    args = dict(locals())
    w_loc = {n: args[n] for n in WEIGHTS}
    m_loc = {n: args['m_' + n] for n in WEIGHTS}
    v_loc = {n: args['v_' + n] for n in WEIGHTS}
    dev = 4 * lax.axis_index("x") + 2 * lax.axis_index("y") + lax.axis_index("c")

    keys = [(n, i) for n in BIG for i in range(DEPTH)]
    conv_names = list(CONV_SHARDED)
    conv_loc_shapes = [w_loc[n].shape for n in conv_names]
    hooks = _Hooks({(n, i): _to_shard(n, w_loc[n][i]).astype(BF16) for n, i in keys})
    conv_all, = hooks.first([_pack_list([w_loc[n] for n in conv_names], 8)])
    small = {n: w_loc[n] for n in WEIGHTS if n not in BIG and n not in CONV_SHARDED}
    per_dev = [_unpack_list(conv_all[j], conv_loc_shapes) for j in range(N_DEV)]
    for idx, n in enumerate(conv_names):
        small[n] = jnp.concatenate([per_dev[j][idx] for j in range(N_DEV)], axis=2)

    loss_part, dx, grads = _local_step(x, p, loss_target, None, small, hooks)
    loss = lax.psum(loss_part, ("x", "y", "c"))
    hooks.flush()
    out = {kind: {n: [None] * DEPTH for n in BIG} for kind in ('g', 'delta', 'm', 'v')}
    for n, i in keys:
        res = _sum_adamw(hooks.part[n, i], hooks.others[n, i],
                         *[_to_shard(n, d[n][i]) for d in (w_loc, m_loc, v_loc)], "sum_adamw_%s%d" % (n, i))
        for kind, r in zip(('g', 'delta', 'm', 'v'), res):
            out[kind][n][i] = _from_shard(n, r)

    small_names = [n for n in WEIGHTS if n not in BIG]
    small_shapes = [grads[n].shape for n in small_names]
    g_small = dict(zip(small_names, _unpack_list(
        _all_reduce_small(_pack_list([grads[n] for n in small_names], 8)), small_shapes)))
    for n in CONV_SHARDED:
        width = CONV_SHARDED[n][2] // N_DEV
        g_small[n] = lax.dynamic_slice_in_dim(g_small[n], dev * width, width, axis=2)
    shapes = [w_loc[n].shape for n in small_names]
    packed = [_pack_list([d[n] for n in small_names], 8) for d in (w_loc, g_small, m_loc, v_loc)]
    upd = [dict(zip(small_names, _unpack_list(t, shapes))) for t in _adamw(*packed)]
    for kind, d in zip(('g', 'delta', 'm', 'v'), [g_small] + upd):
        for n in small_names:
            out[kind][n] = d[n]
        for n in BIG:
            out[kind][n] = jnp.stack(out[kind][n])
    return (loss, dx, *[out['g'][n] for n in WEIGHTS], *[out['delta'][n] for n in WEIGHTS],
            *[out['m'][n] for n in WEIGHTS], *[out['v'][n] for n in WEIGHTS])
```

```python
import functools
import math

import jax
import jax.numpy as jnp
from jax import lax
from jax.experimental import pallas as pl
from jax.experimental.pallas import tpu as pltpu

F32 = jnp.float32
BF16 = jnp.bfloat16

N_DEV = 8
D_MODEL = 1024
DEPTH = 2
HEAD_DIM = 64
SSD_WIDTH = 384
LRU_WIDTH = 256
FOX_WIDTH = 384
SSD_HEADS = 6
SSD_STATE = 128
CHUNK = 128
FOX_HEADS = 6
D_FF = 2816
FF_SHARD = D_FF // N_DEV
FF_SHARD_P = 384
D_FF_P = N_DEV * FF_SHARD_P
PLE_DIM = 256
IN_COLS = 2956
PROJ_COLS = 3072
LRU_C = 8.0
EPS = 1e-6
NEG = -1e30

ADAM_LR = 0.001
ADAM_B1 = 0.9
ADAM_B2 = 0.999
ADAM_EPS = 1e-08
ADAM_WD = 0.01
ADAM_STEP = 10

VMEM_LIMIT = 56 * 1024 * 1024

WEIGHTS = ['norm1_g', 'w_in', 'ssd_conv_w', 'ssd_conv_b', 'ssd_dt_bias', 'ssd_a_log', 'ssd_d', 'ssd_norm_g',
           'lru_conv_w', 'lru_conv_b', 'lru_w_a', 'lru_b_a', 'lru_w_x', 'lru_b_x', 'lru_lambda', 'lru_norm_g',
           'fox_b_f', 'fox_norm_g', 'w_out', 'norm2_g', 'w_gate', 'w_up', 'w_down', 'norm3_g', 'w_ple_gate',
           'b_ple_gate', 'w_ple_proj', 'final_norm_g']
BIG = {'w_in': (1, (DEPTH, D_MODEL, IN_COLS)), 'w_out': (1, (DEPTH, D_MODEL, D_MODEL)),
       'w_gate': (2, (DEPTH, D_MODEL, D_FF)), 'w_up': (2, (DEPTH, D_MODEL, D_FF)),
       'w_down': (1, (DEPTH, D_FF, D_MODEL)), 'w_ple_gate': (1, (DEPTH, D_MODEL, D_MODEL)),
       'w_ple_proj': (2, (DEPTH, PLE_DIM, D_MODEL))}
CONV_SHARDED = {'ssd_conv_w': (DEPTH, 4, 896), 'lru_conv_w': (DEPTH, 4, 256)}


def _dot(a, b):
    return jnp.dot(a, b, preferred_element_type=F32)


def _dot_nt(a, b):
    return lax.dot_general(a, b, (((1,), (1,)), ((), ())), preferred_element_type=F32)


def _dot_tn(a, b):
    return lax.dot_general(a, b, (((0,), (0,)), ((), ())), preferred_element_type=F32)


def _params(sem):
    return pltpu.CompilerParams(dimension_semantics=sem, vmem_limit_bytes=VMEM_LIMIT)


def _pick_tile(n, cap):
    if n <= cap:
        return n
    best = 128
    for t in range(128, cap + 1, 128):
        if n % t == 0:
            best = t
    assert n % best == 0, (n, cap)
    return best


def _matmul(a, b, mode, name, out_dtype=F32, side=None):
    if mode == 'tn':
        k_dim, m_dim = a.shape
        n_dim = b.shape[1]
    else:
        m_dim, k_dim = a.shape
        n_dim = b.shape[1] if mode == 'nn' else b.shape[0]
    tm = _pick_tile(m_dim, 512 if mode != 'tn' else 1024)
    tn = _pick_tile(n_dim, 1536 if mode != 'tn' else 1024)
    tk = _pick_tile(k_dim, 3072 if mode != 'tn' else 2048)
    nk = k_dim // tk
    grid = (n_dim // tn, m_dim // tm, nk)

    n_in = len(side.inputs) if side else 0
    n_out = len(side.out_shape) if side else 0

    n_acc = 1 if nk > 1 else 0

    def body(*refs):
        a_ref, b_ref, o_ref = refs[0], refs[1], refs[2 + n_in]
        acc_ref = refs[3 + n_in + n_out] if n_acc else None
        if side is not None:
            side.run(refs[2:2 + n_in] + refs[3 + n_in:3 + n_in + n_out] + refs[3 + n_acc + n_in + n_out:],
                     *_grid_ends(grid))
        kk = pl.program_id(2)
        prod = {'nn': _dot, 'nt': _dot_nt, 'tn': _dot_tn}[mode](a_ref[...], b_ref[...])
        if nk == 1:
            o_ref[...] = prod.astype(o_ref.dtype)
            return

        @pl.when(kk == 0)
        def _():
            acc_ref[...] = prod

        @pl.when(jnp.logical_and(kk > 0, kk < nk - 1))
        def _():
            acc_ref[...] += prod

        @pl.when(kk == nk - 1)
        def _():
            o_ref[...] = (acc_ref[...] + prod).astype(o_ref.dtype)

    if mode == 'nn':
        a_spec = pl.BlockSpec((tm, tk), lambda j, i, k: (i, k))
        b_spec = pl.BlockSpec((tk, tn), lambda j, i, k: (k, j))
    elif mode == 'nt':
        a_spec = pl.BlockSpec((tm, tk), lambda j, i, k: (i, k))
        b_spec = pl.BlockSpec((tn, tk), lambda j, i, k: (j, k))
    else:
        a_spec = pl.BlockSpec((tk, tm), lambda j, i, k: (k, i))
        b_spec = pl.BlockSpec((tk, tn), lambda j, i, k: (k, j))
    side_in, side_out = side.specs() if side else ([], [])
    res = pl.pallas_call(
        body, name=name, grid=grid,
        out_shape=[jax.ShapeDtypeStruct((m_dim, n_dim), out_dtype)] + (side.out_shape if side else []),
        in_specs=[a_spec, b_spec] + side_in,
        out_specs=[pl.BlockSpec((tm, tn), lambda j, i, k: (i, j))] + side_out,
        scratch_shapes=[pltpu.VMEM((tm, tn), F32)] * n_acc + (side.sems if side else []),
        compiler_params=_params(("arbitrary", "arbitrary", "arbitrary") if side
                                else ("parallel", "parallel", "arbitrary")),
    )(a, b, *(side.inputs if side else []))
    return res if side else res[0]


def _rowcall(name, fn, tiled, consts, outs, accs=(), tile=512, scratch=()):
    specs, arrays = [], []
    for t in tiled:
        if isinstance(t, tuple):
            arr, width, blk = t
            specs.append(pl.BlockSpec((tile, width), functools.partial(lambda i, blk: (i, blk), blk=blk)))
        else:
            arr = t
            specs.append(pl.BlockSpec((tile, arr.shape[1]), lambda i: (i, 0)))
        arrays.append(arr)
    rows = arrays[0].shape[0]
    assert rows % tile == 0, (name, rows, tile)
    for c in consts:
        specs.append(pl.BlockSpec(c.shape, lambda i: (0, 0)))
        arrays.append(c)
    n_in, n_out, n_acc = len(arrays), len(outs), len(accs)
    out_shape = [jax.ShapeDtypeStruct((rows, c), dt) for c, dt in outs]
    out_specs = [pl.BlockSpec((tile, c), lambda i: (i, 0)) for c, _ in outs]
    out_shape += [jax.ShapeDtypeStruct(s, dt) for s, dt in accs]
    out_specs += [pl.BlockSpec(s, lambda i: (0, 0)) for s, _ in accs]

    def body(*refs):
        ins = [r[...] for r in refs[:n_in]]
        out_refs = refs[n_in:n_in + n_out]
        acc_refs = refs[n_in + n_out:n_in + n_out + n_acc]
        scr = refs[n_in + n_out + n_acc:]
        res = fn(*ins, *scr)
        if not isinstance(res, (tuple, list)):
            res = (res,)
        assert len(res) == n_out + n_acc, (name, len(res))
        for r, v in zip(out_refs, res[:n_out]):
            r[...] = v.astype(r.dtype)
        if n_acc:
            first = pl.program_id(0) == 0

            @pl.when(first)
            def _():
                for r, v in zip(acc_refs, res[n_out:]):
                    r[...] = v.astype(r.dtype)

            @pl.when(jnp.logical_not(first))
            def _():
                for r, v in zip(acc_refs, res[n_out:]):
                    r[...] += v.astype(r.dtype)

    res = pl.pallas_call(
        body, name=name, grid=(rows // tile,),
        out_shape=out_shape, in_specs=specs, out_specs=out_specs,
        scratch_shapes=list(scratch),
        compiler_params=_params(("arbitrary",)),
    )(*arrays)
    return res


def _sigmoid(x):
    return 1.0 / (1.0 + jnp.exp(-x))


def _softplus(x):
    return jnp.maximum(x, 0.0) + jnp.log(1.0 + jnp.exp(-jnp.abs(x)))


def _silu(x):
    return x * _sigmoid(x)


def _dsilu(x):
    s = _sigmoid(x)
    return s * (1.0 + x * (1.0 - s))


_GELU_C = math.sqrt(2.0 / math.pi)


def _gelu(x):
    return 0.5 * x * (1.0 + jnp.tanh(_GELU_C * (x + 0.044715 * x * x * x)))


def _dgelu(x):
    t = jnp.tanh(_GELU_C * (x + 0.044715 * x * x * x))
    return 0.5 * (1.0 + t) + 0.5 * x * (1.0 - t * t) * _GELU_C * (1.0 + 3.0 * 0.044715 * x * x)


def _neg_expm1(x):
    series = -x * (1.0 + x * (0.5 + x * (1.0 / 6.0 + x * (1.0 / 24.0 + x * (1.0 / 120.0)))))
    return jnp.where(x > -0.03, series, 1.0 - jnp.exp(x))


def _rms(x, g):
    r = lax.rsqrt(jnp.mean(x * x, axis=-1, keepdims=True) + EPS)
    return x * r * g


def _rms_bwd(x, g, dy):
    r = lax.rsqrt(jnp.mean(x * x, axis=-1, keepdims=True) + EPS)
    xhat = x * r
    dg = jnp.sum(dy * xhat, axis=0, keepdims=True)
    dxhat = dy * g
    dx = r * (dxhat - xhat * jnp.mean(dxhat * xhat, axis=-1, keepdims=True))
    return dx, dg


def _row_iota(shape):
    return lax.broadcasted_iota(jnp.int32, shape, 0)


def _shift_down(x, j):
    if j == 0:
        return x
    return jnp.where(_row_iota(x.shape) >= j, pltpu.roll(x, j, 0), 0.0)


def _shift_up(x, j):
    if j == 0:
        return x
    n = x.shape[0]
    return jnp.where(_row_iota(x.shape) < n - j, pltpu.roll(x, n - j, 0), 0.0)


def _conv(x, w, b):
    y = b + w[3:4, :] * x
    for k in range(3):
        y = y + w[k:k + 1, :] * _shift_down(x, 3 - k)
    return y


def _conv_bwd(x, w, dy):
    dx = w[3:4, :] * dy
    dws = []
    for k in range(3):
        dx = dx + w[k:k + 1, :] * _shift_up(dy, 3 - k)
        dws.append(jnp.sum(dy * _shift_down(x, 3 - k), axis=0, keepdims=True))
    dws.append(jnp.sum(dy * x, axis=0, keepdims=True))
    return dx, jnp.concatenate(dws, axis=0), jnp.sum(dy, axis=0, keepdims=True)


def _split3(x):
    hi = x.astype(BF16)
    r1 = x - hi.astype(F32)
    mid = r1.astype(BF16)
    lo = (r1 - mid.astype(F32)).astype(BF16)
    return hi, mid, lo


def _tri_dot(tri, x):
    hi, mid, lo = _split3(x)
    return _dot(tri, hi) + _dot(tri, mid) + _dot(tri, lo)


def _cumsum_rows(x):
    n = x.shape[0] // CHUNK
    r = lax.broadcasted_iota(jnp.int32, (CHUNK, CHUNK), 0)
    c = lax.broadcasted_iota(jnp.int32, (CHUNK, CHUNK), 1)
    tri = (r >= c).astype(BF16)
    carry = jnp.zeros((1, x.shape[1]), F32)
    cums, prevs, ends = [], [], []
    for i in range(n):
        blk = _tri_dot(tri, x[i * CHUNK:(i + 1) * CHUNK]) + carry
        prevs.append(jnp.broadcast_to(carry, blk.shape))
        carry = blk[CHUNK - 1:CHUNK, :]
        ends.append(jnp.broadcast_to(carry, blk.shape))
        cums.append(blk)
    return jnp.concatenate(cums, 0), jnp.concatenate(prevs, 0), jnp.concatenate(ends, 0)


def _rev_cumsum_rows(x):
    n = x.shape[0] // CHUNK
    r = lax.broadcasted_iota(jnp.int32, (CHUNK, CHUNK), 0)
    c = lax.broadcasted_iota(jnp.int32, (CHUNK, CHUNK), 1)
    tri = (r <= c).astype(BF16)
    carry = jnp.zeros((1, x.shape[1]), F32)
    local, whole = [None] * n, [None] * n
    for i in range(n - 1, -1, -1):
        local[i] = _tri_dot(tri, x[i * CHUNK:(i + 1) * CHUNK])
        whole[i] = local[i] + carry
        carry = whole[i][0:1, :]
    return jnp.concatenate(local, 0), jnp.concatenate(whole, 0)


def _convsilu_fwd(name, seg, seq, w, b, dtype):
    return _rowcall(name, lambda raw, w, b: _silu(_conv(raw, w, b)), [seg], [w, b], [(seg[1], dtype)], tile=seq)[0]


def _convsilu_bwd(name, seg, seq, w, b, dy):
    def fn(raw, dy, w, b):
        return _conv_bwd(raw, w, dy * _dsilu(_conv(raw, w, b)))

    width = seg[1]
    return _rowcall(name, fn, [seg, dy], [w, b], [(width, BF16)], [((4, width), F32), ((1, width), F32)], tile=seq)


def _small_fwd(seg, seq, bias128, alog128):
    def fn(small, bias, alog):
        lane = lax.broadcasted_iota(jnp.int32, small.shape, 1)
        a = -jnp.exp(alog)
        s = small + bias
        dt = _softplus(s)
        logf = -_softplus(-s)
        pre = jnp.where(lane < SSD_HEADS, a * dt, jnp.where(lane < 2 * SSD_HEADS, logf, 0.0))
        cum, prev, end = _cumsum_rows(pre)
        return dt, cum, prev, end

    return _rowcall("small_fwd", fn, [seg], [bias128, alog128], [(128, F32)] * 4, tile=seq)


def _small_bwd(seg, seq, dcum, dend, ddt, dt_arr, bias128, alog128):
    def fn(small, dcum, dend, ddt, dt_arr, bias, alog):
        lane = lax.broadcasted_iota(jnp.int32, small.shape, 1)
        a = -jnp.exp(alog)
        sig = _sigmoid(small + bias)
        local, whole = _rev_cumsum_rows(dcum)
        dadt = local + dend
        d_dt = ddt + a * dadt
        ds = jnp.where(lane < SSD_HEADS, d_dt * sig, jnp.where(lane < 2 * SSD_HEADS, whole * (1.0 - sig), 0.0))
        da = jnp.sum(jnp.where(lane < SSD_HEADS, dadt * dt_arr, 0.0), axis=0, keepdims=True)
        return ds, jnp.sum(ds, axis=0, keepdims=True), da * a

    return _rowcall("small_bwd", fn, [seg, dcum, dend, ddt, dt_arr], [bias128, alog128], [(128, BF16)],
                    [((1, 128), F32), ((1, 128), F32)], tile=seq)


HEAD_PAIRS = SSD_HEADS // 2


def _ssd_specs(nc, reverse):
    def at(c):
        return nc - 1 - c if reverse else c

    x_spec = pl.BlockSpec((1, CHUNK, SSD_WIDTH), lambda b, c: (b, at(c), 0))
    bc_spec = pl.BlockSpec((1, CHUNK, 256), lambda b, c: (b, at(c), 0))
    col_spec = pl.BlockSpec((1, CHUNK, 128), lambda b, c: (b, at(c), 0))
    row_spec = pl.BlockSpec((1, 8, CHUNK), lambda b, c: (b, 0, at(c)))
    st_spec = pl.BlockSpec((1, 1, HEAD_PAIRS, SSD_STATE, 128), lambda b, c: (b, at(c), 0, 0, 0))
    return x_spec, bc_spec, col_spec, row_spec, st_spec


def _ssd_head(h, dtb, acb, apb, aeb, arb):
    return dtb[:, h:h + 1], acb[:, h:h + 1], apb[:, h:h + 1], aeb[:, h:h + 1], arb[h:h + 1, :]


def _ssd_fwd(x, bm, cm, dt_arr, cum, prev, end, a_row):
    nb, seq, _ = x.shape
    nc = seq // CHUNK
    x_spec, bc_spec, col_spec, row_spec, st_spec = _ssd_specs(nc, False)

    def body(x_ref, b_ref, c_ref, dt_ref, ac_ref, ap_ref, ae_ref, ar_ref, y_ref, st_ref, s_scr):
        @pl.when(pl.program_id(1) == 0)
        def _():
            s_scr[...] = jnp.zeros_like(s_scr)

        causal = (lax.broadcasted_iota(jnp.int32, (CHUNK, CHUNK), 0)
                  >= lax.broadcasted_iota(jnp.int32, (CHUNK, CHUNK), 1))
        low = lax.broadcasted_iota(jnp.int32, (CHUNK, 128), 1) < HEAD_DIM
        cols = (dt_ref[0], ac_ref[0], ap_ref[0], ae_ref[0], ar_ref[0])
        bcs = [b_ref[0, :, g * 128:(g + 1) * 128] for g in range(2)]
        ccs = [c_ref[0, :, g * 128:(g + 1) * 128] for g in range(2)]
        ms = [_dot_nt(ccs[g], bcs[g]) for g in range(2)]
        for pi in range(HEAD_PAIRS):
            x2 = x_ref[0, :, pi * 128:(pi + 1) * 128]
            dt2 = jnp.where(low, cols[0][:, 2 * pi:2 * pi + 1], cols[0][:, 2 * pi + 1:2 * pi + 2])
            xdt_f = x2 * dt2
            xdt = xdt_f.astype(BF16)
            sprev = s_scr[pi]
            st_ref[0, 0, pi] = sprev
            spb = sprev.astype(BF16)
            ys, us = [], []
            for h in (2 * pi, 2 * pi + 1):
                g = h // 3
                _, ac, ap, ae, ar = _ssd_head(h, *cols)
                lm = jnp.exp(jnp.where(causal, ac - ar, NEG))
                gm = (ms[g] * lm).astype(BF16)
                ys.append(_dot(gm, xdt) + jnp.exp(ac - ap) * _dot(ccs[g], spb))
                us.append(jnp.exp(ae[0:1, :] - ap[0:1, :]) * sprev
                          + _dot_tn(bcs[g], (xdt_f * jnp.exp(ae - ac)).astype(BF16)))
            y_ref[0, :, pi * 128:(pi + 1) * 128] = jnp.where(low, ys[0], ys[1])
            s_scr[pi] = jnp.where(low, us[0], us[1])

    return pl.pallas_call(
        body, name="ssd_fwd", grid=(nb, nc),
        out_shape=[jax.ShapeDtypeStruct(x.shape, F32),
                   jax.ShapeDtypeStruct((nb, nc, HEAD_PAIRS, SSD_STATE, 128), F32)],
        in_specs=[x_spec, bc_spec, bc_spec, col_spec, col_spec, col_spec, col_spec, row_spec],
        out_specs=[x_spec, st_spec],
        scratch_shapes=[pltpu.VMEM((HEAD_PAIRS, SSD_STATE, 128), F32)],
        compiler_params=_params(("parallel", "arbitrary")),
    )(x, bm, cm, dt_arr, cum, prev, end, a_row)


def _ssd_bwd(x_h, bm, cm, dt_arr, cum, prev, end, a_row, states, dy_h):
    nb, seq, _ = x_h.shape
    nc = seq // CHUNK
    x_spec, bc_spec, col_spec, row_spec, st_spec = _ssd_specs(nc, True)

    def body(x_ref, b_ref, c_ref, dt_ref, ac_ref, ap_ref, ae_ref, ar_ref, st_ref, dy_ref,
             dx_ref, db_ref, dc_ref, da_ref, dend_ref, ddt_ref, ds_scr):
        @pl.when(pl.program_id(1) == 0)
        def _():
            ds_scr[...] = jnp.zeros_like(ds_scr)

        causal = (lax.broadcasted_iota(jnp.int32, (CHUNK, CHUNK), 0)
                  >= lax.broadcasted_iota(jnp.int32, (CHUNK, CHUNK), 1))
        lane = lax.broadcasted_iota(jnp.int32, (CHUNK, 128), 1)
        low = lane < HEAD_DIM
        cols = (dt_ref[0], ac_ref[0], ap_ref[0], ae_ref[0], ar_ref[0])
        bcs = [b_ref[0, :, g * 128:(g + 1) * 128] for g in range(2)]
        ccs = [c_ref[0, :, g * 128:(g + 1) * 128] for g in range(2)]
        ms = [_dot_nt(ccs[g], bcs[g]) for g in range(2)]
        dms = [jnp.zeros((CHUNK, CHUNK), F32) for _ in range(2)]
        dc_accs = [jnp.zeros((CHUNK, SSD_STATE), F32) for _ in range(2)]
        db_accs = [jnp.zeros((CHUNK, SSD_STATE), F32) for _ in range(2)]
        da_blk = jnp.zeros((CHUNK, 128), F32)
        dend_blk = jnp.zeros((CHUNK, 128), F32)
        ddt_blk = jnp.zeros((CHUNK, 128), F32)
        for pi in range(HEAD_PAIRS):
            x2 = x_ref[0, :, pi * 128:(pi + 1) * 128]
            dy2 = dy_ref[0, :, pi * 128:(pi + 1) * 128]
            dt2 = jnp.where(low, cols[0][:, 2 * pi:2 * pi + 1], cols[0][:, 2 * pi + 1:2 * pi + 2])
            xdt_f = x2 * dt2
            xdt = xdt_f.astype(BF16)
            dyb = dy2.astype(BF16)
            dsn = ds_scr[pi]
            dsb = dsn.astype(BF16)
            sprev_f = st_ref[0, 0, pi]
            sprev = sprev_f.astype(BF16)
            dxdts, dss = [], []
            for h in (2 * pi, 2 * pi + 1):
                g = h // 3
                mine = low if h % 2 == 0 else jnp.logical_not(low)
                _, ac, ap, ae, ar = _ssd_head(h, *cols)
                bc, cc, m = bcs[g], ccs[g], ms[g]
                lm = jnp.exp(jnp.where(causal, ac - ar, NEG))
                gm = (m * lm).astype(BF16)
                dy_m = jnp.where(mine, dy2, 0.0)
                dyb_m = dy_m.astype(BF16)
                xdt_m = jnp.where(mine, xdt_f, 0.0)
                e_in = jnp.exp(ac - ap)
                f_out = jnp.exp(ae - ac)
                whole = jnp.exp(ae[0:1, :] - ap[0:1, :])
                dg = _dot_nt(dyb_m, xdt)
                dxdt_off = f_out * _dot(bc, dsb)
                dxdt = _dot_tn(gm, dyb) + dxdt_off
                dmj = dg * lm
                dms[g] = dms[g] + dmj
                dc_accs[g] = dc_accs[g] + e_in * _dot_nt(dyb_m, sprev)
                db_accs[g] = db_accs[g] + f_out * _dot_nt(xdt_m.astype(BF16), dsb)
                dss.append(whole * dsn + _dot_tn(cc, (dy2 * e_in).astype(BF16)))
                wmat = dmj * m
                r_in = jnp.sum(dy_m * (e_in * _dot(cc, sprev)), axis=1, keepdims=True)
                q_out = jnp.sum(xdt_m * dxdt_off, axis=1, keepdims=True)
                daj = (jnp.sum(wmat, axis=1, keepdims=True) - jnp.sum(wmat.T, axis=1, keepdims=True)
                       + r_in - q_out)
                cross = jnp.where(mine, dsn * sprev_f, 0.0)
                dendj = (jnp.sum(q_out, axis=0, keepdims=True)
                         + whole * jnp.sum(jnp.sum(cross, axis=1, keepdims=True), axis=0, keepdims=True))
                ddtj = jnp.sum(jnp.where(mine, dxdt * x2, 0.0), axis=1, keepdims=True)
                dxdts.append(dxdt)
                da_blk = jnp.where(lane == h, daj, da_blk)
                dend_blk = jnp.where(lane == h, dendj, dend_blk)
                ddt_blk = jnp.where(lane == h, ddtj, ddt_blk)
            dx_ref[0, :, pi * 128:(pi + 1) * 128] = jnp.where(low, dxdts[0], dxdts[1]) * dt2
            ds_scr[pi] = jnp.where(low, dss[0], dss[1])
        for g in range(2):
            dmb = dms[g].astype(BF16)
            dc_ref[0, :, g * 128:(g + 1) * 128] = dc_accs[g] + _dot(dmb, bcs[g])
            db_ref[0, :, g * 128:(g + 1) * 128] = db_accs[g] + _dot_tn(dmb, ccs[g])
        da_ref[0] = da_blk
        dend_ref[0] = dend_blk
        ddt_ref[0] = ddt_blk

    col_shape = jax.ShapeDtypeStruct((nb, seq, 128), F32)
    return pl.pallas_call(
        body, name="ssd_bwd", grid=(nb, nc),
        out_shape=[jax.ShapeDtypeStruct(x_h.shape, F32),
                   jax.ShapeDtypeStruct((nb, seq, 256), F32), jax.ShapeDtypeStruct((nb, seq, 256), F32),
                   col_shape, col_shape, col_shape],
        in_specs=[x_spec, bc_spec, bc_spec, col_spec, col_spec, col_spec, col_spec, row_spec, st_spec, x_spec],
        out_specs=[x_spec, bc_spec, bc_spec, col_spec, col_spec, col_spec],
        scratch_shapes=[pltpu.VMEM((HEAD_PAIRS, SSD_STATE, 128), F32)],
        compiler_params=_params(("parallel", "arbitrary")),
    )(x_h, bm, cm, dt_arr, cum, prev, end, a_row, states, dy_h)


def _lru_gates(xl, wa, ba, wx, bx, lam):
    xb = xl.astype(BF16)
    r = _sigmoid(_dot(xb, wa) + ba)
    i = _sigmoid(_dot(xb, wx) + bx)
    sp = _softplus(-lam)
    log_a = -LRU_C * r * sp
    a = jnp.exp(log_a)
    mult = jnp.sqrt(_neg_expm1(2.0 * log_a))
    return r, i, sp, log_a, a, mult


def _scan_chunks(a_ref, u_ref, h_ref, seq, reverse):
    nc = seq // CHUNK
    width = a_ref.shape[1]
    row = lax.broadcasted_iota(jnp.int32, (CHUNK, width), 0)

    def chunk(ci, carry):
        c = nc - 1 - ci if reverse else ci
        rows = pl.ds(pl.multiple_of(c * CHUNK, CHUNK), CHUNK)
        av, bv = a_ref[rows, :], u_ref[rows, :]
        d = 1
        while d < CHUNK:
            if reverse:
                keep = row < CHUNK - d
                a_sh = jnp.where(keep, pltpu.roll(av, CHUNK - d, 0), 1.0)
                b_sh = jnp.where(keep, pltpu.roll(bv, CHUNK - d, 0), 0.0)
            else:
                keep = row >= d
                a_sh = jnp.where(keep, pltpu.roll(av, d, 0), 1.0)
                b_sh = jnp.where(keep, pltpu.roll(bv, d, 0), 0.0)
            bv = av * b_sh + bv
            av = av * a_sh
            d *= 2
        hv = bv + av * carry
        h_ref[rows, :] = hv
        return hv[0:1, :] if reverse else hv[CHUNK - 1:CHUNK, :]

    lax.fori_loop(0, nc, chunk, jnp.zeros((1, width), F32))


def _lru_fwd(proj, seq, cw, cb, wa, ba, wx, bx, lam):
    def fn(raw, cw, cb, wa, ba, wx, bx, lam, a_scr, u_scr, h_scr):
        xl = _conv(raw, cw, cb)
        r, i, sp, log_a, a, mult = _lru_gates(xl, wa, ba, wx, bx, lam)
        a_scr[...] = a
        u_scr[...] = mult * (i * xl)
        _scan_chunks(a_scr, u_scr, h_scr, seq, reverse=False)
        return h_scr[...], xl

    return _rowcall("lru_fwd", fn, [(proj, 256, 2)], [cw, cb, wa, ba, wx, bx, lam],
                    [(256, F32), (256, F32)], tile=seq,
                    scratch=[pltpu.VMEM((seq, 256), F32)] * 3)


def _lru_bwd(proj, seq, xl_all, h_all, dh_all, cw, cb, wa, ba, wx, bx, lam):
    def fn(raw, xl, hseq, dh, cw, cb, wa, ba, wx, bx, lam, a_scr, u_scr, h_scr):
        r, i, sp, log_a, a, mult = _lru_gates(xl, wa, ba, wx, bx, lam)
        a_scr[...] = _shift_up(a, 1)
        u_scr[...] = dh
        _scan_chunks(a_scr, u_scr, h_scr, seq, reverse=True)
        dht = h_scr[...]
        da = dht * _shift_down(hseq, 1)
        gated = i * xl
        dgated = dht * mult
        dmult = dht * gated
        dlog_a = da * a - dmult * (a * a) / mult
        dr = dlog_a * (-LRU_C * sp)
        dsp = jnp.sum(dlog_a * (-LRU_C * r), axis=0, keepdims=True)
        dlam = -dsp * _sigmoid(-lam)
        dpa = dr * r * (1.0 - r)
        dpx = (dgated * xl) * i * (1.0 - i)
        dpa_b, dpx_b = dpa.astype(BF16), dpx.astype(BF16)
        dxl = dgated * i + _dot_nt(dpa_b, wa) + _dot_nt(dpx_b, wx)
        xb = xl.astype(BF16)
        dwa = _dot_tn(xb, dpa_b)
        dwx = _dot_tn(xb, dpx_b)
        draw, dcw, dcb = _conv_bwd(raw, cw, dxl)
        return (draw, dcw, dcb, dwa, jnp.sum(dpa, axis=0, keepdims=True), dwx,
                jnp.sum(dpx, axis=0, keepdims=True), dlam)

    return _rowcall("lru_bwd", fn, [(proj, 256, 2), xl_all, h_all, dh_all], [cw, cb, wa, ba, wx, bx, lam],
                    [(256, BF16)],
                    [((4, 256), F32), ((1, 256), F32), ((256, 256), F32), ((1, 256), F32), ((256, 256), F32),
                     ((1, 256), F32), ((1, 256), F32)],
                    tile=seq, scratch=[pltpu.VMEM((seq, 256), F32)] * 3)


FOX_SCALE = HEAD_DIM ** -0.5
FOX_BLOCK = 512


class _Side:
    def __init__(self, inputs, out_shape, sems, build):
        self.inputs, self.out_shape, self.sems, self.build = list(inputs), list(out_shape), list(sems), build

    def specs(self):
        any_spec = pl.BlockSpec(memory_space=pl.ANY)
        return [any_spec] * len(self.inputs), [any_spec] * len(self.out_shape)

    def run(self, refs, first, last):
        n_in, n_out = len(self.inputs), len(self.out_shape)
        in_refs, out_refs, sem_refs = refs[:n_in], refs[n_in:n_in + n_out], refs[n_in + n_out:]

        @pl.when(first)
        def _():
            for cp in self.build(in_refs, out_refs, sem_refs):
                cp.start()

        @pl.when(last)
        def _():
            for cp in self.build(in_refs, out_refs, sem_refs):
                cp.wait()


def _grid_ends(grid):
    ids = [pl.program_id(a) for a in range(len(grid))]
    first = functools.reduce(jnp.logical_and, [i == 0 for i in ids])
    last = functools.reduce(jnp.logical_and, [i == n - 1 for i, n in zip(ids, grid)])
    return first, last


Q_BLK, K_BLK, V_BLK = 1920 // 128, 2304 // 128, 2688 // 128
FOX_PAIRS = FOX_HEADS // 2


def _fox_bias(cum, nb, seq):
    cf = cum.reshape(nb, seq, 128)[:, :, SSD_HEADS:SSD_HEADS + FOX_HEADS]
    cols = jnp.pad(cf.reshape(nb * seq, FOX_PAIRS, 2), ((0, 0), (0, 0), (0, 126))).reshape(nb * seq, 384)
    rows = jnp.pad(cf.transpose(0, 2, 1).reshape(nb, FOX_PAIRS, 2, seq), ((0, 0), (0, 0), (0, 6), (0, 0)))
    return cols, rows


def _fox_fwd(proj, bias_cols, bias_rows, nb, seq, side=None):
    tb = min(FOX_BLOCK, seq)
    nq = seq // tb
    grid = (nb, FOX_PAIRS, nq)
    n_in = len(side.inputs) if side else 0

    def body(*refs):
        q_ref, k_ref, v_ref, cq_ref, ck_ref = refs[:5]
        o_ref, lse_ref = refs[5 + n_in:7 + n_in]
        if side is not None:
            side.run(refs[5:5 + n_in] + refs[7 + n_in:], *_grid_ends(grid))
        qi = pl.program_id(2)
        low = lax.broadcasted_iota(jnp.int32, (tb, 128), 1) < HEAD_DIM
        q2 = q_ref[...] * FOX_SCALE
        qm = [jnp.where(low, q2, 0.0).astype(BF16), jnp.where(low, 0.0, q2).astype(BF16)]
        cqs = [cq_ref[:, 0:1], cq_ref[:, 1:2]]

        def block(j, carry, diagonal):
            cols = pl.ds(pl.multiple_of(j * tb, tb), tb)
            k2 = k_ref[cols, :].astype(BF16)
            v2 = v_ref[cols, :].astype(BF16)
            new = []
            for hh in range(2):
                m_i, l_i, acc = carry[hh]
                s = _dot_nt(qm[hh], k2) + cqs[hh] - ck_ref[0, 0, hh:hh + 1, cols]
                if diagonal:
                    s = jnp.where(lax.broadcasted_iota(jnp.int32, (tb, tb), 0)
                                  >= lax.broadcasted_iota(jnp.int32, (tb, tb), 1), s, NEG)
                m_new = jnp.maximum(m_i, jnp.max(s, axis=1, keepdims=True))
                p = jnp.exp(s - m_new)
                alpha = jnp.exp(m_i - m_new)
                new.append((m_new, alpha * l_i + jnp.sum(p, axis=1, keepdims=True),
                            alpha * acc + _dot(p.astype(BF16), v2)))
            return tuple(new)

        one = (jnp.full((tb, 1), NEG, F32), jnp.zeros((tb, 1), F32), jnp.zeros((tb, 128), F32))
        carry = lax.fori_loop(0, qi, lambda j, cr: block(j, cr, False), (one, one))
        (m0, l0, a0), (m1, l1, a1) = block(qi, carry, True)
        o_ref[...] = jnp.where(low, a0 / l0, a1 / l1)
        lse_ref[...] = jnp.where(low, m0 + jnp.log(l0), m1 + jnp.log(l1))

    def blk(first):
        return pl.BlockSpec((tb, 128), lambda b, p, i: (b * nq + i, first + p))

    def seq_blk(first):
        return pl.BlockSpec((seq, 128), lambda b, p, i: (b, first + p))

    row_spec = pl.BlockSpec((1, 1, 8, seq), lambda b, p, i: (b, p, 0, 0))
    side_in, side_out = side.specs() if side else ([], [])
    shape = jax.ShapeDtypeStruct((nb * seq, FOX_WIDTH), F32)
    return pl.pallas_call(
        body, name="fox_fwd", grid=grid,
        out_shape=[shape, shape] + (side.out_shape if side else []),
        in_specs=[blk(Q_BLK), seq_blk(K_BLK), seq_blk(V_BLK), blk(0), row_spec] + side_in,
        out_specs=[blk(0), blk(0)] + side_out,
        scratch_shapes=side.sems if side else [],
        compiler_params=_params(("arbitrary", "arbitrary", "arbitrary")),
    )(proj, proj, proj, bias_cols, bias_rows, *(side.inputs if side else []))


def _fox_bwd(proj, o, lse, do, bias_cols, bias_rows, nb, seq, side=None):
    tb = min(FOX_BLOCK, seq)
    nq = seq // tb
    grid = (nb, FOX_PAIRS, nq)
    n_in = len(side.inputs) if side else 0

    def body(*refs):
        q_ref, k_ref, v_ref, o_ref, lse_ref, do_ref, cq_ref, ck_ref = refs[:8]
        dq_ref, dk_ref, dv_ref, dcum_ref, dcq_ref = refs[8 + n_in:13 + n_in]
        if side is not None:
            side.run(refs[8:8 + n_in] + refs[13 + n_in:], *_grid_ends(grid))
        kj = pl.program_id(2)

        @pl.when(kj == 0)
        def _():
            dq_ref[...] = jnp.zeros_like(dq_ref)
            dcq_ref[...] = jnp.zeros_like(dcq_ref)

        lane = lax.broadcasted_iota(jnp.int32, (tb, 128), 1)
        low = lane < HEAD_DIM
        mine = [low, jnp.logical_not(low)]
        k2 = k_ref[...]
        kb = k2.astype(BF16)
        km = [jnp.where(mine[hh], k2, 0.0).astype(BF16) for hh in range(2)]
        vb = v_ref[...].astype(BF16)

        def block(i, carry, diagonal):
            dk, dv, c0, c1 = carry
            csum = [c0, c1]
            rows = pl.ds(pl.multiple_of(i * tb, tb), tb)
            q2 = q_ref[rows, :] * FOX_SCALE
            do2 = do_ref[rows, :]
            prod = do2 * o_ref[rows, :]
            dq_add = jnp.zeros((tb, 128), F32)
            rsum = []
            for hh in range(2):
                qm = jnp.where(mine[hh], q2, 0.0).astype(BF16)
                dom = jnp.where(mine[hh], do2, 0.0).astype(BF16)
                delta = jnp.sum(jnp.where(mine[hh], prod, 0.0), axis=1, keepdims=True)
                s = _dot_nt(qm, kb) + cq_ref[rows, hh:hh + 1] - ck_ref[0, 0, hh:hh + 1, :]
                if diagonal:
                    s = jnp.where(lax.broadcasted_iota(jnp.int32, (tb, tb), 0)
                                  >= lax.broadcasted_iota(jnp.int32, (tb, tb), 1), s, NEG)
                p = jnp.exp(s - lse_ref[rows, HEAD_DIM * hh:HEAD_DIM * hh + 1])
                ds = p * (_dot_nt(dom, vb) - delta)
                dsb = ds.astype(BF16)
                dv = dv + _dot_tn(p.astype(BF16), dom)
                dk = dk + _dot_tn(dsb, qm)
                dq_add = dq_add + _dot(dsb, km[hh])
                rsum.append(jnp.sum(ds, axis=1, keepdims=True))
                csum[hh] = csum[hh] + jnp.sum(ds, axis=0, keepdims=True)
            dq_ref[rows, :] += dq_add * FOX_SCALE
            dcq_ref[rows, :] += jnp.where(lane == 0, rsum[0], jnp.where(lane == 1, rsum[1], 0.0))
            return dk, dv, csum[0], csum[1]

        init = (jnp.zeros((tb, 128), F32), jnp.zeros((tb, 128), F32), jnp.zeros((1, tb), F32),
                jnp.zeros((1, tb), F32))
        carry = block(kj, init, True)
        dk, dv, c0, c1 = lax.fori_loop(kj + 1, nq, lambda i, cr: block(i, cr, False), carry)
        dk_ref[...] = dk.astype(dk_ref.dtype)
        dv_ref[...] = dv.astype(dv_ref.dtype)
        row = lax.broadcasted_iota(jnp.int32, (8, tb), 0)
        dcum_ref[0, 0] = jnp.where(row == 0, -c0, jnp.where(row == 1, -c1, 0.0))

    def blk(first):
        return pl.BlockSpec((tb, 128), lambda b, p, j: (b * nq + j, first + p))

    def seq_blk(first):
        return pl.BlockSpec((seq, 128), lambda b, p, j: (b, first + p))

    row_blk = pl.BlockSpec((1, 1, 8, tb), lambda b, p, j: (b, p, 0, j))
    side_in, side_out = side.specs() if side else ([], [])
    tokens = nb * seq
    return pl.pallas_call(
        body, name="fox_bwd", grid=grid,
        out_shape=[jax.ShapeDtypeStruct((tokens, FOX_WIDTH), F32), jax.ShapeDtypeStruct((tokens, FOX_WIDTH), BF16),
                   jax.ShapeDtypeStruct((tokens, FOX_WIDTH), BF16),
                   jax.ShapeDtypeStruct((nb, FOX_PAIRS, 8, seq), F32),
                   jax.ShapeDtypeStruct((tokens, FOX_WIDTH), F32)] + (side.out_shape if side else []),
        in_specs=[seq_blk(Q_BLK), blk(K_BLK), blk(V_BLK), seq_blk(0), seq_blk(0), seq_blk(0), seq_blk(0), row_blk]
        + side_in,
        out_specs=[seq_blk(0), blk(0), blk(0), row_blk, seq_blk(0)] + side_out,
        scratch_shapes=side.sems if side else [],
        compiler_params=_params(("arbitrary", "arbitrary", "arbitrary")),
    )(proj, proj, proj, o, lse, do, bias_cols, bias_rows, *(side.inputs if side else []))


_ANY = pl.BlockSpec(memory_space=pl.ANY)


def _place():
    return lax.axis_index("x"), lax.axis_index("y"), lax.axis_index("c")


def _all_gather(shards, name):
    n = len(shards)

    def body(*refs):
        x_refs, out_refs = refs[:n], refs[n:2 * n]
        send_sems, recv_sems, local_sems = refs[2 * n:]
        x, y, c = _place()
        me, sibling = (x, y, c), (x, y, 1 - c)
        chips = [(1 - x, y), (x, 1 - y), (1 - x, 1 - y)]

        def rows(a, px, py, pc):
            return out_refs[a].at[4 * px + 2 * py + pc]

        def copy(a, k, block, to, src=None):
            return pltpu.make_async_remote_copy(
                src_ref=rows(a, *block) if src is None else src, dst_ref=rows(a, *block),
                send_sem=send_sems.at[a, k], recv_sem=recv_sems.at[a, k],
                device_id=to, device_id_type=pl.DeviceIdType.MESH)

        mine = [pltpu.make_async_copy(x_refs[a], rows(a, *me), local_sems.at[a]) for a in range(n)]
        for cp in mine:
            cp.start()
        first = []
        for a in range(n):
            first.append(copy(a, 0, me, sibling, src=x_refs[a]))
            first += [copy(a, 1 + j, me, (*chip, c), src=x_refs[a]) for j, chip in enumerate(chips)]
        for cp in first:
            cp.start()
        passed = []
        for j, chip in enumerate(chips):
            for a in range(n):
                copy(a, 1 + j, (*chip, c), me).wait_recv()
                passed.append(copy(a, 4 + j, (*chip, c), sibling))
                passed[-1].start()
        for a in range(n):
            copy(a, 0, sibling, me).wait_recv()
            for j, chip in enumerate(chips):
                copy(a, 4 + j, (*chip, 1 - c), me).wait_recv()
        for cp in first + passed:
            cp.wait_send()
        for cp in mine:
            cp.wait()

    return pl.pallas_call(
        body, name=name,
        out_shape=[jax.ShapeDtypeStruct((N_DEV,) + s.shape, s.dtype) for s in shards],
        in_specs=[_ANY] * n, out_specs=[_ANY] * n,
        scratch_shapes=[pltpu.SemaphoreType.DMA((n, 7)), pltpu.SemaphoreType.DMA((n, 7)),
                        pltpu.SemaphoreType.DMA((n,))],
    )(*shards)


def _remote(src, dst, send_sem, recv_sem, to):
    return pltpu.make_async_remote_copy(src_ref=src, dst_ref=dst, send_sem=send_sem, recv_sem=recv_sem,
                                        device_id=to, device_id_type=pl.DeviceIdType.MESH)


def _sem_pairs(n, k):
    return [pltpu.SemaphoreType.DMA((n, k)), pltpu.SemaphoreType.DMA((n, k))]


def _sibling_side(full):
    def build(g_refs, out_refs, sems):
        x, y, c = _place()
        return [_remote(g_refs[a].at[4 * (k // 2) + 2 * (k % 2) + (1 - c)], out_refs[a].at[k],
                        sems[0].at[a, k], sems[1].at[a, k], (x, y, 1 - c))
                for a in range(len(g_refs)) for k in range(4)]

    return _Side(full, [jax.ShapeDtypeStruct((4,) + f.shape[1:], f.dtype) for f in full],
                 _sem_pairs(len(full), 4), build)


def _chip_side(part):
    def build(p_refs, out_refs, sems):
        x, y, c = _place()
        peers = [(1 - x, y), (x, 1 - y), (1 - x, 1 - y)]
        return [_remote(p_refs[a].at[2 * px + py], out_refs[a].at[k], sems[0].at[a, k], sems[1].at[a, k],
                        (px, py, c))
                for a in range(len(p_refs)) for k, (px, py) in enumerate(peers)]

    return _Side(part, [jax.ShapeDtypeStruct((3,) + p.shape[1:], p.dtype) for p in part],
                 _sem_pairs(len(part), 3), build)


def _spread_side(shards):
    def build(x_refs, out_refs, sems):
        x, y, c = _place()
        targets = [(x, y, 1 - c), (1 - x, y, c), (x, 1 - y, c), (1 - x, 1 - y, c)]
        cps = []
        for a in range(len(x_refs)):
            slot = out_refs[a].at[4 * x + 2 * y + c]
            cps.append(pltpu.make_async_copy(x_refs[a], slot, sems[2].at[a]))
            cps += [_remote(x_refs[a], slot, sems[0].at[a, k], sems[1].at[a, k], to)
                    for k, to in enumerate(targets)]
        return cps

    n = len(shards)
    return _Side(shards, [jax.ShapeDtypeStruct((N_DEV,) + s.shape, s.dtype) for s in shards],
                 _sem_pairs(n, 4) + [pltpu.SemaphoreType.DMA((n,))], build)


def _pass_side(bufs):
    def build(in_refs, out_refs, sems):
        x, y, c = _place()
        chips = [(1 - x, y), (x, 1 - y), (1 - x, 1 - y)]
        return [_remote(in_refs[a].at[4 * px + 2 * py + c], out_refs[a].at[4 * px + 2 * py + c],
                        sems[0].at[a, j], sems[1].at[a, j], (x, y, 1 - c))
                for a in range(len(in_refs)) for j, (px, py) in enumerate(chips)]

    return _Side(bufs, [jax.ShapeDtypeStruct(b.shape, b.dtype) for b in bufs], _sem_pairs(len(bufs), 3), build)


def _run_side(side, name, in_place=False):
    n_in = len(side.inputs)

    def body(*refs):
        copies = side.build(refs[:n_in], refs[n_in:n_in + len(side.out_shape)],
                            refs[n_in + len(side.out_shape):])
        for cp in copies:
            cp.start()
        for cp in copies:
            cp.wait()

    in_specs, out_specs = side.specs()
    return pl.pallas_call(
        body, name=name, out_shape=side.out_shape, in_specs=in_specs, out_specs=out_specs,
        scratch_shapes=side.sems,
        input_output_aliases={a: a for a in range(n_in)} if in_place else {},
    )(*side.inputs)


def _pick_rows(rows, cap=512):
    t = cap
    while t >= 8:
        if rows % t == 0:
            return t
        t //= 2
    raise ValueError(rows)


def _pair_sum(full, got, name):
    _, rows, cols = full.shape
    tile = _pick_rows(rows, 256)
    c = lax.axis_index("c").astype(jnp.int32).reshape(1)

    def body(c_ref, a_ref, b_ref, o_ref):
        o_ref[...] = a_ref[...] + b_ref[...]

    blk = (1, tile, cols)
    return pl.pallas_call(
        body, name=name,
        grid_spec=pltpu.PrefetchScalarGridSpec(
            num_scalar_prefetch=1, grid=(4, rows // tile),
            in_specs=[pl.BlockSpec(blk, lambda k, i, c_ref: (4 * (k // 2) + 2 * (k % 2) + c_ref[0], i, 0)),
                      pl.BlockSpec(blk, lambda k, i, c_ref: (k, i, 0))],
            out_specs=pl.BlockSpec(blk, lambda k, i, c_ref: (k, i, 0))),
        out_shape=jax.ShapeDtypeStruct((4, rows, cols), full.dtype),
        compiler_params=_params(("arbitrary", "arbitrary")),
    )(c, full, got)


def _adam_math(w, g, m, v):
    c1 = 1.0 / (1.0 - ADAM_B1 ** ADAM_STEP)
    c2 = 1.0 / (1.0 - ADAM_B2 ** ADAM_STEP)
    m_new = ADAM_B1 * m + (1.0 - ADAM_B1) * g
    v_new = ADAM_B2 * v + (1.0 - ADAM_B2) * (g * g)
    delta = -ADAM_LR * ((m_new * c1) / (jnp.sqrt(v_new * c2) + ADAM_EPS) + ADAM_WD * w)
    return delta, m_new, v_new


def _sum_adamw(part, others, w, m, v, name):
    _, rows, cols = part.shape
    tile = _pick_rows(rows, 128)
    own = (2 * lax.axis_index("x") + lax.axis_index("y")).astype(jnp.int32).reshape(1)

    def body(own_ref, p_ref, o_ref, w_ref, m_ref, v_ref, g_out, d_out, m_out, v_out):
        g = ((p_ref[0] + o_ref[0]) + o_ref[1]) + o_ref[2]
        delta, m_new, v_new = _adam_math(w_ref[...], g, m_ref[...], v_ref[...])
        g_out[...] = g
        d_out[...] = delta
        m_out[...] = m_new
        v_out[...] = v_new

    flat = pl.BlockSpec((tile, cols), lambda i, own_ref: (i, 0))
    shape = jax.ShapeDtypeStruct((rows, cols), F32)
    return pl.pallas_call(
        body, name=name,
        grid_spec=pltpu.PrefetchScalarGridSpec(
            num_scalar_prefetch=1, grid=(rows // tile,),
            in_specs=[pl.BlockSpec((1, tile, cols), lambda i, own_ref: (own_ref[0], i, 0)),
                      pl.BlockSpec((3, tile, cols), lambda i, own_ref: (0, i, 0)), flat, flat, flat],
            out_specs=[flat] * 4),
        out_shape=[shape] * 4,
        compiler_params=_params(("arbitrary",)),
    )(own, part, others, w, m, v)


def _all_reduce_small(vec):
    gathered = _all_gather([vec], "ar_gather")[0]
    rows = vec.shape[0]

    def fn(*blocks):
        s = blocks[0]
        for b in blocks[1:]:
            s = s + b
        return s

    return _rowcall("ar_sum", fn, [gathered[j] for j in range(N_DEV)], [], [(1024, F32)],
                    tile=_pick_rows(rows))[0]


def _pad_rows(flat, mult):
    n = flat.shape[-1]
    per = mult * 1024
    padded = -(-n // per) * per
    pad = [(0, 0)] * (flat.ndim - 1) + [(0, padded - n)]
    return jnp.pad(flat, pad).reshape(flat.shape[:-1] + (padded // 1024, 1024))


def _regroup_w_in(w):
    pad = jnp.zeros((w.shape[0], 116), w.dtype)
    return jnp.concatenate([w[:, 768:1280], w[:, 1286:1798], w[:, 1280:1286], w[:, 2950:2956], pad,
                            w[:, 0:768], w[:, 1798:2950]], axis=1)


def _ungroup_w_in(wp):
    return jnp.concatenate([wp[:, 1152:1920], wp[:, 0:512], wp[:, 1024:1030], wp[:, 512:1024],
                            wp[:, 1920:3072], wp[:, 1030:1036]], axis=1)


def _to_shard(name, a):
    if name == 'w_in':
        return _regroup_w_in(a)
    if name in ('w_gate', 'w_up'):
        return jnp.pad(a, ((0, 0), (0, FF_SHARD_P - FF_SHARD)))
    if name == 'w_down':
        return jnp.pad(a, ((0, FF_SHARD_P - FF_SHARD), (0, 0)))
    return a


def _from_shard(name, a):
    if name == 'w_in':
        return _ungroup_w_in(a)
    if name in ('w_gate', 'w_up'):
        return a[:, 0:FF_SHARD]
    if name == 'w_down':
        return a[0:FF_SHARD, :]
    return a


def _whole(name, gathered):
    if BIG[name][0] == 1:
        return gathered.reshape(-1, gathered.shape[-1])
    return gathered.transpose(1, 0, 2).reshape(gathered.shape[1], -1)


def _split(name, whole):
    if BIG[name][0] == 1:
        return whole.reshape(N_DEV, whole.shape[0] // N_DEV, whole.shape[1])
    return whole.reshape(whole.shape[0], N_DEV, whole.shape[1] // N_DEV).transpose(1, 0, 2)


def _pack_list(arrays, mult):
    return _pad_rows(jnp.concatenate([a.reshape(-1) for a in arrays]), mult)


def _unpack_list(buf, shapes):
    flat = buf.reshape(-1)
    out, off = [], 0
    for s in shapes:
        n = math.prod(s)
        out.append(flat[off:off + n].reshape(s))
        off += n
    return out


def _adamw(w, g, m, v):
    return _rowcall("adamw", _adam_math, [w, g, m, v], [], [(1024, F32)] * 3, tile=_pick_rows(w.shape[0]))


def _block_diag(w):
    out = jnp.zeros((LRU_WIDTH, LRU_WIDTH), w.dtype)
    for g in range(4):
        out = lax.dynamic_update_slice(out, w[g], (64 * g, 64 * g))
    return out


def _block_diag_grad(full):
    return jnp.stack([full[64 * g:64 * (g + 1), 64 * g:64 * (g + 1)] for g in range(4)])


def _row(v):
    return v.reshape(1, -1).astype(F32)


def _lane128(*pieces):
    flat = jnp.concatenate([p.reshape(-1).astype(F32) for p in pieces])
    return jnp.pad(flat, (0, 128 - flat.shape[0])).reshape(1, 128)


def _layer_consts(w):
    c = {}
    cw, cb = w['ssd_conv_w'], w['ssd_conv_b']
    c['cw_x'], c['cw_b'], c['cw_c'] = cw[:, 0:384], cw[:, 384:640], cw[:, 640:896]
    c['cb_x'], c['cb_b'], c['cb_c'] = _row(cb[0:384]), _row(cb[384:640]), _row(cb[640:896])
    c['bias128'] = _lane128(w['ssd_dt_bias'], w['fox_b_f'])
    c['alog128'] = _lane128(w['ssd_a_log'])
    c['d384'] = _row(jnp.repeat(w['ssd_d'], HEAD_DIM))
    c['lcw'], c['lcb'] = w['lru_conv_w'], _row(w['lru_conv_b'])
    c['wa'], c['wx'] = _block_diag(w['lru_w_a']).astype(BF16), _block_diag(w['lru_w_x']).astype(BF16)
    c['ba'], c['bx'], c['lam'] = _row(w['lru_b_a']), _row(w['lru_b_x']), _row(w['lru_lambda'])
    return c


def _layer_fwd(h0, p_i, w, c, hooks=None, layer=0):
    nb, seq = c['nb'], c['seq']

    def carried(stage):
        return hooks.fwd_side(layer, stage) if hooks is not None else None

    def arrived(outs):
        if hooks is not None:
            hooks.fwd_done(outs)
            w.update(hooks.weights(layer))

    u1 = _rowcall("norm1", lambda h, g: _rms(h, g), [h0], [_row(w['norm1_g'])], [(D_MODEL, BF16)])[0]
    proj = _matmul(u1, w['w_in'], 'nn', "proj")

    xs_c = _convsilu_fwd("conv_x", (proj, 384, 4), seq, c['cw_x'], c['cb_x'], F32)
    b_c = _convsilu_fwd("conv_b", (proj, 256, 0), seq, c['cw_b'], c['cb_b'], BF16)
    c_c = _convsilu_fwd("conv_c", (proj, 256, 1), seq, c['cw_c'], c['cb_c'], BF16)
    dt_arr, cum, prev, end = _small_fwd((proj, 128, 8), seq, c['bias128'], c['alog128'])
    x_h = xs_c.reshape(nb, seq, SSD_WIDTH)
    cum3 = cum.reshape(nb, seq, 128)
    a_row = cum3[:, :, 0:8].transpose(0, 2, 1)
    ssd_in = (x_h, b_c.reshape(nb, seq, 256), c_c.reshape(nb, seq, 256), dt_arr.reshape(nb, seq, 128), cum3,
              prev.reshape(nb, seq, 128), end.reshape(nb, seq, 128), a_row)
    y_h, states = _ssd_fwd(*ssd_in)
    y_core = y_h.reshape(nb * seq, SSD_WIDTH)

    hseq, xl = _lru_fwd(proj, seq, c['lcw'], c['lcb'], c['wa'], c['ba'], c['wx'], c['bx'], c['lam'])

    bias_cols, bias_rows = _fox_bias(cum, nb, seq)
    y_fox, lse, *side_out = _fox_fwd(proj, bias_cols, bias_rows, nb, seq, carried('attention'))
    arrived(side_out)

    def post(yc, xs, z, hs, lg, yf, d, g1, g2, g3):
        y1 = _rms((yc + xs * d) * _silu(z), g1)
        y2 = _rms(hs * _gelu(lg), g2)
        y3 = _rms(yf, g3)
        return jnp.concatenate([y1, y2, y3], axis=-1)

    post_consts = [c['d384'], _row(w['ssd_norm_g']), _row(w['lru_norm_g']), _row(w['fox_norm_g'])]
    ycat = _rowcall("mix_post", post, [y_core, xs_c, (proj, 384, 3), hseq, (proj, 256, 3), y_fox], post_consts,
                    [(D_MODEL, BF16)])[0]
    mix = _matmul(ycat, w['w_out'], 'nn', "mix_out")

    def res_norm(h, d, g):
        hn = h + d
        return hn, _rms(hn, g)

    h1, u2 = _rowcall("res_norm2", res_norm, [h0, mix], [_row(w['norm2_g'])], [(D_MODEL, F32), (D_MODEL, BF16)])
    side = carried('ffn_in')
    if side is None:
        gu = _matmul(u2, w['w_gu'], 'nn', "ffn_in", BF16)
    else:
        gu, *side_out = _matmul(u2, w['w_gu'], 'nn', "ffn_in", BF16, side=side)
        arrived(side_out)
    act = _rowcall("swiglu", lambda gt, up: _silu(gt.astype(F32)) * up.astype(F32),
                   [(gu, D_FF_P, 0), (gu, D_FF_P, 1)], [], [(D_FF_P, BF16)])[0]
    ff = _matmul(act, w['w_down'], 'nn', "ffn_out")
    h2, u3 = _rowcall("res_norm3", res_norm, [h1, ff], [_row(w['norm3_g'])], [(D_MODEL, F32), (D_MODEL, BF16)])
    pg = _matmul(u3, w['w_ple_gate'], 'nn', "ple_gate")
    pp = _matmul(p_i, w['w_ple_proj'], 'nn', "ple_proj")
    h3 = _rowcall("ple", lambda h, a, b, bias: h + _sigmoid(a + bias) * b, [h2, pg, pp], [_row(w['b_ple_gate'])],
                  [(D_MODEL, F32)])[0]
    saved = dict(h0=h0, u1=u1, proj=proj, xs_c=xs_c, dt_arr=dt_arr, ssd_in=ssd_in, states=states,
                 y_core=y_core, hseq=hseq, xl=xl, bias_cols=bias_cols, bias_rows=bias_rows, lse=lse,
                 y_fox=y_fox, post_consts=post_consts, ycat=ycat, h1=h1, u2=u2, gu=gu, act=act, h2=h2, u3=u3,
                 pg=pg, pp=pp, p_i=p_i)
    return h3, saved


def _layer_bwd(dh3, s, w, c, hooks=None, layer=0):
    nb, seq = c['nb'], c['seq']
    g = {}

    def ple_bwd(dh, a, b, bias):
        gate = _sigmoid(a + bias)
        dpg = dh * b * gate * (1.0 - gate)
        return dh * gate, dpg, jnp.sum(dpg, axis=0, keepdims=True)

    dpp, dpg, g['b_ple_gate'] = _rowcall("ple_bwd", ple_bwd, [dh3, s['pg'], s['pp']], [_row(w['b_ple_gate'])],
                                         [(D_MODEL, BF16), (D_MODEL, BF16)], [((1, D_MODEL), F32)])
    g['w_ple_proj'] = _matmul(s['p_i'], dpp, 'tn', "d_w_ple_proj")
    g['w_ple_gate'] = _matmul(s['u3'], dpg, 'tn', "d_w_ple_gate")
    du3 = _matmul(dpg, w['w_ple_gate'], 'nt', "d_u3")

    def norm_bwd(h, du, dh, gain):
        dx, dg = _rms_bwd(h, gain, du)
        dhn = dh + dx
        return dhn, dhn, dg

    dh2, dh2_b, g['norm3_g'] = _rowcall("norm3_bwd", norm_bwd, [s['h2'], du3, dh3], [_row(w['norm3_g'])],
                                        [(D_MODEL, F32), (D_MODEL, BF16)], [((1, D_MODEL), F32)])
    g['w_down'] = _matmul(s['act'], dh2_b, 'tn', "d_w_down")
    dact = _matmul(dh2_b, w['w_down'], 'nt', "d_act", BF16)

    def swiglu_bwd(gt, up, da):
        gt, up, da = gt.astype(F32), up.astype(F32), da.astype(F32)
        return jnp.concatenate([da * up * _dsilu(gt), da * _silu(gt)], axis=-1)

    dgu = _rowcall("swiglu_bwd", swiglu_bwd, [(s['gu'], D_FF_P, 0), (s['gu'], D_FF_P, 1), dact], [],
                   [(2 * D_FF_P, BF16)])[0]
    g['w_gu'] = _matmul(s['u2'], dgu, 'tn', "d_w_gu")
    du2 = _matmul(dgu, w['w_gu'], 'nt', "d_u2")
    dh1, dh1_b, g['norm2_g'] = _rowcall("norm2_bwd", norm_bwd, [s['h1'], du2, dh2], [_row(w['norm2_g'])],
                                        [(D_MODEL, F32), (D_MODEL, BF16)], [((1, D_MODEL), F32)])
    g['w_out'] = _matmul(s['ycat'], dh1_b, 'tn', "d_w_out")
    if hooks is None:
        dycat = _matmul(dh1_b, w['w_out'], 'nt', "d_ycat")
    else:
        ready = {n: g[n] for n in ('w_out', 'w_down', 'w_ple_gate', 'w_ple_proj')}
        ready['w_gate'], ready['w_up'] = g['w_gu'][:, 0:D_FF_P], g['w_gu'][:, D_FF_P:2 * D_FF_P]
        dycat, *side_out = _matmul(dh1_b, w['w_out'], 'nt', "d_ycat", side=hooks.sibling_side(layer, ready))
        hooks.sibling_done(side_out)

    def post_bwd(dy, yc, xs, z, hs, lg, yf, d, g1, g2, g3):
        sz = _silu(z)
        ytot = yc + xs * d
        dpre1, dg1 = _rms_bwd(ytot * sz, g1, dy[:, 0:384])
        dytot = dpre1 * sz
        dz = dpre1 * ytot * _dsilu(z)
        dd = jnp.sum(dytot * xs, axis=0, keepdims=True)
        gl = _gelu(lg)
        dpre2, dg2 = _rms_bwd(hs * gl, g2, dy[:, 384:640])
        dyf, dg3 = _rms_bwd(yf, g3, dy[:, 640:1024])
        return dytot, dytot * d, dz, dpre2 * gl, dpre2 * hs * _dgelu(lg), dyf, dd, dg1, dg2, dg3

    (dy_core, dxs_skip, dz, dhseq, dlg, dy_fox, dd384, g['ssd_norm_g'], g['lru_norm_g'], g['fox_norm_g']) = _rowcall(
        "mix_post_bwd", post_bwd,
        [dycat, s['y_core'], s['xs_c'], (s['proj'], 384, 3), s['hseq'], (s['proj'], 256, 3), s['y_fox']],
        s['post_consts'],
        [(384, F32), (384, F32), (384, BF16), (256, F32), (256, BF16), (384, F32)],
        [((1, 384), F32), ((1, 384), F32), ((1, 256), F32), ((1, 384), F32)])
    g['ssd_d'] = dd384.reshape(SSD_HEADS, HEAD_DIM).sum(axis=1)

    side = hooks.bwd_side() if hooks is not None else None
    dq, dk, dv, dcf_rows, dcf_cols, *side_out = _fox_bwd(s['proj'], s['y_fox'], s['lse'], dy_fox, s['bias_cols'],
                                                         s['bias_rows'], nb, seq, side)
    if hooks is not None:
        hooks.bwd_done(side_out)
    dq = dq.astype(BF16)

    dx_h, db_c, dc_c, da_arr, dend_arr, ddt_arr = _ssd_bwd(*s['ssd_in'], s['states'],
                                                           dy_core.reshape(nb, seq, SSD_WIDTH))
    dxs_c = dx_h.reshape(nb * seq, SSD_WIDTH) + dxs_skip
    dcf = (dcf_rows[:, :, 0:2, :].reshape(nb, FOX_HEADS, seq).transpose(0, 2, 1)
           + dcf_cols.reshape(nb, seq, FOX_PAIRS, 128)[:, :, :, 0:2].reshape(nb, seq, FOX_HEADS))
    dcum = jnp.concatenate([da_arr[:, :, 0:SSD_HEADS], dcf,
                            jnp.zeros((nb, seq, 128 - 2 * SSD_HEADS), F32)], axis=-1).reshape(nb * seq, 128)
    proj = s['proj']
    dxs_raw, dcw_x, dcb_x = _convsilu_bwd("conv_x_bwd", (proj, 384, 4), seq, c['cw_x'], c['cb_x'], dxs_c)
    db_raw, dcw_b, dcb_b = _convsilu_bwd("conv_b_bwd", (proj, 256, 0), seq, c['cw_b'], c['cb_b'],
                                         db_c.reshape(nb * seq, 256))
    dc_raw, dcw_c, dcb_c = _convsilu_bwd("conv_c_bwd", (proj, 256, 1), seq, c['cw_c'], c['cb_c'],
                                         dc_c.reshape(nb * seq, 256))
    dsmall, dbias128, dalog128 = _small_bwd((proj, 128, 8), seq, dcum, dend_arr.reshape(nb * seq, 128),
                                            ddt_arr.reshape(nb * seq, 128), s['dt_arr'], c['bias128'], c['alog128'])
    g['ssd_conv_w'] = jnp.concatenate([dcw_x, dcw_b, dcw_c], axis=1)
    g['ssd_conv_b'] = jnp.concatenate([dcb_x, dcb_b, dcb_c], axis=1).reshape(-1)
    g['ssd_dt_bias'] = dbias128[0, 0:SSD_HEADS]
    g['fox_b_f'] = dbias128[0, SSD_HEADS:2 * SSD_HEADS]
    g['ssd_a_log'] = dalog128[0, 0:SSD_HEADS]

    (dlru_raw, g['lru_conv_w'], dlcb, dwa, dba, dwx, dbx, dlam) = _lru_bwd(
        s['proj'], seq, s['xl'], s['hseq'], dhseq, c['lcw'], c['lcb'], c['wa'], c['ba'], c['wx'], c['bx'], c['lam'])
    g['lru_conv_b'], g['lru_b_a'], g['lru_b_x'], g['lru_lambda'] = (t.reshape(-1) for t in (dlcb, dba, dbx, dlam))
    g['lru_w_a'], g['lru_w_x'] = _block_diag_grad(dwa), _block_diag_grad(dwx)

    dproj = jnp.concatenate([db_raw, dc_raw, dlru_raw, dlg, dsmall, dz, dxs_raw, dq, dk, dv], axis=1)
    g['w_in'] = _matmul(s['u1'], dproj, 'tn', "d_w_in")
    if hooks is None:
        du1 = _matmul(dproj, w['w_in'], 'nt', "d_u1")
    else:
        hooks.bwd_ready(layer, {'w_in': g['w_in']})
        du1, *side_out = _matmul(dproj, w['w_in'], 'nt', "d_u1", side=hooks.bwd_side())
        hooks.bwd_done(side_out)

    def norm1_bwd(h, du, dh, gain):
        dx, dg = _rms_bwd(h, gain, du)
        return dh + dx, dg

    dh0, g['norm1_g'] = _rowcall("norm1_bwd", norm1_bwd, [s['h0'], du1, dh1], [_row(w['norm1_g'])],
                                 [(D_MODEL, F32)], [((1, D_MODEL), F32)])
    for name in ('b_ple_gate', 'norm3_g', 'norm2_g', 'norm1_g', 'ssd_norm_g', 'lru_norm_g', 'fox_norm_g'):
        g[name] = g[name].reshape(-1)
    wgu = g.pop('w_gu')
    g['w_gate'], g['w_up'] = wgu[:, 0:D_FF_P], wgu[:, D_FF_P:2 * D_FF_P]
    return dh0, g


class _Hooks:
    def __init__(self, shard):
        self.shard = shard
        self.whole = {}
        self.part, self.others = {}, {}
        self.pending, self.flying = [], []

    def first(self, extra):
        got = _all_gather([self.shard['w_in', 0]] + extra, "gather_first")
        self.whole['w_in', 0] = _whole('w_in', got[0])
        return got[1:]

    def fwd_side(self, layer, stage):
        if layer != 0:
            return None
        if stage == 'attention':
            self.flying = [(n, 0) for n in BIG if n != 'w_in']
        else:
            self.flying = [(n, 1) for n in BIG]
        return _spread_side([self.shard[k] for k in self.flying])

    def fwd_done(self, outs):
        if self.flying:
            passed = _run_side(_pass_side(outs), "gather_pass_%s%d" % self.flying[0], in_place=True)
            for k, arr in zip(self.flying, passed):
                self.whole[k] = _whole(k[0], arr)
            self.flying = []

    def weights(self, layer):
        w = {n: self.whole[n, layer] for n in BIG if (n, layer) in self.whole}
        if 'w_gate' in w:
            w['w_gu'] = jnp.concatenate([w['w_gate'], w['w_up']], axis=1)
        return w

    def sibling_side(self, layer, grads):
        self.sib_keys = [(n, layer) for n in grads]
        self.sib_full = [_split(n, grads[n]) for n in grads]
        return _sibling_side(self.sib_full)

    def sibling_done(self, got):
        for k, f, r in zip(self.sib_keys, self.sib_full, got):
            self.part[k] = _pair_sum(f, r, "rs_pair_sum_%s%d" % k)
        self.pending += self.sib_keys

    def bwd_ready(self, layer, grads):
        side = self.sibling_side(layer, grads)
        self.sibling_done(_run_side(side, "rs_sibling_%s%d" % self.sib_keys[0]))

    def bwd_side(self):
        self.flying, self.pending = self.pending, []
        return _chip_side([self.part[k] for k in self.flying]) if self.flying else None

    def bwd_done(self, outs):
        for k, o in zip(self.flying, outs):
            self.others[k] = o
        self.flying = []

    def flush(self):
        side = self.bwd_side()
        if side is not None:
            self.bwd_done(_run_side(side, "rs_chips_last"))


def _local_step(x, p, target, big, small, hooks=None):
    nb, seq, _ = x.shape
    tokens = nb * seq
    h = x.reshape(tokens, D_MODEL)
    layers, saves = [], []
    for i in range(DEPTH):
        w = {name: small[name][i] for name in small if name != 'final_norm_g'}
        if hooks is not None:
            w.update(hooks.weights(i))
        else:
            for name in ('w_in', 'w_out', 'w_down', 'w_ple_gate', 'w_ple_proj'):
                w[name] = big[name][i]
            w['w_gu'] = jnp.concatenate([big['w_gate'][i], big['w_up'][i]], axis=1)
        c = _layer_consts(w)
        c['nb'], c['seq'] = nb, seq
        h, s = _layer_fwd(h, p[i].reshape(tokens, PLE_DIM).astype(BF16), w, c, hooks, i)
        layers.append((w, c))
        saves.append(s)

    def head(hf, tgt, gain):
        r = lax.rsqrt(jnp.mean(hf * hf, axis=-1, keepdims=True) + EPS)
        xhat = hf * r
        err = xhat * gain - tgt
        loss = 0.5 * jnp.sum(jnp.mean(err * err, axis=-1, keepdims=True), axis=0, keepdims=True)
        dy = err * (1.0 / D_MODEL)
        dg = jnp.sum(dy * xhat, axis=0, keepdims=True)
        dxhat = dy * gain
        dh = r * (dxhat - xhat * jnp.mean(dxhat * xhat, axis=-1, keepdims=True))
        return dh, jnp.broadcast_to(loss, (1, 128)), dg

    dh, loss128, dgf = _rowcall("loss_head", head, [h, target.reshape(tokens, D_MODEL)],
                                [_row(small['final_norm_g'])], [(D_MODEL, F32)],
                                [((1, 128), F32), ((1, D_MODEL), F32)])
    grads = {'final_norm_g': dgf.reshape(-1)}
    per_layer = [None] * DEPTH
    for i in range(DEPTH - 1, -1, -1):
        w, c = layers[i]
        dh, per_layer[i] = _layer_bwd(dh, saves[i], w, c, hooks, i)
    for name in per_layer[0]:
        if name in BIG:
            grads[name] = [per_layer[i][name] for i in range(DEPTH)]
        else:
            grads[name] = jnp.stack([per_layer[i][name] for i in range(DEPTH)])
    return loss128[0, 0], dh.reshape(nb, seq, D_MODEL), grads


def kernel(x, p, norm1_g, w_in, ssd_conv_w, ssd_conv_b, ssd_dt_bias, ssd_a_log, ssd_d, ssd_norm_g, lru_conv_w, lru_conv_b, lru_w_a, lru_b_a, lru_w_x, lru_b_x, lru_lambda, lru_norm_g, fox_b_f, fox_norm_g, w_out, norm2_g, w_gate, w_up, w_down, norm3_g, w_ple_gate, b_ple_gate, w_ple_proj, final_norm_g, loss_target, m_norm1_g, m_w_in, m_ssd_conv_w, m_ssd_conv_b, m_ssd_dt_bias, m_ssd_a_log, m_ssd_d, m_ssd_norm_g, m_lru_conv_w, m_lru_conv_b, m_lru_w_a, m_lru_b_a, m_lru_w_x, m_lru_b_x, m_lru_lambda, m_lru_norm_g, m_fox_b_f, m_fox_norm_g, m_w_out, m_norm2_g, m_w_gate, m_w_up, m_w_down, m_norm3_g, m_w_ple_gate, m_b_ple_gate, m_w_ple_proj, m_final_norm_g, v_norm1_g, v_w_in, v_ssd_conv_w, v_ssd_conv_b, v_ssd_dt_bias, v_ssd_a_log, v_ssd_d, v_ssd_norm_g, v_lru_conv_w, v_lru_conv_b, v_lru_w_a, v_lru_b_a, v_lru_w_x, v_lru_b_x, v_lru_lambda, v_lru_norm_g, v_fox_b_f, v_fox_norm_g, v_w_out, v_norm2_g, v_w_gate, v_w_up, v_w_down, v_norm3_g, v_w_ple_gate, v_b_ple_gate, v_w_ple_proj, v_final_norm_g):
    args = dict(locals())
    w_loc = {n: args[n] for n in WEIGHTS}
    m_loc = {n: args['m_' + n] for n in WEIGHTS}
    v_loc = {n: args['v_' + n] for n in WEIGHTS}
    dev = 4 * lax.axis_index("x") + 2 * lax.axis_index("y") + lax.axis_index("c")

    keys = [(n, i) for n in BIG for i in range(DEPTH)]
    conv_names = list(CONV_SHARDED)
    conv_loc_shapes = [w_loc[n].shape for n in conv_names]
    hooks = _Hooks({(n, i): _to_shard(n, w_loc[n][i]).astype(BF16) for n, i in keys})
    conv_all, = hooks.first([_pack_list([w_loc[n] for n in conv_names], 8)])
    small = {n: w_loc[n] for n in WEIGHTS if n not in BIG and n not in CONV_SHARDED}
    per_dev = [_unpack_list(conv_all[j], conv_loc_shapes) for j in range(N_DEV)]
    for idx, n in enumerate(conv_names):
        small[n] = jnp.concatenate([per_dev[j][idx] for j in range(N_DEV)], axis=2)

    loss_part, dx, grads = _local_step(x, p, loss_target, None, small, hooks)
    loss = lax.psum(loss_part, ("x", "y", "c"))
    hooks.flush()
    out = {kind: {n: [None] * DEPTH for n in BIG} for kind in ('g', 'delta', 'm', 'v')}
    for n, i in keys:
        res = _sum_adamw(hooks.part[n, i], hooks.others[n, i],
                         *[_to_shard(n, d[n][i]) for d in (w_loc, m_loc, v_loc)], "sum_adamw_%s%d" % (n, i))
        for kind, r in zip(('g', 'delta', 'm', 'v'), res):
            out[kind][n][i] = _from_shard(n, r)

    small_names = [n for n in WEIGHTS if n not in BIG]
    small_shapes = [grads[n].shape for n in small_names]
    g_small = dict(zip(small_names, _unpack_list(
        _all_reduce_small(_pack_list([grads[n] for n in small_names], 8)), small_shapes)))
    for n in CONV_SHARDED:
        width = CONV_SHARDED[n][2] // N_DEV
        g_small[n] = lax.dynamic_slice_in_dim(g_small[n], dev * width, width, axis=2)
    shapes = [w_loc[n].shape for n in small_names]
    packed = [_pack_list([d[n] for n in small_names], 8) for d in (w_loc, g_small, m_loc, v_loc)]
    upd = [dict(zip(small_names, _unpack_list(t, shapes))) for t in _adamw(*packed)]
    for kind, d in zip(('g', 'delta', 'm', 'v'), [g_small] + upd):
        for n in small_names:
            out[kind][n] = d[n]
        for n in BIG:
            out[kind][n] = jnp.stack(out[kind][n])
    return (loss, dx, *[out['g'][n] for n in WEIGHTS], *[out['delta'][n] for n in WEIGHTS],
            *[out['m'][n] for n in WEIGHTS], *[out['v'][n] for n in WEIGHTS])
```

```python
import functools
import math

import jax
import jax.numpy as jnp
from jax import lax
from jax.experimental import pallas as pl
from jax.experimental.pallas import tpu as pltpu

F32 = jnp.float32
BF16 = jnp.bfloat16

N_DEV = 8
D_MODEL = 1024
DEPTH = 2
HEAD_DIM = 64
SSD_WIDTH = 384
LRU_WIDTH = 256
FOX_WIDTH = 384
SSD_HEADS = 6
SSD_STATE = 128
CHUNK = 256
FOX_HEADS = 6
D_FF = 2816
FF_SHARD = D_FF // N_DEV
FF_SHARD_P = 384
D_FF_P = N_DEV * FF_SHARD_P
PLE_DIM = 256
IN_COLS = 2956
PROJ_COLS = 3072
LRU_C = 8.0
EPS = 1e-6
NEG = -1e30

ADAM_LR = 0.001
ADAM_B1 = 0.9
ADAM_B2 = 0.999
ADAM_EPS = 1e-08
ADAM_WD = 0.01
ADAM_STEP = 10

VMEM_LIMIT = 56 * 1024 * 1024

WEIGHTS = ['norm1_g', 'w_in', 'ssd_conv_w', 'ssd_conv_b', 'ssd_dt_bias', 'ssd_a_log', 'ssd_d', 'ssd_norm_g',
           'lru_conv_w', 'lru_conv_b', 'lru_w_a', 'lru_b_a', 'lru_w_x', 'lru_b_x', 'lru_lambda', 'lru_norm_g',
           'fox_b_f', 'fox_norm_g', 'w_out', 'norm2_g', 'w_gate', 'w_up', 'w_down', 'norm3_g', 'w_ple_gate',
           'b_ple_gate', 'w_ple_proj', 'final_norm_g']
BIG = {'w_in': (1, (DEPTH, D_MODEL, IN_COLS)), 'w_out': (1, (DEPTH, D_MODEL, D_MODEL)),
       'w_gate': (2, (DEPTH, D_MODEL, D_FF)), 'w_up': (2, (DEPTH, D_MODEL, D_FF)),
       'w_down': (1, (DEPTH, D_FF, D_MODEL)), 'w_ple_gate': (1, (DEPTH, D_MODEL, D_MODEL)),
       'w_ple_proj': (2, (DEPTH, PLE_DIM, D_MODEL))}
CONV_SHARDED = {'ssd_conv_w': (DEPTH, 4, 896), 'lru_conv_w': (DEPTH, 4, 256)}


def _dot(a, b):
    return jnp.dot(a, b, preferred_element_type=F32)


def _dot_nt(a, b):
    return lax.dot_general(a, b, (((1,), (1,)), ((), ())), preferred_element_type=F32)


def _dot_tn(a, b):
    return lax.dot_general(a, b, (((0,), (0,)), ((), ())), preferred_element_type=F32)


def _params(sem):
    return pltpu.CompilerParams(dimension_semantics=sem, vmem_limit_bytes=VMEM_LIMIT)


def _pick_tile(n, cap):
    if n <= cap:
        return n
    best = 128
    for t in range(128, cap + 1, 128):
        if n % t == 0:
            best = t
    assert n % best == 0, (n, cap)
    return best


def _matmul(a, b, mode, name, out_dtype=F32, side=None, shard_n=None):
    if mode == 'tn':
        k_dim, m_dim = a.shape
        n_dim = b.shape[1]
    else:
        m_dim, k_dim = a.shape
        n_dim = b.shape[1] if mode == 'nn' else b.shape[0]
    tm = _pick_tile(m_dim, 512 if mode != 'tn' else 1024)
    tn = 2 * shard_n if shard_n else _pick_tile(n_dim, 1536 if mode != 'tn' else 1024)
    tk = _pick_tile(k_dim, 3072 if mode != 'tn' else 2048)
    nk = k_dim // tk
    grid = (n_dim // tn, m_dim // tm, nk)

    n_in = len(side.inputs) if side else 0
    n_out = len(side.out_shape) if side else 0

    n_acc = 1 if nk > 1 else 0

    def body(*refs):
        a_ref, b_ref, o_ref = refs[0], refs[1], refs[2 + n_in]
        acc_ref = refs[3 + n_in + n_out] if n_acc else None
        if side is not None:
            side.run(refs[2:2 + n_in] + refs[3 + n_in:3 + n_in + n_out] + refs[3 + n_acc + n_in + n_out:],
                     *_grid_ends(grid))
        kk = pl.program_id(2)
        prod = {'nn': _dot, 'nt': _dot_nt, 'tn': _dot_tn}[mode](a_ref[...], b_ref[...])

        def write(res):
            if shard_n:
                for q in range(tn // shard_n):
                    o_ref[q] = res[:, q * shard_n:(q + 1) * shard_n].astype(o_ref.dtype)
            else:
                o_ref[...] = res.astype(o_ref.dtype)

        if nk == 1:
            write(prod)
            return

        @pl.when(kk == 0)
        def _():
            acc_ref[...] = prod

        @pl.when(jnp.logical_and(kk > 0, kk < nk - 1))
        def _():
            acc_ref[...] += prod

        @pl.when(kk == nk - 1)
        def _():
            write(acc_ref[...] + prod)

    if mode == 'nn':
        a_spec = pl.BlockSpec((tm, tk), lambda j, i, k: (i, k))
        b_spec = pl.BlockSpec((tk, tn), lambda j, i, k: (k, j))
    elif mode == 'nt':
        a_spec = pl.BlockSpec((tm, tk), lambda j, i, k: (i, k))
        b_spec = pl.BlockSpec((tn, tk), lambda j, i, k: (j, k))
    else:
        a_spec = pl.BlockSpec((tk, tm), lambda j, i, k: (k, i))
        b_spec = pl.BlockSpec((tk, tn), lambda j, i, k: (k, j))
    side_in, side_out = side.specs() if side else ([], [])
    if shard_n:
        out_shape = jax.ShapeDtypeStruct((n_dim // shard_n, m_dim, shard_n), out_dtype)
        out_spec = pl.BlockSpec((tn // shard_n, tm, shard_n), lambda j, i, k: (j, i, 0))
    else:
        out_shape = jax.ShapeDtypeStruct((m_dim, n_dim), out_dtype)
        out_spec = pl.BlockSpec((tm, tn), lambda j, i, k: (i, j))
    res = pl.pallas_call(
        body, name=name, grid=grid,
        out_shape=[out_shape] + (side.out_shape if side else []),
        in_specs=[a_spec, b_spec] + side_in,
        out_specs=[out_spec] + side_out,
        scratch_shapes=[pltpu.VMEM((tm, tn), F32)] * n_acc + (side.sems if side else []),
        compiler_params=_params(("arbitrary", "arbitrary", "arbitrary") if side
                                else ("parallel", "parallel", "arbitrary")),
    )(a, b, *(side.inputs if side else []))
    return res if side else res[0]


def _rowcall(name, fn, tiled, consts, outs, accs=(), tile=512, scratch=()):
    specs, arrays = [], []
    for t in tiled:
        if isinstance(t, tuple):
            arr, width, blk = t
            specs.append(pl.BlockSpec((tile, width), functools.partial(lambda i, blk: (i, blk), blk=blk)))
        else:
            arr = t
            specs.append(pl.BlockSpec((tile, arr.shape[1]), lambda i: (i, 0)))
        arrays.append(arr)
    rows = arrays[0].shape[0]
    assert rows % tile == 0, (name, rows, tile)
    for c in consts:
        specs.append(pl.BlockSpec(c.shape, lambda i: (0, 0)))
        arrays.append(c)
    n_in, n_out, n_acc = len(arrays), len(outs), len(accs)
    out_shape = [jax.ShapeDtypeStruct((rows, c), dt) for c, dt in outs]
    out_specs = [pl.BlockSpec((tile, c), lambda i: (i, 0)) for c, _ in outs]
    out_shape += [jax.ShapeDtypeStruct(s, dt) for s, dt in accs]
    out_specs += [pl.BlockSpec(s, lambda i: (0, 0)) for s, _ in accs]

    def body(*refs):
        ins = [r[...] for r in refs[:n_in]]
        out_refs = refs[n_in:n_in + n_out]
        acc_refs = refs[n_in + n_out:n_in + n_out + n_acc]
        scr = refs[n_in + n_out + n_acc:]
        res = fn(*ins, *scr)
        if not isinstance(res, (tuple, list)):
            res = (res,)
        assert len(res) == n_out + n_acc, (name, len(res))
        for r, v in zip(out_refs, res[:n_out]):
            r[...] = v.astype(r.dtype)
        if n_acc:
            first = pl.program_id(0) == 0

            @pl.when(first)
            def _():
                for r, v in zip(acc_refs, res[n_out:]):
                    r[...] = v.astype(r.dtype)

            @pl.when(jnp.logical_not(first))
            def _():
                for r, v in zip(acc_refs, res[n_out:]):
                    r[...] += v.astype(r.dtype)

    res = pl.pallas_call(
        body, name=name, grid=(rows // tile,),
        out_shape=out_shape, in_specs=specs, out_specs=out_specs,
        scratch_shapes=list(scratch),
        compiler_params=_params(("arbitrary",)),
    )(*arrays)
    return res


def _sigmoid(x):
    return 1.0 / (1.0 + jnp.exp(-x))


def _softplus(x):
    return jnp.maximum(x, 0.0) + jnp.log(1.0 + jnp.exp(-jnp.abs(x)))


def _silu(x):
    return x * _sigmoid(x)


def _dsilu(x):
    s = _sigmoid(x)
    return s * (1.0 + x * (1.0 - s))


_GELU_C = math.sqrt(2.0 / math.pi)


def _gelu(x):
    return 0.5 * x * (1.0 + jnp.tanh(_GELU_C * (x + 0.044715 * x * x * x)))


def _dgelu(x):
    t = jnp.tanh(_GELU_C * (x + 0.044715 * x * x * x))
    return 0.5 * (1.0 + t) + 0.5 * x * (1.0 - t * t) * _GELU_C * (1.0 + 3.0 * 0.044715 * x * x)


def _neg_expm1(x):
    series = -x * (1.0 + x * (0.5 + x * (1.0 / 6.0 + x * (1.0 / 24.0 + x * (1.0 / 120.0)))))
    return jnp.where(x > -0.03, series, 1.0 - jnp.exp(x))


def _rms(x, g):
    r = lax.rsqrt(jnp.mean(x * x, axis=-1, keepdims=True) + EPS)
    return x * r * g


def _rms_bwd(x, g, dy):
    r = lax.rsqrt(jnp.mean(x * x, axis=-1, keepdims=True) + EPS)
    xhat = x * r
    dg = jnp.sum(dy * xhat, axis=0, keepdims=True)
    dxhat = dy * g
    dx = r * (dxhat - xhat * jnp.mean(dxhat * xhat, axis=-1, keepdims=True))
    return dx, dg


def _row_iota(shape):
    return lax.broadcasted_iota(jnp.int32, shape, 0)


def _shift_down(x, j):
    if j == 0:
        return x
    return jnp.where(_row_iota(x.shape) >= j, pltpu.roll(x, j, 0), 0.0)


def _shift_up(x, j):
    if j == 0:
        return x
    n = x.shape[0]
    return jnp.where(_row_iota(x.shape) < n - j, pltpu.roll(x, n - j, 0), 0.0)


def _conv(x, w, b):
    y = b + w[3:4, :] * x
    for k in range(3):
        y = y + w[k:k + 1, :] * _shift_down(x, 3 - k)
    return y


def _conv_bwd(x, w, dy):
    dx = w[3:4, :] * dy
    dws = []
    for k in range(3):
        dx = dx + w[k:k + 1, :] * _shift_up(dy, 3 - k)
        dws.append(jnp.sum(dy * _shift_down(x, 3 - k), axis=0, keepdims=True))
    dws.append(jnp.sum(dy * x, axis=0, keepdims=True))
    return dx, jnp.concatenate(dws, axis=0), jnp.sum(dy, axis=0, keepdims=True)


def _split3(x):
    hi = x.astype(BF16)
    r1 = x - hi.astype(F32)
    mid = r1.astype(BF16)
    lo = (r1 - mid.astype(F32)).astype(BF16)
    return hi, mid, lo


def _tri_dot(tri, x):
    hi, mid, lo = _split3(x)
    return _dot(tri, hi) + _dot(tri, mid) + _dot(tri, lo)


def _cumsum_rows(x):
    n = x.shape[0] // CHUNK
    r = lax.broadcasted_iota(jnp.int32, (CHUNK, CHUNK), 0)
    c = lax.broadcasted_iota(jnp.int32, (CHUNK, CHUNK), 1)
    tri = (r >= c).astype(BF16)
    carry = jnp.zeros((1, x.shape[1]), F32)
    cums, prevs, ends = [], [], []
    for i in range(n):
        blk = _tri_dot(tri, x[i * CHUNK:(i + 1) * CHUNK]) + carry
        prevs.append(jnp.broadcast_to(carry, blk.shape))
        carry = blk[CHUNK - 1:CHUNK, :]
        ends.append(jnp.broadcast_to(carry, blk.shape))
        cums.append(blk)
    return jnp.concatenate(cums, 0), jnp.concatenate(prevs, 0), jnp.concatenate(ends, 0)


def _rev_cumsum_rows(x):
    n = x.shape[0] // CHUNK
    r = lax.broadcasted_iota(jnp.int32, (CHUNK, CHUNK), 0)
    c = lax.broadcasted_iota(jnp.int32, (CHUNK, CHUNK), 1)
    tri = (r <= c).astype(BF16)
    carry = jnp.zeros((1, x.shape[1]), F32)
    local, whole = [None] * n, [None] * n
    for i in range(n - 1, -1, -1):
        local[i] = _tri_dot(tri, x[i * CHUNK:(i + 1) * CHUNK])
        whole[i] = local[i] + carry
        carry = whole[i][0:1, :]
    return jnp.concatenate(local, 0), jnp.concatenate(whole, 0)


def _convsilu_fwd(name, seg, seq, w, b, dtype):
    return _rowcall(name, lambda raw, w, b: _silu(_conv(raw, w, b)), [seg], [w, b], [(seg[1], dtype)], tile=seq)[0]


def _convsilu_bwd(name, seg, seq, w, b, dy):
    def fn(raw, dy, w, b):
        return _conv_bwd(raw, w, dy * _dsilu(_conv(raw, w, b)))

    width = seg[1]
    return _rowcall(name, fn, [seg, dy], [w, b], [(width, BF16)], [((4, width), F32), ((1, width), F32)], tile=seq)


def _small_fwd(seg, seq, bias128, alog128):
    def fn(small, bias, alog):
        lane = lax.broadcasted_iota(jnp.int32, small.shape, 1)
        a = -jnp.exp(alog)
        s = small + bias
        dt = _softplus(s)
        logf = -_softplus(-s)
        pre = jnp.where(lane < SSD_HEADS, a * dt, jnp.where(lane < 2 * SSD_HEADS, logf, 0.0))
        cum, prev, end = _cumsum_rows(pre)
        return dt, cum, prev, end

    return _rowcall("small_fwd", fn, [seg], [bias128, alog128], [(128, F32)] * 4, tile=seq)


def _small_bwd(seg, seq, dcum, dcq, dend, ddt, dt_arr, bias128, alog128):
    def fn(small, dcum, dcq, dend, ddt, dt_arr, bias, alog):
        lane = lax.broadcasted_iota(jnp.int32, small.shape, 1)
        for pair in range(FOX_PAIRS):
            first = SSD_HEADS + 2 * pair
            moved = pltpu.roll(dcq[:, pair * 128:(pair + 1) * 128], first, 1)
            dcum = dcum + jnp.where(jnp.logical_or(lane == first, lane == first + 1), moved, 0.0)
        a = -jnp.exp(alog)
        sig = _sigmoid(small + bias)
        local, whole = _rev_cumsum_rows(dcum)
        dadt = local + dend
        d_dt = ddt + a * dadt
        ds = jnp.where(lane < SSD_HEADS, d_dt * sig, jnp.where(lane < 2 * SSD_HEADS, whole * (1.0 - sig), 0.0))
        da = jnp.sum(jnp.where(lane < SSD_HEADS, dadt * dt_arr, 0.0), axis=0, keepdims=True)
        return ds, jnp.sum(ds, axis=0, keepdims=True), da * a

    return _rowcall("small_bwd", fn, [seg, dcum, dcq, dend, ddt, dt_arr], [bias128, alog128], [(128, BF16)],
                    [((1, 128), F32), ((1, 128), F32)], tile=seq)


HEAD_PAIRS = SSD_HEADS // 2


def _ssd_specs(nc, reverse):
    def at(c):
        return nc - 1 - c if reverse else c

    x_spec = pl.BlockSpec((1, CHUNK, SSD_WIDTH), lambda b, c: (b, at(c), 0))
    bc_spec = pl.BlockSpec((1, CHUNK, 256), lambda b, c: (b, at(c), 0))
    col_spec = pl.BlockSpec((1, CHUNK, 128), lambda b, c: (b, at(c), 0))
    row_spec = pl.BlockSpec((1, 8, CHUNK), lambda b, c: (b, 0, at(c)))
    st_spec = pl.BlockSpec((1, 1, HEAD_PAIRS, SSD_STATE, 128), lambda b, c: (b, at(c), 0, 0, 0))
    return x_spec, bc_spec, col_spec, row_spec, st_spec


def _ssd_head(h, dtb, acb, apb, aeb, arb):
    return dtb[:, h:h + 1], acb[:, h:h + 1], apb[:, h:h + 1], aeb[:, h:h + 1], arb[h:h + 1, :]


def _ssd_fwd(x, bm, cm, dt_arr, cum, prev, end, a_row):
    nb, seq, _ = x.shape
    nc = seq // CHUNK
    x_spec, bc_spec, col_spec, row_spec, st_spec = _ssd_specs(nc, False)

    def body(x_ref, b_ref, c_ref, dt_ref, ac_ref, ap_ref, ae_ref, ar_ref, y_ref, st_ref, s_scr):
        @pl.when(pl.program_id(1) == 0)
        def _():
            s_scr[...] = jnp.zeros_like(s_scr)

        causal = (lax.broadcasted_iota(jnp.int32, (CHUNK, CHUNK), 0)
                  >= lax.broadcasted_iota(jnp.int32, (CHUNK, CHUNK), 1))
        low = lax.broadcasted_iota(jnp.int32, (CHUNK, 128), 1) < HEAD_DIM
        cols = (dt_ref[0], ac_ref[0], ap_ref[0], ae_ref[0], ar_ref[0])
        bcs = [b_ref[0, :, g * 128:(g + 1) * 128] for g in range(2)]
        ccs = [c_ref[0, :, g * 128:(g + 1) * 128] for g in range(2)]
        ms = [_dot_nt(ccs[g], bcs[g]) for g in range(2)]
        for pi in range(HEAD_PAIRS):
            x2 = x_ref[0, :, pi * 128:(pi + 1) * 128]
            dt2 = jnp.where(low, cols[0][:, 2 * pi:2 * pi + 1], cols[0][:, 2 * pi + 1:2 * pi + 2])
            xdt_f = x2 * dt2
            xdt = xdt_f.astype(BF16)
            sprev = s_scr[pi]
            st_ref[0, 0, pi] = sprev
            spb = sprev.astype(BF16)
            ys, us = [], []
            for h in (2 * pi, 2 * pi + 1):
                g = h // 3
                _, ac, ap, ae, ar = _ssd_head(h, *cols)
                lm = jnp.exp(jnp.where(causal, ac - ar, NEG))
                gm = (ms[g] * lm).astype(BF16)
                ys.append(_dot(gm, xdt) + jnp.exp(ac - ap) * _dot(ccs[g], spb))
                us.append(jnp.exp(ae[0:1, :] - ap[0:1, :]) * sprev
                          + _dot_tn(bcs[g], (xdt_f * jnp.exp(ae - ac)).astype(BF16)))
            y_ref[0, :, pi * 128:(pi + 1) * 128] = jnp.where(low, ys[0], ys[1])
            s_scr[pi] = jnp.where(lax.broadcasted_iota(jnp.int32, (SSD_STATE, 128), 1) < HEAD_DIM, us[0], us[1])

    return pl.pallas_call(
        body, name="ssd_fwd", grid=(nb, nc),
        out_shape=[jax.ShapeDtypeStruct(x.shape, F32),
                   jax.ShapeDtypeStruct((nb, nc, HEAD_PAIRS, SSD_STATE, 128), F32)],
        in_specs=[x_spec, bc_spec, bc_spec, col_spec, col_spec, col_spec, col_spec, row_spec],
        out_specs=[x_spec, st_spec],
        scratch_shapes=[pltpu.VMEM((HEAD_PAIRS, SSD_STATE, 128), F32)],
        compiler_params=_params(("parallel", "arbitrary")),
    )(x, bm, cm, dt_arr, cum, prev, end, a_row)


def _ssd_bwd(x_h, bm, cm, dt_arr, cum, prev, end, a_row, states, dy_h):
    nb, seq, _ = x_h.shape
    nc = seq // CHUNK
    x_spec, bc_spec, col_spec, row_spec, st_spec = _ssd_specs(nc, True)

    def body(x_ref, b_ref, c_ref, dt_ref, ac_ref, ap_ref, ae_ref, ar_ref, st_ref, dy_ref,
             dx_ref, db_ref, dc_ref, da_ref, dend_ref, ddt_ref, ds_scr):
        @pl.when(pl.program_id(1) == 0)
        def _():
            ds_scr[...] = jnp.zeros_like(ds_scr)

        causal = (lax.broadcasted_iota(jnp.int32, (CHUNK, CHUNK), 0)
                  >= lax.broadcasted_iota(jnp.int32, (CHUNK, CHUNK), 1))
        lane = lax.broadcasted_iota(jnp.int32, (CHUNK, 128), 1)
        low = lane < HEAD_DIM
        low_state = lax.broadcasted_iota(jnp.int32, (SSD_STATE, 128), 1) < HEAD_DIM
        cols = (dt_ref[0], ac_ref[0], ap_ref[0], ae_ref[0], ar_ref[0])
        bcs = [b_ref[0, :, g * 128:(g + 1) * 128] for g in range(2)]
        ccs = [c_ref[0, :, g * 128:(g + 1) * 128] for g in range(2)]
        ms = [_dot_nt(ccs[g], bcs[g]) for g in range(2)]
        dms = [jnp.zeros((CHUNK, CHUNK), F32) for _ in range(2)]
        dc_accs = [jnp.zeros((CHUNK, SSD_STATE), F32) for _ in range(2)]
        db_accs = [jnp.zeros((CHUNK, SSD_STATE), F32) for _ in range(2)]
        da_blk = jnp.zeros((CHUNK, 128), F32)
        dend_blk = jnp.zeros((CHUNK, 128), F32)
        ddt_blk = jnp.zeros((CHUNK, 128), F32)
        for pi in range(HEAD_PAIRS):
            x2 = x_ref[0, :, pi * 128:(pi + 1) * 128]
            dy2 = dy_ref[0, :, pi * 128:(pi + 1) * 128]
            dt2 = jnp.where(low, cols[0][:, 2 * pi:2 * pi + 1], cols[0][:, 2 * pi + 1:2 * pi + 2])
            xdt_f = x2 * dt2
            xdt = xdt_f.astype(BF16)
            dyb = dy2.astype(BF16)
            dsn = ds_scr[pi]
            dsb = dsn.astype(BF16)
            sprev_f = st_ref[0, 0, pi]
            sprev = sprev_f.astype(BF16)
            dxdts, dss = [], []
            for h in (2 * pi, 2 * pi + 1):
                g = h // 3
                mine = low if h % 2 == 0 else jnp.logical_not(low)
                _, ac, ap, ae, ar = _ssd_head(h, *cols)
                bc, cc, m = bcs[g], ccs[g], ms[g]
                lm = jnp.exp(jnp.where(causal, ac - ar, NEG))
                gm = (m * lm).astype(BF16)
                dy_m = jnp.where(mine, dy2, 0.0)
                dyb_m = dy_m.astype(BF16)
                xdt_m = jnp.where(mine, xdt_f, 0.0)
                e_in = jnp.exp(ac - ap)
                f_out = jnp.exp(ae - ac)
                whole = jnp.exp(ae[0:1, :] - ap[0:1, :])
                dg = _dot_nt(dyb_m, xdt)
                dxdt_off = f_out * _dot(bc, dsb)
                dxdt = _dot_tn(gm, dyb) + dxdt_off
                dmj = dg * lm
                dms[g] = dms[g] + dmj
                dc_accs[g] = dc_accs[g] + e_in * _dot_nt(dyb_m, sprev)
                db_accs[g] = db_accs[g] + f_out * _dot_nt(xdt_m.astype(BF16), dsb)
                dss.append(whole * dsn + _dot_tn(cc, (dy2 * e_in).astype(BF16)))
                wmat = dmj * m
                r_in = jnp.sum(dy_m * (e_in * _dot(cc, sprev)), axis=1, keepdims=True)
                q_out = jnp.sum(xdt_m * dxdt_off, axis=1, keepdims=True)
                daj = (jnp.sum(wmat, axis=1, keepdims=True) - jnp.sum(wmat.T, axis=1, keepdims=True)
                       + r_in - q_out)
                cross = jnp.where(low_state if h % 2 == 0 else jnp.logical_not(low_state), dsn * sprev_f, 0.0)
                dendj = (jnp.sum(q_out, axis=0, keepdims=True)
                         + whole * jnp.sum(jnp.sum(cross, axis=1, keepdims=True), axis=0, keepdims=True))
                ddtj = jnp.sum(jnp.where(mine, dxdt * x2, 0.0), axis=1, keepdims=True)
                dxdts.append(dxdt)
                da_blk = jnp.where(lane == h, daj, da_blk)
                dend_blk = jnp.where(lane == h, dendj, dend_blk)
                ddt_blk = jnp.where(lane == h, ddtj, ddt_blk)
            dx_ref[0, :, pi * 128:(pi + 1) * 128] = jnp.where(low, dxdts[0], dxdts[1]) * dt2
            ds_scr[pi] = jnp.where(low_state, dss[0], dss[1])
        for g in range(2):
            dmb = dms[g].astype(BF16)
            dc_ref[0, :, g * 128:(g + 1) * 128] = dc_accs[g] + _dot(dmb, bcs[g])
            db_ref[0, :, g * 128:(g + 1) * 128] = db_accs[g] + _dot_tn(dmb, ccs[g])
        da_ref[0] = da_blk
        dend_ref[0] = dend_blk
        ddt_ref[0] = ddt_blk

    col_shape = jax.ShapeDtypeStruct((nb, seq, 128), F32)
    return pl.pallas_call(
        body, name="ssd_bwd", grid=(nb, nc),
        out_shape=[jax.ShapeDtypeStruct(x_h.shape, F32),
                   jax.ShapeDtypeStruct((nb, seq, 256), F32), jax.ShapeDtypeStruct((nb, seq, 256), F32),
                   col_shape, col_shape, col_shape],
        in_specs=[x_spec, bc_spec, bc_spec, col_spec, col_spec, col_spec, col_spec, row_spec, st_spec, x_spec],
        out_specs=[x_spec, bc_spec, bc_spec, col_spec, col_spec, col_spec],
        scratch_shapes=[pltpu.VMEM((HEAD_PAIRS, SSD_STATE, 128), F32)],
        compiler_params=_params(("parallel", "arbitrary")),
    )(x_h, bm, cm, dt_arr, cum, prev, end, a_row, states, dy_h)


def _lru_gates(xl, wa, ba, wx, bx, lam):
    xb = xl.astype(BF16)
    r = _sigmoid(_dot(xb, wa) + ba)
    i = _sigmoid(_dot(xb, wx) + bx)
    sp = _softplus(-lam)
    log_a = -LRU_C * r * sp
    a = jnp.exp(log_a)
    mult = jnp.sqrt(_neg_expm1(2.0 * log_a))
    return r, i, sp, log_a, a, mult


def _scan_chunks(a_ref, u_ref, h_ref, seq, reverse):
    nc = seq // CHUNK
    width = a_ref.shape[1]
    row = lax.broadcasted_iota(jnp.int32, (CHUNK, width), 0)

    def chunk(ci, carry):
        c = nc - 1 - ci if reverse else ci
        rows = pl.ds(pl.multiple_of(c * CHUNK, CHUNK), CHUNK)
        av, bv = a_ref[rows, :], u_ref[rows, :]
        d = 1
        while d < CHUNK:
            if reverse:
                keep = row < CHUNK - d
                a_sh = jnp.where(keep, pltpu.roll(av, CHUNK - d, 0), 1.0)
                b_sh = jnp.where(keep, pltpu.roll(bv, CHUNK - d, 0), 0.0)
            else:
                keep = row >= d
                a_sh = jnp.where(keep, pltpu.roll(av, d, 0), 1.0)
                b_sh = jnp.where(keep, pltpu.roll(bv, d, 0), 0.0)
            bv = av * b_sh + bv
            av = av * a_sh
            d *= 2
        hv = bv + av * carry
        h_ref[rows, :] = hv
        return hv[0:1, :] if reverse else hv[CHUNK - 1:CHUNK, :]

    lax.fori_loop(0, nc, chunk, jnp.zeros((1, width), F32))


def _lru_fwd(proj, seq, cw, cb, wa, ba, wx, bx, lam):
    def fn(raw, cw, cb, wa, ba, wx, bx, lam, a_scr, u_scr, h_scr):
        xl = _conv(raw, cw, cb)
        r, i, sp, log_a, a, mult = _lru_gates(xl, wa, ba, wx, bx, lam)
        a_scr[...] = a
        u_scr[...] = mult * (i * xl)
        _scan_chunks(a_scr, u_scr, h_scr, seq, reverse=False)
        return h_scr[...], xl

    return _rowcall("lru_fwd", fn, [(proj, 256, 2)], [cw, cb, wa, ba, wx, bx, lam],
                    [(256, F32), (256, F32)], tile=seq,
                    scratch=[pltpu.VMEM((seq, 256), F32)] * 3)


def _lru_bwd(proj, seq, xl_all, h_all, dh_all, cw, cb, wa, ba, wx, bx, lam):
    def fn(raw, xl, hseq, dh, cw, cb, wa, ba, wx, bx, lam, a_scr, u_scr, h_scr):
        r, i, sp, log_a, a, mult = _lru_gates(xl, wa, ba, wx, bx, lam)
        a_scr[...] = _shift_up(a, 1)
        u_scr[...] = dh
        _scan_chunks(a_scr, u_scr, h_scr, seq, reverse=True)
        dht = h_scr[...]
        da = dht * _shift_down(hseq, 1)
        gated = i * xl
        dgated = dht * mult
        dmult = dht * gated
        dlog_a = da * a - dmult * (a * a) / mult
        dr = dlog_a * (-LRU_C * sp)
        dsp = jnp.sum(dlog_a * (-LRU_C * r), axis=0, keepdims=True)
        dlam = -dsp * _sigmoid(-lam)
        dpa = dr * r * (1.0 - r)
        dpx = (dgated * xl) * i * (1.0 - i)
        dpa_b, dpx_b = dpa.astype(BF16), dpx.astype(BF16)
        dxl = dgated * i + _dot_nt(dpa_b, wa) + _dot_nt(dpx_b, wx)
        xb = xl.astype(BF16)
        dwa = _dot_tn(xb, dpa_b)
        dwx = _dot_tn(xb, dpx_b)
        draw, dcw, dcb = _conv_bwd(raw, cw, dxl)
        return (draw, dcw, dcb, dwa, jnp.sum(dpa, axis=0, keepdims=True), dwx,
                jnp.sum(dpx, axis=0, keepdims=True), dlam)

    return _rowcall("lru_bwd", fn, [(proj, 256, 2), xl_all, h_all, dh_all], [cw, cb, wa, ba, wx, bx, lam],
                    [(256, BF16)],
                    [((4, 256), F32), ((1, 256), F32), ((256, 256), F32), ((1, 256), F32), ((256, 256), F32),
                     ((1, 256), F32), ((1, 256), F32)],
                    tile=seq, scratch=[pltpu.VMEM((seq, 256), F32)] * 3)


FOX_SCALE = HEAD_DIM ** -0.5
FOX_BLOCK = 512


class _Side:
    def __init__(self, inputs, out_shape, sems, build):
        self.inputs, self.out_shape, self.sems, self.build = list(inputs), list(out_shape), list(sems), build

    def specs(self):
        any_spec = pl.BlockSpec(memory_space=pl.ANY)
        return [any_spec] * len(self.inputs), [any_spec] * len(self.out_shape)

    def run(self, refs, first, last):
        n_in, n_out = len(self.inputs), len(self.out_shape)
        in_refs, out_refs, sem_refs = refs[:n_in], refs[n_in:n_in + n_out], refs[n_in + n_out:]

        @pl.when(first)
        def _():
            for cp in self.build(in_refs, out_refs, sem_refs):
                cp.start()

        @pl.when(last)
        def _():
            for cp in self.build(in_refs, out_refs, sem_refs):
                cp.wait()


def _grid_ends(grid):
    ids = [pl.program_id(a) for a in range(len(grid))]
    first = functools.reduce(jnp.logical_and, [i == 0 for i in ids])
    last = functools.reduce(jnp.logical_and, [i == n - 1 for i, n in zip(ids, grid)])
    return first, last


Q_BLK, K_BLK, V_BLK = 1920 // 128, 2304 // 128, 2688 // 128
FOX_PAIRS = FOX_HEADS // 2


def _fox_bias(cum, nb, seq):
    cf = cum.reshape(nb, seq, 128)[:, :, SSD_HEADS:SSD_HEADS + FOX_HEADS]
    cols = jnp.pad(cf.reshape(nb * seq, FOX_PAIRS, 2), ((0, 0), (0, 0), (0, 126))).reshape(nb * seq, 384)
    rows = jnp.pad(cf.transpose(0, 2, 1).reshape(nb, FOX_PAIRS, 2, seq), ((0, 0), (0, 0), (0, 6), (0, 0)))
    return cols, rows


def _fox_fwd(proj, bias_cols, bias_rows, nb, seq, side=None):
    tb = min(FOX_BLOCK, seq)
    nq = seq // tb
    grid = (nb, FOX_PAIRS, nq)
    n_in = len(side.inputs) if side else 0

    def body(*refs):
        q_ref, k_ref, v_ref, cq_ref, ck_ref = refs[:5]
        o_ref, lse_ref = refs[5 + n_in:7 + n_in]
        if side is not None:
            side.run(refs[5:5 + n_in] + refs[7 + n_in:], *_grid_ends(grid))
        qi = pl.program_id(2)
        low = lax.broadcasted_iota(jnp.int32, (tb, 128), 1) < HEAD_DIM
        q2 = q_ref[...] * FOX_SCALE
        qm = [jnp.where(low, q2, 0.0).astype(BF16), jnp.where(low, 0.0, q2).astype(BF16)]
        cqs = [cq_ref[:, 0:1], cq_ref[:, 1:2]]

        def block(j, carry, diagonal):
            cols = pl.ds(pl.multiple_of(j * tb, tb), tb)
            k2 = k_ref[cols, :].astype(BF16)
            v2 = v_ref[cols, :].astype(BF16)
            new = []
            for hh in range(2):
                m_i, l_i, acc = carry[hh]
                s = _dot_nt(qm[hh], k2) + cqs[hh] - ck_ref[0, 0, hh:hh + 1, cols]
                if diagonal:
                    s = jnp.where(lax.broadcasted_iota(jnp.int32, (tb, tb), 0)
                                  >= lax.broadcasted_iota(jnp.int32, (tb, tb), 1), s, NEG)
                m_new = jnp.maximum(m_i, jnp.max(s, axis=1, keepdims=True))
                p = jnp.exp(s - m_new)
                alpha = jnp.exp(m_i - m_new)
                new.append((m_new, alpha * l_i + jnp.sum(p, axis=1, keepdims=True),
                            alpha * acc + _dot(p.astype(BF16), v2)))
            return tuple(new)

        one = (jnp.full((tb, 1), NEG, F32), jnp.zeros((tb, 1), F32), jnp.zeros((tb, 128), F32))
        carry = lax.fori_loop(0, qi, lambda j, cr: block(j, cr, False), (one, one))
        (m0, l0, a0), (m1, l1, a1) = block(qi, carry, True)
        o_ref[...] = jnp.where(low, a0 / l0, a1 / l1)
        lse_ref[...] = jnp.where(low, m0 + jnp.log(l0), m1 + jnp.log(l1))

    def blk(first):
        return pl.BlockSpec((tb, 128), lambda b, p, i: (b * nq + i, first + p))

    def seq_blk(first):
        return pl.BlockSpec((seq, 128), lambda b, p, i: (b, first + p))

    row_spec = pl.BlockSpec((1, 1, 8, seq), lambda b, p, i: (b, p, 0, 0))
    side_in, side_out = side.specs() if side else ([], [])
    shape = jax.ShapeDtypeStruct((nb * seq, FOX_WIDTH), F32)
    return pl.pallas_call(
        body, name="fox_fwd", grid=grid,
        out_shape=[shape, shape] + (side.out_shape if side else []),
        in_specs=[blk(Q_BLK), seq_blk(K_BLK), seq_blk(V_BLK), blk(0), row_spec] + side_in,
        out_specs=[blk(0), blk(0)] + side_out,
        scratch_shapes=side.sems if side else [],
        compiler_params=_params(("arbitrary", "arbitrary", "arbitrary")),
    )(proj, proj, proj, bias_cols, bias_rows, *(side.inputs if side else []))


def _fox_bwd(proj, o, lse, do, bias_cols, bias_rows, nb, seq, side=None):
    tb = min(FOX_BLOCK, seq)
    nq = seq // tb
    grid = (nb, FOX_PAIRS, nq)
    n_in = len(side.inputs) if side else 0

    def body(*refs):
        q_ref, k_ref, v_ref, o_ref, lse_ref, do_ref, cq_ref, ck_ref = refs[:8]
        dq_ref, dk_ref, dv_ref, dcum_ref, dcq_ref = refs[8 + n_in:13 + n_in]
        if side is not None:
            side.run(refs[8:8 + n_in] + refs[13 + n_in:], *_grid_ends(grid))
        kj = pl.program_id(2)

        @pl.when(kj == 0)
        def _():
            dq_ref[...] = jnp.zeros_like(dq_ref)
            dcq_ref[...] = jnp.zeros_like(dcq_ref)

        lane = lax.broadcasted_iota(jnp.int32, (tb, 128), 1)
        low = lane < HEAD_DIM
        mine = [low, jnp.logical_not(low)]
        k2 = k_ref[...]
        kb = k2.astype(BF16)
        km = [jnp.where(mine[hh], k2, 0.0).astype(BF16) for hh in range(2)]
        vb = v_ref[...].astype(BF16)

        def block(i, carry, diagonal):
            dk, dv, c0, c1 = carry
            csum = [c0, c1]
            rows = pl.ds(pl.multiple_of(i * tb, tb), tb)
            q2 = q_ref[rows, :] * FOX_SCALE
            do2 = do_ref[rows, :]
            prod = do2 * o_ref[rows, :]
            dq_add = jnp.zeros((tb, 128), F32)
            rsum = []
            for hh in range(2):
                qm = jnp.where(mine[hh], q2, 0.0).astype(BF16)
                dom = jnp.where(mine[hh], do2, 0.0).astype(BF16)
                delta = jnp.sum(jnp.where(mine[hh], prod, 0.0), axis=1, keepdims=True)
                s = _dot_nt(qm, kb) + cq_ref[rows, hh:hh + 1] - ck_ref[0, 0, hh:hh + 1, :]
                if diagonal:
                    s = jnp.where(lax.broadcasted_iota(jnp.int32, (tb, tb), 0)
                                  >= lax.broadcasted_iota(jnp.int32, (tb, tb), 1), s, NEG)
                p = jnp.exp(s - lse_ref[rows, HEAD_DIM * hh:HEAD_DIM * hh + 1])
                ds = p * (_dot_nt(dom, vb) - delta)
                dsb = ds.astype(BF16)
                dv = dv + _dot_tn(p.astype(BF16), dom)
                dk = dk + _dot_tn(dsb, qm)
                dq_add = dq_add + _dot(dsb, km[hh])
                rsum.append(jnp.sum(ds, axis=1, keepdims=True))
                csum[hh] = csum[hh] + jnp.sum(ds, axis=0, keepdims=True)
            dq_ref[rows, :] += dq_add * FOX_SCALE
            dcq_ref[rows, :] += jnp.where(lane == 0, rsum[0], jnp.where(lane == 1, rsum[1], 0.0))
            return dk, dv, csum[0], csum[1]

        init = (jnp.zeros((tb, 128), F32), jnp.zeros((tb, 128), F32), jnp.zeros((1, tb), F32),
                jnp.zeros((1, tb), F32))
        carry = block(kj, init, True)
        dk, dv, c0, c1 = lax.fori_loop(kj + 1, nq, lambda i, cr: block(i, cr, False), carry)
        dk_ref[...] = dk.astype(dk_ref.dtype)
        dv_ref[...] = dv.astype(dv_ref.dtype)
        row = lax.broadcasted_iota(jnp.int32, (8, tb), 0)
        dcum_ref[0, 0] = jnp.where(row == 0, -c0, jnp.where(row == 1, -c1, 0.0))

    def blk(first):
        return pl.BlockSpec((tb, 128), lambda b, p, j: (b * nq + j, first + p))

    def seq_blk(first):
        return pl.BlockSpec((seq, 128), lambda b, p, j: (b, first + p))

    row_blk = pl.BlockSpec((1, 1, 8, tb), lambda b, p, j: (b, p, 0, j))
    side_in, side_out = side.specs() if side else ([], [])
    tokens = nb * seq
    return pl.pallas_call(
        body, name="fox_bwd", grid=grid,
        out_shape=[jax.ShapeDtypeStruct((tokens, FOX_WIDTH), F32), jax.ShapeDtypeStruct((tokens, FOX_WIDTH), BF16),
                   jax.ShapeDtypeStruct((tokens, FOX_WIDTH), BF16),
                   jax.ShapeDtypeStruct((nb, FOX_PAIRS, 8, seq), F32),
                   jax.ShapeDtypeStruct((tokens, FOX_WIDTH), F32)] + (side.out_shape if side else []),
        in_specs=[seq_blk(Q_BLK), blk(K_BLK), blk(V_BLK), seq_blk(0), seq_blk(0), seq_blk(0), seq_blk(0), row_blk]
        + side_in,
        out_specs=[seq_blk(0), blk(0), blk(0), row_blk, seq_blk(0)] + side_out,
        scratch_shapes=side.sems if side else [],
        compiler_params=_params(("arbitrary", "arbitrary", "arbitrary")),
    )(proj, proj, proj, o, lse, do, bias_cols, bias_rows, *(side.inputs if side else []))


_ANY = pl.BlockSpec(memory_space=pl.ANY)


def _place():
    return lax.axis_index("x"), lax.axis_index("y"), lax.axis_index("c")


def _all_gather(shards, name):
    n = len(shards)

    def body(*refs):
        x_refs, out_refs = refs[:n], refs[n:2 * n]
        send_sems, recv_sems, local_sems = refs[2 * n:]
        x, y, c = _place()
        me, sibling = (x, y, c), (x, y, 1 - c)
        chips = [(1 - x, y), (x, 1 - y), (1 - x, 1 - y)]

        def rows(a, px, py, pc):
            return out_refs[a].at[4 * px + 2 * py + pc]

        def copy(a, k, block, to, src=None):
            return pltpu.make_async_remote_copy(
                src_ref=rows(a, *block) if src is None else src, dst_ref=rows(a, *block),
                send_sem=send_sems.at[a, k], recv_sem=recv_sems.at[a, k],
                device_id=to, device_id_type=pl.DeviceIdType.MESH)

        mine = [pltpu.make_async_copy(x_refs[a], rows(a, *me), local_sems.at[a]) for a in range(n)]
        for cp in mine:
            cp.start()
        first = []
        for a in range(n):
            first.append(copy(a, 0, me, sibling, src=x_refs[a]))
            first += [copy(a, 1 + j, me, (*chip, c), src=x_refs[a]) for j, chip in enumerate(chips)]
        for cp in first:
            cp.start()
        passed = []
        for j, chip in enumerate(chips):
            for a in range(n):
                copy(a, 1 + j, (*chip, c), me).wait_recv()
                passed.append(copy(a, 4 + j, (*chip, c), sibling))
                passed[-1].start()
        for a in range(n):
            copy(a, 0, sibling, me).wait_recv()
            for j, chip in enumerate(chips):
                copy(a, 4 + j, (*chip, 1 - c), me).wait_recv()
        for cp in first + passed:
            cp.wait_send()
        for cp in mine:
            cp.wait()

    return pl.pallas_call(
        body, name=name,
        out_shape=[jax.ShapeDtypeStruct((N_DEV,) + s.shape, s.dtype) for s in shards],
        in_specs=[_ANY] * n, out_specs=[_ANY] * n,
        scratch_shapes=[pltpu.SemaphoreType.DMA((n, 7)), pltpu.SemaphoreType.DMA((n, 7)),
                        pltpu.SemaphoreType.DMA((n,))],
    )(*shards)


def _remote(src, dst, send_sem, recv_sem, to):
    return pltpu.make_async_remote_copy(src_ref=src, dst_ref=dst, send_sem=send_sem, recv_sem=recv_sem,
                                        device_id=to, device_id_type=pl.DeviceIdType.MESH)


def _sem_pairs(n, k):
    return [pltpu.SemaphoreType.DMA((n, k)), pltpu.SemaphoreType.DMA((n, k))]


def _sibling_side(full, offsets):
    def build(g_refs, out_refs, sems):
        x, y, c = _place()
        return [_remote(g_refs[a].at[offsets[a] + 4 * (k // 2) + 2 * (k % 2) + (1 - c)], out_refs[a].at[k],
                        sems[0].at[a, k], sems[1].at[a, k], (x, y, 1 - c))
                for a in range(len(g_refs)) for k in range(4)]

    return _Side(full, [jax.ShapeDtypeStruct((4,) + f.shape[1:], f.dtype) for f in full],
                 _sem_pairs(len(full), 4), build)


def _chip_side(part):
    def build(p_refs, out_refs, sems):
        x, y, c = _place()
        peers = [(1 - x, y), (x, 1 - y), (1 - x, 1 - y)]
        return [_remote(p_refs[a].at[2 * px + py], out_refs[a].at[k], sems[0].at[a, k], sems[1].at[a, k],
                        (px, py, c))
                for a in range(len(p_refs)) for k, (px, py) in enumerate(peers)]

    return _Side(part, [jax.ShapeDtypeStruct((3,) + p.shape[1:], p.dtype) for p in part],
                 _sem_pairs(len(part), 3), build)


def _spread_side(shards):
    def build(x_refs, out_refs, sems):
        x, y, c = _place()
        targets = [(x, y, 1 - c), (1 - x, y, c), (x, 1 - y, c), (1 - x, 1 - y, c)]
        cps = []
        for a in range(len(x_refs)):
            slot = out_refs[a].at[4 * x + 2 * y + c]
            cps.append(pltpu.make_async_copy(x_refs[a], slot, sems[2].at[a]))
            cps += [_remote(x_refs[a], slot, sems[0].at[a, k], sems[1].at[a, k], to)
                    for k, to in enumerate(targets)]
        return cps

    n = len(shards)
    return _Side(shards, [jax.ShapeDtypeStruct((N_DEV,) + s.shape, s.dtype) for s in shards],
                 _sem_pairs(n, 4) + [pltpu.SemaphoreType.DMA((n,))], build)


def _pass_side(bufs):
    def build(in_refs, out_refs, sems):
        x, y, c = _place()
        chips = [(1 - x, y), (x, 1 - y), (1 - x, 1 - y)]
        return [_remote(in_refs[a].at[4 * px + 2 * py + c], out_refs[a].at[4 * px + 2 * py + c],
                        sems[0].at[a, j], sems[1].at[a, j], (x, y, 1 - c))
                for a in range(len(in_refs)) for j, (px, py) in enumerate(chips)]

    return _Side(bufs, [jax.ShapeDtypeStruct(b.shape, b.dtype) for b in bufs], _sem_pairs(len(bufs), 3), build)


def _run_side(side, name, in_place=False):
    n_in = len(side.inputs)

    def body(*refs):
        copies = side.build(refs[:n_in], refs[n_in:n_in + len(side.out_shape)],
                            refs[n_in + len(side.out_shape):])
        for cp in copies:
            cp.start()
        for cp in copies:
            cp.wait()

    in_specs, out_specs = side.specs()
    return pl.pallas_call(
        body, name=name, out_shape=side.out_shape, in_specs=in_specs, out_specs=out_specs,
        scratch_shapes=side.sems,
        input_output_aliases={a: a for a in range(n_in)} if in_place else {},
    )(*side.inputs)


def _pick_rows(rows, cap=512):
    t = cap
    while t >= 8:
        if rows % t == 0:
            return t
        t //= 2
    raise ValueError(rows)


def _pair_sum(full, offset, got, name):
    _, rows, cols = full.shape
    tile = _pick_rows(rows, 256)
    c = lax.axis_index("c").astype(jnp.int32).reshape(1)

    def body(c_ref, a_ref, b_ref, o_ref):
        o_ref[...] = a_ref[...] + b_ref[...]

    blk = (1, tile, cols)
    return pl.pallas_call(
        body, name=name,
        grid_spec=pltpu.PrefetchScalarGridSpec(
            num_scalar_prefetch=1, grid=(4, rows // tile),
            in_specs=[pl.BlockSpec(blk, lambda k, i, c_ref: (offset + 4 * (k // 2) + 2 * (k % 2) + c_ref[0], i, 0)),
                      pl.BlockSpec(blk, lambda k, i, c_ref: (k, i, 0))],
            out_specs=pl.BlockSpec(blk, lambda k, i, c_ref: (k, i, 0))),
        out_shape=jax.ShapeDtypeStruct((4, rows, cols), full.dtype),
        compiler_params=_params(("arbitrary", "arbitrary")),
    )(c, full, got)


def _adam_math(w, g, m, v):
    c1 = 1.0 / (1.0 - ADAM_B1 ** ADAM_STEP)
    c2 = 1.0 / (1.0 - ADAM_B2 ** ADAM_STEP)
    m_new = ADAM_B1 * m + (1.0 - ADAM_B1) * g
    v_new = ADAM_B2 * v + (1.0 - ADAM_B2) * (g * g)
    delta = -ADAM_LR * ((m_new * c1) / (jnp.sqrt(v_new * c2) + ADAM_EPS) + ADAM_WD * w)
    return delta, m_new, v_new


def _sum_adamw(part, others, w, m, v, name):
    _, rows, cols = part.shape
    tile = _pick_rows(rows, 128)
    own = (2 * lax.axis_index("x") + lax.axis_index("y")).astype(jnp.int32).reshape(1)

    def body(own_ref, p_ref, o_ref, w_ref, m_ref, v_ref, g_out, d_out, m_out, v_out):
        g = ((p_ref[0] + o_ref[0]) + o_ref[1]) + o_ref[2]
        delta, m_new, v_new = _adam_math(w_ref[...], g, m_ref[...], v_ref[...])
        g_out[...] = g
        d_out[...] = delta
        m_out[...] = m_new
        v_out[...] = v_new

    flat = pl.BlockSpec((tile, cols), lambda i, own_ref: (i, 0))
    shape = jax.ShapeDtypeStruct((rows, cols), F32)
    return pl.pallas_call(
        body, name=name,
        grid_spec=pltpu.PrefetchScalarGridSpec(
            num_scalar_prefetch=1, grid=(rows // tile,),
            in_specs=[pl.BlockSpec((1, tile, cols), lambda i, own_ref: (own_ref[0], i, 0)),
                      pl.BlockSpec((3, tile, cols), lambda i, own_ref: (0, i, 0)), flat, flat, flat],
            out_specs=[flat] * 4),
        out_shape=[shape] * 4,
        compiler_params=_params(("arbitrary",)),
    )(own, part, others, w, m, v)


def _all_reduce_small(vec):
    gathered = _all_gather([vec], "ar_gather")[0]
    rows = vec.shape[0]

    def fn(*blocks):
        s = blocks[0]
        for b in blocks[1:]:
            s = s + b
        return s

    return _rowcall("ar_sum", fn, [gathered[j] for j in range(N_DEV)], [], [(1024, F32)],
                    tile=_pick_rows(rows))[0]


def _pad_rows(flat, mult):
    n = flat.shape[-1]
    per = mult * 1024
    padded = -(-n // per) * per
    pad = [(0, 0)] * (flat.ndim - 1) + [(0, padded - n)]
    return jnp.pad(flat, pad).reshape(flat.shape[:-1] + (padded // 1024, 1024))


def _regroup_w_in(w):
    pad = jnp.zeros((w.shape[0], 116), w.dtype)
    return jnp.concatenate([w[:, 768:1280], w[:, 1286:1798], w[:, 1280:1286], w[:, 2950:2956], pad,
                            w[:, 0:768], w[:, 1798:2950]], axis=1)


def _ungroup_w_in(wp):
    return jnp.concatenate([wp[:, 1152:1920], wp[:, 0:512], wp[:, 1024:1030], wp[:, 512:1024],
                            wp[:, 1920:3072], wp[:, 1030:1036]], axis=1)


def _to_shard(name, a):
    if name == 'w_in':
        return _regroup_w_in(a)
    if name in ('w_gate', 'w_up'):
        return jnp.pad(a, ((0, 0), (0, FF_SHARD_P - FF_SHARD)))
    if name == 'w_down':
        return jnp.pad(a, ((0, FF_SHARD_P - FF_SHARD), (0, 0)))
    return a


def _from_shard(name, a):
    if name == 'w_in':
        return _ungroup_w_in(a)
    if name in ('w_gate', 'w_up'):
        return a[:, 0:FF_SHARD]
    if name == 'w_down':
        return a[0:FF_SHARD, :]
    return a


def _whole(name, gathered):
    if BIG[name][0] == 1:
        return gathered.reshape(-1, gathered.shape[-1])
    return gathered.transpose(1, 0, 2).reshape(gathered.shape[1], -1)


def _split(name, whole):
    if BIG[name][0] == 1:
        return whole.reshape(N_DEV, whole.shape[0] // N_DEV, whole.shape[1])
    return whole.reshape(whole.shape[0], N_DEV, whole.shape[1] // N_DEV).transpose(1, 0, 2)


def _pack_list(arrays, mult):
    return _pad_rows(jnp.concatenate([a.reshape(-1) for a in arrays]), mult)


def _unpack_list(buf, shapes):
    flat = buf.reshape(-1)
    out, off = [], 0
    for s in shapes:
        n = math.prod(s)
        out.append(flat[off:off + n].reshape(s))
        off += n
    return out


def _adamw(w, g, m, v):
    return _rowcall("adamw", _adam_math, [w, g, m, v], [], [(1024, F32)] * 3, tile=_pick_rows(w.shape[0]))


def _block_diag(w):
    out = jnp.zeros((LRU_WIDTH, LRU_WIDTH), w.dtype)
    for g in range(4):
        out = lax.dynamic_update_slice(out, w[g], (64 * g, 64 * g))
    return out


def _block_diag_grad(full):
    return jnp.stack([full[64 * g:64 * (g + 1), 64 * g:64 * (g + 1)] for g in range(4)])


def _row(v):
    return v.reshape(1, -1).astype(F32)


def _lane128(*pieces):
    flat = jnp.concatenate([p.reshape(-1).astype(F32) for p in pieces])
    return jnp.pad(flat, (0, 128 - flat.shape[0])).reshape(1, 128)


def _layer_consts(w):
    c = {}
    cw, cb = w['ssd_conv_w'], w['ssd_conv_b']
    c['cw_x'], c['cw_b'], c['cw_c'] = cw[:, 0:384], cw[:, 384:640], cw[:, 640:896]
    c['cb_x'], c['cb_b'], c['cb_c'] = _row(cb[0:384]), _row(cb[384:640]), _row(cb[640:896])
    c['bias128'] = _lane128(w['ssd_dt_bias'], w['fox_b_f'])
    c['alog128'] = _lane128(w['ssd_a_log'])
    c['d384'] = _row(jnp.repeat(w['ssd_d'], HEAD_DIM))
    c['lcw'], c['lcb'] = w['lru_conv_w'], _row(w['lru_conv_b'])
    c['wa'], c['wx'] = _block_diag(w['lru_w_a']).astype(BF16), _block_diag(w['lru_w_x']).astype(BF16)
    c['ba'], c['bx'], c['lam'] = _row(w['lru_b_a']), _row(w['lru_b_x']), _row(w['lru_lambda'])
    return c


def _layer_fwd(h0, p_i, w, c, hooks=None, layer=0):
    nb, seq = c['nb'], c['seq']

    def carried(stage):
        return hooks.fwd_side(layer, stage) if hooks is not None else None

    def arrived(outs):
        if hooks is not None:
            hooks.fwd_done(outs)
            w.update(hooks.weights(layer))

    u1 = _rowcall("norm1", lambda h, g: _rms(h, g), [h0], [_row(w['norm1_g'])], [(D_MODEL, BF16)])[0]
    proj = _matmul(u1, w['w_in'], 'nn', "proj")

    xs_c = _convsilu_fwd("conv_x", (proj, 384, 4), seq, c['cw_x'], c['cb_x'], F32)
    b_c = _convsilu_fwd("conv_b", (proj, 256, 0), seq, c['cw_b'], c['cb_b'], BF16)
    c_c = _convsilu_fwd("conv_c", (proj, 256, 1), seq, c['cw_c'], c['cb_c'], BF16)
    dt_arr, cum, prev, end = _small_fwd((proj, 128, 8), seq, c['bias128'], c['alog128'])
    x_h = xs_c.reshape(nb, seq, SSD_WIDTH)
    cum3 = cum.reshape(nb, seq, 128)
    a_row = cum3[:, :, 0:8].transpose(0, 2, 1)
    ssd_in = (x_h, b_c.reshape(nb, seq, 256), c_c.reshape(nb, seq, 256), dt_arr.reshape(nb, seq, 128), cum3,
              prev.reshape(nb, seq, 128), end.reshape(nb, seq, 128), a_row)
    y_h, states = _ssd_fwd(*ssd_in)
    y_core = y_h.reshape(nb * seq, SSD_WIDTH)

    hseq, xl = _lru_fwd(proj, seq, c['lcw'], c['lcb'], c['wa'], c['ba'], c['wx'], c['bx'], c['lam'])

    bias_cols, bias_rows = _fox_bias(cum, nb, seq)
    y_fox, lse, *side_out = _fox_fwd(proj, bias_cols, bias_rows, nb, seq, carried('attention'))
    arrived(side_out)

    def post(yc, xs, z, hs, lg, yf, d, g1, g2, g3):
        y1 = _rms((yc + xs * d) * _silu(z), g1)
        y2 = _rms(hs * _gelu(lg), g2)
        y3 = _rms(yf, g3)
        return jnp.concatenate([y1, y2, y3], axis=-1)

    post_consts = [c['d384'], _row(w['ssd_norm_g']), _row(w['lru_norm_g']), _row(w['fox_norm_g'])]
    ycat = _rowcall("mix_post", post, [y_core, xs_c, (proj, 384, 3), hseq, (proj, 256, 3), y_fox], post_consts,
                    [(D_MODEL, BF16)])[0]
    mix = _matmul(ycat, w['w_out'], 'nn', "mix_out")

    def res_norm(h, d, g):
        hn = h + d
        return hn, _rms(hn, g)

    h1, u2 = _rowcall("res_norm2", res_norm, [h0, mix], [_row(w['norm2_g'])], [(D_MODEL, F32), (D_MODEL, BF16)])
    side = carried('ffn_in')
    if side is None:
        gu = _matmul(u2, w['w_gu'], 'nn', "ffn_in", BF16)
    else:
        gu, *side_out = _matmul(u2, w['w_gu'], 'nn', "ffn_in", BF16, side=side)
        arrived(side_out)
    act = _rowcall("swiglu", lambda gt, up: _silu(gt.astype(F32)) * up.astype(F32),
                   [(gu, D_FF_P, 0), (gu, D_FF_P, 1)], [], [(D_FF_P, BF16)])[0]
    ff = _matmul(act, w['w_down'], 'nn', "ffn_out")
    h2, u3 = _rowcall("res_norm3", res_norm, [h1, ff], [_row(w['norm3_g'])], [(D_MODEL, F32), (D_MODEL, BF16)])
    pg = _matmul(u3, w['w_ple_gate'], 'nn', "ple_gate")
    pp = _matmul(p_i, w['w_ple_proj'], 'nn', "ple_proj")
    h3 = _rowcall("ple", lambda h, a, b, bias: h + _sigmoid(a + bias) * b, [h2, pg, pp], [_row(w['b_ple_gate'])],
                  [(D_MODEL, F32)])[0]
    saved = dict(h0=h0, u1=u1, proj=proj, xs_c=xs_c, dt_arr=dt_arr, ssd_in=ssd_in, states=states,
                 y_core=y_core, hseq=hseq, xl=xl, bias_cols=bias_cols, bias_rows=bias_rows, lse=lse,
                 y_fox=y_fox, post_consts=post_consts, ycat=ycat, h1=h1, u2=u2, gu=gu, act=act, h2=h2, u3=u3,
                 pg=pg, pp=pp, p_i=p_i)
    return h3, saved


def _layer_bwd(dh3, s, w, c, hooks=None, layer=0):
    nb, seq = c['nb'], c['seq']
    g = {}

    def ple_bwd(dh, a, b, bias):
        gate = _sigmoid(a + bias)
        dpg = dh * b * gate * (1.0 - gate)
        return dh * gate, dpg, jnp.sum(dpg, axis=0, keepdims=True)

    dpp, dpg, g['b_ple_gate'] = _rowcall("ple_bwd", ple_bwd, [dh3, s['pg'], s['pp']], [_row(w['b_ple_gate'])],
                                         [(D_MODEL, BF16), (D_MODEL, BF16)], [((1, D_MODEL), F32)])
    g['w_ple_proj'] = _matmul(s['p_i'], dpp, 'tn', "d_w_ple_proj")
    g['w_ple_gate'] = _matmul(s['u3'], dpg, 'tn', "d_w_ple_gate")
    du3 = _matmul(dpg, w['w_ple_gate'], 'nt', "d_u3")

    def norm_bwd(h, du, dh, gain):
        dx, dg = _rms_bwd(h, gain, du)
        dhn = dh + dx
        return dhn, dhn, dg

    dh2, dh2_b, g['norm3_g'] = _rowcall("norm3_bwd", norm_bwd, [s['h2'], du3, dh3], [_row(w['norm3_g'])],
                                        [(D_MODEL, F32), (D_MODEL, BF16)], [((1, D_MODEL), F32)])
    g['w_down'] = _matmul(s['act'], dh2_b, 'tn', "d_w_down")
    dact = _matmul(dh2_b, w['w_down'], 'nt', "d_act", BF16)

    def swiglu_bwd(gt, up, da):
        gt, up, da = gt.astype(F32), up.astype(F32), da.astype(F32)
        return jnp.concatenate([da * up * _dsilu(gt), da * _silu(gt)], axis=-1)

    dgu = _rowcall("swiglu_bwd", swiglu_bwd, [(s['gu'], D_FF_P, 0), (s['gu'], D_FF_P, 1), dact], [],
                   [(2 * D_FF_P, BF16)])[0]
    gu16 = _matmul(s['u2'], dgu, 'tn', "d_w_gu", shard_n=FF_SHARD_P)
    du2 = _matmul(dgu, w['w_gu'], 'nt', "d_u2")
    dh1, dh1_b, g['norm2_g'] = _rowcall("norm2_bwd", norm_bwd, [s['h1'], du2, dh2], [_row(w['norm2_g'])],
                                        [(D_MODEL, F32), (D_MODEL, BF16)], [((1, D_MODEL), F32)])
    g['w_out'] = _matmul(s['ycat'], dh1_b, 'tn', "d_w_out")
    if hooks is None:
        dycat = _matmul(dh1_b, w['w_out'], 'nt', "d_ycat")
    else:
        ready = {n: g[n] for n in ('w_out', 'w_down', 'w_ple_gate', 'w_ple_proj')}
        ready['w_gate'], ready['w_up'] = (gu16, 0), (gu16, N_DEV)
        dycat, *side_out = _matmul(dh1_b, w['w_out'], 'nt', "d_ycat", side=hooks.sibling_side(layer, ready))
        hooks.sibling_done(side_out)

    def post_bwd(dy, yc, xs, z, hs, lg, yf, d, g1, g2, g3):
        sz = _silu(z)
        ytot = yc + xs * d
        dpre1, dg1 = _rms_bwd(ytot * sz, g1, dy[:, 0:384])
        dytot = dpre1 * sz
        dz = dpre1 * ytot * _dsilu(z)
        dd = jnp.sum(dytot * xs, axis=0, keepdims=True)
        gl = _gelu(lg)
        dpre2, dg2 = _rms_bwd(hs * gl, g2, dy[:, 384:640])
        dyf, dg3 = _rms_bwd(yf, g3, dy[:, 640:1024])
        return dytot, dytot * d, dz, dpre2 * gl, dpre2 * hs * _dgelu(lg), dyf, dd, dg1, dg2, dg3

    (dy_core, dxs_skip, dz, dhseq, dlg, dy_fox, dd384, g['ssd_norm_g'], g['lru_norm_g'], g['fox_norm_g']) = _rowcall(
        "mix_post_bwd", post_bwd,
        [dycat, s['y_core'], s['xs_c'], (s['proj'], 384, 3), s['hseq'], (s['proj'], 256, 3), s['y_fox']],
        s['post_consts'],
        [(384, F32), (384, F32), (384, BF16), (256, F32), (256, BF16), (384, F32)],
        [((1, 384), F32), ((1, 384), F32), ((1, 256), F32), ((1, 384), F32)])
    g['ssd_d'] = dd384.reshape(SSD_HEADS, HEAD_DIM).sum(axis=1)

    side = hooks.bwd_side() if hooks is not None else None
    dq, dk, dv, dcf_rows, dcf_cols, *side_out = _fox_bwd(s['proj'], s['y_fox'], s['lse'], dy_fox, s['bias_cols'],
                                                         s['bias_rows'], nb, seq, side)
    if hooks is not None:
        hooks.bwd_done(side_out)
    dq = dq.astype(BF16)

    dx_h, db_c, dc_c, da_arr, dend_arr, ddt_arr = _ssd_bwd(*s['ssd_in'], s['states'],
                                                           dy_core.reshape(nb, seq, SSD_WIDTH))
    dxs_c = dx_h.reshape(nb * seq, SSD_WIDTH) + dxs_skip
    dcf = dcf_rows[:, :, 0:2, :].reshape(nb, FOX_HEADS, seq).transpose(0, 2, 1)
    dcum = jnp.concatenate([da_arr[:, :, 0:SSD_HEADS], dcf,
                            jnp.zeros((nb, seq, 128 - 2 * SSD_HEADS), F32)], axis=-1).reshape(nb * seq, 128)
    proj = s['proj']
    dxs_raw, dcw_x, dcb_x = _convsilu_bwd("conv_x_bwd", (proj, 384, 4), seq, c['cw_x'], c['cb_x'], dxs_c)
    db_raw, dcw_b, dcb_b = _convsilu_bwd("conv_b_bwd", (proj, 256, 0), seq, c['cw_b'], c['cb_b'],
                                         db_c.reshape(nb * seq, 256))
    dc_raw, dcw_c, dcb_c = _convsilu_bwd("conv_c_bwd", (proj, 256, 1), seq, c['cw_c'], c['cb_c'],
                                         dc_c.reshape(nb * seq, 256))
    dsmall, dbias128, dalog128 = _small_bwd((proj, 128, 8), seq, dcum, dcf_cols, dend_arr.reshape(nb * seq, 128),
                                            ddt_arr.reshape(nb * seq, 128), s['dt_arr'], c['bias128'], c['alog128'])
    g['ssd_conv_w'] = jnp.concatenate([dcw_x, dcw_b, dcw_c], axis=1)
    g['ssd_conv_b'] = jnp.concatenate([dcb_x, dcb_b, dcb_c], axis=1).reshape(-1)
    g['ssd_dt_bias'] = dbias128[0, 0:SSD_HEADS]
    g['fox_b_f'] = dbias128[0, SSD_HEADS:2 * SSD_HEADS]
    g['ssd_a_log'] = dalog128[0, 0:SSD_HEADS]

    (dlru_raw, g['lru_conv_w'], dlcb, dwa, dba, dwx, dbx, dlam) = _lru_bwd(
        s['proj'], seq, s['xl'], s['hseq'], dhseq, c['lcw'], c['lcb'], c['wa'], c['ba'], c['wx'], c['bx'], c['lam'])
    g['lru_conv_b'], g['lru_b_a'], g['lru_b_x'], g['lru_lambda'] = (t.reshape(-1) for t in (dlcb, dba, dbx, dlam))
    g['lru_w_a'], g['lru_w_x'] = _block_diag_grad(dwa), _block_diag_grad(dwx)

    dproj = jnp.concatenate([db_raw, dc_raw, dlru_raw, dlg, dsmall, dz, dxs_raw, dq, dk, dv], axis=1)
    g['w_in'] = _matmul(s['u1'], dproj, 'tn', "d_w_in")
    if hooks is None:
        du1 = _matmul(dproj, w['w_in'], 'nt', "d_u1")
    else:
        hooks.bwd_ready(layer, {'w_in': g['w_in']})
        du1, *side_out = _matmul(dproj, w['w_in'], 'nt', "d_u1", side=hooks.bwd_side())
        hooks.bwd_done(side_out)

    def norm1_bwd(h, du, dh, gain):
        dx, dg = _rms_bwd(h, gain, du)
        return dh + dx, dg

    dh0, g['norm1_g'] = _rowcall("norm1_bwd", norm1_bwd, [s['h0'], du1, dh1], [_row(w['norm1_g'])],
                                 [(D_MODEL, F32)], [((1, D_MODEL), F32)])
    for name in ('b_ple_gate', 'norm3_g', 'norm2_g', 'norm1_g', 'ssd_norm_g', 'lru_norm_g', 'fox_norm_g'):
        g[name] = g[name].reshape(-1)
    g['w_gate'], g['w_up'] = None, None
    if hooks is None:
        g['w_gate'] = gu16[0:N_DEV].transpose(1, 0, 2).reshape(D_MODEL, D_FF_P)
        g['w_up'] = gu16[N_DEV:2 * N_DEV].transpose(1, 0, 2).reshape(D_MODEL, D_FF_P)
    return dh0, g


class _Hooks:
    def __init__(self, shard):
        self.shard = shard
        self.whole = {}
        self.part, self.others = {}, {}
        self.pending, self.flying = [], []

    def first(self, extra):
        got = _all_gather([self.shard['w_in', 0]] + extra, "gather_first")
        self.whole['w_in', 0] = _whole('w_in', got[0])
        return got[1:]

    def fwd_side(self, layer, stage):
        if layer != 0:
            return None
        if stage == 'attention':
            self.flying = [(n, 0) for n in BIG if n != 'w_in']
        else:
            self.flying = [(n, 1) for n in BIG]
        return _spread_side([self.shard[k] for k in self.flying])

    def fwd_done(self, outs):
        if self.flying:
            passed = _run_side(_pass_side(outs), "gather_pass_%s%d" % self.flying[0], in_place=True)
            for k, arr in zip(self.flying, passed):
                self.whole[k] = _whole(k[0], arr)
            self.flying = []

    def weights(self, layer):
        w = {n: self.whole[n, layer] for n in BIG if (n, layer) in self.whole}
        if 'w_gate' in w:
            w['w_gu'] = jnp.concatenate([w['w_gate'], w['w_up']], axis=1)
        return w

    def sibling_side(self, layer, grads):
        self.sib_keys = [(n, layer) for n in grads]
        self.sib_full = [g if isinstance(g, tuple) else (_split(n, g), 0) for n, g in grads.items()]
        return _sibling_side([f for f, _ in self.sib_full], [off for _, off in self.sib_full])

    def sibling_done(self, got):
        for k, (f, off), r in zip(self.sib_keys, self.sib_full, got):
            self.part[k] = _pair_sum(f, off, r, "rs_pair_sum_%s%d" % k)
        self.pending += self.sib_keys

    def bwd_ready(self, layer, grads):
        side = self.sibling_side(layer, grads)
        self.sibling_done(_run_side(side, "rs_sibling_%s%d" % self.sib_keys[0]))

    def bwd_side(self):
        self.flying, self.pending = self.pending, []
        return _chip_side([self.part[k] for k in self.flying]) if self.flying else None

    def bwd_done(self, outs):
        for k, o in zip(self.flying, outs):
            self.others[k] = o
        self.flying = []

    def flush(self):
        side = self.bwd_side()
        if side is not None:
            self.bwd_done(_run_side(side, "rs_chips_last"))


def _local_step(x, p, target, big, small, hooks=None):
    nb, seq, _ = x.shape
    tokens = nb * seq
    h = x.reshape(tokens, D_MODEL)
    layers, saves = [], []
    for i in range(DEPTH):
        w = {name: small[name][i] for name in small if name != 'final_norm_g'}
        if hooks is not None:
            w.update(hooks.weights(i))
        else:
            for name in ('w_in', 'w_out', 'w_down', 'w_ple_gate', 'w_ple_proj'):
                w[name] = big[name][i]
            w['w_gu'] = jnp.concatenate([big['w_gate'][i], big['w_up'][i]], axis=1)
        c = _layer_consts(w)
        c['nb'], c['seq'] = nb, seq
        h, s = _layer_fwd(h, p[i].reshape(tokens, PLE_DIM).astype(BF16), w, c, hooks, i)
        layers.append((w, c))
        saves.append(s)

    def head(hf, tgt, gain):
        r = lax.rsqrt(jnp.mean(hf * hf, axis=-1, keepdims=True) + EPS)
        xhat = hf * r
        err = xhat * gain - tgt
        loss = 0.5 * jnp.sum(jnp.mean(err * err, axis=-1, keepdims=True), axis=0, keepdims=True)
        dy = err * (1.0 / D_MODEL)
        dg = jnp.sum(dy * xhat, axis=0, keepdims=True)
        dxhat = dy * gain
        dh = r * (dxhat - xhat * jnp.mean(dxhat * xhat, axis=-1, keepdims=True))
        return dh, jnp.broadcast_to(loss, (1, 128)), dg

    dh, loss128, dgf = _rowcall("loss_head", head, [h, target.reshape(tokens, D_MODEL)],
                                [_row(small['final_norm_g'])], [(D_MODEL, F32)],
                                [((1, 128), F32), ((1, D_MODEL), F32)])
    grads = {'final_norm_g': dgf.reshape(-1)}
    per_layer = [None] * DEPTH
    for i in range(DEPTH - 1, -1, -1):
        w, c = layers[i]
        dh, per_layer[i] = _layer_bwd(dh, saves[i], w, c, hooks, i)
    for name in per_layer[0]:
        if name in BIG:
            grads[name] = [per_layer[i][name] for i in range(DEPTH)]
        else:
            grads[name] = jnp.stack([per_layer[i][name] for i in range(DEPTH)])
    return loss128[0, 0], dh.reshape(nb, seq, D_MODEL), grads


def kernel(x, p, norm1_g, w_in, ssd_conv_w, ssd_conv_b, ssd_dt_bias, ssd_a_log, ssd_d, ssd_norm_g, lru_conv_w, lru_conv_b, lru_w_a, lru_b_a, lru_w_x, lru_b_x, lru_lambda, lru_norm_g, fox_b_f, fox_norm_g, w_out, norm2_g, w_gate, w_up, w_down, norm3_g, w_ple_gate, b_ple_gate, w_ple_proj, final_norm_g, loss_target, m_norm1_g, m_w_in, m_ssd_conv_w, m_ssd_conv_b, m_ssd_dt_bias, m_ssd_a_log, m_ssd_d, m_ssd_norm_g, m_lru_conv_w, m_lru_conv_b, m_lru_w_a, m_lru_b_a, m_lru_w_x, m_lru_b_x, m_lru_lambda, m_lru_norm_g, m_fox_b_f, m_fox_norm_g, m_w_out, m_norm2_g, m_w_gate, m_w_up, m_w_down, m_norm3_g, m_w_ple_gate, m_b_ple_gate, m_w_ple_proj, m_final_norm_g, v_norm1_g, v_w_in, v_ssd_conv_w, v_ssd_conv_b, v_ssd_dt_bias, v_ssd_a_log, v_ssd_d, v_ssd_norm_g, v_lru_conv_w, v_lru_conv_b, v_lru_w_a, v_lru_b_a, v_lru_w_x, v_lru_b_x, v_lru_lambda, v_lru_norm_g, v_fox_b_f, v_fox_norm_g, v_w_out, v_norm2_g, v_w_gate, v_w_up, v_w_down, v_norm3_g, v_w_ple_gate, v_b_ple_gate, v_w_ple_proj, v_final_norm_g):
    args = dict(locals())
    w_loc = {n: args[n] for n in WEIGHTS}
    m_loc = {n: args['m_' + n] for n in WEIGHTS}
    v_loc = {n: args['v_' + n] for n in WEIGHTS}
    dev = 4 * lax.axis_index("x") + 2 * lax.axis_index("y") + lax.axis_index("c")

    keys = [(n, i) for n in BIG for i in range(DEPTH)]
    conv_names = list(CONV_SHARDED)
    conv_loc_shapes = [w_loc[n].shape for n in conv_names]
    hooks = _Hooks({(n, i): _to_shard(n, w_loc[n][i]).astype(BF16) for n, i in keys})
    conv_all, = hooks.first([_pack_list([w_loc[n] for n in conv_names], 8)])
    small = {n: w_loc[n] for n in WEIGHTS if n not in BIG and n not in CONV_SHARDED}
    per_dev = [_unpack_list(conv_all[j], conv_loc_shapes) for j in range(N_DEV)]
    for idx, n in enumerate(conv_names):
        small[n] = jnp.concatenate([per_dev[j][idx] for j in range(N_DEV)], axis=2)

    loss_part, dx, grads = _local_step(x, p, loss_target, None, small, hooks)
    loss = lax.psum(loss_part, ("x", "y", "c"))
    hooks.flush()
    out = {kind: {n: [None] * DEPTH for n in BIG} for kind in ('g', 'delta', 'm', 'v')}
    for n, i in keys:
        res = _sum_adamw(hooks.part[n, i], hooks.others[n, i],
                         *[_to_shard(n, d[n][i]) for d in (w_loc, m_loc, v_loc)], "sum_adamw_%s%d" % (n, i))
        for kind, r in zip(('g', 'delta', 'm', 'v'), res):
            out[kind][n][i] = _from_shard(n, r)

    small_names = [n for n in WEIGHTS if n not in BIG]
    small_shapes = [grads[n].shape for n in small_names]
    g_small = dict(zip(small_names, _unpack_list(
        _all_reduce_small(_pack_list([grads[n] for n in small_names], 8)), small_shapes)))
    for n in CONV_SHARDED:
        width = CONV_SHARDED[n][2] // N_DEV
        g_small[n] = lax.dynamic_slice_in_dim(g_small[n], dev * width, width, axis=2)
    shapes = [w_loc[n].shape for n in small_names]
    packed = [_pack_list([d[n] for n in small_names], 8) for d in (w_loc, g_small, m_loc, v_loc)]
    upd = [dict(zip(small_names, _unpack_list(t, shapes))) for t in _adamw(*packed)]
    for kind, d in zip(('g', 'delta', 'm', 'v'), [g_small] + upd):
        for n in small_names:
            out[kind][n] = d[n]
        for n in BIG:
            out[kind][n] = jnp.stack(out[kind][n])
    return (loss, dx, *[out['g'][n] for n in WEIGHTS], *[out['delta'][n] for n in WEIGHTS],
            *[out['m'][n] for n in WEIGHTS], *[out['v'][n] for n in WEIGHTS])
```

```python
import functools
import math

import jax
import jax.numpy as jnp
from jax import lax
from jax.experimental import pallas as pl
from jax.experimental.pallas import tpu as pltpu

F32 = jnp.float32
BF16 = jnp.bfloat16

N_DEV = 8
D_MODEL = 1024
DEPTH = 2
HEAD_DIM = 64
SSD_WIDTH = 384
LRU_WIDTH = 256
FOX_WIDTH = 384
SSD_HEADS = 6
SSD_STATE = 128
CHUNK = 256
FOX_HEADS = 6
D_FF = 2816
FF_SHARD = D_FF // N_DEV
FF_SHARD_P = 384
D_FF_P = N_DEV * FF_SHARD_P
PLE_DIM = 256
IN_COLS = 2956
PROJ_COLS = 3072
LRU_C = 8.0
EPS = 1e-6
NEG = -1e30

ADAM_LR = 0.001
ADAM_B1 = 0.9
ADAM_B2 = 0.999
ADAM_EPS = 1e-08
ADAM_WD = 0.01
ADAM_STEP = 10

VMEM_LIMIT = 56 * 1024 * 1024

WEIGHTS = ['norm1_g', 'w_in', 'ssd_conv_w', 'ssd_conv_b', 'ssd_dt_bias', 'ssd_a_log', 'ssd_d', 'ssd_norm_g',
           'lru_conv_w', 'lru_conv_b', 'lru_w_a', 'lru_b_a', 'lru_w_x', 'lru_b_x', 'lru_lambda', 'lru_norm_g',
           'fox_b_f', 'fox_norm_g', 'w_out', 'norm2_g', 'w_gate', 'w_up', 'w_down', 'norm3_g', 'w_ple_gate',
           'b_ple_gate', 'w_ple_proj', 'final_norm_g']
BIG = {'w_in': (1, (DEPTH, D_MODEL, IN_COLS)), 'w_out': (1, (DEPTH, D_MODEL, D_MODEL)),
       'w_gate': (2, (DEPTH, D_MODEL, D_FF)), 'w_up': (2, (DEPTH, D_MODEL, D_FF)),
       'w_down': (1, (DEPTH, D_FF, D_MODEL)), 'w_ple_gate': (1, (DEPTH, D_MODEL, D_MODEL)),
       'w_ple_proj': (2, (DEPTH, PLE_DIM, D_MODEL))}
CONV_SHARDED = {'ssd_conv_w': (DEPTH, 4, 896), 'lru_conv_w': (DEPTH, 4, 256)}


def _dot(a, b):
    return jnp.dot(a, b, preferred_element_type=F32)


def _dot_nt(a, b):
    return lax.dot_general(a, b, (((1,), (1,)), ((), ())), preferred_element_type=F32)


def _dot_tn(a, b):
    return lax.dot_general(a, b, (((0,), (0,)), ((), ())), preferred_element_type=F32)


def _params(sem):
    return pltpu.CompilerParams(dimension_semantics=sem, vmem_limit_bytes=VMEM_LIMIT)


def _pick_tile(n, cap):
    if n <= cap:
        return n
    best = 128
    for t in range(128, cap + 1, 128):
        if n % t == 0:
            best = t
    assert n % best == 0, (n, cap)
    return best


def _matmul(a, b, mode, name, out_dtype=F32, side=None, shard_n=None):
    if mode == 'tn':
        k_dim, m_dim = a.shape
        n_dim = b.shape[1]
    else:
        m_dim, k_dim = a.shape
        n_dim = b.shape[1] if mode == 'nn' else b.shape[0]
    tm = _pick_tile(m_dim, 512 if mode != 'tn' else 1024)
    tn = 2 * shard_n if shard_n else _pick_tile(n_dim, 1536 if mode != 'tn' else 1024)
    tk = _pick_tile(k_dim, 3072 if mode != 'tn' else 2048)
    nk = k_dim // tk
    grid = (n_dim // tn, m_dim // tm, nk)

    n_in = len(side.inputs) if side else 0
    n_out = len(side.out_shape) if side else 0

    n_acc = 1 if nk > 1 else 0

    def body(*refs):
        a_ref, b_ref, o_ref = refs[0], refs[1], refs[2 + n_in]
        acc_ref = refs[3 + n_in + n_out] if n_acc else None
        if side is not None:
            side.run(refs[2:2 + n_in] + refs[3 + n_in:3 + n_in + n_out] + refs[3 + n_acc + n_in + n_out:],
                     *_grid_ends(grid))
        kk = pl.program_id(2)
        prod = {'nn': _dot, 'nt': _dot_nt, 'tn': _dot_tn}[mode](a_ref[...], b_ref[...])

        def write(res):
            if shard_n:
                for q in range(tn // shard_n):
                    o_ref[q] = res[:, q * shard_n:(q + 1) * shard_n].astype(o_ref.dtype)
            else:
                o_ref[...] = res.astype(o_ref.dtype)

        if nk == 1:
            write(prod)
            return

        @pl.when(kk == 0)
        def _():
            acc_ref[...] = prod

        @pl.when(jnp.logical_and(kk > 0, kk < nk - 1))
        def _():
            acc_ref[...] += prod

        @pl.when(kk == nk - 1)
        def _():
            write(acc_ref[...] + prod)

    if mode == 'nn':
        a_spec = pl.BlockSpec((tm, tk), lambda j, i, k: (i, k))
        b_spec = pl.BlockSpec((tk, tn), lambda j, i, k: (k, j))
    elif mode == 'nt':
        a_spec = pl.BlockSpec((tm, tk), lambda j, i, k: (i, k))
        b_spec = pl.BlockSpec((tn, tk), lambda j, i, k: (j, k))
    else:
        a_spec = pl.BlockSpec((tk, tm), lambda j, i, k: (k, i))
        b_spec = pl.BlockSpec((tk, tn), lambda j, i, k: (k, j))
    side_in, side_out = side.specs() if side else ([], [])
    if shard_n:
        out_shape = jax.ShapeDtypeStruct((n_dim // shard_n, m_dim, shard_n), out_dtype)
        out_spec = pl.BlockSpec((tn // shard_n, tm, shard_n), lambda j, i, k: (j, i, 0))
    else:
        out_shape = jax.ShapeDtypeStruct((m_dim, n_dim), out_dtype)
        out_spec = pl.BlockSpec((tm, tn), lambda j, i, k: (i, j))
    res = pl.pallas_call(
        body, name=name, grid=grid,
        out_shape=[out_shape] + (side.out_shape if side else []),
        in_specs=[a_spec, b_spec] + side_in,
        out_specs=[out_spec] + side_out,
        scratch_shapes=[pltpu.VMEM((tm, tn), F32)] * n_acc + (side.sems if side else []),
        compiler_params=_params(("arbitrary", "arbitrary", "arbitrary") if side
                                else ("parallel", "parallel", "arbitrary")),
    )(a, b, *(side.inputs if side else []))
    return res if side else res[0]


def _ffn_in(u, w_gu):
    tokens, k_dim = u.shape
    width = w_gu.shape[1] // 2
    tm, tn = _pick_tile(tokens, 512), _pick_tile(width, 768)
    nj = width // tn

    def body(a_ref, bg_ref, bu_ref, g_ref, u_ref, act_ref):
        a = a_ref[...]
        g = _dot(a, bg_ref[...]).astype(BF16)
        up = _dot(a, bu_ref[...]).astype(BF16)
        g_ref[...] = g
        u_ref[...] = up
        act_ref[...] = (_silu(g.astype(F32)) * up.astype(F32)).astype(BF16)

    out = jax.ShapeDtypeStruct((tokens, width), BF16)
    tile = pl.BlockSpec((tm, tn), lambda j, i: (i, j))
    return pl.pallas_call(
        body, name="ffn_in", grid=(nj, tokens // tm),
        out_shape=[out, out, out],
        in_specs=[pl.BlockSpec((tm, k_dim), lambda j, i: (i, 0)),
                  pl.BlockSpec((k_dim, tn), lambda j, i: (0, j)),
                  pl.BlockSpec((k_dim, tn), lambda j, i: (0, j + nj))],
        out_specs=[tile, tile, tile],
        compiler_params=_params(("parallel", "parallel")),
    )(u, w_gu, w_gu)


def _rowcall(name, fn, tiled, consts, outs, accs=(), tile=512, scratch=()):
    specs, arrays = [], []
    for t in tiled:
        if isinstance(t, tuple):
            arr, width, blk = t
            specs.append(pl.BlockSpec((tile, width), functools.partial(lambda i, blk: (i, blk), blk=blk)))
        else:
            arr = t
            specs.append(pl.BlockSpec((tile, arr.shape[1]), lambda i: (i, 0)))
        arrays.append(arr)
    rows = arrays[0].shape[0]
    assert rows % tile == 0, (name, rows, tile)
    for c in consts:
        specs.append(pl.BlockSpec(c.shape, lambda i: (0, 0)))
        arrays.append(c)
    n_in, n_out, n_acc = len(arrays), len(outs), len(accs)
    out_shape = [jax.ShapeDtypeStruct((rows, c), dt) for c, dt in outs]
    out_specs = [pl.BlockSpec((tile, c), lambda i: (i, 0)) for c, _ in outs]
    out_shape += [jax.ShapeDtypeStruct(s, dt) for s, dt in accs]
    out_specs += [pl.BlockSpec(s, lambda i: (0, 0)) for s, _ in accs]

    def body(*refs):
        ins = [r[...] for r in refs[:n_in]]
        out_refs = refs[n_in:n_in + n_out]
        acc_refs = refs[n_in + n_out:n_in + n_out + n_acc]
        scr = refs[n_in + n_out + n_acc:]
        res = fn(*ins, *scr)
        if not isinstance(res, (tuple, list)):
            res = (res,)
        assert len(res) == n_out + n_acc, (name, len(res))
        for r, v in zip(out_refs, res[:n_out]):
            r[...] = v.astype(r.dtype)
        if n_acc:
            first = pl.program_id(0) == 0

            @pl.when(first)
            def _():
                for r, v in zip(acc_refs, res[n_out:]):
                    r[...] = v.astype(r.dtype)

            @pl.when(jnp.logical_not(first))
            def _():
                for r, v in zip(acc_refs, res[n_out:]):
                    r[...] += v.astype(r.dtype)

    res = pl.pallas_call(
        body, name=name, grid=(rows // tile,),
        out_shape=out_shape, in_specs=specs, out_specs=out_specs,
        scratch_shapes=list(scratch),
        compiler_params=_params(("arbitrary",)),
    )(*arrays)
    return res


def _sigmoid(x):
    return 1.0 / (1.0 + jnp.exp(-x))


def _softplus(x):
    return jnp.maximum(x, 0.0) + jnp.log(1.0 + jnp.exp(-jnp.abs(x)))


def _silu(x):
    return x * _sigmoid(x)


def _dsilu(x):
    s = _sigmoid(x)
    return s * (1.0 + x * (1.0 - s))


_GELU_C = math.sqrt(2.0 / math.pi)


def _gelu(x):
    return 0.5 * x * (1.0 + jnp.tanh(_GELU_C * (x + 0.044715 * x * x * x)))


def _dgelu(x):
    t = jnp.tanh(_GELU_C * (x + 0.044715 * x * x * x))
    return 0.5 * (1.0 + t) + 0.5 * x * (1.0 - t * t) * _GELU_C * (1.0 + 3.0 * 0.044715 * x * x)


def _neg_expm1(x):
    series = -x * (1.0 + x * (0.5 + x * (1.0 / 6.0 + x * (1.0 / 24.0 + x * (1.0 / 120.0)))))
    return jnp.where(x > -0.03, series, 1.0 - jnp.exp(x))


def _rms(x, g):
    r = lax.rsqrt(jnp.mean(x * x, axis=-1, keepdims=True) + EPS)
    return x * r * g


def _rms_bwd(x, g, dy):
    r = lax.rsqrt(jnp.mean(x * x, axis=-1, keepdims=True) + EPS)
    xhat = x * r
    dg = jnp.sum(dy * xhat, axis=0, keepdims=True)
    dxhat = dy * g
    dx = r * (dxhat - xhat * jnp.mean(dxhat * xhat, axis=-1, keepdims=True))
    return dx, dg


def _row_iota(shape):
    return lax.broadcasted_iota(jnp.int32, shape, 0)


def _shift_down(x, j):
    if j == 0:
        return x
    return jnp.where(_row_iota(x.shape) >= j, pltpu.roll(x, j, 0), 0.0)


def _shift_up(x, j):
    if j == 0:
        return x
    n = x.shape[0]
    return jnp.where(_row_iota(x.shape) < n - j, pltpu.roll(x, n - j, 0), 0.0)


def _conv(x, w, b):
    y = b + w[3:4, :] * x
    for k in range(3):
        y = y + w[k:k + 1, :] * _shift_down(x, 3 - k)
    return y


def _conv_bwd(x, w, dy):
    dx = w[3:4, :] * dy
    dws = []
    for k in range(3):
        dx = dx + w[k:k + 1, :] * _shift_up(dy, 3 - k)
        dws.append(jnp.sum(dy * _shift_down(x, 3 - k), axis=0, keepdims=True))
    dws.append(jnp.sum(dy * x, axis=0, keepdims=True))
    return dx, jnp.concatenate(dws, axis=0), jnp.sum(dy, axis=0, keepdims=True)


def _split3(x):
    hi = x.astype(BF16)
    r1 = x - hi.astype(F32)
    mid = r1.astype(BF16)
    lo = (r1 - mid.astype(F32)).astype(BF16)
    return hi, mid, lo


def _tri_dot(tri, x):
    hi, mid, lo = _split3(x)
    return _dot(tri, hi) + _dot(tri, mid) + _dot(tri, lo)


def _cumsum_rows(x):
    n = x.shape[0] // CHUNK
    r = lax.broadcasted_iota(jnp.int32, (CHUNK, CHUNK), 0)
    c = lax.broadcasted_iota(jnp.int32, (CHUNK, CHUNK), 1)
    tri = (r >= c).astype(BF16)
    carry = jnp.zeros((1, x.shape[1]), F32)
    cums, prevs, ends = [], [], []
    for i in range(n):
        blk = _tri_dot(tri, x[i * CHUNK:(i + 1) * CHUNK]) + carry
        prevs.append(jnp.broadcast_to(carry, blk.shape))
        carry = blk[CHUNK - 1:CHUNK, :]
        ends.append(jnp.broadcast_to(carry, blk.shape))
        cums.append(blk)
    return jnp.concatenate(cums, 0), jnp.concatenate(prevs, 0), jnp.concatenate(ends, 0)


def _rev_cumsum_rows(x):
    n = x.shape[0] // CHUNK
    r = lax.broadcasted_iota(jnp.int32, (CHUNK, CHUNK), 0)
    c = lax.broadcasted_iota(jnp.int32, (CHUNK, CHUNK), 1)
    tri = (r <= c).astype(BF16)
    carry = jnp.zeros((1, x.shape[1]), F32)
    local, whole = [None] * n, [None] * n
    for i in range(n - 1, -1, -1):
        local[i] = _tri_dot(tri, x[i * CHUNK:(i + 1) * CHUNK])
        whole[i] = local[i] + carry
        carry = whole[i][0:1, :]
    return jnp.concatenate(local, 0), jnp.concatenate(whole, 0)


def _convsilu_fwd(name, seg, seq, w, b, dtype):
    return _rowcall(name, lambda raw, w, b: _silu(_conv(raw, w, b)), [seg], [w, b], [(seg[1], dtype)], tile=seq)[0]


def _convsilu_bwd(name, seg, seq, w, b, dy):
    def fn(raw, dy, w, b):
        return _conv_bwd(raw, w, dy * _dsilu(_conv(raw, w, b)))

    width = seg[1]
    return _rowcall(name, fn, [seg, dy], [w, b], [(width, BF16)], [((4, width), F32), ((1, width), F32)], tile=seq)


def _small_fwd(seg, seq, bias128, alog128):
    def fn(small, bias, alog):
        lane = lax.broadcasted_iota(jnp.int32, small.shape, 1)
        a = -jnp.exp(alog)
        s = small + bias
        dt = _softplus(s)
        logf = -_softplus(-s)
        pre = jnp.where(lane < SSD_HEADS, a * dt, jnp.where(lane < 2 * SSD_HEADS, logf, 0.0))
        cum, prev, end = _cumsum_rows(pre)
        return dt, cum, prev, end

    return _rowcall("small_fwd", fn, [seg], [bias128, alog128], [(128, F32)] * 4, tile=seq)


def _small_bwd(seg, seq, dcum, dcq, dend, ddt, dt_arr, bias128, alog128):
    def fn(small, dcum, dcq, dend, ddt, dt_arr, bias, alog):
        lane = lax.broadcasted_iota(jnp.int32, small.shape, 1)
        for pair in range(FOX_PAIRS):
            first = SSD_HEADS + 2 * pair
            moved = pltpu.roll(dcq[:, pair * 128:(pair + 1) * 128], first, 1)
            dcum = dcum + jnp.where(jnp.logical_or(lane == first, lane == first + 1), moved, 0.0)
        a = -jnp.exp(alog)
        sig = _sigmoid(small + bias)
        local, whole = _rev_cumsum_rows(dcum)
        dadt = local + dend
        d_dt = ddt + a * dadt
        ds = jnp.where(lane < SSD_HEADS, d_dt * sig, jnp.where(lane < 2 * SSD_HEADS, whole * (1.0 - sig), 0.0))
        da = jnp.sum(jnp.where(lane < SSD_HEADS, dadt * dt_arr, 0.0), axis=0, keepdims=True)
        return ds, jnp.sum(ds, axis=0, keepdims=True), da * a

    return _rowcall("small_bwd", fn, [seg, dcum, dcq, dend, ddt, dt_arr], [bias128, alog128], [(128, BF16)],
                    [((1, 128), F32), ((1, 128), F32)], tile=seq)


HEAD_PAIRS = SSD_HEADS // 2


def _ssd_specs(nc, reverse):
    def at(c):
        return nc - 1 - c if reverse else c

    x_spec = pl.BlockSpec((1, CHUNK, SSD_WIDTH), lambda b, c: (b, at(c), 0))
    bc_spec = pl.BlockSpec((1, CHUNK, 256), lambda b, c: (b, at(c), 0))
    col_spec = pl.BlockSpec((1, CHUNK, 128), lambda b, c: (b, at(c), 0))
    row_spec = pl.BlockSpec((1, 8, CHUNK), lambda b, c: (b, 0, at(c)))
    st_spec = pl.BlockSpec((1, 1, HEAD_PAIRS, SSD_STATE, 128), lambda b, c: (b, at(c), 0, 0, 0))
    return x_spec, bc_spec, col_spec, row_spec, st_spec


def _ssd_head(h, dtb, acb, apb, aeb, arb):
    return dtb[:, h:h + 1], acb[:, h:h + 1], apb[:, h:h + 1], aeb[:, h:h + 1], arb[h:h + 1, :]


def _ssd_fwd(x, bm, cm, dt_arr, cum, prev, end, a_row):
    nb, seq, _ = x.shape
    nc = seq // CHUNK
    x_spec, bc_spec, col_spec, row_spec, st_spec = _ssd_specs(nc, False)

    def body(x_ref, b_ref, c_ref, dt_ref, ac_ref, ap_ref, ae_ref, ar_ref, y_ref, st_ref, s_scr):
        @pl.when(pl.program_id(1) == 0)
        def _():
            s_scr[...] = jnp.zeros_like(s_scr)

        causal = (lax.broadcasted_iota(jnp.int32, (CHUNK, CHUNK), 0)
                  >= lax.broadcasted_iota(jnp.int32, (CHUNK, CHUNK), 1))
        low = lax.broadcasted_iota(jnp.int32, (CHUNK, 128), 1) < HEAD_DIM
        cols = (dt_ref[0], ac_ref[0], ap_ref[0], ae_ref[0], ar_ref[0])
        bcs = [b_ref[0, :, g * 128:(g + 1) * 128] for g in range(2)]
        ccs = [c_ref[0, :, g * 128:(g + 1) * 128] for g in range(2)]
        ms = [_dot_nt(ccs[g], bcs[g]) for g in range(2)]
        for pi in range(HEAD_PAIRS):
            x2 = x_ref[0, :, pi * 128:(pi + 1) * 128]
            dt2 = jnp.where(low, cols[0][:, 2 * pi:2 * pi + 1], cols[0][:, 2 * pi + 1:2 * pi + 2])
            xdt_f = x2 * dt2
            xdt = xdt_f.astype(BF16)
            sprev = s_scr[pi]
            st_ref[0, 0, pi] = sprev
            spb = sprev.astype(BF16)
            ys, us = [], []
            for h in (2 * pi, 2 * pi + 1):
                g = h // 3
                _, ac, ap, ae, ar = _ssd_head(h, *cols)
                lm = jnp.exp(jnp.where(causal, ac - ar, NEG))
                gm = (ms[g] * lm).astype(BF16)
                ys.append(_dot(gm, xdt) + jnp.exp(ac - ap) * _dot(ccs[g], spb))
                us.append(jnp.exp(ae[0:1, :] - ap[0:1, :]) * sprev
                          + _dot_tn(bcs[g], (xdt_f * jnp.exp(ae - ac)).astype(BF16)))
            y_ref[0, :, pi * 128:(pi + 1) * 128] = jnp.where(low, ys[0], ys[1])
            s_scr[pi] = jnp.where(lax.broadcasted_iota(jnp.int32, (SSD_STATE, 128), 1) < HEAD_DIM, us[0], us[1])

    return pl.pallas_call(
        body, name="ssd_fwd", grid=(nb, nc),
        out_shape=[jax.ShapeDtypeStruct(x.shape, F32),
                   jax.ShapeDtypeStruct((nb, nc, HEAD_PAIRS, SSD_STATE, 128), F32)],
        in_specs=[x_spec, bc_spec, bc_spec, col_spec, col_spec, col_spec, col_spec, row_spec],
        out_specs=[x_spec, st_spec],
        scratch_shapes=[pltpu.VMEM((HEAD_PAIRS, SSD_STATE, 128), F32)],
        compiler_params=_params(("parallel", "arbitrary")),
    )(x, bm, cm, dt_arr, cum, prev, end, a_row)


def _ssd_bwd(x_h, bm, cm, dt_arr, cum, prev, end, a_row, states, dy_h):
    nb, seq, _ = x_h.shape
    nc = seq // CHUNK
    x_spec, bc_spec, col_spec, row_spec, st_spec = _ssd_specs(nc, True)

    def body(x_ref, b_ref, c_ref, dt_ref, ac_ref, ap_ref, ae_ref, ar_ref, st_ref, dy_ref,
             dx_ref, db_ref, dc_ref, da_ref, dend_ref, ddt_ref, ds_scr):
        @pl.when(pl.program_id(1) == 0)
        def _():
            ds_scr[...] = jnp.zeros_like(ds_scr)

        causal = (lax.broadcasted_iota(jnp.int32, (CHUNK, CHUNK), 0)
                  >= lax.broadcasted_iota(jnp.int32, (CHUNK, CHUNK), 1))
        lane = lax.broadcasted_iota(jnp.int32, (CHUNK, 128), 1)
        low = lane < HEAD_DIM
        low_state = lax.broadcasted_iota(jnp.int32, (SSD_STATE, 128), 1) < HEAD_DIM
        cols = (dt_ref[0], ac_ref[0], ap_ref[0], ae_ref[0], ar_ref[0])
        bcs = [b_ref[0, :, g * 128:(g + 1) * 128] for g in range(2)]
        ccs = [c_ref[0, :, g * 128:(g + 1) * 128] for g in range(2)]
        ms = [_dot_nt(ccs[g], bcs[g]) for g in range(2)]
        dms = [jnp.zeros((CHUNK, CHUNK), F32) for _ in range(2)]
        dc_accs = [jnp.zeros((CHUNK, SSD_STATE), F32) for _ in range(2)]
        db_accs = [jnp.zeros((CHUNK, SSD_STATE), F32) for _ in range(2)]
        da_blk = jnp.zeros((CHUNK, 128), F32)
        dend_blk = jnp.zeros((CHUNK, 128), F32)
        ddt_blk = jnp.zeros((CHUNK, 128), F32)
        for pi in range(HEAD_PAIRS):
            x2 = x_ref[0, :, pi * 128:(pi + 1) * 128]
            dy2 = dy_ref[0, :, pi * 128:(pi + 1) * 128]
            dt2 = jnp.where(low, cols[0][:, 2 * pi:2 * pi + 1], cols[0][:, 2 * pi + 1:2 * pi + 2])
            xdt_f = x2 * dt2
            xdt = xdt_f.astype(BF16)
            dyb = dy2.astype(BF16)
            dsn = ds_scr[pi]
            dsb = dsn.astype(BF16)
            sprev_f = st_ref[0, 0, pi]
            sprev = sprev_f.astype(BF16)
            dxdts, dss = [], []
            for h in (2 * pi, 2 * pi + 1):
                g = h // 3
                mine = low if h % 2 == 0 else jnp.logical_not(low)
                _, ac, ap, ae, ar = _ssd_head(h, *cols)
                bc, cc, m = bcs[g], ccs[g], ms[g]
                lm = jnp.exp(jnp.where(causal, ac - ar, NEG))
                gm = (m * lm).astype(BF16)
                dy_m = jnp.where(mine, dy2, 0.0)
                dyb_m = dy_m.astype(BF16)
                xdt_m = jnp.where(mine, xdt_f, 0.0)
                e_in = jnp.exp(ac - ap)
                f_out = jnp.exp(ae - ac)
                whole = jnp.exp(ae[0:1, :] - ap[0:1, :])
                dg = _dot_nt(dyb_m, xdt)
                dxdt_off = f_out * _dot(bc, dsb)
                dxdt = _dot_tn(gm, dyb) + dxdt_off
                dmj = dg * lm
                dms[g] = dms[g] + dmj
                dc_accs[g] = dc_accs[g] + e_in * _dot_nt(dyb_m, sprev)
                db_accs[g] = db_accs[g] + f_out * _dot_nt(xdt_m.astype(BF16), dsb)
                dss.append(whole * dsn + _dot_tn(cc, (dy2 * e_in).astype(BF16)))
                wmat = dmj * m
                r_in = jnp.sum(dy_m * (e_in * _dot(cc, sprev)), axis=1, keepdims=True)
                q_out = jnp.sum(xdt_m * dxdt_off, axis=1, keepdims=True)
                daj = (jnp.sum(wmat, axis=1, keepdims=True) - jnp.sum(wmat.T, axis=1, keepdims=True)
                       + r_in - q_out)
                cross = jnp.where(low_state if h % 2 == 0 else jnp.logical_not(low_state), dsn * sprev_f, 0.0)
                dendj = (jnp.sum(q_out, axis=0, keepdims=True)
                         + whole * jnp.sum(jnp.sum(cross, axis=1, keepdims=True), axis=0, keepdims=True))
                ddtj = jnp.sum(jnp.where(mine, dxdt * x2, 0.0), axis=1, keepdims=True)
                dxdts.append(dxdt)
                da_blk = jnp.where(lane == h, daj, da_blk)
                dend_blk = jnp.where(lane == h, dendj, dend_blk)
                ddt_blk = jnp.where(lane == h, ddtj, ddt_blk)
            dx_ref[0, :, pi * 128:(pi + 1) * 128] = jnp.where(low, dxdts[0], dxdts[1]) * dt2
            ds_scr[pi] = jnp.where(low_state, dss[0], dss[1])
        for g in range(2):
            dmb = dms[g].astype(BF16)
            dc_ref[0, :, g * 128:(g + 1) * 128] = dc_accs[g] + _dot(dmb, bcs[g])
            db_ref[0, :, g * 128:(g + 1) * 128] = db_accs[g] + _dot_tn(dmb, ccs[g])
        da_ref[0] = da_blk
        dend_ref[0] = dend_blk
        ddt_ref[0] = ddt_blk

    col_shape = jax.ShapeDtypeStruct((nb, seq, 128), F32)
    return pl.pallas_call(
        body, name="ssd_bwd", grid=(nb, nc),
        out_shape=[jax.ShapeDtypeStruct(x_h.shape, F32),
                   jax.ShapeDtypeStruct((nb, seq, 256), F32), jax.ShapeDtypeStruct((nb, seq, 256), F32),
                   col_shape, col_shape, col_shape],
        in_specs=[x_spec, bc_spec, bc_spec, col_spec, col_spec, col_spec, col_spec, row_spec, st_spec, x_spec],
        out_specs=[x_spec, bc_spec, bc_spec, col_spec, col_spec, col_spec],
        scratch_shapes=[pltpu.VMEM((HEAD_PAIRS, SSD_STATE, 128), F32)],
        compiler_params=_params(("parallel", "arbitrary")),
    )(x_h, bm, cm, dt_arr, cum, prev, end, a_row, states, dy_h)


def _lru_gates(xl, wa, ba, wx, bx, lam):
    xb = xl.astype(BF16)
    r = _sigmoid(_dot(xb, wa) + ba)
    i = _sigmoid(_dot(xb, wx) + bx)
    sp = _softplus(-lam)
    log_a = -LRU_C * r * sp
    a = jnp.exp(log_a)
    mult = jnp.sqrt(_neg_expm1(2.0 * log_a))
    return r, i, sp, log_a, a, mult


def _scan_chunks(a_ref, u_ref, h_ref, seq, reverse):
    nc = seq // CHUNK
    width = a_ref.shape[1]
    row = lax.broadcasted_iota(jnp.int32, (CHUNK, width), 0)

    def chunk(ci, carry):
        c = nc - 1 - ci if reverse else ci
        rows = pl.ds(pl.multiple_of(c * CHUNK, CHUNK), CHUNK)
        av, bv = a_ref[rows, :], u_ref[rows, :]
        d = 1
        while d < CHUNK:
            if reverse:
                keep = row < CHUNK - d
                a_sh = jnp.where(keep, pltpu.roll(av, CHUNK - d, 0), 1.0)
                b_sh = jnp.where(keep, pltpu.roll(bv, CHUNK - d, 0), 0.0)
            else:
                keep = row >= d
                a_sh = jnp.where(keep, pltpu.roll(av, d, 0), 1.0)
                b_sh = jnp.where(keep, pltpu.roll(bv, d, 0), 0.0)
            bv = av * b_sh + bv
            av = av * a_sh
            d *= 2
        hv = bv + av * carry
        h_ref[rows, :] = hv
        return hv[0:1, :] if reverse else hv[CHUNK - 1:CHUNK, :]

    lax.fori_loop(0, nc, chunk, jnp.zeros((1, width), F32))


def _lru_fwd(proj, seq, cw, cb, wa, ba, wx, bx, lam):
    def fn(raw, cw, cb, wa, ba, wx, bx, lam, a_scr, u_scr, h_scr):
        xl = _conv(raw, cw, cb)
        r, i, sp, log_a, a, mult = _lru_gates(xl, wa, ba, wx, bx, lam)
        a_scr[...] = a
        u_scr[...] = mult * (i * xl)
        _scan_chunks(a_scr, u_scr, h_scr, seq, reverse=False)
        return h_scr[...], xl

    return _rowcall("lru_fwd", fn, [(proj, 256, 2)], [cw, cb, wa, ba, wx, bx, lam],
                    [(256, F32), (256, F32)], tile=seq,
                    scratch=[pltpu.VMEM((seq, 256), F32)] * 3)


def _lru_bwd(proj, seq, xl_all, h_all, dh_all, cw, cb, wa, ba, wx, bx, lam):
    def fn(raw, xl, hseq, dh, cw, cb, wa, ba, wx, bx, lam, a_scr, u_scr, h_scr):
        r, i, sp, log_a, a, mult = _lru_gates(xl, wa, ba, wx, bx, lam)
        a_scr[...] = _shift_up(a, 1)
        u_scr[...] = dh
        _scan_chunks(a_scr, u_scr, h_scr, seq, reverse=True)
        dht = h_scr[...]
        da = dht * _shift_down(hseq, 1)
        gated = i * xl
        dgated = dht * mult
        dmult = dht * gated
        dlog_a = da * a - dmult * (a * a) / mult
        dr = dlog_a * (-LRU_C * sp)
        dsp = jnp.sum(dlog_a * (-LRU_C * r), axis=0, keepdims=True)
        dlam = -dsp * _sigmoid(-lam)
        dpa = dr * r * (1.0 - r)
        dpx = (dgated * xl) * i * (1.0 - i)
        dpa_b, dpx_b = dpa.astype(BF16), dpx.astype(BF16)
        dxl = dgated * i + _dot_nt(dpa_b, wa) + _dot_nt(dpx_b, wx)
        xb = xl.astype(BF16)
        dwa = _dot_tn(xb, dpa_b)
        dwx = _dot_tn(xb, dpx_b)
        draw, dcw, dcb = _conv_bwd(raw, cw, dxl)
        return (draw, dcw, dcb, dwa, jnp.sum(dpa, axis=0, keepdims=True), dwx,
                jnp.sum(dpx, axis=0, keepdims=True), dlam)

    return _rowcall("lru_bwd", fn, [(proj, 256, 2), xl_all, h_all, dh_all], [cw, cb, wa, ba, wx, bx, lam],
                    [(256, BF16)],
                    [((4, 256), F32), ((1, 256), F32), ((256, 256), F32), ((1, 256), F32), ((256, 256), F32),
                     ((1, 256), F32), ((1, 256), F32)],
                    tile=seq, scratch=[pltpu.VMEM((seq, 256), F32)] * 3)


FOX_SCALE = HEAD_DIM ** -0.5
FOX_BLOCK = 512


class _Side:
    def __init__(self, inputs, out_shape, sems, build):
        self.inputs, self.out_shape, self.sems, self.build = list(inputs), list(out_shape), list(sems), build

    def specs(self):
        any_spec = pl.BlockSpec(memory_space=pl.ANY)
        return [any_spec] * len(self.inputs), [any_spec] * len(self.out_shape)

    def run(self, refs, first, last):
        n_in, n_out = len(self.inputs), len(self.out_shape)
        in_refs, out_refs, sem_refs = refs[:n_in], refs[n_in:n_in + n_out], refs[n_in + n_out:]

        @pl.when(first)
        def _():
            for cp in self.build(in_refs, out_refs, sem_refs):
                cp.start()

        @pl.when(last)
        def _():
            for cp in self.build(in_refs, out_refs, sem_refs):
                cp.wait()


def _grid_ends(grid):
    ids = [pl.program_id(a) for a in range(len(grid))]
    first = functools.reduce(jnp.logical_and, [i == 0 for i in ids])
    last = functools.reduce(jnp.logical_and, [i == n - 1 for i, n in zip(ids, grid)])
    return first, last


Q_BLK, K_BLK, V_BLK = 1920 // 128, 2304 // 128, 2688 // 128
FOX_PAIRS = FOX_HEADS // 2


def _fox_bias(cum, nb, seq):
    cf = cum.reshape(nb, seq, 128)[:, :, SSD_HEADS:SSD_HEADS + FOX_HEADS]
    cols = jnp.pad(cf.reshape(nb * seq, FOX_PAIRS, 2), ((0, 0), (0, 0), (0, 126))).reshape(nb * seq, 384)
    rows = jnp.pad(cf.transpose(0, 2, 1).reshape(nb, FOX_PAIRS, 2, seq), ((0, 0), (0, 0), (0, 6), (0, 0)))
    return cols, rows


def _fox_fwd(proj, bias_cols, bias_rows, nb, seq, side=None):
    tb = min(FOX_BLOCK, seq)
    nq = seq // tb
    grid = (nb, FOX_PAIRS, nq)
    n_in = len(side.inputs) if side else 0

    def body(*refs):
        q_ref, k_ref, v_ref, cq_ref, ck_ref = refs[:5]
        o_ref, lse_ref = refs[5 + n_in:7 + n_in]
        if side is not None:
            side.run(refs[5:5 + n_in] + refs[7 + n_in:], *_grid_ends(grid))
        qi = pl.program_id(2)
        low = lax.broadcasted_iota(jnp.int32, (tb, 128), 1) < HEAD_DIM
        q2 = q_ref[...] * FOX_SCALE
        qm = [jnp.where(low, q2, 0.0).astype(BF16), jnp.where(low, 0.0, q2).astype(BF16)]
        cqs = [cq_ref[:, 0:1], cq_ref[:, 1:2]]

        def block(j, carry, diagonal):
            cols = pl.ds(pl.multiple_of(j * tb, tb), tb)
            k2 = k_ref[cols, :].astype(BF16)
            v2 = v_ref[cols, :].astype(BF16)
            new = []
            for hh in range(2):
                m_i, l_i, acc = carry[hh]
                s = _dot_nt(qm[hh], k2) + cqs[hh] - ck_ref[0, 0, hh:hh + 1, cols]
                if diagonal:
                    s = jnp.where(lax.broadcasted_iota(jnp.int32, (tb, tb), 0)
                                  >= lax.broadcasted_iota(jnp.int32, (tb, tb), 1), s, NEG)
                m_new = jnp.maximum(m_i, jnp.max(s, axis=1, keepdims=True))
                p = jnp.exp(s - m_new)
                alpha = jnp.exp(m_i - m_new)
                new.append((m_new, alpha * l_i + jnp.sum(p, axis=1, keepdims=True),
                            alpha * acc + _dot(p.astype(BF16), v2)))
            return tuple(new)

        one = (jnp.full((tb, 1), NEG, F32), jnp.zeros((tb, 1), F32), jnp.zeros((tb, 128), F32))
        carry = lax.fori_loop(0, qi, lambda j, cr: block(j, cr, False), (one, one))
        (m0, l0, a0), (m1, l1, a1) = block(qi, carry, True)
        o_ref[...] = jnp.where(low, a0 / l0, a1 / l1)
        lse_ref[...] = jnp.where(low, m0 + jnp.log(l0), m1 + jnp.log(l1))

    def blk(first):
        return pl.BlockSpec((tb, 128), lambda b, p, i: (b * nq + i, first + p))

    def seq_blk(first):
        return pl.BlockSpec((seq, 128), lambda b, p, i: (b, first + p))

    row_spec = pl.BlockSpec((1, 1, 8, seq), lambda b, p, i: (b, p, 0, 0))
    side_in, side_out = side.specs() if side else ([], [])
    shape = jax.ShapeDtypeStruct((nb * seq, FOX_WIDTH), F32)
    return pl.pallas_call(
        body, name="fox_fwd", grid=grid,
        out_shape=[shape, shape] + (side.out_shape if side else []),
        in_specs=[blk(Q_BLK), seq_blk(K_BLK), seq_blk(V_BLK), blk(0), row_spec] + side_in,
        out_specs=[blk(0), blk(0)] + side_out,
        scratch_shapes=side.sems if side else [],
        compiler_params=_params(("arbitrary", "arbitrary", "arbitrary")),
    )(proj, proj, proj, bias_cols, bias_rows, *(side.inputs if side else []))


def _fox_bwd(proj, o, lse, do, bias_cols, bias_rows, nb, seq, side=None):
    tb = min(FOX_BLOCK, seq)
    nq = seq // tb
    grid = (nb, FOX_PAIRS, nq)
    n_in = len(side.inputs) if side else 0

    def body(*refs):
        q_ref, k_ref, v_ref, o_ref, lse_ref, do_ref, cq_ref, ck_ref = refs[:8]
        dq_ref, dk_ref, dv_ref, dcum_ref, dcq_ref = refs[8 + n_in:13 + n_in]
        if side is not None:
            side.run(refs[8:8 + n_in] + refs[13 + n_in:], *_grid_ends(grid))
        kj = pl.program_id(2)

        @pl.when(kj == 0)
        def _():
            dq_ref[...] = jnp.zeros_like(dq_ref)
            dcq_ref[...] = jnp.zeros_like(dcq_ref)

        lane = lax.broadcasted_iota(jnp.int32, (tb, 128), 1)
        low = lane < HEAD_DIM
        mine = [low, jnp.logical_not(low)]
        k2 = k_ref[...]
        kb = k2.astype(BF16)
        km = [jnp.where(mine[hh], k2, 0.0).astype(BF16) for hh in range(2)]
        vb = v_ref[...].astype(BF16)

        def block(i, carry, diagonal):
            dk, dv, c0, c1 = carry
            csum = [c0, c1]
            rows = pl.ds(pl.multiple_of(i * tb, tb), tb)
            q2 = q_ref[rows, :] * FOX_SCALE
            do2 = do_ref[rows, :]
            prod = do2 * o_ref[rows, :]
            dq_add = jnp.zeros((tb, 128), F32)
            rsum = []
            for hh in range(2):
                qm = jnp.where(mine[hh], q2, 0.0).astype(BF16)
                dom = jnp.where(mine[hh], do2, 0.0).astype(BF16)
                delta = jnp.sum(jnp.where(mine[hh], prod, 0.0), axis=1, keepdims=True)
                s = _dot_nt(qm, kb) + cq_ref[rows, hh:hh + 1] - ck_ref[0, 0, hh:hh + 1, :]
                if diagonal:
                    s = jnp.where(lax.broadcasted_iota(jnp.int32, (tb, tb), 0)
                                  >= lax.broadcasted_iota(jnp.int32, (tb, tb), 1), s, NEG)
                p = jnp.exp(s - lse_ref[rows, HEAD_DIM * hh:HEAD_DIM * hh + 1])
                ds = p * (_dot_nt(dom, vb) - delta)
                dsb = ds.astype(BF16)
                dv = dv + _dot_tn(p.astype(BF16), dom)
                dk = dk + _dot_tn(dsb, qm)
                dq_add = dq_add + _dot(dsb, km[hh])
                rsum.append(jnp.sum(ds, axis=1, keepdims=True))
                csum[hh] = csum[hh] + jnp.sum(ds, axis=0, keepdims=True)
            dq_ref[rows, :] += dq_add * FOX_SCALE
            dcq_ref[rows, :] += jnp.where(lane == 0, rsum[0], jnp.where(lane == 1, rsum[1], 0.0))
            return dk, dv, csum[0], csum[1]

        init = (jnp.zeros((tb, 128), F32), jnp.zeros((tb, 128), F32), jnp.zeros((1, tb), F32),
                jnp.zeros((1, tb), F32))
        carry = block(kj, init, True)
        dk, dv, c0, c1 = lax.fori_loop(kj + 1, nq, lambda i, cr: block(i, cr, False), carry)
        dk_ref[...] = dk.astype(dk_ref.dtype)
        dv_ref[...] = dv.astype(dv_ref.dtype)
        row = lax.broadcasted_iota(jnp.int32, (8, tb), 0)
        dcum_ref[0, 0] = jnp.where(row == 0, -c0, jnp.where(row == 1, -c1, 0.0))

    def blk(first):
        return pl.BlockSpec((tb, 128), lambda b, p, j: (b * nq + j, first + p))

    def seq_blk(first):
        return pl.BlockSpec((seq, 128), lambda b, p, j: (b, first + p))

    row_blk = pl.BlockSpec((1, 1, 8, tb), lambda b, p, j: (b, p, 0, j))
    side_in, side_out = side.specs() if side else ([], [])
    tokens = nb * seq
    return pl.pallas_call(
        body, name="fox_bwd", grid=grid,
        out_shape=[jax.ShapeDtypeStruct((tokens, FOX_WIDTH), F32), jax.ShapeDtypeStruct((tokens, FOX_WIDTH), BF16),
                   jax.ShapeDtypeStruct((tokens, FOX_WIDTH), BF16),
                   jax.ShapeDtypeStruct((nb, FOX_PAIRS, 8, seq), F32),
                   jax.ShapeDtypeStruct((tokens, FOX_WIDTH), F32)] + (side.out_shape if side else []),
        in_specs=[seq_blk(Q_BLK), blk(K_BLK), blk(V_BLK), seq_blk(0), seq_blk(0), seq_blk(0), seq_blk(0), row_blk]
        + side_in,
        out_specs=[seq_blk(0), blk(0), blk(0), row_blk, seq_blk(0)] + side_out,
        scratch_shapes=side.sems if side else [],
        compiler_params=_params(("arbitrary", "arbitrary", "arbitrary")),
    )(proj, proj, proj, o, lse, do, bias_cols, bias_rows, *(side.inputs if side else []))


_ANY = pl.BlockSpec(memory_space=pl.ANY)


def _place():
    return lax.axis_index("x"), lax.axis_index("y"), lax.axis_index("c")


def _all_gather(shards, name):
    n = len(shards)

    def body(*refs):
        x_refs, out_refs = refs[:n], refs[n:2 * n]
        send_sems, recv_sems, local_sems = refs[2 * n:]
        x, y, c = _place()
        me, sibling = (x, y, c), (x, y, 1 - c)
        chips = [(1 - x, y), (x, 1 - y), (1 - x, 1 - y)]

        def rows(a, px, py, pc):
            return out_refs[a].at[4 * px + 2 * py + pc]

        def copy(a, k, block, to, src=None):
            return pltpu.make_async_remote_copy(
                src_ref=rows(a, *block) if src is None else src, dst_ref=rows(a, *block),
                send_sem=send_sems.at[a, k], recv_sem=recv_sems.at[a, k],
                device_id=to, device_id_type=pl.DeviceIdType.MESH)

        mine = [pltpu.make_async_copy(x_refs[a], rows(a, *me), local_sems.at[a]) for a in range(n)]
        for cp in mine:
            cp.start()
        first = []
        for a in range(n):
            first.append(copy(a, 0, me, sibling, src=x_refs[a]))
            first += [copy(a, 1 + j, me, (*chip, c), src=x_refs[a]) for j, chip in enumerate(chips)]
        for cp in first:
            cp.start()
        passed = []
        for j, chip in enumerate(chips):
            for a in range(n):
                copy(a, 1 + j, (*chip, c), me).wait_recv()
                passed.append(copy(a, 4 + j, (*chip, c), sibling))
                passed[-1].start()
        for a in range(n):
            copy(a, 0, sibling, me).wait_recv()
            for j, chip in enumerate(chips):
                copy(a, 4 + j, (*chip, 1 - c), me).wait_recv()
        for cp in first + passed:
            cp.wait_send()
        for cp in mine:
            cp.wait()

    return pl.pallas_call(
        body, name=name,
        out_shape=[jax.ShapeDtypeStruct((N_DEV,) + s.shape, s.dtype) for s in shards],
        in_specs=[_ANY] * n, out_specs=[_ANY] * n,
        scratch_shapes=[pltpu.SemaphoreType.DMA((n, 7)), pltpu.SemaphoreType.DMA((n, 7)),
                        pltpu.SemaphoreType.DMA((n,))],
    )(*shards)


def _remote(src, dst, send_sem, recv_sem, to):
    return pltpu.make_async_remote_copy(src_ref=src, dst_ref=dst, send_sem=send_sem, recv_sem=recv_sem,
                                        device_id=to, device_id_type=pl.DeviceIdType.MESH)


def _sem_pairs(n, k):
    return [pltpu.SemaphoreType.DMA((n, k)), pltpu.SemaphoreType.DMA((n, k))]


def _sibling_side(full, offsets):
    def build(g_refs, out_refs, sems):
        x, y, c = _place()
        return [_remote(g_refs[a].at[offsets[a] + 4 * (k // 2) + 2 * (k % 2) + (1 - c)], out_refs[a].at[k],
                        sems[0].at[a, k], sems[1].at[a, k], (x, y, 1 - c))
                for a in range(len(g_refs)) for k in range(4)]

    return _Side(full, [jax.ShapeDtypeStruct((4,) + f.shape[1:], f.dtype) for f in full],
                 _sem_pairs(len(full), 4), build)


def _chip_side(part):
    def build(p_refs, out_refs, sems):
        x, y, c = _place()
        peers = [(1 - x, y), (x, 1 - y), (1 - x, 1 - y)]
        return [_remote(p_refs[a].at[2 * px + py], out_refs[a].at[k], sems[0].at[a, k], sems[1].at[a, k],
                        (px, py, c))
                for a in range(len(p_refs)) for k, (px, py) in enumerate(peers)]

    return _Side(part, [jax.ShapeDtypeStruct((3,) + p.shape[1:], p.dtype) for p in part],
                 _sem_pairs(len(part), 3), build)


def _spread_side(shards):
    def build(x_refs, out_refs, sems):
        x, y, c = _place()
        targets = [(x, y, 1 - c), (1 - x, y, c), (x, 1 - y, c), (1 - x, 1 - y, c)]
        cps = []
        for a in range(len(x_refs)):
            slot = out_refs[a].at[4 * x + 2 * y + c]
            cps.append(pltpu.make_async_copy(x_refs[a], slot, sems[2].at[a]))
            cps += [_remote(x_refs[a], slot, sems[0].at[a, k], sems[1].at[a, k], to)
                    for k, to in enumerate(targets)]
        return cps

    n = len(shards)
    return _Side(shards, [jax.ShapeDtypeStruct((N_DEV,) + s.shape, s.dtype) for s in shards],
                 _sem_pairs(n, 4) + [pltpu.SemaphoreType.DMA((n,))], build)


def _pass_side(bufs):
    def build(in_refs, out_refs, sems):
        x, y, c = _place()
        chips = [(1 - x, y), (x, 1 - y), (1 - x, 1 - y)]
        return [_remote(in_refs[a].at[4 * px + 2 * py + c], out_refs[a].at[4 * px + 2 * py + c],
                        sems[0].at[a, j], sems[1].at[a, j], (x, y, 1 - c))
                for a in range(len(in_refs)) for j, (px, py) in enumerate(chips)]

    return _Side(bufs, [jax.ShapeDtypeStruct(b.shape, b.dtype) for b in bufs], _sem_pairs(len(bufs), 3), build)


def _run_side(side, name, in_place=False):
    n_in = len(side.inputs)

    def body(*refs):
        copies = side.build(refs[:n_in], refs[n_in:n_in + len(side.out_shape)],
                            refs[n_in + len(side.out_shape):])
        for cp in copies:
            cp.start()
        for cp in copies:
            cp.wait()

    in_specs, out_specs = side.specs()
    return pl.pallas_call(
        body, name=name, out_shape=side.out_shape, in_specs=in_specs, out_specs=out_specs,
        scratch_shapes=side.sems,
        input_output_aliases={a: a for a in range(n_in)} if in_place else {},
    )(*side.inputs)


def _pick_rows(rows, cap=512):
    t = cap
    while t >= 8:
        if rows % t == 0:
            return t
        t //= 2
    raise ValueError(rows)


def _pair_sum(full, offset, got, name):
    _, rows, cols = full.shape
    tile = _pick_rows(rows, 256)
    c = lax.axis_index("c").astype(jnp.int32).reshape(1)

    def body(c_ref, a_ref, b_ref, o_ref):
        o_ref[...] = a_ref[...] + b_ref[...]

    blk = (1, tile, cols)
    return pl.pallas_call(
        body, name=name,
        grid_spec=pltpu.PrefetchScalarGridSpec(
            num_scalar_prefetch=1, grid=(4, rows // tile),
            in_specs=[pl.BlockSpec(blk, lambda k, i, c_ref: (offset + 4 * (k // 2) + 2 * (k % 2) + c_ref[0], i, 0)),
                      pl.BlockSpec(blk, lambda k, i, c_ref: (k, i, 0))],
            out_specs=pl.BlockSpec(blk, lambda k, i, c_ref: (k, i, 0))),
        out_shape=jax.ShapeDtypeStruct((4, rows, cols), full.dtype),
        compiler_params=_params(("arbitrary", "arbitrary")),
    )(c, full, got)


def _adam_math(w, g, m, v):
    c1 = 1.0 / (1.0 - ADAM_B1 ** ADAM_STEP)
    c2 = 1.0 / (1.0 - ADAM_B2 ** ADAM_STEP)
    m_new = ADAM_B1 * m + (1.0 - ADAM_B1) * g
    v_new = ADAM_B2 * v + (1.0 - ADAM_B2) * (g * g)
    delta = -ADAM_LR * ((m_new * c1) / (jnp.sqrt(v_new * c2) + ADAM_EPS) + ADAM_WD * w)
    return delta, m_new, v_new


def _sum_adamw(part, others, w, m, v, name):
    _, rows, cols = part.shape
    tile = _pick_rows(rows, 128)
    own = (2 * lax.axis_index("x") + lax.axis_index("y")).astype(jnp.int32).reshape(1)

    def body(own_ref, p_ref, o_ref, w_ref, m_ref, v_ref, g_out, d_out, m_out, v_out):
        g = ((p_ref[0] + o_ref[0]) + o_ref[1]) + o_ref[2]
        delta, m_new, v_new = _adam_math(w_ref[...], g, m_ref[...], v_ref[...])
        g_out[...] = g
        d_out[...] = delta
        m_out[...] = m_new
        v_out[...] = v_new

    flat = pl.BlockSpec((tile, cols), lambda i, own_ref: (i, 0))
    shape = jax.ShapeDtypeStruct((rows, cols), F32)
    return pl.pallas_call(
        body, name=name,
        grid_spec=pltpu.PrefetchScalarGridSpec(
            num_scalar_prefetch=1, grid=(rows // tile,),
            in_specs=[pl.BlockSpec((1, tile, cols), lambda i, own_ref: (own_ref[0], i, 0)),
                      pl.BlockSpec((3, tile, cols), lambda i, own_ref: (0, i, 0)), flat, flat, flat],
            out_specs=[flat] * 4),
        out_shape=[shape] * 4,
        compiler_params=_params(("arbitrary",)),
    )(own, part, others, w, m, v)


def _all_reduce_small(vec):
    gathered = _all_gather([vec], "ar_gather")[0]
    rows = vec.shape[0]

    def fn(*blocks):
        s = blocks[0]
        for b in blocks[1:]:
            s = s + b
        return s

    return _rowcall("ar_sum", fn, [gathered[j] for j in range(N_DEV)], [], [(1024, F32)],
                    tile=_pick_rows(rows))[0]


def _pad_rows(flat, mult):
    n = flat.shape[-1]
    per = mult * 1024
    padded = -(-n // per) * per
    pad = [(0, 0)] * (flat.ndim - 1) + [(0, padded - n)]
    return jnp.pad(flat, pad).reshape(flat.shape[:-1] + (padded // 1024, 1024))


def _regroup_w_in(w):
    pad = jnp.zeros((w.shape[0], 116), w.dtype)
    return jnp.concatenate([w[:, 768:1280], w[:, 1286:1798], w[:, 1280:1286], w[:, 2950:2956], pad,
                            w[:, 0:768], w[:, 1798:2950]], axis=1)


def _ungroup_w_in(wp):
    return jnp.concatenate([wp[:, 1152:1920], wp[:, 0:512], wp[:, 1024:1030], wp[:, 512:1024],
                            wp[:, 1920:3072], wp[:, 1030:1036]], axis=1)


def _to_shard(name, a):
    if name == 'w_in':
        return _regroup_w_in(a)
    if name in ('w_gate', 'w_up'):
        return jnp.pad(a, ((0, 0), (0, FF_SHARD_P - FF_SHARD)))
    if name == 'w_down':
        return jnp.pad(a, ((0, FF_SHARD_P - FF_SHARD), (0, 0)))
    return a


def _from_shard(name, a):
    if name == 'w_in':
        return _ungroup_w_in(a)
    if name in ('w_gate', 'w_up'):
        return a[:, 0:FF_SHARD]
    if name == 'w_down':
        return a[0:FF_SHARD, :]
    return a


def _whole(name, gathered):
    if BIG[name][0] == 1:
        return gathered.reshape(-1, gathered.shape[-1])
    return gathered.transpose(1, 0, 2).reshape(gathered.shape[1], -1)


def _split(name, whole):
    if BIG[name][0] == 1:
        return whole.reshape(N_DEV, whole.shape[0] // N_DEV, whole.shape[1])
    return whole.reshape(whole.shape[0], N_DEV, whole.shape[1] // N_DEV).transpose(1, 0, 2)


def _pack_list(arrays, mult):
    return _pad_rows(jnp.concatenate([a.reshape(-1) for a in arrays]), mult)


def _unpack_list(buf, shapes):
    flat = buf.reshape(-1)
    out, off = [], 0
    for s in shapes:
        n = math.prod(s)
        out.append(flat[off:off + n].reshape(s))
        off += n
    return out


def _adamw(w, g, m, v):
    return _rowcall("adamw", _adam_math, [w, g, m, v], [], [(1024, F32)] * 3, tile=_pick_rows(w.shape[0]))


def _block_diag(w):
    out = jnp.zeros((LRU_WIDTH, LRU_WIDTH), w.dtype)
    for g in range(4):
        out = lax.dynamic_update_slice(out, w[g], (64 * g, 64 * g))
    return out


def _block_diag_grad(full):
    return jnp.stack([full[64 * g:64 * (g + 1), 64 * g:64 * (g + 1)] for g in range(4)])


def _row(v):
    return v.reshape(1, -1).astype(F32)


def _lane128(*pieces):
    flat = jnp.concatenate([p.reshape(-1).astype(F32) for p in pieces])
    return jnp.pad(flat, (0, 128 - flat.shape[0])).reshape(1, 128)


def _layer_consts(w):
    c = {}
    cw, cb = w['ssd_conv_w'], w['ssd_conv_b']
    c['cw_x'], c['cw_b'], c['cw_c'] = cw[:, 0:384], cw[:, 384:640], cw[:, 640:896]
    c['cb_x'], c['cb_b'], c['cb_c'] = _row(cb[0:384]), _row(cb[384:640]), _row(cb[640:896])
    c['bias128'] = _lane128(w['ssd_dt_bias'], w['fox_b_f'])
    c['alog128'] = _lane128(w['ssd_a_log'])
    c['d384'] = _row(jnp.repeat(w['ssd_d'], HEAD_DIM))
    c['lcw'], c['lcb'] = w['lru_conv_w'], _row(w['lru_conv_b'])
    c['wa'], c['wx'] = _block_diag(w['lru_w_a']).astype(BF16), _block_diag(w['lru_w_x']).astype(BF16)
    c['ba'], c['bx'], c['lam'] = _row(w['lru_b_a']), _row(w['lru_b_x']), _row(w['lru_lambda'])
    return c


def _layer_fwd(h0, p_i, w, c, hooks=None, layer=0):
    nb, seq = c['nb'], c['seq']

    def carried(stage):
        return hooks.fwd_side(layer, stage) if hooks is not None else None

    def arrived(outs):
        if hooks is not None:
            hooks.fwd_done(outs)
            w.update(hooks.weights(layer))

    u1 = _rowcall("norm1", lambda h, g: _rms(h, g), [h0], [_row(w['norm1_g'])], [(D_MODEL, BF16)])[0]
    proj = _matmul(u1, w['w_in'], 'nn', "proj")

    xs_c = _convsilu_fwd("conv_x", (proj, 384, 4), seq, c['cw_x'], c['cb_x'], F32)
    b_c = _convsilu_fwd("conv_b", (proj, 256, 0), seq, c['cw_b'], c['cb_b'], BF16)
    c_c = _convsilu_fwd("conv_c", (proj, 256, 1), seq, c['cw_c'], c['cb_c'], BF16)
    dt_arr, cum, prev, end = _small_fwd((proj, 128, 8), seq, c['bias128'], c['alog128'])
    x_h = xs_c.reshape(nb, seq, SSD_WIDTH)
    cum3 = cum.reshape(nb, seq, 128)
    a_row = cum3[:, :, 0:8].transpose(0, 2, 1)
    ssd_in = (x_h, b_c.reshape(nb, seq, 256), c_c.reshape(nb, seq, 256), dt_arr.reshape(nb, seq, 128), cum3,
              prev.reshape(nb, seq, 128), end.reshape(nb, seq, 128), a_row)
    y_h, states = _ssd_fwd(*ssd_in)
    y_core = y_h.reshape(nb * seq, SSD_WIDTH)

    hseq, xl = _lru_fwd(proj, seq, c['lcw'], c['lcb'], c['wa'], c['ba'], c['wx'], c['bx'], c['lam'])

    bias_cols, bias_rows = _fox_bias(cum, nb, seq)
    y_fox, lse, *side_out = _fox_fwd(proj, bias_cols, bias_rows, nb, seq, carried('attention'))
    arrived(side_out)

    def post(yc, xs, z, hs, lg, yf, d, g1, g2, g3):
        y1 = _rms((yc + xs * d) * _silu(z), g1)
        y2 = _rms(hs * _gelu(lg), g2)
        y3 = _rms(yf, g3)
        return jnp.concatenate([y1, y2, y3], axis=-1)

    post_consts = [c['d384'], _row(w['ssd_norm_g']), _row(w['lru_norm_g']), _row(w['fox_norm_g'])]
    ycat = _rowcall("mix_post", post, [y_core, xs_c, (proj, 384, 3), hseq, (proj, 256, 3), y_fox], post_consts,
                    [(D_MODEL, BF16)])[0]
    mix = _matmul(ycat, w['w_out'], 'nn', "mix_out")

    def res_norm(h, d, g):
        hn = h + d
        return hn, _rms(hn, g)

    h1, u2 = _rowcall("res_norm2", res_norm, [h0, mix], [_row(w['norm2_g'])], [(D_MODEL, F32), (D_MODEL, BF16)])
    gate_pre, up_pre, act = _ffn_in(u2, w['w_gu'])
    side = carried('ffn_out')
    if side is None:
        ff = _matmul(act, w['w_down'], 'nn', "ffn_out")
    else:
        ff, *side_out = _matmul(act, w['w_down'], 'nn', "ffn_out", side=side)
        arrived(side_out)
    h2, u3 = _rowcall("res_norm3", res_norm, [h1, ff], [_row(w['norm3_g'])], [(D_MODEL, F32), (D_MODEL, BF16)])
    pg = _matmul(u3, w['w_ple_gate'], 'nn', "ple_gate")
    pp = _matmul(p_i, w['w_ple_proj'], 'nn', "ple_proj")
    h3 = _rowcall("ple", lambda h, a, b, bias: h + _sigmoid(a + bias) * b, [h2, pg, pp], [_row(w['b_ple_gate'])],
                  [(D_MODEL, F32)])[0]
    saved = dict(h0=h0, u1=u1, proj=proj, xs_c=xs_c, dt_arr=dt_arr, ssd_in=ssd_in, states=states,
                 y_core=y_core, hseq=hseq, xl=xl, bias_cols=bias_cols, bias_rows=bias_rows, lse=lse,
                 y_fox=y_fox, post_consts=post_consts, ycat=ycat, h1=h1, u2=u2, gate_pre=gate_pre, up_pre=up_pre, act=act, h2=h2, u3=u3,
                 pg=pg, pp=pp, p_i=p_i)
    return h3, saved


def _layer_bwd(dh3, s, w, c, hooks=None, layer=0):
    nb, seq = c['nb'], c['seq']
    g = {}

    def ple_bwd(dh, a, b, bias):
        gate = _sigmoid(a + bias)
        dpg = dh * b * gate * (1.0 - gate)
        return dh * gate, dpg, jnp.sum(dpg, axis=0, keepdims=True)

    dpp, dpg, g['b_ple_gate'] = _rowcall("ple_bwd", ple_bwd, [dh3, s['pg'], s['pp']], [_row(w['b_ple_gate'])],
                                         [(D_MODEL, BF16), (D_MODEL, BF16)], [((1, D_MODEL), F32)])
    g['w_ple_proj'] = _matmul(s['p_i'], dpp, 'tn', "d_w_ple_proj")
    g['w_ple_gate'] = _matmul(s['u3'], dpg, 'tn', "d_w_ple_gate")
    du3 = _matmul(dpg, w['w_ple_gate'], 'nt', "d_u3", BF16)

    def norm_bwd(h, du, dh, gain):
        dx, dg = _rms_bwd(h, gain, du.astype(F32))
        dhn = dh + dx
        return dhn, dhn, dg

    dh2, dh2_b, g['norm3_g'] = _rowcall("norm3_bwd", norm_bwd, [s['h2'], du3, dh3], [_row(w['norm3_g'])],
                                        [(D_MODEL, F32), (D_MODEL, BF16)], [((1, D_MODEL), F32)])
    g['w_down'] = _matmul(s['act'], dh2_b, 'tn', "d_w_down")
    dact = _matmul(dh2_b, w['w_down'], 'nt', "d_act", BF16)

    def swiglu_bwd(gt, up, da):
        gt, up, da = gt.astype(F32), up.astype(F32), da.astype(F32)
        return jnp.concatenate([da * up * _dsilu(gt), da * _silu(gt)], axis=-1)

    dgu = _rowcall("swiglu_bwd", swiglu_bwd, [s['gate_pre'], s['up_pre'], dact], [],
                   [(2 * D_FF_P, BF16)])[0]
    gu16 = _matmul(s['u2'], dgu, 'tn', "d_w_gu", shard_n=FF_SHARD_P)
    du2 = _matmul(dgu, w['w_gu'], 'nt', "d_u2", BF16)
    dh1, dh1_b, g['norm2_g'] = _rowcall("norm2_bwd", norm_bwd, [s['h1'], du2, dh2], [_row(w['norm2_g'])],
                                        [(D_MODEL, F32), (D_MODEL, BF16)], [((1, D_MODEL), F32)])
    g['w_out'] = _matmul(s['ycat'], dh1_b, 'tn', "d_w_out")
    if hooks is None:
        dycat = _matmul(dh1_b, w['w_out'], 'nt', "d_ycat", BF16)
    else:
        ready = {n: g[n] for n in ('w_out', 'w_down', 'w_ple_gate', 'w_ple_proj')}
        ready['w_gate'], ready['w_up'] = (gu16, 0), (gu16, N_DEV)
        dycat, *side_out = _matmul(dh1_b, w['w_out'], 'nt', "d_ycat", BF16, side=hooks.sibling_side(layer, ready))
        hooks.sibling_done(side_out)

    def post_bwd(dy, yc, xs, z, hs, lg, yf, d, g1, g2, g3):
        dy = dy.astype(F32)
        sz = _silu(z)
        ytot = yc + xs * d
        dpre1, dg1 = _rms_bwd(ytot * sz, g1, dy[:, 0:384])
        dytot = dpre1 * sz
        dz = dpre1 * ytot * _dsilu(z)
        dd = jnp.sum(dytot * xs, axis=0, keepdims=True)
        gl = _gelu(lg)
        dpre2, dg2 = _rms_bwd(hs * gl, g2, dy[:, 384:640])
        dyf, dg3 = _rms_bwd(yf, g3, dy[:, 640:1024])
        return dytot, dytot * d, dz, dpre2 * gl, dpre2 * hs * _dgelu(lg), dyf, dd, dg1, dg2, dg3

    (dy_core, dxs_skip, dz, dhseq, dlg, dy_fox, dd384, g['ssd_norm_g'], g['lru_norm_g'], g['fox_norm_g']) = _rowcall(
        "mix_post_bwd", post_bwd,
        [dycat, s['y_core'], s['xs_c'], (s['proj'], 384, 3), s['hseq'], (s['proj'], 256, 3), s['y_fox']],
        s['post_consts'],
        [(384, F32), (384, F32), (384, BF16), (256, F32), (256, BF16), (384, F32)],
        [((1, 384), F32), ((1, 384), F32), ((1, 256), F32), ((1, 384), F32)])
    g['ssd_d'] = dd384.reshape(SSD_HEADS, HEAD_DIM).sum(axis=1)

    side = hooks.bwd_side() if hooks is not None else None
    dq, dk, dv, dcf_rows, dcf_cols, *side_out = _fox_bwd(s['proj'], s['y_fox'], s['lse'], dy_fox, s['bias_cols'],
                                                         s['bias_rows'], nb, seq, side)
    if hooks is not None:
        hooks.bwd_done(side_out)
    dq = dq.astype(BF16)

    dx_h, db_c, dc_c, da_arr, dend_arr, ddt_arr = _ssd_bwd(*s['ssd_in'], s['states'],
                                                           dy_core.reshape(nb, seq, SSD_WIDTH))
    dxs_c = dx_h.reshape(nb * seq, SSD_WIDTH) + dxs_skip
    dcf = dcf_rows[:, :, 0:2, :].reshape(nb, FOX_HEADS, seq).transpose(0, 2, 1)
    dcum = jnp.concatenate([da_arr[:, :, 0:SSD_HEADS], dcf,
                            jnp.zeros((nb, seq, 128 - 2 * SSD_HEADS), F32)], axis=-1).reshape(nb * seq, 128)
    proj = s['proj']
    dxs_raw, dcw_x, dcb_x = _convsilu_bwd("conv_x_bwd", (proj, 384, 4), seq, c['cw_x'], c['cb_x'], dxs_c)
    db_raw, dcw_b, dcb_b = _convsilu_bwd("conv_b_bwd", (proj, 256, 0), seq, c['cw_b'], c['cb_b'],
                                         db_c.reshape(nb * seq, 256))
    dc_raw, dcw_c, dcb_c = _convsilu_bwd("conv_c_bwd", (proj, 256, 1), seq, c['cw_c'], c['cb_c'],
                                         dc_c.reshape(nb * seq, 256))
    dsmall, dbias128, dalog128 = _small_bwd((proj, 128, 8), seq, dcum, dcf_cols, dend_arr.reshape(nb * seq, 128),
                                            ddt_arr.reshape(nb * seq, 128), s['dt_arr'], c['bias128'], c['alog128'])
    g['ssd_conv_w'] = jnp.concatenate([dcw_x, dcw_b, dcw_c], axis=1)
    g['ssd_conv_b'] = jnp.concatenate([dcb_x, dcb_b, dcb_c], axis=1).reshape(-1)
    g['ssd_dt_bias'] = dbias128[0, 0:SSD_HEADS]
    g['fox_b_f'] = dbias128[0, SSD_HEADS:2 * SSD_HEADS]
    g['ssd_a_log'] = dalog128[0, 0:SSD_HEADS]

    (dlru_raw, g['lru_conv_w'], dlcb, dwa, dba, dwx, dbx, dlam) = _lru_bwd(
        s['proj'], seq, s['xl'], s['hseq'], dhseq, c['lcw'], c['lcb'], c['wa'], c['ba'], c['wx'], c['bx'], c['lam'])
    g['lru_conv_b'], g['lru_b_a'], g['lru_b_x'], g['lru_lambda'] = (t.reshape(-1) for t in (dlcb, dba, dbx, dlam))
    g['lru_w_a'], g['lru_w_x'] = _block_diag_grad(dwa), _block_diag_grad(dwx)

    dproj = jnp.concatenate([db_raw, dc_raw, dlru_raw, dlg, dsmall, dz, dxs_raw, dq, dk, dv], axis=1)
    g['w_in'] = _matmul(s['u1'], dproj, 'tn', "d_w_in")
    if hooks is None:
        du1 = _matmul(dproj, w['w_in'], 'nt', "d_u1", BF16)
    else:
        hooks.bwd_ready(layer, {'w_in': g['w_in']})
        du1, *side_out = _matmul(dproj, w['w_in'], 'nt', "d_u1", BF16, side=hooks.bwd_side())
        hooks.bwd_done(side_out)

    def norm1_bwd(h, du, dh, gain):
        dx, dg = _rms_bwd(h, gain, du.astype(F32))
        return dh + dx, dg

    dh0, g['norm1_g'] = _rowcall("norm1_bwd", norm1_bwd, [s['h0'], du1, dh1], [_row(w['norm1_g'])],
                                 [(D_MODEL, F32)], [((1, D_MODEL), F32)])
    for name in ('b_ple_gate', 'norm3_g', 'norm2_g', 'norm1_g', 'ssd_norm_g', 'lru_norm_g', 'fox_norm_g'):
        g[name] = g[name].reshape(-1)
    g['w_gate'], g['w_up'] = None, None
    if hooks is None:
        g['w_gate'] = gu16[0:N_DEV].transpose(1, 0, 2).reshape(D_MODEL, D_FF_P)
        g['w_up'] = gu16[N_DEV:2 * N_DEV].transpose(1, 0, 2).reshape(D_MODEL, D_FF_P)
    return dh0, g


class _Hooks:
    def __init__(self, shard):
        self.shard = shard
        self.whole = {}
        self.part, self.others = {}, {}
        self.pending, self.flying = [], []

    def first(self, extra):
        got = _all_gather([self.shard['w_in', 0]] + extra, "gather_first")
        self.whole['w_in', 0] = _whole('w_in', got[0])
        return got[1:]

    def fwd_side(self, layer, stage):
        if stage == 'attention':
            self.flying = [(n, layer) for n in BIG if n != 'w_in']
        elif layer + 1 < DEPTH:
            self.flying = [('w_in', layer + 1)]
        else:
            return None
        return _spread_side([self.shard[k] for k in self.flying])

    def fwd_done(self, outs):
        if self.flying:
            passed = _run_side(_pass_side(outs), "gather_pass_%s%d" % self.flying[0], in_place=True)
            for k, arr in zip(self.flying, passed):
                self.whole[k] = _whole(k[0], arr)
            self.flying = []

    def weights(self, layer):
        w = {n: self.whole[n, layer] for n in BIG if (n, layer) in self.whole}
        if 'w_gate' in w:
            w['w_gu'] = jnp.concatenate([w['w_gate'], w['w_up']], axis=1)
        return w

    def sibling_side(self, layer, grads):
        self.sib_keys = [(n, layer) for n in grads]
        self.sib_full = [g if isinstance(g, tuple) else (_split(n, g), 0) for n, g in grads.items()]
        return _sibling_side([f for f, _ in self.sib_full], [off for _, off in self.sib_full])

    def sibling_done(self, got):
        for k, (f, off), r in zip(self.sib_keys, self.sib_full, got):
            self.part[k] = _pair_sum(f, off, r, "rs_pair_sum_%s%d" % k)
        self.pending += self.sib_keys

    def bwd_ready(self, layer, grads):
        side = self.sibling_side(layer, grads)
        self.sibling_done(_run_side(side, "rs_sibling_%s%d" % self.sib_keys[0]))

    def bwd_side(self):
        self.flying, self.pending = self.pending, []
        return _chip_side([self.part[k] for k in self.flying]) if self.flying else None

    def bwd_done(self, outs):
        for k, o in zip(self.flying, outs):
            self.others[k] = o
        self.flying = []

    def flush(self):
        side = self.bwd_side()
        if side is not None:
            self.bwd_done(_run_side(side, "rs_chips_last"))


def _local_step(x, p, target, big, small, hooks=None):
    nb, seq, _ = x.shape
    tokens = nb * seq
    h = x.reshape(tokens, D_MODEL)
    layers, saves = [], []
    for i in range(DEPTH):
        w = {name: small[name][i] for name in small if name != 'final_norm_g'}
        if hooks is not None:
            w.update(hooks.weights(i))
        else:
            for name in ('w_in', 'w_out', 'w_down', 'w_ple_gate', 'w_ple_proj'):
                w[name] = big[name][i]
            w['w_gu'] = jnp.concatenate([big['w_gate'][i], big['w_up'][i]], axis=1)
        c = _layer_consts(w)
        c['nb'], c['seq'] = nb, seq
        h, s = _layer_fwd(h, p[i].reshape(tokens, PLE_DIM).astype(BF16), w, c, hooks, i)
        layers.append((w, c))
        saves.append(s)

    def head(hf, tgt, gain):
        r = lax.rsqrt(jnp.mean(hf * hf, axis=-1, keepdims=True) + EPS)
        xhat = hf * r
        err = xhat * gain - tgt
        loss = 0.5 * jnp.sum(jnp.mean(err * err, axis=-1, keepdims=True), axis=0, keepdims=True)
        dy = err * (1.0 / D_MODEL)
        dg = jnp.sum(dy * xhat, axis=0, keepdims=True)
        dxhat = dy * gain
        dh = r * (dxhat - xhat * jnp.mean(dxhat * xhat, axis=-1, keepdims=True))
        return dh, jnp.broadcast_to(loss, (1, 128)), dg

    dh, loss128, dgf = _rowcall("loss_head", head, [h, target.reshape(tokens, D_MODEL)],
                                [_row(small['final_norm_g'])], [(D_MODEL, F32)],
                                [((1, 128), F32), ((1, D_MODEL), F32)])
    grads = {'final_norm_g': dgf.reshape(-1)}
    per_layer = [None] * DEPTH
    for i in range(DEPTH - 1, -1, -1):
        w, c = layers[i]
        dh, per_layer[i] = _layer_bwd(dh, saves[i], w, c, hooks, i)
    for name in per_layer[0]:
        if name in BIG:
            grads[name] = [per_layer[i][name] for i in range(DEPTH)]
        else:
            grads[name] = jnp.stack([per_layer[i][name] for i in range(DEPTH)])
    return loss128[0, 0], dh.reshape(nb, seq, D_MODEL), grads


def kernel(x, p, norm1_g, w_in, ssd_conv_w, ssd_conv_b, ssd_dt_bias, ssd_a_log, ssd_d, ssd_norm_g, lru_conv_w, lru_conv_b, lru_w_a, lru_b_a, lru_w_x, lru_b_x, lru_lambda, lru_norm_g, fox_b_f, fox_norm_g, w_out, norm2_g, w_gate, w_up, w_down, norm3_g, w_ple_gate, b_ple_gate, w_ple_proj, final_norm_g, loss_target, m_norm1_g, m_w_in, m_ssd_conv_w, m_ssd_conv_b, m_ssd_dt_bias, m_ssd_a_log, m_ssd_d, m_ssd_norm_g, m_lru_conv_w, m_lru_conv_b, m_lru_w_a, m_lru_b_a, m_lru_w_x, m_lru_b_x, m_lru_lambda, m_lru_norm_g, m_fox_b_f, m_fox_norm_g, m_w_out, m_norm2_g, m_w_gate, m_w_up, m_w_down, m_norm3_g, m_w_ple_gate, m_b_ple_gate, m_w_ple_proj, m_final_norm_g, v_norm1_g, v_w_in, v_ssd_conv_w, v_ssd_conv_b, v_ssd_dt_bias, v_ssd_a_log, v_ssd_d, v_ssd_norm_g, v_lru_conv_w, v_lru_conv_b, v_lru_w_a, v_lru_b_a, v_lru_w_x, v_lru_b_x, v_lru_lambda, v_lru_norm_g, v_fox_b_f, v_fox_norm_g, v_w_out, v_norm2_g, v_w_gate, v_w_up, v_w_down, v_norm3_g, v_w_ple_gate, v_b_ple_gate, v_w_ple_proj, v_final_norm_g):
    args = dict(locals())
    w_loc = {n: args[n] for n in WEIGHTS}
    m_loc = {n: args['m_' + n] for n in WEIGHTS}
    v_loc = {n: args['v_' + n] for n in WEIGHTS}
    dev = 4 * lax.axis_index("x") + 2 * lax.axis_index("y") + lax.axis_index("c")

    keys = [(n, i) for n in BIG for i in range(DEPTH)]
    conv_names = list(CONV_SHARDED)
    conv_loc_shapes = [w_loc[n].shape for n in conv_names]
    hooks = _Hooks({(n, i): _to_shard(n, w_loc[n][i]).astype(BF16) for n, i in keys})
    conv_all, = hooks.first([_pack_list([w_loc[n] for n in conv_names], 8)])
    small = {n: w_loc[n] for n in WEIGHTS if n not in BIG and n not in CONV_SHARDED}
    per_dev = [_unpack_list(conv_all[j], conv_loc_shapes) for j in range(N_DEV)]
    for idx, n in enumerate(conv_names):
        small[n] = jnp.concatenate([per_dev[j][idx] for j in range(N_DEV)], axis=2)

    loss_part, dx, grads = _local_step(x, p, loss_target, None, small, hooks)
    loss = lax.psum(loss_part, ("x", "y", "c"))
    hooks.flush()
    out = {kind: {n: [None] * DEPTH for n in BIG} for kind in ('g', 'delta', 'm', 'v')}
    for n, i in keys:
        res = _sum_adamw(hooks.part[n, i], hooks.others[n, i],
                         *[_to_shard(n, d[n][i]) for d in (w_loc, m_loc, v_loc)], "sum_adamw_%s%d" % (n, i))
        for kind, r in zip(('g', 'delta', 'm', 'v'), res):
            out[kind][n][i] = _from_shard(n, r)

    small_names = [n for n in WEIGHTS if n not in BIG]
    small_shapes = [grads[n].shape for n in small_names]
    g_small = dict(zip(small_names, _unpack_list(
        _all_reduce_small(_pack_list([grads[n] for n in small_names], 8)), small_shapes)))
    for n in CONV_SHARDED:
        width = CONV_SHARDED[n][2] // N_DEV
        g_small[n] = lax.dynamic_slice_in_dim(g_small[n], dev * width, width, axis=2)
    shapes = [w_loc[n].shape for n in small_names]
    packed = [_pack_list([d[n] for n in small_names], 8) for d in (w_loc, g_small, m_loc, v_loc)]
    upd = [dict(zip(small_names, _unpack_list(t, shapes))) for t in _adamw(*packed)]
    for kind, d in zip(('g', 'delta', 'm', 'v'), [g_small] + upd):
        for n in small_names:
            out[kind][n] = d[n]
        for n in BIG:
            out[kind][n] = jnp.stack(out[kind][n])
    return (loss, dx, *[out['g'][n] for n in WEIGHTS], *[out['delta'][n] for n in WEIGHTS],
            *[out['m'][n] for n in WEIGHTS], *[out['v'][n] for n in WEIGHTS])
```

```python
import functools
import math

import jax
import jax.numpy as jnp
from jax import lax
from jax.experimental import pallas as pl
from jax.experimental.pallas import tpu as pltpu

F32 = jnp.float32
BF16 = jnp.bfloat16

N_DEV = 8
D_MODEL = 1024
DEPTH = 2
HEAD_DIM = 64
SSD_WIDTH = 384
LRU_WIDTH = 256
FOX_WIDTH = 384
SSD_HEADS = 6
SSD_STATE = 128
CHUNK = 256
FOX_HEADS = 6
D_FF = 2816
FF_SHARD = D_FF // N_DEV
FF_SHARD_P = 384
D_FF_P = N_DEV * FF_SHARD_P
PLE_DIM = 256
IN_COLS = 2956
PROJ_COLS = 3072
LRU_C = 8.0
EPS = 1e-6
NEG = -1e30

ADAM_LR = 0.001
ADAM_B1 = 0.9
ADAM_B2 = 0.999
ADAM_EPS = 1e-08
ADAM_WD = 0.01
ADAM_STEP = 10

VMEM_LIMIT = 56 * 1024 * 1024

WEIGHTS = ['norm1_g', 'w_in', 'ssd_conv_w', 'ssd_conv_b', 'ssd_dt_bias', 'ssd_a_log', 'ssd_d', 'ssd_norm_g',
           'lru_conv_w', 'lru_conv_b', 'lru_w_a', 'lru_b_a', 'lru_w_x', 'lru_b_x', 'lru_lambda', 'lru_norm_g',
           'fox_b_f', 'fox_norm_g', 'w_out', 'norm2_g', 'w_gate', 'w_up', 'w_down', 'norm3_g', 'w_ple_gate',
           'b_ple_gate', 'w_ple_proj', 'final_norm_g']
BIG = {'w_in': (1, (DEPTH, D_MODEL, IN_COLS)), 'w_out': (1, (DEPTH, D_MODEL, D_MODEL)),
       'w_gate': (2, (DEPTH, D_MODEL, D_FF)), 'w_up': (2, (DEPTH, D_MODEL, D_FF)),
       'w_down': (1, (DEPTH, D_FF, D_MODEL)), 'w_ple_gate': (1, (DEPTH, D_MODEL, D_MODEL)),
       'w_ple_proj': (2, (DEPTH, PLE_DIM, D_MODEL))}
CONV_SHARDED = {'ssd_conv_w': (DEPTH, 4, 896), 'lru_conv_w': (DEPTH, 4, 256)}


def _dot(a, b):
    return jnp.dot(a, b, preferred_element_type=F32)


def _dot_nt(a, b):
    return lax.dot_general(a, b, (((1,), (1,)), ((), ())), preferred_element_type=F32)


def _dot_tn(a, b):
    return lax.dot_general(a, b, (((0,), (0,)), ((), ())), preferred_element_type=F32)


def _params(sem):
    return pltpu.CompilerParams(dimension_semantics=sem, vmem_limit_bytes=VMEM_LIMIT)


def _pick_tile(n, cap):
    if n <= cap:
        return n
    best = 128
    for t in range(128, cap + 1, 128):
        if n % t == 0:
            best = t
    assert n % best == 0, (n, cap)
    return best


def _matmul(a, b, mode, name, out_dtype=F32, side=None, shard_n=None):
    if mode == 'tn':
        k_dim, m_dim = a.shape
        n_dim = b.shape[1]
    else:
        m_dim, k_dim = a.shape
        n_dim = b.shape[1] if mode == 'nn' else b.shape[0]
    tm = _pick_tile(m_dim, 512 if mode != 'tn' else 1024)
    tn = 2 * shard_n if shard_n else _pick_tile(n_dim, 1536 if mode != 'tn' else 1024)
    tk = _pick_tile(k_dim, 3072 if mode != 'tn' else 2048)
    nk = k_dim // tk
    grid = (n_dim // tn, m_dim // tm, nk)

    n_in = len(side.inputs) if side else 0
    n_out = len(side.out_shape) if side else 0

    n_acc = 1 if nk > 1 else 0

    def body(*refs):
        a_ref, b_ref, o_ref = refs[0], refs[1], refs[2 + n_in]
        acc_ref = refs[3 + n_in + n_out] if n_acc else None
        if side is not None:
            side.run(refs[2:2 + n_in] + refs[3 + n_in:3 + n_in + n_out] + refs[3 + n_acc + n_in + n_out:],
                     *_grid_ends(grid))
        kk = pl.program_id(2)
        prod = {'nn': _dot, 'nt': _dot_nt, 'tn': _dot_tn}[mode](a_ref[...], b_ref[...])

        def write(res):
            if shard_n:
                for q in range(tn // shard_n):
                    o_ref[q] = res[:, q * shard_n:(q + 1) * shard_n].astype(o_ref.dtype)
            else:
                o_ref[...] = res.astype(o_ref.dtype)

        if nk == 1:
            write(prod)
            return

        @pl.when(kk == 0)
        def _():
            acc_ref[...] = prod

        @pl.when(jnp.logical_and(kk > 0, kk < nk - 1))
        def _():
            acc_ref[...] += prod

        @pl.when(kk == nk - 1)
        def _():
            write(acc_ref[...] + prod)

    if mode == 'nn':
        a_spec = pl.BlockSpec((tm, tk), lambda j, i, k: (i, k))
        b_spec = pl.BlockSpec((tk, tn), lambda j, i, k: (k, j))
    elif mode == 'nt':
        a_spec = pl.BlockSpec((tm, tk), lambda j, i, k: (i, k))
        b_spec = pl.BlockSpec((tn, tk), lambda j, i, k: (j, k))
    else:
        a_spec = pl.BlockSpec((tk, tm), lambda j, i, k: (k, i))
        b_spec = pl.BlockSpec((tk, tn), lambda j, i, k: (k, j))
    side_in, side_out = side.specs() if side else ([], [])
    if shard_n:
        out_shape = jax.ShapeDtypeStruct((n_dim // shard_n, m_dim, shard_n), out_dtype)
        out_spec = pl.BlockSpec((tn // shard_n, tm, shard_n), lambda j, i, k: (j, i, 0))
    else:
        out_shape = jax.ShapeDtypeStruct((m_dim, n_dim), out_dtype)
        out_spec = pl.BlockSpec((tm, tn), lambda j, i, k: (i, j))
    res = pl.pallas_call(
        body, name=name, grid=grid,
        out_shape=[out_shape] + (side.out_shape if side else []),
        in_specs=[a_spec, b_spec] + side_in,
        out_specs=[out_spec] + side_out,
        scratch_shapes=[pltpu.VMEM((tm, tn), F32)] * n_acc + (side.sems if side else []),
        compiler_params=_params(("arbitrary", "arbitrary", "arbitrary") if side
                                else ("parallel", "parallel", "arbitrary")),
    )(a, b, *(side.inputs if side else []))
    return res if side else res[0]


def _ffn_in(u, w_gu):
    tokens, k_dim = u.shape
    width = w_gu.shape[1] // 2
    tm, tn = _pick_tile(tokens, 512), _pick_tile(width, 768)
    nj = width // tn

    def body(a_ref, bg_ref, bu_ref, g_ref, u_ref, act_ref):
        a = a_ref[...]
        g = _dot(a, bg_ref[...]).astype(BF16)
        up = _dot(a, bu_ref[...]).astype(BF16)
        g_ref[...] = g
        u_ref[...] = up
        act_ref[...] = (_silu(g.astype(F32)) * up.astype(F32)).astype(BF16)

    out = jax.ShapeDtypeStruct((tokens, width), BF16)
    tile = pl.BlockSpec((tm, tn), lambda j, i: (i, j))
    return pl.pallas_call(
        body, name="ffn_in", grid=(nj, tokens // tm),
        out_shape=[out, out, out],
        in_specs=[pl.BlockSpec((tm, k_dim), lambda j, i: (i, 0)),
                  pl.BlockSpec((k_dim, tn), lambda j, i: (0, j)),
                  pl.BlockSpec((k_dim, tn), lambda j, i: (0, j + nj))],
        out_specs=[tile, tile, tile],
        compiler_params=_params(("parallel", "parallel")),
    )(u, w_gu, w_gu)


def _rowcall(name, fn, tiled, consts, outs, accs=(), tile=512, scratch=()):
    specs, arrays = [], []
    for t in tiled:
        if isinstance(t, tuple):
            arr, width, blk = t
            specs.append(pl.BlockSpec((tile, width), functools.partial(lambda i, blk: (i, blk), blk=blk)))
        else:
            arr = t
            specs.append(pl.BlockSpec((tile, arr.shape[1]), lambda i: (i, 0)))
        arrays.append(arr)
    rows = arrays[0].shape[0]
    assert rows % tile == 0, (name, rows, tile)
    for c in consts:
        specs.append(pl.BlockSpec(c.shape, lambda i: (0, 0)))
        arrays.append(c)
    n_in, n_out, n_acc = len(arrays), len(outs), len(accs)
    out_shape = [jax.ShapeDtypeStruct((rows, c), dt) for c, dt in outs]
    out_specs = [pl.BlockSpec((tile, c), lambda i: (i, 0)) for c, _ in outs]
    out_shape += [jax.ShapeDtypeStruct(s, dt) for s, dt in accs]
    out_specs += [pl.BlockSpec(s, lambda i: (0, 0)) for s, _ in accs]

    def body(*refs):
        ins = [r[...] for r in refs[:n_in]]
        out_refs = refs[n_in:n_in + n_out]
        acc_refs = refs[n_in + n_out:n_in + n_out + n_acc]
        scr = refs[n_in + n_out + n_acc:]
        res = fn(*ins, *scr)
        if not isinstance(res, (tuple, list)):
            res = (res,)
        assert len(res) == n_out + n_acc, (name, len(res))
        for r, v in zip(out_refs, res[:n_out]):
            r[...] = v.astype(r.dtype)
        if n_acc:
            first = pl.program_id(0) == 0

            @pl.when(first)
            def _():
                for r, v in zip(acc_refs, res[n_out:]):
                    r[...] = v.astype(r.dtype)

            @pl.when(jnp.logical_not(first))
            def _():
                for r, v in zip(acc_refs, res[n_out:]):
                    r[...] += v.astype(r.dtype)

    res = pl.pallas_call(
        body, name=name, grid=(rows // tile,),
        out_shape=out_shape, in_specs=specs, out_specs=out_specs,
        scratch_shapes=list(scratch),
        compiler_params=_params(("arbitrary",)),
    )(*arrays)
    return res


def _sigmoid(x):
    return 1.0 / (1.0 + jnp.exp(-x))


def _softplus(x):
    return jnp.maximum(x, 0.0) + jnp.log(1.0 + jnp.exp(-jnp.abs(x)))


def _silu(x):
    return x * _sigmoid(x)


def _dsilu(x):
    s = _sigmoid(x)
    return s * (1.0 + x * (1.0 - s))


_GELU_C = math.sqrt(2.0 / math.pi)


def _gelu(x):
    return 0.5 * x * (1.0 + jnp.tanh(_GELU_C * (x + 0.044715 * x * x * x)))


def _dgelu(x):
    t = jnp.tanh(_GELU_C * (x + 0.044715 * x * x * x))
    return 0.5 * (1.0 + t) + 0.5 * x * (1.0 - t * t) * _GELU_C * (1.0 + 3.0 * 0.044715 * x * x)


def _neg_expm1(x):
    series = -x * (1.0 + x * (0.5 + x * (1.0 / 6.0 + x * (1.0 / 24.0 + x * (1.0 / 120.0)))))
    return jnp.where(x > -0.03, series, 1.0 - jnp.exp(x))


def _rms(x, g):
    r = lax.rsqrt(jnp.mean(x * x, axis=-1, keepdims=True) + EPS)
    return x * r * g


def _rms_bwd(x, g, dy):
    r = lax.rsqrt(jnp.mean(x * x, axis=-1, keepdims=True) + EPS)
    xhat = x * r
    dg = jnp.sum(dy * xhat, axis=0, keepdims=True)
    dxhat = dy * g
    dx = r * (dxhat - xhat * jnp.mean(dxhat * xhat, axis=-1, keepdims=True))
    return dx, dg


def _row_iota(shape):
    return lax.broadcasted_iota(jnp.int32, shape, 0)


def _shift_down(x, j):
    if j == 0:
        return x
    return jnp.where(_row_iota(x.shape) >= j, pltpu.roll(x, j, 0), 0.0)


def _shift_up(x, j):
    if j == 0:
        return x
    n = x.shape[0]
    return jnp.where(_row_iota(x.shape) < n - j, pltpu.roll(x, n - j, 0), 0.0)


def _conv(x, w, b):
    y = b + w[3:4, :] * x
    for k in range(3):
        y = y + w[k:k + 1, :] * _shift_down(x, 3 - k)
    return y


def _conv_bwd(x, w, dy):
    dx = w[3:4, :] * dy
    dws = []
    for k in range(3):
        dx = dx + w[k:k + 1, :] * _shift_up(dy, 3 - k)
        dws.append(jnp.sum(dy * _shift_down(x, 3 - k), axis=0, keepdims=True))
    dws.append(jnp.sum(dy * x, axis=0, keepdims=True))
    return dx, jnp.concatenate(dws, axis=0), jnp.sum(dy, axis=0, keepdims=True)


def _split3(x):
    hi = x.astype(BF16)
    r1 = x - hi.astype(F32)
    mid = r1.astype(BF16)
    lo = (r1 - mid.astype(F32)).astype(BF16)
    return hi, mid, lo


def _tri_dot(tri, x):
    hi, mid, lo = _split3(x)
    return _dot(tri, hi) + _dot(tri, mid) + _dot(tri, lo)


def _cumsum_rows(x):
    n = x.shape[0] // CHUNK
    r = lax.broadcasted_iota(jnp.int32, (CHUNK, CHUNK), 0)
    c = lax.broadcasted_iota(jnp.int32, (CHUNK, CHUNK), 1)
    tri = (r >= c).astype(BF16)
    carry = jnp.zeros((1, x.shape[1]), F32)
    cums, prevs, ends = [], [], []
    for i in range(n):
        blk = _tri_dot(tri, x[i * CHUNK:(i + 1) * CHUNK]) + carry
        prevs.append(jnp.broadcast_to(carry, blk.shape))
        carry = blk[CHUNK - 1:CHUNK, :]
        ends.append(jnp.broadcast_to(carry, blk.shape))
        cums.append(blk)
    return jnp.concatenate(cums, 0), jnp.concatenate(prevs, 0), jnp.concatenate(ends, 0)


def _rev_cumsum_rows(x):
    n = x.shape[0] // CHUNK
    r = lax.broadcasted_iota(jnp.int32, (CHUNK, CHUNK), 0)
    c = lax.broadcasted_iota(jnp.int32, (CHUNK, CHUNK), 1)
    tri = (r <= c).astype(BF16)
    carry = jnp.zeros((1, x.shape[1]), F32)
    local, whole = [None] * n, [None] * n
    for i in range(n - 1, -1, -1):
        local[i] = _tri_dot(tri, x[i * CHUNK:(i + 1) * CHUNK])
        whole[i] = local[i] + carry
        carry = whole[i][0:1, :]
    return jnp.concatenate(local, 0), jnp.concatenate(whole, 0)


def _convsilu_fwd(name, seg, seq, w, b, dtype):
    return _rowcall(name, lambda raw, w, b: _silu(_conv(raw, w, b)), [seg], [w, b], [(seg[1], dtype)], tile=seq)[0]


def _convsilu_bwd(name, seg, seq, w, b, dy):
    def fn(raw, dy, w, b):
        return _conv_bwd(raw, w, dy * _dsilu(_conv(raw, w, b)))

    width = seg[1]
    return _rowcall(name, fn, [seg, dy], [w, b], [(width, BF16)], [((4, width), F32), ((1, width), F32)], tile=seq)


def _small_fwd(seg, seq, bias128, alog128):
    def fn(small, bias, alog):
        lane = lax.broadcasted_iota(jnp.int32, small.shape, 1)
        a = -jnp.exp(alog)
        s = small + bias
        dt = _softplus(s)
        logf = -_softplus(-s)
        pre = jnp.where(lane < SSD_HEADS, a * dt, jnp.where(lane < 2 * SSD_HEADS, logf, 0.0))
        cum, prev, end = _cumsum_rows(pre)
        return dt, cum, prev, end

    return _rowcall("small_fwd", fn, [seg], [bias128, alog128], [(128, F32)] * 4, tile=seq)


def _small_bwd(seg, seq, dcum, dcq, dend, ddt, dt_arr, bias128, alog128):
    def fn(small, dcum, dcq, dend, ddt, dt_arr, bias, alog):
        lane = lax.broadcasted_iota(jnp.int32, small.shape, 1)
        for pair in range(FOX_PAIRS):
            first = SSD_HEADS + 2 * pair
            moved = pltpu.roll(dcq[:, pair * 128:(pair + 1) * 128], first, 1)
            dcum = dcum + jnp.where(jnp.logical_or(lane == first, lane == first + 1), moved, 0.0)
        a = -jnp.exp(alog)
        sig = _sigmoid(small + bias)
        local, whole = _rev_cumsum_rows(dcum)
        dadt = local + dend
        d_dt = ddt + a * dadt
        ds = jnp.where(lane < SSD_HEADS, d_dt * sig, jnp.where(lane < 2 * SSD_HEADS, whole * (1.0 - sig), 0.0))
        da = jnp.sum(jnp.where(lane < SSD_HEADS, dadt * dt_arr, 0.0), axis=0, keepdims=True)
        return ds, jnp.sum(ds, axis=0, keepdims=True), da * a

    return _rowcall("small_bwd", fn, [seg, dcum, dcq, dend, ddt, dt_arr], [bias128, alog128], [(128, BF16)],
                    [((1, 128), F32), ((1, 128), F32)], tile=seq)


HEAD_PAIRS = SSD_HEADS // 2


def _ssd_specs(nc, reverse):
    def at(c):
        return nc - 1 - c if reverse else c

    x_spec = pl.BlockSpec((1, CHUNK, SSD_WIDTH), lambda b, c: (b, at(c), 0))
    bc_spec = pl.BlockSpec((1, CHUNK, 256), lambda b, c: (b, at(c), 0))
    col_spec = pl.BlockSpec((1, CHUNK, 128), lambda b, c: (b, at(c), 0))
    row_spec = pl.BlockSpec((1, 8, CHUNK), lambda b, c: (b, 0, at(c)))
    st_spec = pl.BlockSpec((1, 1, HEAD_PAIRS, SSD_STATE, 128), lambda b, c: (b, at(c), 0, 0, 0))
    return x_spec, bc_spec, col_spec, row_spec, st_spec


def _ssd_head(h, dtb, acb, apb, aeb, arb):
    return dtb[:, h:h + 1], acb[:, h:h + 1], apb[:, h:h + 1], aeb[:, h:h + 1], arb[h:h + 1, :]


def _ssd_fwd(x, bm, cm, dt_arr, cum, prev, end, a_row):
    nb, seq, _ = x.shape
    nc = seq // CHUNK
    x_spec, bc_spec, col_spec, row_spec, st_spec = _ssd_specs(nc, False)

    def body(x_ref, b_ref, c_ref, dt_ref, ac_ref, ap_ref, ae_ref, ar_ref, y_ref, st_ref, s_scr):
        @pl.when(pl.program_id(1) == 0)
        def _():
            s_scr[...] = jnp.zeros_like(s_scr)

        causal = (lax.broadcasted_iota(jnp.int32, (CHUNK, CHUNK), 0)
                  >= lax.broadcasted_iota(jnp.int32, (CHUNK, CHUNK), 1))
        low = lax.broadcasted_iota(jnp.int32, (CHUNK, 128), 1) < HEAD_DIM
        cols = (dt_ref[0], ac_ref[0], ap_ref[0], ae_ref[0], ar_ref[0])
        bcs = [b_ref[0, :, g * 128:(g + 1) * 128] for g in range(2)]
        ccs = [c_ref[0, :, g * 128:(g + 1) * 128] for g in range(2)]
        ms = [_dot_nt(ccs[g], bcs[g]) for g in range(2)]
        for pi in range(HEAD_PAIRS):
            x2 = x_ref[0, :, pi * 128:(pi + 1) * 128]
            dt2 = jnp.where(low, cols[0][:, 2 * pi:2 * pi + 1], cols[0][:, 2 * pi + 1:2 * pi + 2])
            xdt_f = x2 * dt2
            xdt = xdt_f.astype(BF16)
            sprev = s_scr[pi]
            st_ref[0, 0, pi] = sprev
            spb = sprev.astype(BF16)
            ys, us = [], []
            for h in (2 * pi, 2 * pi + 1):
                g = h // 3
                _, ac, ap, ae, ar = _ssd_head(h, *cols)
                lm = jnp.exp(jnp.where(causal, ac - ar, NEG))
                gm = (ms[g] * lm).astype(BF16)
                ys.append(_dot(gm, xdt) + jnp.exp(ac - ap) * _dot(ccs[g], spb))
                us.append(jnp.exp(ae[0:1, :] - ap[0:1, :]) * sprev
                          + _dot_tn(bcs[g], (xdt_f * jnp.exp(ae - ac)).astype(BF16)))
            y_ref[0, :, pi * 128:(pi + 1) * 128] = jnp.where(low, ys[0], ys[1])
            s_scr[pi] = jnp.where(lax.broadcasted_iota(jnp.int32, (SSD_STATE, 128), 1) < HEAD_DIM, us[0], us[1])

    return pl.pallas_call(
        body, name="ssd_fwd", grid=(nb, nc),
        out_shape=[jax.ShapeDtypeStruct(x.shape, F32),
                   jax.ShapeDtypeStruct((nb, nc, HEAD_PAIRS, SSD_STATE, 128), F32)],
        in_specs=[x_spec, bc_spec, bc_spec, col_spec, col_spec, col_spec, col_spec, row_spec],
        out_specs=[x_spec, st_spec],
        scratch_shapes=[pltpu.VMEM((HEAD_PAIRS, SSD_STATE, 128), F32)],
        compiler_params=_params(("parallel", "arbitrary")),
    )(x, bm, cm, dt_arr, cum, prev, end, a_row)


def _ssd_bwd(x_h, bm, cm, dt_arr, cum, prev, end, a_row, states, dy_h):
    nb, seq, _ = x_h.shape
    nc = seq // CHUNK
    x_spec, bc_spec, col_spec, row_spec, st_spec = _ssd_specs(nc, True)

    def body(x_ref, b_ref, c_ref, dt_ref, ac_ref, ap_ref, ae_ref, ar_ref, st_ref, dy_ref,
             dx_ref, db_ref, dc_ref, da_ref, dend_ref, ddt_ref, ds_scr):
        @pl.when(pl.program_id(1) == 0)
        def _():
            ds_scr[...] = jnp.zeros_like(ds_scr)

        causal = (lax.broadcasted_iota(jnp.int32, (CHUNK, CHUNK), 0)
                  >= lax.broadcasted_iota(jnp.int32, (CHUNK, CHUNK), 1))
        lane = lax.broadcasted_iota(jnp.int32, (CHUNK, 128), 1)
        low = lane < HEAD_DIM
        low_state = lax.broadcasted_iota(jnp.int32, (SSD_STATE, 128), 1) < HEAD_DIM
        cols = (dt_ref[0], ac_ref[0], ap_ref[0], ae_ref[0], ar_ref[0])
        bcs = [b_ref[0, :, g * 128:(g + 1) * 128] for g in range(2)]
        ccs = [c_ref[0, :, g * 128:(g + 1) * 128] for g in range(2)]
        ms = [_dot_nt(ccs[g], bcs[g]) for g in range(2)]
        dms = [jnp.zeros((CHUNK, CHUNK), F32) for _ in range(2)]
        dc_accs = [jnp.zeros((CHUNK, SSD_STATE), F32) for _ in range(2)]
        db_accs = [jnp.zeros((CHUNK, SSD_STATE), F32) for _ in range(2)]
        da_blk = jnp.zeros((CHUNK, 128), F32)
        dend_blk = jnp.zeros((CHUNK, 128), F32)
        ddt_blk = jnp.zeros((CHUNK, 128), F32)
        for pi in range(HEAD_PAIRS):
            x2 = x_ref[0, :, pi * 128:(pi + 1) * 128]
            dy2 = dy_ref[0, :, pi * 128:(pi + 1) * 128]
            dt2 = jnp.where(low, cols[0][:, 2 * pi:2 * pi + 1], cols[0][:, 2 * pi + 1:2 * pi + 2])
            xdt_f = x2 * dt2
            xdt = xdt_f.astype(BF16)
            dyb = dy2.astype(BF16)
            dsn = ds_scr[pi]
            dsb = dsn.astype(BF16)
            sprev_f = st_ref[0, 0, pi]
            sprev = sprev_f.astype(BF16)
            dxdts, dss = [], []
            for h in (2 * pi, 2 * pi + 1):
                g = h // 3
                mine = low if h % 2 == 0 else jnp.logical_not(low)
                _, ac, ap, ae, ar = _ssd_head(h, *cols)
                bc, cc, m = bcs[g], ccs[g], ms[g]
                lm = jnp.exp(jnp.where(causal, ac - ar, NEG))
                gm = (m * lm).astype(BF16)
                dy_m = jnp.where(mine, dy2, 0.0)
                dyb_m = dy_m.astype(BF16)
                xdt_m = jnp.where(mine, xdt_f, 0.0)
                e_in = jnp.exp(ac - ap)
                f_out = jnp.exp(ae - ac)
                whole = jnp.exp(ae[0:1, :] - ap[0:1, :])
                dg = _dot_nt(dyb_m, xdt)
                dxdt_off = f_out * _dot(bc, dsb)
                dxdt = _dot_tn(gm, dyb) + dxdt_off
                dmj = dg * lm
                dms[g] = dms[g] + dmj
                dc_accs[g] = dc_accs[g] + e_in * _dot_nt(dyb_m, sprev)
                db_accs[g] = db_accs[g] + f_out * _dot_nt(xdt_m.astype(BF16), dsb)
                dss.append(whole * dsn + _dot_tn(cc, (dy2 * e_in).astype(BF16)))
                wmat = dmj * m
                r_in = jnp.sum(dy_m * (e_in * _dot(cc, sprev)), axis=1, keepdims=True)
                q_out = jnp.sum(xdt_m * dxdt_off, axis=1, keepdims=True)
                daj = (jnp.sum(wmat, axis=1, keepdims=True) - jnp.sum(wmat.T, axis=1, keepdims=True)
                       + r_in - q_out)
                cross = jnp.where(low_state if h % 2 == 0 else jnp.logical_not(low_state), dsn * sprev_f, 0.0)
                dendj = (jnp.sum(q_out, axis=0, keepdims=True)
                         + whole * jnp.sum(jnp.sum(cross, axis=1, keepdims=True), axis=0, keepdims=True))
                ddtj = jnp.sum(jnp.where(mine, dxdt * x2, 0.0), axis=1, keepdims=True)
                dxdts.append(dxdt)
                da_blk = jnp.where(lane == h, daj, da_blk)
                dend_blk = jnp.where(lane == h, dendj, dend_blk)
                ddt_blk = jnp.where(lane == h, ddtj, ddt_blk)
            dx_ref[0, :, pi * 128:(pi + 1) * 128] = jnp.where(low, dxdts[0], dxdts[1]) * dt2
            ds_scr[pi] = jnp.where(low_state, dss[0], dss[1])
        for g in range(2):
            dmb = dms[g].astype(BF16)
            dc_ref[0, :, g * 128:(g + 1) * 128] = dc_accs[g] + _dot(dmb, bcs[g])
            db_ref[0, :, g * 128:(g + 1) * 128] = db_accs[g] + _dot_tn(dmb, ccs[g])
        da_ref[0] = da_blk
        dend_ref[0] = dend_blk
        ddt_ref[0] = ddt_blk

    col_shape = jax.ShapeDtypeStruct((nb, seq, 128), F32)
    return pl.pallas_call(
        body, name="ssd_bwd", grid=(nb, nc),
        out_shape=[jax.ShapeDtypeStruct(x_h.shape, F32),
                   jax.ShapeDtypeStruct((nb, seq, 256), F32), jax.ShapeDtypeStruct((nb, seq, 256), F32),
                   col_shape, col_shape, col_shape],
        in_specs=[x_spec, bc_spec, bc_spec, col_spec, col_spec, col_spec, col_spec, row_spec, st_spec, x_spec],
        out_specs=[x_spec, bc_spec, bc_spec, col_spec, col_spec, col_spec],
        scratch_shapes=[pltpu.VMEM((HEAD_PAIRS, SSD_STATE, 128), F32)],
        compiler_params=_params(("parallel", "arbitrary")),
    )(x_h, bm, cm, dt_arr, cum, prev, end, a_row, states, dy_h)


def _lru_gates(xl, wa, ba, wx, bx, lam):
    xb = xl.astype(BF16)
    r = _sigmoid(_dot(xb, wa) + ba)
    i = _sigmoid(_dot(xb, wx) + bx)
    sp = _softplus(-lam)
    log_a = -LRU_C * r * sp
    a = jnp.exp(log_a)
    mult = jnp.sqrt(_neg_expm1(2.0 * log_a))
    return r, i, sp, log_a, a, mult


def _scan_chunks(a_ref, u_ref, h_ref, seq, reverse):
    nc = seq // CHUNK
    width = a_ref.shape[1]
    row = lax.broadcasted_iota(jnp.int32, (CHUNK, width), 0)

    def chunk(ci, carry):
        c = nc - 1 - ci if reverse else ci
        rows = pl.ds(pl.multiple_of(c * CHUNK, CHUNK), CHUNK)
        av, bv = a_ref[rows, :], u_ref[rows, :]
        d = 1
        while d < CHUNK:
            if reverse:
                keep = row < CHUNK - d
                a_sh = jnp.where(keep, pltpu.roll(av, CHUNK - d, 0), 1.0)
                b_sh = jnp.where(keep, pltpu.roll(bv, CHUNK - d, 0), 0.0)
            else:
                keep = row >= d
                a_sh = jnp.where(keep, pltpu.roll(av, d, 0), 1.0)
                b_sh = jnp.where(keep, pltpu.roll(bv, d, 0), 0.0)
            bv = av * b_sh + bv
            av = av * a_sh
            d *= 2
        hv = bv + av * carry
        h_ref[rows, :] = hv
        return hv[0:1, :] if reverse else hv[CHUNK - 1:CHUNK, :]

    lax.fori_loop(0, nc, chunk, jnp.zeros((1, width), F32))


def _lru_fwd(proj, seq, cw, cb, wa, ba, wx, bx, lam):
    def fn(raw, cw, cb, wa, ba, wx, bx, lam, a_scr, u_scr, h_scr):
        xl = _conv(raw, cw, cb)
        r, i, sp, log_a, a, mult = _lru_gates(xl, wa, ba, wx, bx, lam)
        a_scr[...] = a
        u_scr[...] = mult * (i * xl)
        _scan_chunks(a_scr, u_scr, h_scr, seq, reverse=False)
        return h_scr[...], xl

    return _rowcall("lru_fwd", fn, [(proj, 256, 2)], [cw, cb, wa, ba, wx, bx, lam],
                    [(256, F32), (256, F32)], tile=seq,
                    scratch=[pltpu.VMEM((seq, 256), F32)] * 3)


def _lru_bwd(proj, seq, xl_all, h_all, dh_all, cw, cb, wa, ba, wx, bx, lam):
    def fn(raw, xl, hseq, dh, cw, cb, wa, ba, wx, bx, lam, a_scr, u_scr, h_scr):
        r, i, sp, log_a, a, mult = _lru_gates(xl, wa, ba, wx, bx, lam)
        a_scr[...] = _shift_up(a, 1)
        u_scr[...] = dh
        _scan_chunks(a_scr, u_scr, h_scr, seq, reverse=True)
        dht = h_scr[...]
        da = dht * _shift_down(hseq, 1)
        gated = i * xl
        dgated = dht * mult
        dmult = dht * gated
        dlog_a = da * a - dmult * (a * a) / mult
        dr = dlog_a * (-LRU_C * sp)
        dsp = jnp.sum(dlog_a * (-LRU_C * r), axis=0, keepdims=True)
        dlam = -dsp * _sigmoid(-lam)
        dpa = dr * r * (1.0 - r)
        dpx = (dgated * xl) * i * (1.0 - i)
        dpa_b, dpx_b = dpa.astype(BF16), dpx.astype(BF16)
        dxl = dgated * i + _dot_nt(dpa_b, wa) + _dot_nt(dpx_b, wx)
        xb = xl.astype(BF16)
        dwa = _dot_tn(xb, dpa_b)
        dwx = _dot_tn(xb, dpx_b)
        draw, dcw, dcb = _conv_bwd(raw, cw, dxl)
        return (draw, dcw, dcb, dwa, jnp.sum(dpa, axis=0, keepdims=True), dwx,
                jnp.sum(dpx, axis=0, keepdims=True), dlam)

    return _rowcall("lru_bwd", fn, [(proj, 256, 2), xl_all, h_all, dh_all], [cw, cb, wa, ba, wx, bx, lam],
                    [(256, BF16)],
                    [((4, 256), F32), ((1, 256), F32), ((256, 256), F32), ((1, 256), F32), ((256, 256), F32),
                     ((1, 256), F32), ((1, 256), F32)],
                    tile=seq, scratch=[pltpu.VMEM((seq, 256), F32)] * 3)


FOX_SCALE = HEAD_DIM ** -0.5
FOX_BLOCK = 512


class _Side:
    def __init__(self, inputs, out_shape, sems, build):
        self.inputs, self.out_shape, self.sems, self.build = list(inputs), list(out_shape), list(sems), build

    def specs(self):
        any_spec = pl.BlockSpec(memory_space=pl.ANY)
        return [any_spec] * len(self.inputs), [any_spec] * len(self.out_shape)

    def run(self, refs, first, last):
        n_in, n_out = len(self.inputs), len(self.out_shape)
        in_refs, out_refs, sem_refs = refs[:n_in], refs[n_in:n_in + n_out], refs[n_in + n_out:]

        @pl.when(first)
        def _():
            for cp in self.build(in_refs, out_refs, sem_refs):
                cp.start()

        @pl.when(last)
        def _():
            for cp in self.build(in_refs, out_refs, sem_refs):
                cp.wait()


def _grid_ends(grid):
    ids = [pl.program_id(a) for a in range(len(grid))]
    first = functools.reduce(jnp.logical_and, [i == 0 for i in ids])
    last = functools.reduce(jnp.logical_and, [i == n - 1 for i, n in zip(ids, grid)])
    return first, last


Q_BLK, K_BLK, V_BLK = 1920 // 128, 2304 // 128, 2688 // 128
FOX_PAIRS = FOX_HEADS // 2


def _fox_bias(cum, nb, seq):
    cf = cum.reshape(nb, seq, 128)[:, :, SSD_HEADS:SSD_HEADS + FOX_HEADS]
    cols = jnp.pad(cf.reshape(nb * seq, FOX_PAIRS, 2), ((0, 0), (0, 0), (0, 126))).reshape(nb * seq, 384)
    rows = jnp.pad(cf.transpose(0, 2, 1).reshape(nb, FOX_PAIRS, 2, seq), ((0, 0), (0, 0), (0, 6), (0, 0)))
    return cols, rows


def _fox_fwd(proj, bias_cols, bias_rows, nb, seq, side=None):
    tb = min(FOX_BLOCK, seq)
    nq = seq // tb
    grid = (nb, FOX_PAIRS, nq)
    n_in = len(side.inputs) if side else 0

    def body(*refs):
        q_ref, k_ref, v_ref, cq_ref, ck_ref = refs[:5]
        o_ref, lse_ref = refs[5 + n_in:7 + n_in]
        if side is not None:
            side.run(refs[5:5 + n_in] + refs[7 + n_in:], *_grid_ends(grid))
        qi = pl.program_id(2)
        low = lax.broadcasted_iota(jnp.int32, (tb, 128), 1) < HEAD_DIM
        q2 = q_ref[...] * FOX_SCALE
        qm = [jnp.where(low, q2, 0.0).astype(BF16), jnp.where(low, 0.0, q2).astype(BF16)]
        cqs = [cq_ref[:, 0:1], cq_ref[:, 1:2]]

        def block(j, carry, diagonal):
            cols = pl.ds(pl.multiple_of(j * tb, tb), tb)
            k2 = k_ref[cols, :].astype(BF16)
            v2 = v_ref[cols, :].astype(BF16)
            new = []
            for hh in range(2):
                m_i, l_i, acc = carry[hh]
                s = _dot_nt(qm[hh], k2) + cqs[hh] - ck_ref[0, 0, hh:hh + 1, cols]
                if diagonal:
                    s = jnp.where(lax.broadcasted_iota(jnp.int32, (tb, tb), 0)
                                  >= lax.broadcasted_iota(jnp.int32, (tb, tb), 1), s, NEG)
                m_new = jnp.maximum(m_i, jnp.max(s, axis=1, keepdims=True))
                p = jnp.exp(s - m_new)
                alpha = jnp.exp(m_i - m_new)
                new.append((m_new, alpha * l_i + jnp.sum(p, axis=1, keepdims=True),
                            alpha * acc + _dot(p.astype(BF16), v2)))
            return tuple(new)

        one = (jnp.full((tb, 1), NEG, F32), jnp.zeros((tb, 1), F32), jnp.zeros((tb, 128), F32))
        carry = lax.fori_loop(0, qi, lambda j, cr: block(j, cr, False), (one, one))
        (m0, l0, a0), (m1, l1, a1) = block(qi, carry, True)
        o_ref[...] = jnp.where(low, a0 / l0, a1 / l1)
        lse_ref[...] = jnp.where(low, m0 + jnp.log(l0), m1 + jnp.log(l1))

    def blk(first):
        return pl.BlockSpec((tb, 128), lambda b, p, i: (b * nq + i, first + p))

    def seq_blk(first):
        return pl.BlockSpec((seq, 128), lambda b, p, i: (b, first + p))

    row_spec = pl.BlockSpec((1, 1, 8, seq), lambda b, p, i: (b, p, 0, 0))
    side_in, side_out = side.specs() if side else ([], [])
    shape = jax.ShapeDtypeStruct((nb * seq, FOX_WIDTH), F32)
    return pl.pallas_call(
        body, name="fox_fwd", grid=grid,
        out_shape=[shape, shape] + (side.out_shape if side else []),
        in_specs=[blk(Q_BLK), seq_blk(K_BLK), seq_blk(V_BLK), blk(0), row_spec] + side_in,
        out_specs=[blk(0), blk(0)] + side_out,
        scratch_shapes=side.sems if side else [],
        compiler_params=_params(("arbitrary", "arbitrary", "arbitrary")),
    )(proj, proj, proj, bias_cols, bias_rows, *(side.inputs if side else []))


def _fox_bwd(proj, o, lse, do, bias_cols, bias_rows, nb, seq, side=None):
    tb = min(FOX_BLOCK, seq)
    nq = seq // tb
    grid = (nb, FOX_PAIRS, nq)
    n_in = len(side.inputs) if side else 0

    def body(*refs):
        q_ref, k_ref, v_ref, o_ref, lse_ref, do_ref, cq_ref, ck_ref = refs[:8]
        dq_ref, dk_ref, dv_ref, dcum_ref, dcq_ref = refs[8 + n_in:13 + n_in]
        if side is not None:
            side.run(refs[8:8 + n_in] + refs[13 + n_in:], *_grid_ends(grid))
        kj = pl.program_id(2)

        @pl.when(kj == 0)
        def _():
            dq_ref[...] = jnp.zeros_like(dq_ref)
            dcq_ref[...] = jnp.zeros_like(dcq_ref)

        lane = lax.broadcasted_iota(jnp.int32, (tb, 128), 1)
        low = lane < HEAD_DIM
        mine = [low, jnp.logical_not(low)]
        k2 = k_ref[...]
        kb = k2.astype(BF16)
        km = [jnp.where(mine[hh], k2, 0.0).astype(BF16) for hh in range(2)]
        vb = v_ref[...].astype(BF16)

        def block(i, carry, diagonal):
            dk, dv, c0, c1 = carry
            csum = [c0, c1]
            rows = pl.ds(pl.multiple_of(i * tb, tb), tb)
            q2 = q_ref[rows, :] * FOX_SCALE
            do2 = do_ref[rows, :]
            prod = do2 * o_ref[rows, :]
            dq_add = jnp.zeros((tb, 128), F32)
            rsum = []
            for hh in range(2):
                qm = jnp.where(mine[hh], q2, 0.0).astype(BF16)
                dom = jnp.where(mine[hh], do2, 0.0).astype(BF16)
                delta = jnp.sum(jnp.where(mine[hh], prod, 0.0), axis=1, keepdims=True)
                s = _dot_nt(qm, kb) + cq_ref[rows, hh:hh + 1] - ck_ref[0, 0, hh:hh + 1, :]
                if diagonal:
                    s = jnp.where(lax.broadcasted_iota(jnp.int32, (tb, tb), 0)
                                  >= lax.broadcasted_iota(jnp.int32, (tb, tb), 1), s, NEG)
                p = jnp.exp(s - lse_ref[rows, HEAD_DIM * hh:HEAD_DIM * hh + 1])
                ds = p * (_dot_nt(dom, vb) - delta)
                dsb = ds.astype(BF16)
                dv = dv + _dot_tn(p.astype(BF16), dom)
                dk = dk + _dot_tn(dsb, qm)
                dq_add = dq_add + _dot(dsb, km[hh])
                rsum.append(jnp.sum(ds, axis=1, keepdims=True))
                csum[hh] = csum[hh] + jnp.sum(ds, axis=0, keepdims=True)
            dq_ref[rows, :] += dq_add * FOX_SCALE
            dcq_ref[rows, :] += jnp.where(lane == 0, rsum[0], jnp.where(lane == 1, rsum[1], 0.0))
            return dk, dv, csum[0], csum[1]

        init = (jnp.zeros((tb, 128), F32), jnp.zeros((tb, 128), F32), jnp.zeros((1, tb), F32),
                jnp.zeros((1, tb), F32))
        carry = block(kj, init, True)
        dk, dv, c0, c1 = lax.fori_loop(kj + 1, nq, lambda i, cr: block(i, cr, False), carry)
        dk_ref[...] = dk.astype(dk_ref.dtype)
        dv_ref[...] = dv.astype(dv_ref.dtype)
        row = lax.broadcasted_iota(jnp.int32, (8, tb), 0)
        dcum_ref[0, 0] = jnp.where(row == 0, -c0, jnp.where(row == 1, -c1, 0.0))

    def blk(first):
        return pl.BlockSpec((tb, 128), lambda b, p, j: (b * nq + j, first + p))

    def seq_blk(first):
        return pl.BlockSpec((seq, 128), lambda b, p, j: (b, first + p))

    row_blk = pl.BlockSpec((1, 1, 8, tb), lambda b, p, j: (b, p, 0, j))
    side_in, side_out = side.specs() if side else ([], [])
    tokens = nb * seq
    return pl.pallas_call(
        body, name="fox_bwd", grid=grid,
        out_shape=[jax.ShapeDtypeStruct((tokens, FOX_WIDTH), F32), jax.ShapeDtypeStruct((tokens, FOX_WIDTH), BF16),
                   jax.ShapeDtypeStruct((tokens, FOX_WIDTH), BF16),
                   jax.ShapeDtypeStruct((nb, FOX_PAIRS, 8, seq), F32),
                   jax.ShapeDtypeStruct((tokens, FOX_WIDTH), F32)] + (side.out_shape if side else []),
        in_specs=[seq_blk(Q_BLK), blk(K_BLK), blk(V_BLK), seq_blk(0), seq_blk(0), seq_blk(0), seq_blk(0), row_blk]
        + side_in,
        out_specs=[seq_blk(0), blk(0), blk(0), row_blk, seq_blk(0)] + side_out,
        scratch_shapes=side.sems if side else [],
        compiler_params=_params(("arbitrary", "arbitrary", "arbitrary")),
    )(proj, proj, proj, o, lse, do, bias_cols, bias_rows, *(side.inputs if side else []))


_ANY = pl.BlockSpec(memory_space=pl.ANY)


def _place():
    return lax.axis_index("x"), lax.axis_index("y"), lax.axis_index("c")


def _all_gather(shards, name):
    n = len(shards)

    def body(*refs):
        x_refs, out_refs = refs[:n], refs[n:2 * n]
        send_sems, recv_sems, local_sems = refs[2 * n:]
        x, y, c = _place()
        me, sibling = (x, y, c), (x, y, 1 - c)
        chips = [(1 - x, y), (x, 1 - y), (1 - x, 1 - y)]

        def rows(a, px, py, pc):
            return out_refs[a].at[4 * px + 2 * py + pc]

        def copy(a, k, block, to, src=None):
            return pltpu.make_async_remote_copy(
                src_ref=rows(a, *block) if src is None else src, dst_ref=rows(a, *block),
                send_sem=send_sems.at[a, k], recv_sem=recv_sems.at[a, k],
                device_id=to, device_id_type=pl.DeviceIdType.MESH)

        mine = [pltpu.make_async_copy(x_refs[a], rows(a, *me), local_sems.at[a]) for a in range(n)]
        for cp in mine:
            cp.start()
        first = []
        for a in range(n):
            first.append(copy(a, 0, me, sibling, src=x_refs[a]))
            first += [copy(a, 1 + j, me, (*chip, c), src=x_refs[a]) for j, chip in enumerate(chips)]
        for cp in first:
            cp.start()
        passed = []
        for j, chip in enumerate(chips):
            for a in range(n):
                copy(a, 1 + j, (*chip, c), me).wait_recv()
                passed.append(copy(a, 4 + j, (*chip, c), sibling))
                passed[-1].start()
        for a in range(n):
            copy(a, 0, sibling, me).wait_recv()
            for j, chip in enumerate(chips):
                copy(a, 4 + j, (*chip, 1 - c), me).wait_recv()
        for cp in first + passed:
            cp.wait_send()
        for cp in mine:
            cp.wait()

    return pl.pallas_call(
        body, name=name,
        out_shape=[jax.ShapeDtypeStruct((N_DEV,) + s.shape, s.dtype) for s in shards],
        in_specs=[_ANY] * n, out_specs=[_ANY] * n,
        scratch_shapes=[pltpu.SemaphoreType.DMA((n, 7)), pltpu.SemaphoreType.DMA((n, 7)),
                        pltpu.SemaphoreType.DMA((n,))],
    )(*shards)


def _remote(src, dst, send_sem, recv_sem, to):
    return pltpu.make_async_remote_copy(src_ref=src, dst_ref=dst, send_sem=send_sem, recv_sem=recv_sem,
                                        device_id=to, device_id_type=pl.DeviceIdType.MESH)


def _sem_pairs(n, k):
    return [pltpu.SemaphoreType.DMA((n, k)), pltpu.SemaphoreType.DMA((n, k))]


def _sibling_side(full, offsets):
    def build(g_refs, out_refs, sems):
        x, y, c = _place()
        return [_remote(g_refs[a].at[offsets[a] + 4 * (k // 2) + 2 * (k % 2) + (1 - c)], out_refs[a].at[k],
                        sems[0].at[a, k], sems[1].at[a, k], (x, y, 1 - c))
                for a in range(len(g_refs)) for k in range(4)]

    return _Side(full, [jax.ShapeDtypeStruct((4,) + f.shape[1:], f.dtype) for f in full],
                 _sem_pairs(len(full), 4), build)


def _chip_side(part):
    def build(p_refs, out_refs, sems):
        x, y, c = _place()
        peers = [(1 - x, y), (x, 1 - y), (1 - x, 1 - y)]
        return [_remote(p_refs[a].at[2 * px + py], out_refs[a].at[k], sems[0].at[a, k], sems[1].at[a, k],
                        (px, py, c))
                for a in range(len(p_refs)) for k, (px, py) in enumerate(peers)]

    return _Side(part, [jax.ShapeDtypeStruct((3,) + p.shape[1:], p.dtype) for p in part],
                 _sem_pairs(len(part), 3), build)


def _spread_side(shards):
    def build(x_refs, out_refs, sems):
        x, y, c = _place()
        targets = [(x, y, 1 - c), (1 - x, y, c), (x, 1 - y, c), (1 - x, 1 - y, c)]
        cps = []
        for a in range(len(x_refs)):
            slot = out_refs[a].at[4 * x + 2 * y + c]
            cps.append(pltpu.make_async_copy(x_refs[a], slot, sems[2].at[a]))
            cps += [_remote(x_refs[a], slot, sems[0].at[a, k], sems[1].at[a, k], to)
                    for k, to in enumerate(targets)]
        return cps

    n = len(shards)
    return _Side(shards, [jax.ShapeDtypeStruct((N_DEV,) + s.shape, s.dtype) for s in shards],
                 _sem_pairs(n, 4) + [pltpu.SemaphoreType.DMA((n,))], build)


def _pass_side(bufs):
    def build(in_refs, out_refs, sems):
        x, y, c = _place()
        chips = [(1 - x, y), (x, 1 - y), (1 - x, 1 - y)]
        return [_remote(in_refs[a].at[4 * px + 2 * py + c], out_refs[a].at[4 * px + 2 * py + c],
                        sems[0].at[a, j], sems[1].at[a, j], (x, y, 1 - c))
                for a in range(len(in_refs)) for j, (px, py) in enumerate(chips)]

    return _Side(bufs, [jax.ShapeDtypeStruct(b.shape, b.dtype) for b in bufs], _sem_pairs(len(bufs), 3), build)


def _run_side(side, name, in_place=False):
    n_in = len(side.inputs)

    def body(*refs):
        copies = side.build(refs[:n_in], refs[n_in:n_in + len(side.out_shape)],
                            refs[n_in + len(side.out_shape):])
        for cp in copies:
            cp.start()
        for cp in copies:
            cp.wait()

    in_specs, out_specs = side.specs()
    return pl.pallas_call(
        body, name=name, out_shape=side.out_shape, in_specs=in_specs, out_specs=out_specs,
        scratch_shapes=side.sems,
        input_output_aliases={a: a for a in range(n_in)} if in_place else {},
    )(*side.inputs)


def _pick_rows(rows, cap=512):
    t = cap
    while t >= 8:
        if rows % t == 0:
            return t
        t //= 2
    raise ValueError(rows)


def _pair_sum(full, offset, got, name):
    _, rows, cols = full.shape
    tile = _pick_rows(rows, 256)
    c = lax.axis_index("c").astype(jnp.int32).reshape(1)

    def body(c_ref, a_ref, b_ref, o_ref):
        o_ref[...] = a_ref[...] + b_ref[...]

    blk = (1, tile, cols)
    return pl.pallas_call(
        body, name=name,
        grid_spec=pltpu.PrefetchScalarGridSpec(
            num_scalar_prefetch=1, grid=(4, rows // tile),
            in_specs=[pl.BlockSpec(blk, lambda k, i, c_ref: (offset + 4 * (k // 2) + 2 * (k % 2) + c_ref[0], i, 0)),
                      pl.BlockSpec(blk, lambda k, i, c_ref: (k, i, 0))],
            out_specs=pl.BlockSpec(blk, lambda k, i, c_ref: (k, i, 0))),
        out_shape=jax.ShapeDtypeStruct((4, rows, cols), full.dtype),
        compiler_params=_params(("arbitrary", "arbitrary")),
    )(c, full, got)


def _adam_math(w, g, m, v):
    c1 = 1.0 / (1.0 - ADAM_B1 ** ADAM_STEP)
    c2 = 1.0 / (1.0 - ADAM_B2 ** ADAM_STEP)
    m_new = ADAM_B1 * m + (1.0 - ADAM_B1) * g
    v_new = ADAM_B2 * v + (1.0 - ADAM_B2) * (g * g)
    delta = -ADAM_LR * ((m_new * c1) / (jnp.sqrt(v_new * c2) + ADAM_EPS) + ADAM_WD * w)
    return delta, m_new, v_new


def _chip_sum(part, others, name):
    _, rows, cols = part.shape
    tile = _pick_rows(rows, 256)
    own = (2 * lax.axis_index("x") + lax.axis_index("y")).astype(jnp.int32).reshape(1)

    def body(own_ref, p_ref, o_ref, g_out):
        g_out[...] = ((p_ref[0] + o_ref[0]) + o_ref[1]) + o_ref[2]

    return pl.pallas_call(
        body, name=name,
        grid_spec=pltpu.PrefetchScalarGridSpec(
            num_scalar_prefetch=1, grid=(rows // tile,),
            in_specs=[pl.BlockSpec((1, tile, cols), lambda i, own_ref: (own_ref[0], i, 0)),
                      pl.BlockSpec((3, tile, cols), lambda i, own_ref: (0, i, 0))],
            out_specs=pl.BlockSpec((tile, cols), lambda i, own_ref: (i, 0))),
        out_shape=jax.ShapeDtypeStruct((rows, cols), F32),
        compiler_params=_params(("arbitrary",)),
    )(own, part, others)


def _all_reduce_small(vec):
    gathered = _all_gather([vec], "ar_gather")[0]
    rows = vec.shape[0]

    def fn(*blocks):
        s = blocks[0]
        for b in blocks[1:]:
            s = s + b
        return s

    return _rowcall("ar_sum", fn, [gathered[j] for j in range(N_DEV)], [], [(1024, F32)],
                    tile=_pick_rows(rows))[0]


def _pad_rows(flat, mult):
    n = flat.shape[-1]
    per = mult * 1024
    padded = -(-n // per) * per
    pad = [(0, 0)] * (flat.ndim - 1) + [(0, padded - n)]
    return jnp.pad(flat, pad).reshape(flat.shape[:-1] + (padded // 1024, 1024))


def _regroup_w_in(w):
    pad = jnp.zeros((w.shape[0], 116), w.dtype)
    return jnp.concatenate([w[:, 768:1280], w[:, 1286:1798], w[:, 1280:1286], w[:, 2950:2956], pad,
                            w[:, 0:768], w[:, 1798:2950]], axis=1)


def _ungroup_w_in(wp):
    return jnp.concatenate([wp[:, 1152:1920], wp[:, 0:512], wp[:, 1024:1030], wp[:, 512:1024],
                            wp[:, 1920:3072], wp[:, 1030:1036]], axis=1)


def _to_shard(name, a):
    if name == 'w_in':
        return _regroup_w_in(a)
    if name in ('w_gate', 'w_up'):
        return jnp.pad(a, ((0, 0), (0, FF_SHARD_P - FF_SHARD)))
    if name == 'w_down':
        return jnp.pad(a, ((0, FF_SHARD_P - FF_SHARD), (0, 0)))
    return a


def _from_shard(name, a):
    if name == 'w_in':
        return _ungroup_w_in(a)
    if name in ('w_gate', 'w_up'):
        return a[:, 0:FF_SHARD]
    if name == 'w_down':
        return a[0:FF_SHARD, :]
    return a


def _whole(name, gathered):
    if BIG[name][0] == 1:
        return gathered.reshape(-1, gathered.shape[-1])
    return gathered.transpose(1, 0, 2).reshape(gathered.shape[1], -1)


def _split(name, whole):
    if BIG[name][0] == 1:
        return whole.reshape(N_DEV, whole.shape[0] // N_DEV, whole.shape[1])
    return whole.reshape(whole.shape[0], N_DEV, whole.shape[1] // N_DEV).transpose(1, 0, 2)


def _pack_list(arrays, mult):
    return _pad_rows(jnp.concatenate([a.reshape(-1) for a in arrays]), mult)


def _unpack_list(buf, shapes):
    flat = buf.reshape(-1)
    out, off = [], 0
    for s in shapes:
        n = math.prod(s)
        out.append(flat[off:off + n].reshape(s))
        off += n
    return out


def _adamw(w, g, m, v, name="adamw"):
    rows, cols = w.shape
    return _rowcall(name, _adam_math, [w, g, m, v], [], [(cols, F32)] * 3, tile=_pick_rows(rows, 256))


def _block_diag(w):
    out = jnp.zeros((LRU_WIDTH, LRU_WIDTH), w.dtype)
    for g in range(4):
        out = lax.dynamic_update_slice(out, w[g], (64 * g, 64 * g))
    return out


def _block_diag_grad(full):
    return jnp.stack([full[64 * g:64 * (g + 1), 64 * g:64 * (g + 1)] for g in range(4)])


def _row(v):
    return v.reshape(1, -1).astype(F32)


def _lane128(*pieces):
    flat = jnp.concatenate([p.reshape(-1).astype(F32) for p in pieces])
    return jnp.pad(flat, (0, 128 - flat.shape[0])).reshape(1, 128)


def _layer_consts(w):
    c = {}
    cw, cb = w['ssd_conv_w'], w['ssd_conv_b']
    c['cw_x'], c['cw_b'], c['cw_c'] = cw[:, 0:384], cw[:, 384:640], cw[:, 640:896]
    c['cb_x'], c['cb_b'], c['cb_c'] = _row(cb[0:384]), _row(cb[384:640]), _row(cb[640:896])
    c['bias128'] = _lane128(w['ssd_dt_bias'], w['fox_b_f'])
    c['alog128'] = _lane128(w['ssd_a_log'])
    c['d384'] = _row(jnp.repeat(w['ssd_d'], HEAD_DIM))
    c['lcw'], c['lcb'] = w['lru_conv_w'], _row(w['lru_conv_b'])
    c['wa'], c['wx'] = _block_diag(w['lru_w_a']).astype(BF16), _block_diag(w['lru_w_x']).astype(BF16)
    c['ba'], c['bx'], c['lam'] = _row(w['lru_b_a']), _row(w['lru_b_x']), _row(w['lru_lambda'])
    return c


def _layer_fwd(h0, p_i, w, c, hooks=None, layer=0):
    nb, seq = c['nb'], c['seq']

    def carried(stage):
        return hooks.fwd_side(layer, stage) if hooks is not None else None

    def arrived(outs):
        if hooks is not None:
            hooks.fwd_done(outs)
            w.update(hooks.weights(layer))

    u1 = _rowcall("norm1", lambda h, g: _rms(h, g), [h0], [_row(w['norm1_g'])], [(D_MODEL, BF16)])[0]
    proj = _matmul(u1, w['w_in'], 'nn', "proj")

    xs_c = _convsilu_fwd("conv_x", (proj, 384, 4), seq, c['cw_x'], c['cb_x'], F32)
    b_c = _convsilu_fwd("conv_b", (proj, 256, 0), seq, c['cw_b'], c['cb_b'], BF16)
    c_c = _convsilu_fwd("conv_c", (proj, 256, 1), seq, c['cw_c'], c['cb_c'], BF16)
    dt_arr, cum, prev, end = _small_fwd((proj, 128, 8), seq, c['bias128'], c['alog128'])
    x_h = xs_c.reshape(nb, seq, SSD_WIDTH)
    cum3 = cum.reshape(nb, seq, 128)
    a_row = cum3[:, :, 0:8].transpose(0, 2, 1)
    ssd_in = (x_h, b_c.reshape(nb, seq, 256), c_c.reshape(nb, seq, 256), dt_arr.reshape(nb, seq, 128), cum3,
              prev.reshape(nb, seq, 128), end.reshape(nb, seq, 128), a_row)
    y_h, states = _ssd_fwd(*ssd_in)
    y_core = y_h.reshape(nb * seq, SSD_WIDTH)

    hseq, xl = _lru_fwd(proj, seq, c['lcw'], c['lcb'], c['wa'], c['ba'], c['wx'], c['bx'], c['lam'])

    bias_cols, bias_rows = _fox_bias(cum, nb, seq)
    y_fox, lse, *side_out = _fox_fwd(proj, bias_cols, bias_rows, nb, seq, carried('attention'))
    arrived(side_out)

    def post(yc, xs, z, hs, lg, yf, d, g1, g2, g3):
        y1 = _rms((yc + xs * d) * _silu(z), g1)
        y2 = _rms(hs * _gelu(lg), g2)
        y3 = _rms(yf, g3)
        return jnp.concatenate([y1, y2, y3], axis=-1)

    post_consts = [c['d384'], _row(w['ssd_norm_g']), _row(w['lru_norm_g']), _row(w['fox_norm_g'])]
    ycat = _rowcall("mix_post", post, [y_core, xs_c, (proj, 384, 3), hseq, (proj, 256, 3), y_fox], post_consts,
                    [(D_MODEL, BF16)])[0]
    mix = _matmul(ycat, w['w_out'], 'nn', "mix_out", BF16)

    def res_norm(h, d, g):
        hn = h + d.astype(F32)
        return hn, _rms(hn, g)

    h1, u2 = _rowcall("res_norm2", res_norm, [h0, mix], [_row(w['norm2_g'])], [(D_MODEL, F32), (D_MODEL, BF16)])
    gate_pre, up_pre, act = _ffn_in(u2, w['w_gu'])
    side = carried('ffn_out')
    if side is None:
        ff = _matmul(act, w['w_down'], 'nn', "ffn_out", BF16)
    else:
        ff, *side_out = _matmul(act, w['w_down'], 'nn', "ffn_out", BF16, side=side)
        arrived(side_out)
    h2, u3 = _rowcall("res_norm3", res_norm, [h1, ff], [_row(w['norm3_g'])], [(D_MODEL, F32), (D_MODEL, BF16)])
    pg = _matmul(u3, w['w_ple_gate'], 'nn', "ple_gate", BF16)
    pp = _matmul(p_i, w['w_ple_proj'], 'nn', "ple_proj", BF16)
    h3 = _rowcall("ple", lambda h, a, b, bias: h + _sigmoid(a.astype(F32) + bias) * b.astype(F32), [h2, pg, pp],
                  [_row(w['b_ple_gate'])],
                  [(D_MODEL, F32)])[0]
    saved = dict(h0=h0, u1=u1, proj=proj, xs_c=xs_c, dt_arr=dt_arr, ssd_in=ssd_in, states=states,
                 y_core=y_core, hseq=hseq, xl=xl, bias_cols=bias_cols, bias_rows=bias_rows, lse=lse,
                 y_fox=y_fox, post_consts=post_consts, ycat=ycat, h1=h1, u2=u2, gate_pre=gate_pre, up_pre=up_pre, act=act, h2=h2, u3=u3,
                 pg=pg, pp=pp, p_i=p_i)
    return h3, saved


def _layer_bwd(dh3, s, w, c, hooks=None, layer=0):
    nb, seq = c['nb'], c['seq']
    g = {}

    def ple_bwd(dh, a, b, bias):
        a, b = a.astype(F32), b.astype(F32)
        gate = _sigmoid(a + bias)
        dpg = dh * b * gate * (1.0 - gate)
        return dh * gate, dpg, jnp.sum(dpg, axis=0, keepdims=True)

    dpp, dpg, g['b_ple_gate'] = _rowcall("ple_bwd", ple_bwd, [dh3, s['pg'], s['pp']], [_row(w['b_ple_gate'])],
                                         [(D_MODEL, BF16), (D_MODEL, BF16)], [((1, D_MODEL), F32)])
    g['w_ple_proj'] = _matmul(s['p_i'], dpp, 'tn', "d_w_ple_proj")
    g['w_ple_gate'] = _matmul(s['u3'], dpg, 'tn', "d_w_ple_gate")
    du3 = _matmul(dpg, w['w_ple_gate'], 'nt', "d_u3", BF16)

    def norm_bwd(h, du, dh, gain):
        dx, dg = _rms_bwd(h, gain, du.astype(F32))
        dhn = dh + dx
        return dhn, dhn, dg

    dh2, dh2_b, g['norm3_g'] = _rowcall("norm3_bwd", norm_bwd, [s['h2'], du3, dh3], [_row(w['norm3_g'])],
                                        [(D_MODEL, F32), (D_MODEL, BF16)], [((1, D_MODEL), F32)])
    g['w_down'] = _matmul(s['act'], dh2_b, 'tn', "d_w_down")
    dact = _matmul(dh2_b, w['w_down'], 'nt', "d_act", BF16)

    def swiglu_bwd(gt, up, da):
        gt, up, da = gt.astype(F32), up.astype(F32), da.astype(F32)
        return jnp.concatenate([da * up * _dsilu(gt), da * _silu(gt)], axis=-1)

    dgu = _rowcall("swiglu_bwd", swiglu_bwd, [s['gate_pre'], s['up_pre'], dact], [],
                   [(2 * D_FF_P, BF16)])[0]
    gu16 = _matmul(s['u2'], dgu, 'tn', "d_w_gu", shard_n=FF_SHARD_P)
    du2 = _matmul(dgu, w['w_gu'], 'nt', "d_u2", BF16)
    dh1, dh1_b, g['norm2_g'] = _rowcall("norm2_bwd", norm_bwd, [s['h1'], du2, dh2], [_row(w['norm2_g'])],
                                        [(D_MODEL, F32), (D_MODEL, BF16)], [((1, D_MODEL), F32)])
    g['w_out'] = _matmul(s['ycat'], dh1_b, 'tn', "d_w_out")
    if hooks is None:
        dycat = _matmul(dh1_b, w['w_out'], 'nt', "d_ycat", BF16)
    else:
        ready = {n: g[n] for n in ('w_out', 'w_down', 'w_ple_gate', 'w_ple_proj')}
        ready['w_gate'], ready['w_up'] = (gu16, 0), (gu16, N_DEV)
        dycat, *side_out = _matmul(dh1_b, w['w_out'], 'nt', "d_ycat", BF16, side=hooks.sibling_side(layer, ready))
        hooks.sibling_done(side_out)

    def post_bwd(dy, yc, xs, z, hs, lg, yf, d, g1, g2, g3):
        dy = dy.astype(F32)
        sz = _silu(z)
        ytot = yc + xs * d
        dpre1, dg1 = _rms_bwd(ytot * sz, g1, dy[:, 0:384])
        dytot = dpre1 * sz
        dz = dpre1 * ytot * _dsilu(z)
        dd = jnp.sum(dytot * xs, axis=0, keepdims=True)
        gl = _gelu(lg)
        dpre2, dg2 = _rms_bwd(hs * gl, g2, dy[:, 384:640])
        dyf, dg3 = _rms_bwd(yf, g3, dy[:, 640:1024])
        return dytot, dytot * d, dz, dpre2 * gl, dpre2 * hs * _dgelu(lg), dyf, dd, dg1, dg2, dg3

    (dy_core, dxs_skip, dz, dhseq, dlg, dy_fox, dd384, g['ssd_norm_g'], g['lru_norm_g'], g['fox_norm_g']) = _rowcall(
        "mix_post_bwd", post_bwd,
        [dycat, s['y_core'], s['xs_c'], (s['proj'], 384, 3), s['hseq'], (s['proj'], 256, 3), s['y_fox']],
        s['post_consts'],
        [(384, F32), (384, F32), (384, BF16), (256, F32), (256, BF16), (384, F32)],
        [((1, 384), F32), ((1, 384), F32), ((1, 256), F32), ((1, 384), F32)])
    g['ssd_d'] = dd384.reshape(SSD_HEADS, HEAD_DIM).sum(axis=1)

    side = hooks.bwd_side() if hooks is not None else None
    dq, dk, dv, dcf_rows, dcf_cols, *side_out = _fox_bwd(s['proj'], s['y_fox'], s['lse'], dy_fox, s['bias_cols'],
                                                         s['bias_rows'], nb, seq, side)
    if hooks is not None:
        hooks.bwd_done(side_out)
    dq = dq.astype(BF16)

    dx_h, db_c, dc_c, da_arr, dend_arr, ddt_arr = _ssd_bwd(*s['ssd_in'], s['states'],
                                                           dy_core.reshape(nb, seq, SSD_WIDTH))
    dxs_c = dx_h.reshape(nb * seq, SSD_WIDTH) + dxs_skip
    dcf = dcf_rows[:, :, 0:2, :].reshape(nb, FOX_HEADS, seq).transpose(0, 2, 1)
    dcum = jnp.concatenate([da_arr[:, :, 0:SSD_HEADS], dcf,
                            jnp.zeros((nb, seq, 128 - 2 * SSD_HEADS), F32)], axis=-1).reshape(nb * seq, 128)
    proj = s['proj']
    dxs_raw, dcw_x, dcb_x = _convsilu_bwd("conv_x_bwd", (proj, 384, 4), seq, c['cw_x'], c['cb_x'], dxs_c)
    db_raw, dcw_b, dcb_b = _convsilu_bwd("conv_b_bwd", (proj, 256, 0), seq, c['cw_b'], c['cb_b'],
                                         db_c.reshape(nb * seq, 256))
    dc_raw, dcw_c, dcb_c = _convsilu_bwd("conv_c_bwd", (proj, 256, 1), seq, c['cw_c'], c['cb_c'],
                                         dc_c.reshape(nb * seq, 256))
    dsmall, dbias128, dalog128 = _small_bwd((proj, 128, 8), seq, dcum, dcf_cols, dend_arr.reshape(nb * seq, 128),
                                            ddt_arr.reshape(nb * seq, 128), s['dt_arr'], c['bias128'], c['alog128'])
    g['ssd_conv_w'] = jnp.concatenate([dcw_x, dcw_b, dcw_c], axis=1)
    g['ssd_conv_b'] = jnp.concatenate([dcb_x, dcb_b, dcb_c], axis=1).reshape(-1)
    g['ssd_dt_bias'] = dbias128[0, 0:SSD_HEADS]
    g['fox_b_f'] = dbias128[0, SSD_HEADS:2 * SSD_HEADS]
    g['ssd_a_log'] = dalog128[0, 0:SSD_HEADS]

    (dlru_raw, g['lru_conv_w'], dlcb, dwa, dba, dwx, dbx, dlam) = _lru_bwd(
        s['proj'], seq, s['xl'], s['hseq'], dhseq, c['lcw'], c['lcb'], c['wa'], c['ba'], c['wx'], c['bx'], c['lam'])
    g['lru_conv_b'], g['lru_b_a'], g['lru_b_x'], g['lru_lambda'] = (t.reshape(-1) for t in (dlcb, dba, dbx, dlam))
    g['lru_w_a'], g['lru_w_x'] = _block_diag_grad(dwa), _block_diag_grad(dwx)

    dproj = jnp.concatenate([db_raw, dc_raw, dlru_raw, dlg, dsmall, dz, dxs_raw, dq, dk, dv], axis=1)
    g['w_in'] = _matmul(s['u1'], dproj, 'tn', "d_w_in")
    if hooks is None:
        du1 = _matmul(dproj, w['w_in'], 'nt', "d_u1", BF16)
    else:
        hooks.bwd_ready(layer, {'w_in': g['w_in']})
        du1, *side_out = _matmul(dproj, w['w_in'], 'nt', "d_u1", BF16, side=hooks.bwd_side())
        hooks.bwd_done(side_out)

    def norm1_bwd(h, du, dh, gain):
        dx, dg = _rms_bwd(h, gain, du.astype(F32))
        return dh + dx, dg

    dh0, g['norm1_g'] = _rowcall("norm1_bwd", norm1_bwd, [s['h0'], du1, dh1], [_row(w['norm1_g'])],
                                 [(D_MODEL, F32)], [((1, D_MODEL), F32)])
    for name in ('b_ple_gate', 'norm3_g', 'norm2_g', 'norm1_g', 'ssd_norm_g', 'lru_norm_g', 'fox_norm_g'):
        g[name] = g[name].reshape(-1)
    g['w_gate'], g['w_up'] = None, None
    if hooks is None:
        g['w_gate'] = gu16[0:N_DEV].transpose(1, 0, 2).reshape(D_MODEL, D_FF_P)
        g['w_up'] = gu16[N_DEV:2 * N_DEV].transpose(1, 0, 2).reshape(D_MODEL, D_FF_P)
    return dh0, g


class _Hooks:
    def __init__(self, shard):
        self.shard = shard
        self.whole = {}
        self.part, self.others = {}, {}
        self.pending, self.flying = [], []

    def first(self, extra):
        got = _all_gather([self.shard['w_in', 0]] + extra, "gather_first")
        self.whole['w_in', 0] = _whole('w_in', got[0])
        return got[1:]

    def fwd_side(self, layer, stage):
        if stage == 'attention':
            self.flying = [(n, layer) for n in BIG if n != 'w_in']
        elif layer + 1 < DEPTH:
            self.flying = [('w_in', layer + 1)]
        else:
            return None
        return _spread_side([self.shard[k] for k in self.flying])

    def fwd_done(self, outs):
        if self.flying:
            passed = _run_side(_pass_side(outs), "gather_pass_%s%d" % self.flying[0], in_place=True)
            for k, arr in zip(self.flying, passed):
                self.whole[k] = _whole(k[0], arr)
            self.flying = []

    def weights(self, layer):
        w = {n: self.whole[n, layer] for n in BIG if (n, layer) in self.whole}
        if 'w_gate' in w:
            w['w_gu'] = jnp.concatenate([w['w_gate'], w['w_up']], axis=1)
        return w

    def sibling_side(self, layer, grads):
        self.sib_keys = [(n, layer) for n in grads]
        self.sib_full = [g if isinstance(g, tuple) else (_split(n, g), 0) for n, g in grads.items()]
        return _sibling_side([f for f, _ in self.sib_full], [off for _, off in self.sib_full])

    def sibling_done(self, got):
        for k, (f, off), r in zip(self.sib_keys, self.sib_full, got):
            self.part[k] = _pair_sum(f, off, r, "rs_pair_sum_%s%d" % k)
        self.pending += self.sib_keys

    def bwd_ready(self, layer, grads):
        side = self.sibling_side(layer, grads)
        self.sibling_done(_run_side(side, "rs_sibling_%s%d" % self.sib_keys[0]))

    def bwd_side(self):
        self.flying, self.pending = self.pending, []
        return _chip_side([self.part[k] for k in self.flying]) if self.flying else None

    def bwd_done(self, outs):
        for k, o in zip(self.flying, outs):
            self.others[k] = o
        self.flying = []

    def flush(self):
        side = self.bwd_side()
        if side is not None:
            self.bwd_done(_run_side(side, "rs_chips_last"))


def _local_step(x, p, target, big, small, hooks=None):
    nb, seq, _ = x.shape
    tokens = nb * seq
    h = x.reshape(tokens, D_MODEL)
    layers, saves = [], []
    for i in range(DEPTH):
        w = {name: small[name][i] for name in small if name != 'final_norm_g'}
        if hooks is not None:
            w.update(hooks.weights(i))
        else:
            for name in ('w_in', 'w_out', 'w_down', 'w_ple_gate', 'w_ple_proj'):
                w[name] = big[name][i]
            w['w_gu'] = jnp.concatenate([big['w_gate'][i], big['w_up'][i]], axis=1)
        c = _layer_consts(w)
        c['nb'], c['seq'] = nb, seq
        h, s = _layer_fwd(h, p[i].reshape(tokens, PLE_DIM).astype(BF16), w, c, hooks, i)
        layers.append((w, c))
        saves.append(s)

    def head(hf, tgt, gain):
        r = lax.rsqrt(jnp.mean(hf * hf, axis=-1, keepdims=True) + EPS)
        xhat = hf * r
        err = xhat * gain - tgt
        loss = 0.5 * jnp.sum(jnp.mean(err * err, axis=-1, keepdims=True), axis=0, keepdims=True)
        dy = err * (1.0 / D_MODEL)
        dg = jnp.sum(dy * xhat, axis=0, keepdims=True)
        dxhat = dy * gain
        dh = r * (dxhat - xhat * jnp.mean(dxhat * xhat, axis=-1, keepdims=True))
        return dh, jnp.broadcast_to(loss, (1, 128)), dg

    dh, loss128, dgf = _rowcall("loss_head", head, [h, target.reshape(tokens, D_MODEL)],
                                [_row(small['final_norm_g'])], [(D_MODEL, F32)],
                                [((1, 128), F32), ((1, D_MODEL), F32)])
    grads = {'final_norm_g': dgf.reshape(-1)}
    per_layer = [None] * DEPTH
    for i in range(DEPTH - 1, -1, -1):
        w, c = layers[i]
        dh, per_layer[i] = _layer_bwd(dh, saves[i], w, c, hooks, i)
    for name in per_layer[0]:
        if name in BIG:
            grads[name] = [per_layer[i][name] for i in range(DEPTH)]
        else:
            grads[name] = jnp.stack([per_layer[i][name] for i in range(DEPTH)])
    return loss128[0, 0], dh.reshape(nb, seq, D_MODEL), grads


def kernel(x, p, norm1_g, w_in, ssd_conv_w, ssd_conv_b, ssd_dt_bias, ssd_a_log, ssd_d, ssd_norm_g, lru_conv_w, lru_conv_b, lru_w_a, lru_b_a, lru_w_x, lru_b_x, lru_lambda, lru_norm_g, fox_b_f, fox_norm_g, w_out, norm2_g, w_gate, w_up, w_down, norm3_g, w_ple_gate, b_ple_gate, w_ple_proj, final_norm_g, loss_target, m_norm1_g, m_w_in, m_ssd_conv_w, m_ssd_conv_b, m_ssd_dt_bias, m_ssd_a_log, m_ssd_d, m_ssd_norm_g, m_lru_conv_w, m_lru_conv_b, m_lru_w_a, m_lru_b_a, m_lru_w_x, m_lru_b_x, m_lru_lambda, m_lru_norm_g, m_fox_b_f, m_fox_norm_g, m_w_out, m_norm2_g, m_w_gate, m_w_up, m_w_down, m_norm3_g, m_w_ple_gate, m_b_ple_gate, m_w_ple_proj, m_final_norm_g, v_norm1_g, v_w_in, v_ssd_conv_w, v_ssd_conv_b, v_ssd_dt_bias, v_ssd_a_log, v_ssd_d, v_ssd_norm_g, v_lru_conv_w, v_lru_conv_b, v_lru_w_a, v_lru_b_a, v_lru_w_x, v_lru_b_x, v_lru_lambda, v_lru_norm_g, v_fox_b_f, v_fox_norm_g, v_w_out, v_norm2_g, v_w_gate, v_w_up, v_w_down, v_norm3_g, v_w_ple_gate, v_b_ple_gate, v_w_ple_proj, v_final_norm_g):
    args = dict(locals())
    w_loc = {n: args[n] for n in WEIGHTS}
    m_loc = {n: args['m_' + n] for n in WEIGHTS}
    v_loc = {n: args['v_' + n] for n in WEIGHTS}
    dev = 4 * lax.axis_index("x") + 2 * lax.axis_index("y") + lax.axis_index("c")

    keys = [(n, i) for n in BIG for i in range(DEPTH)]
    conv_names = list(CONV_SHARDED)
    conv_loc_shapes = [w_loc[n].shape for n in conv_names]
    hooks = _Hooks({(n, i): _to_shard(n, w_loc[n][i]).astype(BF16) for n, i in keys})
    conv_all, = hooks.first([_pack_list([w_loc[n] for n in conv_names], 8)])
    small = {n: w_loc[n] for n in WEIGHTS if n not in BIG and n not in CONV_SHARDED}
    per_dev = [_unpack_list(conv_all[j], conv_loc_shapes) for j in range(N_DEV)]
    for idx, n in enumerate(conv_names):
        small[n] = jnp.concatenate([per_dev[j][idx] for j in range(N_DEV)], axis=2)

    loss_part, dx, grads = _local_step(x, p, loss_target, None, small, hooks)
    loss = lax.psum(loss_part, ("x", "y", "c"))
    hooks.flush()
    out = {kind: {} for kind in ('g', 'delta', 'm', 'v')}
    for n in BIG:
        g_nat = jnp.stack([_from_shard(n, _chip_sum(hooks.part[n, i], hooks.others[n, i], "rs_chip_sum_%s%d" % (n, i)))
                           for i in range(DEPTH)])
        shape = w_loc[n].shape
        flat = [d.reshape(-1, shape[-1]) for d in (w_loc[n], g_nat, m_loc[n], v_loc[n])]
        out['g'][n] = g_nat
        for kind, r in zip(('delta', 'm', 'v'), _adamw(*flat, name="adamw_" + n)):
            out[kind][n] = r.reshape(shape)

    small_names = [n for n in WEIGHTS if n not in BIG]
    small_shapes = [grads[n].shape for n in small_names]
    g_small = dict(zip(small_names, _unpack_list(
        _all_reduce_small(_pack_list([grads[n] for n in small_names], 8)), small_shapes)))
    for n in CONV_SHARDED:
        width = CONV_SHARDED[n][2] // N_DEV
        g_small[n] = lax.dynamic_slice_in_dim(g_small[n], dev * width, width, axis=2)
    shapes = [w_loc[n].shape for n in small_names]
    packed = [_pack_list([d[n] for n in small_names], 8) for d in (w_loc, g_small, m_loc, v_loc)]
    upd = [dict(zip(small_names, _unpack_list(t, shapes))) for t in _adamw(*packed)]
    for kind, d in zip(('g', 'delta', 'm', 'v'), [g_small] + upd):
        for n in small_names:
            out[kind][n] = d[n]
    return (loss, dx, *[out['g'][n] for n in WEIGHTS], *[out['delta'][n] for n in WEIGHTS],
            *[out['m'][n] for n in WEIGHTS], *[out['v'][n] for n in WEIGHTS])
```

```python
import functools
import math

import jax
import jax.numpy as jnp
from jax import lax
from jax.experimental import pallas as pl
from jax.experimental.pallas import tpu as pltpu

F32 = jnp.float32
BF16 = jnp.bfloat16

N_DEV = 8
D_MODEL = 1024
DEPTH = 2
HEAD_DIM = 64
SSD_WIDTH = 384
LRU_WIDTH = 256
FOX_WIDTH = 384
SSD_HEADS = 6
SSD_STATE = 128
CHUNK = 256
FOX_HEADS = 6
D_FF = 2816
FF_SHARD = D_FF // N_DEV
FF_SHARD_P = 384
D_FF_P = N_DEV * FF_SHARD_P
PLE_DIM = 256
IN_COLS = 2956
PROJ_COLS = 3072
LRU_C = 8.0
EPS = 1e-6
NEG = -1e30

ADAM_LR = 0.001
ADAM_B1 = 0.9
ADAM_B2 = 0.999
ADAM_EPS = 1e-08
ADAM_WD = 0.01
ADAM_STEP = 10

VMEM_LIMIT = 56 * 1024 * 1024

WEIGHTS = ['norm1_g', 'w_in', 'ssd_conv_w', 'ssd_conv_b', 'ssd_dt_bias', 'ssd_a_log', 'ssd_d', 'ssd_norm_g',
           'lru_conv_w', 'lru_conv_b', 'lru_w_a', 'lru_b_a', 'lru_w_x', 'lru_b_x', 'lru_lambda', 'lru_norm_g',
           'fox_b_f', 'fox_norm_g', 'w_out', 'norm2_g', 'w_gate', 'w_up', 'w_down', 'norm3_g', 'w_ple_gate',
           'b_ple_gate', 'w_ple_proj', 'final_norm_g']
BIG = {'w_in': (1, (DEPTH, D_MODEL, IN_COLS)), 'w_out': (1, (DEPTH, D_MODEL, D_MODEL)),
       'w_gate': (2, (DEPTH, D_MODEL, D_FF)), 'w_up': (2, (DEPTH, D_MODEL, D_FF)),
       'w_down': (1, (DEPTH, D_FF, D_MODEL)), 'w_ple_gate': (1, (DEPTH, D_MODEL, D_MODEL)),
       'w_ple_proj': (2, (DEPTH, PLE_DIM, D_MODEL))}
CONV_SHARDED = {'ssd_conv_w': (DEPTH, 4, 896), 'lru_conv_w': (DEPTH, 4, 256)}


def _dot(a, b):
    return jnp.dot(a, b, preferred_element_type=F32)


def _dot_nt(a, b):
    return lax.dot_general(a, b, (((1,), (1,)), ((), ())), preferred_element_type=F32)


def _dot_tn(a, b):
    return lax.dot_general(a, b, (((0,), (0,)), ((), ())), preferred_element_type=F32)


def _params(sem):
    return pltpu.CompilerParams(dimension_semantics=sem, vmem_limit_bytes=VMEM_LIMIT)


def _pick_tile(n, cap):
    if n <= cap:
        return n
    best = 128
    for t in range(128, cap + 1, 128):
        if n % t == 0:
            best = t
    assert n % best == 0, (n, cap)
    return best


def _matmul(a, b, mode, name, out_dtype=F32, side=None, shard_n=None):
    if mode == 'tn':
        k_dim, m_dim = a.shape
        n_dim = b.shape[1]
    else:
        m_dim, k_dim = a.shape
        n_dim = b.shape[1] if mode == 'nn' else b.shape[0]
    tm = _pick_tile(m_dim, 512 if mode != 'tn' else 1024)
    tn = 2 * shard_n if shard_n else _pick_tile(n_dim, 1536 if mode != 'tn' else 1024)
    tk = _pick_tile(k_dim, 3072 if mode != 'tn' else 2048)
    nk = k_dim // tk
    grid = (n_dim // tn, m_dim // tm, nk)

    n_in = len(side.inputs) if side else 0
    n_out = len(side.out_shape) if side else 0

    n_acc = 1 if nk > 1 else 0

    def body(*refs):
        a_ref, b_ref, o_ref = refs[0], refs[1], refs[2 + n_in]
        acc_ref = refs[3 + n_in + n_out] if n_acc else None
        if side is not None:
            side.run(refs[2:2 + n_in] + refs[3 + n_in:3 + n_in + n_out] + refs[3 + n_acc + n_in + n_out:],
                     *_grid_ends(grid))
        kk = pl.program_id(2)
        prod = {'nn': _dot, 'nt': _dot_nt, 'tn': _dot_tn}[mode](a_ref[...], b_ref[...])

        def write(res):
            if shard_n:
                for q in range(tn // shard_n):
                    o_ref[q] = res[:, q * shard_n:(q + 1) * shard_n].astype(o_ref.dtype)
            else:
                o_ref[...] = res.astype(o_ref.dtype)

        if nk == 1:
            write(prod)
            return

        @pl.when(kk == 0)
        def _():
            acc_ref[...] = prod

        @pl.when(jnp.logical_and(kk > 0, kk < nk - 1))
        def _():
            acc_ref[...] += prod

        @pl.when(kk == nk - 1)
        def _():
            write(acc_ref[...] + prod)

    if mode == 'nn':
        a_spec = pl.BlockSpec((tm, tk), lambda j, i, k: (i, k))
        b_spec = pl.BlockSpec((tk, tn), lambda j, i, k: (k, j))
    elif mode == 'nt':
        a_spec = pl.BlockSpec((tm, tk), lambda j, i, k: (i, k))
        b_spec = pl.BlockSpec((tn, tk), lambda j, i, k: (j, k))
    else:
        a_spec = pl.BlockSpec((tk, tm), lambda j, i, k: (k, i))
        b_spec = pl.BlockSpec((tk, tn), lambda j, i, k: (k, j))
    side_in, side_out = side.specs() if side else ([], [])
    if shard_n:
        out_shape = jax.ShapeDtypeStruct((n_dim // shard_n, m_dim, shard_n), out_dtype)
        out_spec = pl.BlockSpec((tn // shard_n, tm, shard_n), lambda j, i, k: (j, i, 0))
    else:
        out_shape = jax.ShapeDtypeStruct((m_dim, n_dim), out_dtype)
        out_spec = pl.BlockSpec((tm, tn), lambda j, i, k: (i, j))
    res = pl.pallas_call(
        body, name=name, grid=grid,
        out_shape=[out_shape] + (side.out_shape if side else []),
        in_specs=[a_spec, b_spec] + side_in,
        out_specs=[out_spec] + side_out,
        scratch_shapes=[pltpu.VMEM((tm, tn), F32)] * n_acc + (side.sems if side else []),
        compiler_params=_params(("arbitrary", "arbitrary", "arbitrary") if side
                                else ("parallel", "parallel", "arbitrary")),
    )(a, b, *(side.inputs if side else []))
    return res if side else res[0]


def _ffn_in(u, w_gu):
    tokens, k_dim = u.shape
    width = w_gu.shape[1] // 2
    tm, tn = _pick_tile(tokens, 512), _pick_tile(width, 768)
    nj = width // tn

    def body(a_ref, bg_ref, bu_ref, g_ref, u_ref, act_ref):
        a = a_ref[...]
        g = _dot(a, bg_ref[...]).astype(BF16)
        up = _dot(a, bu_ref[...]).astype(BF16)
        g_ref[...] = g
        u_ref[...] = up
        act_ref[...] = (_silu(g.astype(F32)) * up.astype(F32)).astype(BF16)

    out = jax.ShapeDtypeStruct((tokens, width), BF16)
    tile = pl.BlockSpec((tm, tn), lambda j, i: (i, j))
    return pl.pallas_call(
        body, name="ffn_in", grid=(nj, tokens // tm),
        out_shape=[out, out, out],
        in_specs=[pl.BlockSpec((tm, k_dim), lambda j, i: (i, 0)),
                  pl.BlockSpec((k_dim, tn), lambda j, i: (0, j)),
                  pl.BlockSpec((k_dim, tn), lambda j, i: (0, j + nj))],
        out_specs=[tile, tile, tile],
        compiler_params=_params(("parallel", "parallel")),
    )(u, w_gu, w_gu)


def _rowcall(name, fn, tiled, consts, outs, accs=(), tile=512, scratch=()):
    specs, arrays = [], []
    for t in tiled:
        if isinstance(t, tuple):
            arr, width, blk = t
            specs.append(pl.BlockSpec((tile, width), functools.partial(lambda i, blk: (i, blk), blk=blk)))
        else:
            arr = t
            specs.append(pl.BlockSpec((tile, arr.shape[1]), lambda i: (i, 0)))
        arrays.append(arr)
    rows = arrays[0].shape[0]
    assert rows % tile == 0, (name, rows, tile)
    for c in consts:
        specs.append(pl.BlockSpec(c.shape, lambda i: (0, 0)))
        arrays.append(c)
    n_in, n_out, n_acc = len(arrays), len(outs), len(accs)
    out_shape = [jax.ShapeDtypeStruct((rows, c), dt) for c, dt in outs]
    out_specs = [pl.BlockSpec((tile, c), lambda i: (i, 0)) for c, _ in outs]
    out_shape += [jax.ShapeDtypeStruct(s, dt) for s, dt in accs]
    out_specs += [pl.BlockSpec(s, lambda i: (0, 0)) for s, _ in accs]

    def body(*refs):
        ins = [r[...] for r in refs[:n_in]]
        out_refs = refs[n_in:n_in + n_out]
        acc_refs = refs[n_in + n_out:n_in + n_out + n_acc]
        scr = refs[n_in + n_out + n_acc:]
        res = fn(*ins, *scr)
        if not isinstance(res, (tuple, list)):
            res = (res,)
        assert len(res) == n_out + n_acc, (name, len(res))
        for r, v in zip(out_refs, res[:n_out]):
            r[...] = v.astype(r.dtype)
        if n_acc:
            first = pl.program_id(0) == 0

            @pl.when(first)
            def _():
                for r, v in zip(acc_refs, res[n_out:]):
                    r[...] = v.astype(r.dtype)

            @pl.when(jnp.logical_not(first))
            def _():
                for r, v in zip(acc_refs, res[n_out:]):
                    r[...] += v.astype(r.dtype)

    res = pl.pallas_call(
        body, name=name, grid=(rows // tile,),
        out_shape=out_shape, in_specs=specs, out_specs=out_specs,
        scratch_shapes=list(scratch),
        compiler_params=_params(("arbitrary",)),
    )(*arrays)
    return res


def _sigmoid(x):
    return 1.0 / (1.0 + jnp.exp(-x))


def _softplus(x):
    return jnp.maximum(x, 0.0) + jnp.log(1.0 + jnp.exp(-jnp.abs(x)))


def _silu(x):
    return x * _sigmoid(x)


def _dsilu(x):
    s = _sigmoid(x)
    return s * (1.0 + x * (1.0 - s))


_GELU_C = math.sqrt(2.0 / math.pi)


def _gelu(x):
    return 0.5 * x * (1.0 + jnp.tanh(_GELU_C * (x + 0.044715 * x * x * x)))


def _dgelu(x):
    t = jnp.tanh(_GELU_C * (x + 0.044715 * x * x * x))
    return 0.5 * (1.0 + t) + 0.5 * x * (1.0 - t * t) * _GELU_C * (1.0 + 3.0 * 0.044715 * x * x)


def _neg_expm1(x):
    series = -x * (1.0 + x * (0.5 + x * (1.0 / 6.0 + x * (1.0 / 24.0 + x * (1.0 / 120.0)))))
    return jnp.where(x > -0.03, series, 1.0 - jnp.exp(x))


def _rms(x, g):
    r = lax.rsqrt(jnp.mean(x * x, axis=-1, keepdims=True) + EPS)
    return x * r * g


def _rms_bwd(x, g, dy):
    r = lax.rsqrt(jnp.mean(x * x, axis=-1, keepdims=True) + EPS)
    xhat = x * r
    dg = jnp.sum(dy * xhat, axis=0, keepdims=True)
    dxhat = dy * g
    dx = r * (dxhat - xhat * jnp.mean(dxhat * xhat, axis=-1, keepdims=True))
    return dx, dg


def _row_iota(shape):
    return lax.broadcasted_iota(jnp.int32, shape, 0)


def _shift_down(x, j):
    if j == 0:
        return x
    return jnp.where(_row_iota(x.shape) >= j, pltpu.roll(x, j, 0), 0.0)


def _shift_up(x, j):
    if j == 0:
        return x
    n = x.shape[0]
    return jnp.where(_row_iota(x.shape) < n - j, pltpu.roll(x, n - j, 0), 0.0)


def _conv(x, w, b):
    y = b + w[3:4, :] * x
    for k in range(3):
        y = y + w[k:k + 1, :] * _shift_down(x, 3 - k)
    return y


def _conv_bwd(x, w, dy):
    dx = w[3:4, :] * dy
    dws = []
    for k in range(3):
        dx = dx + w[k:k + 1, :] * _shift_up(dy, 3 - k)
        dws.append(jnp.sum(dy * _shift_down(x, 3 - k), axis=0, keepdims=True))
    dws.append(jnp.sum(dy * x, axis=0, keepdims=True))
    return dx, jnp.concatenate(dws, axis=0), jnp.sum(dy, axis=0, keepdims=True)


def _split3(x):
    hi = x.astype(BF16)
    r1 = x - hi.astype(F32)
    mid = r1.astype(BF16)
    lo = (r1 - mid.astype(F32)).astype(BF16)
    return hi, mid, lo


def _tri_dot(tri, x):
    hi, mid, lo = _split3(x)
    return _dot(tri, hi) + _dot(tri, mid) + _dot(tri, lo)


def _cumsum_rows(x):
    n = x.shape[0] // CHUNK
    r = lax.broadcasted_iota(jnp.int32, (CHUNK, CHUNK), 0)
    c = lax.broadcasted_iota(jnp.int32, (CHUNK, CHUNK), 1)
    tri = (r >= c).astype(BF16)
    carry = jnp.zeros((1, x.shape[1]), F32)
    cums, prevs, ends = [], [], []
    for i in range(n):
        blk = _tri_dot(tri, x[i * CHUNK:(i + 1) * CHUNK]) + carry
        prevs.append(jnp.broadcast_to(carry, blk.shape))
        carry = blk[CHUNK - 1:CHUNK, :]
        ends.append(jnp.broadcast_to(carry, blk.shape))
        cums.append(blk)
    return jnp.concatenate(cums, 0), jnp.concatenate(prevs, 0), jnp.concatenate(ends, 0)


def _rev_cumsum_rows(x):
    n = x.shape[0] // CHUNK
    r = lax.broadcasted_iota(jnp.int32, (CHUNK, CHUNK), 0)
    c = lax.broadcasted_iota(jnp.int32, (CHUNK, CHUNK), 1)
    tri = (r <= c).astype(BF16)
    carry = jnp.zeros((1, x.shape[1]), F32)
    local, whole = [None] * n, [None] * n
    for i in range(n - 1, -1, -1):
        local[i] = _tri_dot(tri, x[i * CHUNK:(i + 1) * CHUNK])
        whole[i] = local[i] + carry
        carry = whole[i][0:1, :]
    return jnp.concatenate(local, 0), jnp.concatenate(whole, 0)


def _convsilu_fwd(name, seg, seq, w, b, dtype):
    return _rowcall(name, lambda raw, w, b: _silu(_conv(raw, w, b)), [seg], [w, b], [(seg[1], dtype)], tile=seq)[0]


def _convsilu_bwd(name, seg, seq, w, b, dy):
    def fn(raw, dy, w, b):
        return _conv_bwd(raw, w, dy * _dsilu(_conv(raw, w, b)))

    width = seg[1]
    return _rowcall(name, fn, [seg, dy], [w, b], [(width, BF16)], [((4, width), F32), ((1, width), F32)], tile=seq)


def _small_fwd(seg, seq, bias128, alog128):
    def fn(small, bias, alog):
        lane = lax.broadcasted_iota(jnp.int32, small.shape, 1)
        a = -jnp.exp(alog)
        s = small + bias
        dt = _softplus(s)
        logf = -_softplus(-s)
        pre = jnp.where(lane < SSD_HEADS, a * dt, jnp.where(lane < 2 * SSD_HEADS, logf, 0.0))
        cum, prev, end = _cumsum_rows(pre)
        return dt, cum, prev, end

    return _rowcall("small_fwd", fn, [seg], [bias128, alog128], [(128, F32)] * 4, tile=seq)


def _small_bwd(seg, seq, dcum, dcq, dend, ddt, dt_arr, bias128, alog128):
    def fn(small, dcum, dcq, dend, ddt, dt_arr, bias, alog):
        lane = lax.broadcasted_iota(jnp.int32, small.shape, 1)
        for pair in range(FOX_PAIRS):
            first = SSD_HEADS + 2 * pair
            moved = pltpu.roll(dcq[:, pair * 128:(pair + 1) * 128], first, 1)
            dcum = dcum + jnp.where(jnp.logical_or(lane == first, lane == first + 1), moved, 0.0)
        a = -jnp.exp(alog)
        sig = _sigmoid(small + bias)
        local, whole = _rev_cumsum_rows(dcum)
        dadt = local + dend
        d_dt = ddt + a * dadt
        ds = jnp.where(lane < SSD_HEADS, d_dt * sig, jnp.where(lane < 2 * SSD_HEADS, whole * (1.0 - sig), 0.0))
        da = jnp.sum(jnp.where(lane < SSD_HEADS, dadt * dt_arr, 0.0), axis=0, keepdims=True)
        return ds, jnp.sum(ds, axis=0, keepdims=True), da * a

    return _rowcall("small_bwd", fn, [seg, dcum, dcq, dend, ddt, dt_arr], [bias128, alog128], [(128, BF16)],
                    [((1, 128), F32), ((1, 128), F32)], tile=seq)


HEAD_PAIRS = SSD_HEADS // 2


def _ssd_specs(nc, reverse):
    def at(c):
        return nc - 1 - c if reverse else c

    x_spec = pl.BlockSpec((1, CHUNK, SSD_WIDTH), lambda b, c: (b, at(c), 0))
    bc_spec = pl.BlockSpec((1, CHUNK, 256), lambda b, c: (b, at(c), 0))
    col_spec = pl.BlockSpec((1, CHUNK, 128), lambda b, c: (b, at(c), 0))
    row_spec = pl.BlockSpec((1, 8, CHUNK), lambda b, c: (b, 0, at(c)))
    st_spec = pl.BlockSpec((1, 1, HEAD_PAIRS, SSD_STATE, 128), lambda b, c: (b, at(c), 0, 0, 0))
    return x_spec, bc_spec, col_spec, row_spec, st_spec


def _ssd_head(h, dtb, acb, apb, aeb, arb):
    return dtb[:, h:h + 1], acb[:, h:h + 1], apb[:, h:h + 1], aeb[:, h:h + 1], arb[h:h + 1, :]


def _ssd_fwd(x, bm, cm, dt_arr, cum, prev, end, a_row):
    nb, seq, _ = x.shape
    nc = seq // CHUNK
    x_spec, bc_spec, col_spec, row_spec, st_spec = _ssd_specs(nc, False)

    def body(x_ref, b_ref, c_ref, dt_ref, ac_ref, ap_ref, ae_ref, ar_ref, y_ref, st_ref, s_scr):
        @pl.when(pl.program_id(1) == 0)
        def _():
            s_scr[...] = jnp.zeros_like(s_scr)

        causal = (lax.broadcasted_iota(jnp.int32, (CHUNK, CHUNK), 0)
                  >= lax.broadcasted_iota(jnp.int32, (CHUNK, CHUNK), 1))
        low = lax.broadcasted_iota(jnp.int32, (CHUNK, 128), 1) < HEAD_DIM
        cols = (dt_ref[0], ac_ref[0], ap_ref[0], ae_ref[0], ar_ref[0])
        bcs = [b_ref[0, :, g * 128:(g + 1) * 128] for g in range(2)]
        ccs = [c_ref[0, :, g * 128:(g + 1) * 128] for g in range(2)]
        ms = [_dot_nt(ccs[g], bcs[g]) for g in range(2)]
        for pi in range(HEAD_PAIRS):
            x2 = x_ref[0, :, pi * 128:(pi + 1) * 128]
            dt2 = jnp.where(low, cols[0][:, 2 * pi:2 * pi + 1], cols[0][:, 2 * pi + 1:2 * pi + 2])
            xdt_f = x2 * dt2
            xdt = xdt_f.astype(BF16)
            sprev = s_scr[pi]
            st_ref[0, 0, pi] = sprev
            spb = sprev.astype(BF16)
            ys, us = [], []
            for h in (2 * pi, 2 * pi + 1):
                g = h // 3
                _, ac, ap, ae, ar = _ssd_head(h, *cols)
                lm = jnp.exp(jnp.where(causal, ac - ar, NEG))
                gm = (ms[g] * lm).astype(BF16)
                ys.append(_dot(gm, xdt) + jnp.exp(ac - ap) * _dot(ccs[g], spb))
                us.append(jnp.exp(ae[0:1, :] - ap[0:1, :]) * sprev
                          + _dot_tn(bcs[g], (xdt_f * jnp.exp(ae - ac)).astype(BF16)))
            y_ref[0, :, pi * 128:(pi + 1) * 128] = jnp.where(low, ys[0], ys[1])
            s_scr[pi] = jnp.where(lax.broadcasted_iota(jnp.int32, (SSD_STATE, 128), 1) < HEAD_DIM, us[0], us[1])

    return pl.pallas_call(
        body, name="ssd_fwd", grid=(nb, nc),
        out_shape=[jax.ShapeDtypeStruct(x.shape, F32),
                   jax.ShapeDtypeStruct((nb, nc, HEAD_PAIRS, SSD_STATE, 128), F32)],
        in_specs=[x_spec, bc_spec, bc_spec, col_spec, col_spec, col_spec, col_spec, row_spec],
        out_specs=[x_spec, st_spec],
        scratch_shapes=[pltpu.VMEM((HEAD_PAIRS, SSD_STATE, 128), F32)],
        compiler_params=_params(("parallel", "arbitrary")),
    )(x, bm, cm, dt_arr, cum, prev, end, a_row)


def _ssd_bwd(x_h, bm, cm, dt_arr, cum, prev, end, a_row, states, dy_h):
    nb, seq, _ = x_h.shape
    nc = seq // CHUNK
    x_spec, bc_spec, col_spec, row_spec, st_spec = _ssd_specs(nc, True)

    def body(x_ref, b_ref, c_ref, dt_ref, ac_ref, ap_ref, ae_ref, ar_ref, st_ref, dy_ref,
             dx_ref, db_ref, dc_ref, da_ref, dend_ref, ddt_ref, ds_scr):
        @pl.when(pl.program_id(1) == 0)
        def _():
            ds_scr[...] = jnp.zeros_like(ds_scr)

        causal = (lax.broadcasted_iota(jnp.int32, (CHUNK, CHUNK), 0)
                  >= lax.broadcasted_iota(jnp.int32, (CHUNK, CHUNK), 1))
        lane = lax.broadcasted_iota(jnp.int32, (CHUNK, 128), 1)
        low = lane < HEAD_DIM
        low_state = lax.broadcasted_iota(jnp.int32, (SSD_STATE, 128), 1) < HEAD_DIM
        cols = (dt_ref[0], ac_ref[0], ap_ref[0], ae_ref[0], ar_ref[0])
        bcs = [b_ref[0, :, g * 128:(g + 1) * 128] for g in range(2)]
        ccs = [c_ref[0, :, g * 128:(g + 1) * 128] for g in range(2)]
        ms = [_dot_nt(ccs[g], bcs[g]) for g in range(2)]
        dms = [jnp.zeros((CHUNK, CHUNK), F32) for _ in range(2)]
        dc_accs = [jnp.zeros((CHUNK, SSD_STATE), F32) for _ in range(2)]
        db_accs = [jnp.zeros((CHUNK, SSD_STATE), F32) for _ in range(2)]
        da_blk = jnp.zeros((CHUNK, 128), F32)
        dend_blk = jnp.zeros((CHUNK, 128), F32)
        ddt_blk = jnp.zeros((CHUNK, 128), F32)
        for pi in range(HEAD_PAIRS):
            x2 = x_ref[0, :, pi * 128:(pi + 1) * 128]
            dy2 = dy_ref[0, :, pi * 128:(pi + 1) * 128]
            dt2 = jnp.where(low, cols[0][:, 2 * pi:2 * pi + 1], cols[0][:, 2 * pi + 1:2 * pi + 2])
            xdt_f = x2 * dt2
            xdt = xdt_f.astype(BF16)
            dyb = dy2.astype(BF16)
            dsn = ds_scr[pi]
            dsb = dsn.astype(BF16)
            sprev_f = st_ref[0, 0, pi]
            sprev = sprev_f.astype(BF16)
            dxdts, dss = [], []
            for h in (2 * pi, 2 * pi + 1):
                g = h // 3
                mine = low if h % 2 == 0 else jnp.logical_not(low)
                _, ac, ap, ae, ar = _ssd_head(h, *cols)
                bc, cc, m = bcs[g], ccs[g], ms[g]
                lm = jnp.exp(jnp.where(causal, ac - ar, NEG))
                gm = (m * lm).astype(BF16)
                dy_m = jnp.where(mine, dy2, 0.0)
                dyb_m = dy_m.astype(BF16)
                xdt_m = jnp.where(mine, xdt_f, 0.0)
                e_in = jnp.exp(ac - ap)
                f_out = jnp.exp(ae - ac)
                whole = jnp.exp(ae[0:1, :] - ap[0:1, :])
                dg = _dot_nt(dyb_m, xdt)
                dxdt_off = f_out * _dot(bc, dsb)
                dxdt = _dot_tn(gm, dyb) + dxdt_off
                dmj = dg * lm
                dms[g] = dms[g] + dmj
                dc_accs[g] = dc_accs[g] + e_in * _dot_nt(dyb_m, sprev)
                db_accs[g] = db_accs[g] + f_out * _dot_nt(xdt_m.astype(BF16), dsb)
                dss.append(whole * dsn + _dot_tn(cc, (dy2 * e_in).astype(BF16)))
                wmat = dmj * m
                r_in = jnp.sum(dy_m * (e_in * _dot(cc, sprev)), axis=1, keepdims=True)
                q_out = jnp.sum(xdt_m * dxdt_off, axis=1, keepdims=True)
                daj = (jnp.sum(wmat, axis=1, keepdims=True) - jnp.sum(wmat.T, axis=1, keepdims=True)
                       + r_in - q_out)
                cross = jnp.where(low_state if h % 2 == 0 else jnp.logical_not(low_state), dsn * sprev_f, 0.0)
                dendj = (jnp.sum(q_out, axis=0, keepdims=True)
                         + whole * jnp.sum(jnp.sum(cross, axis=1, keepdims=True), axis=0, keepdims=True))
                ddtj = jnp.sum(jnp.where(mine, dxdt * x2, 0.0), axis=1, keepdims=True)
                dxdts.append(dxdt)
                da_blk = jnp.where(lane == h, daj, da_blk)
                dend_blk = jnp.where(lane == h, dendj, dend_blk)
                ddt_blk = jnp.where(lane == h, ddtj, ddt_blk)
            dx_ref[0, :, pi * 128:(pi + 1) * 128] = jnp.where(low, dxdts[0], dxdts[1]) * dt2
            ds_scr[pi] = jnp.where(low_state, dss[0], dss[1])
        for g in range(2):
            dmb = dms[g].astype(BF16)
            dc_ref[0, :, g * 128:(g + 1) * 128] = dc_accs[g] + _dot(dmb, bcs[g])
            db_ref[0, :, g * 128:(g + 1) * 128] = db_accs[g] + _dot_tn(dmb, ccs[g])
        da_ref[0] = da_blk
        dend_ref[0] = dend_blk
        ddt_ref[0] = ddt_blk

    col_shape = jax.ShapeDtypeStruct((nb, seq, 128), F32)
    return pl.pallas_call(
        body, name="ssd_bwd", grid=(nb, nc),
        out_shape=[jax.ShapeDtypeStruct(x_h.shape, F32),
                   jax.ShapeDtypeStruct((nb, seq, 256), F32), jax.ShapeDtypeStruct((nb, seq, 256), F32),
                   col_shape, col_shape, col_shape],
        in_specs=[x_spec, bc_spec, bc_spec, col_spec, col_spec, col_spec, col_spec, row_spec, st_spec, x_spec],
        out_specs=[x_spec, bc_spec, bc_spec, col_spec, col_spec, col_spec],
        scratch_shapes=[pltpu.VMEM((HEAD_PAIRS, SSD_STATE, 128), F32)],
        compiler_params=_params(("parallel", "arbitrary")),
    )(x_h, bm, cm, dt_arr, cum, prev, end, a_row, states, dy_h)


def _lru_gates(xl, wa, ba, wx, bx, lam):
    xb = xl.astype(BF16)
    r = _sigmoid(_dot(xb, wa) + ba)
    i = _sigmoid(_dot(xb, wx) + bx)
    sp = _softplus(-lam)
    log_a = -LRU_C * r * sp
    a = jnp.exp(log_a)
    mult = jnp.sqrt(_neg_expm1(2.0 * log_a))
    return r, i, sp, log_a, a, mult


def _scan_chunks(a_ref, u_ref, h_ref, seq, reverse):
    nc = seq // CHUNK
    width = a_ref.shape[1]
    row = lax.broadcasted_iota(jnp.int32, (CHUNK, width), 0)

    def chunk(ci, carry):
        c = nc - 1 - ci if reverse else ci
        rows = pl.ds(pl.multiple_of(c * CHUNK, CHUNK), CHUNK)
        av, bv = a_ref[rows, :], u_ref[rows, :]
        d = 1
        while d < CHUNK:
            if reverse:
                keep = row < CHUNK - d
                a_sh = jnp.where(keep, pltpu.roll(av, CHUNK - d, 0), 1.0)
                b_sh = jnp.where(keep, pltpu.roll(bv, CHUNK - d, 0), 0.0)
            else:
                keep = row >= d
                a_sh = jnp.where(keep, pltpu.roll(av, d, 0), 1.0)
                b_sh = jnp.where(keep, pltpu.roll(bv, d, 0), 0.0)
            bv = av * b_sh + bv
            av = av * a_sh
            d *= 2
        hv = bv + av * carry
        h_ref[rows, :] = hv
        return hv[0:1, :] if reverse else hv[CHUNK - 1:CHUNK, :]

    lax.fori_loop(0, nc, chunk, jnp.zeros((1, width), F32))


def _lru_fwd(proj, seq, cw, cb, wa, ba, wx, bx, lam):
    def fn(raw, cw, cb, wa, ba, wx, bx, lam, a_scr, u_scr, h_scr):
        xl = _conv(raw, cw, cb)
        r, i, sp, log_a, a, mult = _lru_gates(xl, wa, ba, wx, bx, lam)
        a_scr[...] = a
        u_scr[...] = mult * (i * xl)
        _scan_chunks(a_scr, u_scr, h_scr, seq, reverse=False)
        return h_scr[...], xl

    return _rowcall("lru_fwd", fn, [(proj, 256, 2)], [cw, cb, wa, ba, wx, bx, lam],
                    [(256, F32), (256, F32)], tile=seq,
                    scratch=[pltpu.VMEM((seq, 256), F32)] * 3)


def _lru_bwd(proj, seq, xl_all, h_all, dh_all, cw, cb, wa, ba, wx, bx, lam):
    def fn(raw, xl, hseq, dh, cw, cb, wa, ba, wx, bx, lam, a_scr, u_scr, h_scr):
        r, i, sp, log_a, a, mult = _lru_gates(xl, wa, ba, wx, bx, lam)
        a_scr[...] = _shift_up(a, 1)
        u_scr[...] = dh
        _scan_chunks(a_scr, u_scr, h_scr, seq, reverse=True)
        dht = h_scr[...]
        da = dht * _shift_down(hseq, 1)
        gated = i * xl
        dgated = dht * mult
        dmult = dht * gated
        dlog_a = da * a - dmult * (a * a) / mult
        dr = dlog_a * (-LRU_C * sp)
        dsp = jnp.sum(dlog_a * (-LRU_C * r), axis=0, keepdims=True)
        dlam = -dsp * _sigmoid(-lam)
        dpa = dr * r * (1.0 - r)
        dpx = (dgated * xl) * i * (1.0 - i)
        dpa_b, dpx_b = dpa.astype(BF16), dpx.astype(BF16)
        dxl = dgated * i + _dot_nt(dpa_b, wa) + _dot_nt(dpx_b, wx)
        xb = xl.astype(BF16)
        dwa = _dot_tn(xb, dpa_b)
        dwx = _dot_tn(xb, dpx_b)
        draw, dcw, dcb = _conv_bwd(raw, cw, dxl)
        return (draw, dcw, dcb, dwa, jnp.sum(dpa, axis=0, keepdims=True), dwx,
                jnp.sum(dpx, axis=0, keepdims=True), dlam)

    return _rowcall("lru_bwd", fn, [(proj, 256, 2), xl_all, h_all, dh_all], [cw, cb, wa, ba, wx, bx, lam],
                    [(256, BF16)],
                    [((4, 256), F32), ((1, 256), F32), ((256, 256), F32), ((1, 256), F32), ((256, 256), F32),
                     ((1, 256), F32), ((1, 256), F32)],
                    tile=seq, scratch=[pltpu.VMEM((seq, 256), F32)] * 3)


FOX_SCALE = HEAD_DIM ** -0.5
FOX_BLOCK = 1024


class _Side:
    def __init__(self, inputs, out_shape, sems, build):
        self.inputs, self.out_shape, self.sems, self.build = list(inputs), list(out_shape), list(sems), build

    def specs(self):
        any_spec = pl.BlockSpec(memory_space=pl.ANY)
        return [any_spec] * len(self.inputs), [any_spec] * len(self.out_shape)

    def run(self, refs, first, last):
        n_in, n_out = len(self.inputs), len(self.out_shape)
        in_refs, out_refs, sem_refs = refs[:n_in], refs[n_in:n_in + n_out], refs[n_in + n_out:]

        @pl.when(first)
        def _():
            for cp in self.build(in_refs, out_refs, sem_refs):
                cp.start()

        @pl.when(last)
        def _():
            for cp in self.build(in_refs, out_refs, sem_refs):
                cp.wait()


def _grid_ends(grid):
    ids = [pl.program_id(a) for a in range(len(grid))]
    first = functools.reduce(jnp.logical_and, [i == 0 for i in ids])
    last = functools.reduce(jnp.logical_and, [i == n - 1 for i, n in zip(ids, grid)])
    return first, last


Q_BLK, K_BLK, V_BLK = 1920 // 128, 2304 // 128, 2688 // 128
FOX_PAIRS = FOX_HEADS // 2


def _fox_bias(cum, nb, seq):
    cf = cum.reshape(nb, seq, 128)[:, :, SSD_HEADS:SSD_HEADS + FOX_HEADS]
    cols = jnp.pad(cf.reshape(nb * seq, FOX_PAIRS, 2), ((0, 0), (0, 0), (0, 126))).reshape(nb * seq, 384)
    rows = jnp.pad(cf.transpose(0, 2, 1).reshape(nb, FOX_PAIRS, 2, seq), ((0, 0), (0, 0), (0, 6), (0, 0)))
    return cols, rows


def _fox_fwd(proj, bias_cols, bias_rows, nb, seq, side=None):
    tb = min(FOX_BLOCK, seq)
    nq = seq // tb
    grid = (nb, FOX_PAIRS, nq)
    n_in = len(side.inputs) if side else 0

    def body(*refs):
        q_ref, k_ref, v_ref, cq_ref, ck_ref = refs[:5]
        o_ref, lse_ref = refs[5 + n_in:7 + n_in]
        if side is not None:
            side.run(refs[5:5 + n_in] + refs[7 + n_in:], *_grid_ends(grid))
        qi = pl.program_id(2)
        low = lax.broadcasted_iota(jnp.int32, (tb, 128), 1) < HEAD_DIM
        q2 = q_ref[...] * FOX_SCALE
        qm = [jnp.where(low, q2, 0.0).astype(BF16), jnp.where(low, 0.0, q2).astype(BF16)]
        cqs = [cq_ref[:, 0:1], cq_ref[:, 1:2]]

        def block(j, carry, diagonal):
            cols = pl.ds(pl.multiple_of(j * tb, tb), tb)
            k2 = k_ref[cols, :].astype(BF16)
            v2 = v_ref[cols, :].astype(BF16)
            new = []
            for hh in range(2):
                m_i, l_i, acc = carry[hh]
                s = _dot_nt(qm[hh], k2) + cqs[hh] - ck_ref[0, 0, hh:hh + 1, cols]
                if diagonal:
                    s = jnp.where(lax.broadcasted_iota(jnp.int32, (tb, tb), 0)
                                  >= lax.broadcasted_iota(jnp.int32, (tb, tb), 1), s, NEG)
                m_new = jnp.maximum(m_i, jnp.max(s, axis=1, keepdims=True))
                p = jnp.exp(s - m_new)
                alpha = jnp.exp(m_i - m_new)
                new.append((m_new, alpha * l_i + jnp.sum(p, axis=1, keepdims=True),
                            alpha * acc + _dot(p.astype(BF16), v2)))
            return tuple(new)

        one = (jnp.full((tb, 1), NEG, F32), jnp.zeros((tb, 1), F32), jnp.zeros((tb, 128), F32))
        carry = lax.fori_loop(0, qi, lambda j, cr: block(j, cr, False), (one, one))
        (m0, l0, a0), (m1, l1, a1) = block(qi, carry, True)
        o_ref[...] = jnp.where(low, a0 / l0, a1 / l1)
        lse_ref[...] = jnp.where(low, m0 + jnp.log(l0), m1 + jnp.log(l1))

    def blk(first):
        return pl.BlockSpec((tb, 128), lambda b, p, i: (b * nq + i, first + p))

    def seq_blk(first):
        return pl.BlockSpec((seq, 128), lambda b, p, i: (b, first + p))

    row_spec = pl.BlockSpec((1, 1, 8, seq), lambda b, p, i: (b, p, 0, 0))
    side_in, side_out = side.specs() if side else ([], [])
    shape = jax.ShapeDtypeStruct((nb * seq, FOX_WIDTH), F32)
    return pl.pallas_call(
        body, name="fox_fwd", grid=grid,
        out_shape=[shape, shape] + (side.out_shape if side else []),
        in_specs=[blk(Q_BLK), seq_blk(K_BLK), seq_blk(V_BLK), blk(0), row_spec] + side_in,
        out_specs=[blk(0), blk(0)] + side_out,
        scratch_shapes=side.sems if side else [],
        compiler_params=_params(("arbitrary", "arbitrary", "arbitrary")),
    )(proj, proj, proj, bias_cols, bias_rows, *(side.inputs if side else []))


def _fox_bwd(proj, o, lse, do, bias_cols, bias_rows, nb, seq, side=None):
    tb = min(FOX_BLOCK, seq)
    nq = seq // tb
    grid = (nb, FOX_PAIRS, nq)
    n_in = len(side.inputs) if side else 0

    def body(*refs):
        q_ref, k_ref, v_ref, o_ref, lse_ref, do_ref, cq_ref, ck_ref = refs[:8]
        dq_ref, dk_ref, dv_ref, dcum_ref, dcq_ref = refs[8 + n_in:13 + n_in]
        if side is not None:
            side.run(refs[8:8 + n_in] + refs[13 + n_in:], *_grid_ends(grid))
        kj = pl.program_id(2)

        @pl.when(kj == 0)
        def _():
            dq_ref[...] = jnp.zeros_like(dq_ref)
            dcq_ref[...] = jnp.zeros_like(dcq_ref)

        lane = lax.broadcasted_iota(jnp.int32, (tb, 128), 1)
        low = lane < HEAD_DIM
        mine = [low, jnp.logical_not(low)]
        k2 = k_ref[...]
        kb = k2.astype(BF16)
        km = [jnp.where(mine[hh], k2, 0.0).astype(BF16) for hh in range(2)]
        vb = v_ref[...].astype(BF16)

        def block(i, carry, diagonal):
            dk, dv, c0, c1 = carry
            csum = [c0, c1]
            rows = pl.ds(pl.multiple_of(i * tb, tb), tb)
            q2 = q_ref[rows, :] * FOX_SCALE
            do2 = do_ref[rows, :]
            prod = do2 * o_ref[rows, :]
            dq_add = jnp.zeros((tb, 128), F32)
            rsum = []
            for hh in range(2):
                qm = jnp.where(mine[hh], q2, 0.0).astype(BF16)
                dom = jnp.where(mine[hh], do2, 0.0).astype(BF16)
                delta = jnp.sum(jnp.where(mine[hh], prod, 0.0), axis=1, keepdims=True)
                s = _dot_nt(qm, kb) + cq_ref[rows, hh:hh + 1] - ck_ref[0, 0, hh:hh + 1, :]
                if diagonal:
                    s = jnp.where(lax.broadcasted_iota(jnp.int32, (tb, tb), 0)
                                  >= lax.broadcasted_iota(jnp.int32, (tb, tb), 1), s, NEG)
                p = jnp.exp(s - lse_ref[rows, HEAD_DIM * hh:HEAD_DIM * hh + 1])
                ds = p * (_dot_nt(dom, vb) - delta)
                dsb = ds.astype(BF16)
                dv = dv + _dot_tn(p.astype(BF16), dom)
                dk = dk + _dot_tn(dsb, qm)
                dq_add = dq_add + _dot(dsb, km[hh])
                rsum.append(jnp.sum(ds, axis=1, keepdims=True))
                csum[hh] = csum[hh] + jnp.sum(ds, axis=0, keepdims=True)
            dq_ref[rows, :] += dq_add * FOX_SCALE
            dcq_ref[rows, :] += jnp.where(lane == 0, rsum[0], jnp.where(lane == 1, rsum[1], 0.0))
            return dk, dv, csum[0], csum[1]

        init = (jnp.zeros((tb, 128), F32), jnp.zeros((tb, 128), F32), jnp.zeros((1, tb), F32),
                jnp.zeros((1, tb), F32))
        carry = block(kj, init, True)
        dk, dv, c0, c1 = lax.fori_loop(kj + 1, nq, lambda i, cr: block(i, cr, False), carry)
        dk_ref[...] = dk.astype(dk_ref.dtype)
        dv_ref[...] = dv.astype(dv_ref.dtype)
        row = lax.broadcasted_iota(jnp.int32, (8, tb), 0)
        dcum_ref[0, 0] = jnp.where(row == 0, -c0, jnp.where(row == 1, -c1, 0.0))

    def blk(first):
        return pl.BlockSpec((tb, 128), lambda b, p, j: (b * nq + j, first + p))

    def seq_blk(first):
        return pl.BlockSpec((seq, 128), lambda b, p, j: (b, first + p))

    row_blk = pl.BlockSpec((1, 1, 8, tb), lambda b, p, j: (b, p, 0, j))
    side_in, side_out = side.specs() if side else ([], [])
    tokens = nb * seq
    return pl.pallas_call(
        body, name="fox_bwd", grid=grid,
        out_shape=[jax.ShapeDtypeStruct((tokens, FOX_WIDTH), F32), jax.ShapeDtypeStruct((tokens, FOX_WIDTH), BF16),
                   jax.ShapeDtypeStruct((tokens, FOX_WIDTH), BF16),
                   jax.ShapeDtypeStruct((nb, FOX_PAIRS, 8, seq), F32),
                   jax.ShapeDtypeStruct((tokens, FOX_WIDTH), F32)] + (side.out_shape if side else []),
        in_specs=[seq_blk(Q_BLK), blk(K_BLK), blk(V_BLK), seq_blk(0), seq_blk(0), seq_blk(0), seq_blk(0), row_blk]
        + side_in,
        out_specs=[seq_blk(0), blk(0), blk(0), row_blk, seq_blk(0)] + side_out,
        scratch_shapes=side.sems if side else [],
        compiler_params=_params(("arbitrary", "arbitrary", "arbitrary")),
    )(proj, proj, proj, o, lse, do, bias_cols, bias_rows, *(side.inputs if side else []))


_ANY = pl.BlockSpec(memory_space=pl.ANY)


def _place():
    return lax.axis_index("x"), lax.axis_index("y"), lax.axis_index("c")


def _all_gather(shards, name):
    n = len(shards)

    def body(*refs):
        x_refs, out_refs = refs[:n], refs[n:2 * n]
        send_sems, recv_sems, local_sems = refs[2 * n:]
        x, y, c = _place()
        me, sibling = (x, y, c), (x, y, 1 - c)
        chips = [(1 - x, y), (x, 1 - y), (1 - x, 1 - y)]

        def rows(a, px, py, pc):
            return out_refs[a].at[4 * px + 2 * py + pc]

        def copy(a, k, block, to, src=None):
            return pltpu.make_async_remote_copy(
                src_ref=rows(a, *block) if src is None else src, dst_ref=rows(a, *block),
                send_sem=send_sems.at[a, k], recv_sem=recv_sems.at[a, k],
                device_id=to, device_id_type=pl.DeviceIdType.MESH)

        mine = [pltpu.make_async_copy(x_refs[a], rows(a, *me), local_sems.at[a]) for a in range(n)]
        for cp in mine:
            cp.start()
        first = []
        for a in range(n):
            first.append(copy(a, 0, me, sibling, src=x_refs[a]))
            first += [copy(a, 1 + j, me, (*chip, c), src=x_refs[a]) for j, chip in enumerate(chips)]
        for cp in first:
            cp.start()
        passed = []
        for j, chip in enumerate(chips):
            for a in range(n):
                copy(a, 1 + j, (*chip, c), me).wait_recv()
                passed.append(copy(a, 4 + j, (*chip, c), sibling))
                passed[-1].start()
        for a in range(n):
            copy(a, 0, sibling, me).wait_recv()
            for j, chip in enumerate(chips):
                copy(a, 4 + j, (*chip, 1 - c), me).wait_recv()
        for cp in first + passed:
            cp.wait_send()
        for cp in mine:
            cp.wait()

    return pl.pallas_call(
        body, name=name,
        out_shape=[jax.ShapeDtypeStruct((N_DEV,) + s.shape, s.dtype) for s in shards],
        in_specs=[_ANY] * n, out_specs=[_ANY] * n,
        scratch_shapes=[pltpu.SemaphoreType.DMA((n, 7)), pltpu.SemaphoreType.DMA((n, 7)),
                        pltpu.SemaphoreType.DMA((n,))],
    )(*shards)


def _remote(src, dst, send_sem, recv_sem, to):
    return pltpu.make_async_remote_copy(src_ref=src, dst_ref=dst, send_sem=send_sem, recv_sem=recv_sem,
                                        device_id=to, device_id_type=pl.DeviceIdType.MESH)


def _sem_pairs(n, k):
    return [pltpu.SemaphoreType.DMA((n, k)), pltpu.SemaphoreType.DMA((n, k))]


def _sibling_side(full, offsets):
    def build(g_refs, out_refs, sems):
        x, y, c = _place()
        return [_remote(g_refs[a].at[offsets[a] + 4 * (k // 2) + 2 * (k % 2) + (1 - c)], out_refs[a].at[k],
                        sems[0].at[a, k], sems[1].at[a, k], (x, y, 1 - c))
                for a in range(len(g_refs)) for k in range(4)]

    return _Side(full, [jax.ShapeDtypeStruct((4,) + f.shape[1:], f.dtype) for f in full],
                 _sem_pairs(len(full), 4), build)


def _chip_side(part):
    def build(p_refs, out_refs, sems):
        x, y, c = _place()
        peers = [(1 - x, y), (x, 1 - y), (1 - x, 1 - y)]
        return [_remote(p_refs[a].at[2 * px + py], out_refs[a].at[k], sems[0].at[a, k], sems[1].at[a, k],
                        (px, py, c))
                for a in range(len(p_refs)) for k, (px, py) in enumerate(peers)]

    return _Side(part, [jax.ShapeDtypeStruct((3,) + p.shape[1:], p.dtype) for p in part],
                 _sem_pairs(len(part), 3), build)


def _spread_side(shards):
    def build(x_refs, out_refs, sems):
        x, y, c = _place()
        targets = [(x, y, 1 - c), (1 - x, y, c), (x, 1 - y, c), (1 - x, 1 - y, c)]
        cps = []
        for a in range(len(x_refs)):
            slot = out_refs[a].at[4 * x + 2 * y + c]
            cps.append(pltpu.make_async_copy(x_refs[a], slot, sems[2].at[a]))
            cps += [_remote(x_refs[a], slot, sems[0].at[a, k], sems[1].at[a, k], to)
                    for k, to in enumerate(targets)]
        return cps

    n = len(shards)
    return _Side(shards, [jax.ShapeDtypeStruct((N_DEV,) + s.shape, s.dtype) for s in shards],
                 _sem_pairs(n, 4) + [pltpu.SemaphoreType.DMA((n,))], build)


def _pass_side(bufs):
    def build(in_refs, out_refs, sems):
        x, y, c = _place()
        chips = [(1 - x, y), (x, 1 - y), (1 - x, 1 - y)]
        return [_remote(in_refs[a].at[4 * px + 2 * py + c], out_refs[a].at[4 * px + 2 * py + c],
                        sems[0].at[a, j], sems[1].at[a, j], (x, y, 1 - c))
                for a in range(len(in_refs)) for j, (px, py) in enumerate(chips)]

    return _Side(bufs, [jax.ShapeDtypeStruct(b.shape, b.dtype) for b in bufs], _sem_pairs(len(bufs), 3), build)


def _run_side(side, name, in_place=False):
    n_in = len(side.inputs)

    def body(*refs):
        copies = side.build(refs[:n_in], refs[n_in:n_in + len(side.out_shape)],
                            refs[n_in + len(side.out_shape):])
        for cp in copies:
            cp.start()
        for cp in copies:
            cp.wait()

    in_specs, out_specs = side.specs()
    return pl.pallas_call(
        body, name=name, out_shape=side.out_shape, in_specs=in_specs, out_specs=out_specs,
        scratch_shapes=side.sems,
        input_output_aliases={a: a for a in range(n_in)} if in_place else {},
    )(*side.inputs)


def _pick_rows(rows, cap=512):
    t = cap
    while t >= 8:
        if rows % t == 0:
            return t
        t //= 2
    raise ValueError(rows)


def _pair_sum(full, offset, got, name):
    _, rows, cols = full.shape
    tile = _pick_rows(rows, 256)
    c = lax.axis_index("c").astype(jnp.int32).reshape(1)

    def body(c_ref, a_ref, b_ref, o_ref):
        o_ref[...] = a_ref[...] + b_ref[...]

    blk = (1, tile, cols)
    return pl.pallas_call(
        body, name=name,
        grid_spec=pltpu.PrefetchScalarGridSpec(
            num_scalar_prefetch=1, grid=(4, rows // tile),
            in_specs=[pl.BlockSpec(blk, lambda k, i, c_ref: (offset + 4 * (k // 2) + 2 * (k % 2) + c_ref[0], i, 0)),
                      pl.BlockSpec(blk, lambda k, i, c_ref: (k, i, 0))],
            out_specs=pl.BlockSpec(blk, lambda k, i, c_ref: (k, i, 0))),
        out_shape=jax.ShapeDtypeStruct((4, rows, cols), full.dtype),
        compiler_params=_params(("arbitrary", "arbitrary")),
    )(c, full, got)


def _adam_math(w, g, m, v):
    c1 = 1.0 / (1.0 - ADAM_B1 ** ADAM_STEP)
    c2 = 1.0 / (1.0 - ADAM_B2 ** ADAM_STEP)
    m_new = ADAM_B1 * m + (1.0 - ADAM_B1) * g
    v_new = ADAM_B2 * v + (1.0 - ADAM_B2) * (g * g)
    delta = -ADAM_LR * ((m_new * c1) / (jnp.sqrt(v_new * c2) + ADAM_EPS) + ADAM_WD * w)
    return delta, m_new, v_new


def _chip_sum(part, others, name):
    _, rows, cols = part.shape
    tile = _pick_rows(rows, 256)
    own = (2 * lax.axis_index("x") + lax.axis_index("y")).astype(jnp.int32).reshape(1)

    def body(own_ref, p_ref, o_ref, g_out):
        g_out[...] = ((p_ref[0] + o_ref[0]) + o_ref[1]) + o_ref[2]

    return pl.pallas_call(
        body, name=name,
        grid_spec=pltpu.PrefetchScalarGridSpec(
            num_scalar_prefetch=1, grid=(rows // tile,),
            in_specs=[pl.BlockSpec((1, tile, cols), lambda i, own_ref: (own_ref[0], i, 0)),
                      pl.BlockSpec((3, tile, cols), lambda i, own_ref: (0, i, 0))],
            out_specs=pl.BlockSpec((tile, cols), lambda i, own_ref: (i, 0))),
        out_shape=jax.ShapeDtypeStruct((rows, cols), F32),
        compiler_params=_params(("arbitrary",)),
    )(own, part, others)


def _all_reduce_small(vec):
    gathered = _all_gather([vec], "ar_gather")[0]
    rows = vec.shape[0]

    def fn(*blocks):
        s = blocks[0]
        for b in blocks[1:]:
            s = s + b
        return s

    return _rowcall("ar_sum", fn, [gathered[j] for j in range(N_DEV)], [], [(1024, F32)],
                    tile=_pick_rows(rows))[0]


def _pad_rows(flat, mult):
    n = flat.shape[-1]
    per = mult * 1024
    padded = -(-n // per) * per
    pad = [(0, 0)] * (flat.ndim - 1) + [(0, padded - n)]
    return jnp.pad(flat, pad).reshape(flat.shape[:-1] + (padded // 1024, 1024))


def _regroup_w_in(w):
    pad = jnp.zeros((w.shape[0], 116), w.dtype)
    return jnp.concatenate([w[:, 768:1280], w[:, 1286:1798], w[:, 1280:1286], w[:, 2950:2956], pad,
                            w[:, 0:768], w[:, 1798:2950]], axis=1)


def _ungroup_w_in(wp):
    return jnp.concatenate([wp[:, 1152:1920], wp[:, 0:512], wp[:, 1024:1030], wp[:, 512:1024],
                            wp[:, 1920:3072], wp[:, 1030:1036]], axis=1)


def _to_shard(name, a):
    if name == 'w_in':
        return _regroup_w_in(a)
    if name in ('w_gate', 'w_up'):
        return jnp.pad(a, ((0, 0), (0, FF_SHARD_P - FF_SHARD)))
    if name == 'w_down':
        return jnp.pad(a, ((0, FF_SHARD_P - FF_SHARD), (0, 0)))
    return a


def _from_shard(name, a):
    if name == 'w_in':
        return _ungroup_w_in(a)
    if name in ('w_gate', 'w_up'):
        return a[:, 0:FF_SHARD]
    if name == 'w_down':
        return a[0:FF_SHARD, :]
    return a


def _whole(name, gathered):
    if BIG[name][0] == 1:
        return gathered.reshape(-1, gathered.shape[-1])
    return gathered.transpose(1, 0, 2).reshape(gathered.shape[1], -1)


def _split(name, whole):
    if BIG[name][0] == 1:
        return whole.reshape(N_DEV, whole.shape[0] // N_DEV, whole.shape[1])
    return whole.reshape(whole.shape[0], N_DEV, whole.shape[1] // N_DEV).transpose(1, 0, 2)


def _pack_list(arrays, mult):
    return _pad_rows(jnp.concatenate([a.reshape(-1) for a in arrays]), mult)


def _unpack_list(buf, shapes):
    flat = buf.reshape(-1)
    out, off = [], 0
    for s in shapes:
        n = math.prod(s)
        out.append(flat[off:off + n].reshape(s))
        off += n
    return out


def _adamw(w, g, m, v, name="adamw"):
    rows, cols = w.shape
    return _rowcall(name, _adam_math, [w, g, m, v], [], [(cols, F32)] * 3, tile=_pick_rows(rows, 256))


def _block_diag(w):
    out = jnp.zeros((LRU_WIDTH, LRU_WIDTH), w.dtype)
    for g in range(4):
        out = lax.dynamic_update_slice(out, w[g], (64 * g, 64 * g))
    return out


def _block_diag_grad(full):
    return jnp.stack([full[64 * g:64 * (g + 1), 64 * g:64 * (g + 1)] for g in range(4)])


def _row(v):
    return v.reshape(1, -1).astype(F32)


def _lane128(*pieces):
    flat = jnp.concatenate([p.reshape(-1).astype(F32) for p in pieces])
    return jnp.pad(flat, (0, 128 - flat.shape[0])).reshape(1, 128)


def _layer_consts(w):
    c = {}
    cw, cb = w['ssd_conv_w'], w['ssd_conv_b']
    c['cw_x'], c['cw_b'], c['cw_c'] = cw[:, 0:384], cw[:, 384:640], cw[:, 640:896]
    c['cb_x'], c['cb_b'], c['cb_c'] = _row(cb[0:384]), _row(cb[384:640]), _row(cb[640:896])
    c['bias128'] = _lane128(w['ssd_dt_bias'], w['fox_b_f'])
    c['alog128'] = _lane128(w['ssd_a_log'])
    c['d384'] = _row(jnp.repeat(w['ssd_d'], HEAD_DIM))
    c['lcw'], c['lcb'] = w['lru_conv_w'], _row(w['lru_conv_b'])
    c['wa'], c['wx'] = _block_diag(w['lru_w_a']).astype(BF16), _block_diag(w['lru_w_x']).astype(BF16)
    c['ba'], c['bx'], c['lam'] = _row(w['lru_b_a']), _row(w['lru_b_x']), _row(w['lru_lambda'])
    return c


def _layer_fwd(h0, p_i, w, c, hooks=None, layer=0):
    nb, seq = c['nb'], c['seq']

    def carried(stage):
        return hooks.fwd_side(layer, stage) if hooks is not None else None

    def arrived(outs):
        if hooks is not None:
            hooks.fwd_done(outs)
            w.update(hooks.weights(layer))

    u1 = _rowcall("norm1", lambda h, g: _rms(h, g), [h0], [_row(w['norm1_g'])], [(D_MODEL, BF16)])[0]
    proj = _matmul(u1, w['w_in'], 'nn', "proj")

    xs_c = _convsilu_fwd("conv_x", (proj, 384, 4), seq, c['cw_x'], c['cb_x'], F32)
    b_c = _convsilu_fwd("conv_b", (proj, 256, 0), seq, c['cw_b'], c['cb_b'], BF16)
    c_c = _convsilu_fwd("conv_c", (proj, 256, 1), seq, c['cw_c'], c['cb_c'], BF16)
    dt_arr, cum, prev, end = _small_fwd((proj, 128, 8), seq, c['bias128'], c['alog128'])
    x_h = xs_c.reshape(nb, seq, SSD_WIDTH)
    cum3 = cum.reshape(nb, seq, 128)
    a_row = cum3[:, :, 0:8].transpose(0, 2, 1)
    ssd_in = (x_h, b_c.reshape(nb, seq, 256), c_c.reshape(nb, seq, 256), dt_arr.reshape(nb, seq, 128), cum3,
              prev.reshape(nb, seq, 128), end.reshape(nb, seq, 128), a_row)
    y_h, states = _ssd_fwd(*ssd_in)
    y_core = y_h.reshape(nb * seq, SSD_WIDTH)

    hseq, xl = _lru_fwd(proj, seq, c['lcw'], c['lcb'], c['wa'], c['ba'], c['wx'], c['bx'], c['lam'])

    bias_cols, bias_rows = _fox_bias(cum, nb, seq)
    y_fox, lse, *side_out = _fox_fwd(proj, bias_cols, bias_rows, nb, seq, carried('attention'))
    arrived(side_out)

    def post(yc, xs, z, hs, lg, yf, d, g1, g2, g3):
        y1 = _rms((yc + xs * d) * _silu(z), g1)
        y2 = _rms(hs * _gelu(lg), g2)
        y3 = _rms(yf, g3)
        return jnp.concatenate([y1, y2, y3], axis=-1)

    post_consts = [c['d384'], _row(w['ssd_norm_g']), _row(w['lru_norm_g']), _row(w['fox_norm_g'])]
    ycat = _rowcall("mix_post", post, [y_core, xs_c, (proj, 384, 3), hseq, (proj, 256, 3), y_fox], post_consts,
                    [(D_MODEL, BF16)])[0]
    mix = _matmul(ycat, w['w_out'], 'nn', "mix_out", BF16)

    def res_norm(h, d, g):
        hn = h + d.astype(F32)
        return hn, _rms(hn, g)

    h1, u2 = _rowcall("res_norm2", res_norm, [h0, mix], [_row(w['norm2_g'])], [(D_MODEL, F32), (D_MODEL, BF16)])
    gate_pre, up_pre, act = _ffn_in(u2, w['w_gu'])
    side = carried('ffn_out')
    if side is None:
        ff = _matmul(act, w['w_down'], 'nn', "ffn_out", BF16)
    else:
        ff, *side_out = _matmul(act, w['w_down'], 'nn', "ffn_out", BF16, side=side)
        arrived(side_out)
    h2, u3 = _rowcall("res_norm3", res_norm, [h1, ff], [_row(w['norm3_g'])], [(D_MODEL, F32), (D_MODEL, BF16)])
    pg = _matmul(u3, w['w_ple_gate'], 'nn', "ple_gate", BF16)
    pp = _matmul(p_i, w['w_ple_proj'], 'nn', "ple_proj", BF16)
    h3 = _rowcall("ple", lambda h, a, b, bias: h + _sigmoid(a.astype(F32) + bias) * b.astype(F32), [h2, pg, pp],
                  [_row(w['b_ple_gate'])],
                  [(D_MODEL, F32)])[0]
    saved = dict(h0=h0, u1=u1, proj=proj, xs_c=xs_c, dt_arr=dt_arr, ssd_in=ssd_in, states=states,
                 y_core=y_core, hseq=hseq, xl=xl, bias_cols=bias_cols, bias_rows=bias_rows, lse=lse,
                 y_fox=y_fox, post_consts=post_consts, ycat=ycat, h1=h1, u2=u2, gate_pre=gate_pre, up_pre=up_pre, act=act, h2=h2, u3=u3,
                 pg=pg, pp=pp, p_i=p_i)
    return h3, saved


def _layer_bwd(dh3, s, w, c, hooks=None, layer=0):
    nb, seq = c['nb'], c['seq']
    g = {}

    def ple_bwd(dh, a, b, bias):
        a, b = a.astype(F32), b.astype(F32)
        gate = _sigmoid(a + bias)
        dpg = dh * b * gate * (1.0 - gate)
        return dh * gate, dpg, jnp.sum(dpg, axis=0, keepdims=True)

    dpp, dpg, g['b_ple_gate'] = _rowcall("ple_bwd", ple_bwd, [dh3, s['pg'], s['pp']], [_row(w['b_ple_gate'])],
                                         [(D_MODEL, BF16), (D_MODEL, BF16)], [((1, D_MODEL), F32)])
    g['w_ple_proj'] = _matmul(s['p_i'], dpp, 'tn', "d_w_ple_proj")
    g['w_ple_gate'] = _matmul(s['u3'], dpg, 'tn', "d_w_ple_gate")
    du3 = _matmul(dpg, w['w_ple_gate'], 'nt', "d_u3", BF16)

    def norm_bwd(h, du, dh, gain):
        dx, dg = _rms_bwd(h, gain, du.astype(F32))
        dhn = dh + dx
        return dhn, dhn, dg

    dh2, dh2_b, g['norm3_g'] = _rowcall("norm3_bwd", norm_bwd, [s['h2'], du3, dh3], [_row(w['norm3_g'])],
                                        [(D_MODEL, F32), (D_MODEL, BF16)], [((1, D_MODEL), F32)])
    g['w_down'] = _matmul(s['act'], dh2_b, 'tn', "d_w_down")
    dact = _matmul(dh2_b, w['w_down'], 'nt', "d_act", BF16)

    def swiglu_bwd(gt, up, da):
        gt, up, da = gt.astype(F32), up.astype(F32), da.astype(F32)
        return jnp.concatenate([da * up * _dsilu(gt), da * _silu(gt)], axis=-1)

    dgu = _rowcall("swiglu_bwd", swiglu_bwd, [s['gate_pre'], s['up_pre'], dact], [],
                   [(2 * D_FF_P, BF16)])[0]
    gu16 = _matmul(s['u2'], dgu, 'tn', "d_w_gu", shard_n=FF_SHARD_P)
    du2 = _matmul(dgu, w['w_gu'], 'nt', "d_u2", BF16)
    dh1, dh1_b, g['norm2_g'] = _rowcall("norm2_bwd", norm_bwd, [s['h1'], du2, dh2], [_row(w['norm2_g'])],
                                        [(D_MODEL, F32), (D_MODEL, BF16)], [((1, D_MODEL), F32)])
    g['w_out'] = _matmul(s['ycat'], dh1_b, 'tn', "d_w_out")
    if hooks is None:
        dycat = _matmul(dh1_b, w['w_out'], 'nt', "d_ycat", BF16)
    else:
        ready = {n: g[n] for n in ('w_out', 'w_down', 'w_ple_gate', 'w_ple_proj')}
        ready['w_gate'], ready['w_up'] = (gu16, 0), (gu16, N_DEV)
        dycat, *side_out = _matmul(dh1_b, w['w_out'], 'nt', "d_ycat", BF16, side=hooks.sibling_side(layer, ready))
        hooks.sibling_done(side_out)

    def post_bwd(dy, yc, xs, z, hs, lg, yf, d, g1, g2, g3):
        dy = dy.astype(F32)
        sz = _silu(z)
        ytot = yc + xs * d
        dpre1, dg1 = _rms_bwd(ytot * sz, g1, dy[:, 0:384])
        dytot = dpre1 * sz
        dz = dpre1 * ytot * _dsilu(z)
        dd = jnp.sum(dytot * xs, axis=0, keepdims=True)
        gl = _gelu(lg)
        dpre2, dg2 = _rms_bwd(hs * gl, g2, dy[:, 384:640])
        dyf, dg3 = _rms_bwd(yf, g3, dy[:, 640:1024])
        return dytot, dytot * d, dz, dpre2 * gl, dpre2 * hs * _dgelu(lg), dyf, dd, dg1, dg2, dg3

    (dy_core, dxs_skip, dz, dhseq, dlg, dy_fox, dd384, g['ssd_norm_g'], g['lru_norm_g'], g['fox_norm_g']) = _rowcall(
        "mix_post_bwd", post_bwd,
        [dycat, s['y_core'], s['xs_c'], (s['proj'], 384, 3), s['hseq'], (s['proj'], 256, 3), s['y_fox']],
        s['post_consts'],
        [(384, F32), (384, F32), (384, BF16), (256, F32), (256, BF16), (384, F32)],
        [((1, 384), F32), ((1, 384), F32), ((1, 256), F32), ((1, 384), F32)])
    g['ssd_d'] = dd384.reshape(SSD_HEADS, HEAD_DIM).sum(axis=1)

    side = hooks.bwd_side() if hooks is not None else None
    dq, dk, dv, dcf_rows, dcf_cols, *side_out = _fox_bwd(s['proj'], s['y_fox'], s['lse'], dy_fox, s['bias_cols'],
                                                         s['bias_rows'], nb, seq, side)
    if hooks is not None:
        hooks.bwd_done(side_out)
    dq = dq.astype(BF16)

    dx_h, db_c, dc_c, da_arr, dend_arr, ddt_arr = _ssd_bwd(*s['ssd_in'], s['states'],
                                                           dy_core.reshape(nb, seq, SSD_WIDTH))
    dxs_c = dx_h.reshape(nb * seq, SSD_WIDTH) + dxs_skip
    dcf = dcf_rows[:, :, 0:2, :].reshape(nb, FOX_HEADS, seq).transpose(0, 2, 1)
    dcum = jnp.concatenate([da_arr[:, :, 0:SSD_HEADS], dcf,
                            jnp.zeros((nb, seq, 128 - 2 * SSD_HEADS), F32)], axis=-1).reshape(nb * seq, 128)
    proj = s['proj']
    dxs_raw, dcw_x, dcb_x = _convsilu_bwd("conv_x_bwd", (proj, 384, 4), seq, c['cw_x'], c['cb_x'], dxs_c)
    db_raw, dcw_b, dcb_b = _convsilu_bwd("conv_b_bwd", (proj, 256, 0), seq, c['cw_b'], c['cb_b'],
                                         db_c.reshape(nb * seq, 256))
    dc_raw, dcw_c, dcb_c = _convsilu_bwd("conv_c_bwd", (proj, 256, 1), seq, c['cw_c'], c['cb_c'],
                                         dc_c.reshape(nb * seq, 256))
    dsmall, dbias128, dalog128 = _small_bwd((proj, 128, 8), seq, dcum, dcf_cols, dend_arr.reshape(nb * seq, 128),
                                            ddt_arr.reshape(nb * seq, 128), s['dt_arr'], c['bias128'], c['alog128'])
    g['ssd_conv_w'] = jnp.concatenate([dcw_x, dcw_b, dcw_c], axis=1)
    g['ssd_conv_b'] = jnp.concatenate([dcb_x, dcb_b, dcb_c], axis=1).reshape(-1)
    g['ssd_dt_bias'] = dbias128[0, 0:SSD_HEADS]
    g['fox_b_f'] = dbias128[0, SSD_HEADS:2 * SSD_HEADS]
    g['ssd_a_log'] = dalog128[0, 0:SSD_HEADS]

    (dlru_raw, g['lru_conv_w'], dlcb, dwa, dba, dwx, dbx, dlam) = _lru_bwd(
        s['proj'], seq, s['xl'], s['hseq'], dhseq, c['lcw'], c['lcb'], c['wa'], c['ba'], c['wx'], c['bx'], c['lam'])
    g['lru_conv_b'], g['lru_b_a'], g['lru_b_x'], g['lru_lambda'] = (t.reshape(-1) for t in (dlcb, dba, dbx, dlam))
    g['lru_w_a'], g['lru_w_x'] = _block_diag_grad(dwa), _block_diag_grad(dwx)

    dproj = jnp.concatenate([db_raw, dc_raw, dlru_raw, dlg, dsmall, dz, dxs_raw, dq, dk, dv], axis=1)
    g['w_in'] = _matmul(s['u1'], dproj, 'tn', "d_w_in")
    if hooks is None:
        du1 = _matmul(dproj, w['w_in'], 'nt', "d_u1", BF16)
    else:
        hooks.bwd_ready(layer, {'w_in': g['w_in']})
        du1, *side_out = _matmul(dproj, w['w_in'], 'nt', "d_u1", BF16, side=hooks.bwd_side())
        hooks.bwd_done(side_out)

    def norm1_bwd(h, du, dh, gain):
        dx, dg = _rms_bwd(h, gain, du.astype(F32))
        return dh + dx, dg

    dh0, g['norm1_g'] = _rowcall("norm1_bwd", norm1_bwd, [s['h0'], du1, dh1], [_row(w['norm1_g'])],
                                 [(D_MODEL, F32)], [((1, D_MODEL), F32)])
    for name in ('b_ple_gate', 'norm3_g', 'norm2_g', 'norm1_g', 'ssd_norm_g', 'lru_norm_g', 'fox_norm_g'):
        g[name] = g[name].reshape(-1)
    g['w_gate'], g['w_up'] = None, None
    if hooks is None:
        g['w_gate'] = gu16[0:N_DEV].transpose(1, 0, 2).reshape(D_MODEL, D_FF_P)
        g['w_up'] = gu16[N_DEV:2 * N_DEV].transpose(1, 0, 2).reshape(D_MODEL, D_FF_P)
    return dh0, g


class _Hooks:
    def __init__(self, shard):
        self.shard = shard
        self.whole = {}
        self.part, self.others = {}, {}
        self.pending, self.flying = [], []

    def first(self, extra):
        got = _all_gather([self.shard['w_in', 0]] + extra, "gather_first")
        self.whole['w_in', 0] = _whole('w_in', got[0])
        return got[1:]

    def fwd_side(self, layer, stage):
        if stage == 'attention':
            self.flying = [(n, layer) for n in BIG if n != 'w_in']
        elif layer + 1 < DEPTH:
            self.flying = [('w_in', layer + 1)]
        else:
            return None
        return _spread_side([self.shard[k] for k in self.flying])

    def fwd_done(self, outs):
        if self.flying:
            passed = _run_side(_pass_side(outs), "gather_pass_%s%d" % self.flying[0], in_place=True)
            for k, arr in zip(self.flying, passed):
                self.whole[k] = _whole(k[0], arr)
            self.flying = []

    def weights(self, layer):
        w = {n: self.whole[n, layer] for n in BIG if (n, layer) in self.whole}
        if 'w_gate' in w:
            w['w_gu'] = jnp.concatenate([w['w_gate'], w['w_up']], axis=1)
        return w

    def sibling_side(self, layer, grads):
        self.sib_keys = [(n, layer) for n in grads]
        self.sib_full = [g if isinstance(g, tuple) else (_split(n, g), 0) for n, g in grads.items()]
        return _sibling_side([f for f, _ in self.sib_full], [off for _, off in self.sib_full])

    def sibling_done(self, got):
        for k, (f, off), r in zip(self.sib_keys, self.sib_full, got):
            self.part[k] = _pair_sum(f, off, r, "rs_pair_sum_%s%d" % k)
        self.pending += self.sib_keys

    def bwd_ready(self, layer, grads):
        side = self.sibling_side(layer, grads)
        self.sibling_done(_run_side(side, "rs_sibling_%s%d" % self.sib_keys[0]))

    def bwd_side(self):
        self.flying, self.pending = self.pending, []
        return _chip_side([self.part[k] for k in self.flying]) if self.flying else None

    def bwd_done(self, outs):
        for k, o in zip(self.flying, outs):
            self.others[k] = o
        self.flying = []

    def flush(self):
        side = self.bwd_side()
        if side is not None:
            self.bwd_done(_run_side(side, "rs_chips_last"))


def _local_step(x, p, target, big, small, hooks=None):
    nb, seq, _ = x.shape
    tokens = nb * seq
    h = x.reshape(tokens, D_MODEL)
    layers, saves = [], []
    for i in range(DEPTH):
        w = {name: small[name][i] for name in small if name != 'final_norm_g'}
        if hooks is not None:
            w.update(hooks.weights(i))
        else:
            for name in ('w_in', 'w_out', 'w_down', 'w_ple_gate', 'w_ple_proj'):
                w[name] = big[name][i]
            w['w_gu'] = jnp.concatenate([big['w_gate'][i], big['w_up'][i]], axis=1)
        c = _layer_consts(w)
        c['nb'], c['seq'] = nb, seq
        h, s = _layer_fwd(h, p[i].reshape(tokens, PLE_DIM).astype(BF16), w, c, hooks, i)
        layers.append((w, c))
        saves.append(s)

    def head(hf, tgt, gain):
        r = lax.rsqrt(jnp.mean(hf * hf, axis=-1, keepdims=True) + EPS)
        xhat = hf * r
        err = xhat * gain - tgt
        loss = 0.5 * jnp.sum(jnp.mean(err * err, axis=-1, keepdims=True), axis=0, keepdims=True)
        dy = err * (1.0 / D_MODEL)
        dg = jnp.sum(dy * xhat, axis=0, keepdims=True)
        dxhat = dy * gain
        dh = r * (dxhat - xhat * jnp.mean(dxhat * xhat, axis=-1, keepdims=True))
        return dh, jnp.broadcast_to(loss, (1, 128)), dg

    dh, loss128, dgf = _rowcall("loss_head", head, [h, target.reshape(tokens, D_MODEL)],
                                [_row(small['final_norm_g'])], [(D_MODEL, F32)],
                                [((1, 128), F32), ((1, D_MODEL), F32)])
    grads = {'final_norm_g': dgf.reshape(-1)}
    per_layer = [None] * DEPTH
    for i in range(DEPTH - 1, -1, -1):
        w, c = layers[i]
        dh, per_layer[i] = _layer_bwd(dh, saves[i], w, c, hooks, i)
    for name in per_layer[0]:
        if name in BIG:
            grads[name] = [per_layer[i][name] for i in range(DEPTH)]
        else:
            grads[name] = jnp.stack([per_layer[i][name] for i in range(DEPTH)])
    return loss128[0, 0], dh.reshape(nb, seq, D_MODEL), grads


def kernel(x, p, norm1_g, w_in, ssd_conv_w, ssd_conv_b, ssd_dt_bias, ssd_a_log, ssd_d, ssd_norm_g, lru_conv_w, lru_conv_b, lru_w_a, lru_b_a, lru_w_x, lru_b_x, lru_lambda, lru_norm_g, fox_b_f, fox_norm_g, w_out, norm2_g, w_gate, w_up, w_down, norm3_g, w_ple_gate, b_ple_gate, w_ple_proj, final_norm_g, loss_target, m_norm1_g, m_w_in, m_ssd_conv_w, m_ssd_conv_b, m_ssd_dt_bias, m_ssd_a_log, m_ssd_d, m_ssd_norm_g, m_lru_conv_w, m_lru_conv_b, m_lru_w_a, m_lru_b_a, m_lru_w_x, m_lru_b_x, m_lru_lambda, m_lru_norm_g, m_fox_b_f, m_fox_norm_g, m_w_out, m_norm2_g, m_w_gate, m_w_up, m_w_down, m_norm3_g, m_w_ple_gate, m_b_ple_gate, m_w_ple_proj, m_final_norm_g, v_norm1_g, v_w_in, v_ssd_conv_w, v_ssd_conv_b, v_ssd_dt_bias, v_ssd_a_log, v_ssd_d, v_ssd_norm_g, v_lru_conv_w, v_lru_conv_b, v_lru_w_a, v_lru_b_a, v_lru_w_x, v_lru_b_x, v_lru_lambda, v_lru_norm_g, v_fox_b_f, v_fox_norm_g, v_w_out, v_norm2_g, v_w_gate, v_w_up, v_w_down, v_norm3_g, v_w_ple_gate, v_b_ple_gate, v_w_ple_proj, v_final_norm_g):
    args = dict(locals())
    w_loc = {n: args[n] for n in WEIGHTS}
    m_loc = {n: args['m_' + n] for n in WEIGHTS}
    v_loc = {n: args['v_' + n] for n in WEIGHTS}
    dev = 4 * lax.axis_index("x") + 2 * lax.axis_index("y") + lax.axis_index("c")

    keys = [(n, i) for n in BIG for i in range(DEPTH)]
    conv_names = list(CONV_SHARDED)
    conv_loc_shapes = [w_loc[n].shape for n in conv_names]
    hooks = _Hooks({(n, i): _to_shard(n, w_loc[n][i]).astype(BF16) for n, i in keys})
    conv_all, = hooks.first([_pack_list([w_loc[n] for n in conv_names], 8)])
    small = {n: w_loc[n] for n in WEIGHTS if n not in BIG and n not in CONV_SHARDED}
    per_dev = [_unpack_list(conv_all[j], conv_loc_shapes) for j in range(N_DEV)]
    for idx, n in enumerate(conv_names):
        small[n] = jnp.concatenate([per_dev[j][idx] for j in range(N_DEV)], axis=2)

    loss_part, dx, grads = _local_step(x, p, loss_target, None, small, hooks)
    loss = lax.psum(loss_part, ("x", "y", "c"))
    hooks.flush()
    out = {kind: {} for kind in ('g', 'delta', 'm', 'v')}
    for n in BIG:
        g_nat = jnp.stack([_from_shard(n, _chip_sum(hooks.part[n, i], hooks.others[n, i], "rs_chip_sum_%s%d" % (n, i)))
                           for i in range(DEPTH)])
        shape = w_loc[n].shape
        flat = [d.reshape(-1, shape[-1]) for d in (w_loc[n], g_nat, m_loc[n], v_loc[n])]
        out['g'][n] = g_nat
        for kind, r in zip(('delta', 'm', 'v'), _adamw(*flat, name="adamw_" + n)):
            out[kind][n] = r.reshape(shape)

    small_names = [n for n in WEIGHTS if n not in BIG]
    small_shapes = [grads[n].shape for n in small_names]
    g_small = dict(zip(small_names, _unpack_list(
        _all_reduce_small(_pack_list([grads[n] for n in small_names], 8)), small_shapes)))
    for n in CONV_SHARDED:
        width = CONV_SHARDED[n][2] // N_DEV
        g_small[n] = lax.dynamic_slice_in_dim(g_small[n], dev * width, width, axis=2)
    shapes = [w_loc[n].shape for n in small_names]
    packed = [_pack_list([d[n] for n in small_names], 8) for d in (w_loc, g_small, m_loc, v_loc)]
    upd = [dict(zip(small_names, _unpack_list(t, shapes))) for t in _adamw(*packed)]
    for kind, d in zip(('g', 'delta', 'm', 'v'), [g_small] + upd):
        for n in small_names:
            out[kind][n] = d[n]
    return (loss, dx, *[out['g'][n] for n in WEIGHTS], *[out['delta'][n] for n in WEIGHTS],
            *[out['m'][n] for n in WEIGHTS], *[out['v'][n] for n in WEIGHTS])
```

```python
import functools
import math

import jax
import jax.numpy as jnp
from jax import lax
from jax.experimental import pallas as pl
from jax.experimental.pallas import tpu as pltpu

F32 = jnp.float32
BF16 = jnp.bfloat16

N_DEV = 8
D_MODEL = 1024
DEPTH = 2
HEAD_DIM = 64
SSD_WIDTH = 384
LRU_WIDTH = 256
FOX_WIDTH = 384
SSD_HEADS = 6
SSD_STATE = 128
CHUNK = 256
FOX_HEADS = 6
D_FF = 2816
FF_SHARD = D_FF // N_DEV
FF_SHARD_P = 384
D_FF_P = N_DEV * FF_SHARD_P
PLE_DIM = 256
IN_COLS = 2956
PROJ_COLS = 3072
LRU_C = 8.0
EPS = 1e-6
NEG = -1e30

ADAM_LR = 0.001
ADAM_B1 = 0.9
ADAM_B2 = 0.999
ADAM_EPS = 1e-08
ADAM_WD = 0.01
ADAM_STEP = 10

VMEM_LIMIT = 56 * 1024 * 1024

WEIGHTS = ['norm1_g', 'w_in', 'ssd_conv_w', 'ssd_conv_b', 'ssd_dt_bias', 'ssd_a_log', 'ssd_d', 'ssd_norm_g',
           'lru_conv_w', 'lru_conv_b', 'lru_w_a', 'lru_b_a', 'lru_w_x', 'lru_b_x', 'lru_lambda', 'lru_norm_g',
           'fox_b_f', 'fox_norm_g', 'w_out', 'norm2_g', 'w_gate', 'w_up', 'w_down', 'norm3_g', 'w_ple_gate',
           'b_ple_gate', 'w_ple_proj', 'final_norm_g']
BIG = {'w_in': (1, (DEPTH, D_MODEL, IN_COLS)), 'w_out': (1, (DEPTH, D_MODEL, D_MODEL)),
       'w_gate': (2, (DEPTH, D_MODEL, D_FF)), 'w_up': (2, (DEPTH, D_MODEL, D_FF)),
       'w_down': (1, (DEPTH, D_FF, D_MODEL)), 'w_ple_gate': (1, (DEPTH, D_MODEL, D_MODEL)),
       'w_ple_proj': (2, (DEPTH, PLE_DIM, D_MODEL))}
CONV_SHARDED = {'ssd_conv_w': (DEPTH, 4, 896), 'lru_conv_w': (DEPTH, 4, 256)}


def _dot(a, b):
    return jnp.dot(a, b, preferred_element_type=F32)


def _dot_nt(a, b):
    return lax.dot_general(a, b, (((1,), (1,)), ((), ())), preferred_element_type=F32)


def _dot_tn(a, b):
    return lax.dot_general(a, b, (((0,), (0,)), ((), ())), preferred_element_type=F32)


def _params(sem):
    return pltpu.CompilerParams(dimension_semantics=sem, vmem_limit_bytes=VMEM_LIMIT)


def _pick_tile(n, cap):
    if n <= cap:
        return n
    best = 128
    for t in range(128, cap + 1, 128):
        if n % t == 0:
            best = t
    assert n % best == 0, (n, cap)
    return best


def _matmul(a, b, mode, name, out_dtype=F32, side=None, shard_n=None):
    if mode == 'tn':
        k_dim, m_dim = a.shape
        n_dim = b.shape[1]
    else:
        m_dim, k_dim = a.shape
        n_dim = b.shape[1] if mode == 'nn' else b.shape[0]
    tm = _pick_tile(m_dim, 1024)
    tn = 2 * shard_n if shard_n else _pick_tile(n_dim, 1536 if mode != 'tn' else 1024)
    tk = _pick_tile(k_dim, 3072 if mode != 'tn' else 2048)
    nk = k_dim // tk
    grid = (n_dim // tn, m_dim // tm, nk)

    n_in = len(side.inputs) if side else 0
    n_out = len(side.out_shape) if side else 0

    n_acc = 1 if nk > 1 else 0

    def body(*refs):
        a_ref, b_ref, o_ref = refs[0], refs[1], refs[2 + n_in]
        acc_ref = refs[3 + n_in + n_out] if n_acc else None
        if side is not None:
            side.run(refs[2:2 + n_in] + refs[3 + n_in:3 + n_in + n_out] + refs[3 + n_acc + n_in + n_out:],
                     *_grid_ends(grid))
        kk = pl.program_id(2)
        prod = {'nn': _dot, 'nt': _dot_nt, 'tn': _dot_tn}[mode](a_ref[...], b_ref[...])

        def write(res):
            if shard_n:
                for q in range(tn // shard_n):
                    o_ref[q] = res[:, q * shard_n:(q + 1) * shard_n].astype(o_ref.dtype)
            else:
                o_ref[...] = res.astype(o_ref.dtype)

        if nk == 1:
            write(prod)
            return

        @pl.when(kk == 0)
        def _():
            acc_ref[...] = prod

        @pl.when(jnp.logical_and(kk > 0, kk < nk - 1))
        def _():
            acc_ref[...] += prod

        @pl.when(kk == nk - 1)
        def _():
            write(acc_ref[...] + prod)

    if mode == 'nn':
        a_spec = pl.BlockSpec((tm, tk), lambda j, i, k: (i, k))
        b_spec = pl.BlockSpec((tk, tn), lambda j, i, k: (k, j))
    elif mode == 'nt':
        a_spec = pl.BlockSpec((tm, tk), lambda j, i, k: (i, k))
        b_spec = pl.BlockSpec((tn, tk), lambda j, i, k: (j, k))
    else:
        a_spec = pl.BlockSpec((tk, tm), lambda j, i, k: (k, i))
        b_spec = pl.BlockSpec((tk, tn), lambda j, i, k: (k, j))
    side_in, side_out = side.specs() if side else ([], [])
    if shard_n:
        out_shape = jax.ShapeDtypeStruct((n_dim // shard_n, m_dim, shard_n), out_dtype)
        out_spec = pl.BlockSpec((tn // shard_n, tm, shard_n), lambda j, i, k: (j, i, 0))
    else:
        out_shape = jax.ShapeDtypeStruct((m_dim, n_dim), out_dtype)
        out_spec = pl.BlockSpec((tm, tn), lambda j, i, k: (i, j))
    res = pl.pallas_call(
        body, name=name, grid=grid,
        out_shape=[out_shape] + (side.out_shape if side else []),
        in_specs=[a_spec, b_spec] + side_in,
        out_specs=[out_spec] + side_out,
        scratch_shapes=[pltpu.VMEM((tm, tn), F32)] * n_acc + (side.sems if side else []),
        compiler_params=_params(("arbitrary", "arbitrary", "arbitrary") if side
                                else ("parallel", "parallel", "arbitrary")),
    )(a, b, *(side.inputs if side else []))
    return res if side else res[0]


def _ffn_in(u, w_gu):
    tokens, k_dim = u.shape
    width = w_gu.shape[1] // 2
    tm, tn = _pick_tile(tokens, 512), _pick_tile(width, 768)
    nj = width // tn

    def body(a_ref, bg_ref, bu_ref, g_ref, u_ref, act_ref):
        a = a_ref[...]
        g = _dot(a, bg_ref[...]).astype(BF16)
        up = _dot(a, bu_ref[...]).astype(BF16)
        g_ref[...] = g
        u_ref[...] = up
        act_ref[...] = (_silu(g.astype(F32)) * up.astype(F32)).astype(BF16)

    out = jax.ShapeDtypeStruct((tokens, width), BF16)
    tile = pl.BlockSpec((tm, tn), lambda j, i: (i, j))
    return pl.pallas_call(
        body, name="ffn_in", grid=(nj, tokens // tm),
        out_shape=[out, out, out],
        in_specs=[pl.BlockSpec((tm, k_dim), lambda j, i: (i, 0)),
                  pl.BlockSpec((k_dim, tn), lambda j, i: (0, j)),
                  pl.BlockSpec((k_dim, tn), lambda j, i: (0, j + nj))],
        out_specs=[tile, tile, tile],
        compiler_params=_params(("parallel", "parallel")),
    )(u, w_gu, w_gu)


def _rowcall(name, fn, tiled, consts, outs, accs=(), tile=512, scratch=()):
    specs, arrays = [], []
    for t in tiled:
        if isinstance(t, tuple):
            arr, width, blk = t
            specs.append(pl.BlockSpec((tile, width), functools.partial(lambda i, blk: (i, blk), blk=blk)))
        else:
            arr = t
            specs.append(pl.BlockSpec((tile, arr.shape[1]), lambda i: (i, 0)))
        arrays.append(arr)
    rows = arrays[0].shape[0]
    assert rows % tile == 0, (name, rows, tile)
    for c in consts:
        specs.append(pl.BlockSpec(c.shape, lambda i: (0, 0)))
        arrays.append(c)
    n_in, n_out, n_acc = len(arrays), len(outs), len(accs)
    out_shape = [jax.ShapeDtypeStruct((rows, c), dt) for c, dt in outs]
    out_specs = [pl.BlockSpec((tile, c), lambda i: (i, 0)) for c, _ in outs]
    out_shape += [jax.ShapeDtypeStruct(s, dt) for s, dt in accs]
    out_specs += [pl.BlockSpec(s, lambda i: (0, 0)) for s, _ in accs]

    def body(*refs):
        ins = [r[...] for r in refs[:n_in]]
        out_refs = refs[n_in:n_in + n_out]
        acc_refs = refs[n_in + n_out:n_in + n_out + n_acc]
        scr = refs[n_in + n_out + n_acc:]
        res = fn(*ins, *scr)
        if not isinstance(res, (tuple, list)):
            res = (res,)
        assert len(res) == n_out + n_acc, (name, len(res))
        for r, v in zip(out_refs, res[:n_out]):
            r[...] = v.astype(r.dtype)
        if n_acc:
            first = pl.program_id(0) == 0

            @pl.when(first)
            def _():
                for r, v in zip(acc_refs, res[n_out:]):
                    r[...] = v.astype(r.dtype)

            @pl.when(jnp.logical_not(first))
            def _():
                for r, v in zip(acc_refs, res[n_out:]):
                    r[...] += v.astype(r.dtype)

    res = pl.pallas_call(
        body, name=name, grid=(rows // tile,),
        out_shape=out_shape, in_specs=specs, out_specs=out_specs,
        scratch_shapes=list(scratch),
        compiler_params=_params(("arbitrary",)),
    )(*arrays)
    return res


def _sigmoid(x):
    return 1.0 / (1.0 + jnp.exp(-x))


def _softplus(x):
    return jnp.maximum(x, 0.0) + jnp.log(1.0 + jnp.exp(-jnp.abs(x)))


def _silu(x):
    return x * _sigmoid(x)


def _dsilu(x):
    s = _sigmoid(x)
    return s * (1.0 + x * (1.0 - s))


_GELU_C = math.sqrt(2.0 / math.pi)


def _gelu(x):
    return 0.5 * x * (1.0 + jnp.tanh(_GELU_C * (x + 0.044715 * x * x * x)))


def _dgelu(x):
    t = jnp.tanh(_GELU_C * (x + 0.044715 * x * x * x))
    return 0.5 * (1.0 + t) + 0.5 * x * (1.0 - t * t) * _GELU_C * (1.0 + 3.0 * 0.044715 * x * x)


def _neg_expm1(x):
    series = -x * (1.0 + x * (0.5 + x * (1.0 / 6.0 + x * (1.0 / 24.0 + x * (1.0 / 120.0)))))
    return jnp.where(x > -0.03, series, 1.0 - jnp.exp(x))


def _rms(x, g):
    r = lax.rsqrt(jnp.mean(x * x, axis=-1, keepdims=True) + EPS)
    return x * r * g


def _rms_bwd(x, g, dy):
    r = lax.rsqrt(jnp.mean(x * x, axis=-1, keepdims=True) + EPS)
    xhat = x * r
    dg = jnp.sum(dy * xhat, axis=0, keepdims=True)
    dxhat = dy * g
    dx = r * (dxhat - xhat * jnp.mean(dxhat * xhat, axis=-1, keepdims=True))
    return dx, dg


def _row_iota(shape):
    return lax.broadcasted_iota(jnp.int32, shape, 0)


def _shift_down(x, j):
    if j == 0:
        return x
    return jnp.where(_row_iota(x.shape) >= j, pltpu.roll(x, j, 0), 0.0)


def _shift_up(x, j):
    if j == 0:
        return x
    n = x.shape[0]
    return jnp.where(_row_iota(x.shape) < n - j, pltpu.roll(x, n - j, 0), 0.0)


def _conv(x, w, b):
    y = b + w[3:4, :] * x
    for k in range(3):
        y = y + w[k:k + 1, :] * _shift_down(x, 3 - k)
    return y


def _conv_bwd(x, w, dy):
    dx = w[3:4, :] * dy
    dws = []
    for k in range(3):
        dx = dx + w[k:k + 1, :] * _shift_up(dy, 3 - k)
        dws.append(jnp.sum(dy * _shift_down(x, 3 - k), axis=0, keepdims=True))
    dws.append(jnp.sum(dy * x, axis=0, keepdims=True))
    return dx, jnp.concatenate(dws, axis=0), jnp.sum(dy, axis=0, keepdims=True)


def _split3(x):
    hi = x.astype(BF16)
    r1 = x - hi.astype(F32)
    mid = r1.astype(BF16)
    lo = (r1 - mid.astype(F32)).astype(BF16)
    return hi, mid, lo


def _tri_dot(tri, x):
    hi, mid, lo = _split3(x)
    return _dot(tri, hi) + _dot(tri, mid) + _dot(tri, lo)


def _cumsum_rows(x):
    n = x.shape[0] // CHUNK
    r = lax.broadcasted_iota(jnp.int32, (CHUNK, CHUNK), 0)
    c = lax.broadcasted_iota(jnp.int32, (CHUNK, CHUNK), 1)
    tri = (r >= c).astype(BF16)
    carry = jnp.zeros((1, x.shape[1]), F32)
    cums, prevs, ends = [], [], []
    for i in range(n):
        blk = _tri_dot(tri, x[i * CHUNK:(i + 1) * CHUNK]) + carry
        prevs.append(jnp.broadcast_to(carry, blk.shape))
        carry = blk[CHUNK - 1:CHUNK, :]
        ends.append(jnp.broadcast_to(carry, blk.shape))
        cums.append(blk)
    return jnp.concatenate(cums, 0), jnp.concatenate(prevs, 0), jnp.concatenate(ends, 0)


def _rev_cumsum_rows(x):
    n = x.shape[0] // CHUNK
    r = lax.broadcasted_iota(jnp.int32, (CHUNK, CHUNK), 0)
    c = lax.broadcasted_iota(jnp.int32, (CHUNK, CHUNK), 1)
    tri = (r <= c).astype(BF16)
    carry = jnp.zeros((1, x.shape[1]), F32)
    local, whole = [None] * n, [None] * n
    for i in range(n - 1, -1, -1):
        local[i] = _tri_dot(tri, x[i * CHUNK:(i + 1) * CHUNK])
        whole[i] = local[i] + carry
        carry = whole[i][0:1, :]
    return jnp.concatenate(local, 0), jnp.concatenate(whole, 0)


def _convsilu_fwd(name, seg, seq, w, b, dtype):
    return _rowcall(name, lambda raw, w, b: _silu(_conv(raw, w, b)), [seg], [w, b], [(seg[1], dtype)], tile=seq)[0]


def _convsilu_bwd(name, seg, seq, w, b, dy):
    def fn(raw, dy, w, b):
        return _conv_bwd(raw, w, dy * _dsilu(_conv(raw, w, b)))

    width = seg[1]
    return _rowcall(name, fn, [seg, dy], [w, b], [(width, BF16)], [((4, width), F32), ((1, width), F32)], tile=seq)


def _small_fwd(seg, seq, bias128, alog128):
    def fn(small, bias, alog):
        lane = lax.broadcasted_iota(jnp.int32, small.shape, 1)
        a = -jnp.exp(alog)
        s = small + bias
        dt = _softplus(s)
        logf = -_softplus(-s)
        pre = jnp.where(lane < SSD_HEADS, a * dt, jnp.where(lane < 2 * SSD_HEADS, logf, 0.0))
        cum, prev, end = _cumsum_rows(pre)
        return dt, cum, prev, end

    return _rowcall("small_fwd", fn, [seg], [bias128, alog128], [(128, F32)] * 4, tile=seq)


def _small_bwd(seg, seq, dcum, dcq, dend, ddt, dt_arr, bias128, alog128):
    def fn(small, dcum, dcq, dend, ddt, dt_arr, bias, alog):
        lane = lax.broadcasted_iota(jnp.int32, small.shape, 1)
        for pair in range(FOX_PAIRS):
            first = SSD_HEADS + 2 * pair
            moved = pltpu.roll(dcq[:, pair * 128:(pair + 1) * 128], first, 1)
            dcum = dcum + jnp.where(jnp.logical_or(lane == first, lane == first + 1), moved, 0.0)
        a = -jnp.exp(alog)
        sig = _sigmoid(small + bias)
        local, whole = _rev_cumsum_rows(dcum)
        dadt = local + dend
        d_dt = ddt + a * dadt
        ds = jnp.where(lane < SSD_HEADS, d_dt * sig, jnp.where(lane < 2 * SSD_HEADS, whole * (1.0 - sig), 0.0))
        da = jnp.sum(jnp.where(lane < SSD_HEADS, dadt * dt_arr, 0.0), axis=0, keepdims=True)
        return ds, jnp.sum(ds, axis=0, keepdims=True), da * a

    return _rowcall("small_bwd", fn, [seg, dcum, dcq, dend, ddt, dt_arr], [bias128, alog128], [(128, BF16)],
                    [((1, 128), F32), ((1, 128), F32)], tile=seq)


HEAD_PAIRS = SSD_HEADS // 2


def _ssd_specs(nc, reverse):
    def at(c):
        return nc - 1 - c if reverse else c

    x_spec = pl.BlockSpec((1, CHUNK, SSD_WIDTH), lambda b, c: (b, at(c), 0))
    bc_spec = pl.BlockSpec((1, CHUNK, 256), lambda b, c: (b, at(c), 0))
    col_spec = pl.BlockSpec((1, CHUNK, 128), lambda b, c: (b, at(c), 0))
    row_spec = pl.BlockSpec((1, 8, CHUNK), lambda b, c: (b, 0, at(c)))
    st_spec = pl.BlockSpec((1, 1, HEAD_PAIRS, SSD_STATE, 128), lambda b, c: (b, at(c), 0, 0, 0))
    return x_spec, bc_spec, col_spec, row_spec, st_spec


def _ssd_head(h, dtb, acb, apb, aeb, arb):
    return dtb[:, h:h + 1], acb[:, h:h + 1], apb[:, h:h + 1], aeb[:, h:h + 1], arb[h:h + 1, :]


def _ssd_fwd(x, bm, cm, dt_arr, cum, prev, end, a_row):
    nb, seq, _ = x.shape
    nc = seq // CHUNK
    x_spec, bc_spec, col_spec, row_spec, st_spec = _ssd_specs(nc, False)

    def body(x_ref, b_ref, c_ref, dt_ref, ac_ref, ap_ref, ae_ref, ar_ref, y_ref, st_ref, s_scr):
        @pl.when(pl.program_id(1) == 0)
        def _():
            s_scr[...] = jnp.zeros_like(s_scr)

        causal = (lax.broadcasted_iota(jnp.int32, (CHUNK, CHUNK), 0)
                  >= lax.broadcasted_iota(jnp.int32, (CHUNK, CHUNK), 1))
        low = lax.broadcasted_iota(jnp.int32, (CHUNK, 128), 1) < HEAD_DIM
        cols = (dt_ref[0], ac_ref[0], ap_ref[0], ae_ref[0], ar_ref[0])
        bcs = [b_ref[0, :, g * 128:(g + 1) * 128] for g in range(2)]
        ccs = [c_ref[0, :, g * 128:(g + 1) * 128] for g in range(2)]
        ms = [_dot_nt(ccs[g], bcs[g]) for g in range(2)]
        for pi in range(HEAD_PAIRS):
            x2 = x_ref[0, :, pi * 128:(pi + 1) * 128]
            dt2 = jnp.where(low, cols[0][:, 2 * pi:2 * pi + 1], cols[0][:, 2 * pi + 1:2 * pi + 2])
            xdt_f = x2 * dt2
            xdt = xdt_f.astype(BF16)
            sprev = s_scr[pi]
            st_ref[0, 0, pi] = sprev
            spb = sprev.astype(BF16)
            ys, us = [], []
            for h in (2 * pi, 2 * pi + 1):
                g = h // 3
                _, ac, ap, ae, ar = _ssd_head(h, *cols)
                lm = jnp.exp(jnp.where(causal, ac - ar, NEG))
                gm = (ms[g] * lm).astype(BF16)
                ys.append(_dot(gm, xdt) + jnp.exp(ac - ap) * _dot(ccs[g], spb))
                us.append(jnp.exp(ae[0:1, :] - ap[0:1, :]) * sprev
                          + _dot_tn(bcs[g], (xdt_f * jnp.exp(ae - ac)).astype(BF16)))
            y_ref[0, :, pi * 128:(pi + 1) * 128] = jnp.where(low, ys[0], ys[1])
            s_scr[pi] = jnp.where(lax.broadcasted_iota(jnp.int32, (SSD_STATE, 128), 1) < HEAD_DIM, us[0], us[1])

    return pl.pallas_call(
        body, name="ssd_fwd", grid=(nb, nc),
        out_shape=[jax.ShapeDtypeStruct(x.shape, F32),
                   jax.ShapeDtypeStruct((nb, nc, HEAD_PAIRS, SSD_STATE, 128), F32)],
        in_specs=[x_spec, bc_spec, bc_spec, col_spec, col_spec, col_spec, col_spec, row_spec],
        out_specs=[x_spec, st_spec],
        scratch_shapes=[pltpu.VMEM((HEAD_PAIRS, SSD_STATE, 128), F32)],
        compiler_params=_params(("parallel", "arbitrary")),
    )(x, bm, cm, dt_arr, cum, prev, end, a_row)


def _ssd_bwd(x_h, bm, cm, dt_arr, cum, prev, end, a_row, states, dy_h):
    nb, seq, _ = x_h.shape
    nc = seq // CHUNK
    x_spec, bc_spec, col_spec, row_spec, st_spec = _ssd_specs(nc, True)

    def body(x_ref, b_ref, c_ref, dt_ref, ac_ref, ap_ref, ae_ref, ar_ref, st_ref, dy_ref,
             dx_ref, db_ref, dc_ref, da_ref, dend_ref, ddt_ref, ds_scr):
        @pl.when(pl.program_id(1) == 0)
        def _():
            ds_scr[...] = jnp.zeros_like(ds_scr)

        causal = (lax.broadcasted_iota(jnp.int32, (CHUNK, CHUNK), 0)
                  >= lax.broadcasted_iota(jnp.int32, (CHUNK, CHUNK), 1))
        lane = lax.broadcasted_iota(jnp.int32, (CHUNK, 128), 1)
        low = lane < HEAD_DIM
        low_state = lax.broadcasted_iota(jnp.int32, (SSD_STATE, 128), 1) < HEAD_DIM
        cols = (dt_ref[0], ac_ref[0], ap_ref[0], ae_ref[0], ar_ref[0])
        bcs = [b_ref[0, :, g * 128:(g + 1) * 128] for g in range(2)]
        ccs = [c_ref[0, :, g * 128:(g + 1) * 128] for g in range(2)]
        ms = [_dot_nt(ccs[g], bcs[g]) for g in range(2)]
        dms = [jnp.zeros((CHUNK, CHUNK), F32) for _ in range(2)]
        dc_accs = [jnp.zeros((CHUNK, SSD_STATE), F32) for _ in range(2)]
        db_accs = [jnp.zeros((CHUNK, SSD_STATE), F32) for _ in range(2)]
        da_blk = jnp.zeros((CHUNK, 128), F32)
        dend_blk = jnp.zeros((CHUNK, 128), F32)
        ddt_blk = jnp.zeros((CHUNK, 128), F32)
        for pi in range(HEAD_PAIRS):
            x2 = x_ref[0, :, pi * 128:(pi + 1) * 128]
            dy2 = dy_ref[0, :, pi * 128:(pi + 1) * 128]
            dt2 = jnp.where(low, cols[0][:, 2 * pi:2 * pi + 1], cols[0][:, 2 * pi + 1:2 * pi + 2])
            xdt_f = x2 * dt2
            xdt = xdt_f.astype(BF16)
            dyb = dy2.astype(BF16)
            dsn = ds_scr[pi]
            dsb = dsn.astype(BF16)
            sprev_f = st_ref[0, 0, pi]
            sprev = sprev_f.astype(BF16)
            dxdts, dss = [], []
            for h in (2 * pi, 2 * pi + 1):
                g = h // 3
                mine = low if h % 2 == 0 else jnp.logical_not(low)
                _, ac, ap, ae, ar = _ssd_head(h, *cols)
                bc, cc, m = bcs[g], ccs[g], ms[g]
                lm = jnp.exp(jnp.where(causal, ac - ar, NEG))
                gm = (m * lm).astype(BF16)
                dy_m = jnp.where(mine, dy2, 0.0)
                dyb_m = dy_m.astype(BF16)
                xdt_m = jnp.where(mine, xdt_f, 0.0)
                e_in = jnp.exp(ac - ap)
                f_out = jnp.exp(ae - ac)
                whole = jnp.exp(ae[0:1, :] - ap[0:1, :])
                dg = _dot_nt(dyb_m, xdt)
                dxdt_off = f_out * _dot(bc, dsb)
                dxdt = _dot_tn(gm, dyb) + dxdt_off
                dmj = dg * lm
                dms[g] = dms[g] + dmj
                dc_accs[g] = dc_accs[g] + e_in * _dot_nt(dyb_m, sprev)
                db_accs[g] = db_accs[g] + f_out * _dot_nt(xdt_m.astype(BF16), dsb)
                dss.append(whole * dsn + _dot_tn(cc, (dy2 * e_in).astype(BF16)))
                wmat = dmj * m
                r_in = jnp.sum(dy_m * (e_in * _dot(cc, sprev)), axis=1, keepdims=True)
                q_out = jnp.sum(xdt_m * dxdt_off, axis=1, keepdims=True)
                daj = (jnp.sum(wmat, axis=1, keepdims=True) - jnp.sum(wmat.T, axis=1, keepdims=True)
                       + r_in - q_out)
                cross = jnp.where(low_state if h % 2 == 0 else jnp.logical_not(low_state), dsn * sprev_f, 0.0)
                dendj = (jnp.sum(q_out, axis=0, keepdims=True)
                         + whole * jnp.sum(jnp.sum(cross, axis=1, keepdims=True), axis=0, keepdims=True))
                ddtj = jnp.sum(jnp.where(mine, dxdt * x2, 0.0), axis=1, keepdims=True)
                dxdts.append(dxdt)
                da_blk = jnp.where(lane == h, daj, da_blk)
                dend_blk = jnp.where(lane == h, dendj, dend_blk)
                ddt_blk = jnp.where(lane == h, ddtj, ddt_blk)
            dx_ref[0, :, pi * 128:(pi + 1) * 128] = jnp.where(low, dxdts[0], dxdts[1]) * dt2
            ds_scr[pi] = jnp.where(low_state, dss[0], dss[1])
        for g in range(2):
            dmb = dms[g].astype(BF16)
            dc_ref[0, :, g * 128:(g + 1) * 128] = dc_accs[g] + _dot(dmb, bcs[g])
            db_ref[0, :, g * 128:(g + 1) * 128] = db_accs[g] + _dot_tn(dmb, ccs[g])
        da_ref[0] = da_blk
        dend_ref[0] = dend_blk
        ddt_ref[0] = ddt_blk

    col_shape = jax.ShapeDtypeStruct((nb, seq, 128), F32)
    return pl.pallas_call(
        body, name="ssd_bwd", grid=(nb, nc),
        out_shape=[jax.ShapeDtypeStruct(x_h.shape, F32),
                   jax.ShapeDtypeStruct((nb, seq, 256), F32), jax.ShapeDtypeStruct((nb, seq, 256), F32),
                   col_shape, col_shape, col_shape],
        in_specs=[x_spec, bc_spec, bc_spec, col_spec, col_spec, col_spec, col_spec, row_spec, st_spec, x_spec],
        out_specs=[x_spec, bc_spec, bc_spec, col_spec, col_spec, col_spec],
        scratch_shapes=[pltpu.VMEM((HEAD_PAIRS, SSD_STATE, 128), F32)],
        compiler_params=_params(("parallel", "arbitrary")),
    )(x_h, bm, cm, dt_arr, cum, prev, end, a_row, states, dy_h)


def _lru_gates(xl, wa, ba, wx, bx, lam):
    xb = xl.astype(BF16)
    r = _sigmoid(_dot(xb, wa) + ba)
    i = _sigmoid(_dot(xb, wx) + bx)
    sp = _softplus(-lam)
    log_a = -LRU_C * r * sp
    a = jnp.exp(log_a)
    mult = jnp.sqrt(_neg_expm1(2.0 * log_a))
    return r, i, sp, log_a, a, mult


def _scan_chunks(a_ref, u_ref, h_ref, seq, reverse):
    nc = seq // CHUNK
    width = a_ref.shape[1]
    row = lax.broadcasted_iota(jnp.int32, (CHUNK, width), 0)

    def chunk(ci, carry):
        c = nc - 1 - ci if reverse else ci
        rows = pl.ds(pl.multiple_of(c * CHUNK, CHUNK), CHUNK)
        av, bv = a_ref[rows, :], u_ref[rows, :]
        d = 1
        while d < CHUNK:
            if reverse:
                keep = row < CHUNK - d
                a_sh = jnp.where(keep, pltpu.roll(av, CHUNK - d, 0), 1.0)
                b_sh = jnp.where(keep, pltpu.roll(bv, CHUNK - d, 0), 0.0)
            else:
                keep = row >= d
                a_sh = jnp.where(keep, pltpu.roll(av, d, 0), 1.0)
                b_sh = jnp.where(keep, pltpu.roll(bv, d, 0), 0.0)
            bv = av * b_sh + bv
            av = av * a_sh
            d *= 2
        hv = bv + av * carry
        h_ref[rows, :] = hv
        return hv[0:1, :] if reverse else hv[CHUNK - 1:CHUNK, :]

    lax.fori_loop(0, nc, chunk, jnp.zeros((1, width), F32))


def _lru_fwd(proj, seq, cw, cb, wa, ba, wx, bx, lam):
    def fn(raw, cw, cb, wa, ba, wx, bx, lam, a_scr, u_scr, h_scr):
        xl = _conv(raw, cw, cb)
        r, i, sp, log_a, a, mult = _lru_gates(xl, wa, ba, wx, bx, lam)
        a_scr[...] = a
        u_scr[...] = mult * (i * xl)
        _scan_chunks(a_scr, u_scr, h_scr, seq, reverse=False)
        return h_scr[...], xl

    return _rowcall("lru_fwd", fn, [(proj, 256, 2)], [cw, cb, wa, ba, wx, bx, lam],
                    [(256, F32), (256, F32)], tile=seq,
                    scratch=[pltpu.VMEM((seq, 256), F32)] * 3)


def _lru_bwd(proj, seq, xl_all, h_all, dh_all, cw, cb, wa, ba, wx, bx, lam):
    def fn(raw, xl, hseq, dh, cw, cb, wa, ba, wx, bx, lam, a_scr, u_scr, h_scr):
        r, i, sp, log_a, a, mult = _lru_gates(xl, wa, ba, wx, bx, lam)
        a_scr[...] = _shift_up(a, 1)
        u_scr[...] = dh
        _scan_chunks(a_scr, u_scr, h_scr, seq, reverse=True)
        dht = h_scr[...]
        da = dht * _shift_down(hseq, 1)
        gated = i * xl
        dgated = dht * mult
        dmult = dht * gated
        dlog_a = da * a - dmult * (a * a) / mult
        dr = dlog_a * (-LRU_C * sp)
        dsp = jnp.sum(dlog_a * (-LRU_C * r), axis=0, keepdims=True)
        dlam = -dsp * _sigmoid(-lam)
        dpa = dr * r * (1.0 - r)
        dpx = (dgated * xl) * i * (1.0 - i)
        dpa_b, dpx_b = dpa.astype(BF16), dpx.astype(BF16)
        dxl = dgated * i + _dot_nt(dpa_b, wa) + _dot_nt(dpx_b, wx)
        xb = xl.astype(BF16)
        dwa = _dot_tn(xb, dpa_b)
        dwx = _dot_tn(xb, dpx_b)
        draw, dcw, dcb = _conv_bwd(raw, cw, dxl)
        return (draw, dcw, dcb, dwa, jnp.sum(dpa, axis=0, keepdims=True), dwx,
                jnp.sum(dpx, axis=0, keepdims=True), dlam)

    return _rowcall("lru_bwd", fn, [(proj, 256, 2), xl_all, h_all, dh_all], [cw, cb, wa, ba, wx, bx, lam],
                    [(256, BF16)],
                    [((4, 256), F32), ((1, 256), F32), ((256, 256), F32), ((1, 256), F32), ((256, 256), F32),
                     ((1, 256), F32), ((1, 256), F32)],
                    tile=seq, scratch=[pltpu.VMEM((seq, 256), F32)] * 3)


FOX_SCALE = HEAD_DIM ** -0.5
FOX_BLOCK = 1024


class _Side:
    def __init__(self, inputs, out_shape, sems, build):
        self.inputs, self.out_shape, self.sems, self.build = list(inputs), list(out_shape), list(sems), build

    def specs(self):
        any_spec = pl.BlockSpec(memory_space=pl.ANY)
        return [any_spec] * len(self.inputs), [any_spec] * len(self.out_shape)

    def run(self, refs, first, last):
        n_in, n_out = len(self.inputs), len(self.out_shape)
        in_refs, out_refs, sem_refs = refs[:n_in], refs[n_in:n_in + n_out], refs[n_in + n_out:]

        @pl.when(first)
        def _():
            for cp in self.build(in_refs, out_refs, sem_refs):
                cp.start()

        @pl.when(last)
        def _():
            for cp in self.build(in_refs, out_refs, sem_refs):
                cp.wait()


def _grid_ends(grid):
    ids = [pl.program_id(a) for a in range(len(grid))]
    first = functools.reduce(jnp.logical_and, [i == 0 for i in ids])
    last = functools.reduce(jnp.logical_and, [i == n - 1 for i, n in zip(ids, grid)])
    return first, last


Q_BLK, K_BLK, V_BLK = 1920 // 128, 2304 // 128, 2688 // 128
FOX_PAIRS = FOX_HEADS // 2


def _fox_bias(cum, nb, seq):
    cf = cum.reshape(nb, seq, 128)[:, :, SSD_HEADS:SSD_HEADS + FOX_HEADS]
    cols = jnp.pad(cf.reshape(nb * seq, FOX_PAIRS, 2), ((0, 0), (0, 0), (0, 126))).reshape(nb * seq, 384)
    rows = jnp.pad(cf.transpose(0, 2, 1).reshape(nb, FOX_PAIRS, 2, seq), ((0, 0), (0, 0), (0, 6), (0, 0)))
    return cols, rows


def _fox_fwd(proj, bias_cols, bias_rows, nb, seq, side=None):
    tb = min(FOX_BLOCK, seq)
    nq = seq // tb
    grid = (nb, FOX_PAIRS, nq)
    n_in = len(side.inputs) if side else 0

    def body(*refs):
        q_ref, k_ref, v_ref, cq_ref, ck_ref = refs[:5]
        o_ref, lse_ref = refs[5 + n_in:7 + n_in]
        if side is not None:
            side.run(refs[5:5 + n_in] + refs[7 + n_in:], *_grid_ends(grid))
        qi = pl.program_id(2)
        low = lax.broadcasted_iota(jnp.int32, (tb, 128), 1) < HEAD_DIM
        q2 = q_ref[...] * FOX_SCALE
        qm = [jnp.where(low, q2, 0.0).astype(BF16), jnp.where(low, 0.0, q2).astype(BF16)]
        cqs = [cq_ref[:, 0:1], cq_ref[:, 1:2]]

        def block(j, carry, diagonal):
            cols = pl.ds(pl.multiple_of(j * tb, tb), tb)
            k2 = k_ref[cols, :].astype(BF16)
            v2 = v_ref[cols, :].astype(BF16)
            new = []
            for hh in range(2):
                m_i, l_i, acc = carry[hh]
                s = _dot_nt(qm[hh], k2) + cqs[hh] - ck_ref[0, 0, hh:hh + 1, cols]
                if diagonal:
                    s = jnp.where(lax.broadcasted_iota(jnp.int32, (tb, tb), 0)
                                  >= lax.broadcasted_iota(jnp.int32, (tb, tb), 1), s, NEG)
                m_new = jnp.maximum(m_i, jnp.max(s, axis=1, keepdims=True))
                p = jnp.exp(s - m_new)
                alpha = jnp.exp(m_i - m_new)
                new.append((m_new, alpha * l_i + jnp.sum(p, axis=1, keepdims=True),
                            alpha * acc + _dot(p.astype(BF16), v2)))
            return tuple(new)

        one = (jnp.full((tb, 1), NEG, F32), jnp.zeros((tb, 1), F32), jnp.zeros((tb, 128), F32))
        carry = lax.fori_loop(0, qi, lambda j, cr: block(j, cr, False), (one, one))
        (m0, l0, a0), (m1, l1, a1) = block(qi, carry, True)
        o_ref[...] = jnp.where(low, a0 / l0, a1 / l1)
        lse_ref[...] = jnp.where(low, m0 + jnp.log(l0), m1 + jnp.log(l1))

    def blk(first):
        return pl.BlockSpec((tb, 128), lambda b, p, i: (b * nq + i, first + p))

    def seq_blk(first):
        return pl.BlockSpec((seq, 128), lambda b, p, i: (b, first + p))

    row_spec = pl.BlockSpec((1, 1, 8, seq), lambda b, p, i: (b, p, 0, 0))
    side_in, side_out = side.specs() if side else ([], [])
    shape = jax.ShapeDtypeStruct((nb * seq, FOX_WIDTH), F32)
    return pl.pallas_call(
        body, name="fox_fwd", grid=grid,
        out_shape=[shape, shape] + (side.out_shape if side else []),
        in_specs=[blk(Q_BLK), seq_blk(K_BLK), seq_blk(V_BLK), blk(0), row_spec] + side_in,
        out_specs=[blk(0), blk(0)] + side_out,
        scratch_shapes=side.sems if side else [],
        compiler_params=_params(("arbitrary", "arbitrary", "arbitrary")),
    )(proj, proj, proj, bias_cols, bias_rows, *(side.inputs if side else []))


def _fox_bwd(proj, o, lse, do, bias_cols, bias_rows, nb, seq, side=None):
    tb = min(FOX_BLOCK, seq)
    nq = seq // tb
    grid = (nb, FOX_PAIRS, nq)
    n_in = len(side.inputs) if side else 0

    def body(*refs):
        q_ref, k_ref, v_ref, o_ref, lse_ref, do_ref, cq_ref, ck_ref = refs[:8]
        dq_ref, dk_ref, dv_ref, dcum_ref, dcq_ref = refs[8 + n_in:13 + n_in]
        if side is not None:
            side.run(refs[8:8 + n_in] + refs[13 + n_in:], *_grid_ends(grid))
        kj = pl.program_id(2)

        @pl.when(kj == 0)
        def _():
            dq_ref[...] = jnp.zeros_like(dq_ref)
            dcq_ref[...] = jnp.zeros_like(dcq_ref)

        lane = lax.broadcasted_iota(jnp.int32, (tb, 128), 1)
        low = lane < HEAD_DIM
        mine = [low, jnp.logical_not(low)]
        k2 = k_ref[...]
        kb = k2.astype(BF16)
        km = [jnp.where(mine[hh], k2, 0.0).astype(BF16) for hh in range(2)]
        vb = v_ref[...].astype(BF16)

        def block(i, carry, diagonal):
            dk, dv, c0, c1 = carry
            csum = [c0, c1]
            rows = pl.ds(pl.multiple_of(i * tb, tb), tb)
            q2 = q_ref[rows, :] * FOX_SCALE
            do2 = do_ref[rows, :]
            prod = do2 * o_ref[rows, :]
            dq_add = jnp.zeros((tb, 128), F32)
            rsum = []
            for hh in range(2):
                qm = jnp.where(mine[hh], q2, 0.0).astype(BF16)
                dom = jnp.where(mine[hh], do2, 0.0).astype(BF16)
                delta = jnp.sum(jnp.where(mine[hh], prod, 0.0), axis=1, keepdims=True)
                s = _dot_nt(qm, kb) + cq_ref[rows, hh:hh + 1] - ck_ref[0, 0, hh:hh + 1, :]
                if diagonal:
                    s = jnp.where(lax.broadcasted_iota(jnp.int32, (tb, tb), 0)
                                  >= lax.broadcasted_iota(jnp.int32, (tb, tb), 1), s, NEG)
                p = jnp.exp(s - lse_ref[rows, HEAD_DIM * hh:HEAD_DIM * hh + 1])
                ds = p * (_dot_nt(dom, vb) - delta)
                dsb = ds.astype(BF16)
                dv = dv + _dot_tn(p.astype(BF16), dom)
                dk = dk + _dot_tn(dsb, qm)
                dq_add = dq_add + _dot(dsb, km[hh])
                rsum.append(jnp.sum(ds, axis=1, keepdims=True))
                csum[hh] = csum[hh] + jnp.sum(ds, axis=0, keepdims=True)
            dq_ref[rows, :] += dq_add * FOX_SCALE
            dcq_ref[rows, :] += jnp.where(lane == 0, rsum[0], jnp.where(lane == 1, rsum[1], 0.0))
            return dk, dv, csum[0], csum[1]

        init = (jnp.zeros((tb, 128), F32), jnp.zeros((tb, 128), F32), jnp.zeros((1, tb), F32),
                jnp.zeros((1, tb), F32))
        carry = block(kj, init, True)
        dk, dv, c0, c1 = lax.fori_loop(kj + 1, nq, lambda i, cr: block(i, cr, False), carry)
        dk_ref[...] = dk.astype(dk_ref.dtype)
        dv_ref[...] = dv.astype(dv_ref.dtype)
        row = lax.broadcasted_iota(jnp.int32, (8, tb), 0)
        dcum_ref[0, 0] = jnp.where(row == 0, -c0, jnp.where(row == 1, -c1, 0.0))

    def blk(first):
        return pl.BlockSpec((tb, 128), lambda b, p, j: (b * nq + j, first + p))

    def seq_blk(first):
        return pl.BlockSpec((seq, 128), lambda b, p, j: (b, first + p))

    row_blk = pl.BlockSpec((1, 1, 8, tb), lambda b, p, j: (b, p, 0, j))
    side_in, side_out = side.specs() if side else ([], [])
    tokens = nb * seq
    return pl.pallas_call(
        body, name="fox_bwd", grid=grid,
        out_shape=[jax.ShapeDtypeStruct((tokens, FOX_WIDTH), F32), jax.ShapeDtypeStruct((tokens, FOX_WIDTH), BF16),
                   jax.ShapeDtypeStruct((tokens, FOX_WIDTH), BF16),
                   jax.ShapeDtypeStruct((nb, FOX_PAIRS, 8, seq), F32),
                   jax.ShapeDtypeStruct((tokens, FOX_WIDTH), F32)] + (side.out_shape if side else []),
        in_specs=[seq_blk(Q_BLK), blk(K_BLK), blk(V_BLK), seq_blk(0), seq_blk(0), seq_blk(0), seq_blk(0), row_blk]
        + side_in,
        out_specs=[seq_blk(0), blk(0), blk(0), row_blk, seq_blk(0)] + side_out,
        scratch_shapes=side.sems if side else [],
        compiler_params=_params(("arbitrary", "arbitrary", "arbitrary")),
    )(proj, proj, proj, o, lse, do, bias_cols, bias_rows, *(side.inputs if side else []))


_ANY = pl.BlockSpec(memory_space=pl.ANY)


def _place():
    return lax.axis_index("x"), lax.axis_index("y"), lax.axis_index("c")


def _all_gather(shards, name):
    n = len(shards)

    def body(*refs):
        x_refs, out_refs = refs[:n], refs[n:2 * n]
        send_sems, recv_sems, local_sems = refs[2 * n:]
        x, y, c = _place()
        me, sibling = (x, y, c), (x, y, 1 - c)
        chips = [(1 - x, y), (x, 1 - y), (1 - x, 1 - y)]

        def rows(a, px, py, pc):
            return out_refs[a].at[4 * px + 2 * py + pc]

        def copy(a, k, block, to, src=None):
            return pltpu.make_async_remote_copy(
                src_ref=rows(a, *block) if src is None else src, dst_ref=rows(a, *block),
                send_sem=send_sems.at[a, k], recv_sem=recv_sems.at[a, k],
                device_id=to, device_id_type=pl.DeviceIdType.MESH)

        mine = [pltpu.make_async_copy(x_refs[a], rows(a, *me), local_sems.at[a]) for a in range(n)]
        for cp in mine:
            cp.start()
        first = []
        for a in range(n):
            first.append(copy(a, 0, me, sibling, src=x_refs[a]))
            first += [copy(a, 1 + j, me, (*chip, c), src=x_refs[a]) for j, chip in enumerate(chips)]
        for cp in first:
            cp.start()
        passed = []
        for j, chip in enumerate(chips):
            for a in range(n):
                copy(a, 1 + j, (*chip, c), me).wait_recv()
                passed.append(copy(a, 4 + j, (*chip, c), sibling))
                passed[-1].start()
        for a in range(n):
            copy(a, 0, sibling, me).wait_recv()
            for j, chip in enumerate(chips):
                copy(a, 4 + j, (*chip, 1 - c), me).wait_recv()
        for cp in first + passed:
            cp.wait_send()
        for cp in mine:
            cp.wait()

    return pl.pallas_call(
        body, name=name,
        out_shape=[jax.ShapeDtypeStruct((N_DEV,) + s.shape, s.dtype) for s in shards],
        in_specs=[_ANY] * n, out_specs=[_ANY] * n,
        scratch_shapes=[pltpu.SemaphoreType.DMA((n, 7)), pltpu.SemaphoreType.DMA((n, 7)),
                        pltpu.SemaphoreType.DMA((n,))],
    )(*shards)


def _remote(src, dst, send_sem, recv_sem, to):
    return pltpu.make_async_remote_copy(src_ref=src, dst_ref=dst, send_sem=send_sem, recv_sem=recv_sem,
                                        device_id=to, device_id_type=pl.DeviceIdType.MESH)


def _sem_pairs(n, k):
    return [pltpu.SemaphoreType.DMA((n, k)), pltpu.SemaphoreType.DMA((n, k))]


def _sibling_side(full, offsets):
    def build(g_refs, out_refs, sems):
        x, y, c = _place()
        return [_remote(g_refs[a].at[offsets[a] + 4 * (k // 2) + 2 * (k % 2) + (1 - c)], out_refs[a].at[k],
                        sems[0].at[a, k], sems[1].at[a, k], (x, y, 1 - c))
                for a in range(len(g_refs)) for k in range(4)]

    return _Side(full, [jax.ShapeDtypeStruct((4,) + f.shape[1:], f.dtype) for f in full],
                 _sem_pairs(len(full), 4), build)


def _chip_side(part):
    def build(p_refs, out_refs, sems):
        x, y, c = _place()
        peers = [(1 - x, y), (x, 1 - y), (1 - x, 1 - y)]
        return [_remote(p_refs[a].at[2 * px + py], out_refs[a].at[k], sems[0].at[a, k], sems[1].at[a, k],
                        (px, py, c))
                for a in range(len(p_refs)) for k, (px, py) in enumerate(peers)]

    return _Side(part, [jax.ShapeDtypeStruct((3,) + p.shape[1:], p.dtype) for p in part],
                 _sem_pairs(len(part), 3), build)


def _spread_side(shards):
    def build(x_refs, out_refs, sems):
        x, y, c = _place()
        targets = [(x, y, 1 - c), (1 - x, y, c), (x, 1 - y, c), (1 - x, 1 - y, c)]
        cps = []
        for a in range(len(x_refs)):
            slot = out_refs[a].at[4 * x + 2 * y + c]
            cps.append(pltpu.make_async_copy(x_refs[a], slot, sems[2].at[a]))
            cps += [_remote(x_refs[a], slot, sems[0].at[a, k], sems[1].at[a, k], to)
                    for k, to in enumerate(targets)]
        return cps

    n = len(shards)
    return _Side(shards, [jax.ShapeDtypeStruct((N_DEV,) + s.shape, s.dtype) for s in shards],
                 _sem_pairs(n, 4) + [pltpu.SemaphoreType.DMA((n,))], build)


def _pass_side(bufs):
    def build(in_refs, out_refs, sems):
        x, y, c = _place()
        chips = [(1 - x, y), (x, 1 - y), (1 - x, 1 - y)]
        return [_remote(in_refs[a].at[4 * px + 2 * py + c], out_refs[a].at[4 * px + 2 * py + c],
                        sems[0].at[a, j], sems[1].at[a, j], (x, y, 1 - c))
                for a in range(len(in_refs)) for j, (px, py) in enumerate(chips)]

    return _Side(bufs, [jax.ShapeDtypeStruct(b.shape, b.dtype) for b in bufs], _sem_pairs(len(bufs), 3), build)


def _run_side(side, name, in_place=False):
    n_in = len(side.inputs)

    def body(*refs):
        copies = side.build(refs[:n_in], refs[n_in:n_in + len(side.out_shape)],
                            refs[n_in + len(side.out_shape):])
        for cp in copies:
            cp.start()
        for cp in copies:
            cp.wait()

    in_specs, out_specs = side.specs()
    return pl.pallas_call(
        body, name=name, out_shape=side.out_shape, in_specs=in_specs, out_specs=out_specs,
        scratch_shapes=side.sems,
        input_output_aliases={a: a for a in range(n_in)} if in_place else {},
    )(*side.inputs)


def _pick_rows(rows, cap=512):
    t = cap
    while t >= 8:
        if rows % t == 0:
            return t
        t //= 2
    raise ValueError(rows)


def _pair_sum(full, offset, got, name):
    _, rows, cols = full.shape
    tile = _pick_rows(rows, 256)
    c = lax.axis_index("c").astype(jnp.int32).reshape(1)

    def body(c_ref, a_ref, b_ref, o_ref):
        o_ref[...] = a_ref[...] + b_ref[...]

    blk = (1, tile, cols)
    return pl.pallas_call(
        body, name=name,
        grid_spec=pltpu.PrefetchScalarGridSpec(
            num_scalar_prefetch=1, grid=(4, rows // tile),
            in_specs=[pl.BlockSpec(blk, lambda k, i, c_ref: (offset + 4 * (k // 2) + 2 * (k % 2) + c_ref[0], i, 0)),
                      pl.BlockSpec(blk, lambda k, i, c_ref: (k, i, 0))],
            out_specs=pl.BlockSpec(blk, lambda k, i, c_ref: (k, i, 0))),
        out_shape=jax.ShapeDtypeStruct((4, rows, cols), full.dtype),
        compiler_params=_params(("arbitrary", "arbitrary")),
    )(c, full, got)


def _adam_math(w, g, m, v):
    c1 = 1.0 / (1.0 - ADAM_B1 ** ADAM_STEP)
    c2 = 1.0 / (1.0 - ADAM_B2 ** ADAM_STEP)
    m_new = ADAM_B1 * m + (1.0 - ADAM_B1) * g
    v_new = ADAM_B2 * v + (1.0 - ADAM_B2) * (g * g)
    delta = -ADAM_LR * ((m_new * c1) / (jnp.sqrt(v_new * c2) + ADAM_EPS) + ADAM_WD * w)
    return delta, m_new, v_new


def _chip_sum(part, others, name):
    _, rows, cols = part.shape
    tile = _pick_rows(rows, 256)
    own = (2 * lax.axis_index("x") + lax.axis_index("y")).astype(jnp.int32).reshape(1)

    def body(own_ref, p_ref, o_ref, g_out):
        g_out[...] = ((p_ref[0] + o_ref[0]) + o_ref[1]) + o_ref[2]

    return pl.pallas_call(
        body, name=name,
        grid_spec=pltpu.PrefetchScalarGridSpec(
            num_scalar_prefetch=1, grid=(rows // tile,),
            in_specs=[pl.BlockSpec((1, tile, cols), lambda i, own_ref: (own_ref[0], i, 0)),
                      pl.BlockSpec((3, tile, cols), lambda i, own_ref: (0, i, 0))],
            out_specs=pl.BlockSpec((tile, cols), lambda i, own_ref: (i, 0))),
        out_shape=jax.ShapeDtypeStruct((rows, cols), F32),
        compiler_params=_params(("arbitrary",)),
    )(own, part, others)


def _all_reduce_small(vec):
    gathered = _all_gather([vec], "ar_gather")[0]
    rows = vec.shape[0]

    def fn(*blocks):
        s = blocks[0]
        for b in blocks[1:]:
            s = s + b
        return s

    return _rowcall("ar_sum", fn, [gathered[j] for j in range(N_DEV)], [], [(1024, F32)],
                    tile=_pick_rows(rows))[0]


def _pad_rows(flat, mult):
    n = flat.shape[-1]
    per = mult * 1024
    padded = -(-n // per) * per
    pad = [(0, 0)] * (flat.ndim - 1) + [(0, padded - n)]
    return jnp.pad(flat, pad).reshape(flat.shape[:-1] + (padded // 1024, 1024))


def _regroup_w_in(w):
    pad = jnp.zeros((w.shape[0], 116), w.dtype)
    return jnp.concatenate([w[:, 768:1280], w[:, 1286:1798], w[:, 1280:1286], w[:, 2950:2956], pad,
                            w[:, 0:768], w[:, 1798:2950]], axis=1)


def _ungroup_w_in(wp):
    return jnp.concatenate([wp[:, 1152:1920], wp[:, 0:512], wp[:, 1024:1030], wp[:, 512:1024],
                            wp[:, 1920:3072], wp[:, 1030:1036]], axis=1)


def _to_shard(name, a):
    if name == 'w_in':
        return _regroup_w_in(a)
    if name in ('w_gate', 'w_up'):
        return jnp.pad(a, ((0, 0), (0, FF_SHARD_P - FF_SHARD)))
    if name == 'w_down':
        return jnp.pad(a, ((0, FF_SHARD_P - FF_SHARD), (0, 0)))
    return a


def _from_shard(name, a):
    if name == 'w_in':
        return _ungroup_w_in(a)
    if name in ('w_gate', 'w_up'):
        return a[:, 0:FF_SHARD]
    if name == 'w_down':
        return a[0:FF_SHARD, :]
    return a


def _whole(name, gathered):
    if BIG[name][0] == 1:
        return gathered.reshape(-1, gathered.shape[-1])
    return gathered.transpose(1, 0, 2).reshape(gathered.shape[1], -1)


def _split(name, whole):
    if BIG[name][0] == 1:
        return whole.reshape(N_DEV, whole.shape[0] // N_DEV, whole.shape[1])
    return whole.reshape(whole.shape[0], N_DEV, whole.shape[1] // N_DEV).transpose(1, 0, 2)


def _pack_list(arrays, mult):
    return _pad_rows(jnp.concatenate([a.reshape(-1) for a in arrays]), mult)


def _unpack_list(buf, shapes):
    flat = buf.reshape(-1)
    out, off = [], 0
    for s in shapes:
        n = math.prod(s)
        out.append(flat[off:off + n].reshape(s))
        off += n
    return out


def _adamw(w, g, m, v, name="adamw"):
    rows, cols = w.shape
    return _rowcall(name, _adam_math, [w, g, m, v], [], [(cols, F32)] * 3, tile=_pick_rows(rows, 256))


def _block_diag(w):
    out = jnp.zeros((LRU_WIDTH, LRU_WIDTH), w.dtype)
    for g in range(4):
        out = lax.dynamic_update_slice(out, w[g], (64 * g, 64 * g))
    return out


def _block_diag_grad(full):
    return jnp.stack([full[64 * g:64 * (g + 1), 64 * g:64 * (g + 1)] for g in range(4)])


def _row(v):
    return v.reshape(1, -1).astype(F32)


def _lane128(*pieces):
    flat = jnp.concatenate([p.reshape(-1).astype(F32) for p in pieces])
    return jnp.pad(flat, (0, 128 - flat.shape[0])).reshape(1, 128)


def _layer_consts(w):
    c = {}
    cw, cb = w['ssd_conv_w'], w['ssd_conv_b']
    c['cw_x'], c['cw_b'], c['cw_c'] = cw[:, 0:384], cw[:, 384:640], cw[:, 640:896]
    c['cb_x'], c['cb_b'], c['cb_c'] = _row(cb[0:384]), _row(cb[384:640]), _row(cb[640:896])
    c['bias128'] = _lane128(w['ssd_dt_bias'], w['fox_b_f'])
    c['alog128'] = _lane128(w['ssd_a_log'])
    c['d384'] = _row(jnp.repeat(w['ssd_d'], HEAD_DIM))
    c['lcw'], c['lcb'] = w['lru_conv_w'], _row(w['lru_conv_b'])
    c['wa'], c['wx'] = _block_diag(w['lru_w_a']).astype(BF16), _block_diag(w['lru_w_x']).astype(BF16)
    c['ba'], c['bx'], c['lam'] = _row(w['lru_b_a']), _row(w['lru_b_x']), _row(w['lru_lambda'])
    return c


def _layer_fwd(h0, p_i, w, c, hooks=None, layer=0):
    nb, seq = c['nb'], c['seq']

    def carried(stage):
        return hooks.fwd_side(layer, stage) if hooks is not None else None

    def arrived(outs):
        if hooks is not None:
            hooks.fwd_done(outs)
            w.update(hooks.weights(layer))

    u1 = _rowcall("norm1", lambda h, g: _rms(h, g), [h0], [_row(w['norm1_g'])], [(D_MODEL, BF16)])[0]
    proj = _matmul(u1, w['w_in'], 'nn', "proj")

    xs_c = _convsilu_fwd("conv_x", (proj, 384, 4), seq, c['cw_x'], c['cb_x'], F32)
    b_c = _convsilu_fwd("conv_b", (proj, 256, 0), seq, c['cw_b'], c['cb_b'], BF16)
    c_c = _convsilu_fwd("conv_c", (proj, 256, 1), seq, c['cw_c'], c['cb_c'], BF16)
    dt_arr, cum, prev, end = _small_fwd((proj, 128, 8), seq, c['bias128'], c['alog128'])
    x_h = xs_c.reshape(nb, seq, SSD_WIDTH)
    cum3 = cum.reshape(nb, seq, 128)
    a_row = cum3[:, :, 0:8].transpose(0, 2, 1)
    ssd_in = (x_h, b_c.reshape(nb, seq, 256), c_c.reshape(nb, seq, 256), dt_arr.reshape(nb, seq, 128), cum3,
              prev.reshape(nb, seq, 128), end.reshape(nb, seq, 128), a_row)
    y_h, states = _ssd_fwd(*ssd_in)
    y_core = y_h.reshape(nb * seq, SSD_WIDTH)

    hseq, xl = _lru_fwd(proj, seq, c['lcw'], c['lcb'], c['wa'], c['ba'], c['wx'], c['bx'], c['lam'])

    bias_cols, bias_rows = _fox_bias(cum, nb, seq)
    y_fox, lse, *side_out = _fox_fwd(proj, bias_cols, bias_rows, nb, seq, carried('attention'))
    arrived(side_out)

    def post(yc, xs, z, hs, lg, yf, d, g1, g2, g3):
        y1 = _rms((yc + xs * d) * _silu(z), g1)
        y2 = _rms(hs * _gelu(lg), g2)
        y3 = _rms(yf, g3)
        return jnp.concatenate([y1, y2, y3], axis=-1)

    post_consts = [c['d384'], _row(w['ssd_norm_g']), _row(w['lru_norm_g']), _row(w['fox_norm_g'])]
    ycat = _rowcall("mix_post", post, [y_core, xs_c, (proj, 384, 3), hseq, (proj, 256, 3), y_fox], post_consts,
                    [(D_MODEL, BF16)])[0]
    mix = _matmul(ycat, w['w_out'], 'nn', "mix_out", BF16)

    def res_norm(h, d, g):
        hn = h + d.astype(F32)
        return hn, _rms(hn, g)

    h1, u2 = _rowcall("res_norm2", res_norm, [h0, mix], [_row(w['norm2_g'])], [(D_MODEL, F32), (D_MODEL, BF16)])
    gate_pre, up_pre, act = _ffn_in(u2, w['w_gu'])
    side = carried('ffn_out')
    if side is None:
        ff = _matmul(act, w['w_down'], 'nn', "ffn_out", BF16)
    else:
        ff, *side_out = _matmul(act, w['w_down'], 'nn', "ffn_out", BF16, side=side)
        arrived(side_out)
    h2, u3 = _rowcall("res_norm3", res_norm, [h1, ff], [_row(w['norm3_g'])], [(D_MODEL, F32), (D_MODEL, BF16)])
    pg = _matmul(u3, w['w_ple_gate'], 'nn', "ple_gate", BF16)
    pp = _matmul(p_i, w['w_ple_proj'], 'nn', "ple_proj", BF16)
    h3 = _rowcall("ple", lambda h, a, b, bias: h + _sigmoid(a.astype(F32) + bias) * b.astype(F32), [h2, pg, pp],
                  [_row(w['b_ple_gate'])],
                  [(D_MODEL, F32)])[0]
    saved = dict(h0=h0, u1=u1, proj=proj, xs_c=xs_c, dt_arr=dt_arr, ssd_in=ssd_in, states=states,
                 y_core=y_core, hseq=hseq, xl=xl, bias_cols=bias_cols, bias_rows=bias_rows, lse=lse,
                 y_fox=y_fox, post_consts=post_consts, ycat=ycat, h1=h1, u2=u2, gate_pre=gate_pre, up_pre=up_pre, act=act, h2=h2, u3=u3,
                 pg=pg, pp=pp, p_i=p_i)
    return h3, saved


def _layer_bwd(dh3, s, w, c, hooks=None, layer=0):
    nb, seq = c['nb'], c['seq']
    g = {}

    def ple_bwd(dh, a, b, bias):
        a, b = a.astype(F32), b.astype(F32)
        gate = _sigmoid(a + bias)
        dpg = dh * b * gate * (1.0 - gate)
        return dh * gate, dpg, jnp.sum(dpg, axis=0, keepdims=True)

    dpp, dpg, g['b_ple_gate'] = _rowcall("ple_bwd", ple_bwd, [dh3, s['pg'], s['pp']], [_row(w['b_ple_gate'])],
                                         [(D_MODEL, BF16), (D_MODEL, BF16)], [((1, D_MODEL), F32)])
    g['w_ple_proj'] = _matmul(s['p_i'], dpp, 'tn', "d_w_ple_proj")
    g['w_ple_gate'] = _matmul(s['u3'], dpg, 'tn', "d_w_ple_gate")
    du3 = _matmul(dpg, w['w_ple_gate'], 'nt', "d_u3", BF16)

    def norm_bwd(h, du, dh, gain):
        dx, dg = _rms_bwd(h, gain, du.astype(F32))
        dhn = dh + dx
        return dhn, dhn, dg

    dh2, dh2_b, g['norm3_g'] = _rowcall("norm3_bwd", norm_bwd, [s['h2'], du3, dh3], [_row(w['norm3_g'])],
                                        [(D_MODEL, F32), (D_MODEL, BF16)], [((1, D_MODEL), F32)])
    g['w_down'] = _matmul(s['act'], dh2_b, 'tn', "d_w_down")
    dact = _matmul(dh2_b, w['w_down'], 'nt', "d_act", BF16)

    def swiglu_bwd(gt, up, da):
        gt, up, da = gt.astype(F32), up.astype(F32), da.astype(F32)
        return jnp.concatenate([da * up * _dsilu(gt), da * _silu(gt)], axis=-1)

    dgu = _rowcall("swiglu_bwd", swiglu_bwd, [s['gate_pre'], s['up_pre'], dact], [],
                   [(2 * D_FF_P, BF16)])[0]
    gu16 = _matmul(s['u2'], dgu, 'tn', "d_w_gu", shard_n=FF_SHARD_P)
    du2 = _matmul(dgu, w['w_gu'], 'nt', "d_u2", BF16)
    dh1, dh1_b, g['norm2_g'] = _rowcall("norm2_bwd", norm_bwd, [s['h1'], du2, dh2], [_row(w['norm2_g'])],
                                        [(D_MODEL, F32), (D_MODEL, BF16)], [((1, D_MODEL), F32)])
    g['w_out'] = _matmul(s['ycat'], dh1_b, 'tn', "d_w_out")
    if hooks is None:
        dycat = _matmul(dh1_b, w['w_out'], 'nt', "d_ycat", BF16)
    else:
        ready = {n: g[n] for n in ('w_out', 'w_down', 'w_ple_gate', 'w_ple_proj')}
        ready['w_gate'], ready['w_up'] = (gu16, 0), (gu16, N_DEV)
        dycat, *side_out = _matmul(dh1_b, w['w_out'], 'nt', "d_ycat", BF16, side=hooks.sibling_side(layer, ready))
        hooks.sibling_done(side_out)

    def post_bwd(dy, yc, xs, z, hs, lg, yf, d, g1, g2, g3):
        dy = dy.astype(F32)
        sz = _silu(z)
        ytot = yc + xs * d
        dpre1, dg1 = _rms_bwd(ytot * sz, g1, dy[:, 0:384])
        dytot = dpre1 * sz
        dz = dpre1 * ytot * _dsilu(z)
        dd = jnp.sum(dytot * xs, axis=0, keepdims=True)
        gl = _gelu(lg)
        dpre2, dg2 = _rms_bwd(hs * gl, g2, dy[:, 384:640])
        dyf, dg3 = _rms_bwd(yf, g3, dy[:, 640:1024])
        return dytot, dytot * d, dz, dpre2 * gl, dpre2 * hs * _dgelu(lg), dyf, dd, dg1, dg2, dg3

    (dy_core, dxs_skip, dz, dhseq, dlg, dy_fox, dd384, g['ssd_norm_g'], g['lru_norm_g'], g['fox_norm_g']) = _rowcall(
        "mix_post_bwd", post_bwd,
        [dycat, s['y_core'], s['xs_c'], (s['proj'], 384, 3), s['hseq'], (s['proj'], 256, 3), s['y_fox']],
        s['post_consts'],
        [(384, F32), (384, F32), (384, BF16), (256, F32), (256, BF16), (384, F32)],
        [((1, 384), F32), ((1, 384), F32), ((1, 256), F32), ((1, 384), F32)])
    g['ssd_d'] = dd384.reshape(SSD_HEADS, HEAD_DIM).sum(axis=1)

    side = hooks.bwd_side() if hooks is not None else None
    dq, dk, dv, dcf_rows, dcf_cols, *side_out = _fox_bwd(s['proj'], s['y_fox'], s['lse'], dy_fox, s['bias_cols'],
                                                         s['bias_rows'], nb, seq, side)
    if hooks is not None:
        hooks.bwd_done(side_out)
    dq = dq.astype(BF16)

    dx_h, db_c, dc_c, da_arr, dend_arr, ddt_arr = _ssd_bwd(*s['ssd_in'], s['states'],
                                                           dy_core.reshape(nb, seq, SSD_WIDTH))
    dxs_c = dx_h.reshape(nb * seq, SSD_WIDTH) + dxs_skip
    dcf = dcf_rows[:, :, 0:2, :].reshape(nb, FOX_HEADS, seq).transpose(0, 2, 1)
    dcum = jnp.concatenate([da_arr[:, :, 0:SSD_HEADS], dcf,
                            jnp.zeros((nb, seq, 128 - 2 * SSD_HEADS), F32)], axis=-1).reshape(nb * seq, 128)
    proj = s['proj']
    dxs_raw, dcw_x, dcb_x = _convsilu_bwd("conv_x_bwd", (proj, 384, 4), seq, c['cw_x'], c['cb_x'], dxs_c)
    db_raw, dcw_b, dcb_b = _convsilu_bwd("conv_b_bwd", (proj, 256, 0), seq, c['cw_b'], c['cb_b'],
                                         db_c.reshape(nb * seq, 256))
    dc_raw, dcw_c, dcb_c = _convsilu_bwd("conv_c_bwd", (proj, 256, 1), seq, c['cw_c'], c['cb_c'],
                                         dc_c.reshape(nb * seq, 256))
    dsmall, dbias128, dalog128 = _small_bwd((proj, 128, 8), seq, dcum, dcf_cols, dend_arr.reshape(nb * seq, 128),
                                            ddt_arr.reshape(nb * seq, 128), s['dt_arr'], c['bias128'], c['alog128'])
    g['ssd_conv_w'] = jnp.concatenate([dcw_x, dcw_b, dcw_c], axis=1)
    g['ssd_conv_b'] = jnp.concatenate([dcb_x, dcb_b, dcb_c], axis=1).reshape(-1)
    g['ssd_dt_bias'] = dbias128[0, 0:SSD_HEADS]
    g['fox_b_f'] = dbias128[0, SSD_HEADS:2 * SSD_HEADS]
    g['ssd_a_log'] = dalog128[0, 0:SSD_HEADS]

    (dlru_raw, g['lru_conv_w'], dlcb, dwa, dba, dwx, dbx, dlam) = _lru_bwd(
        s['proj'], seq, s['xl'], s['hseq'], dhseq, c['lcw'], c['lcb'], c['wa'], c['ba'], c['wx'], c['bx'], c['lam'])
    g['lru_conv_b'], g['lru_b_a'], g['lru_b_x'], g['lru_lambda'] = (t.reshape(-1) for t in (dlcb, dba, dbx, dlam))
    g['lru_w_a'], g['lru_w_x'] = _block_diag_grad(dwa), _block_diag_grad(dwx)

    dproj = jnp.concatenate([db_raw, dc_raw, dlru_raw, dlg, dsmall, dz, dxs_raw, dq, dk, dv], axis=1)
    g['w_in'] = _matmul(s['u1'], dproj, 'tn', "d_w_in")
    if hooks is None:
        du1 = _matmul(dproj, w['w_in'], 'nt', "d_u1", BF16)
    else:
        hooks.bwd_ready(layer, {'w_in': g['w_in']})
        du1, *side_out = _matmul(dproj, w['w_in'], 'nt', "d_u1", BF16, side=hooks.bwd_side())
        hooks.bwd_done(side_out)

    def norm1_bwd(h, du, dh, gain):
        dx, dg = _rms_bwd(h, gain, du.astype(F32))
        return dh + dx, dg

    dh0, g['norm1_g'] = _rowcall("norm1_bwd", norm1_bwd, [s['h0'], du1, dh1], [_row(w['norm1_g'])],
                                 [(D_MODEL, F32)], [((1, D_MODEL), F32)])
    for name in ('b_ple_gate', 'norm3_g', 'norm2_g', 'norm1_g', 'ssd_norm_g', 'lru_norm_g', 'fox_norm_g'):
        g[name] = g[name].reshape(-1)
    g['w_gate'], g['w_up'] = None, None
    if hooks is None:
        g['w_gate'] = gu16[0:N_DEV].transpose(1, 0, 2).reshape(D_MODEL, D_FF_P)
        g['w_up'] = gu16[N_DEV:2 * N_DEV].transpose(1, 0, 2).reshape(D_MODEL, D_FF_P)
    return dh0, g


class _Hooks:
    def __init__(self, shard):
        self.shard = shard
        self.whole = {}
        self.part, self.others = {}, {}
        self.pending, self.flying = [], []

    def first(self, extra):
        got = _all_gather([self.shard['w_in', 0]] + extra, "gather_first")
        self.whole['w_in', 0] = _whole('w_in', got[0])
        return got[1:]

    def fwd_side(self, layer, stage):
        if stage == 'attention':
            self.flying = [(n, layer) for n in BIG if n != 'w_in']
        elif layer + 1 < DEPTH:
            self.flying = [('w_in', layer + 1)]
        else:
            return None
        return _spread_side([self.shard[k] for k in self.flying])

    def fwd_done(self, outs):
        if self.flying:
            passed = _run_side(_pass_side(outs), "gather_pass_%s%d" % self.flying[0], in_place=True)
            for k, arr in zip(self.flying, passed):
                self.whole[k] = _whole(k[0], arr)
            self.flying = []

    def weights(self, layer):
        w = {n: self.whole[n, layer] for n in BIG if (n, layer) in self.whole}
        if 'w_gate' in w:
            w['w_gu'] = jnp.concatenate([w['w_gate'], w['w_up']], axis=1)
        return w

    def sibling_side(self, layer, grads):
        self.sib_keys = [(n, layer) for n in grads]
        self.sib_full = [g if isinstance(g, tuple) else (_split(n, g), 0) for n, g in grads.items()]
        return _sibling_side([f for f, _ in self.sib_full], [off for _, off in self.sib_full])

    def sibling_done(self, got):
        for k, (f, off), r in zip(self.sib_keys, self.sib_full, got):
            self.part[k] = _pair_sum(f, off, r, "rs_pair_sum_%s%d" % k)
        self.pending += self.sib_keys

    def bwd_ready(self, layer, grads):
        side = self.sibling_side(layer, grads)
        self.sibling_done(_run_side(side, "rs_sibling_%s%d" % self.sib_keys[0]))

    def bwd_side(self):
        self.flying, self.pending = self.pending, []
        return _chip_side([self.part[k] for k in self.flying]) if self.flying else None

    def bwd_done(self, outs):
        for k, o in zip(self.flying, outs):
            self.others[k] = o
        self.flying = []

    def flush(self):
        side = self.bwd_side()
        if side is not None:
            self.bwd_done(_run_side(side, "rs_chips_last"))


def _local_step(x, p, target, big, small, hooks=None):
    nb, seq, _ = x.shape
    tokens = nb * seq
    h = x.reshape(tokens, D_MODEL)
    layers, saves = [], []
    for i in range(DEPTH):
        w = {name: small[name][i] for name in small if name != 'final_norm_g'}
        if hooks is not None:
            w.update(hooks.weights(i))
        else:
            for name in ('w_in', 'w_out', 'w_down', 'w_ple_gate', 'w_ple_proj'):
                w[name] = big[name][i]
            w['w_gu'] = jnp.concatenate([big['w_gate'][i], big['w_up'][i]], axis=1)
        c = _layer_consts(w)
        c['nb'], c['seq'] = nb, seq
        h, s = _layer_fwd(h, p[i].reshape(tokens, PLE_DIM).astype(BF16), w, c, hooks, i)
        layers.append((w, c))
        saves.append(s)

    def head(hf, tgt, gain):
        r = lax.rsqrt(jnp.mean(hf * hf, axis=-1, keepdims=True) + EPS)
        xhat = hf * r
        err = xhat * gain - tgt
        loss = 0.5 * jnp.sum(jnp.mean(err * err, axis=-1, keepdims=True), axis=0, keepdims=True)
        dy = err * (1.0 / D_MODEL)
        dg = jnp.sum(dy * xhat, axis=0, keepdims=True)
        dxhat = dy * gain
        dh = r * (dxhat - xhat * jnp.mean(dxhat * xhat, axis=-1, keepdims=True))
        return dh, jnp.broadcast_to(loss, (1, 128)), dg

    dh, loss128, dgf = _rowcall("loss_head", head, [h, target.reshape(tokens, D_MODEL)],
                                [_row(small['final_norm_g'])], [(D_MODEL, F32)],
                                [((1, 128), F32), ((1, D_MODEL), F32)])
    grads = {'final_norm_g': dgf.reshape(-1)}
    per_layer = [None] * DEPTH
    for i in range(DEPTH - 1, -1, -1):
        w, c = layers[i]
        dh, per_layer[i] = _layer_bwd(dh, saves[i], w, c, hooks, i)
    for name in per_layer[0]:
        if name in BIG:
            grads[name] = [per_layer[i][name] for i in range(DEPTH)]
        else:
            grads[name] = jnp.stack([per_layer[i][name] for i in range(DEPTH)])
    return loss128[0, 0], dh.reshape(nb, seq, D_MODEL), grads


def kernel(x, p, norm1_g, w_in, ssd_conv_w, ssd_conv_b, ssd_dt_bias, ssd_a_log, ssd_d, ssd_norm_g, lru_conv_w, lru_conv_b, lru_w_a, lru_b_a, lru_w_x, lru_b_x, lru_lambda, lru_norm_g, fox_b_f, fox_norm_g, w_out, norm2_g, w_gate, w_up, w_down, norm3_g, w_ple_gate, b_ple_gate, w_ple_proj, final_norm_g, loss_target, m_norm1_g, m_w_in, m_ssd_conv_w, m_ssd_conv_b, m_ssd_dt_bias, m_ssd_a_log, m_ssd_d, m_ssd_norm_g, m_lru_conv_w, m_lru_conv_b, m_lru_w_a, m_lru_b_a, m_lru_w_x, m_lru_b_x, m_lru_lambda, m_lru_norm_g, m_fox_b_f, m_fox_norm_g, m_w_out, m_norm2_g, m_w_gate, m_w_up, m_w_down, m_norm3_g, m_w_ple_gate, m_b_ple_gate, m_w_ple_proj, m_final_norm_g, v_norm1_g, v_w_in, v_ssd_conv_w, v_ssd_conv_b, v_ssd_dt_bias, v_ssd_a_log, v_ssd_d, v_ssd_norm_g, v_lru_conv_w, v_lru_conv_b, v_lru_w_a, v_lru_b_a, v_lru_w_x, v_lru_b_x, v_lru_lambda, v_lru_norm_g, v_fox_b_f, v_fox_norm_g, v_w_out, v_norm2_g, v_w_gate, v_w_up, v_w_down, v_norm3_g, v_w_ple_gate, v_b_ple_gate, v_w_ple_proj, v_final_norm_g):
    args = dict(locals())
    w_loc = {n: args[n] for n in WEIGHTS}
    m_loc = {n: args['m_' + n] for n in WEIGHTS}
    v_loc = {n: args['v_' + n] for n in WEIGHTS}
    dev = 4 * lax.axis_index("x") + 2 * lax.axis_index("y") + lax.axis_index("c")

    keys = [(n, i) for n in BIG for i in range(DEPTH)]
    conv_names = list(CONV_SHARDED)
    conv_loc_shapes = [w_loc[n].shape for n in conv_names]
    hooks = _Hooks({(n, i): _to_shard(n, w_loc[n][i]).astype(BF16) for n, i in keys})
    conv_all, = hooks.first([_pack_list([w_loc[n] for n in conv_names], 8)])
    small = {n: w_loc[n] for n in WEIGHTS if n not in BIG and n not in CONV_SHARDED}
    per_dev = [_unpack_list(conv_all[j], conv_loc_shapes) for j in range(N_DEV)]
    for idx, n in enumerate(conv_names):
        small[n] = jnp.concatenate([per_dev[j][idx] for j in range(N_DEV)], axis=2)

    loss_part, dx, grads = _local_step(x, p, loss_target, None, small, hooks)
    loss = lax.psum(loss_part, ("x", "y", "c"))
    hooks.flush()
    out = {kind: {} for kind in ('g', 'delta', 'm', 'v')}
    for n in BIG:
        g_nat = jnp.stack([_from_shard(n, _chip_sum(hooks.part[n, i], hooks.others[n, i], "rs_chip_sum_%s%d" % (n, i)))
                           for i in range(DEPTH)])
        shape = w_loc[n].shape
        flat = [d.reshape(-1, shape[-1]) for d in (w_loc[n], g_nat, m_loc[n], v_loc[n])]
        out['g'][n] = g_nat
        for kind, r in zip(('delta', 'm', 'v'), _adamw(*flat, name="adamw_" + n)):
            out[kind][n] = r.reshape(shape)

    small_names = [n for n in WEIGHTS if n not in BIG]
    small_shapes = [grads[n].shape for n in small_names]
    g_small = dict(zip(small_names, _unpack_list(
        _all_reduce_small(_pack_list([grads[n] for n in small_names], 8)), small_shapes)))
    for n in CONV_SHARDED:
        width = CONV_SHARDED[n][2] // N_DEV
        g_small[n] = lax.dynamic_slice_in_dim(g_small[n], dev * width, width, axis=2)
    shapes = [w_loc[n].shape for n in small_names]
    packed = [_pack_list([d[n] for n in small_names], 8) for d in (w_loc, g_small, m_loc, v_loc)]
    upd = [dict(zip(small_names, _unpack_list(t, shapes))) for t in _adamw(*packed)]
    for kind, d in zip(('g', 'delta', 'm', 'v'), [g_small] + upd):
        for n in small_names:
            out[kind][n] = d[n]
    return (loss, dx, *[out['g'][n] for n in WEIGHTS], *[out['delta'][n] for n in WEIGHTS],
            *[out['m'][n] for n in WEIGHTS], *[out['v'][n] for n in WEIGHTS])
```

```python
import functools
import math

import jax
import jax.numpy as jnp
from jax import lax
from jax.experimental import pallas as pl
from jax.experimental.pallas import tpu as pltpu

F32 = jnp.float32
BF16 = jnp.bfloat16

N_DEV = 8
D_MODEL = 1024
DEPTH = 2
HEAD_DIM = 64
SSD_WIDTH = 384
LRU_WIDTH = 256
FOX_WIDTH = 384
SSD_HEADS = 6
SSD_STATE = 128
CHUNK = 256
FOX_HEADS = 6
D_FF = 2816
FF_SHARD = D_FF // N_DEV
FF_SHARD_P = 384
D_FF_P = N_DEV * FF_SHARD_P
PLE_DIM = 256
IN_COLS = 2956
PROJ_COLS = 3072
LRU_C = 8.0
EPS = 1e-6
NEG = -1e30

ADAM_LR = 0.001
ADAM_B1 = 0.9
ADAM_B2 = 0.999
ADAM_EPS = 1e-08
ADAM_WD = 0.01
ADAM_STEP = 10

VMEM_LIMIT = 56 * 1024 * 1024

WEIGHTS = ['norm1_g', 'w_in', 'ssd_conv_w', 'ssd_conv_b', 'ssd_dt_bias', 'ssd_a_log', 'ssd_d', 'ssd_norm_g',
           'lru_conv_w', 'lru_conv_b', 'lru_w_a', 'lru_b_a', 'lru_w_x', 'lru_b_x', 'lru_lambda', 'lru_norm_g',
           'fox_b_f', 'fox_norm_g', 'w_out', 'norm2_g', 'w_gate', 'w_up', 'w_down', 'norm3_g', 'w_ple_gate',
           'b_ple_gate', 'w_ple_proj', 'final_norm_g']
BIG = {'w_in': (1, (DEPTH, D_MODEL, IN_COLS)), 'w_out': (1, (DEPTH, D_MODEL, D_MODEL)),
       'w_gate': (2, (DEPTH, D_MODEL, D_FF)), 'w_up': (2, (DEPTH, D_MODEL, D_FF)),
       'w_down': (1, (DEPTH, D_FF, D_MODEL)), 'w_ple_gate': (1, (DEPTH, D_MODEL, D_MODEL)),
       'w_ple_proj': (2, (DEPTH, PLE_DIM, D_MODEL))}
CONV_SHARDED = {'ssd_conv_w': (DEPTH, 4, 896), 'lru_conv_w': (DEPTH, 4, 256)}


def _dot(a, b):
    return jnp.dot(a, b, preferred_element_type=F32)


def _dot_nt(a, b):
    return lax.dot_general(a, b, (((1,), (1,)), ((), ())), preferred_element_type=F32)


def _dot_tn(a, b):
    return lax.dot_general(a, b, (((0,), (0,)), ((), ())), preferred_element_type=F32)


def _params(sem):
    return pltpu.CompilerParams(dimension_semantics=sem, vmem_limit_bytes=VMEM_LIMIT)


def _pick_tile(n, cap):
    if n <= cap:
        return n
    best = 128
    for t in range(128, cap + 1, 128):
        if n % t == 0:
            best = t
    assert n % best == 0, (n, cap)
    return best


def _matmul(a, b, mode, name, out_dtype=F32, side=None, shard_n=None):
    if mode == 'tn':
        k_dim, m_dim = a.shape
        n_dim = b.shape[1]
    else:
        m_dim, k_dim = a.shape
        n_dim = b.shape[1] if mode == 'nn' else b.shape[0]
    tm = _pick_tile(m_dim, 1024)
    tn = 2 * shard_n if shard_n else _pick_tile(n_dim, 1536 if mode != 'tn' else 1024)
    tk = _pick_tile(k_dim, 3072 if mode != 'tn' else 2048)
    nk = k_dim // tk
    grid = (n_dim // tn, m_dim // tm, nk)

    n_in = len(side.inputs) if side else 0
    n_out = len(side.out_shape) if side else 0

    n_acc = 1 if nk > 1 else 0

    def body(*refs):
        a_ref, b_ref, o_ref = refs[0], refs[1], refs[2 + n_in]
        acc_ref = refs[3 + n_in + n_out] if n_acc else None
        if side is not None:
            side.run(refs[2:2 + n_in] + refs[3 + n_in:3 + n_in + n_out] + refs[3 + n_acc + n_in + n_out:],
                     *_grid_ends(grid))
        kk = pl.program_id(2)
        prod = {'nn': _dot, 'nt': _dot_nt, 'tn': _dot_tn}[mode](a_ref[...], b_ref[...])

        def write(res):
            if shard_n:
                for q in range(tn // shard_n):
                    o_ref[q] = res[:, q * shard_n:(q + 1) * shard_n].astype(o_ref.dtype)
            else:
                o_ref[...] = res.astype(o_ref.dtype)

        if nk == 1:
            write(prod)
            return

        @pl.when(kk == 0)
        def _():
            acc_ref[...] = prod

        @pl.when(jnp.logical_and(kk > 0, kk < nk - 1))
        def _():
            acc_ref[...] += prod

        @pl.when(kk == nk - 1)
        def _():
            write(acc_ref[...] + prod)

    if mode == 'nn':
        a_spec = pl.BlockSpec((tm, tk), lambda j, i, k: (i, k))
        b_spec = pl.BlockSpec((tk, tn), lambda j, i, k: (k, j))
    elif mode == 'nt':
        a_spec = pl.BlockSpec((tm, tk), lambda j, i, k: (i, k))
        b_spec = pl.BlockSpec((tn, tk), lambda j, i, k: (j, k))
    else:
        a_spec = pl.BlockSpec((tk, tm), lambda j, i, k: (k, i))
        b_spec = pl.BlockSpec((tk, tn), lambda j, i, k: (k, j))
    side_in, side_out = side.specs() if side else ([], [])
    if shard_n:
        out_shape = jax.ShapeDtypeStruct((n_dim // shard_n, m_dim, shard_n), out_dtype)
        out_spec = pl.BlockSpec((tn // shard_n, tm, shard_n), lambda j, i, k: (j, i, 0))
    else:
        out_shape = jax.ShapeDtypeStruct((m_dim, n_dim), out_dtype)
        out_spec = pl.BlockSpec((tm, tn), lambda j, i, k: (i, j))
    res = pl.pallas_call(
        body, name=name, grid=grid,
        out_shape=[out_shape] + (side.out_shape if side else []),
        in_specs=[a_spec, b_spec] + side_in,
        out_specs=[out_spec] + side_out,
        scratch_shapes=[pltpu.VMEM((tm, tn), F32)] * n_acc + (side.sems if side else []),
        compiler_params=_params(("arbitrary", "arbitrary", "arbitrary") if side
                                else ("parallel", "parallel", "arbitrary")),
    )(a, b, *(side.inputs if side else []))
    return res if side else res[0]


def _ffn_in(u, w_gu):
    tokens, k_dim = u.shape
    width = w_gu.shape[1] // 2
    tm, tn = _pick_tile(tokens, 1024), _pick_tile(width, 768)
    nj = width // tn

    def body(a_ref, bg_ref, bu_ref, g_ref, u_ref, act_ref):
        a = a_ref[...]
        g = _dot(a, bg_ref[...]).astype(BF16)
        up = _dot(a, bu_ref[...]).astype(BF16)
        g_ref[...] = g
        u_ref[...] = up
        act_ref[...] = (_silu(g.astype(F32)) * up.astype(F32)).astype(BF16)

    out = jax.ShapeDtypeStruct((tokens, width), BF16)
    tile = pl.BlockSpec((tm, tn), lambda j, i: (i, j))
    return pl.pallas_call(
        body, name="ffn_in", grid=(nj, tokens // tm),
        out_shape=[out, out, out],
        in_specs=[pl.BlockSpec((tm, k_dim), lambda j, i: (i, 0)),
                  pl.BlockSpec((k_dim, tn), lambda j, i: (0, j)),
                  pl.BlockSpec((k_dim, tn), lambda j, i: (0, j + nj))],
        out_specs=[tile, tile, tile],
        compiler_params=_params(("parallel", "parallel")),
    )(u, w_gu, w_gu)


def _rowcall(name, fn, tiled, consts, outs, accs=(), tile=512, scratch=()):
    specs, arrays = [], []
    for t in tiled:
        if isinstance(t, tuple):
            arr, width, blk = t
            specs.append(pl.BlockSpec((tile, width), functools.partial(lambda i, blk: (i, blk), blk=blk)))
        else:
            arr = t
            specs.append(pl.BlockSpec((tile, arr.shape[1]), lambda i: (i, 0)))
        arrays.append(arr)
    rows = arrays[0].shape[0]
    assert rows % tile == 0, (name, rows, tile)
    for c in consts:
        specs.append(pl.BlockSpec(c.shape, lambda i: (0, 0)))
        arrays.append(c)
    n_in, n_out, n_acc = len(arrays), len(outs), len(accs)
    out_shape = [jax.ShapeDtypeStruct((rows, c), dt) for c, dt in outs]
    out_specs = [pl.BlockSpec((tile, c), lambda i: (i, 0)) for c, _ in outs]
    out_shape += [jax.ShapeDtypeStruct(s, dt) for s, dt in accs]
    out_specs += [pl.BlockSpec(s, lambda i: (0, 0)) for s, _ in accs]

    def body(*refs):
        ins = [r[...] for r in refs[:n_in]]
        out_refs = refs[n_in:n_in + n_out]
        acc_refs = refs[n_in + n_out:n_in + n_out + n_acc]
        scr = refs[n_in + n_out + n_acc:]
        res = fn(*ins, *scr)
        if not isinstance(res, (tuple, list)):
            res = (res,)
        assert len(res) == n_out + n_acc, (name, len(res))
        for r, v in zip(out_refs, res[:n_out]):
            r[...] = v.astype(r.dtype)
        if n_acc:
            first = pl.program_id(0) == 0

            @pl.when(first)
            def _():
                for r, v in zip(acc_refs, res[n_out:]):
                    r[...] = v.astype(r.dtype)

            @pl.when(jnp.logical_not(first))
            def _():
                for r, v in zip(acc_refs, res[n_out:]):
                    r[...] += v.astype(r.dtype)

    res = pl.pallas_call(
        body, name=name, grid=(rows // tile,),
        out_shape=out_shape, in_specs=specs, out_specs=out_specs,
        scratch_shapes=list(scratch),
        compiler_params=_params(("arbitrary",)),
    )(*arrays)
    return res


def _sigmoid(x):
    return 1.0 / (1.0 + jnp.exp(-x))


def _softplus(x):
    return jnp.maximum(x, 0.0) + jnp.log(1.0 + jnp.exp(-jnp.abs(x)))


def _silu(x):
    return x * _sigmoid(x)


def _dsilu(x):
    s = _sigmoid(x)
    return s * (1.0 + x * (1.0 - s))


_GELU_C = math.sqrt(2.0 / math.pi)


def _gelu(x):
    return 0.5 * x * (1.0 + jnp.tanh(_GELU_C * (x + 0.044715 * x * x * x)))


def _dgelu(x):
    t = jnp.tanh(_GELU_C * (x + 0.044715 * x * x * x))
    return 0.5 * (1.0 + t) + 0.5 * x * (1.0 - t * t) * _GELU_C * (1.0 + 3.0 * 0.044715 * x * x)


def _neg_expm1(x):
    series = -x * (1.0 + x * (0.5 + x * (1.0 / 6.0 + x * (1.0 / 24.0 + x * (1.0 / 120.0)))))
    return jnp.where(x > -0.03, series, 1.0 - jnp.exp(x))


def _rms(x, g):
    r = lax.rsqrt(jnp.mean(x * x, axis=-1, keepdims=True) + EPS)
    return x * r * g


def _rms_bwd(x, g, dy):
    r = lax.rsqrt(jnp.mean(x * x, axis=-1, keepdims=True) + EPS)
    xhat = x * r
    dg = jnp.sum(dy * xhat, axis=0, keepdims=True)
    dxhat = dy * g
    dx = r * (dxhat - xhat * jnp.mean(dxhat * xhat, axis=-1, keepdims=True))
    return dx, dg


def _row_iota(shape):
    return lax.broadcasted_iota(jnp.int32, shape, 0)


def _shift_down(x, j):
    if j == 0:
        return x
    return jnp.where(_row_iota(x.shape) >= j, pltpu.roll(x, j, 0), 0.0)


def _shift_up(x, j):
    if j == 0:
        return x
    n = x.shape[0]
    return jnp.where(_row_iota(x.shape) < n - j, pltpu.roll(x, n - j, 0), 0.0)


def _conv(x, w, b):
    y = b + w[3:4, :] * x
    for k in range(3):
        y = y + w[k:k + 1, :] * _shift_down(x, 3 - k)
    return y


def _conv_bwd(x, w, dy):
    dx = w[3:4, :] * dy
    dws = []
    for k in range(3):
        dx = dx + w[k:k + 1, :] * _shift_up(dy, 3 - k)
        dws.append(jnp.sum(dy * _shift_down(x, 3 - k), axis=0, keepdims=True))
    dws.append(jnp.sum(dy * x, axis=0, keepdims=True))
    return dx, jnp.concatenate(dws, axis=0), jnp.sum(dy, axis=0, keepdims=True)


def _split3(x):
    hi = x.astype(BF16)
    r1 = x - hi.astype(F32)
    mid = r1.astype(BF16)
    lo = (r1 - mid.astype(F32)).astype(BF16)
    return hi, mid, lo


def _tri_dot(tri, x):
    hi, mid, lo = _split3(x)
    return _dot(tri, hi) + _dot(tri, mid) + _dot(tri, lo)


def _cumsum_rows(x):
    n = x.shape[0] // CHUNK
    r = lax.broadcasted_iota(jnp.int32, (CHUNK, CHUNK), 0)
    c = lax.broadcasted_iota(jnp.int32, (CHUNK, CHUNK), 1)
    tri = (r >= c).astype(BF16)
    carry = jnp.zeros((1, x.shape[1]), F32)
    cums, prevs, ends = [], [], []
    for i in range(n):
        blk = _tri_dot(tri, x[i * CHUNK:(i + 1) * CHUNK]) + carry
        prevs.append(jnp.broadcast_to(carry, blk.shape))
        carry = blk[CHUNK - 1:CHUNK, :]
        ends.append(jnp.broadcast_to(carry, blk.shape))
        cums.append(blk)
    return jnp.concatenate(cums, 0), jnp.concatenate(prevs, 0), jnp.concatenate(ends, 0)


def _rev_cumsum_rows(x):
    n = x.shape[0] // CHUNK
    r = lax.broadcasted_iota(jnp.int32, (CHUNK, CHUNK), 0)
    c = lax.broadcasted_iota(jnp.int32, (CHUNK, CHUNK), 1)
    tri = (r <= c).astype(BF16)
    carry = jnp.zeros((1, x.shape[1]), F32)
    local, whole = [None] * n, [None] * n
    for i in range(n - 1, -1, -1):
        local[i] = _tri_dot(tri, x[i * CHUNK:(i + 1) * CHUNK])
        whole[i] = local[i] + carry
        carry = whole[i][0:1, :]
    return jnp.concatenate(local, 0), jnp.concatenate(whole, 0)


def _convsilu_fwd(name, seg, seq, w, b, dtype):
    return _rowcall(name, lambda raw, w, b: _silu(_conv(raw, w, b)), [seg], [w, b], [(seg[1], dtype)], tile=seq)[0]


def _convsilu_bwd(name, seg, seq, w, b, dy):
    def fn(raw, dy, w, b):
        return _conv_bwd(raw, w, dy * _dsilu(_conv(raw, w, b)))

    width = seg[1]
    return _rowcall(name, fn, [seg, dy], [w, b], [(width, BF16)], [((4, width), F32), ((1, width), F32)], tile=seq)


def _small_fwd(seg, seq, bias128, alog128):
    def fn(small, bias, alog):
        lane = lax.broadcasted_iota(jnp.int32, small.shape, 1)
        a = -jnp.exp(alog)
        s = small + bias
        dt = _softplus(s)
        logf = -_softplus(-s)
        pre = jnp.where(lane < SSD_HEADS, a * dt, jnp.where(lane < 2 * SSD_HEADS, logf, 0.0))
        cum, prev, end = _cumsum_rows(pre)
        return dt, cum, prev, end

    return _rowcall("small_fwd", fn, [seg], [bias128, alog128], [(128, F32)] * 4, tile=seq)


def _small_bwd(seg, seq, dcum, dcq, dend, ddt, dt_arr, bias128, alog128):
    def fn(small, dcum, dcq, dend, ddt, dt_arr, bias, alog):
        lane = lax.broadcasted_iota(jnp.int32, small.shape, 1)
        for pair in range(FOX_PAIRS):
            first = SSD_HEADS + 2 * pair
            moved = pltpu.roll(dcq[:, pair * 128:(pair + 1) * 128], first, 1)
            dcum = dcum + jnp.where(jnp.logical_or(lane == first, lane == first + 1), moved, 0.0)
        a = -jnp.exp(alog)
        sig = _sigmoid(small + bias)
        local, whole = _rev_cumsum_rows(dcum)
        dadt = local + dend
        d_dt = ddt + a * dadt
        ds = jnp.where(lane < SSD_HEADS, d_dt * sig, jnp.where(lane < 2 * SSD_HEADS, whole * (1.0 - sig), 0.0))
        da = jnp.sum(jnp.where(lane < SSD_HEADS, dadt * dt_arr, 0.0), axis=0, keepdims=True)
        return ds, jnp.sum(ds, axis=0, keepdims=True), da * a

    return _rowcall("small_bwd", fn, [seg, dcum, dcq, dend, ddt, dt_arr], [bias128, alog128], [(128, BF16)],
                    [((1, 128), F32), ((1, 128), F32)], tile=seq)


HEAD_PAIRS = SSD_HEADS // 2


def _ssd_specs(nc, reverse):
    def at(c):
        return nc - 1 - c if reverse else c

    x_spec = pl.BlockSpec((1, CHUNK, SSD_WIDTH), lambda b, c: (b, at(c), 0))
    bc_spec = pl.BlockSpec((1, CHUNK, 256), lambda b, c: (b, at(c), 0))
    col_spec = pl.BlockSpec((1, CHUNK, 128), lambda b, c: (b, at(c), 0))
    row_spec = pl.BlockSpec((1, 8, CHUNK), lambda b, c: (b, 0, at(c)))
    st_spec = pl.BlockSpec((1, 1, HEAD_PAIRS, SSD_STATE, 128), lambda b, c: (b, at(c), 0, 0, 0))
    return x_spec, bc_spec, col_spec, row_spec, st_spec


def _ssd_head(h, dtb, acb, apb, aeb, arb):
    return dtb[:, h:h + 1], acb[:, h:h + 1], apb[:, h:h + 1], aeb[:, h:h + 1], arb[h:h + 1, :]


def _ssd_fwd(x, bm, cm, dt_arr, cum, prev, end, a_row):
    nb, seq, _ = x.shape
    nc = seq // CHUNK
    x_spec, bc_spec, col_spec, row_spec, st_spec = _ssd_specs(nc, False)

    def body(x_ref, b_ref, c_ref, dt_ref, ac_ref, ap_ref, ae_ref, ar_ref, y_ref, st_ref, s_scr):
        @pl.when(pl.program_id(1) == 0)
        def _():
            s_scr[...] = jnp.zeros_like(s_scr)

        causal = (lax.broadcasted_iota(jnp.int32, (CHUNK, CHUNK), 0)
                  >= lax.broadcasted_iota(jnp.int32, (CHUNK, CHUNK), 1))
        low = lax.broadcasted_iota(jnp.int32, (CHUNK, 128), 1) < HEAD_DIM
        cols = (dt_ref[0], ac_ref[0], ap_ref[0], ae_ref[0], ar_ref[0])
        bcs = [b_ref[0, :, g * 128:(g + 1) * 128] for g in range(2)]
        ccs = [c_ref[0, :, g * 128:(g + 1) * 128] for g in range(2)]
        ms = [_dot_nt(ccs[g], bcs[g]) for g in range(2)]
        for pi in range(HEAD_PAIRS):
            x2 = x_ref[0, :, pi * 128:(pi + 1) * 128]
            dt2 = jnp.where(low, cols[0][:, 2 * pi:2 * pi + 1], cols[0][:, 2 * pi + 1:2 * pi + 2])
            xdt_f = x2 * dt2
            xdt = xdt_f.astype(BF16)
            sprev = s_scr[pi]
            st_ref[0, 0, pi] = sprev
            spb = sprev.astype(BF16)
            ys, us = [], []
            for h in (2 * pi, 2 * pi + 1):
                g = h // 3
                _, ac, ap, ae, ar = _ssd_head(h, *cols)
                lm = jnp.exp(jnp.where(causal, ac - ar, NEG))
                gm = (ms[g] * lm).astype(BF16)
                ys.append(_dot(gm, xdt) + jnp.exp(ac - ap) * _dot(ccs[g], spb))
                us.append(jnp.exp(ae[0:1, :] - ap[0:1, :]) * sprev
                          + _dot_tn(bcs[g], (xdt_f * jnp.exp(ae - ac)).astype(BF16)))
            y_ref[0, :, pi * 128:(pi + 1) * 128] = jnp.where(low, ys[0], ys[1])
            s_scr[pi] = jnp.where(lax.broadcasted_iota(jnp.int32, (SSD_STATE, 128), 1) < HEAD_DIM, us[0], us[1])

    return pl.pallas_call(
        body, name="ssd_fwd", grid=(nb, nc),
        out_shape=[jax.ShapeDtypeStruct(x.shape, F32),
                   jax.ShapeDtypeStruct((nb, nc, HEAD_PAIRS, SSD_STATE, 128), F32)],
        in_specs=[x_spec, bc_spec, bc_spec, col_spec, col_spec, col_spec, col_spec, row_spec],
        out_specs=[x_spec, st_spec],
        scratch_shapes=[pltpu.VMEM((HEAD_PAIRS, SSD_STATE, 128), F32)],
        compiler_params=_params(("parallel", "arbitrary")),
    )(x, bm, cm, dt_arr, cum, prev, end, a_row)


def _ssd_bwd(x_h, bm, cm, dt_arr, cum, prev, end, a_row, states, dy_h):
    nb, seq, _ = x_h.shape
    nc = seq // CHUNK
    x_spec, bc_spec, col_spec, row_spec, st_spec = _ssd_specs(nc, True)

    def body(x_ref, b_ref, c_ref, dt_ref, ac_ref, ap_ref, ae_ref, ar_ref, st_ref, dy_ref,
             dx_ref, db_ref, dc_ref, da_ref, dend_ref, ddt_ref, ds_scr):
        @pl.when(pl.program_id(1) == 0)
        def _():
            ds_scr[...] = jnp.zeros_like(ds_scr)

        causal = (lax.broadcasted_iota(jnp.int32, (CHUNK, CHUNK), 0)
                  >= lax.broadcasted_iota(jnp.int32, (CHUNK, CHUNK), 1))
        lane = lax.broadcasted_iota(jnp.int32, (CHUNK, 128), 1)
        low = lane < HEAD_DIM
        low_state = lax.broadcasted_iota(jnp.int32, (SSD_STATE, 128), 1) < HEAD_DIM
        cols = (dt_ref[0], ac_ref[0], ap_ref[0], ae_ref[0], ar_ref[0])
        bcs = [b_ref[0, :, g * 128:(g + 1) * 128] for g in range(2)]
        ccs = [c_ref[0, :, g * 128:(g + 1) * 128] for g in range(2)]
        ms = [_dot_nt(ccs[g], bcs[g]) for g in range(2)]
        dms = [jnp.zeros((CHUNK, CHUNK), F32) for _ in range(2)]
        dc_accs = [jnp.zeros((CHUNK, SSD_STATE), F32) for _ in range(2)]
        db_accs = [jnp.zeros((CHUNK, SSD_STATE), F32) for _ in range(2)]
        da_blk = jnp.zeros((CHUNK, 128), F32)
        dend_blk = jnp.zeros((CHUNK, 128), F32)
        ddt_blk = jnp.zeros((CHUNK, 128), F32)
        for pi in range(HEAD_PAIRS):
            x2 = x_ref[0, :, pi * 128:(pi + 1) * 128]
            dy2 = dy_ref[0, :, pi * 128:(pi + 1) * 128]
            dt2 = jnp.where(low, cols[0][:, 2 * pi:2 * pi + 1], cols[0][:, 2 * pi + 1:2 * pi + 2])
            xdt_f = x2 * dt2
            xdt = xdt_f.astype(BF16)
            dyb = dy2.astype(BF16)
            dsn = ds_scr[pi]
            dsb = dsn.astype(BF16)
            sprev_f = st_ref[0, 0, pi]
            sprev = sprev_f.astype(BF16)
            dxdts, dss = [], []
            for h in (2 * pi, 2 * pi + 1):
                g = h // 3
                mine = low if h % 2 == 0 else jnp.logical_not(low)
                _, ac, ap, ae, ar = _ssd_head(h, *cols)
                bc, cc, m = bcs[g], ccs[g], ms[g]
                lm = jnp.exp(jnp.where(causal, ac - ar, NEG))
                gm = (m * lm).astype(BF16)
                dy_m = jnp.where(mine, dy2, 0.0)
                dyb_m = dy_m.astype(BF16)
                xdt_m = jnp.where(mine, xdt_f, 0.0)
                e_in = jnp.exp(ac - ap)
                f_out = jnp.exp(ae - ac)
                whole = jnp.exp(ae[0:1, :] - ap[0:1, :])
                dg = _dot_nt(dyb_m, xdt)
                dxdt_off = f_out * _dot(bc, dsb)
                dxdt = _dot_tn(gm, dyb) + dxdt_off
                dmj = dg * lm
                dms[g] = dms[g] + dmj
                dc_accs[g] = dc_accs[g] + e_in * _dot_nt(dyb_m, sprev)
                db_accs[g] = db_accs[g] + f_out * _dot_nt(xdt_m.astype(BF16), dsb)
                dss.append(whole * dsn + _dot_tn(cc, (dy2 * e_in).astype(BF16)))
                wmat = dmj * m
                r_in = jnp.sum(dy_m * (e_in * _dot(cc, sprev)), axis=1, keepdims=True)
                q_out = jnp.sum(xdt_m * dxdt_off, axis=1, keepdims=True)
                daj = (jnp.sum(wmat, axis=1, keepdims=True) - jnp.sum(wmat.T, axis=1, keepdims=True)
                       + r_in - q_out)
                cross = jnp.where(low_state if h % 2 == 0 else jnp.logical_not(low_state), dsn * sprev_f, 0.0)
                dendj = (jnp.sum(q_out, axis=0, keepdims=True)
                         + whole * jnp.sum(jnp.sum(cross, axis=1, keepdims=True), axis=0, keepdims=True))
                ddtj = jnp.sum(jnp.where(mine, dxdt * x2, 0.0), axis=1, keepdims=True)
                dxdts.append(dxdt)
                da_blk = jnp.where(lane == h, daj, da_blk)
                dend_blk = jnp.where(lane == h, dendj, dend_blk)
                ddt_blk = jnp.where(lane == h, ddtj, ddt_blk)
            dx_ref[0, :, pi * 128:(pi + 1) * 128] = jnp.where(low, dxdts[0], dxdts[1]) * dt2
            ds_scr[pi] = jnp.where(low_state, dss[0], dss[1])
        for g in range(2):
            dmb = dms[g].astype(BF16)
            dc_ref[0, :, g * 128:(g + 1) * 128] = dc_accs[g] + _dot(dmb, bcs[g])
            db_ref[0, :, g * 128:(g + 1) * 128] = db_accs[g] + _dot_tn(dmb, ccs[g])
        da_ref[0] = da_blk
        dend_ref[0] = dend_blk
        ddt_ref[0] = ddt_blk

    col_shape = jax.ShapeDtypeStruct((nb, seq, 128), F32)
    return pl.pallas_call(
        body, name="ssd_bwd", grid=(nb, nc),
        out_shape=[jax.ShapeDtypeStruct(x_h.shape, F32),
                   jax.ShapeDtypeStruct((nb, seq, 256), F32), jax.ShapeDtypeStruct((nb, seq, 256), F32),
                   col_shape, col_shape, col_shape],
        in_specs=[x_spec, bc_spec, bc_spec, col_spec, col_spec, col_spec, col_spec, row_spec, st_spec, x_spec],
        out_specs=[x_spec, bc_spec, bc_spec, col_spec, col_spec, col_spec],
        scratch_shapes=[pltpu.VMEM((HEAD_PAIRS, SSD_STATE, 128), F32)],
        compiler_params=_params(("parallel", "arbitrary")),
    )(x_h, bm, cm, dt_arr, cum, prev, end, a_row, states, dy_h)


def _lru_gates(xl, wa, ba, wx, bx, lam):
    xb = xl.astype(BF16)
    r = _sigmoid(_dot(xb, wa) + ba)
    i = _sigmoid(_dot(xb, wx) + bx)
    sp = _softplus(-lam)
    log_a = -LRU_C * r * sp
    a = jnp.exp(log_a)
    mult = jnp.sqrt(_neg_expm1(2.0 * log_a))
    return r, i, sp, log_a, a, mult


def _scan_chunks(a_ref, u_ref, h_ref, seq, reverse):
    nc = seq // CHUNK
    width = a_ref.shape[1]
    row = lax.broadcasted_iota(jnp.int32, (CHUNK, width), 0)

    def chunk(ci, carry):
        c = nc - 1 - ci if reverse else ci
        rows = pl.ds(pl.multiple_of(c * CHUNK, CHUNK), CHUNK)
        av, bv = a_ref[rows, :], u_ref[rows, :]
        d = 1
        while d < CHUNK:
            if reverse:
                keep = row < CHUNK - d
                a_sh = jnp.where(keep, pltpu.roll(av, CHUNK - d, 0), 1.0)
                b_sh = jnp.where(keep, pltpu.roll(bv, CHUNK - d, 0), 0.0)
            else:
                keep = row >= d
                a_sh = jnp.where(keep, pltpu.roll(av, d, 0), 1.0)
                b_sh = jnp.where(keep, pltpu.roll(bv, d, 0), 0.0)
            bv = av * b_sh + bv
            av = av * a_sh
            d *= 2
        hv = bv + av * carry
        h_ref[rows, :] = hv
        return hv[0:1, :] if reverse else hv[CHUNK - 1:CHUNK, :]

    lax.fori_loop(0, nc, chunk, jnp.zeros((1, width), F32))


def _lru_fwd(proj, seq, cw, cb, wa, ba, wx, bx, lam):
    def fn(raw, cw, cb, wa, ba, wx, bx, lam, a_scr, u_scr, h_scr):
        xl = _conv(raw, cw, cb)
        r, i, sp, log_a, a, mult = _lru_gates(xl, wa, ba, wx, bx, lam)
        a_scr[...] = a
        u_scr[...] = mult * (i * xl)
        _scan_chunks(a_scr, u_scr, h_scr, seq, reverse=False)
        return h_scr[...], xl

    return _rowcall("lru_fwd", fn, [(proj, 256, 2)], [cw, cb, wa, ba, wx, bx, lam],
                    [(256, F32), (256, F32)], tile=seq,
                    scratch=[pltpu.VMEM((seq, 256), F32)] * 3)


def _lru_bwd(proj, seq, xl_all, h_all, dh_all, cw, cb, wa, ba, wx, bx, lam):
    def fn(raw, xl, hseq, dh, cw, cb, wa, ba, wx, bx, lam, a_scr, u_scr, h_scr):
        r, i, sp, log_a, a, mult = _lru_gates(xl, wa, ba, wx, bx, lam)
        a_scr[...] = _shift_up(a, 1)
        u_scr[...] = dh
        _scan_chunks(a_scr, u_scr, h_scr, seq, reverse=True)
        dht = h_scr[...]
        da = dht * _shift_down(hseq, 1)
        gated = i * xl
        dgated = dht * mult
        dmult = dht * gated
        dlog_a = da * a - dmult * (a * a) / mult
        dr = dlog_a * (-LRU_C * sp)
        dsp = jnp.sum(dlog_a * (-LRU_C * r), axis=0, keepdims=True)
        dlam = -dsp * _sigmoid(-lam)
        dpa = dr * r * (1.0 - r)
        dpx = (dgated * xl) * i * (1.0 - i)
        dpa_b, dpx_b = dpa.astype(BF16), dpx.astype(BF16)
        dxl = dgated * i + _dot_nt(dpa_b, wa) + _dot_nt(dpx_b, wx)
        xb = xl.astype(BF16)
        dwa = _dot_tn(xb, dpa_b)
        dwx = _dot_tn(xb, dpx_b)
        draw, dcw, dcb = _conv_bwd(raw, cw, dxl)
        return (draw, dcw, dcb, dwa, jnp.sum(dpa, axis=0, keepdims=True), dwx,
                jnp.sum(dpx, axis=0, keepdims=True), dlam)

    return _rowcall("lru_bwd", fn, [(proj, 256, 2), xl_all, h_all, dh_all], [cw, cb, wa, ba, wx, bx, lam],
                    [(256, BF16)],
                    [((4, 256), F32), ((1, 256), F32), ((256, 256), F32), ((1, 256), F32), ((256, 256), F32),
                     ((1, 256), F32), ((1, 256), F32)],
                    tile=seq, scratch=[pltpu.VMEM((seq, 256), F32)] * 3)


FOX_SCALE = HEAD_DIM ** -0.5
FOX_BLOCK = 1024


class _Side:
    def __init__(self, inputs, out_shape, sems, build):
        self.inputs, self.out_shape, self.sems, self.build = list(inputs), list(out_shape), list(sems), build

    def specs(self):
        any_spec = pl.BlockSpec(memory_space=pl.ANY)
        return [any_spec] * len(self.inputs), [any_spec] * len(self.out_shape)

    def run(self, refs, first, last):
        n_in, n_out = len(self.inputs), len(self.out_shape)
        in_refs, out_refs, sem_refs = refs[:n_in], refs[n_in:n_in + n_out], refs[n_in + n_out:]

        @pl.when(first)
        def _():
            for cp in self.build(in_refs, out_refs, sem_refs):
                cp.start()

        @pl.when(last)
        def _():
            for cp in self.build(in_refs, out_refs, sem_refs):
                cp.wait()


def _grid_ends(grid):
    ids = [pl.program_id(a) for a in range(len(grid))]
    first = functools.reduce(jnp.logical_and, [i == 0 for i in ids])
    last = functools.reduce(jnp.logical_and, [i == n - 1 for i, n in zip(ids, grid)])
    return first, last


Q_BLK, K_BLK, V_BLK = 1920 // 128, 2304 // 128, 2688 // 128
FOX_PAIRS = FOX_HEADS // 2


def _fox_bias(cum, nb, seq):
    cf = cum.reshape(nb, seq, 128)[:, :, SSD_HEADS:SSD_HEADS + FOX_HEADS]
    cols = jnp.pad(cf.reshape(nb * seq, FOX_PAIRS, 2), ((0, 0), (0, 0), (0, 126))).reshape(nb * seq, 384)
    rows = jnp.pad(cf.transpose(0, 2, 1).reshape(nb, FOX_PAIRS, 2, seq), ((0, 0), (0, 0), (0, 6), (0, 0)))
    return cols, rows


def _fox_fwd(proj, bias_cols, bias_rows, nb, seq, side=None):
    tb = min(FOX_BLOCK, seq)
    nq = seq // tb
    grid = (nb, FOX_PAIRS, nq)
    n_in = len(side.inputs) if side else 0

    def body(*refs):
        q_ref, k_ref, v_ref, cq_ref, ck_ref = refs[:5]
        o_ref, lse_ref = refs[5 + n_in:7 + n_in]
        if side is not None:
            side.run(refs[5:5 + n_in] + refs[7 + n_in:], *_grid_ends(grid))
        qi = pl.program_id(2)
        low = lax.broadcasted_iota(jnp.int32, (tb, 128), 1) < HEAD_DIM
        q2 = q_ref[...] * FOX_SCALE
        qm = [jnp.where(low, q2, 0.0).astype(BF16), jnp.where(low, 0.0, q2).astype(BF16)]
        cqs = [cq_ref[:, 0:1], cq_ref[:, 1:2]]

        def block(j, carry, diagonal):
            cols = pl.ds(pl.multiple_of(j * tb, tb), tb)
            k2 = k_ref[cols, :].astype(BF16)
            v2 = v_ref[cols, :].astype(BF16)
            new = []
            for hh in range(2):
                m_i, l_i, acc = carry[hh]
                s = _dot_nt(qm[hh], k2) + cqs[hh] - ck_ref[0, 0, hh:hh + 1, cols]
                if diagonal:
                    s = jnp.where(lax.broadcasted_iota(jnp.int32, (tb, tb), 0)
                                  >= lax.broadcasted_iota(jnp.int32, (tb, tb), 1), s, NEG)
                m_new = jnp.maximum(m_i, jnp.max(s, axis=1, keepdims=True))
                p = jnp.exp(s - m_new)
                alpha = jnp.exp(m_i - m_new)
                new.append((m_new, alpha * l_i + jnp.sum(p, axis=1, keepdims=True),
                            alpha * acc + _dot(p.astype(BF16), v2)))
            return tuple(new)

        one = (jnp.full((tb, 1), NEG, F32), jnp.zeros((tb, 1), F32), jnp.zeros((tb, 128), F32))
        carry = lax.fori_loop(0, qi, lambda j, cr: block(j, cr, False), (one, one))
        (m0, l0, a0), (m1, l1, a1) = block(qi, carry, True)
        o_ref[...] = jnp.where(low, a0 / l0, a1 / l1)
        lse_ref[...] = jnp.where(low, m0 + jnp.log(l0), m1 + jnp.log(l1))

    def blk(first):
        return pl.BlockSpec((tb, 128), lambda b, p, i: (b * nq + i, first + p))

    def seq_blk(first):
        return pl.BlockSpec((seq, 128), lambda b, p, i: (b, first + p))

    row_spec = pl.BlockSpec((1, 1, 8, seq), lambda b, p, i: (b, p, 0, 0))
    side_in, side_out = side.specs() if side else ([], [])
    shape = jax.ShapeDtypeStruct((nb * seq, FOX_WIDTH), F32)
    return pl.pallas_call(
        body, name="fox_fwd", grid=grid,
        out_shape=[shape, shape] + (side.out_shape if side else []),
        in_specs=[blk(Q_BLK), seq_blk(K_BLK), seq_blk(V_BLK), blk(0), row_spec] + side_in,
        out_specs=[blk(0), blk(0)] + side_out,
        scratch_shapes=side.sems if side else [],
        compiler_params=_params(("arbitrary", "arbitrary", "arbitrary")),
    )(proj, proj, proj, bias_cols, bias_rows, *(side.inputs if side else []))


def _fox_bwd(proj, o, lse, do, bias_cols, bias_rows, nb, seq, side=None):
    tb = min(FOX_BLOCK, seq)
    nq = seq // tb
    grid = (nb, FOX_PAIRS, nq)
    n_in = len(side.inputs) if side else 0

    def body(*refs):
        q_ref, k_ref, v_ref, o_ref, lse_ref, do_ref, cq_ref, ck_ref = refs[:8]
        dq_ref, dk_ref, dv_ref, dcum_ref, dcq_ref = refs[8 + n_in:13 + n_in]
        if side is not None:
            side.run(refs[8:8 + n_in] + refs[13 + n_in:], *_grid_ends(grid))
        kj = pl.program_id(2)

        @pl.when(kj == 0)
        def _():
            dq_ref[...] = jnp.zeros_like(dq_ref)
            dcq_ref[...] = jnp.zeros_like(dcq_ref)

        lane = lax.broadcasted_iota(jnp.int32, (tb, 128), 1)
        low = lane < HEAD_DIM
        mine = [low, jnp.logical_not(low)]
        k2 = k_ref[...]
        kb = k2.astype(BF16)
        km = [jnp.where(mine[hh], k2, 0.0).astype(BF16) for hh in range(2)]
        vb = v_ref[...].astype(BF16)

        def block(i, carry, diagonal):
            dk, dv, c0, c1 = carry
            csum = [c0, c1]
            rows = pl.ds(pl.multiple_of(i * tb, tb), tb)
            q2 = q_ref[rows, :] * FOX_SCALE
            do2 = do_ref[rows, :]
            prod = do2 * o_ref[rows, :]
            dq_add = jnp.zeros((tb, 128), F32)
            rsum = []
            for hh in range(2):
                qm = jnp.where(mine[hh], q2, 0.0).astype(BF16)
                dom = jnp.where(mine[hh], do2, 0.0).astype(BF16)
                delta = jnp.sum(jnp.where(mine[hh], prod, 0.0), axis=1, keepdims=True)
                s = _dot_nt(qm, kb) + cq_ref[rows, hh:hh + 1] - ck_ref[0, 0, hh:hh + 1, :]
                if diagonal:
                    s = jnp.where(lax.broadcasted_iota(jnp.int32, (tb, tb), 0)
                                  >= lax.broadcasted_iota(jnp.int32, (tb, tb), 1), s, NEG)
                p = jnp.exp(s - lse_ref[rows, HEAD_DIM * hh:HEAD_DIM * hh + 1])
                ds = p * (_dot_nt(dom, vb) - delta)
                dsb = ds.astype(BF16)
                dv = dv + _dot_tn(p.astype(BF16), dom)
                dk = dk + _dot_tn(dsb, qm)
                dq_add = dq_add + _dot(dsb, km[hh])
                rsum.append(jnp.sum(ds, axis=1, keepdims=True))
                csum[hh] = csum[hh] + jnp.sum(ds, axis=0, keepdims=True)
            dq_ref[rows, :] += dq_add * FOX_SCALE
            dcq_ref[rows, :] += jnp.where(lane == 0, rsum[0], jnp.where(lane == 1, rsum[1], 0.0))
            return dk, dv, csum[0], csum[1]

        init = (jnp.zeros((tb, 128), F32), jnp.zeros((tb, 128), F32), jnp.zeros((1, tb), F32),
                jnp.zeros((1, tb), F32))
        carry = block(kj, init, True)
        dk, dv, c0, c1 = lax.fori_loop(kj + 1, nq, lambda i, cr: block(i, cr, False), carry)
        dk_ref[...] = dk.astype(dk_ref.dtype)
        dv_ref[...] = dv.astype(dv_ref.dtype)
        row = lax.broadcasted_iota(jnp.int32, (8, tb), 0)
        dcum_ref[0, 0] = jnp.where(row == 0, -c0, jnp.where(row == 1, -c1, 0.0))

    def blk(first):
        return pl.BlockSpec((tb, 128), lambda b, p, j: (b * nq + j, first + p))

    def seq_blk(first):
        return pl.BlockSpec((seq, 128), lambda b, p, j: (b, first + p))

    row_blk = pl.BlockSpec((1, 1, 8, tb), lambda b, p, j: (b, p, 0, j))
    side_in, side_out = side.specs() if side else ([], [])
    tokens = nb * seq
    return pl.pallas_call(
        body, name="fox_bwd", grid=grid,
        out_shape=[jax.ShapeDtypeStruct((tokens, FOX_WIDTH), F32), jax.ShapeDtypeStruct((tokens, FOX_WIDTH), BF16),
                   jax.ShapeDtypeStruct((tokens, FOX_WIDTH), BF16),
                   jax.ShapeDtypeStruct((nb, FOX_PAIRS, 8, seq), F32),
                   jax.ShapeDtypeStruct((tokens, FOX_WIDTH), F32)] + (side.out_shape if side else []),
        in_specs=[seq_blk(Q_BLK), blk(K_BLK), blk(V_BLK), seq_blk(0), seq_blk(0), seq_blk(0), seq_blk(0), row_blk]
        + side_in,
        out_specs=[seq_blk(0), blk(0), blk(0), row_blk, seq_blk(0)] + side_out,
        scratch_shapes=side.sems if side else [],
        compiler_params=_params(("arbitrary", "arbitrary", "arbitrary")),
    )(proj, proj, proj, o, lse, do, bias_cols, bias_rows, *(side.inputs if side else []))


_ANY = pl.BlockSpec(memory_space=pl.ANY)


def _place():
    return lax.axis_index("x"), lax.axis_index("y"), lax.axis_index("c")


def _all_gather(shards, name):
    n = len(shards)

    def body(*refs):
        x_refs, out_refs = refs[:n], refs[n:2 * n]
        send_sems, recv_sems, local_sems = refs[2 * n:]
        x, y, c = _place()
        me, sibling = (x, y, c), (x, y, 1 - c)
        chips = [(1 - x, y), (x, 1 - y), (1 - x, 1 - y)]

        def rows(a, px, py, pc):
            return out_refs[a].at[4 * px + 2 * py + pc]

        def copy(a, k, block, to, src=None):
            return pltpu.make_async_remote_copy(
                src_ref=rows(a, *block) if src is None else src, dst_ref=rows(a, *block),
                send_sem=send_sems.at[a, k], recv_sem=recv_sems.at[a, k],
                device_id=to, device_id_type=pl.DeviceIdType.MESH)

        mine = [pltpu.make_async_copy(x_refs[a], rows(a, *me), local_sems.at[a]) for a in range(n)]
        for cp in mine:
            cp.start()
        first = []
        for a in range(n):
            first.append(copy(a, 0, me, sibling, src=x_refs[a]))
            first += [copy(a, 1 + j, me, (*chip, c), src=x_refs[a]) for j, chip in enumerate(chips)]
        for cp in first:
            cp.start()
        passed = []
        for j, chip in enumerate(chips):
            for a in range(n):
                copy(a, 1 + j, (*chip, c), me).wait_recv()
                passed.append(copy(a, 4 + j, (*chip, c), sibling))
                passed[-1].start()
        for a in range(n):
            copy(a, 0, sibling, me).wait_recv()
            for j, chip in enumerate(chips):
                copy(a, 4 + j, (*chip, 1 - c), me).wait_recv()
        for cp in first + passed:
            cp.wait_send()
        for cp in mine:
            cp.wait()

    return pl.pallas_call(
        body, name=name,
        out_shape=[jax.ShapeDtypeStruct((N_DEV,) + s.shape, s.dtype) for s in shards],
        in_specs=[_ANY] * n, out_specs=[_ANY] * n,
        scratch_shapes=[pltpu.SemaphoreType.DMA((n, 7)), pltpu.SemaphoreType.DMA((n, 7)),
                        pltpu.SemaphoreType.DMA((n,))],
    )(*shards)


def _remote(src, dst, send_sem, recv_sem, to):
    return pltpu.make_async_remote_copy(src_ref=src, dst_ref=dst, send_sem=send_sem, recv_sem=recv_sem,
                                        device_id=to, device_id_type=pl.DeviceIdType.MESH)


def _sem_pairs(n, k):
    return [pltpu.SemaphoreType.DMA((n, k)), pltpu.SemaphoreType.DMA((n, k))]


def _sibling_side(full, offsets):
    def build(g_refs, out_refs, sems):
        x, y, c = _place()
        return [_remote(g_refs[a].at[offsets[a] + 4 * (k // 2) + 2 * (k % 2) + (1 - c)], out_refs[a].at[k],
                        sems[0].at[a, k], sems[1].at[a, k], (x, y, 1 - c))
                for a in range(len(g_refs)) for k in range(4)]

    return _Side(full, [jax.ShapeDtypeStruct((4,) + f.shape[1:], f.dtype) for f in full],
                 _sem_pairs(len(full), 4), build)


def _chip_side(part):
    def build(p_refs, out_refs, sems):
        x, y, c = _place()
        peers = [(1 - x, y), (x, 1 - y), (1 - x, 1 - y)]
        return [_remote(p_refs[a].at[2 * px + py], out_refs[a].at[k], sems[0].at[a, k], sems[1].at[a, k],
                        (px, py, c))
                for a in range(len(p_refs)) for k, (px, py) in enumerate(peers)]

    return _Side(part, [jax.ShapeDtypeStruct((3,) + p.shape[1:], p.dtype) for p in part],
                 _sem_pairs(len(part), 3), build)


def _spread_side(shards):
    def build(x_refs, out_refs, sems):
        x, y, c = _place()
        targets = [(x, y, 1 - c), (1 - x, y, c), (x, 1 - y, c), (1 - x, 1 - y, c)]
        cps = []
        for a in range(len(x_refs)):
            slot = out_refs[a].at[4 * x + 2 * y + c]
            cps.append(pltpu.make_async_copy(x_refs[a], slot, sems[2].at[a]))
            cps += [_remote(x_refs[a], slot, sems[0].at[a, k], sems[1].at[a, k], to)
                    for k, to in enumerate(targets)]
        return cps

    n = len(shards)
    return _Side(shards, [jax.ShapeDtypeStruct((N_DEV,) + s.shape, s.dtype) for s in shards],
                 _sem_pairs(n, 4) + [pltpu.SemaphoreType.DMA((n,))], build)


def _pass_side(bufs):
    def build(in_refs, out_refs, sems):
        x, y, c = _place()
        chips = [(1 - x, y), (x, 1 - y), (1 - x, 1 - y)]
        return [_remote(in_refs[a].at[4 * px + 2 * py + c], out_refs[a].at[4 * px + 2 * py + c],
                        sems[0].at[a, j], sems[1].at[a, j], (x, y, 1 - c))
                for a in range(len(in_refs)) for j, (px, py) in enumerate(chips)]

    return _Side(bufs, [jax.ShapeDtypeStruct(b.shape, b.dtype) for b in bufs], _sem_pairs(len(bufs), 3), build)


def _run_side(side, name, in_place=False):
    n_in = len(side.inputs)

    def body(*refs):
        copies = side.build(refs[:n_in], refs[n_in:n_in + len(side.out_shape)],
                            refs[n_in + len(side.out_shape):])
        for cp in copies:
            cp.start()
        for cp in copies:
            cp.wait()

    in_specs, out_specs = side.specs()
    return pl.pallas_call(
        body, name=name, out_shape=side.out_shape, in_specs=in_specs, out_specs=out_specs,
        scratch_shapes=side.sems,
        input_output_aliases={a: a for a in range(n_in)} if in_place else {},
    )(*side.inputs)


def _pick_rows(rows, cap=512):
    t = cap
    while t >= 8:
        if rows % t == 0:
            return t
        t //= 2
    raise ValueError(rows)


def _pair_sum(full, offset, got, name):
    _, rows, cols = full.shape
    tile = _pick_rows(rows, 256)
    c = lax.axis_index("c").astype(jnp.int32).reshape(1)

    def body(c_ref, a_ref, b_ref, o_ref):
        o_ref[...] = a_ref[...] + b_ref[...]

    blk = (1, tile, cols)
    return pl.pallas_call(
        body, name=name,
        grid_spec=pltpu.PrefetchScalarGridSpec(
            num_scalar_prefetch=1, grid=(4, rows // tile),
            in_specs=[pl.BlockSpec(blk, lambda k, i, c_ref: (offset + 4 * (k // 2) + 2 * (k % 2) + c_ref[0], i, 0)),
                      pl.BlockSpec(blk, lambda k, i, c_ref: (k, i, 0))],
            out_specs=pl.BlockSpec(blk, lambda k, i, c_ref: (k, i, 0))),
        out_shape=jax.ShapeDtypeStruct((4, rows, cols), full.dtype),
        compiler_params=_params(("arbitrary", "arbitrary")),
    )(c, full, got)


def _adam_math(w, g, m, v):
    c1 = 1.0 / (1.0 - ADAM_B1 ** ADAM_STEP)
    c2 = 1.0 / (1.0 - ADAM_B2 ** ADAM_STEP)
    m_new = ADAM_B1 * m + (1.0 - ADAM_B1) * g
    v_new = ADAM_B2 * v + (1.0 - ADAM_B2) * (g * g)
    delta = -ADAM_LR * ((m_new * c1) / (jnp.sqrt(v_new * c2) + ADAM_EPS) + ADAM_WD * w)
    return delta, m_new, v_new


def _chip_sum(part, others, name):
    _, rows, cols = part.shape
    tile = _pick_rows(rows, 256)
    own = (2 * lax.axis_index("x") + lax.axis_index("y")).astype(jnp.int32).reshape(1)

    def body(own_ref, p_ref, o_ref, g_out):
        g_out[...] = ((p_ref[0] + o_ref[0]) + o_ref[1]) + o_ref[2]

    return pl.pallas_call(
        body, name=name,
        grid_spec=pltpu.PrefetchScalarGridSpec(
            num_scalar_prefetch=1, grid=(rows // tile,),
            in_specs=[pl.BlockSpec((1, tile, cols), lambda i, own_ref: (own_ref[0], i, 0)),
                      pl.BlockSpec((3, tile, cols), lambda i, own_ref: (0, i, 0))],
            out_specs=pl.BlockSpec((tile, cols), lambda i, own_ref: (i, 0))),
        out_shape=jax.ShapeDtypeStruct((rows, cols), F32),
        compiler_params=_params(("arbitrary",)),
    )(own, part, others)


def _all_reduce_small(vec):
    gathered = _all_gather([vec], "ar_gather")[0]
    rows = vec.shape[0]

    def fn(*blocks):
        s = blocks[0]
        for b in blocks[1:]:
            s = s + b
        return s

    return _rowcall("ar_sum", fn, [gathered[j] for j in range(N_DEV)], [], [(1024, F32)],
                    tile=_pick_rows(rows))[0]


def _pad_rows(flat, mult):
    n = flat.shape[-1]
    per = mult * 1024
    padded = -(-n // per) * per
    pad = [(0, 0)] * (flat.ndim - 1) + [(0, padded - n)]
    return jnp.pad(flat, pad).reshape(flat.shape[:-1] + (padded // 1024, 1024))


def _regroup_w_in(w):
    pad = jnp.zeros((w.shape[0], 116), w.dtype)
    return jnp.concatenate([w[:, 768:1280], w[:, 1286:1798], w[:, 1280:1286], w[:, 2950:2956], pad,
                            w[:, 0:768], w[:, 1798:2950]], axis=1)


def _ungroup_w_in(wp):
    return jnp.concatenate([wp[:, 1152:1920], wp[:, 0:512], wp[:, 1024:1030], wp[:, 512:1024],
                            wp[:, 1920:3072], wp[:, 1030:1036]], axis=1)


def _to_shard(name, a):
    if name == 'w_in':
        return _regroup_w_in(a)
    if name in ('w_gate', 'w_up'):
        return jnp.pad(a, ((0, 0), (0, FF_SHARD_P - FF_SHARD)))
    if name == 'w_down':
        return jnp.pad(a, ((0, FF_SHARD_P - FF_SHARD), (0, 0)))
    return a


def _from_shard(name, a):
    if name == 'w_in':
        return _ungroup_w_in(a)
    if name in ('w_gate', 'w_up'):
        return a[:, 0:FF_SHARD]
    if name == 'w_down':
        return a[0:FF_SHARD, :]
    return a


def _whole(name, gathered):
    if BIG[name][0] == 1:
        return gathered.reshape(-1, gathered.shape[-1])
    return gathered.transpose(1, 0, 2).reshape(gathered.shape[1], -1)


def _split(name, whole):
    if BIG[name][0] == 1:
        return whole.reshape(N_DEV, whole.shape[0] // N_DEV, whole.shape[1])
    return whole.reshape(whole.shape[0], N_DEV, whole.shape[1] // N_DEV).transpose(1, 0, 2)


def _pack_list(arrays, mult):
    return _pad_rows(jnp.concatenate([a.reshape(-1) for a in arrays]), mult)


def _unpack_list(buf, shapes):
    flat = buf.reshape(-1)
    out, off = [], 0
    for s in shapes:
        n = math.prod(s)
        out.append(flat[off:off + n].reshape(s))
        off += n
    return out


def _adamw(w, g, m, v, name="adamw"):
    rows, cols = w.shape
    return _rowcall(name, _adam_math, [w, g, m, v], [], [(cols, F32)] * 3, tile=_pick_rows(rows, 256))


def _block_diag(w):
    out = jnp.zeros((LRU_WIDTH, LRU_WIDTH), w.dtype)
    for g in range(4):
        out = lax.dynamic_update_slice(out, w[g], (64 * g, 64 * g))
    return out


def _block_diag_grad(full):
    return jnp.stack([full[64 * g:64 * (g + 1), 64 * g:64 * (g + 1)] for g in range(4)])


def _row(v):
    return v.reshape(1, -1).astype(F32)


def _lane128(*pieces):
    flat = jnp.concatenate([p.reshape(-1).astype(F32) for p in pieces])
    return jnp.pad(flat, (0, 128 - flat.shape[0])).reshape(1, 128)


def _layer_consts(w):
    c = {}
    cw, cb = w['ssd_conv_w'], w['ssd_conv_b']
    c['cw_x'], c['cw_b'], c['cw_c'] = cw[:, 0:384], cw[:, 384:640], cw[:, 640:896]
    c['cb_x'], c['cb_b'], c['cb_c'] = _row(cb[0:384]), _row(cb[384:640]), _row(cb[640:896])
    c['bias128'] = _lane128(w['ssd_dt_bias'], w['fox_b_f'])
    c['alog128'] = _lane128(w['ssd_a_log'])
    c['d384'] = _row(jnp.repeat(w['ssd_d'], HEAD_DIM))
    c['lcw'], c['lcb'] = w['lru_conv_w'], _row(w['lru_conv_b'])
    c['wa'], c['wx'] = _block_diag(w['lru_w_a']).astype(BF16), _block_diag(w['lru_w_x']).astype(BF16)
    c['ba'], c['bx'], c['lam'] = _row(w['lru_b_a']), _row(w['lru_b_x']), _row(w['lru_lambda'])
    return c


def _layer_fwd(h0, p_i, w, c, hooks=None, layer=0):
    nb, seq = c['nb'], c['seq']

    def carried(stage):
        return hooks.fwd_side(layer, stage) if hooks is not None else None

    def arrived(outs):
        if hooks is not None:
            hooks.fwd_done(outs)
            w.update(hooks.weights(layer))

    u1 = _rowcall("norm1", lambda h, g: _rms(h, g), [h0], [_row(w['norm1_g'])], [(D_MODEL, BF16)])[0]
    proj = _matmul(u1, w['w_in'], 'nn', "proj")

    xs_c = _convsilu_fwd("conv_x", (proj, 384, 4), seq, c['cw_x'], c['cb_x'], F32)
    b_c = _convsilu_fwd("conv_b", (proj, 256, 0), seq, c['cw_b'], c['cb_b'], BF16)
    c_c = _convsilu_fwd("conv_c", (proj, 256, 1), seq, c['cw_c'], c['cb_c'], BF16)
    dt_arr, cum, prev, end = _small_fwd((proj, 128, 8), seq, c['bias128'], c['alog128'])
    x_h = xs_c.reshape(nb, seq, SSD_WIDTH)
    cum3 = cum.reshape(nb, seq, 128)
    a_row = cum3[:, :, 0:8].transpose(0, 2, 1)
    ssd_in = (x_h, b_c.reshape(nb, seq, 256), c_c.reshape(nb, seq, 256), dt_arr.reshape(nb, seq, 128), cum3,
              prev.reshape(nb, seq, 128), end.reshape(nb, seq, 128), a_row)
    y_h, states = _ssd_fwd(*ssd_in)
    y_core = y_h.reshape(nb * seq, SSD_WIDTH)

    hseq, xl = _lru_fwd(proj, seq, c['lcw'], c['lcb'], c['wa'], c['ba'], c['wx'], c['bx'], c['lam'])

    bias_cols, bias_rows = _fox_bias(cum, nb, seq)
    y_fox, lse, *side_out = _fox_fwd(proj, bias_cols, bias_rows, nb, seq, carried('attention'))
    arrived(side_out)

    def post(yc, xs, z, hs, lg, yf, d, g1, g2, g3):
        y1 = _rms((yc + xs * d) * _silu(z), g1)
        y2 = _rms(hs * _gelu(lg), g2)
        y3 = _rms(yf, g3)
        return jnp.concatenate([y1, y2, y3], axis=-1)

    post_consts = [c['d384'], _row(w['ssd_norm_g']), _row(w['lru_norm_g']), _row(w['fox_norm_g'])]
    ycat = _rowcall("mix_post", post, [y_core, xs_c, (proj, 384, 3), hseq, (proj, 256, 3), y_fox], post_consts,
                    [(D_MODEL, BF16)])[0]
    mix = _matmul(ycat, w['w_out'], 'nn', "mix_out", BF16)

    def res_norm(h, d, g):
        hn = h + d.astype(F32)
        return hn, _rms(hn, g)

    h1, u2 = _rowcall("res_norm2", res_norm, [h0, mix], [_row(w['norm2_g'])], [(D_MODEL, F32), (D_MODEL, BF16)])
    gate_pre, up_pre, act = _ffn_in(u2, w['w_gu'])
    side = carried('ffn_out')
    if side is None:
        ff = _matmul(act, w['w_down'], 'nn', "ffn_out", BF16)
    else:
        ff, *side_out = _matmul(act, w['w_down'], 'nn', "ffn_out", BF16, side=side)
        arrived(side_out)
    h2, u3 = _rowcall("res_norm3", res_norm, [h1, ff], [_row(w['norm3_g'])], [(D_MODEL, F32), (D_MODEL, BF16)])
    pg = _matmul(u3, w['w_ple_gate'], 'nn', "ple_gate", BF16)
    pp = _matmul(p_i, w['w_ple_proj'], 'nn', "ple_proj", BF16)
    h3 = _rowcall("ple", lambda h, a, b, bias: h + _sigmoid(a.astype(F32) + bias) * b.astype(F32), [h2, pg, pp],
                  [_row(w['b_ple_gate'])],
                  [(D_MODEL, F32)])[0]
    saved = dict(h0=h0, u1=u1, proj=proj, xs_c=xs_c, dt_arr=dt_arr, ssd_in=ssd_in, states=states,
                 y_core=y_core, hseq=hseq, xl=xl, bias_cols=bias_cols, bias_rows=bias_rows, lse=lse,
                 y_fox=y_fox, post_consts=post_consts, ycat=ycat, h1=h1, u2=u2, gate_pre=gate_pre, up_pre=up_pre, act=act, h2=h2, u3=u3,
                 pg=pg, pp=pp, p_i=p_i)
    return h3, saved


def _layer_bwd(dh3, s, w, c, hooks=None, layer=0):
    nb, seq = c['nb'], c['seq']
    g = {}

    def ple_bwd(dh, a, b, bias):
        a, b = a.astype(F32), b.astype(F32)
        gate = _sigmoid(a + bias)
        dpg = dh * b * gate * (1.0 - gate)
        return dh * gate, dpg, jnp.sum(dpg, axis=0, keepdims=True)

    dpp, dpg, g['b_ple_gate'] = _rowcall("ple_bwd", ple_bwd, [dh3, s['pg'], s['pp']], [_row(w['b_ple_gate'])],
                                         [(D_MODEL, BF16), (D_MODEL, BF16)], [((1, D_MODEL), F32)])
    g['w_ple_proj'] = _matmul(s['p_i'], dpp, 'tn', "d_w_ple_proj")
    g['w_ple_gate'] = _matmul(s['u3'], dpg, 'tn', "d_w_ple_gate")
    du3 = _matmul(dpg, w['w_ple_gate'], 'nt', "d_u3", BF16)

    def norm_bwd(h, du, dh, gain):
        dx, dg = _rms_bwd(h, gain, du.astype(F32))
        dhn = dh + dx
        return dhn, dhn, dg

    dh2, dh2_b, g['norm3_g'] = _rowcall("norm3_bwd", norm_bwd, [s['h2'], du3, dh3], [_row(w['norm3_g'])],
                                        [(D_MODEL, F32), (D_MODEL, BF16)], [((1, D_MODEL), F32)])
    g['w_down'] = _matmul(s['act'], dh2_b, 'tn', "d_w_down")
    dact = _matmul(dh2_b, w['w_down'], 'nt', "d_act", BF16)

    def swiglu_bwd(gt, up, da):
        gt, up, da = gt.astype(F32), up.astype(F32), da.astype(F32)
        return jnp.concatenate([da * up * _dsilu(gt), da * _silu(gt)], axis=-1)

    dgu = _rowcall("swiglu_bwd", swiglu_bwd, [s['gate_pre'], s['up_pre'], dact], [],
                   [(2 * D_FF_P, BF16)])[0]
    gu16 = _matmul(s['u2'], dgu, 'tn', "d_w_gu", shard_n=FF_SHARD_P)
    du2 = _matmul(dgu, w['w_gu'], 'nt', "d_u2", BF16)
    dh1, dh1_b, g['norm2_g'] = _rowcall("norm2_bwd", norm_bwd, [s['h1'], du2, dh2], [_row(w['norm2_g'])],
                                        [(D_MODEL, F32), (D_MODEL, BF16)], [((1, D_MODEL), F32)])
    g['w_out'] = _matmul(s['ycat'], dh1_b, 'tn', "d_w_out")
    if hooks is None:
        dycat = _matmul(dh1_b, w['w_out'], 'nt', "d_ycat", BF16)
    else:
        ready = {n: g[n] for n in ('w_out', 'w_down', 'w_ple_gate', 'w_ple_proj')}
        ready['w_gate'], ready['w_up'] = (gu16, 0), (gu16, N_DEV)
        dycat, *side_out = _matmul(dh1_b, w['w_out'], 'nt', "d_ycat", BF16, side=hooks.sibling_side(layer, ready))
        hooks.sibling_done(side_out)

    def post_bwd(dy, yc, xs, z, hs, lg, yf, d, g1, g2, g3):
        dy = dy.astype(F32)
        sz = _silu(z)
        ytot = yc + xs * d
        dpre1, dg1 = _rms_bwd(ytot * sz, g1, dy[:, 0:384])
        dytot = dpre1 * sz
        dz = dpre1 * ytot * _dsilu(z)
        dd = jnp.sum(dytot * xs, axis=0, keepdims=True)
        gl = _gelu(lg)
        dpre2, dg2 = _rms_bwd(hs * gl, g2, dy[:, 384:640])
        dyf, dg3 = _rms_bwd(yf, g3, dy[:, 640:1024])
        return dytot, dytot * d, dz, dpre2 * gl, dpre2 * hs * _dgelu(lg), dyf, dd, dg1, dg2, dg3

    (dy_core, dxs_skip, dz, dhseq, dlg, dy_fox, dd384, g['ssd_norm_g'], g['lru_norm_g'], g['fox_norm_g']) = _rowcall(
        "mix_post_bwd", post_bwd,
        [dycat, s['y_core'], s['xs_c'], (s['proj'], 384, 3), s['hseq'], (s['proj'], 256, 3), s['y_fox']],
        s['post_consts'],
        [(384, F32), (384, F32), (384, BF16), (256, F32), (256, BF16), (384, F32)],
        [((1, 384), F32), ((1, 384), F32), ((1, 256), F32), ((1, 384), F32)])
    g['ssd_d'] = dd384.reshape(SSD_HEADS, HEAD_DIM).sum(axis=1)

    side = hooks.bwd_side() if hooks is not None else None
    dq, dk, dv, dcf_rows, dcf_cols, *side_out = _fox_bwd(s['proj'], s['y_fox'], s['lse'], dy_fox, s['bias_cols'],
                                                         s['bias_rows'], nb, seq, side)
    if hooks is not None:
        hooks.bwd_done(side_out)
    dq = dq.astype(BF16)

    dx_h, db_c, dc_c, da_arr, dend_arr, ddt_arr = _ssd_bwd(*s['ssd_in'], s['states'],
                                                           dy_core.reshape(nb, seq, SSD_WIDTH))
    dxs_c = dx_h.reshape(nb * seq, SSD_WIDTH) + dxs_skip
    dcf = dcf_rows[:, :, 0:2, :].reshape(nb, FOX_HEADS, seq).transpose(0, 2, 1)
    dcum = jnp.concatenate([da_arr[:, :, 0:SSD_HEADS], dcf,
                            jnp.zeros((nb, seq, 128 - 2 * SSD_HEADS), F32)], axis=-1).reshape(nb * seq, 128)
    proj = s['proj']
    dxs_raw, dcw_x, dcb_x = _convsilu_bwd("conv_x_bwd", (proj, 384, 4), seq, c['cw_x'], c['cb_x'], dxs_c)
    db_raw, dcw_b, dcb_b = _convsilu_bwd("conv_b_bwd", (proj, 256, 0), seq, c['cw_b'], c['cb_b'],
                                         db_c.reshape(nb * seq, 256))
    dc_raw, dcw_c, dcb_c = _convsilu_bwd("conv_c_bwd", (proj, 256, 1), seq, c['cw_c'], c['cb_c'],
                                         dc_c.reshape(nb * seq, 256))
    dsmall, dbias128, dalog128 = _small_bwd((proj, 128, 8), seq, dcum, dcf_cols, dend_arr.reshape(nb * seq, 128),
                                            ddt_arr.reshape(nb * seq, 128), s['dt_arr'], c['bias128'], c['alog128'])
    g['ssd_conv_w'] = jnp.concatenate([dcw_x, dcw_b, dcw_c], axis=1)
    g['ssd_conv_b'] = jnp.concatenate([dcb_x, dcb_b, dcb_c], axis=1).reshape(-1)
    g['ssd_dt_bias'] = dbias128[0, 0:SSD_HEADS]
    g['fox_b_f'] = dbias128[0, SSD_HEADS:2 * SSD_HEADS]
    g['ssd_a_log'] = dalog128[0, 0:SSD_HEADS]

    (dlru_raw, g['lru_conv_w'], dlcb, dwa, dba, dwx, dbx, dlam) = _lru_bwd(
        s['proj'], seq, s['xl'], s['hseq'], dhseq, c['lcw'], c['lcb'], c['wa'], c['ba'], c['wx'], c['bx'], c['lam'])
    g['lru_conv_b'], g['lru_b_a'], g['lru_b_x'], g['lru_lambda'] = (t.reshape(-1) for t in (dlcb, dba, dbx, dlam))
    g['lru_w_a'], g['lru_w_x'] = _block_diag_grad(dwa), _block_diag_grad(dwx)

    dproj = jnp.concatenate([db_raw, dc_raw, dlru_raw, dlg, dsmall, dz, dxs_raw, dq, dk, dv], axis=1)
    g['w_in'] = _matmul(s['u1'], dproj, 'tn', "d_w_in")
    if hooks is None:
        du1 = _matmul(dproj, w['w_in'], 'nt', "d_u1", BF16)
    else:
        hooks.bwd_ready(layer, {'w_in': g['w_in']})
        du1, *side_out = _matmul(dproj, w['w_in'], 'nt', "d_u1", BF16, side=hooks.bwd_side())
        hooks.bwd_done(side_out)

    def norm1_bwd(h, du, dh, gain):
        dx, dg = _rms_bwd(h, gain, du.astype(F32))
        return dh + dx, dg

    dh0, g['norm1_g'] = _rowcall("norm1_bwd", norm1_bwd, [s['h0'], du1, dh1], [_row(w['norm1_g'])],
                                 [(D_MODEL, F32)], [((1, D_MODEL), F32)])
    for name in ('b_ple_gate', 'norm3_g', 'norm2_g', 'norm1_g', 'ssd_norm_g', 'lru_norm_g', 'fox_norm_g'):
        g[name] = g[name].reshape(-1)
    g['w_gate'], g['w_up'] = None, None
    if hooks is None:
        g['w_gate'] = gu16[0:N_DEV].transpose(1, 0, 2).reshape(D_MODEL, D_FF_P)
        g['w_up'] = gu16[N_DEV:2 * N_DEV].transpose(1, 0, 2).reshape(D_MODEL, D_FF_P)
    return dh0, g


class _Hooks:
    def __init__(self, shard):
        self.shard = shard
        self.whole = {}
        self.part, self.others = {}, {}
        self.pending, self.flying = [], []

    def first(self, extra):
        got = _all_gather([self.shard['w_in', 0]] + extra, "gather_first")
        self.whole['w_in', 0] = _whole('w_in', got[0])
        return got[1:]

    def fwd_side(self, layer, stage):
        if stage == 'attention':
            self.flying = [(n, layer) for n in BIG if n != 'w_in']
        elif layer + 1 < DEPTH:
            self.flying = [('w_in', layer + 1)]
        else:
            return None
        return _spread_side([self.shard[k] for k in self.flying])

    def fwd_done(self, outs):
        if self.flying:
            passed = _run_side(_pass_side(outs), "gather_pass_%s%d" % self.flying[0], in_place=True)
            for k, arr in zip(self.flying, passed):
                self.whole[k] = _whole(k[0], arr)
            self.flying = []

    def weights(self, layer):
        w = {n: self.whole[n, layer] for n in BIG if (n, layer) in self.whole}
        if 'w_gate' in w:
            w['w_gu'] = jnp.concatenate([w['w_gate'], w['w_up']], axis=1)
        return w

    def sibling_side(self, layer, grads):
        self.sib_keys = [(n, layer) for n in grads]
        self.sib_full = [g if isinstance(g, tuple) else (_split(n, g), 0) for n, g in grads.items()]
        return _sibling_side([f for f, _ in self.sib_full], [off for _, off in self.sib_full])

    def sibling_done(self, got):
        for k, (f, off), r in zip(self.sib_keys, self.sib_full, got):
            self.part[k] = _pair_sum(f, off, r, "rs_pair_sum_%s%d" % k)
        self.pending += self.sib_keys

    def bwd_ready(self, layer, grads):
        side = self.sibling_side(layer, grads)
        self.sibling_done(_run_side(side, "rs_sibling_%s%d" % self.sib_keys[0]))

    def bwd_side(self):
        self.flying, self.pending = self.pending, []
        return _chip_side([self.part[k] for k in self.flying]) if self.flying else None

    def bwd_done(self, outs):
        for k, o in zip(self.flying, outs):
            self.others[k] = o
        self.flying = []

    def flush(self):
        side = self.bwd_side()
        if side is not None:
            self.bwd_done(_run_side(side, "rs_chips_last"))


def _local_step(x, p, target, big, small, hooks=None):
    nb, seq, _ = x.shape
    tokens = nb * seq
    h = x.reshape(tokens, D_MODEL)
    layers, saves = [], []
    for i in range(DEPTH):
        w = {name: small[name][i] for name in small if name != 'final_norm_g'}
        if hooks is not None:
            w.update(hooks.weights(i))
        else:
            for name in ('w_in', 'w_out', 'w_down', 'w_ple_gate', 'w_ple_proj'):
                w[name] = big[name][i]
            w['w_gu'] = jnp.concatenate([big['w_gate'][i], big['w_up'][i]], axis=1)
        c = _layer_consts(w)
        c['nb'], c['seq'] = nb, seq
        h, s = _layer_fwd(h, p[i].reshape(tokens, PLE_DIM).astype(BF16), w, c, hooks, i)
        layers.append((w, c))
        saves.append(s)

    def head(hf, tgt, gain):
        r = lax.rsqrt(jnp.mean(hf * hf, axis=-1, keepdims=True) + EPS)
        xhat = hf * r
        err = xhat * gain - tgt
        loss = 0.5 * jnp.sum(jnp.mean(err * err, axis=-1, keepdims=True), axis=0, keepdims=True)
        dy = err * (1.0 / D_MODEL)
        dg = jnp.sum(dy * xhat, axis=0, keepdims=True)
        dxhat = dy * gain
        dh = r * (dxhat - xhat * jnp.mean(dxhat * xhat, axis=-1, keepdims=True))
        return dh, jnp.broadcast_to(loss, (1, 128)), dg

    dh, loss128, dgf = _rowcall("loss_head", head, [h, target.reshape(tokens, D_MODEL)],
                                [_row(small['final_norm_g'])], [(D_MODEL, F32)],
                                [((1, 128), F32), ((1, D_MODEL), F32)])
    grads = {'final_norm_g': dgf.reshape(-1)}
    per_layer = [None] * DEPTH
    for i in range(DEPTH - 1, -1, -1):
        w, c = layers[i]
        dh, per_layer[i] = _layer_bwd(dh, saves[i], w, c, hooks, i)
    for name in per_layer[0]:
        if name in BIG:
            grads[name] = [per_layer[i][name] for i in range(DEPTH)]
        else:
            grads[name] = jnp.stack([per_layer[i][name] for i in range(DEPTH)])
    return loss128[0, 0], dh.reshape(nb, seq, D_MODEL), grads


def kernel(x, p, norm1_g, w_in, ssd_conv_w, ssd_conv_b, ssd_dt_bias, ssd_a_log, ssd_d, ssd_norm_g, lru_conv_w, lru_conv_b, lru_w_a, lru_b_a, lru_w_x, lru_b_x, lru_lambda, lru_norm_g, fox_b_f, fox_norm_g, w_out, norm2_g, w_gate, w_up, w_down, norm3_g, w_ple_gate, b_ple_gate, w_ple_proj, final_norm_g, loss_target, m_norm1_g, m_w_in, m_ssd_conv_w, m_ssd_conv_b, m_ssd_dt_bias, m_ssd_a_log, m_ssd_d, m_ssd_norm_g, m_lru_conv_w, m_lru_conv_b, m_lru_w_a, m_lru_b_a, m_lru_w_x, m_lru_b_x, m_lru_lambda, m_lru_norm_g, m_fox_b_f, m_fox_norm_g, m_w_out, m_norm2_g, m_w_gate, m_w_up, m_w_down, m_norm3_g, m_w_ple_gate, m_b_ple_gate, m_w_ple_proj, m_final_norm_g, v_norm1_g, v_w_in, v_ssd_conv_w, v_ssd_conv_b, v_ssd_dt_bias, v_ssd_a_log, v_ssd_d, v_ssd_norm_g, v_lru_conv_w, v_lru_conv_b, v_lru_w_a, v_lru_b_a, v_lru_w_x, v_lru_b_x, v_lru_lambda, v_lru_norm_g, v_fox_b_f, v_fox_norm_g, v_w_out, v_norm2_g, v_w_gate, v_w_up, v_w_down, v_norm3_g, v_w_ple_gate, v_b_ple_gate, v_w_ple_proj, v_final_norm_g):
    args = dict(locals())
    w_loc = {n: args[n] for n in WEIGHTS}
    m_loc = {n: args['m_' + n] for n in WEIGHTS}
    v_loc = {n: args['v_' + n] for n in WEIGHTS}
    dev = 4 * lax.axis_index("x") + 2 * lax.axis_index("y") + lax.axis_index("c")

    keys = [(n, i) for n in BIG for i in range(DEPTH)]
    conv_names = list(CONV_SHARDED)
    conv_loc_shapes = [w_loc[n].shape for n in conv_names]
    hooks = _Hooks({(n, i): _to_shard(n, w_loc[n][i]).astype(BF16) for n, i in keys})
    conv_all, = hooks.first([_pack_list([w_loc[n] for n in conv_names], 8)])
    small = {n: w_loc[n] for n in WEIGHTS if n not in BIG and n not in CONV_SHARDED}
    per_dev = [_unpack_list(conv_all[j], conv_loc_shapes) for j in range(N_DEV)]
    for idx, n in enumerate(conv_names):
        small[n] = jnp.concatenate([per_dev[j][idx] for j in range(N_DEV)], axis=2)

    loss_part, dx, grads = _local_step(x, p, loss_target, None, small, hooks)
    loss = lax.psum(loss_part, ("x", "y", "c"))
    hooks.flush()
    out = {kind: {} for kind in ('g', 'delta', 'm', 'v')}
    for n in BIG:
        g_nat = jnp.stack([_from_shard(n, _chip_sum(hooks.part[n, i], hooks.others[n, i], "rs_chip_sum_%s%d" % (n, i)))
                           for i in range(DEPTH)])
        shape = w_loc[n].shape
        flat = [d.reshape(-1, shape[-1]) for d in (w_loc[n], g_nat, m_loc[n], v_loc[n])]
        out['g'][n] = g_nat
        for kind, r in zip(('delta', 'm', 'v'), _adamw(*flat, name="adamw_" + n)):
            out[kind][n] = r.reshape(shape)

    small_names = [n for n in WEIGHTS if n not in BIG]
    small_shapes = [grads[n].shape for n in small_names]
    g_small = dict(zip(small_names, _unpack_list(
        _all_reduce_small(_pack_list([grads[n] for n in small_names], 8)), small_shapes)))
    for n in CONV_SHARDED:
        width = CONV_SHARDED[n][2] // N_DEV
        g_small[n] = lax.dynamic_slice_in_dim(g_small[n], dev * width, width, axis=2)
    shapes = [w_loc[n].shape for n in small_names]
    packed = [_pack_list([d[n] for n in small_names], 8) for d in (w_loc, g_small, m_loc, v_loc)]
    upd = [dict(zip(small_names, _unpack_list(t, shapes))) for t in _adamw(*packed)]
    for kind, d in zip(('g', 'delta', 'm', 'v'), [g_small] + upd):
        for n in small_names:
            out[kind][n] = d[n]
    return (loss, dx, *[out['g'][n] for n in WEIGHTS], *[out['delta'][n] for n in WEIGHTS],
            *[out['m'][n] for n in WEIGHTS], *[out['v'][n] for n in WEIGHTS])
```
